```python
import jax, jax.numpy as jnp
from jax import lax
import numpy as np

D_MODEL = 2048
BATCH = 8
SEQ = 4096
DEPTH = 4

CHUNK = 64
N_MIXERS = 2
N_LAYERS_A = (DEPTH + 1) // 2
N_LAYERS_B = DEPTH // 2
CONV_A_WIDTH = 31
CONV_B_WIDTH = 3
D_FF = 4 * D_MODEL
PLE_DIM = 256
EPS = 1e-6

kernel_name = "hybrid_conformer_shortconv_trunk"


def rmsnorm(x, g):
    xf = x.astype(jnp.float32)
    r = lax.rsqrt(jnp.mean(xf * xf, axis=-1, keepdims=True) + EPS)
    return (xf * r).astype(x.dtype) * g


def causal_depthwise_conv(x, w):
    k_width, channels = w.shape
    return lax.conv_general_dilated(
        x, w[:, None, :],
        window_strides=(1,),
        padding=[(k_width - 1, 0)],
        dimension_numbers=("NWC", "WIO", "NWC"),
        feature_group_count=channels)


def conformer_conv_module(u, w_pw1, b_pw1, w_dw, b_dw, g_norm, w_pw2, b_pw2):
    a = jnp.einsum("bsd,de->bse", u, w_pw1) + b_pw1
    val, gate = jnp.split(a, 2, axis=-1)
    v = val * jax.nn.sigmoid(gate)
    v = causal_depthwise_conv(v, w_dw) + b_dw
    v = rmsnorm(v, g_norm)
    v = jax.nn.silu(v)
    return jnp.einsum("bsd,de->bse", v, w_pw2) + b_pw2


def short_gated_conv(u, w_in, w_conv, w_out):
    bcv = jnp.einsum("bsd,de->bse", u, w_in)
    gate_b, gate_c, v = jnp.split(bcv, 3, axis=-1)
    y = gate_b * causal_depthwise_conv(gate_c * v, w_conv)
    return jnp.einsum("bsd,de->bse", y, w_out)


def squared_relu_mlp(u, w1, w2):
    hdn = jnp.square(jax.nn.relu(jnp.einsum("bsd,df->bsf", u, w1)))
    return jnp.einsum("bsf,fd->bsd", hdn, w2)


def per_layer_embedding(h, p_i, g_norm, w_proj, w_gate):
    e = jnp.einsum("bsk,kd->bsd", p_i, w_proj)
    g = jax.nn.sigmoid(jnp.einsum("bsd,de->bse", rmsnorm(h, g_norm), w_gate))
    return g * e


def _fwd_setup_inputs(seed: int = 0) -> dict:
    key = jax.random.key(seed)
    ks = jax.random.split(key, 24)

    def nrm(k, shape, scale):
        return jax.random.normal(k, shape, jnp.float32) * scale

    def gain(k, shape):
        return 1.0 + 0.05 * jax.random.normal(k, shape, jnp.float32)

    D = D_MODEL
    return {
        "x": nrm(ks[0], (BATCH, SEQ, D), 1.0),
        "p": nrm(ks[1], (DEPTH, BATCH, SEQ, PLE_DIM), 1.0),
        "norm_mix": gain(ks[2], (DEPTH, D)),
        "norm_mlp": gain(ks[3], (DEPTH, D)),
        "norm_ple": gain(ks[4], (DEPTH, D)),
        "cf_w_pw1": nrm(ks[5], (N_LAYERS_A, D, 2 * D), D ** -0.5),
        "cf_b_pw1": nrm(ks[6], (N_LAYERS_A, 2 * D), 0.02),
        "cf_w_dw": nrm(ks[7], (N_LAYERS_A, CONV_A_WIDTH, D), CONV_A_WIDTH ** -0.5),
        "cf_b_dw": nrm(ks[8], (N_LAYERS_A, D), 0.02),
        "cf_norm": gain(ks[9], (N_LAYERS_A, D)),
        "cf_w_pw2": nrm(ks[10], (N_LAYERS_A, D, D), D ** -0.5),
        "cf_b_pw2": nrm(ks[11], (N_LAYERS_A, D), 0.02),
        "sc_w_in": nrm(ks[12], (N_LAYERS_B, D, 3 * D), D ** -0.5),
        "sc_w_conv": nrm(ks[13], (N_LAYERS_B, CONV_B_WIDTH, D), CONV_B_WIDTH ** -0.5),
        "sc_w_out": nrm(ks[14], (N_LAYERS_B, D, D), D ** -0.5),
        "mlp_w1": nrm(ks[15], (DEPTH, D, D_FF), D ** -0.5),
        "mlp_w2": nrm(ks[16], (DEPTH, D_FF, D), D_FF ** -0.5),
        "ple_w_proj": nrm(ks[17], (DEPTH, PLE_DIM, D), PLE_DIM ** -0.5),
        "ple_w_gate": nrm(ks[18], (DEPTH, D, D), D ** -0.5),
        "norm_final": gain(ks[19], (D,)),
    }


def _fwd_reference(x, p, norm_mix, norm_mlp, norm_ple,
              cf_w_pw1, cf_b_pw1, cf_w_dw, cf_b_dw, cf_norm, cf_w_pw2, cf_b_pw2,
              sc_w_in, sc_w_conv, sc_w_out,
              mlp_w1, mlp_w2, ple_w_proj, ple_w_gate, norm_final):
    h = x
    for i in range(DEPTH):
        j = i // N_MIXERS
        u = rmsnorm(h, norm_mix[i])
        if i % N_MIXERS == 0:
            m = conformer_conv_module(u, cf_w_pw1[j], cf_b_pw1[j], cf_w_dw[j], cf_b_dw[j],
                                      cf_norm[j], cf_w_pw2[j], cf_b_pw2[j])
        else:
            m = short_gated_conv(u, sc_w_in[j], sc_w_conv[j], sc_w_out[j])
        h = h + m
        h = h + squared_relu_mlp(rmsnorm(h, norm_mlp[i]), mlp_w1[i], mlp_w2[i])
        h = h + per_layer_embedding(h, p[i], norm_ple[i], ple_w_proj[i], ple_w_gate[i])
    return rmsnorm(h, norm_final)


import jax as _jax
import jax.numpy as _jnp

TWIN_FORMAT = 'train_step'
FWD_PARAMS = ['x', 'p', 'norm_mix', 'norm_mlp', 'norm_ple', 'cf_w_pw1', 'cf_b_pw1', 'cf_w_dw', 'cf_b_dw', 'cf_norm', 'cf_w_pw2', 'cf_b_pw2', 'sc_w_in', 'sc_w_conv', 'sc_w_out', 'mlp_w1', 'mlp_w2', 'ple_w_proj', 'ple_w_gate', 'norm_final']
TWIN_WEIGHTS = ['norm_mix', 'norm_mlp', 'norm_ple', 'cf_w_pw1', 'cf_b_pw1', 'cf_w_dw', 'cf_b_dw', 'cf_norm', 'cf_w_pw2', 'cf_b_pw2', 'sc_w_in', 'sc_w_conv', 'sc_w_out', 'mlp_w1', 'mlp_w2', 'ple_w_proj', 'ple_w_gate', 'norm_final']
TWIN_DIFF_INPUT = 'x'
TWIN_INPUTS = ['x', 'p', 'norm_mix', 'norm_mlp', 'norm_ple', 'cf_w_pw1', 'cf_b_pw1', 'cf_w_dw', 'cf_b_dw', 'cf_norm', 'cf_w_pw2', 'cf_b_pw2', 'sc_w_in', 'sc_w_conv', 'sc_w_out', 'mlp_w1', 'mlp_w2', 'ple_w_proj', 'ple_w_gate', 'norm_final', 'loss_target', 'm_norm_mix', 'm_norm_mlp', 'm_norm_ple', 'm_cf_w_pw1', 'm_cf_b_pw1', 'm_cf_w_dw', 'm_cf_b_dw', 'm_cf_norm', 'm_cf_w_pw2', 'm_cf_b_pw2', 'm_sc_w_in', 'm_sc_w_conv', 'm_sc_w_out', 'm_mlp_w1', 'm_mlp_w2', 'm_ple_w_proj', 'm_ple_w_gate', 'm_norm_final', 'v_norm_mix', 'v_norm_mlp', 'v_norm_ple', 'v_cf_w_pw1', 'v_cf_b_pw1', 'v_cf_w_dw', 'v_cf_b_dw', 'v_cf_norm', 'v_cf_w_pw2', 'v_cf_b_pw2', 'v_sc_w_in', 'v_sc_w_conv', 'v_sc_w_out', 'v_mlp_w1', 'v_mlp_w2', 'v_ple_w_proj', 'v_ple_w_gate', 'v_norm_final']
TWIN_OUTPUTS = ['loss', 'grad_x', 'grad_norm_mix', 'grad_norm_mlp', 'grad_norm_ple', 'grad_cf_w_pw1', 'grad_cf_b_pw1', 'grad_cf_w_dw', 'grad_cf_b_dw', 'grad_cf_norm', 'grad_cf_w_pw2', 'grad_cf_b_pw2', 'grad_sc_w_in', 'grad_sc_w_conv', 'grad_sc_w_out', 'grad_mlp_w1', 'grad_mlp_w2', 'grad_ple_w_proj', 'grad_ple_w_gate', 'grad_norm_final', 'delta_norm_mix', 'delta_norm_mlp', 'delta_norm_ple', 'delta_cf_w_pw1', 'delta_cf_b_pw1', 'delta_cf_w_dw', 'delta_cf_b_dw', 'delta_cf_norm', 'delta_cf_w_pw2', 'delta_cf_b_pw2', 'delta_sc_w_in', 'delta_sc_w_conv', 'delta_sc_w_out', 'delta_mlp_w1', 'delta_mlp_w2', 'delta_ple_w_proj', 'delta_ple_w_gate', 'delta_norm_final', 'new_m_norm_mix', 'new_m_norm_mlp', 'new_m_norm_ple', 'new_m_cf_w_pw1', 'new_m_cf_b_pw1', 'new_m_cf_w_dw', 'new_m_cf_b_dw', 'new_m_cf_norm', 'new_m_cf_w_pw2', 'new_m_cf_b_pw2', 'new_m_sc_w_in', 'new_m_sc_w_conv', 'new_m_sc_w_out', 'new_m_mlp_w1', 'new_m_mlp_w2', 'new_m_ple_w_proj', 'new_m_ple_w_gate', 'new_m_norm_final', 'new_v_norm_mix', 'new_v_norm_mlp', 'new_v_norm_ple', 'new_v_cf_w_pw1', 'new_v_cf_b_pw1', 'new_v_cf_w_dw', 'new_v_cf_b_dw', 'new_v_cf_norm', 'new_v_cf_w_pw2', 'new_v_cf_b_pw2', 'new_v_sc_w_in', 'new_v_sc_w_conv', 'new_v_sc_w_out', 'new_v_mlp_w1', 'new_v_mlp_w2', 'new_v_ple_w_proj', 'new_v_ple_w_gate', 'new_v_norm_final']
TWIN_LEAF_KINDS = {'loss': 'loss', 'grad_x': 'grad_x', 'grad_norm_mix': 'grad_w', 'grad_norm_mlp': 'grad_w', 'grad_norm_ple': 'grad_w', 'grad_cf_w_pw1': 'grad_w', 'grad_cf_b_pw1': 'grad_w', 'grad_cf_w_dw': 'grad_w', 'grad_cf_b_dw': 'grad_w', 'grad_cf_norm': 'grad_w', 'grad_cf_w_pw2': 'grad_w', 'grad_cf_b_pw2': 'grad_w', 'grad_sc_w_in': 'grad_w', 'grad_sc_w_conv': 'grad_w', 'grad_sc_w_out': 'grad_w', 'grad_mlp_w1': 'grad_w', 'grad_mlp_w2': 'grad_w', 'grad_ple_w_proj': 'grad_w', 'grad_ple_w_gate': 'grad_w', 'grad_norm_final': 'grad_w', 'delta_norm_mix': 'delta_w', 'delta_norm_mlp': 'delta_w', 'delta_norm_ple': 'delta_w', 'delta_cf_w_pw1': 'delta_w', 'delta_cf_b_pw1': 'delta_w', 'delta_cf_w_dw': 'delta_w', 'delta_cf_b_dw': 'delta_w', 'delta_cf_norm': 'delta_w', 'delta_cf_w_pw2': 'delta_w', 'delta_cf_b_pw2': 'delta_w', 'delta_sc_w_in': 'delta_w', 'delta_sc_w_conv': 'delta_w', 'delta_sc_w_out': 'delta_w', 'delta_mlp_w1': 'delta_w', 'delta_mlp_w2': 'delta_w', 'delta_ple_w_proj': 'delta_w', 'delta_ple_w_gate': 'delta_w', 'delta_norm_final': 'delta_w', 'new_m_norm_mix': 'new_m', 'new_m_norm_mlp': 'new_m', 'new_m_norm_ple': 'new_m', 'new_m_cf_w_pw1': 'new_m', 'new_m_cf_b_pw1': 'new_m', 'new_m_cf_w_dw': 'new_m', 'new_m_cf_b_dw': 'new_m', 'new_m_cf_norm': 'new_m', 'new_m_cf_w_pw2': 'new_m', 'new_m_cf_b_pw2': 'new_m', 'new_m_sc_w_in': 'new_m', 'new_m_sc_w_conv': 'new_m', 'new_m_sc_w_out': 'new_m', 'new_m_mlp_w1': 'new_m', 'new_m_mlp_w2': 'new_m', 'new_m_ple_w_proj': 'new_m', 'new_m_ple_w_gate': 'new_m', 'new_m_norm_final': 'new_m', 'new_v_norm_mix': 'new_v', 'new_v_norm_mlp': 'new_v', 'new_v_norm_ple': 'new_v', 'new_v_cf_w_pw1': 'new_v', 'new_v_cf_b_pw1': 'new_v', 'new_v_cf_w_dw': 'new_v', 'new_v_cf_b_dw': 'new_v', 'new_v_cf_norm': 'new_v', 'new_v_cf_w_pw2': 'new_v', 'new_v_cf_b_pw2': 'new_v', 'new_v_sc_w_in': 'new_v', 'new_v_sc_w_conv': 'new_v', 'new_v_sc_w_out': 'new_v', 'new_v_mlp_w1': 'new_v', 'new_v_mlp_w2': 'new_v', 'new_v_ple_w_proj': 'new_v', 'new_v_ple_w_gate': 'new_v', 'new_v_norm_final': 'new_v'}


def _forward(args):
    return _fwd_reference(*[args[k] for k in FWD_PARAMS])


def _output_shape():
    def fwd():
        inp = _fwd_setup_inputs(0)
        return _fwd_reference(*[inp[k] for k in FWD_PARAMS])
    out = _jax.eval_shape(fwd)
    return out.shape, out.dtype

N_MICROBATCH = 1
ADAM_LR = 0.001
ADAM_B1 = 0.9
ADAM_B2 = 0.999
ADAM_EPS = 1e-08
ADAM_WD = 0.01
ADAM_STEP = 10
PER_EXAMPLE_BATCH_AXIS = {'x': 0, 'p': 1, 'loss_target': 0}
SHARED_INPUTS = []
_WEIGHT_DTYPES = {'norm_mix': _jnp.float32, 'norm_mlp': _jnp.float32, 'norm_ple': _jnp.float32, 'cf_w_pw1': _jnp.float32, 'cf_b_pw1': _jnp.float32, 'cf_w_dw': _jnp.float32, 'cf_b_dw': _jnp.float32, 'cf_norm': _jnp.float32, 'cf_w_pw2': _jnp.float32, 'cf_b_pw2': _jnp.float32, 'sc_w_in': _jnp.float32, 'sc_w_conv': _jnp.float32, 'sc_w_out': _jnp.float32, 'mlp_w1': _jnp.float32, 'mlp_w2': _jnp.float32, 'ple_w_proj': _jnp.float32, 'ple_w_gate': _jnp.float32, 'norm_final': _jnp.float32}
MOMENT_SCALE = {'norm_mix': 6.833251e-02, 'norm_mlp': 7.506528e-02, 'norm_ple': 1.046030e-02, 'cf_w_pw1': 4.121046e-02, 'cf_b_pw1': 7.422126e-02, 'cf_w_dw': 5.488235e-02, 'cf_b_dw': 1.696558e-01, 'cf_norm': 8.463219e-02, 'cf_w_pw2': 6.235126e-02, 'cf_b_pw2': 1.982646e-01, 'sc_w_in': 4.235458e-02, 'sc_w_conv': 4.288453e-02, 'sc_w_out': 4.248327e-02, 'mlp_w1': 3.664129e-02, 'mlp_w2': 9.456037e-02, 'ple_w_proj': 2.526682e-02, 'ple_w_gate': 1.046113e-02, 'norm_final': 1.628623e+01}


def _to_microbatches(a, axis):
    t = _jnp.moveaxis(a, axis, 0)
    t = t.reshape((N_MICROBATCH, t.shape[0] // N_MICROBATCH) + t.shape[1:])
    return _jnp.moveaxis(t, 1, axis + 1)


def setup_inputs(seed: int = 0) -> dict:
    inp = _fwd_setup_inputs(seed)
    key = _jax.random.fold_in(_jax.random.key(seed), 7919)
    shape, _ = _output_shape()
    out = dict(inp)
    out["loss_target"] = _jax.random.normal(_jax.random.fold_in(key, 0), shape, _jnp.float32)
    for i, name in enumerate(TWIN_WEIGHTS):
        w = inp[name].astype(_jnp.float32)
        if MOMENT_SCALE is None:
            s = _jnp.sqrt(_jnp.mean(_jnp.square(w)) + 1e-30)
        else:
            s = MOMENT_SCALE[name]
        km, kv = _jax.random.split(_jax.random.fold_in(key, i + 1))
        out[name] = w
        out["m_" + name] = s * _jax.random.normal(km, w.shape, _jnp.float32)
        out["v_" + name] = (s * s) * _jax.random.uniform(kv, w.shape, _jnp.float32, 0.5, 1.5)
    if N_MICROBATCH > 1:
        for name, axis in PER_EXAMPLE_BATCH_AXIS.items():
            out[name] = _to_microbatches(out[name], axis)
    return {'x': out['x'], 'p': out['p'], 'norm_mix': out['norm_mix'], 'norm_mlp': out['norm_mlp'], 'norm_ple': out['norm_ple'], 'cf_w_pw1': out['cf_w_pw1'], 'cf_b_pw1': out['cf_b_pw1'], 'cf_w_dw': out['cf_w_dw'], 'cf_b_dw': out['cf_b_dw'], 'cf_norm': out['cf_norm'], 'cf_w_pw2': out['cf_w_pw2'], 'cf_b_pw2': out['cf_b_pw2'], 'sc_w_in': out['sc_w_in'], 'sc_w_conv': out['sc_w_conv'], 'sc_w_out': out['sc_w_out'], 'mlp_w1': out['mlp_w1'], 'mlp_w2': out['mlp_w2'], 'ple_w_proj': out['ple_w_proj'], 'ple_w_gate': out['ple_w_gate'], 'norm_final': out['norm_final'], 'loss_target': out['loss_target'], 'm_norm_mix': out['m_norm_mix'], 'm_norm_mlp': out['m_norm_mlp'], 'm_norm_ple': out['m_norm_ple'], 'm_cf_w_pw1': out['m_cf_w_pw1'], 'm_cf_b_pw1': out['m_cf_b_pw1'], 'm_cf_w_dw': out['m_cf_w_dw'], 'm_cf_b_dw': out['m_cf_b_dw'], 'm_cf_norm': out['m_cf_norm'], 'm_cf_w_pw2': out['m_cf_w_pw2'], 'm_cf_b_pw2': out['m_cf_b_pw2'], 'm_sc_w_in': out['m_sc_w_in'], 'm_sc_w_conv': out['m_sc_w_conv'], 'm_sc_w_out': out['m_sc_w_out'], 'm_mlp_w1': out['m_mlp_w1'], 'm_mlp_w2': out['m_mlp_w2'], 'm_ple_w_proj': out['m_ple_w_proj'], 'm_ple_w_gate': out['m_ple_w_gate'], 'm_norm_final': out['m_norm_final'], 'v_norm_mix': out['v_norm_mix'], 'v_norm_mlp': out['v_norm_mlp'], 'v_norm_ple': out['v_norm_ple'], 'v_cf_w_pw1': out['v_cf_w_pw1'], 'v_cf_b_pw1': out['v_cf_b_pw1'], 'v_cf_w_dw': out['v_cf_w_dw'], 'v_cf_b_dw': out['v_cf_b_dw'], 'v_cf_norm': out['v_cf_norm'], 'v_cf_w_pw2': out['v_cf_w_pw2'], 'v_cf_b_pw2': out['v_cf_b_pw2'], 'v_sc_w_in': out['v_sc_w_in'], 'v_sc_w_conv': out['v_sc_w_conv'], 'v_sc_w_out': out['v_sc_w_out'], 'v_mlp_w1': out['v_mlp_w1'], 'v_mlp_w2': out['v_mlp_w2'], 'v_ple_w_proj': out['v_ple_w_proj'], 'v_ple_w_gate': out['v_ple_w_gate'], 'v_norm_final': out['v_norm_final']}


def _loss(weights, diff, rest, loss_target):
    with _jax.named_scope("forward"):
        args = {**rest, TWIN_DIFF_INPUT: diff, **{k: w.astype(_WEIGHT_DTYPES[k]) for k, w in weights.items()}}
        y = _forward(args)
    with _jax.named_scope("loss_head"):
        err = _jnp.square(y.astype(_jnp.float32) - loss_target)
        return 0.5 * _jnp.sum(_jnp.mean(err, axis=-1)) if err.ndim else 0.5 * err


def _adamw(w, g, m, v):
    m = ADAM_B1 * m + (1.0 - ADAM_B1) * g
    v = ADAM_B2 * v + (1.0 - ADAM_B2) * _jnp.square(g)
    m_hat = m / (1.0 - ADAM_B1 ** ADAM_STEP)
    v_hat = v / (1.0 - ADAM_B2 ** ADAM_STEP)
    delta = -ADAM_LR * (m_hat / (_jnp.sqrt(v_hat) + ADAM_EPS) + ADAM_WD * w)
    return delta, m, v


def reference(x, p, norm_mix, norm_mlp, norm_ple, cf_w_pw1, cf_b_pw1, cf_w_dw, cf_b_dw, cf_norm, cf_w_pw2, cf_b_pw2, sc_w_in, sc_w_conv, sc_w_out, mlp_w1, mlp_w2, ple_w_proj, ple_w_gate, norm_final, loss_target, m_norm_mix, m_norm_mlp, m_norm_ple, m_cf_w_pw1, m_cf_b_pw1, m_cf_w_dw, m_cf_b_dw, m_cf_norm, m_cf_w_pw2, m_cf_b_pw2, m_sc_w_in, m_sc_w_conv, m_sc_w_out, m_mlp_w1, m_mlp_w2, m_ple_w_proj, m_ple_w_gate, m_norm_final, v_norm_mix, v_norm_mlp, v_norm_ple, v_cf_w_pw1, v_cf_b_pw1, v_cf_w_dw, v_cf_b_dw, v_cf_norm, v_cf_w_pw2, v_cf_b_pw2, v_sc_w_in, v_sc_w_conv, v_sc_w_out, v_mlp_w1, v_mlp_w2, v_ple_w_proj, v_ple_w_gate, v_norm_final):
    given = dict(x=x, p=p, norm_mix=norm_mix, norm_mlp=norm_mlp, norm_ple=norm_ple, cf_w_pw1=cf_w_pw1, cf_b_pw1=cf_b_pw1, cf_w_dw=cf_w_dw, cf_b_dw=cf_b_dw, cf_norm=cf_norm, cf_w_pw2=cf_w_pw2, cf_b_pw2=cf_b_pw2, sc_w_in=sc_w_in, sc_w_conv=sc_w_conv, sc_w_out=sc_w_out, mlp_w1=mlp_w1, mlp_w2=mlp_w2, ple_w_proj=ple_w_proj, ple_w_gate=ple_w_gate, norm_final=norm_final, loss_target=loss_target, m_norm_mix=m_norm_mix, m_norm_mlp=m_norm_mlp, m_norm_ple=m_norm_ple, m_cf_w_pw1=m_cf_w_pw1, m_cf_b_pw1=m_cf_b_pw1, m_cf_w_dw=m_cf_w_dw, m_cf_b_dw=m_cf_b_dw, m_cf_norm=m_cf_norm, m_cf_w_pw2=m_cf_w_pw2, m_cf_b_pw2=m_cf_b_pw2, m_sc_w_in=m_sc_w_in, m_sc_w_conv=m_sc_w_conv, m_sc_w_out=m_sc_w_out, m_mlp_w1=m_mlp_w1, m_mlp_w2=m_mlp_w2, m_ple_w_proj=m_ple_w_proj, m_ple_w_gate=m_ple_w_gate, m_norm_final=m_norm_final, v_norm_mix=v_norm_mix, v_norm_mlp=v_norm_mlp, v_norm_ple=v_norm_ple, v_cf_w_pw1=v_cf_w_pw1, v_cf_b_pw1=v_cf_b_pw1, v_cf_w_dw=v_cf_w_dw, v_cf_b_dw=v_cf_b_dw, v_cf_norm=v_cf_norm, v_cf_w_pw2=v_cf_w_pw2, v_cf_b_pw2=v_cf_b_pw2, v_sc_w_in=v_sc_w_in, v_sc_w_conv=v_sc_w_conv, v_sc_w_out=v_sc_w_out, v_mlp_w1=v_mlp_w1, v_mlp_w2=v_mlp_w2, v_ple_w_proj=v_ple_w_proj, v_ple_w_gate=v_ple_w_gate, v_norm_final=v_norm_final)
    weights = {n: given[n] for n in TWIN_WEIGHTS}
    shared = {n: given[n] for n in SHARED_INPUTS}
    per_example = {n: given[n] for n in ['x', 'p']}
    grad_fn = _jax.value_and_grad(_loss, argnums=(0, 1))

    def one_microbatch(ex, loss_target):
        ex = dict(ex)
        diff = ex.pop(TWIN_DIFF_INPUT)
        return grad_fn(weights, diff, {**shared, **ex}, loss_target)

    if N_MICROBATCH == 1:
        loss, (grad_w, grad_x) = one_microbatch(per_example, given["loss_target"])
    else:
        def body(carry, xs):
            loss_sum, grad_sum = carry
            l_k, (gw_k, gx_k) = one_microbatch(xs[0], xs[1])
            with _jax.named_scope("update"):
                return (loss_sum + l_k, _jax.tree.map(_jnp.add, grad_sum, gw_k)), gx_k

        init = (_jnp.zeros((), _jnp.float32), _jax.tree.map(_jnp.zeros_like, weights))
        (loss, grad_w), grad_x = _jax.lax.scan(body, init, (per_example, given["loss_target"]))
    with _jax.named_scope("update"):
        delta_w, new_m, new_v = {}, {}, {}
        for n in TWIN_WEIGHTS:
            delta_w[n], new_m[n], new_v[n] = _adamw(weights[n], grad_w[n], given["m_" + n], given["v_" + n])
    return (loss, grad_x, *[grad_w[n] for n in TWIN_WEIGHTS], *[delta_w[n] for n in TWIN_WEIGHTS],
            *[new_m[n] for n in TWIN_WEIGHTS], *[new_v[n] for n in TWIN_WEIGHTS])
```

```python
import functools

import jax
import jax.numpy as jnp
from jax import lax
from jax.experimental import pallas as pl
from jax.experimental.pallas import tpu as pltpu

F32 = jnp.float32
BF16 = jnp.bfloat16
EPS = 1e-6
NDEV = 8
N_PEERS = NDEV - 1
MESH = pl.DeviceIdType.MESH

ADAM_LR = 0.001
ADAM_B1 = 0.9
ADAM_B2 = 0.999
ADAM_EPS = 1e-08
ADAM_WD = 0.01
ADAM_STEP = 10

V7X_VMEM_BYTES = 64 * 1024 * 1024
VMEM_LIMIT_MAX = 56 * 1024 * 1024
SUBLANES = 8
LANES = 128
CONV_A_TAPS = 31
CONV_A_HALO = 32
CONV_B_TAPS = 3
CONV_B_HALO = 8


def _nbytes(shape, dtype):
    n = 1
    for s in shape:
        if s is not None:
            n *= s
    return n * jnp.dtype(dtype).itemsize


def _vmem_limit(block_bytes, scratch_bytes=0):
    need = 2 * block_bytes + scratch_bytes
    return int(min(VMEM_LIMIT_MAX, max(32 * 1024 * 1024, need + need // 2 + (4 << 20))))


def _params(sem, block_bytes, scratch_bytes=0):
    return pltpu.CompilerParams(dimension_semantics=sem, vmem_limit_bytes=_vmem_limit(block_bytes, scratch_bytes))


_DIMS = {
    "nn": (((1,), (0,)), ((), ())),
    "nt": (((1,), (1,)), ((), ())),
    "tn": (((0,), (0,)), ((), ())),
}


def _mm(name, dims, grid, acc_shape, a, a_spec, b, b_spec, extras, outs, epi):
    ni, nj, nk = grid
    n_ex, n_out = len(extras), len(outs)
    dn = _DIMS[dims]

    def body(*refs):
        a_ref, b_ref = refs[0], refs[1]
        ex_refs = refs[2:2 + n_ex]
        out_refs = refs[2 + n_ex:2 + n_ex + n_out]
        d = lax.dot_general(a_ref[...].astype(BF16), b_ref[...].astype(BF16), dn, preferred_element_type=F32)

        def finish(acc):
            res = epi(acc, *[r[...] for r in ex_refs])
            for o_ref, r in zip(out_refs, res):
                o_ref[...] = r.astype(o_ref.dtype)

        if nk == 1:
            finish(d)
        else:
            acc_ref = refs[2 + n_ex + n_out]
            k = pl.program_id(2)

            @pl.when(k == 0)
            def _():
                acc_ref[...] = d

            @pl.when(jnp.logical_and(k > 0, k < nk - 1))
            def _():
                acc_ref[...] += d

            @pl.when(k == nk - 1)
            def _():
                finish(acc_ref[...] + d)

    blk = _nbytes(a_spec.block_shape, a.dtype) + _nbytes(b_spec.block_shape, b.dtype)
    for arr, spec in list(extras) + list(outs):
        blk += _nbytes(spec.block_shape, arr.dtype)
    acc_bytes = _nbytes(acc_shape, F32)
    scratch = [pltpu.VMEM(acc_shape, F32)] if nk > 1 else []
    return pl.pallas_call(
        body,
        name=name,
        grid=grid,
        in_specs=[a_spec, b_spec] + [s for _, s in extras],
        out_specs=[s for _, s in outs],
        out_shape=[o for o, _ in outs],
        scratch_shapes=scratch,
        compiler_params=_params(("parallel", "parallel", "arbitrary"), blk, 3 * acc_bytes),
    )(a, b, *[e for e, _ in extras])


def _tile(n, pref):
    if n <= pref:
        return n
    t = pref - pref % LANES
    while t > LANES and n % t:
        t -= LANES
    assert n % t == 0, (n, pref)
    return t


def _row_tile(n, pref):
    if n <= pref:
        return n
    t = max(SUBLANES, pref - pref % SUBLANES)
    while t > SUBLANES and n % t:
        t -= SUBLANES
    assert n % t == 0, (n, pref)
    return t


def _id_epi(acc):
    return (acc,)


def mm_x_wcol(name, x, w, layer, extras=(), outs_dtypes=(F32,), epi=_id_epi, tm=1024, tn=512):
    m, kdim = x.shape
    c = w.shape[3]
    n = NDEV * c
    tm, tn = _tile(m, tm), _tile(c, tn)
    tk = _tile(kdim, 2048)
    grid = (m // tm, n // tn, kdim // tk)
    per = c // tn
    a_spec = pl.BlockSpec((tm, tk), lambda i, j, k: (i, k))
    b_spec = pl.BlockSpec((None, None, tk, tn), lambda i, j, k: (j // per, layer, k, j % per))
    ex = [(e, _ex_spec(e, kind, tm, tn)) for e, kind in extras]
    o_spec = pl.BlockSpec((tm, tn), lambda i, j, k: (i, j))
    outs = [(jax.ShapeDtypeStruct((m, n), dt), o_spec) for dt in outs_dtypes]
    return _mm(name, "nn", grid, (tm, tn), x, a_spec, w, b_spec, ex, outs, epi)


def mm_x_wrow(name, x, w, layer, extras=(), outs_dtypes=(F32,), epi=_id_epi, tm=1024, tn=1024):
    m, kdim = x.shape
    r, n = w.shape[2], w.shape[3]
    assert kdim == NDEV * r
    tm, tn = _tile(m, tm), _tile(n, tn)
    tk = _tile(r, 2048)
    per = r // tk
    grid = (m // tm, n // tn, kdim // tk)
    a_spec = pl.BlockSpec((tm, tk), lambda i, j, k: (i, k))
    b_spec = pl.BlockSpec((None, None, tk, tn), lambda i, j, k: (k // per, layer, k % per, j))
    ex = [(e, _ex_spec(e, kind, tm, tn)) for e, kind in extras]
    o_spec = pl.BlockSpec((tm, tn), lambda i, j, k: (i, j))
    outs = [(jax.ShapeDtypeStruct((m, n), dt), o_spec) for dt in outs_dtypes]
    return _mm(name, "nn", grid, (tm, tn), x, a_spec, w, b_spec, ex, outs, epi)


def mm_dy_wcol_t(name, dy, w, layer, extras=(), outs_dtypes=(F32,), epi=_id_epi, tm=1024, tn=1024):
    m, n = dy.shape
    kdim, c = w.shape[2], w.shape[3]
    assert n == NDEV * c
    tm, tn = _tile(m, tm), _tile(kdim, tn)
    tk = _tile(c, 2048)
    per = c // tk
    grid = (m // tm, kdim // tn, n // tk)
    a_spec = pl.BlockSpec((tm, tk), lambda i, j, k: (i, k))
    b_spec = pl.BlockSpec((None, None, tn, tk), lambda i, j, k: (k // per, layer, j, k % per))
    ex = [(e, _ex_spec(e, kind, tm, tn)) for e, kind in extras]
    o_spec = pl.BlockSpec((tm, tn), lambda i, j, k: (i, j))
    outs = [(jax.ShapeDtypeStruct((m, kdim), dt), o_spec) for dt in outs_dtypes]
    return _mm(name, "nt", grid, (tm, tn), dy, a_spec, w, b_spec, ex, outs, epi)


def mm_dy_wrow_t(name, dy, w, layer, extras=(), outs_dtypes=(F32,), epi=_id_epi, tm=1024, tn=512):
    m, n = dy.shape
    r = w.shape[2]
    assert n == w.shape[3]
    tm, tn = _tile(m, tm), _tile(r, tn)
    tk = _tile(n, 2048)
    per = r // tn
    grid = (m // tm, NDEV * r // tn, n // tk)
    a_spec = pl.BlockSpec((tm, tk), lambda i, j, k: (i, k))
    b_spec = pl.BlockSpec((None, None, tn, tk), lambda i, j, k: (j // per, layer, j % per, k))
    ex = [(e, _ex_spec(e, kind, tm, tn)) for e, kind in extras]
    o_spec = pl.BlockSpec((tm, tn), lambda i, j, k: (i, j))
    outs = [(jax.ShapeDtypeStruct((m, NDEV * r), dt), o_spec) for dt in outs_dtypes]
    return _mm(name, "nt", grid, (tm, tn), dy, a_spec, w, b_spec, ex, outs, epi)


def mm_xt_dy(name, x, dy, col_shards, tm=1024, tn=512):
    m, kdim = x.shape
    n = dy.shape[1]
    tk = _tile(m, 2048)
    if col_shards:
        c = n // NDEV
        tm, tn = _tile(kdim, tm), _tile(c, tn)
        per = c // tn
        out = jax.ShapeDtypeStruct((NDEV, kdim, c), BF16)
        o_spec = pl.BlockSpec((None, tm, tn), lambda i, j, k: (j // per, i, j % per))
    else:
        r = kdim // NDEV
        tm, tn = _tile(r, tm), _tile(n, tn)
        per = r // tm
        out = jax.ShapeDtypeStruct((NDEV, r, n), BF16)
        o_spec = pl.BlockSpec((None, tm, tn), lambda i, j, k: (i // per, i % per, j))
    grid = (kdim // tm, n // tn, m // tk)
    a_spec = pl.BlockSpec((tk, tm), lambda i, j, k: (k, i))
    b_spec = pl.BlockSpec((tk, tn), lambda i, j, k: (k, j))
    return _mm(name, "tn", grid, (tm, tn), x, a_spec, dy, b_spec, [], [(out, o_spec)], _id_epi)[0]


def _ex_spec(e, kind, tm, tn):
    if kind == "tile":
        return pl.BlockSpec((tm, tn), lambda i, j, k: (i, j))
    if kind == "row":
        return pl.BlockSpec((1, tn), lambda i, j, k: (0, j))
    raise ValueError(kind)


def _rows_call(name, body, n_rows, ts, ins, outs, scratch=(), scratch_bytes=0):
    blk = sum(_nbytes(s.block_shape, a.dtype) for a, s in list(ins) + list(outs))
    return pl.pallas_call(
        body,
        name=name,
        grid=(n_rows // ts,),
        in_specs=[s for _, s in ins],
        out_specs=[s for _, s in outs],
        out_shape=[o for o, _ in outs],
        scratch_shapes=list(scratch),
        compiler_params=_params(("arbitrary",), blk, scratch_bytes + 4 * blk // 2),
    )(*[a for a, _ in ins])


def _blk(ts, d):
    return pl.BlockSpec((ts, d), lambda i: (i, 0))


def _full(shape):
    return pl.BlockSpec(shape, lambda i: tuple(0 for _ in shape))


def _rowsum8(v):
    t, d = v.shape
    return jnp.sum(v.reshape(t // SUBLANES, SUBLANES, d), axis=0)


def _accumulate(ref, val):
    @pl.when(pl.program_id(0) == 0)
    def _():
        ref[...] = val

    @pl.when(pl.program_id(0) > 0)
    def _():
        ref[...] += val


def _rstd(x):
    return lax.rsqrt(jnp.mean(x * x, axis=-1, keepdims=True) + EPS)


def _rms_bwd_math(dy, x, g):
    r = _rstd(x)
    gdy = dy * g
    c = jnp.mean(gdy * x, axis=-1, keepdims=True)
    dx = r * gdy - x * (r * r * r * c)
    return dx, dy * (x * r)


def rms_fwd(name, h, g, ts=512):
    s, d = h.shape
    ts = min(ts, s)

    def body(h_ref, g_ref, u_ref):
        x = h_ref[...]
        u_ref[...] = ((x * _rstd(x)) * g_ref[...]).astype(BF16)

    return _rows_call(name, body, s, ts, [(h, _blk(ts, d)), (g, _full((1, d)))],
                      [(jax.ShapeDtypeStruct((s, d), BF16), _blk(ts, d))])[0]


def rms_bwd(name, du, h, g, dres, ts=256):
    s, d = h.shape
    ts = min(ts, s)

    def body(du_ref, h_ref, g_ref, dres_ref, dh_ref, dhb_ref, dg_ref, cs_ref):
        dx, dg = _rms_bwd_math(du_ref[...], h_ref[...], g_ref[...])
        dh = dres_ref[...] + dx
        dh_ref[...] = dh
        dhb_ref[...] = dh.astype(BF16)
        _accumulate(dg_ref, _rowsum8(dg))
        _accumulate(cs_ref, _rowsum8(dh))

    return _rows_call(
        name, body, s, ts,
        [(du, _blk(ts, d)), (h, _blk(ts, d)), (g, _full((1, d))), (dres, _blk(ts, d))],
        [(jax.ShapeDtypeStruct((s, d), F32), _blk(ts, d)), (jax.ShapeDtypeStruct((s, d), BF16), _blk(ts, d)),
         (jax.ShapeDtypeStruct((SUBLANES, d), F32), _full((SUBLANES, d))),
         (jax.ShapeDtypeStruct((SUBLANES, d), F32), _full((SUBLANES, d)))])


def final_loss(name, h, g, target, ts=256):
    s, d = h.shape
    ts = min(ts, s)

    def body(h_ref, g_ref, t_ref, loss_ref, dh_ref, dg_ref):
        x = h_ref[...]
        gf = g_ref[...]
        y = (x * _rstd(x)) * gf
        err = y - t_ref[...]
        _accumulate(loss_ref, _rowsum8(err * err) * (0.5 / d))
        dx, dg = _rms_bwd_math(err * (1.0 / d), x, gf)
        dh_ref[...] = dx
        _accumulate(dg_ref, _rowsum8(dg))

    return _rows_call(
        name, body, s, ts,
        [(h, _blk(ts, d)), (g, _full((1, d))), (target, _blk(ts, d))],
        [(jax.ShapeDtypeStruct((SUBLANES, d), F32), _full((SUBLANES, d))),
         (jax.ShapeDtypeStruct((s, d), F32), _blk(ts, d)),
         (jax.ShapeDtypeStruct((SUBLANES, d), F32), _full((SUBLANES, d)))])


def ple_bwd_elem(name, dh, g, e, ts=512):
    s, d = dh.shape
    ts = min(ts, s)

    def body(dh_ref, g_ref, e_ref, de_ref, dgl_ref):
        dh_v, g_v = dh_ref[...], g_ref[...]
        de_ref[...] = (dh_v * g_v).astype(BF16)
        dgl_ref[...] = (dh_v * e_ref[...] * (g_v * (1.0 - g_v))).astype(BF16)

    return _rows_call(name, body, s, ts, [(dh, _blk(ts, d)), (g, _blk(ts, d)), (e, _blk(ts, d))],
                      [(jax.ShapeDtypeStruct((s, d), BF16), _blk(ts, d)),
                       (jax.ShapeDtypeStruct((s, d), BF16), _blk(ts, d))])


def _chunks(n_rows, d, fn):
    lc = min(256, d)
    rc = min(64, n_rows)

    def lane_body(c, carry):
        lanes = pl.ds(pl.multiple_of(c * lc, lc), lc)
        for r in range(n_rows // rc):
            fn(r * rc, rc, lanes)
        return carry

    lax.fori_loop(0, d // lc, lane_body, 0)


def _prev_halo_spec(ts, halo, width):
    per = ts // halo
    return pl.BlockSpec((halo, width), lambda i: (jnp.maximum(i * per - 1, 0), 0))


def _next_halo_spec(ts, halo, width, n_rows):
    per = ts // halo
    last = n_rows // halo - 1
    return pl.BlockSpec((halo, width), lambda i: (jnp.minimum((i + 1) * per, last), 0))


def cf_fwd_mid(name, a, w_dw, b_dw, gn, ts=256):
    s, d2 = a.shape
    d = d2 // 2
    ts = min(ts, s)
    hl = CONV_A_HALO
    off = hl - (CONV_A_TAPS - 1)

    def body(a_ref, ah_ref, w_ref, b_ref, gn_ref, v0_ref, v1_ref, v3_ref, buf):
        first = pl.program_id(0) == 0
        halo = ah_ref[...]
        hv0 = halo[:, :d] * jax.nn.sigmoid(halo[:, d:])
        buf[pl.ds(0, hl), :] = jnp.where(first, 0.0, hv0)
        main = a_ref[...]
        v0 = main[:, :d] * jax.nn.sigmoid(main[:, d:])
        buf[pl.ds(hl, ts), :] = v0
        v0_ref[...] = v0

        def conv(r0, rc, lanes):
            acc = jnp.zeros((rc, lanes.size), F32)
            for k in range(CONV_A_TAPS):
                acc = acc + w_ref[pl.ds(k, 1), lanes] * buf[pl.ds(r0 + off + k, rc), lanes]
            v1_ref[pl.ds(r0, rc), lanes] = acc + b_ref[:, lanes]

        _chunks(ts, d, conv)
        v1 = v1_ref[...]
        v2 = (v1 * _rstd(v1)) * gn_ref[...]
        v3_ref[...] = (v2 * jax.nn.sigmoid(v2)).astype(BF16)

    return _rows_call(
        name, body, s, ts,
        [(a, _blk(ts, d2)), (a, _prev_halo_spec(ts, hl, d2)), (w_dw, _full(w_dw.shape)),
         (b_dw, _full((1, d))), (gn, _full((1, d)))],
        [(jax.ShapeDtypeStruct((s, d), F32), _blk(ts, d)), (jax.ShapeDtypeStruct((s, d), F32), _blk(ts, d)),
         (jax.ShapeDtypeStruct((s, d), BF16), _blk(ts, d))],
        scratch=[pltpu.VMEM((hl + ts, d), F32)], scratch_bytes=_nbytes((hl + ts, d), F32))


def cf_bwd_rows(name, dv3, v1, gn, ts=256):
    s, d = v1.shape
    ts = min(ts, s)

    def body(dv3_ref, v1_ref, gn_ref, dv1_ref, dgn_ref, db_ref):
        v1 = v1_ref[...]
        gn_v = gn_ref[...]
        v2 = (v1 * _rstd(v1)) * gn_v
        sg = jax.nn.sigmoid(v2)
        dv2 = dv3_ref[...] * (sg * (1.0 + v2 * (1.0 - sg)))
        dv1, dgn = _rms_bwd_math(dv2, v1, gn_v)
        dv1_ref[...] = dv1
        _accumulate(dgn_ref, _rowsum8(dgn))
        _accumulate(db_ref, _rowsum8(dv1))

    return _rows_call(
        name, body, s, ts, [(dv3, _blk(ts, d)), (v1, _blk(ts, d)), (gn, _full((1, d)))],
        [(jax.ShapeDtypeStruct((s, d), F32), _blk(ts, d)),
         (jax.ShapeDtypeStruct((SUBLANES, d), F32), _full((SUBLANES, d))),
         (jax.ShapeDtypeStruct((SUBLANES, d), F32), _full((SUBLANES, d)))])


def cf_bwd_conv(name, dv1, v0, a, w_dw, ts=256):
    s, d = dv1.shape
    ts = min(ts, s)
    hl = CONV_A_HALO
    taps = CONV_A_TAPS
    off = hl - (taps - 1)
    last_blk = s // ts - 1

    def body(dv1_ref, dv1n_ref, v0_ref, v0p_ref, a_ref, w_ref, da_ref, dw_ref, db_ref, dbuf, vbuf, dv0_buf, dw_acc):
        i = pl.program_id(0)
        dbuf[pl.ds(0, ts), :] = dv1_ref[...]
        dbuf[pl.ds(ts, hl), :] = jnp.where(i == last_blk, 0.0, dv1n_ref[...])
        vbuf[pl.ds(0, hl), :] = jnp.where(i == 0, 0.0, v0p_ref[...])
        vbuf[pl.ds(hl, ts), :] = v0_ref[...]

        @pl.when(i == 0)
        def _():
            dw_acc[...] = jnp.zeros_like(dw_acc)

        def conv_t(r0, rc, lanes):
            g = dbuf[pl.ds(r0, rc), lanes]
            acc = jnp.zeros((rc, lanes.size), F32)
            for k in range(taps):
                acc = acc + w_ref[pl.ds(k, 1), lanes] * dbuf[pl.ds(r0 + taps - 1 - k, rc), lanes]
                prod = g * vbuf[pl.ds(r0 + off + k, rc), lanes]
                dw_acc[pl.ds(k * SUBLANES, SUBLANES), lanes] += _rowsum8(prod)
            dv0_buf[pl.ds(r0, rc), lanes] = acc

        _chunks(ts, d, conv_t)
        dv0 = dv0_buf[...]
        av = a_ref[...]
        val, sg = av[:, :d], jax.nn.sigmoid(av[:, d:])
        dval = dv0 * sg
        dgate = dv0 * val * (sg * (1.0 - sg))
        da_ref[:, :d] = dval.astype(BF16)
        da_ref[:, d:] = dgate.astype(BF16)
        _accumulate(db_ref.at[:, pl.ds(0, d)], _rowsum8(dval))
        _accumulate(db_ref.at[:, pl.ds(d, d)], _rowsum8(dgate))

        @pl.when(i == last_blk)
        def _():
            dw_ref[...] = jnp.sum(dw_acc[...].reshape(hl, SUBLANES, d), axis=1)

    scratch = [pltpu.VMEM((ts + hl, d), F32), pltpu.VMEM((hl + ts, d), F32), pltpu.VMEM((ts, d), F32),
               pltpu.VMEM((hl * SUBLANES, d), F32)]
    sbytes = _nbytes((3 * ts + 2 * hl + hl * SUBLANES, d), F32)
    return _rows_call(
        name, body, s, ts,
        [(dv1, _blk(ts, d)), (dv1, _next_halo_spec(ts, hl, d, s)), (v0, _blk(ts, d)), (v0, _prev_halo_spec(ts, hl, d)),
         (a, _blk(ts, 2 * d)), (w_dw, _full(w_dw.shape))],
        [(jax.ShapeDtypeStruct((s, 2 * d), BF16), _blk(ts, 2 * d)),
         (jax.ShapeDtypeStruct((hl, d), F32), _full((hl, d))),
         (jax.ShapeDtypeStruct((SUBLANES, 2 * d), F32), _full((SUBLANES, 2 * d)))],
        scratch=scratch, scratch_bytes=sbytes)


def sc_fwd_mid(name, bcv, w_conv, ts=256):
    s, d3 = bcv.shape
    d = d3 // 3
    ts = min(ts, s)
    hl = CONV_B_HALO
    off = hl - (CONV_B_TAPS - 1)

    def body(x_ref, xp_ref, w_ref, y_ref, buf):
        hp = xp_ref[...]
        buf[pl.ds(0, hl), :] = jnp.where(pl.program_id(0) == 0, 0.0, hp[:, d:2 * d] * hp[:, 2 * d:])
        buf[pl.ds(hl, ts), :] = x_ref[:, d:2 * d] * x_ref[:, 2 * d:]
        cc = jnp.zeros((ts, d), F32)
        for k in range(CONV_B_TAPS):
            cc = cc + w_ref[pl.ds(k, 1), :] * buf[pl.ds(off + k, ts), :]
        y_ref[...] = (x_ref[:, :d] * cc).astype(BF16)

    return _rows_call(
        name, body, s, ts,
        [(bcv, _blk(ts, d3)), (bcv, _prev_halo_spec(ts, hl, d3)), (w_conv, _full(w_conv.shape))],
        [(jax.ShapeDtypeStruct((s, d), BF16), _blk(ts, d))],
        scratch=[pltpu.VMEM((hl + ts, d), F32)], scratch_bytes=_nbytes((hl + ts, d), F32))[0]


def sc_bwd_mid(name, dy, bcv, w_conv, ts=256):
    s, d3 = bcv.shape
    d = d3 // 3
    ts = min(ts, s)
    hl = CONV_B_HALO
    taps = CONV_B_TAPS
    off = hl - (taps - 1)
    last_blk = s // ts - 1

    def body(dy_ref, dyn_ref, x_ref, xp_ref, xn_ref, w_ref, dx_ref, dw_ref, cvbuf, dbuf, dw_acc):
        i = pl.program_id(0)
        hp = xp_ref[...]
        cvbuf[pl.ds(0, hl), :] = jnp.where(i == 0, 0.0, hp[:, d:2 * d] * hp[:, 2 * d:])
        gb, gc, v = x_ref[:, :d], x_ref[:, d:2 * d], x_ref[:, 2 * d:]
        cvbuf[pl.ds(hl, ts), :] = gc * v
        dy_v = dy_ref[...]
        dcc = dy_v * gb
        dbuf[pl.ds(0, ts), :] = dcc
        dbuf[pl.ds(ts, hl), :] = jnp.where(i == last_blk, 0.0, dyn_ref[...] * xn_ref[:, :d])

        @pl.when(i == 0)
        def _():
            dw_acc[...] = jnp.zeros_like(dw_acc)

        cc = jnp.zeros((ts, d), F32)
        dcv = jnp.zeros((ts, d), F32)
        for k in range(taps):
            win = cvbuf[pl.ds(off + k, ts), :]
            cc = cc + w_ref[pl.ds(k, 1), :] * win
            dcv = dcv + w_ref[pl.ds(k, 1), :] * dbuf[pl.ds(taps - 1 - k, ts), :]
            dw_acc[pl.ds(k * SUBLANES, SUBLANES), :] += _rowsum8(dcc * win)
        dx_ref[:, :d] = (dy_v * cc).astype(BF16)
        dx_ref[:, d:2 * d] = (dcv * v).astype(BF16)
        dx_ref[:, 2 * d:] = (dcv * gc).astype(BF16)

        @pl.when(i == last_blk)
        def _():
            dw_ref[...] = jnp.sum(dw_acc[...].reshape(hl, SUBLANES, d), axis=1)

    scratch = [pltpu.VMEM((hl + ts, d), F32), pltpu.VMEM((ts + hl, d), F32), pltpu.VMEM((hl * SUBLANES, d), F32)]
    sbytes = _nbytes((2 * ts + 2 * hl + hl * SUBLANES, d), F32)
    return _rows_call(
        name, body, s, ts,
        [(dy, _blk(ts, d)), (dy, _next_halo_spec(ts, hl, d, s)), (bcv, _blk(ts, d3)), (bcv, _prev_halo_spec(ts, hl, d3)),
         (bcv, _next_halo_spec(ts, hl, d3, s)), (w_conv, _full(w_conv.shape))],
        [(jax.ShapeDtypeStruct((s, d3), BF16), _blk(ts, d3)), (jax.ShapeDtypeStruct((hl, d), F32), _full((hl, d)))],
        scratch=scratch, scratch_bytes=sbytes)


def _row(a, i):
    return lax.slice_in_dim(a, i, i + 1, axis=0)


def _local_step(x, p, target, small, gw, conv_w):
    depth = p.shape[0]
    acts = []
    h = x
    for i in range(depth):
        j = i // 2
        act = {"h": h}
        u = rms_fwd(f"rms_mix_{i}", h, _row(small["norm_mix"], i))
        act["u"] = u
        if i % 2 == 0:
            a = mm_x_wcol(f"cf_pw1_{i}", u, gw["cf_w_pw1"][j], 0, extras=[(_row(small["cf_b_pw1"], j), "row")],
                          epi=lambda acc, b: (acc + b,))[0]
            v0, v1, v3 = cf_fwd_mid(f"cf_mid_{i}", a, conv_w["cf"][j], _row(small["cf_b_dw"], j), _row(small["cf_norm"], j))
            act.update(a=a, v0=v0, v1=v1, v3=v3)
            h1 = mm_x_wrow(f"cf_pw2_{i}", v3, gw["cf_w_pw2"][j], 0,
                           extras=[(_row(small["cf_b_pw2"], j), "row"), (h, "tile")],
                           epi=lambda acc, b, res: (res + (acc + b),))[0]
        else:
            bcv = mm_x_wcol(f"sc_in_{i}", u, gw["sc_w_in"][j], 0, tn=768)[0]
            y = sc_fwd_mid(f"sc_mid_{i}", bcv, conv_w["sc"][j])
            act.update(bcv=bcv, y=y)
            h1 = mm_x_wrow(f"sc_out_{i}", y, gw["sc_w_out"][j], 0, extras=[(h, "tile")],
                           epi=lambda acc, res: (res + acc,))[0]
        act["h1"] = h1
        u2 = rms_fwd(f"rms_mlp_{i}", h1, _row(small["norm_mlp"], i))
        z, hd = mm_x_wcol(f"mlp_w1_{i}", u2, gw["mlp_w1"][i], 0, outs_dtypes=(F32, BF16),
                          epi=lambda acc: (acc, jnp.square(jnp.maximum(acc, 0.0))))
        h2 = mm_x_wrow(f"mlp_w2_{i}", hd, gw["mlp_w2"][i], 0, extras=[(h1, "tile")],
                       epi=lambda acc, res: (res + acc,))[0]
        act.update(u2=u2, z=z, hd=hd, h2=h2)
        n3 = rms_fwd(f"rms_ple_{i}", h2, _row(small["norm_ple"], i))
        e = mm_x_wcol(f"ple_proj_{i}", p[i], gw["ple_w_proj"][i], 0)[0]

        def ple_epi(acc, e_t, res):
            g_t = jax.nn.sigmoid(acc)
            return g_t, res + g_t * e_t

        g, h3 = mm_x_wrow(f"ple_gate_{i}", n3, gw["ple_w_gate"][i], 0, extras=[(e, "tile"), (h2, "tile")],
                          outs_dtypes=(F32, F32), epi=ple_epi)
        act.update(n3=n3, e=e, g=g)
        acts.append(act)
        h = h3

    loss_part, dh, dg_final = final_loss("final_loss", h, small["norm_final"], target)
    wg = {k: [None] * len(gw[k]) for k in gw}
    sg = {k: [None] * small[k].shape[0] for k in small if k != "norm_final"}
    sg["norm_final"] = [dg_final]
    sg["cf_w_dw"] = [None] * conv_w["cf"].shape[0]
    sg["sc_w_conv"] = [None] * conv_w["sc"].shape[0]

    for i in reversed(range(depth)):
        j = i // 2
        act = acts[i]
        de, dgl = ple_bwd_elem(f"ple_bwd_{i}", dh, act["g"], act["e"])
        wg["ple_w_proj"][i] = mm_xt_dy(f"d_ple_proj_{i}", p[i], de, True)
        wg["ple_w_gate"][i] = mm_xt_dy(f"d_ple_gate_{i}", act["n3"], dgl, False)
        dn3 = mm_dy_wrow_t(f"dn3_{i}", dgl, gw["ple_w_gate"][i], 0)[0]
        dh2, dh2b, sg["norm_ple"][i], _ = rms_bwd(f"rms_ple_bwd_{i}", dn3, act["h2"], _row(small["norm_ple"], i), dh)
        wg["mlp_w2"][i] = mm_xt_dy(f"d_mlp_w2_{i}", act["hd"], dh2b, False)
        dz = mm_dy_wrow_t(f"dz_{i}", dh2b, gw["mlp_w2"][i], 0, extras=[(act["z"], "tile")], outs_dtypes=(BF16,),
                          epi=lambda acc, z_t: (acc * (2.0 * jnp.maximum(z_t, 0.0)),))[0]
        wg["mlp_w1"][i] = mm_xt_dy(f"d_mlp_w1_{i}", act["u2"], dz, True)
        du2 = mm_dy_wcol_t(f"du2_{i}", dz, gw["mlp_w1"][i], 0)[0]
        dh1, dh1b, sg["norm_mlp"][i], cs1 = rms_bwd(f"rms_mlp_bwd_{i}", du2, act["h1"], _row(small["norm_mlp"], i), dh2)
        if i % 2 == 0:
            wg["cf_w_pw2"][j] = mm_xt_dy(f"d_cf_pw2_{i}", act["v3"], dh1b, False)
            sg["cf_b_pw2"][j] = cs1
            dv3 = mm_dy_wrow_t(f"dv3_{i}", dh1b, gw["cf_w_pw2"][j], 0)[0]
            dv1, sg["cf_norm"][j], sg["cf_b_dw"][j] = cf_bwd_rows(f"cf_bwd_rows_{i}", dv3, act["v1"], _row(small["cf_norm"], j))
            da, sg["cf_w_dw"][j], sg["cf_b_pw1"][j] = cf_bwd_conv(f"cf_bwd_conv_{i}", dv1, act["v0"], act["a"], conv_w["cf"][j])
            wg["cf_w_pw1"][j] = mm_xt_dy(f"d_cf_pw1_{i}", act["u"], da, True)
            du = mm_dy_wcol_t(f"du_{i}", da, gw["cf_w_pw1"][j], 0)[0]
        else:
            wg["sc_w_out"][j] = mm_xt_dy(f"d_sc_out_{i}", act["y"], dh1b, False)
            dy = mm_dy_wrow_t(f"dy_{i}", dh1b, gw["sc_w_out"][j], 0)[0]
            dbcv, sg["sc_w_conv"][j] = sc_bwd_mid(f"sc_bwd_mid_{i}", dy, act["bcv"], conv_w["sc"][j])
            wg["sc_w_in"][j] = mm_xt_dy(f"d_sc_in_{i}", act["u"], dbcv, True, tn=768)
            du = mm_dy_wcol_t(f"du_{i}", dbcv, gw["sc_w_in"][j], 0)[0]
        dh, _, sg["norm_mix"][i], _ = rms_bwd(f"rms_mix_bwd_{i}", du, act["h"], _row(small["norm_mix"], i), dh1)
    return loss_part, dh, wg, sg


def _me_and_peers():
    x, y, c = lax.axis_index("x"), lax.axis_index("y"), lax.axis_index("c")
    me = 4 * x + 2 * y + c
    peers = []
    for q in range(1, NDEV):
        px = 1 - x if q & 4 else x
        py = 1 - y if q & 2 else y
        pc = 1 - c if q & 1 else c
        peers.append(((px, py, pc), 4 * px + 2 * py + pc))
    return me, peers


def _exchange(name, srcs, out_shapes, src_fns, dst_fns):
    n = len(srcs)

    def body(*refs):
        ins, outs = refs[:n], refs[n:2 * n]
        send_sems, recv_sems, local_sems = refs[2 * n:]
        me, peers = _me_and_peers()
        local, remote = [], []
        for k in range(n):
            cp = pltpu.make_async_copy(src_fns[k](ins[k], me), dst_fns[k](outs[k], me), local_sems.at[k])
            cp.start()
            local.append(cp)
        for q, (peer, peer_blk) in enumerate(peers):
            for k in range(n):
                cp = pltpu.make_async_remote_copy(
                    src_ref=src_fns[k](ins[k], peer_blk), dst_ref=dst_fns[k](outs[k], me),
                    send_sem=send_sems.at[k, q], recv_sem=recv_sems.at[k, q],
                    device_id=peer, device_id_type=MESH)
                cp.start()
                remote.append(cp)
        for q, (peer, peer_blk) in enumerate(peers):
            for k in range(n):
                pltpu.make_async_remote_copy(
                    src_ref=src_fns[k](ins[k], peer_blk), dst_ref=dst_fns[k](outs[k], peer_blk),
                    send_sem=send_sems.at[k, q], recv_sem=recv_sems.at[k, q],
                    device_id=peer, device_id_type=MESH).wait_recv()
        for cp in remote:
            cp.wait_send()
        for cp in local:
            cp.wait()

    any_spec = pl.BlockSpec(memory_space=pl.ANY)
    return pl.pallas_call(
        body,
        name=name,
        in_specs=[any_spec] * n,
        out_specs=[any_spec] * n,
        out_shape=out_shapes,
        scratch_shapes=[pltpu.SemaphoreType.DMA((n, N_PEERS)), pltpu.SemaphoreType.DMA((n, N_PEERS)),
                        pltpu.SemaphoreType.DMA((n,))],
    )(*srcs)


def all_gather(name, items):
    outs = [jax.ShapeDtypeStruct((NDEV, 1) + a.shape[1:], a.dtype) for a, _ in items]
    src_fns = [functools.partial(lambda ref, blk, layer: ref.at[layer], layer=l) for _, l in items]
    dst_fns = [lambda ref, blk: ref.at[blk, 0]] * len(items)
    return _exchange(name, [a for a, _ in items], outs, src_fns, dst_fns)


def grad_exchange(name, grads):
    outs = [jax.ShapeDtypeStruct(g.shape, g.dtype) for g in grads]
    fns = [lambda ref, blk: ref.at[blk]] * len(grads)
    return _exchange(name, list(grads), outs, fns, fns)


def cast_bf16(name, w, tr_elems=512 * 1024):
    l, r, c = w.shape
    tr = _row_tile(r, tr_elems // c)
    spec = pl.BlockSpec((None, tr, c), lambda li, i: (li, i, 0))

    def body(w_ref, o_ref):
        o_ref[...] = w_ref[...].astype(BF16)

    return pl.pallas_call(
        body, name=name, grid=(l, r // tr), in_specs=[spec], out_specs=spec,
        out_shape=jax.ShapeDtypeStruct(w.shape, BF16),
        compiler_params=_params(("parallel", "parallel"), 6 * tr * c),
    )(w)


def _adamw_math(w, g, m, v):
    m = ADAM_B1 * m + (1.0 - ADAM_B1) * g
    v = ADAM_B2 * v + (1.0 - ADAM_B2) * (g * g)
    m_hat = m / (1.0 - ADAM_B1 ** ADAM_STEP)
    v_hat = v / (1.0 - ADAM_B2 ** ADAM_STEP)
    delta = -ADAM_LR * (m_hat / (jnp.sqrt(v_hat) + ADAM_EPS) + ADAM_WD * w)
    return delta, m, v


def _sum_blocks(ref):
    g = ref[0].astype(F32)
    for d in range(1, ref.shape[0]):
        g = g + ref[d].astype(F32)
    return g


def adamw_layer(name, recv, w, m, v, layer, tr_elems=128 * 1024):
    nd, r, c = recv.shape
    tr = _row_tile(r, tr_elems // c)
    r_spec = pl.BlockSpec((nd, tr, c), lambda i: (0, i, 0))
    w_spec = pl.BlockSpec((None, tr, c), lambda i: (layer, i, 0))
    o_spec = pl.BlockSpec((tr, c), lambda i: (i, 0))

    def body(r_ref, w_ref, m_ref, v_ref, g_out, d_out, m_out, v_out):
        g = _sum_blocks(r_ref)
        delta, m_new, v_new = _adamw_math(w_ref[...], g, m_ref[...], v_ref[...])
        g_out[...] = g
        d_out[...] = delta
        m_out[...] = m_new
        v_out[...] = v_new

    out = jax.ShapeDtypeStruct((r, c), F32)
    return pl.pallas_call(
        body, name=name, grid=(r // tr,), in_specs=[r_spec, w_spec, w_spec, w_spec], out_specs=[o_spec] * 4,
        out_shape=[out] * 4,
        compiler_params=_params(("parallel",), tr * c * (2 * nd + 7 * 4)),
    )(recv, w, m, v)


def small_update(name, part_g, tap_g, w_a, m_a, v_a, w_b, m_b, v_b):
    nd, rows, d = part_g.shape
    na = rows // SUBLANES
    nb, cb = w_b.shape

    def body(pg_ref, tg_ref, wa_ref, ma_ref, va_ref, wb_ref, mb_ref, vb_ref,
             ga_out, da_out, ma_out, va_out, gb_out, db_out, mb_out, vb_out, loss_out):
        ga = jnp.sum(_sum_blocks(pg_ref).reshape(na, SUBLANES, d), axis=1)
        delta, m_new, v_new = _adamw_math(wa_ref[...], ga, ma_ref[...], va_ref[...])
        ga_out[...] = ga
        da_out[...] = delta
        ma_out[...] = m_new
        va_out[...] = v_new
        loss_out[...] = jnp.broadcast_to(jnp.sum(ga[na - 1:na, :], axis=1, keepdims=True), loss_out.shape)
        gb = _sum_blocks(tg_ref)
        delta, m_new, v_new = _adamw_math(wb_ref[...], gb, mb_ref[...], vb_ref[...])
        gb_out[...] = gb
        db_out[...] = delta
        mb_out[...] = m_new
        vb_out[...] = v_new

    oa, ob = jax.ShapeDtypeStruct((na, d), F32), jax.ShapeDtypeStruct((nb, cb), F32)
    vm = pl.BlockSpec(memory_space=pltpu.VMEM)
    return pl.pallas_call(
        body, name=name, in_specs=[vm] * 8, out_specs=[vm] * 9,
        out_shape=[oa] * 4 + [ob] * 4 + [jax.ShapeDtypeStruct((1, LANES), F32)],
        compiler_params=pltpu.CompilerParams(vmem_limit_bytes=_vmem_limit(_nbytes(part_g.shape, F32))),
    )(part_g, tap_g, w_a, m_a, v_a, w_b, m_b, v_b)


BIG = ("cf_w_pw1", "cf_w_pw2", "sc_w_in", "sc_w_out", "mlp_w1", "mlp_w2", "ple_w_proj", "ple_w_gate")
WEIGHT_ORDER = ("norm_mix", "norm_mlp", "norm_ple", "cf_w_pw1", "cf_b_pw1", "cf_w_dw", "cf_b_dw", "cf_norm",
                "cf_w_pw2", "cf_b_pw2", "sc_w_in", "sc_w_conv", "sc_w_out", "mlp_w1", "mlp_w2", "ple_w_proj",
                "ple_w_gate", "norm_final")
SMALL_ROWS = (("norm_mix", 4), ("norm_mlp", 4), ("norm_ple", 4), ("cf_b_pw1", 4), ("cf_b_dw", 2), ("cf_norm", 2),
              ("cf_b_pw2", 2), ("norm_final", 1))


def _layer_weights(i):
    mixer = (("cf_w_pw1", i // 2), ("cf_w_pw2", i // 2)) if i % 2 == 0 else (("sc_w_in", i // 2), ("sc_w_out", i // 2))
    return mixer + (("mlp_w1", i), ("mlp_w2", i), ("ple_w_proj", i), ("ple_w_gate", i))


def _pad_rows(a, rows):
    return jnp.pad(a, ((0, 0), (0, rows - a.shape[1]), (0, 0)))


def _pack_taps(cf, sc):
    c = cf.shape[2]
    return jnp.concatenate([_pad_rows(cf, CONV_A_HALO).reshape(-1, c), _pad_rows(sc, CONV_B_HALO).reshape(-1, c)], axis=0)


def _unpack_taps(t, n_cf):
    c = t.shape[1]
    cf = t[:n_cf * CONV_A_HALO].reshape(n_cf, CONV_A_HALO, c)[:, :CONV_A_TAPS]
    sc = t[n_cf * CONV_A_HALO:].reshape(-1, CONV_B_HALO, c)[:, :CONV_B_TAPS]
    return cf, sc


def _pack_small(vals, d):
    return jnp.concatenate([vals[k].reshape(-1, d) for k, _ in SMALL_ROWS] + [jnp.zeros((1, d), F32)], axis=0)


def _unpack_small(a, shapes):
    out, r = {}, 0
    for k, n in SMALL_ROWS:
        out[k] = a[r:r + n].reshape(shapes[k])
        r += n
    return out


def kernel(x, p, norm_mix, norm_mlp, norm_ple, cf_w_pw1, cf_b_pw1, cf_w_dw, cf_b_dw, cf_norm, cf_w_pw2, cf_b_pw2, sc_w_in, sc_w_conv, sc_w_out, mlp_w1, mlp_w2, ple_w_proj, ple_w_gate, norm_final, loss_target, m_norm_mix, m_norm_mlp, m_norm_ple, m_cf_w_pw1, m_cf_b_pw1, m_cf_w_dw, m_cf_b_dw, m_cf_norm, m_cf_w_pw2, m_cf_b_pw2, m_sc_w_in, m_sc_w_conv, m_sc_w_out, m_mlp_w1, m_mlp_w2, m_ple_w_proj, m_ple_w_gate, m_norm_final, v_norm_mix, v_norm_mlp, v_norm_ple, v_cf_w_pw1, v_cf_b_pw1, v_cf_w_dw, v_cf_b_dw, v_cf_norm, v_cf_w_pw2, v_cf_b_pw2, v_sc_w_in, v_sc_w_conv, v_sc_w_out, v_mlp_w1, v_mlp_w2, v_ple_w_proj, v_ple_w_gate, v_norm_final):
    w = dict(norm_mix=norm_mix, norm_mlp=norm_mlp, norm_ple=norm_ple, cf_w_pw1=cf_w_pw1, cf_b_pw1=cf_b_pw1,
             cf_w_dw=cf_w_dw, cf_b_dw=cf_b_dw, cf_norm=cf_norm, cf_w_pw2=cf_w_pw2, cf_b_pw2=cf_b_pw2,
             sc_w_in=sc_w_in, sc_w_conv=sc_w_conv, sc_w_out=sc_w_out, mlp_w1=mlp_w1, mlp_w2=mlp_w2,
             ple_w_proj=ple_w_proj, ple_w_gate=ple_w_gate, norm_final=norm_final)
    m = dict(norm_mix=m_norm_mix, norm_mlp=m_norm_mlp, norm_ple=m_norm_ple, cf_w_pw1=m_cf_w_pw1, cf_b_pw1=m_cf_b_pw1,
             cf_w_dw=m_cf_w_dw, cf_b_dw=m_cf_b_dw, cf_norm=m_cf_norm, cf_w_pw2=m_cf_w_pw2, cf_b_pw2=m_cf_b_pw2,
             sc_w_in=m_sc_w_in, sc_w_conv=m_sc_w_conv, sc_w_out=m_sc_w_out, mlp_w1=m_mlp_w1, mlp_w2=m_mlp_w2,
             ple_w_proj=m_ple_w_proj, ple_w_gate=m_ple_w_gate, norm_final=m_norm_final)
    v = dict(norm_mix=v_norm_mix, norm_mlp=v_norm_mlp, norm_ple=v_norm_ple, cf_w_pw1=v_cf_w_pw1, cf_b_pw1=v_cf_b_pw1,
             cf_w_dw=v_cf_w_dw, cf_b_dw=v_cf_b_dw, cf_norm=v_cf_norm, cf_w_pw2=v_cf_w_pw2, cf_b_pw2=v_cf_b_pw2,
             sc_w_in=v_sc_w_in, sc_w_conv=v_sc_w_conv, sc_w_out=v_sc_w_out, mlp_w1=v_mlp_w1, mlp_w2=v_mlp_w2,
             ple_w_proj=v_ple_w_proj, ple_w_gate=v_ple_w_gate, norm_final=v_norm_final)
    depth, d = norm_mix.shape
    n_cf = cf_w_dw.shape[0]
    me = 4 * lax.axis_index("x") + 2 * lax.axis_index("y") + lax.axis_index("c")

    taps_w = _pack_taps(cf_w_dw, sc_w_conv)
    taps_all = all_gather("ag_taps", [(taps_w[None], 0)])[0]
    taps_full = jnp.transpose(taps_all[:, 0], (1, 0, 2)).reshape(taps_w.shape[0], d)
    conv_w = {"cf": taps_full[:n_cf * CONV_A_HALO].reshape(n_cf, CONV_A_HALO, d),
              "sc": taps_full[n_cf * CONV_A_HALO:].reshape(-1, CONV_B_HALO, d)}

    shards = {k: cast_bf16(f"cast_{k}", w[k]) for k in BIG}
    gw = {k: [None] * w[k].shape[0] for k in BIG}
    for i in range(depth):
        names = _layer_weights(i)
        got = all_gather(f"ag_layer_{i}", [(shards[k], l) for k, l in names])
        for (k, l), g in zip(names, got):
            gw[k][l] = g

    small = {k: w[k] for k, _ in SMALL_ROWS}
    small["norm_final"] = norm_final[None]
    loss_part, grad_x, wg, sg = _local_step(x[0], p[:, 0], loss_target[0], small, gw, conv_w)

    res = {k: [None] * w[k].shape[0] for k in BIG}
    for i in reversed(range(depth)):
        names = _layer_weights(i)
        recv = grad_exchange(f"grad_exchange_{i}", [wg[k][l] for k, l in names])
        for (k, l), r in zip(names, recv):
            res[k][l] = adamw_layer(f"adamw_{k}_{l}", r, w[k], m[k], v[k], l)
    out = {k: [jnp.stack([res[k][l][t] for l in range(len(res[k]))]) for t in range(4)] for k in BIG}

    parts = []
    for k, n in SMALL_ROWS:
        for g in sg[k]:
            parts += [g[:, :d], g[:, d:]] if g.shape[1] == 2 * d else [g]
    parts.append(loss_part)
    part_pack = jnp.concatenate(parts, axis=0)
    tap_pack = jnp.concatenate(sg["cf_w_dw"] + sg["sc_w_conv"], axis=0)
    part_all, tap_all = all_gather("ag_small", [(part_pack[None], 0), (tap_pack[None], 0)])
    cb = cf_w_dw.shape[2]
    tap_mine = lax.dynamic_slice_in_dim(tap_all[:, 0], me * cb, cb, axis=2)
    sm = small_update("small_update", part_all[:, 0], tap_mine,
                      _pack_small(w, d), _pack_small(m, d), _pack_small(v, d),
                      taps_w, _pack_taps(m["cf_w_dw"], m["sc_w_conv"]), _pack_taps(v["cf_w_dw"], v["sc_w_conv"]))
    shapes = {k: w[k].shape for k, _ in SMALL_ROWS}
    for t in range(4):
        un = _unpack_small(sm[t], shapes)
        cf_t, sc_t = _unpack_taps(sm[4 + t], n_cf)
        for k in un:
            out.setdefault(k, [None] * 4)[t] = un[k]
        out.setdefault("cf_w_dw", [None] * 4)[t] = cf_t
        out.setdefault("sc_w_conv", [None] * 4)[t] = sc_t
    loss = sm[8][0, 0]

    return (loss, grad_x[None], *[out[k][0] for k in WEIGHT_ORDER], *[out[k][1] for k in WEIGHT_ORDER],
            *[out[k][2] for k in WEIGHT_ORDER], *[out[k][3] for k in WEIGHT_ORDER])
```

```python
import functools

import jax
import jax.numpy as jnp
from jax import lax
from jax.experimental import pallas as pl
from jax.experimental.pallas import tpu as pltpu

F32 = jnp.float32
BF16 = jnp.bfloat16
EPS = 1e-6
NDEV = 8
N_PEERS = NDEV - 1
MESH = pl.DeviceIdType.MESH

ADAM_LR = 0.001
ADAM_B1 = 0.9
ADAM_B2 = 0.999
ADAM_EPS = 1e-08
ADAM_WD = 0.01
ADAM_STEP = 10

V7X_VMEM_BYTES = 64 * 1024 * 1024
VMEM_LIMIT_MAX = 56 * 1024 * 1024
SUBLANES = 8
LANES = 128
CONV_A_TAPS = 31
CONV_A_HALO = 32
CONV_B_TAPS = 3
CONV_B_HALO = 8


def _nbytes(shape, dtype):
    n = 1
    for s in shape:
        if s is not None:
            n *= s
    return n * jnp.dtype(dtype).itemsize


def _vmem_limit(block_bytes, scratch_bytes=0):
    need = 2 * block_bytes + scratch_bytes
    return int(min(VMEM_LIMIT_MAX, max(32 * 1024 * 1024, need + need // 2 + (4 << 20))))


def _params(sem, block_bytes, scratch_bytes=0):
    return pltpu.CompilerParams(dimension_semantics=sem, vmem_limit_bytes=_vmem_limit(block_bytes, scratch_bytes))


_DIMS = {
    "nn": (((1,), (0,)), ((), ())),
    "nt": (((1,), (1,)), ((), ())),
    "tn": (((0,), (0,)), ((), ())),
}


def _mm(name, dims, grid, acc_shape, a, a_spec, b, b_spec, extras, outs, epi, token=None):
    ni, nj, nk = grid
    n_ex, n_out = len(extras), len(outs)
    dn = _DIMS[dims]
    tok = [] if token is None else [(token, pl.BlockSpec((SUBLANES, LANES), lambda i, j, k: (0, 0)))]
    n_tok = len(tok)

    def body(*refs):
        a_ref, b_ref = refs[0], refs[1]
        ex_refs = refs[2:2 + n_ex]
        refs = refs[:2 + n_ex] + refs[2 + n_ex + n_tok:]
        out_refs = refs[2 + n_ex:2 + n_ex + n_out]
        d = lax.dot_general(a_ref[...].astype(BF16), b_ref[...].astype(BF16), dn, preferred_element_type=F32)

        def finish(acc):
            res = epi(acc, *[r[...] for r in ex_refs])
            for o_ref, r in zip(out_refs, res):
                o_ref[...] = r.astype(o_ref.dtype)

        if nk == 1:
            finish(d)
        else:
            acc_ref = refs[2 + n_ex + n_out]
            k = pl.program_id(2)

            @pl.when(k == 0)
            def _():
                acc_ref[...] = d

            @pl.when(jnp.logical_and(k > 0, k < nk - 1))
            def _():
                acc_ref[...] += d

            @pl.when(k == nk - 1)
            def _():
                finish(acc_ref[...] + d)

    blk = _nbytes(a_spec.block_shape, a.dtype) + _nbytes(b_spec.block_shape, b.dtype)
    for arr, spec in list(extras) + list(outs):
        blk += _nbytes(spec.block_shape, arr.dtype)
    acc_bytes = _nbytes(acc_shape, F32)
    scratch = [pltpu.VMEM(acc_shape, F32)] if nk > 1 else []
    return pl.pallas_call(
        body,
        name=name,
        grid=grid,
        in_specs=[a_spec, b_spec] + [s for _, s in extras] + [s for _, s in tok],
        out_specs=[s for _, s in outs],
        out_shape=[o for o, _ in outs],
        scratch_shapes=scratch,
        compiler_params=_params(("parallel", "parallel", "arbitrary"), blk, 3 * acc_bytes),
    )(a, b, *[e for e, _ in extras], *[t for t, _ in tok])


def _tile(n, pref):
    if n <= pref:
        return n
    t = pref - pref % LANES
    while t > LANES and n % t:
        t -= LANES
    assert n % t == 0, (n, pref)
    return t


def _row_tile(n, pref):
    if n <= pref:
        return n
    t = max(SUBLANES, pref - pref % SUBLANES)
    while t > SUBLANES and n % t:
        t -= SUBLANES
    assert n % t == 0, (n, pref)
    return t


def _id_epi(acc):
    return (acc,)


def mm_x_wcol(name, x, w, layer, extras=(), outs_dtypes=(F32,), epi=_id_epi, tm=1024, tn=512):
    m, kdim = x.shape
    c = w.shape[3]
    n = NDEV * c
    tm, tn = _tile(m, tm), _tile(c, tn)
    tk = _tile(kdim, 2048)
    grid = (m // tm, n // tn, kdim // tk)
    per = c // tn
    a_spec = pl.BlockSpec((tm, tk), lambda i, j, k: (i, k))
    b_spec = pl.BlockSpec((None, None, tk, tn), lambda i, j, k: (j // per, layer, k, j % per))
    ex = [(e, _ex_spec(e, kind, tm, tn)) for e, kind in extras]
    o_spec = pl.BlockSpec((tm, tn), lambda i, j, k: (i, j))
    outs = [(jax.ShapeDtypeStruct((m, n), dt), o_spec) for dt in outs_dtypes]
    return _mm(name, "nn", grid, (tm, tn), x, a_spec, w, b_spec, ex, outs, epi)


def mm_x_wrow(name, x, w, extras=(), outs_dtypes=(F32,), epi=_id_epi, tm=1024, tn=512):
    m, kdim = x.shape
    n = w.shape[1]
    assert kdim == w.shape[0]
    tm, tn = _tile(m, tm), _tile(n, tn)
    tk = _tile(kdim, 2048)
    grid = (m // tm, n // tn, kdim // tk)
    a_spec = pl.BlockSpec((tm, tk), lambda i, j, k: (i, k))
    b_spec = pl.BlockSpec((tk, tn), lambda i, j, k: (k, j))
    ex = [(e, _ex_spec(e, kind, tm, tn)) for e, kind in extras]
    o_spec = pl.BlockSpec((tm, tn), lambda i, j, k: (i, j))
    outs = [(jax.ShapeDtypeStruct((m, n), dt), o_spec) for dt in outs_dtypes]
    return _mm(name, "nn", grid, (tm, tn), x, a_spec, w, b_spec, ex, outs, epi)


def mm_dy_wcol_t(name, dy, w, layer, extras=(), outs_dtypes=(F32,), epi=_id_epi, tm=1024, tn=1024):
    m, n = dy.shape
    kdim, c = w.shape[2], w.shape[3]
    assert n == NDEV * c
    tm, tn = _tile(m, tm), _tile(kdim, tn)
    tk = _tile(c, 2048)
    per = c // tk
    grid = (m // tm, kdim // tn, n // tk)
    a_spec = pl.BlockSpec((tm, tk), lambda i, j, k: (i, k))
    b_spec = pl.BlockSpec((None, None, tn, tk), lambda i, j, k: (k // per, layer, j, k % per))
    ex = [(e, _ex_spec(e, kind, tm, tn)) for e, kind in extras]
    o_spec = pl.BlockSpec((tm, tn), lambda i, j, k: (i, j))
    outs = [(jax.ShapeDtypeStruct((m, kdim), dt), o_spec) for dt in outs_dtypes]
    return _mm(name, "nt", grid, (tm, tn), dy, a_spec, w, b_spec, ex, outs, epi)


def mm_dy_wrow_t(name, dy, w, extras=(), outs_dtypes=(F32,), epi=_id_epi, tm=1024, tn=512):
    m, n = dy.shape
    kdim = w.shape[0]
    assert n == w.shape[1]
    tm, tn = _tile(m, tm), _tile(kdim, tn)
    tk = _tile(n, 2048)
    grid = (m // tm, kdim // tn, n // tk)
    a_spec = pl.BlockSpec((tm, tk), lambda i, j, k: (i, k))
    b_spec = pl.BlockSpec((tn, tk), lambda i, j, k: (j, k))
    ex = [(e, _ex_spec(e, kind, tm, tn)) for e, kind in extras]
    o_spec = pl.BlockSpec((tm, tn), lambda i, j, k: (i, j))
    outs = [(jax.ShapeDtypeStruct((m, kdim), dt), o_spec) for dt in outs_dtypes]
    return _mm(name, "nt", grid, (tm, tn), dy, a_spec, w, b_spec, ex, outs, epi)


def mm_xt_dy(name, x, dy, col_shards, tm=1024, tn=512, token=None):
    m, kdim = x.shape
    n = dy.shape[1]
    tk = _tile(m, 2048)
    if col_shards:
        c = n // NDEV
        tm, tn = _tile(kdim, tm), _tile(c, tn)
        per = c // tn
        out = jax.ShapeDtypeStruct((NDEV, kdim, c), BF16)
        o_spec = pl.BlockSpec((None, tm, tn), lambda i, j, k: (j // per, i, j % per))
    else:
        tm, tn = _tile(kdim, tm), _tile(n, tn)
        out = jax.ShapeDtypeStruct((kdim, n), BF16)
        o_spec = pl.BlockSpec((tm, tn), lambda i, j, k: (i, j))
    grid = (kdim // tm, n // tn, m // tk)
    a_spec = pl.BlockSpec((tk, tm), lambda i, j, k: (k, i))
    b_spec = pl.BlockSpec((tk, tn), lambda i, j, k: (k, j))
    g = _mm(name, "tn", grid, (tm, tn), x, a_spec, dy, b_spec, [], [(out, o_spec)], _id_epi, token=token)[0]
    return g if col_shards else g.reshape(NDEV, kdim // NDEV, n)


def _ex_spec(e, kind, tm, tn):
    if kind == "tile":
        return pl.BlockSpec((tm, tn), lambda i, j, k: (i, j))
    if kind == "row":
        return pl.BlockSpec((1, tn), lambda i, j, k: (0, j))
    raise ValueError(kind)


def _rows_call(name, body, n_rows, ts, ins, outs, scratch=(), scratch_bytes=0):
    blk = sum(_nbytes(s.block_shape, a.dtype) for a, s in list(ins) + list(outs))
    return pl.pallas_call(
        body,
        name=name,
        grid=(n_rows // ts,),
        in_specs=[s for _, s in ins],
        out_specs=[s for _, s in outs],
        out_shape=[o for o, _ in outs],
        scratch_shapes=list(scratch),
        compiler_params=_params(("arbitrary",), blk, scratch_bytes + 4 * blk // 2),
    )(*[a for a, _ in ins])


def _blk(ts, d):
    return pl.BlockSpec((ts, d), lambda i: (i, 0))


def _full(shape):
    return pl.BlockSpec(shape, lambda i: tuple(0 for _ in shape))


def _rowsum8(v):
    t, d = v.shape
    return jnp.sum(v.reshape(t // SUBLANES, SUBLANES, d), axis=0)


def _accumulate(ref, val):
    @pl.when(pl.program_id(0) == 0)
    def _():
        ref[...] = val

    @pl.when(pl.program_id(0) > 0)
    def _():
        ref[...] += val


def _rstd(x):
    return lax.rsqrt(jnp.mean(x * x, axis=-1, keepdims=True) + EPS)


def _rms_bwd_math(dy, x, g):
    r = _rstd(x)
    gdy = dy * g
    c = jnp.mean(gdy * x, axis=-1, keepdims=True)
    dx = r * gdy - x * (r * r * r * c)
    return dx, dy * (x * r)


def rms_fwd(name, h, g, ts=512):
    s, d = h.shape
    ts = min(ts, s)

    def body(h_ref, g_ref, u_ref):
        x = h_ref[...]
        u_ref[...] = ((x * _rstd(x)) * g_ref[...]).astype(BF16)

    return _rows_call(name, body, s, ts, [(h, _blk(ts, d)), (g, _full((1, d)))],
                      [(jax.ShapeDtypeStruct((s, d), BF16), _blk(ts, d))])[0]


def rms_bwd(name, du, h, g, dres, ts=256):
    s, d = h.shape
    ts = min(ts, s)

    def body(du_ref, h_ref, g_ref, dres_ref, dh_ref, dhb_ref, dg_ref, cs_ref):
        dx, dg = _rms_bwd_math(du_ref[...], h_ref[...], g_ref[...])
        dh = dres_ref[...] + dx
        dh_ref[...] = dh
        dhb_ref[...] = dh.astype(BF16)
        _accumulate(dg_ref, _rowsum8(dg))
        _accumulate(cs_ref, _rowsum8(dh))

    return _rows_call(
        name, body, s, ts,
        [(du, _blk(ts, d)), (h, _blk(ts, d)), (g, _full((1, d))), (dres, _blk(ts, d))],
        [(jax.ShapeDtypeStruct((s, d), F32), _blk(ts, d)), (jax.ShapeDtypeStruct((s, d), BF16), _blk(ts, d)),
         (jax.ShapeDtypeStruct((SUBLANES, d), F32), _full((SUBLANES, d))),
         (jax.ShapeDtypeStruct((SUBLANES, d), F32), _full((SUBLANES, d)))])


def final_loss(name, h, g, target, ts=256):
    s, d = h.shape
    ts = min(ts, s)

    def body(h_ref, g_ref, t_ref, loss_ref, dh_ref, dg_ref):
        x = h_ref[...]
        gf = g_ref[...]
        y = (x * _rstd(x)) * gf
        err = y - t_ref[...]
        _accumulate(loss_ref, _rowsum8(err * err) * (0.5 / d))
        dx, dg = _rms_bwd_math(err * (1.0 / d), x, gf)
        dh_ref[...] = dx
        _accumulate(dg_ref, _rowsum8(dg))

    return _rows_call(
        name, body, s, ts,
        [(h, _blk(ts, d)), (g, _full((1, d))), (target, _blk(ts, d))],
        [(jax.ShapeDtypeStruct((SUBLANES, d), F32), _full((SUBLANES, d))),
         (jax.ShapeDtypeStruct((s, d), F32), _blk(ts, d)),
         (jax.ShapeDtypeStruct((SUBLANES, d), F32), _full((SUBLANES, d)))])


def ple_bwd_elem(name, dh, g, e, ts=512):
    s, d = dh.shape
    ts = min(ts, s)

    def body(dh_ref, g_ref, e_ref, de_ref, dgl_ref):
        dh_v, g_v = dh_ref[...], g_ref[...]
        de_ref[...] = (dh_v * g_v).astype(BF16)
        dgl_ref[...] = (dh_v * e_ref[...] * (g_v * (1.0 - g_v))).astype(BF16)

    return _rows_call(name, body, s, ts, [(dh, _blk(ts, d)), (g, _blk(ts, d)), (e, _blk(ts, d))],
                      [(jax.ShapeDtypeStruct((s, d), BF16), _blk(ts, d)),
                       (jax.ShapeDtypeStruct((s, d), BF16), _blk(ts, d))])


def _chunks(n_rows, d, fn):
    lc = min(256, d)
    rc = min(64, n_rows)

    def lane_body(c, carry):
        lanes = pl.ds(pl.multiple_of(c * lc, lc), lc)
        for r in range(n_rows // rc):
            fn(r * rc, rc, lanes)
        return carry

    lax.fori_loop(0, d // lc, lane_body, 0)


def _prev_halo_spec(ts, halo, width):
    per = ts // halo
    return pl.BlockSpec((halo, width), lambda i: (jnp.maximum(i * per - 1, 0), 0))


def _next_halo_spec(ts, halo, width, n_rows):
    per = ts // halo
    last = n_rows // halo - 1
    return pl.BlockSpec((halo, width), lambda i: (jnp.minimum((i + 1) * per, last), 0))


def cf_fwd_mid(name, a, w_dw, b_dw, gn, ts=256):
    s, d2 = a.shape
    d = d2 // 2
    ts = min(ts, s)
    hl = CONV_A_HALO
    off = hl - (CONV_A_TAPS - 1)

    def body(a_ref, ah_ref, w_ref, b_ref, gn_ref, v0_ref, v1_ref, v3_ref, buf):
        first = pl.program_id(0) == 0
        halo = ah_ref[...]
        hv0 = halo[:, :d] * jax.nn.sigmoid(halo[:, d:])
        buf[pl.ds(0, hl), :] = jnp.where(first, 0.0, hv0)
        main = a_ref[...]
        v0 = main[:, :d] * jax.nn.sigmoid(main[:, d:])
        buf[pl.ds(hl, ts), :] = v0
        v0_ref[...] = v0

        def conv(r0, rc, lanes):
            acc = jnp.zeros((rc, lanes.size), F32)
            for k in range(CONV_A_TAPS):
                acc = acc + w_ref[pl.ds(k, 1), lanes] * buf[pl.ds(r0 + off + k, rc), lanes]
            v1_ref[pl.ds(r0, rc), lanes] = acc + b_ref[:, lanes]

        _chunks(ts, d, conv)
        v1 = v1_ref[...]
        v2 = (v1 * _rstd(v1)) * gn_ref[...]
        v3_ref[...] = (v2 * jax.nn.sigmoid(v2)).astype(BF16)

    return _rows_call(
        name, body, s, ts,
        [(a, _blk(ts, d2)), (a, _prev_halo_spec(ts, hl, d2)), (w_dw, _full(w_dw.shape)),
         (b_dw, _full((1, d))), (gn, _full((1, d)))],
        [(jax.ShapeDtypeStruct((s, d), F32), _blk(ts, d)), (jax.ShapeDtypeStruct((s, d), F32), _blk(ts, d)),
         (jax.ShapeDtypeStruct((s, d), BF16), _blk(ts, d))],
        scratch=[pltpu.VMEM((hl + ts, d), F32)], scratch_bytes=_nbytes((hl + ts, d), F32))


def cf_bwd_rows(name, dv3, v1, gn, ts=256):
    s, d = v1.shape
    ts = min(ts, s)

    def body(dv3_ref, v1_ref, gn_ref, dv1_ref, dgn_ref, db_ref):
        v1 = v1_ref[...]
        gn_v = gn_ref[...]
        v2 = (v1 * _rstd(v1)) * gn_v
        sg = jax.nn.sigmoid(v2)
        dv2 = dv3_ref[...] * (sg * (1.0 + v2 * (1.0 - sg)))
        dv1, dgn = _rms_bwd_math(dv2, v1, gn_v)
        dv1_ref[...] = dv1
        _accumulate(dgn_ref, _rowsum8(dgn))
        _accumulate(db_ref, _rowsum8(dv1))

    return _rows_call(
        name, body, s, ts, [(dv3, _blk(ts, d)), (v1, _blk(ts, d)), (gn, _full((1, d)))],
        [(jax.ShapeDtypeStruct((s, d), F32), _blk(ts, d)),
         (jax.ShapeDtypeStruct((SUBLANES, d), F32), _full((SUBLANES, d))),
         (jax.ShapeDtypeStruct((SUBLANES, d), F32), _full((SUBLANES, d)))])


def cf_bwd_conv(name, dv1, v0, a, w_dw, ts=256):
    s, d = dv1.shape
    ts = min(ts, s)
    hl = CONV_A_HALO
    taps = CONV_A_TAPS
    off = hl - (taps - 1)
    last_blk = s // ts - 1

    def body(dv1_ref, dv1n_ref, v0_ref, v0p_ref, a_ref, w_ref, da_ref, dw_ref, db_ref, dbuf, vbuf, dv0_buf, dw_acc):
        i = pl.program_id(0)
        dbuf[pl.ds(0, ts), :] = dv1_ref[...]
        dbuf[pl.ds(ts, hl), :] = jnp.where(i == last_blk, 0.0, dv1n_ref[...])
        vbuf[pl.ds(0, hl), :] = jnp.where(i == 0, 0.0, v0p_ref[...])
        vbuf[pl.ds(hl, ts), :] = v0_ref[...]

        @pl.when(i == 0)
        def _():
            dw_acc[...] = jnp.zeros_like(dw_acc)

        def conv_t(r0, rc, lanes):
            g = dbuf[pl.ds(r0, rc), lanes]
            acc = jnp.zeros((rc, lanes.size), F32)
            for k in range(taps):
                acc = acc + w_ref[pl.ds(k, 1), lanes] * dbuf[pl.ds(r0 + taps - 1 - k, rc), lanes]
                prod = g * vbuf[pl.ds(r0 + off + k, rc), lanes]
                dw_acc[pl.ds(k * SUBLANES, SUBLANES), lanes] += _rowsum8(prod)
            dv0_buf[pl.ds(r0, rc), lanes] = acc

        _chunks(ts, d, conv_t)
        dv0 = dv0_buf[...]
        av = a_ref[...]
        val, sg = av[:, :d], jax.nn.sigmoid(av[:, d:])
        dval = dv0 * sg
        dgate = dv0 * val * (sg * (1.0 - sg))
        da_ref[:, :d] = dval.astype(BF16)
        da_ref[:, d:] = dgate.astype(BF16)
        _accumulate(db_ref.at[:, pl.ds(0, d)], _rowsum8(dval))
        _accumulate(db_ref.at[:, pl.ds(d, d)], _rowsum8(dgate))

        @pl.when(i == last_blk)
        def _():
            dw_ref[...] = jnp.sum(dw_acc[...].reshape(hl, SUBLANES, d), axis=1)

    scratch = [pltpu.VMEM((ts + hl, d), F32), pltpu.VMEM((hl + ts, d), F32), pltpu.VMEM((ts, d), F32),
               pltpu.VMEM((hl * SUBLANES, d), F32)]
    sbytes = _nbytes((3 * ts + 2 * hl + hl * SUBLANES, d), F32)
    return _rows_call(
        name, body, s, ts,
        [(dv1, _blk(ts, d)), (dv1, _next_halo_spec(ts, hl, d, s)), (v0, _blk(ts, d)), (v0, _prev_halo_spec(ts, hl, d)),
         (a, _blk(ts, 2 * d)), (w_dw, _full(w_dw.shape))],
        [(jax.ShapeDtypeStruct((s, 2 * d), BF16), _blk(ts, 2 * d)),
         (jax.ShapeDtypeStruct((hl, d), F32), _full((hl, d))),
         (jax.ShapeDtypeStruct((SUBLANES, 2 * d), F32), _full((SUBLANES, 2 * d)))],
        scratch=scratch, scratch_bytes=sbytes)


def sc_fwd_mid(name, bcv, w_conv, ts=256):
    s, d3 = bcv.shape
    d = d3 // 3
    ts = min(ts, s)
    hl = CONV_B_HALO
    off = hl - (CONV_B_TAPS - 1)

    def body(x_ref, xp_ref, w_ref, y_ref, buf):
        hp = xp_ref[...]
        buf[pl.ds(0, hl), :] = jnp.where(pl.program_id(0) == 0, 0.0, hp[:, d:2 * d] * hp[:, 2 * d:])
        buf[pl.ds(hl, ts), :] = x_ref[:, d:2 * d] * x_ref[:, 2 * d:]
        cc = jnp.zeros((ts, d), F32)
        for k in range(CONV_B_TAPS):
            cc = cc + w_ref[pl.ds(k, 1), :] * buf[pl.ds(off + k, ts), :]
        y_ref[...] = (x_ref[:, :d] * cc).astype(BF16)

    return _rows_call(
        name, body, s, ts,
        [(bcv, _blk(ts, d3)), (bcv, _prev_halo_spec(ts, hl, d3)), (w_conv, _full(w_conv.shape))],
        [(jax.ShapeDtypeStruct((s, d), BF16), _blk(ts, d))],
        scratch=[pltpu.VMEM((hl + ts, d), F32)], scratch_bytes=_nbytes((hl + ts, d), F32))[0]


def sc_bwd_mid(name, dy, bcv, w_conv, ts=256):
    s, d3 = bcv.shape
    d = d3 // 3
    ts = min(ts, s)
    hl = CONV_B_HALO
    taps = CONV_B_TAPS
    off = hl - (taps - 1)
    last_blk = s // ts - 1

    def body(dy_ref, dyn_ref, x_ref, xp_ref, xn_ref, w_ref, dx_ref, dw_ref, cvbuf, dbuf, dw_acc):
        i = pl.program_id(0)
        hp = xp_ref[...]
        cvbuf[pl.ds(0, hl), :] = jnp.where(i == 0, 0.0, hp[:, d:2 * d] * hp[:, 2 * d:])
        gb, gc, v = x_ref[:, :d], x_ref[:, d:2 * d], x_ref[:, 2 * d:]
        cvbuf[pl.ds(hl, ts), :] = gc * v
        dy_v = dy_ref[...]
        dcc = dy_v * gb
        dbuf[pl.ds(0, ts), :] = dcc
        dbuf[pl.ds(ts, hl), :] = jnp.where(i == last_blk, 0.0, dyn_ref[...] * xn_ref[:, :d])

        @pl.when(i == 0)
        def _():
            dw_acc[...] = jnp.zeros_like(dw_acc)

        cc = jnp.zeros((ts, d), F32)
        dcv = jnp.zeros((ts, d), F32)
        for k in range(taps):
            win = cvbuf[pl.ds(off + k, ts), :]
            cc = cc + w_ref[pl.ds(k, 1), :] * win
            dcv = dcv + w_ref[pl.ds(k, 1), :] * dbuf[pl.ds(taps - 1 - k, ts), :]
            dw_acc[pl.ds(k * SUBLANES, SUBLANES), :] += _rowsum8(dcc * win)
        dx_ref[:, :d] = (dy_v * cc).astype(BF16)
        dx_ref[:, d:2 * d] = (dcv * v).astype(BF16)
        dx_ref[:, 2 * d:] = (dcv * gc).astype(BF16)

        @pl.when(i == last_blk)
        def _():
            dw_ref[...] = jnp.sum(dw_acc[...].reshape(hl, SUBLANES, d), axis=1)

    scratch = [pltpu.VMEM((hl + ts, d), F32), pltpu.VMEM((ts + hl, d), F32), pltpu.VMEM((hl * SUBLANES, d), F32)]
    sbytes = _nbytes((2 * ts + 2 * hl + hl * SUBLANES, d), F32)
    return _rows_call(
        name, body, s, ts,
        [(dy, _blk(ts, d)), (dy, _next_halo_spec(ts, hl, d, s)), (bcv, _blk(ts, d3)), (bcv, _prev_halo_spec(ts, hl, d3)),
         (bcv, _next_halo_spec(ts, hl, d3, s)), (w_conv, _full(w_conv.shape))],
        [(jax.ShapeDtypeStruct((s, d3), BF16), _blk(ts, d3)), (jax.ShapeDtypeStruct((hl, d), F32), _full((hl, d)))],
        scratch=scratch, scratch_bytes=sbytes)


def _row(a, i):
    return lax.slice_in_dim(a, i, i + 1, axis=0)


def _local_step(x, p, target, small, get_w, conv_w, put_grads):
    depth = p.shape[0]
    acts = []
    h = x
    for i in range(depth):
        j = i // 2
        act = {"h": h}
        u = rms_fwd(f"rms_mix_{i}", h, _row(small["norm_mix"], i))
        act["u"] = u
        if i % 2 == 0:
            a = mm_x_wcol(f"cf_pw1_{i}", u, get_w("cf_w_pw1", j, u), 0, extras=[(_row(small["cf_b_pw1"], j), "row")],
                          epi=lambda acc, b: (acc + b,))[0]
            v0, v1, v3 = cf_fwd_mid(f"cf_mid_{i}", a, conv_w["cf"][j], _row(small["cf_b_dw"], j), _row(small["cf_norm"], j))
            act.update(a=a, v0=v0, v1=v1, v3=v3)
            h1 = mm_x_wrow(f"cf_pw2_{i}", v3, get_w("cf_w_pw2", j, v3),
                           extras=[(_row(small["cf_b_pw2"], j), "row"), (h, "tile")],
                           epi=lambda acc, b, res: (res + (acc + b),))[0]
        else:
            bcv = mm_x_wcol(f"sc_in_{i}", u, get_w("sc_w_in", j, u), 0, tn=768)[0]
            y = sc_fwd_mid(f"sc_mid_{i}", bcv, conv_w["sc"][j])
            act.update(bcv=bcv, y=y)
            h1 = mm_x_wrow(f"sc_out_{i}", y, get_w("sc_w_out", j, y), extras=[(h, "tile")],
                           epi=lambda acc, res: (res + acc,))[0]
        act["h1"] = h1
        u2 = rms_fwd(f"rms_mlp_{i}", h1, _row(small["norm_mlp"], i))
        z, hd = mm_x_wcol(f"mlp_w1_{i}", u2, get_w("mlp_w1", i, u2), 0, outs_dtypes=(F32, BF16),
                          epi=lambda acc: (acc, jnp.square(jnp.maximum(acc, 0.0))))
        h2 = mm_x_wrow(f"mlp_w2_{i}", hd, get_w("mlp_w2", i, hd), extras=[(h1, "tile")],
                       epi=lambda acc, res: (res + acc,), tn=1024)[0]
        act.update(u2=u2, z=z, hd=hd, h2=h2)
        n3 = rms_fwd(f"rms_ple_{i}", h2, _row(small["norm_ple"], i))
        e = mm_x_wcol(f"ple_proj_{i}", p[i], get_w("ple_w_proj", i, n3), 0)[0]

        def ple_epi(acc, e_t, res):
            g_t = jax.nn.sigmoid(acc)
            return g_t, res + g_t * e_t

        g, h3 = mm_x_wrow(f"ple_gate_{i}", n3, get_w("ple_w_gate", i, e), extras=[(e, "tile"), (h2, "tile")],
                          outs_dtypes=(F32, F32), epi=ple_epi)
        act.update(n3=n3, e=e, g=g)
        acts.append(act)
        h = h3

    loss_part, dh, dg_final = final_loss("final_loss", h, small["norm_final"], target)
    sg = {k: [None] * small[k].shape[0] for k in small if k != "norm_final"}
    sg["norm_final"] = [dg_final]
    sg["cf_w_dw"] = [None] * conv_w["cf"].shape[0]
    sg["sc_w_conv"] = [None] * conv_w["sc"].shape[0]

    token = None
    for i in reversed(range(depth)):
        j = i // 2
        act = acts[i]
        de, dgl = ple_bwd_elem(f"ple_bwd_{i}", dh, act["g"], act["e"])
        g_proj = mm_xt_dy(f"d_ple_proj_{i}", p[i], de, True, token=token)
        g_gate = mm_xt_dy(f"d_ple_gate_{i}", act["n3"], dgl, False)
        dn3 = mm_dy_wrow_t(f"dn3_{i}", dgl, get_w("ple_w_gate", i))[0]
        dh2, dh2b, sg["norm_ple"][i], _ = rms_bwd(f"rms_ple_bwd_{i}", dn3, act["h2"], _row(small["norm_ple"], i), dh)
        g_w2 = mm_xt_dy(f"d_mlp_w2_{i}", act["hd"], dh2b, False)
        dz = mm_dy_wrow_t(f"dz_{i}", dh2b, get_w("mlp_w2", i), extras=[(act["z"], "tile")], outs_dtypes=(BF16,),
                          epi=lambda acc, z_t: (acc * (2.0 * jnp.maximum(z_t, 0.0)),))[0]
        g_w1 = mm_xt_dy(f"d_mlp_w1_{i}", act["u2"], dz, True)
        token = put_grads({("ple_w_proj", i): g_proj, ("ple_w_gate", i): g_gate, ("mlp_w2", i): g_w2, ("mlp_w1", i): g_w1})
        du2 = mm_dy_wcol_t(f"du2_{i}", dz, get_w("mlp_w1", i), 0)[0]
        dh1, dh1b, sg["norm_mlp"][i], cs1 = rms_bwd(f"rms_mlp_bwd_{i}", du2, act["h1"], _row(small["norm_mlp"], i), dh2)
        if i % 2 == 0:
            g_out = mm_xt_dy(f"d_cf_pw2_{i}", act["v3"], dh1b, False, token=token)
            sg["cf_b_pw2"][j] = cs1
            dv3 = mm_dy_wrow_t(f"dv3_{i}", dh1b, get_w("cf_w_pw2", j))[0]
            dv1, sg["cf_norm"][j], sg["cf_b_dw"][j] = cf_bwd_rows(f"cf_bwd_rows_{i}", dv3, act["v1"], _row(small["cf_norm"], j))
            da, sg["cf_w_dw"][j], sg["cf_b_pw1"][j] = cf_bwd_conv(f"cf_bwd_conv_{i}", dv1, act["v0"], act["a"], conv_w["cf"][j])
            g_in = mm_xt_dy(f"d_cf_pw1_{i}", act["u"], da, True)
            token = put_grads({("cf_w_pw2", j): g_out, ("cf_w_pw1", j): g_in})
            du = mm_dy_wcol_t(f"du_{i}", da, get_w("cf_w_pw1", j), 0)[0]
        else:
            g_out = mm_xt_dy(f"d_sc_out_{i}", act["y"], dh1b, False, token=token)
            dy = mm_dy_wrow_t(f"dy_{i}", dh1b, get_w("sc_w_out", j))[0]
            dbcv, sg["sc_w_conv"][j] = sc_bwd_mid(f"sc_bwd_mid_{i}", dy, act["bcv"], conv_w["sc"][j])
            g_in = mm_xt_dy(f"d_sc_in_{i}", act["u"], dbcv, True, tn=768)
            token = put_grads({("sc_w_out", j): g_out, ("sc_w_in", j): g_in})
            du = mm_dy_wcol_t(f"du_{i}", dbcv, get_w("sc_w_in", j), 0)[0]
        dh, _, sg["norm_mix"][i], _ = rms_bwd(f"rms_mix_bwd_{i}", du, act["h"], _row(small["norm_mix"], i), dh1)
    return loss_part, dh, sg


def _me_and_peers():
    x, y, c = lax.axis_index("x"), lax.axis_index("y"), lax.axis_index("c")
    me = 4 * x + 2 * y + c
    peers = []
    for q in range(1, NDEV):
        px = 1 - x if q & 4 else x
        py = 1 - y if q & 2 else y
        pc = 1 - c if q & 1 else c
        peers.append(((px, py, pc), 4 * px + 2 * py + pc))
    return me, peers


def _exchange(name, srcs, out_shapes, src_fns, dst_fns):
    n = len(srcs)

    def body(*refs):
        ins, outs = refs[:n], refs[n:2 * n]
        send_sems, recv_sems, local_sems = refs[2 * n:]
        me, peers = _me_and_peers()
        local, remote = [], []
        for k in range(n):
            cp = pltpu.make_async_copy(src_fns[k](ins[k], me), dst_fns[k](outs[k], me), local_sems.at[k])
            cp.start()
            local.append(cp)
        for q, (peer, peer_blk) in enumerate(peers):
            for k in range(n):
                cp = pltpu.make_async_remote_copy(
                    src_ref=src_fns[k](ins[k], peer_blk), dst_ref=dst_fns[k](outs[k], me),
                    send_sem=send_sems.at[k, q], recv_sem=recv_sems.at[k, q],
                    device_id=peer, device_id_type=MESH)
                cp.start()
                remote.append(cp)
        for q, (peer, peer_blk) in enumerate(peers):
            for k in range(n):
                pltpu.make_async_remote_copy(
                    src_ref=src_fns[k](ins[k], peer_blk), dst_ref=dst_fns[k](outs[k], peer_blk),
                    send_sem=send_sems.at[k, q], recv_sem=recv_sems.at[k, q],
                    device_id=peer, device_id_type=MESH).wait_recv()
        for cp in remote:
            cp.wait_send()
        for cp in local:
            cp.wait()

    any_spec = pl.BlockSpec(memory_space=pl.ANY)
    return pl.pallas_call(
        body,
        name=name,
        in_specs=[any_spec] * n,
        out_specs=[any_spec] * n,
        out_shape=out_shapes,
        scratch_shapes=[pltpu.SemaphoreType.DMA((n, N_PEERS)), pltpu.SemaphoreType.DMA((n, N_PEERS)),
                        pltpu.SemaphoreType.DMA((n,))],
    )(*srcs)


def all_gather(name, items):
    outs = [jax.ShapeDtypeStruct((NDEV, 1) + a.shape[1:], a.dtype) for a, _ in items]
    src_fns = [functools.partial(lambda ref, blk, layer: ref.at[layer], layer=l) for _, l in items]
    dst_fns = [lambda ref, blk: ref.at[blk, 0]] * len(items)
    return _exchange(name, [a for a, _ in items], outs, src_fns, dst_fns)


HBM_SPEC = pl.BlockSpec(memory_space=pltpu.HBM)
SEM_SPEC = pl.BlockSpec(memory_space=pltpu.SEMAPHORE)
DATAFLOW_EFFECT = pltpu.SideEffectType.DATAFLOW_SIDE_EFFECTING


def _in_hbm(a):
    return pltpu.with_memory_space_constraint(a, pltpu.HBM)


def exchange_start(name, srcs, land_shapes, src_fns, dst_fns, after=None):
    n = len(srcs)
    n_after = 0 if after is None else 1

    def body(*refs):
        ins, lands = refs[:n], refs[n:2 * n]
        outs = refs[2 * n + n_after:]
        send, recv = outs[:n], outs[n:2 * n]
        token, local_sems = outs[4 * n], outs[4 * n + 1]
        me, peers = _me_and_peers()
        local = []
        for k in range(n):
            cp = pltpu.make_async_copy(src_fns[k](ins[k], me), dst_fns[k](lands[k], me), local_sems.at[k])
            cp.start()
            local.append(cp)
        for q, (peer, peer_blk) in enumerate(peers):
            for k in range(n):
                pltpu.make_async_remote_copy(
                    src_ref=src_fns[k](ins[k], peer_blk), dst_ref=dst_fns[k](lands[k], me),
                    send_sem=send[k].at[q], recv_sem=recv[k].at[q],
                    device_id=peer, device_id_type=MESH).start()
        for cp in local:
            cp.wait()
        token[...] = jnp.zeros_like(token)

    lands = [_in_hbm(lax.empty(s.shape, s.dtype)) for s in land_shapes]
    sems = [pltpu.SemaphoreType.DMA((N_PEERS,))] * (2 * n)
    thru = [pltpu.HBM(a.shape, a.dtype) for a in srcs] + [pltpu.HBM(s.shape, s.dtype) for s in land_shapes]
    extra_in = [] if after is None else [after]
    res = pl.pallas_call(
        body,
        name=name,
        in_specs=[HBM_SPEC] * (2 * n) + [pl.BlockSpec(memory_space=pl.ANY)] * n_after,
        out_specs=[SEM_SPEC] * (2 * n) + [HBM_SPEC] * (2 * n) + [pl.BlockSpec(memory_space=pltpu.VMEM)],
        out_shape=sems + thru + [jax.ShapeDtypeStruct((SUBLANES, LANES), F32)],
        scratch_shapes=[pltpu.SemaphoreType.DMA((n,))],
        input_output_aliases={**{k: 2 * n + k for k in range(n)}, **{n + k: 3 * n + k for k in range(n)}},
        compiler_params=pltpu.CompilerParams(has_side_effects=DATAFLOW_EFFECT),
    )(*[_in_hbm(a) for a in srcs], *lands, *extra_in)
    per_array = [(res[k], res[n + k], res[2 * n + k], res[3 * n + k]) for k in range(n)]
    return per_array, res[4 * n]


def exchange_wait(name, started, src_fn, dst_fn, after):
    send_sem, recv_sem, src, land = started

    def body(src_ref, land_ref, send_ref, recv_ref, after_ref, src_out, land_out):
        me, peers = _me_and_peers()
        for q, (peer, peer_blk) in enumerate(peers):
            cp = pltpu.make_async_remote_copy(
                src_ref=src_fn(src_ref, peer_blk), dst_ref=dst_fn(land_ref, peer_blk),
                send_sem=send_ref.at[q], recv_sem=recv_ref.at[q],
                device_id=peer, device_id_type=MESH)
            cp.wait_send()
            cp.wait_recv()

    return pl.pallas_call(
        body,
        name=name,
        in_specs=[HBM_SPEC, HBM_SPEC, SEM_SPEC, SEM_SPEC, pl.BlockSpec(memory_space=pl.ANY)],
        out_specs=[HBM_SPEC, HBM_SPEC],
        out_shape=[pltpu.HBM(src.shape, src.dtype), pltpu.HBM(land.shape, land.dtype)],
        input_output_aliases={0: 0, 1: 1},
        compiler_params=pltpu.CompilerParams(has_side_effects=DATAFLOW_EFFECT),
    )(src, land, send_sem, recv_sem, after)[1]


def _gather_src(layer):
    return lambda ref, blk: ref.at[layer]


def _gather_dst(ref, blk):
    return ref.at[blk, 0]


def _slice_of(ref, blk):
    return ref.at[blk]


def cast_bf16(name, w, tr_elems=512 * 1024):
    l, r, c = w.shape
    tr = _row_tile(r, tr_elems // c)
    spec = pl.BlockSpec((None, tr, c), lambda li, i: (li, i, 0))

    def body(w_ref, o_ref):
        o_ref[...] = w_ref[...].astype(BF16)

    return pl.pallas_call(
        body, name=name, grid=(l, r // tr), in_specs=[spec], out_specs=spec,
        out_shape=jax.ShapeDtypeStruct(w.shape, BF16),
        compiler_params=_params(("parallel", "parallel"), 6 * tr * c),
    )(w)


def _adamw_math(w, g, m, v):
    m = ADAM_B1 * m + (1.0 - ADAM_B1) * g
    v = ADAM_B2 * v + (1.0 - ADAM_B2) * (g * g)
    m_hat = m / (1.0 - ADAM_B1 ** ADAM_STEP)
    v_hat = v / (1.0 - ADAM_B2 ** ADAM_STEP)
    delta = -ADAM_LR * (m_hat / (jnp.sqrt(v_hat) + ADAM_EPS) + ADAM_WD * w)
    return delta, m, v


def _sum_blocks(ref):
    g = ref[0].astype(F32)
    for d in range(1, ref.shape[0]):
        g = g + ref[d].astype(F32)
    return g


def adamw_layer(name, recv, w, m, v, layer, stacked, tr_elems=128 * 1024):
    nd, r, c = recv.shape
    tr = _row_tile(r, tr_elems // c)
    r_spec = pl.BlockSpec((nd, tr, c), lambda i: (0, i, 0))
    w_spec = pl.BlockSpec((None, tr, c), lambda i: (layer, i, 0))
    if stacked is None:
        stacked = [lax.empty(w.shape, F32) for _ in range(4)]

    def body(r_ref, w_ref, m_ref, v_ref, g_in, d_in, m_in, v_in, g_out, d_out, m_out, v_out):
        g = _sum_blocks(r_ref)
        delta, m_new, v_new = _adamw_math(w_ref[...], g, m_ref[...], v_ref[...])
        g_out[...] = g
        d_out[...] = delta
        m_out[...] = m_new
        v_out[...] = v_new

    out = jax.ShapeDtypeStruct(w.shape, F32)
    return pl.pallas_call(
        body, name=name, grid=(r // tr,),
        in_specs=[r_spec, w_spec, w_spec, w_spec] + [pl.BlockSpec(memory_space=pl.ANY)] * 4,
        out_specs=[w_spec] * 4, out_shape=[out] * 4,
        input_output_aliases={4: 0, 5: 1, 6: 2, 7: 3},
        compiler_params=_params(("parallel",), tr * c * (2 * nd + 7 * 4)),
    )(recv, w, m, v, *stacked)


def small_update(name, part_g, tap_g, w_a, m_a, v_a, w_b, m_b, v_b):
    nd, rows, d = part_g.shape
    na = rows // SUBLANES
    nb, cb = w_b.shape

    def body(pg_ref, tg_ref, wa_ref, ma_ref, va_ref, wb_ref, mb_ref, vb_ref,
             ga_out, da_out, ma_out, va_out, gb_out, db_out, mb_out, vb_out, loss_out):
        ga = jnp.sum(_sum_blocks(pg_ref).reshape(na, SUBLANES, d), axis=1)
        delta, m_new, v_new = _adamw_math(wa_ref[...], ga, ma_ref[...], va_ref[...])
        ga_out[...] = ga
        da_out[...] = delta
        ma_out[...] = m_new
        va_out[...] = v_new
        loss_out[...] = jnp.broadcast_to(jnp.sum(ga[na - 1:na, :], axis=1, keepdims=True), loss_out.shape)
        gb = _sum_blocks(tg_ref)
        delta, m_new, v_new = _adamw_math(wb_ref[...], gb, mb_ref[...], vb_ref[...])
        gb_out[...] = gb
        db_out[...] = delta
        mb_out[...] = m_new
        vb_out[...] = v_new

    oa, ob = jax.ShapeDtypeStruct((na, d), F32), jax.ShapeDtypeStruct((nb, cb), F32)
    vm = pl.BlockSpec(memory_space=pltpu.VMEM)
    return pl.pallas_call(
        body, name=name, in_specs=[vm] * 8, out_specs=[vm] * 9,
        out_shape=[oa] * 4 + [ob] * 4 + [jax.ShapeDtypeStruct((1, LANES), F32)],
        compiler_params=pltpu.CompilerParams(vmem_limit_bytes=_vmem_limit(_nbytes(part_g.shape, F32))),
    )(part_g, tap_g, w_a, m_a, v_a, w_b, m_b, v_b)


BIG = ("cf_w_pw1", "cf_w_pw2", "sc_w_in", "sc_w_out", "mlp_w1", "mlp_w2", "ple_w_proj", "ple_w_gate")
COL_SHARDED = ("cf_w_pw1", "sc_w_in", "mlp_w1", "ple_w_proj")
WEIGHT_ORDER = ("norm_mix", "norm_mlp", "norm_ple", "cf_w_pw1", "cf_b_pw1", "cf_w_dw", "cf_b_dw", "cf_norm",
                "cf_w_pw2", "cf_b_pw2", "sc_w_in", "sc_w_conv", "sc_w_out", "mlp_w1", "mlp_w2", "ple_w_proj",
                "ple_w_gate", "norm_final")
SMALL_ROWS = (("norm_mix", 4), ("norm_mlp", 4), ("norm_ple", 4), ("cf_b_pw1", 4), ("cf_b_dw", 2), ("cf_norm", 2),
              ("cf_b_pw2", 2), ("norm_final", 1))


def _layer_weights(i):
    mixer = (("cf_w_pw1", i // 2), ("cf_w_pw2", i // 2)) if i % 2 == 0 else (("sc_w_in", i // 2), ("sc_w_out", i // 2))
    return mixer + (("mlp_w1", i), ("mlp_w2", i), ("ple_w_proj", i), ("ple_w_gate", i))


def _pad_rows(a, rows):
    return jnp.pad(a, ((0, 0), (0, rows - a.shape[1]), (0, 0)))


def _pack_taps(cf, sc):
    c = cf.shape[2]
    return jnp.concatenate([_pad_rows(cf, CONV_A_HALO).reshape(-1, c), _pad_rows(sc, CONV_B_HALO).reshape(-1, c)], axis=0)


def _unpack_taps(t, n_cf):
    c = t.shape[1]
    cf = t[:n_cf * CONV_A_HALO].reshape(n_cf, CONV_A_HALO, c)[:, :CONV_A_TAPS]
    sc = t[n_cf * CONV_A_HALO:].reshape(-1, CONV_B_HALO, c)[:, :CONV_B_TAPS]
    return cf, sc


def _pack_small(vals, d):
    return jnp.concatenate([vals[k].reshape(-1, d) for k, _ in SMALL_ROWS] + [jnp.zeros((1, d), F32)], axis=0)


def _unpack_small(a, shapes):
    out, r = {}, 0
    for k, n in SMALL_ROWS:
        out[k] = a[r:r + n].reshape(shapes[k])
        r += n
    return out


def kernel(x, p, norm_mix, norm_mlp, norm_ple, cf_w_pw1, cf_b_pw1, cf_w_dw, cf_b_dw, cf_norm, cf_w_pw2, cf_b_pw2, sc_w_in, sc_w_conv, sc_w_out, mlp_w1, mlp_w2, ple_w_proj, ple_w_gate, norm_final, loss_target, m_norm_mix, m_norm_mlp, m_norm_ple, m_cf_w_pw1, m_cf_b_pw1, m_cf_w_dw, m_cf_b_dw, m_cf_norm, m_cf_w_pw2, m_cf_b_pw2, m_sc_w_in, m_sc_w_conv, m_sc_w_out, m_mlp_w1, m_mlp_w2, m_ple_w_proj, m_ple_w_gate, m_norm_final, v_norm_mix, v_norm_mlp, v_norm_ple, v_cf_w_pw1, v_cf_b_pw1, v_cf_w_dw, v_cf_b_dw, v_cf_norm, v_cf_w_pw2, v_cf_b_pw2, v_sc_w_in, v_sc_w_conv, v_sc_w_out, v_mlp_w1, v_mlp_w2, v_ple_w_proj, v_ple_w_gate, v_norm_final):
    w = dict(norm_mix=norm_mix, norm_mlp=norm_mlp, norm_ple=norm_ple, cf_w_pw1=cf_w_pw1, cf_b_pw1=cf_b_pw1,
             cf_w_dw=cf_w_dw, cf_b_dw=cf_b_dw, cf_norm=cf_norm, cf_w_pw2=cf_w_pw2, cf_b_pw2=cf_b_pw2,
             sc_w_in=sc_w_in, sc_w_conv=sc_w_conv, sc_w_out=sc_w_out, mlp_w1=mlp_w1, mlp_w2=mlp_w2,
             ple_w_proj=ple_w_proj, ple_w_gate=ple_w_gate, norm_final=norm_final)
    m = dict(norm_mix=m_norm_mix, norm_mlp=m_norm_mlp, norm_ple=m_norm_ple, cf_w_pw1=m_cf_w_pw1, cf_b_pw1=m_cf_b_pw1,
             cf_w_dw=m_cf_w_dw, cf_b_dw=m_cf_b_dw, cf_norm=m_cf_norm, cf_w_pw2=m_cf_w_pw2, cf_b_pw2=m_cf_b_pw2,
             sc_w_in=m_sc_w_in, sc_w_conv=m_sc_w_conv, sc_w_out=m_sc_w_out, mlp_w1=m_mlp_w1, mlp_w2=m_mlp_w2,
             ple_w_proj=m_ple_w_proj, ple_w_gate=m_ple_w_gate, norm_final=m_norm_final)
    v = dict(norm_mix=v_norm_mix, norm_mlp=v_norm_mlp, norm_ple=v_norm_ple, cf_w_pw1=v_cf_w_pw1, cf_b_pw1=v_cf_b_pw1,
             cf_w_dw=v_cf_w_dw, cf_b_dw=v_cf_b_dw, cf_norm=v_cf_norm, cf_w_pw2=v_cf_w_pw2, cf_b_pw2=v_cf_b_pw2,
             sc_w_in=v_sc_w_in, sc_w_conv=v_sc_w_conv, sc_w_out=v_sc_w_out, mlp_w1=v_mlp_w1, mlp_w2=v_mlp_w2,
             ple_w_proj=v_ple_w_proj, ple_w_gate=v_ple_w_gate, norm_final=v_norm_final)
    depth, d = norm_mix.shape
    n_cf = cf_w_dw.shape[0]
    me = 4 * lax.axis_index("x") + 2 * lax.axis_index("y") + lax.axis_index("c")

    taps_w = _pack_taps(cf_w_dw, sc_w_conv)
    taps_all = all_gather("ag_taps", [(taps_w[None], 0)])[0]
    taps_full = jnp.transpose(taps_all[:, 0], (1, 0, 2)).reshape(taps_w.shape[0], d)
    conv_w = {"cf": taps_full[:n_cf * CONV_A_HALO].reshape(n_cf, CONV_A_HALO, d),
              "sc": taps_full[n_cf * CONV_A_HALO:].reshape(-1, CONV_B_HALO, d)}

    shards = {k: cast_bf16(f"cast_{k}", w[k]) for k in BIG}
    gathering, gathered = {}, {}
    token = None
    for i in range(depth):
        names = _layer_weights(i)
        started, token = exchange_start(
            f"gather_start_{i}", [shards[k] for k, _ in names],
            [jax.ShapeDtypeStruct((NDEV, 1) + shards[k].shape[1:], BF16) for k, _ in names],
            [_gather_src(l) for _, l in names], [_gather_dst] * len(names), after=token)
        gathering.update(zip(names, started))
    first_wait_after = [token]

    def get_w(k, l, after=None):
        if (k, l) not in gathered:
            if first_wait_after:
                after = first_wait_after.pop()
            got = exchange_wait(f"gather_wait_{k}_{l}", gathering[(k, l)], _gather_src(l), _gather_dst, after)
            gathered[(k, l)] = got if k in COL_SHARDED else got.reshape(-1, got.shape[3])
        return gathered[(k, l)]

    exchanging = {}

    def put_grads(grads):
        names = list(grads)
        tag = "_".join(f"{k}{l}" for k, l in names[:1])
        started, tok = exchange_start(f"grad_start_{tag}", [grads[n] for n in names],
                                      [jax.ShapeDtypeStruct(grads[n].shape, BF16) for n in names],
                                      [_slice_of] * len(names), [_slice_of] * len(names))
        exchanging.update(zip(names, started))
        return tok

    small = {k: w[k] for k, _ in SMALL_ROWS}
    small["norm_final"] = norm_final[None]
    loss_part, grad_x, sg = _local_step(x[0], p[:, 0], loss_target[0], small, get_w, conv_w, put_grads)

    out = {k: None for k in BIG}
    for (k, l), started in exchanging.items():
        recv = exchange_wait(f"grad_wait_{k}_{l}", started, _slice_of, _slice_of, grad_x)
        out[k] = adamw_layer(f"adamw_{k}_{l}", recv, w[k], m[k], v[k], l, out[k])

    parts = []
    for k, n in SMALL_ROWS:
        for g in sg[k]:
            parts += [g[:, :d], g[:, d:]] if g.shape[1] == 2 * d else [g]
    parts.append(loss_part)
    part_pack = jnp.concatenate(parts, axis=0)
    tap_pack = jnp.concatenate(sg["cf_w_dw"] + sg["sc_w_conv"], axis=0)
    part_all, tap_all = all_gather("ag_small", [(part_pack[None], 0), (tap_pack[None], 0)])
    cb = cf_w_dw.shape[2]
    tap_mine = lax.dynamic_slice_in_dim(tap_all[:, 0], me * cb, cb, axis=2)
    sm = small_update("small_update", part_all[:, 0], tap_mine,
                      _pack_small(w, d), _pack_small(m, d), _pack_small(v, d),
                      taps_w, _pack_taps(m["cf_w_dw"], m["sc_w_conv"]), _pack_taps(v["cf_w_dw"], v["sc_w_conv"]))
    shapes = {k: w[k].shape for k, _ in SMALL_ROWS}
    for t in range(4):
        un = _unpack_small(sm[t], shapes)
        cf_t, sc_t = _unpack_taps(sm[4 + t], n_cf)
        for k in un:
            out.setdefault(k, [None] * 4)[t] = un[k]
        out.setdefault("cf_w_dw", [None] * 4)[t] = cf_t
        out.setdefault("sc_w_conv", [None] * 4)[t] = sc_t
    loss = sm[8][0, 0]

    return (loss, grad_x[None], *[out[k][0] for k in WEIGHT_ORDER], *[out[k][1] for k in WEIGHT_ORDER],
            *[out[k][2] for k in WEIGHT_ORDER], *[out[k][3] for k in WEIGHT_ORDER])
```

```python
import functools

import jax
import jax.numpy as jnp
from jax import lax
from jax.experimental import pallas as pl
from jax.experimental.pallas import tpu as pltpu
from jax.experimental.pallas import tpu_sc as plsc

F32 = jnp.float32
BF16 = jnp.bfloat16
EPS = 1e-6
NDEV = 8
N_PEERS = NDEV - 1
MESH = pl.DeviceIdType.MESH

ADAM_LR = 0.001
ADAM_B1 = 0.9
ADAM_B2 = 0.999
ADAM_EPS = 1e-08
ADAM_WD = 0.01
ADAM_STEP = 10

V7X_VMEM_BYTES = 64 * 1024 * 1024
VMEM_LIMIT_MAX = 56 * 1024 * 1024
SUBLANES = 8
LANES = 128
CONV_A_TAPS = 31
CONV_A_HALO = 32
CONV_B_TAPS = 3
CONV_B_HALO = 8


def _nbytes(shape, dtype):
    n = 1
    for s in shape:
        if s is not None:
            n *= s
    return n * jnp.dtype(dtype).itemsize


def _vmem_limit(block_bytes, scratch_bytes=0):
    need = 2 * block_bytes + scratch_bytes
    return int(min(VMEM_LIMIT_MAX, max(32 * 1024 * 1024, need + need // 2 + (4 << 20))))


def _params(sem, block_bytes, scratch_bytes=0):
    return pltpu.CompilerParams(dimension_semantics=sem, vmem_limit_bytes=_vmem_limit(block_bytes, scratch_bytes))


_DIMS = {
    "nn": (((1,), (0,)), ((), ())),
    "nt": (((1,), (1,)), ((), ())),
    "tn": (((0,), (0,)), ((), ())),
}


def _mm(name, dims, grid, acc_shape, a, a_spec, b, b_spec, extras, outs, epi):
    ni, nj, nk = grid
    n_ex, n_out = len(extras), len(outs)
    dn = _DIMS[dims]

    def body(*refs):
        a_ref, b_ref = refs[0], refs[1]
        ex_refs = refs[2:2 + n_ex]
        out_refs = refs[2 + n_ex:2 + n_ex + n_out]
        d = lax.dot_general(a_ref[...].astype(BF16), b_ref[...].astype(BF16), dn, preferred_element_type=F32)

        def finish(acc):
            res = epi(acc, *[r[...] for r in ex_refs])
            for o_ref, r in zip(out_refs, res):
                o_ref[...] = r.astype(o_ref.dtype)

        if nk == 1:
            finish(d)
        else:
            acc_ref = refs[2 + n_ex + n_out]
            k = pl.program_id(2)

            @pl.when(k == 0)
            def _():
                acc_ref[...] = d

            @pl.when(jnp.logical_and(k > 0, k < nk - 1))
            def _():
                acc_ref[...] += d

            @pl.when(k == nk - 1)
            def _():
                finish(acc_ref[...] + d)

    blk = _nbytes(a_spec.block_shape, a.dtype) + _nbytes(b_spec.block_shape, b.dtype)
    for arr, spec in list(extras) + list(outs):
        blk += _nbytes(spec.block_shape, arr.dtype)
    acc_bytes = _nbytes(acc_shape, F32)
    scratch = [pltpu.VMEM(acc_shape, F32)] if nk > 1 else []
    return pl.pallas_call(
        body,
        name=name,
        grid=grid,
        in_specs=[a_spec, b_spec] + [s for _, s in extras],
        out_specs=[s for _, s in outs],
        out_shape=[o for o, _ in outs],
        scratch_shapes=scratch,
        compiler_params=_params(("parallel", "parallel", "arbitrary"), blk, 3 * acc_bytes),
    )(a, b, *[e for e, _ in extras])


def _tile(n, pref):
    if n <= pref:
        return n
    t = pref - pref % LANES
    while t > LANES and n % t:
        t -= LANES
    assert n % t == 0, (n, pref)
    return t


def _row_tile(n, pref):
    if n <= pref:
        return n
    t = max(SUBLANES, pref - pref % SUBLANES)
    while t > SUBLANES and n % t:
        t -= SUBLANES
    assert n % t == 0, (n, pref)
    return t


def _id_epi(acc):
    return (acc,)


def mm_x_wcol(name, x, w, layer, extras=(), outs_dtypes=(F32,), epi=_id_epi, tm=1024, tn=512):
    m, kdim = x.shape
    c = w.shape[3]
    n = NDEV * c
    tm, tn = _tile(m, tm), _tile(c, tn)
    tk = _tile(kdim, 2048)
    grid = (m // tm, n // tn, kdim // tk)
    per = c // tn
    a_spec = pl.BlockSpec((tm, tk), lambda i, j, k: (i, k))
    b_spec = pl.BlockSpec((None, None, tk, tn), lambda i, j, k: (j // per, layer, k, j % per))
    ex = [(e, _ex_spec(e, kind, tm, tn)) for e, kind in extras]
    o_spec = pl.BlockSpec((tm, tn), lambda i, j, k: (i, j))
    outs = [(jax.ShapeDtypeStruct((m, n), dt), o_spec) for dt in outs_dtypes]
    return _mm(name, "nn", grid, (tm, tn), x, a_spec, w, b_spec, ex, outs, epi)


def mm_x_wrow(name, x, w, extras=(), outs_dtypes=(F32,), epi=_id_epi, tm=1024, tn=512):
    m, kdim = x.shape
    n = w.shape[1]
    assert kdim == w.shape[0]
    tm, tn = _tile(m, tm), _tile(n, tn)
    tk = _tile(kdim, 2048)
    grid = (m // tm, n // tn, kdim // tk)
    a_spec = pl.BlockSpec((tm, tk), lambda i, j, k: (i, k))
    b_spec = pl.BlockSpec((tk, tn), lambda i, j, k: (k, j))
    ex = [(e, _ex_spec(e, kind, tm, tn)) for e, kind in extras]
    o_spec = pl.BlockSpec((tm, tn), lambda i, j, k: (i, j))
    outs = [(jax.ShapeDtypeStruct((m, n), dt), o_spec) for dt in outs_dtypes]
    return _mm(name, "nn", grid, (tm, tn), x, a_spec, w, b_spec, ex, outs, epi)


def mm_dy_wcol_t(name, dy, w, layer, extras=(), outs_dtypes=(F32,), epi=_id_epi, tm=1024, tn=1024):
    m, n = dy.shape
    kdim, c = w.shape[2], w.shape[3]
    assert n == NDEV * c
    tm, tn = _tile(m, tm), _tile(kdim, tn)
    tk = _tile(c, 2048)
    per = c // tk
    grid = (m // tm, kdim // tn, n // tk)
    a_spec = pl.BlockSpec((tm, tk), lambda i, j, k: (i, k))
    b_spec = pl.BlockSpec((None, None, tn, tk), lambda i, j, k: (k // per, layer, j, k % per))
    ex = [(e, _ex_spec(e, kind, tm, tn)) for e, kind in extras]
    o_spec = pl.BlockSpec((tm, tn), lambda i, j, k: (i, j))
    outs = [(jax.ShapeDtypeStruct((m, kdim), dt), o_spec) for dt in outs_dtypes]
    return _mm(name, "nt", grid, (tm, tn), dy, a_spec, w, b_spec, ex, outs, epi)


def mm_dy_wrow_t(name, dy, w, extras=(), outs_dtypes=(F32,), epi=_id_epi, tm=1024, tn=512):
    m, n = dy.shape
    kdim = w.shape[0]
    assert n == w.shape[1]
    tm, tn = _tile(m, tm), _tile(kdim, tn)
    tk = _tile(n, 2048)
    grid = (m // tm, kdim // tn, n // tk)
    a_spec = pl.BlockSpec((tm, tk), lambda i, j, k: (i, k))
    b_spec = pl.BlockSpec((tn, tk), lambda i, j, k: (j, k))
    ex = [(e, _ex_spec(e, kind, tm, tn)) for e, kind in extras]
    o_spec = pl.BlockSpec((tm, tn), lambda i, j, k: (i, j))
    outs = [(jax.ShapeDtypeStruct((m, kdim), dt), o_spec) for dt in outs_dtypes]
    return _mm(name, "nt", grid, (tm, tn), dy, a_spec, w, b_spec, ex, outs, epi)


def mm_xt_dy(name, x, dy, col_shards, tm=1024, tn=512):
    m, kdim = x.shape
    n = dy.shape[1]
    tk = _tile(m, 2048)
    if col_shards:
        c = n // NDEV
        tm, tn = _tile(kdim, tm), _tile(c, tn)
        per = c // tn
        out = jax.ShapeDtypeStruct((NDEV, kdim, c), BF16)
        o_spec = pl.BlockSpec((None, tm, tn), lambda i, j, k: (j // per, i, j % per))
    else:
        tm, tn = _tile(kdim, tm), _tile(n, tn)
        out = jax.ShapeDtypeStruct((kdim, n), BF16)
        o_spec = pl.BlockSpec((tm, tn), lambda i, j, k: (i, j))
    grid = (kdim // tm, n // tn, m // tk)
    a_spec = pl.BlockSpec((tk, tm), lambda i, j, k: (k, i))
    b_spec = pl.BlockSpec((tk, tn), lambda i, j, k: (k, j))
    g = _mm(name, "tn", grid, (tm, tn), x, a_spec, dy, b_spec, [], [(out, o_spec)], _id_epi)[0]
    return g if col_shards else g.reshape(NDEV, kdim // NDEV, n)


def _ex_spec(e, kind, tm, tn):
    if kind == "tile":
        return pl.BlockSpec((tm, tn), lambda i, j, k: (i, j))
    if kind == "row":
        return pl.BlockSpec((1, tn), lambda i, j, k: (0, j))
    raise ValueError(kind)


def _rows_call(name, body, n_rows, ts, ins, outs, scratch=(), scratch_bytes=0):
    blk = sum(_nbytes(s.block_shape, a.dtype) for a, s in list(ins) + list(outs))
    return pl.pallas_call(
        body,
        name=name,
        grid=(n_rows // ts,),
        in_specs=[s for _, s in ins],
        out_specs=[s for _, s in outs],
        out_shape=[o for o, _ in outs],
        scratch_shapes=list(scratch),
        compiler_params=_params(("arbitrary",), blk, scratch_bytes + 4 * blk // 2),
    )(*[a for a, _ in ins])


def _blk(ts, d):
    return pl.BlockSpec((ts, d), lambda i: (i, 0))


def _full(shape):
    return pl.BlockSpec(shape, lambda i: tuple(0 for _ in shape))


def _rowsum8(v):
    t, d = v.shape
    return jnp.sum(v.reshape(t // SUBLANES, SUBLANES, d), axis=0)


def _accumulate(ref, val):
    @pl.when(pl.program_id(0) == 0)
    def _():
        ref[...] = val

    @pl.when(pl.program_id(0) > 0)
    def _():
        ref[...] += val


def _rstd(x):
    return lax.rsqrt(jnp.mean(x * x, axis=-1, keepdims=True) + EPS)


def _rms_bwd_math(dy, x, g):
    r = _rstd(x)
    gdy = dy * g
    c = jnp.mean(gdy * x, axis=-1, keepdims=True)
    dx = r * gdy - x * (r * r * r * c)
    return dx, dy * (x * r)


def rms_fwd(name, h, g, ts=512):
    s, d = h.shape
    ts = min(ts, s)

    def body(h_ref, g_ref, u_ref):
        x = h_ref[...]
        u_ref[...] = ((x * _rstd(x)) * g_ref[...]).astype(BF16)

    return _rows_call(name, body, s, ts, [(h, _blk(ts, d)), (g, _full((1, d)))],
                      [(jax.ShapeDtypeStruct((s, d), BF16), _blk(ts, d))])[0]


def rms_bwd(name, du, h, g, dres, ts=256):
    s, d = h.shape
    ts = min(ts, s)

    def body(du_ref, h_ref, g_ref, dres_ref, dh_ref, dhb_ref, dg_ref, cs_ref):
        dx, dg = _rms_bwd_math(du_ref[...], h_ref[...], g_ref[...])
        dh = dres_ref[...] + dx
        dh_ref[...] = dh
        dhb_ref[...] = dh.astype(BF16)
        _accumulate(dg_ref, _rowsum8(dg))
        _accumulate(cs_ref, _rowsum8(dh))

    return _rows_call(
        name, body, s, ts,
        [(du, _blk(ts, d)), (h, _blk(ts, d)), (g, _full((1, d))), (dres, _blk(ts, d))],
        [(jax.ShapeDtypeStruct((s, d), F32), _blk(ts, d)), (jax.ShapeDtypeStruct((s, d), BF16), _blk(ts, d)),
         (jax.ShapeDtypeStruct((SUBLANES, d), F32), _full((SUBLANES, d))),
         (jax.ShapeDtypeStruct((SUBLANES, d), F32), _full((SUBLANES, d)))])


def final_loss(name, h, g, target, ts=256):
    s, d = h.shape
    ts = min(ts, s)

    def body(h_ref, g_ref, t_ref, loss_ref, dh_ref, dg_ref):
        x = h_ref[...]
        gf = g_ref[...]
        y = (x * _rstd(x)) * gf
        err = y - t_ref[...]
        _accumulate(loss_ref, _rowsum8(err * err) * (0.5 / d))
        dx, dg = _rms_bwd_math(err * (1.0 / d), x, gf)
        dh_ref[...] = dx
        _accumulate(dg_ref, _rowsum8(dg))

    return _rows_call(
        name, body, s, ts,
        [(h, _blk(ts, d)), (g, _full((1, d))), (target, _blk(ts, d))],
        [(jax.ShapeDtypeStruct((SUBLANES, d), F32), _full((SUBLANES, d))),
         (jax.ShapeDtypeStruct((s, d), F32), _blk(ts, d)),
         (jax.ShapeDtypeStruct((SUBLANES, d), F32), _full((SUBLANES, d)))])


def ple_bwd_elem(name, dh, g, e, ts=512):
    s, d = dh.shape
    ts = min(ts, s)

    def body(dh_ref, g_ref, e_ref, de_ref, dgl_ref):
        dh_v, g_v = dh_ref[...], g_ref[...]
        de_ref[...] = (dh_v * g_v).astype(BF16)
        dgl_ref[...] = (dh_v * e_ref[...] * (g_v * (1.0 - g_v))).astype(BF16)

    return _rows_call(name, body, s, ts, [(dh, _blk(ts, d)), (g, _blk(ts, d)), (e, _blk(ts, d))],
                      [(jax.ShapeDtypeStruct((s, d), BF16), _blk(ts, d)),
                       (jax.ShapeDtypeStruct((s, d), BF16), _blk(ts, d))])


def _chunks(n_rows, d, fn):
    lc = min(256, d)
    rc = min(64, n_rows)

    def lane_body(c, carry):
        lanes = pl.ds(pl.multiple_of(c * lc, lc), lc)
        for r in range(n_rows // rc):
            fn(r * rc, rc, lanes)
        return carry

    lax.fori_loop(0, d // lc, lane_body, 0)


def _prev_halo_spec(ts, halo, width):
    per = ts // halo
    return pl.BlockSpec((halo, width), lambda i: (jnp.maximum(i * per - 1, 0), 0))


def _next_halo_spec(ts, halo, width, n_rows):
    per = ts // halo
    last = n_rows // halo - 1
    return pl.BlockSpec((halo, width), lambda i: (jnp.minimum((i + 1) * per, last), 0))


def cf_fwd_mid(name, a, w_dw, b_dw, gn, ts=256):
    s, d2 = a.shape
    d = d2 // 2
    ts = min(ts, s)
    hl = CONV_A_HALO
    off = hl - (CONV_A_TAPS - 1)

    def body(a_ref, ah_ref, w_ref, b_ref, gn_ref, v0_ref, v1_ref, v3_ref, buf):
        first = pl.program_id(0) == 0
        halo = ah_ref[...]
        hv0 = halo[:, :d] * jax.nn.sigmoid(halo[:, d:])
        buf[pl.ds(0, hl), :] = jnp.where(first, 0.0, hv0)
        main = a_ref[...]
        v0 = main[:, :d] * jax.nn.sigmoid(main[:, d:])
        buf[pl.ds(hl, ts), :] = v0
        v0_ref[...] = v0

        def conv(r0, rc, lanes):
            acc = jnp.zeros((rc, lanes.size), F32)
            for k in range(CONV_A_TAPS):
                acc = acc + w_ref[pl.ds(k, 1), lanes] * buf[pl.ds(r0 + off + k, rc), lanes]
            v1_ref[pl.ds(r0, rc), lanes] = acc + b_ref[:, lanes]

        _chunks(ts, d, conv)
        v1 = v1_ref[...]
        v2 = (v1 * _rstd(v1)) * gn_ref[...]
        v3_ref[...] = (v2 * jax.nn.sigmoid(v2)).astype(BF16)

    return _rows_call(
        name, body, s, ts,
        [(a, _blk(ts, d2)), (a, _prev_halo_spec(ts, hl, d2)), (w_dw, _full(w_dw.shape)),
         (b_dw, _full((1, d))), (gn, _full((1, d)))],
        [(jax.ShapeDtypeStruct((s, d), F32), _blk(ts, d)), (jax.ShapeDtypeStruct((s, d), F32), _blk(ts, d)),
         (jax.ShapeDtypeStruct((s, d), BF16), _blk(ts, d))],
        scratch=[pltpu.VMEM((hl + ts, d), F32)], scratch_bytes=_nbytes((hl + ts, d), F32))


def cf_bwd_rows(name, dv3, v1, gn, ts=256):
    s, d = v1.shape
    ts = min(ts, s)

    def body(dv3_ref, v1_ref, gn_ref, dv1_ref, dgn_ref, db_ref):
        v1 = v1_ref[...]
        gn_v = gn_ref[...]
        v2 = (v1 * _rstd(v1)) * gn_v
        sg = jax.nn.sigmoid(v2)
        dv2 = dv3_ref[...] * (sg * (1.0 + v2 * (1.0 - sg)))
        dv1, dgn = _rms_bwd_math(dv2, v1, gn_v)
        dv1_ref[...] = dv1
        _accumulate(dgn_ref, _rowsum8(dgn))
        _accumulate(db_ref, _rowsum8(dv1))

    return _rows_call(
        name, body, s, ts, [(dv3, _blk(ts, d)), (v1, _blk(ts, d)), (gn, _full((1, d)))],
        [(jax.ShapeDtypeStruct((s, d), F32), _blk(ts, d)),
         (jax.ShapeDtypeStruct((SUBLANES, d), F32), _full((SUBLANES, d))),
         (jax.ShapeDtypeStruct((SUBLANES, d), F32), _full((SUBLANES, d)))])


def cf_bwd_conv(name, dv1, v0, a, w_dw, ts=256):
    s, d = dv1.shape
    ts = min(ts, s)
    hl = CONV_A_HALO
    taps = CONV_A_TAPS
    off = hl - (taps - 1)
    last_blk = s // ts - 1

    def body(dv1_ref, dv1n_ref, v0_ref, v0p_ref, a_ref, w_ref, da_ref, dw_ref, db_ref, dbuf, vbuf, dv0_buf, dw_acc):
        i = pl.program_id(0)
        dbuf[pl.ds(0, ts), :] = dv1_ref[...]
        dbuf[pl.ds(ts, hl), :] = jnp.where(i == last_blk, 0.0, dv1n_ref[...])
        vbuf[pl.ds(0, hl), :] = jnp.where(i == 0, 0.0, v0p_ref[...])
        vbuf[pl.ds(hl, ts), :] = v0_ref[...]

        @pl.when(i == 0)
        def _():
            dw_acc[...] = jnp.zeros_like(dw_acc)

        def conv_t(r0, rc, lanes):
            g = dbuf[pl.ds(r0, rc), lanes]
            acc = jnp.zeros((rc, lanes.size), F32)
            for k in range(taps):
                acc = acc + w_ref[pl.ds(k, 1), lanes] * dbuf[pl.ds(r0 + taps - 1 - k, rc), lanes]
                prod = g * vbuf[pl.ds(r0 + off + k, rc), lanes]
                dw_acc[pl.ds(k * SUBLANES, SUBLANES), lanes] += _rowsum8(prod)
            dv0_buf[pl.ds(r0, rc), lanes] = acc

        _chunks(ts, d, conv_t)
        dv0 = dv0_buf[...]
        av = a_ref[...]
        val, sg = av[:, :d], jax.nn.sigmoid(av[:, d:])
        dval = dv0 * sg
        dgate = dv0 * val * (sg * (1.0 - sg))
        da_ref[:, :d] = dval.astype(BF16)
        da_ref[:, d:] = dgate.astype(BF16)
        _accumulate(db_ref.at[:, pl.ds(0, d)], _rowsum8(dval))
        _accumulate(db_ref.at[:, pl.ds(d, d)], _rowsum8(dgate))

        @pl.when(i == last_blk)
        def _():
            dw_ref[...] = jnp.sum(dw_acc[...].reshape(hl, SUBLANES, d), axis=1)

    scratch = [pltpu.VMEM((ts + hl, d), F32), pltpu.VMEM((hl + ts, d), F32), pltpu.VMEM((ts, d), F32),
               pltpu.VMEM((hl * SUBLANES, d), F32)]
    sbytes = _nbytes((3 * ts + 2 * hl + hl * SUBLANES, d), F32)
    return _rows_call(
        name, body, s, ts,
        [(dv1, _blk(ts, d)), (dv1, _next_halo_spec(ts, hl, d, s)), (v0, _blk(ts, d)), (v0, _prev_halo_spec(ts, hl, d)),
         (a, _blk(ts, 2 * d)), (w_dw, _full(w_dw.shape))],
        [(jax.ShapeDtypeStruct((s, 2 * d), BF16), _blk(ts, 2 * d)),
         (jax.ShapeDtypeStruct((hl, d), F32), _full((hl, d))),
         (jax.ShapeDtypeStruct((SUBLANES, 2 * d), F32), _full((SUBLANES, 2 * d)))],
        scratch=scratch, scratch_bytes=sbytes)


def sc_fwd_mid(name, bcv, w_conv, ts=256):
    s, d3 = bcv.shape
    d = d3 // 3
    ts = min(ts, s)
    hl = CONV_B_HALO
    off = hl - (CONV_B_TAPS - 1)

    def body(x_ref, xp_ref, w_ref, y_ref, buf):
        hp = xp_ref[...]
        buf[pl.ds(0, hl), :] = jnp.where(pl.program_id(0) == 0, 0.0, hp[:, d:2 * d] * hp[:, 2 * d:])
        buf[pl.ds(hl, ts), :] = x_ref[:, d:2 * d] * x_ref[:, 2 * d:]
        cc = jnp.zeros((ts, d), F32)
        for k in range(CONV_B_TAPS):
            cc = cc + w_ref[pl.ds(k, 1), :] * buf[pl.ds(off + k, ts), :]
        y_ref[...] = (x_ref[:, :d] * cc).astype(BF16)

    return _rows_call(
        name, body, s, ts,
        [(bcv, _blk(ts, d3)), (bcv, _prev_halo_spec(ts, hl, d3)), (w_conv, _full(w_conv.shape))],
        [(jax.ShapeDtypeStruct((s, d), BF16), _blk(ts, d))],
        scratch=[pltpu.VMEM((hl + ts, d), F32)], scratch_bytes=_nbytes((hl + ts, d), F32))[0]


def sc_bwd_mid(name, dy, bcv, w_conv, ts=256):
    s, d3 = bcv.shape
    d = d3 // 3
    ts = min(ts, s)
    hl = CONV_B_HALO
    taps = CONV_B_TAPS
    off = hl - (taps - 1)
    last_blk = s // ts - 1

    def body(dy_ref, dyn_ref, x_ref, xp_ref, xn_ref, w_ref, dx_ref, dw_ref, cvbuf, dbuf, dw_acc):
        i = pl.program_id(0)
        hp = xp_ref[...]
        cvbuf[pl.ds(0, hl), :] = jnp.where(i == 0, 0.0, hp[:, d:2 * d] * hp[:, 2 * d:])
        gb, gc, v = x_ref[:, :d], x_ref[:, d:2 * d], x_ref[:, 2 * d:]
        cvbuf[pl.ds(hl, ts), :] = gc * v
        dy_v = dy_ref[...]
        dcc = dy_v * gb
        dbuf[pl.ds(0, ts), :] = dcc
        dbuf[pl.ds(ts, hl), :] = jnp.where(i == last_blk, 0.0, dyn_ref[...] * xn_ref[:, :d])

        @pl.when(i == 0)
        def _():
            dw_acc[...] = jnp.zeros_like(dw_acc)

        cc = jnp.zeros((ts, d), F32)
        dcv = jnp.zeros((ts, d), F32)
        for k in range(taps):
            win = cvbuf[pl.ds(off + k, ts), :]
            cc = cc + w_ref[pl.ds(k, 1), :] * win
            dcv = dcv + w_ref[pl.ds(k, 1), :] * dbuf[pl.ds(taps - 1 - k, ts), :]
            dw_acc[pl.ds(k * SUBLANES, SUBLANES), :] += _rowsum8(dcc * win)
        dx_ref[:, :d] = (dy_v * cc).astype(BF16)
        dx_ref[:, d:2 * d] = (dcv * v).astype(BF16)
        dx_ref[:, 2 * d:] = (dcv * gc).astype(BF16)

        @pl.when(i == last_blk)
        def _():
            dw_ref[...] = jnp.sum(dw_acc[...].reshape(hl, SUBLANES, d), axis=1)

    scratch = [pltpu.VMEM((hl + ts, d), F32), pltpu.VMEM((ts + hl, d), F32), pltpu.VMEM((hl * SUBLANES, d), F32)]
    sbytes = _nbytes((2 * ts + 2 * hl + hl * SUBLANES, d), F32)
    return _rows_call(
        name, body, s, ts,
        [(dy, _blk(ts, d)), (dy, _next_halo_spec(ts, hl, d, s)), (bcv, _blk(ts, d3)), (bcv, _prev_halo_spec(ts, hl, d3)),
         (bcv, _next_halo_spec(ts, hl, d3, s)), (w_conv, _full(w_conv.shape))],
        [(jax.ShapeDtypeStruct((s, d3), BF16), _blk(ts, d3)), (jax.ShapeDtypeStruct((hl, d), F32), _full((hl, d)))],
        scratch=scratch, scratch_bytes=sbytes)


def _row(a, i):
    return lax.slice_in_dim(a, i, i + 1, axis=0)


def _local_step(x, p, target, small, get_w, conv_w, put_grads):
    depth = p.shape[0]
    acts = []
    h = x
    for i in range(depth):
        j = i // 2
        act = {"h": h}
        u = rms_fwd(f"rms_mix_{i}", h, _row(small["norm_mix"], i))
        act["u"] = u
        if i % 2 == 0:
            a = mm_x_wcol(f"cf_pw1_{i}", u, get_w("cf_w_pw1", j, u), 0, extras=[(_row(small["cf_b_pw1"], j), "row")],
                          epi=lambda acc, b: (acc + b,))[0]
            v0, v1, v3 = cf_fwd_mid(f"cf_mid_{i}", a, conv_w["cf"][j], _row(small["cf_b_dw"], j), _row(small["cf_norm"], j))
            act.update(a=a, v0=v0, v1=v1, v3=v3)
            h1 = mm_x_wrow(f"cf_pw2_{i}", v3, get_w("cf_w_pw2", j, v3),
                           extras=[(_row(small["cf_b_pw2"], j), "row"), (h, "tile")],
                           epi=lambda acc, b, res: (res + (acc + b),))[0]
        else:
            bcv = mm_x_wcol(f"sc_in_{i}", u, get_w("sc_w_in", j, u), 0, tn=768)[0]
            y = sc_fwd_mid(f"sc_mid_{i}", bcv, conv_w["sc"][j])
            act.update(bcv=bcv, y=y)
            h1 = mm_x_wrow(f"sc_out_{i}", y, get_w("sc_w_out", j, y), extras=[(h, "tile")],
                           epi=lambda acc, res: (res + acc,))[0]
        act["h1"] = h1
        u2 = rms_fwd(f"rms_mlp_{i}", h1, _row(small["norm_mlp"], i))
        z, hd = mm_x_wcol(f"mlp_w1_{i}", u2, get_w("mlp_w1", i, u2), 0, outs_dtypes=(F32, BF16),
                          epi=lambda acc: (acc, jnp.square(jnp.maximum(acc, 0.0))))
        h2 = mm_x_wrow(f"mlp_w2_{i}", hd, get_w("mlp_w2", i, hd), extras=[(h1, "tile")],
                       epi=lambda acc, res: (res + acc,), tn=1024)[0]
        act.update(u2=u2, z=z, hd=hd, h2=h2)
        n3 = rms_fwd(f"rms_ple_{i}", h2, _row(small["norm_ple"], i))
        e = mm_x_wcol(f"ple_proj_{i}", p[i], get_w("ple_w_proj", i, n3), 0)[0]

        def ple_epi(acc, e_t, res):
            g_t = jax.nn.sigmoid(acc)
            return g_t, res + g_t * e_t

        g, h3 = mm_x_wrow(f"ple_gate_{i}", n3, get_w("ple_w_gate", i, e), extras=[(e, "tile"), (h2, "tile")],
                          outs_dtypes=(F32, F32), epi=ple_epi)
        act.update(n3=n3, e=e, g=g)
        acts.append(act)
        h = h3

    loss_part, dh, dg_final = final_loss("final_loss", h, small["norm_final"], target)
    sg = {k: [None] * small[k].shape[0] for k in small if k != "norm_final"}
    sg["norm_final"] = [dg_final]
    sg["cf_w_dw"] = [None] * conv_w["cf"].shape[0]
    sg["sc_w_conv"] = [None] * conv_w["sc"].shape[0]

    for i in reversed(range(depth)):
        j = i // 2
        act = acts[i]
        de, dgl = ple_bwd_elem(f"ple_bwd_{i}", dh, act["g"], act["e"])
        g_proj = mm_xt_dy(f"d_ple_proj_{i}", p[i], de, True)
        g_gate = mm_xt_dy(f"d_ple_gate_{i}", act["n3"], dgl, False)
        dn3 = mm_dy_wrow_t(f"dn3_{i}", dgl, get_w("ple_w_gate", i))[0]
        dh2, dh2b, sg["norm_ple"][i], _ = rms_bwd(f"rms_ple_bwd_{i}", dn3, act["h2"], _row(small["norm_ple"], i), dh)
        g_w2 = mm_xt_dy(f"d_mlp_w2_{i}", act["hd"], dh2b, False)
        dz = mm_dy_wrow_t(f"dz_{i}", dh2b, get_w("mlp_w2", i), extras=[(act["z"], "tile")], outs_dtypes=(BF16,),
                          epi=lambda acc, z_t: (acc * (2.0 * jnp.maximum(z_t, 0.0)),))[0]
        g_w1 = mm_xt_dy(f"d_mlp_w1_{i}", act["u2"], dz, True)
        put_grads({("ple_w_proj", i): g_proj, ("ple_w_gate", i): g_gate, ("mlp_w2", i): g_w2, ("mlp_w1", i): g_w1})
        du2 = mm_dy_wcol_t(f"du2_{i}", dz, get_w("mlp_w1", i), 0)[0]
        dh1, dh1b, sg["norm_mlp"][i], cs1 = rms_bwd(f"rms_mlp_bwd_{i}", du2, act["h1"], _row(small["norm_mlp"], i), dh2)
        if i % 2 == 0:
            g_out = mm_xt_dy(f"d_cf_pw2_{i}", act["v3"], dh1b, False)
            sg["cf_b_pw2"][j] = cs1
            dv3 = mm_dy_wrow_t(f"dv3_{i}", dh1b, get_w("cf_w_pw2", j))[0]
            dv1, sg["cf_norm"][j], sg["cf_b_dw"][j] = cf_bwd_rows(f"cf_bwd_rows_{i}", dv3, act["v1"], _row(small["cf_norm"], j))
            da, sg["cf_w_dw"][j], sg["cf_b_pw1"][j] = cf_bwd_conv(f"cf_bwd_conv_{i}", dv1, act["v0"], act["a"], conv_w["cf"][j])
            g_in = mm_xt_dy(f"d_cf_pw1_{i}", act["u"], da, True)
            put_grads({("cf_w_pw2", j): g_out, ("cf_w_pw1", j): g_in})
            du = mm_dy_wcol_t(f"du_{i}", da, get_w("cf_w_pw1", j), 0)[0]
        else:
            g_out = mm_xt_dy(f"d_sc_out_{i}", act["y"], dh1b, False)
            dy = mm_dy_wrow_t(f"dy_{i}", dh1b, get_w("sc_w_out", j))[0]
            dbcv, sg["sc_w_conv"][j] = sc_bwd_mid(f"sc_bwd_mid_{i}", dy, act["bcv"], conv_w["sc"][j])
            g_in = mm_xt_dy(f"d_sc_in_{i}", act["u"], dbcv, True, tn=768)
            put_grads({("sc_w_out", j): g_out, ("sc_w_in", j): g_in})
            du = mm_dy_wcol_t(f"du_{i}", dbcv, get_w("sc_w_in", j), 0)[0]
        dh, _, sg["norm_mix"][i], _ = rms_bwd(f"rms_mix_bwd_{i}", du, act["h"], _row(small["norm_mix"], i), dh1)
    return loss_part, dh, sg


def _me_and_peers():
    x, y, c = lax.axis_index("x"), lax.axis_index("y"), lax.axis_index("c")
    me = 4 * x + 2 * y + c
    peers = []
    for q in range(1, NDEV):
        px = 1 - x if q & 4 else x
        py = 1 - y if q & 2 else y
        pc = 1 - c if q & 1 else c
        peers.append(((px, py, pc), 4 * px + 2 * py + pc))
    return me, peers


def _exchange(name, srcs, out_shapes, src_fns, dst_fns):
    n = len(srcs)

    def body(*refs):
        ins, outs = refs[:n], refs[n:2 * n]
        send_sems, recv_sems, local_sems = refs[2 * n:]
        me, peers = _me_and_peers()
        local, remote = [], []
        for k in range(n):
            cp = pltpu.make_async_copy(src_fns[k](ins[k], me), dst_fns[k](outs[k], me), local_sems.at[k])
            cp.start()
            local.append(cp)
        for q, (peer, peer_blk) in enumerate(peers):
            for k in range(n):
                cp = pltpu.make_async_remote_copy(
                    src_ref=src_fns[k](ins[k], peer_blk), dst_ref=dst_fns[k](outs[k], me),
                    send_sem=send_sems.at[k, q], recv_sem=recv_sems.at[k, q],
                    device_id=peer, device_id_type=MESH)
                cp.start()
                remote.append(cp)
        for q, (peer, peer_blk) in enumerate(peers):
            for k in range(n):
                pltpu.make_async_remote_copy(
                    src_ref=src_fns[k](ins[k], peer_blk), dst_ref=dst_fns[k](outs[k], peer_blk),
                    send_sem=send_sems.at[k, q], recv_sem=recv_sems.at[k, q],
                    device_id=peer, device_id_type=MESH).wait_recv()
        for cp in remote:
            cp.wait_send()
        for cp in local:
            cp.wait()

    any_spec = pl.BlockSpec(memory_space=pl.ANY)
    return pl.pallas_call(
        body,
        name=name,
        in_specs=[any_spec] * n,
        out_specs=[any_spec] * n,
        out_shape=out_shapes,
        scratch_shapes=[pltpu.SemaphoreType.DMA((n, N_PEERS)), pltpu.SemaphoreType.DMA((n, N_PEERS)),
                        pltpu.SemaphoreType.DMA((n,))],
    )(*srcs)


def sc_exchange(name, collective_id, srcs, out_shapes, src_fns, dst_fns):
    n = len(srcs)

    def body(*refs):
        ins, outs = refs[:n], refs[n:2 * n]
        send_sems, recv_sems, local_sems = refs[2 * n:]
        me, peers = _me_and_peers()
        barrier = pltpu.get_barrier_semaphore()
        for peer, _ in peers:
            pl.semaphore_signal(barrier, inc=1, device_id=peer, device_id_type=MESH)
        pl.semaphore_wait(barrier, N_PEERS)
        local, remote = [], []
        for k in range(n):
            cp = pltpu.make_async_copy(src_fns[k](ins[k], me), dst_fns[k](outs[k], me), local_sems.at[k])
            cp.start()
            local.append(cp)
        for q, (peer, peer_blk) in enumerate(peers):
            for k in range(n):
                cp = pltpu.make_async_remote_copy(
                    src_ref=src_fns[k](ins[k], peer_blk), dst_ref=dst_fns[k](outs[k], me),
                    send_sem=send_sems.at[k, q], recv_sem=recv_sems.at[k, q],
                    device_id=peer, device_id_type=MESH)
                cp.start()
                remote.append(cp)
        for q, (peer, peer_blk) in enumerate(peers):
            for k in range(n):
                pltpu.make_async_remote_copy(
                    src_ref=src_fns[k](ins[k], peer_blk), dst_ref=dst_fns[k](outs[k], peer_blk),
                    send_sem=send_sems.at[k, q], recv_sem=recv_sems.at[k, q],
                    device_id=peer, device_id_type=MESH).wait_recv()
        for cp in remote:
            cp.wait_send()
        for cp in local:
            cp.wait()

    return pl.kernel(
        body,
        out_type=out_shapes,
        mesh=plsc.ScalarSubcoreMesh(axis_name="sequencer", num_cores=1),
        name=name,
        scratch_types=[pltpu.SemaphoreType.DMA((n, N_PEERS)), pltpu.SemaphoreType.DMA((n, N_PEERS)),
                       pltpu.SemaphoreType.DMA((n,))],
        compiler_params=pltpu.CompilerParams(collective_id=collective_id),
    )(*srcs)


def all_gather(name, items):
    outs = [jax.ShapeDtypeStruct((NDEV, 1) + a.shape[1:], a.dtype) for a, _ in items]
    src_fns = [functools.partial(lambda ref, blk, layer: ref.at[layer], layer=l) for _, l in items]
    dst_fns = [lambda ref, blk: ref.at[blk, 0]] * len(items)
    return _exchange(name, [a for a, _ in items], outs, src_fns, dst_fns)


def _gather_src(layer):
    return lambda ref, blk: ref.at[layer]


def _gather_dst(ref, blk):
    return ref.at[blk, 0]


def _slice_of(ref, blk):
    return ref.at[blk]


def cast_bf16(name, w, tr_elems=512 * 1024):
    l, r, c = w.shape
    tr = _row_tile(r, tr_elems // c)
    spec = pl.BlockSpec((None, tr, c), lambda li, i: (li, i, 0))

    def body(w_ref, o_ref):
        o_ref[...] = w_ref[...].astype(BF16)

    return pl.pallas_call(
        body, name=name, grid=(l, r // tr), in_specs=[spec], out_specs=spec,
        out_shape=jax.ShapeDtypeStruct(w.shape, BF16),
        compiler_params=_params(("parallel", "parallel"), 6 * tr * c),
    )(w)


def _adamw_math(w, g, m, v):
    m = ADAM_B1 * m + (1.0 - ADAM_B1) * g
    v = ADAM_B2 * v + (1.0 - ADAM_B2) * (g * g)
    m_hat = m / (1.0 - ADAM_B1 ** ADAM_STEP)
    v_hat = v / (1.0 - ADAM_B2 ** ADAM_STEP)
    delta = -ADAM_LR * (m_hat / (jnp.sqrt(v_hat) + ADAM_EPS) + ADAM_WD * w)
    return delta, m, v


def _sum_blocks(ref):
    g = ref[0].astype(F32)
    for d in range(1, ref.shape[0]):
        g = g + ref[d].astype(F32)
    return g


def adamw_layer(name, recv, w, m, v, layer, stacked, tr_elems=128 * 1024):
    nd, r, c = recv.shape
    tr = _row_tile(r, tr_elems // c)
    r_spec = pl.BlockSpec((nd, tr, c), lambda i: (0, i, 0))
    w_spec = pl.BlockSpec((None, tr, c), lambda i: (layer, i, 0))
    if stacked is None:
        stacked = [lax.empty(w.shape, F32) for _ in range(4)]

    def body(r_ref, w_ref, m_ref, v_ref, g_in, d_in, m_in, v_in, g_out, d_out, m_out, v_out):
        g = _sum_blocks(r_ref)
        delta, m_new, v_new = _adamw_math(w_ref[...], g, m_ref[...], v_ref[...])
        g_out[...] = g
        d_out[...] = delta
        m_out[...] = m_new
        v_out[...] = v_new

    out = jax.ShapeDtypeStruct(w.shape, F32)
    return pl.pallas_call(
        body, name=name, grid=(r // tr,),
        in_specs=[r_spec, w_spec, w_spec, w_spec] + [pl.BlockSpec(memory_space=pl.ANY)] * 4,
        out_specs=[w_spec] * 4, out_shape=[out] * 4,
        input_output_aliases={4: 0, 5: 1, 6: 2, 7: 3},
        compiler_params=_params(("parallel",), tr * c * (2 * nd + 7 * 4)),
    )(recv, w, m, v, *stacked)


def small_update(name, part_g, tap_g, w_a, m_a, v_a, w_b, m_b, v_b):
    nd, rows, d = part_g.shape
    na = rows // SUBLANES
    nb, cb = w_b.shape

    def body(pg_ref, tg_ref, wa_ref, ma_ref, va_ref, wb_ref, mb_ref, vb_ref,
             ga_out, da_out, ma_out, va_out, gb_out, db_out, mb_out, vb_out, loss_out):
        ga = jnp.sum(_sum_blocks(pg_ref).reshape(na, SUBLANES, d), axis=1)
        delta, m_new, v_new = _adamw_math(wa_ref[...], ga, ma_ref[...], va_ref[...])
        ga_out[...] = ga
        da_out[...] = delta
        ma_out[...] = m_new
        va_out[...] = v_new
        loss_out[...] = jnp.broadcast_to(jnp.sum(ga[na - 1:na, :], axis=1, keepdims=True), loss_out.shape)
        gb = _sum_blocks(tg_ref)
        delta, m_new, v_new = _adamw_math(wb_ref[...], gb, mb_ref[...], vb_ref[...])
        gb_out[...] = gb
        db_out[...] = delta
        mb_out[...] = m_new
        vb_out[...] = v_new

    oa, ob = jax.ShapeDtypeStruct((na, d), F32), jax.ShapeDtypeStruct((nb, cb), F32)
    vm = pl.BlockSpec(memory_space=pltpu.VMEM)
    return pl.pallas_call(
        body, name=name, in_specs=[vm] * 8, out_specs=[vm] * 9,
        out_shape=[oa] * 4 + [ob] * 4 + [jax.ShapeDtypeStruct((1, LANES), F32)],
        compiler_params=pltpu.CompilerParams(vmem_limit_bytes=_vmem_limit(_nbytes(part_g.shape, F32))),
    )(part_g, tap_g, w_a, m_a, v_a, w_b, m_b, v_b)


BIG = ("cf_w_pw1", "cf_w_pw2", "sc_w_in", "sc_w_out", "mlp_w1", "mlp_w2", "ple_w_proj", "ple_w_gate")
COL_SHARDED = ("cf_w_pw1", "sc_w_in", "mlp_w1", "ple_w_proj")
WEIGHT_ORDER = ("norm_mix", "norm_mlp", "norm_ple", "cf_w_pw1", "cf_b_pw1", "cf_w_dw", "cf_b_dw", "cf_norm",
                "cf_w_pw2", "cf_b_pw2", "sc_w_in", "sc_w_conv", "sc_w_out", "mlp_w1", "mlp_w2", "ple_w_proj",
                "ple_w_gate", "norm_final")
SMALL_ROWS = (("norm_mix", 4), ("norm_mlp", 4), ("norm_ple", 4), ("cf_b_pw1", 4), ("cf_b_dw", 2), ("cf_norm", 2),
              ("cf_b_pw2", 2), ("norm_final", 1))


def _layer_weights(i):
    mixer = (("cf_w_pw1", i // 2), ("cf_w_pw2", i // 2)) if i % 2 == 0 else (("sc_w_in", i // 2), ("sc_w_out", i // 2))
    return mixer + (("mlp_w1", i), ("mlp_w2", i), ("ple_w_proj", i), ("ple_w_gate", i))


def _pad_rows(a, rows):
    return jnp.pad(a, ((0, 0), (0, rows - a.shape[1]), (0, 0)))


def _pack_taps(cf, sc):
    c = cf.shape[2]
    return jnp.concatenate([_pad_rows(cf, CONV_A_HALO).reshape(-1, c), _pad_rows(sc, CONV_B_HALO).reshape(-1, c)], axis=0)


def _unpack_taps(t, n_cf):
    c = t.shape[1]
    cf = t[:n_cf * CONV_A_HALO].reshape(n_cf, CONV_A_HALO, c)[:, :CONV_A_TAPS]
    sc = t[n_cf * CONV_A_HALO:].reshape(-1, CONV_B_HALO, c)[:, :CONV_B_TAPS]
    return cf, sc


def _pack_small(vals, d):
    return jnp.concatenate([vals[k].reshape(-1, d) for k, _ in SMALL_ROWS] + [jnp.zeros((1, d), F32)], axis=0)


def _unpack_small(a, shapes):
    out, r = {}, 0
    for k, n in SMALL_ROWS:
        out[k] = a[r:r + n].reshape(shapes[k])
        r += n
    return out


def kernel(x, p, norm_mix, norm_mlp, norm_ple, cf_w_pw1, cf_b_pw1, cf_w_dw, cf_b_dw, cf_norm, cf_w_pw2, cf_b_pw2, sc_w_in, sc_w_conv, sc_w_out, mlp_w1, mlp_w2, ple_w_proj, ple_w_gate, norm_final, loss_target, m_norm_mix, m_norm_mlp, m_norm_ple, m_cf_w_pw1, m_cf_b_pw1, m_cf_w_dw, m_cf_b_dw, m_cf_norm, m_cf_w_pw2, m_cf_b_pw2, m_sc_w_in, m_sc_w_conv, m_sc_w_out, m_mlp_w1, m_mlp_w2, m_ple_w_proj, m_ple_w_gate, m_norm_final, v_norm_mix, v_norm_mlp, v_norm_ple, v_cf_w_pw1, v_cf_b_pw1, v_cf_w_dw, v_cf_b_dw, v_cf_norm, v_cf_w_pw2, v_cf_b_pw2, v_sc_w_in, v_sc_w_conv, v_sc_w_out, v_mlp_w1, v_mlp_w2, v_ple_w_proj, v_ple_w_gate, v_norm_final):
    w = dict(norm_mix=norm_mix, norm_mlp=norm_mlp, norm_ple=norm_ple, cf_w_pw1=cf_w_pw1, cf_b_pw1=cf_b_pw1,
             cf_w_dw=cf_w_dw, cf_b_dw=cf_b_dw, cf_norm=cf_norm, cf_w_pw2=cf_w_pw2, cf_b_pw2=cf_b_pw2,
             sc_w_in=sc_w_in, sc_w_conv=sc_w_conv, sc_w_out=sc_w_out, mlp_w1=mlp_w1, mlp_w2=mlp_w2,
             ple_w_proj=ple_w_proj, ple_w_gate=ple_w_gate, norm_final=norm_final)
    m = dict(norm_mix=m_norm_mix, norm_mlp=m_norm_mlp, norm_ple=m_norm_ple, cf_w_pw1=m_cf_w_pw1, cf_b_pw1=m_cf_b_pw1,
             cf_w_dw=m_cf_w_dw, cf_b_dw=m_cf_b_dw, cf_norm=m_cf_norm, cf_w_pw2=m_cf_w_pw2, cf_b_pw2=m_cf_b_pw2,
             sc_w_in=m_sc_w_in, sc_w_conv=m_sc_w_conv, sc_w_out=m_sc_w_out, mlp_w1=m_mlp_w1, mlp_w2=m_mlp_w2,
             ple_w_proj=m_ple_w_proj, ple_w_gate=m_ple_w_gate, norm_final=m_norm_final)
    v = dict(norm_mix=v_norm_mix, norm_mlp=v_norm_mlp, norm_ple=v_norm_ple, cf_w_pw1=v_cf_w_pw1, cf_b_pw1=v_cf_b_pw1,
             cf_w_dw=v_cf_w_dw, cf_b_dw=v_cf_b_dw, cf_norm=v_cf_norm, cf_w_pw2=v_cf_w_pw2, cf_b_pw2=v_cf_b_pw2,
             sc_w_in=v_sc_w_in, sc_w_conv=v_sc_w_conv, sc_w_out=v_sc_w_out, mlp_w1=v_mlp_w1, mlp_w2=v_mlp_w2,
             ple_w_proj=v_ple_w_proj, ple_w_gate=v_ple_w_gate, norm_final=v_norm_final)
    depth, d = norm_mix.shape
    n_cf = cf_w_dw.shape[0]
    me = 4 * lax.axis_index("x") + 2 * lax.axis_index("y") + lax.axis_index("c")

    taps_w = _pack_taps(cf_w_dw, sc_w_conv)
    shards = {k: cast_bf16(f"cast_{k}", w[k]) for k in BIG}
    gathered = {}
    ids = iter(range(3 * depth))
    for i in range(depth):
        names = _layer_weights(i)
        srcs = [shards[k] for k, _ in names] + ([taps_w[None]] if i == 0 else [])
        layers = [l for _, l in names] + ([0] if i == 0 else [])
        got = sc_exchange(f"gather_{i}", next(ids), srcs,
                          [jax.ShapeDtypeStruct((NDEV, 1) + a.shape[1:], a.dtype) for a in srcs],
                          [_gather_src(l) for l in layers], [_gather_dst] * len(srcs))
        for (k, l), g in zip(names, got):
            gathered[(k, l)] = g if k in COL_SHARDED else g.reshape(-1, g.shape[3])
        if i == 0:
            taps_full = jnp.transpose(got[-1][:, 0], (1, 0, 2)).reshape(taps_w.shape[0], d)
    conv_w = {"cf": taps_full[:n_cf * CONV_A_HALO].reshape(n_cf, CONV_A_HALO, d),
              "sc": taps_full[n_cf * CONV_A_HALO:].reshape(-1, CONV_B_HALO, d)}

    def get_w(k, l, after=None):
        return gathered[(k, l)]

    received = {}

    def put_grads(grads):
        names = list(grads)
        got = sc_exchange(f"grad_exchange_{names[0][0]}_{names[0][1]}", next(ids), [grads[n] for n in names],
                          [jax.ShapeDtypeStruct(grads[n].shape, BF16) for n in names],
                          [_slice_of] * len(names), [_slice_of] * len(names))
        received.update(zip(names, got))
        return None

    small = {k: w[k] for k, _ in SMALL_ROWS}
    small["norm_final"] = norm_final[None]
    loss_part, grad_x, sg = _local_step(x[0], p[:, 0], loss_target[0], small, get_w, conv_w, put_grads)

    out = {k: None for k in BIG}
    for (k, l), recv in received.items():
        out[k] = adamw_layer(f"adamw_{k}_{l}", recv, w[k], m[k], v[k], l, out[k])

    parts = []
    for k, n in SMALL_ROWS:
        for g in sg[k]:
            parts += [g[:, :d], g[:, d:]] if g.shape[1] == 2 * d else [g]
    parts.append(loss_part)
    part_pack = jnp.concatenate(parts, axis=0)
    tap_pack = jnp.concatenate(sg["cf_w_dw"] + sg["sc_w_conv"], axis=0)
    part_all, tap_all = all_gather("ag_small", [(part_pack[None], 0), (tap_pack[None], 0)])
    cb = cf_w_dw.shape[2]
    tap_mine = lax.dynamic_slice_in_dim(tap_all[:, 0], me * cb, cb, axis=2)
    sm = small_update("small_update", part_all[:, 0], tap_mine,
                      _pack_small(w, d), _pack_small(m, d), _pack_small(v, d),
                      taps_w, _pack_taps(m["cf_w_dw"], m["sc_w_conv"]), _pack_taps(v["cf_w_dw"], v["sc_w_conv"]))
    shapes = {k: w[k].shape for k, _ in SMALL_ROWS}
    for t in range(4):
        un = _unpack_small(sm[t], shapes)
        cf_t, sc_t = _unpack_taps(sm[4 + t], n_cf)
        for k in un:
            out.setdefault(k, [None] * 4)[t] = un[k]
        out.setdefault("cf_w_dw", [None] * 4)[t] = cf_t
        out.setdefault("sc_w_conv", [None] * 4)[t] = sc_t
    loss = sm[8][0, 0]

    return (loss, grad_x[None], *[out[k][0] for k in WEIGHT_ORDER], *[out[k][1] for k in WEIGHT_ORDER],
            *[out[k][2] for k in WEIGHT_ORDER], *[out[k][3] for k in WEIGHT_ORDER])
```

```python
import functools

import jax
import jax.numpy as jnp
from jax import lax
from jax.experimental import pallas as pl
from jax.experimental.pallas import tpu as pltpu
from jax.experimental.pallas import tpu_sc as plsc

F32 = jnp.float32
BF16 = jnp.bfloat16
EPS = 1e-6
NDEV = 8
N_PEERS = NDEV - 1
MESH = pl.DeviceIdType.MESH

ADAM_LR = 0.001
ADAM_B1 = 0.9
ADAM_B2 = 0.999
ADAM_EPS = 1e-08
ADAM_WD = 0.01
ADAM_STEP = 10

V7X_VMEM_BYTES = 64 * 1024 * 1024
VMEM_LIMIT_MAX = 56 * 1024 * 1024
SUBLANES = 8
LANES = 128
CONV_A_TAPS = 31
CONV_A_HALO = 32
CONV_B_TAPS = 3
CONV_B_HALO = 8


def _nbytes(shape, dtype):
    n = 1
    for s in shape:
        if s is not None:
            n *= s
    return n * jnp.dtype(dtype).itemsize


def _vmem_limit(block_bytes, scratch_bytes=0):
    need = 2 * block_bytes + scratch_bytes
    return int(min(VMEM_LIMIT_MAX, max(32 * 1024 * 1024, need + need // 2 + (4 << 20))))


def _params(sem, block_bytes, scratch_bytes=0):
    return pltpu.CompilerParams(dimension_semantics=sem, vmem_limit_bytes=_vmem_limit(block_bytes, scratch_bytes))


_DIMS = {
    "nn": (((1,), (0,)), ((), ())),
    "nt": (((1,), (1,)), ((), ())),
    "tn": (((0,), (0,)), ((), ())),
}


def _mm(name, dims, grid, acc_shape, a, a_spec, b, b_spec, extras, outs, epi):
    ni, nj, nk = grid
    n_ex, n_out = len(extras), len(outs)
    dn = _DIMS[dims]

    def body(*refs):
        a_ref, b_ref = refs[0], refs[1]
        ex_refs = refs[2:2 + n_ex]
        out_refs = refs[2 + n_ex:2 + n_ex + n_out]
        d = lax.dot_general(a_ref[...].astype(BF16), b_ref[...].astype(BF16), dn, preferred_element_type=F32)

        def finish(acc):
            res = epi(acc, *[r[...] for r in ex_refs])
            for o_ref, r in zip(out_refs, res):
                o_ref[...] = r.astype(o_ref.dtype)

        if nk == 1:
            finish(d)
        else:
            acc_ref = refs[2 + n_ex + n_out]
            k = pl.program_id(2)

            @pl.when(k == 0)
            def _():
                acc_ref[...] = d

            @pl.when(jnp.logical_and(k > 0, k < nk - 1))
            def _():
                acc_ref[...] += d

            @pl.when(k == nk - 1)
            def _():
                finish(acc_ref[...] + d)

    blk = _nbytes(a_spec.block_shape, a.dtype) + _nbytes(b_spec.block_shape, b.dtype)
    for arr, spec in list(extras) + list(outs):
        blk += _nbytes(spec.block_shape, arr.dtype)
    acc_bytes = _nbytes(acc_shape, F32)
    scratch = [pltpu.VMEM(acc_shape, F32)] if nk > 1 else []
    return pl.pallas_call(
        body,
        name=name,
        grid=grid,
        in_specs=[a_spec, b_spec] + [s for _, s in extras],
        out_specs=[s for _, s in outs],
        out_shape=[o for o, _ in outs],
        scratch_shapes=scratch,
        compiler_params=_params(("parallel", "parallel", "arbitrary"), blk, 3 * acc_bytes),
    )(a, b, *[e for e, _ in extras])


def _tile(n, pref):
    if n <= pref:
        return n
    t = pref - pref % LANES
    while t > LANES and n % t:
        t -= LANES
    assert n % t == 0, (n, pref)
    return t


def _row_tile(n, pref):
    if n <= pref:
        return n
    t = max(SUBLANES, pref - pref % SUBLANES)
    while t > SUBLANES and n % t:
        t -= SUBLANES
    assert n % t == 0, (n, pref)
    return t


def _id_epi(acc):
    return (acc,)


def mm_x_wcol(name, x, w, layer, extras=(), outs_dtypes=(F32,), epi=_id_epi, tm=1024, tn=512):
    m, kdim = x.shape
    c = w.shape[3]
    n = NDEV * c
    tm, tn = _tile(m, tm), _tile(c, tn)
    tk = _tile(kdim, 2048)
    grid = (m // tm, n // tn, kdim // tk)
    per = c // tn
    a_spec = pl.BlockSpec((tm, tk), lambda i, j, k: (i, k))
    b_spec = pl.BlockSpec((None, None, tk, tn), lambda i, j, k: (j // per, layer, k, j % per))
    ex = [(e, _ex_spec(e, kind, tm, tn)) for e, kind in extras]
    o_spec = pl.BlockSpec((tm, tn), lambda i, j, k: (i, j))
    outs = [(jax.ShapeDtypeStruct((m, n), dt), o_spec) for dt in outs_dtypes]
    return _mm(name, "nn", grid, (tm, tn), x, a_spec, w, b_spec, ex, outs, epi)


def mm_x_wrow(name, x, w, extras=(), outs_dtypes=(F32,), epi=_id_epi, tm=1024, tn=512):
    m, kdim = x.shape
    n = w.shape[1]
    assert kdim == w.shape[0]
    tm, tn = _tile(m, tm), _tile(n, tn)
    tk = _tile(kdim, 2048)
    grid = (m // tm, n // tn, kdim // tk)
    a_spec = pl.BlockSpec((tm, tk), lambda i, j, k: (i, k))
    b_spec = pl.BlockSpec((tk, tn), lambda i, j, k: (k, j))
    ex = [(e, _ex_spec(e, kind, tm, tn)) for e, kind in extras]
    o_spec = pl.BlockSpec((tm, tn), lambda i, j, k: (i, j))
    outs = [(jax.ShapeDtypeStruct((m, n), dt), o_spec) for dt in outs_dtypes]
    return _mm(name, "nn", grid, (tm, tn), x, a_spec, w, b_spec, ex, outs, epi)


def mm_dy_wcol_t(name, dy, w, layer, extras=(), outs_dtypes=(F32,), epi=_id_epi, tm=1024, tn=1024):
    m, n = dy.shape
    kdim, c = w.shape[2], w.shape[3]
    assert n == NDEV * c
    tm, tn = _tile(m, tm), _tile(kdim, tn)
    tk = _tile(c, 2048)
    per = c // tk
    grid = (m // tm, kdim // tn, n // tk)
    a_spec = pl.BlockSpec((tm, tk), lambda i, j, k: (i, k))
    b_spec = pl.BlockSpec((None, None, tn, tk), lambda i, j, k: (k // per, layer, j, k % per))
    ex = [(e, _ex_spec(e, kind, tm, tn)) for e, kind in extras]
    o_spec = pl.BlockSpec((tm, tn), lambda i, j, k: (i, j))
    outs = [(jax.ShapeDtypeStruct((m, kdim), dt), o_spec) for dt in outs_dtypes]
    return _mm(name, "nt", grid, (tm, tn), dy, a_spec, w, b_spec, ex, outs, epi)


def mm_dy_wrow_t(name, dy, w, extras=(), outs_dtypes=(F32,), epi=_id_epi, tm=1024, tn=512):
    m, n = dy.shape
    kdim = w.shape[0]
    assert n == w.shape[1]
    tm, tn = _tile(m, tm), _tile(kdim, tn)
    tk = _tile(n, 2048)
    grid = (m // tm, kdim // tn, n // tk)
    a_spec = pl.BlockSpec((tm, tk), lambda i, j, k: (i, k))
    b_spec = pl.BlockSpec((tn, tk), lambda i, j, k: (j, k))
    ex = [(e, _ex_spec(e, kind, tm, tn)) for e, kind in extras]
    o_spec = pl.BlockSpec((tm, tn), lambda i, j, k: (i, j))
    outs = [(jax.ShapeDtypeStruct((m, kdim), dt), o_spec) for dt in outs_dtypes]
    return _mm(name, "nt", grid, (tm, tn), dy, a_spec, w, b_spec, ex, outs, epi)


def mm_xt_dy(name, x, dy, col_shards, tm=1024, tn=512):
    m, kdim = x.shape
    n = dy.shape[1]
    tk = _tile(m, 2048)
    if col_shards:
        c = n // NDEV
        tm, tn = _tile(kdim, tm), _tile(c, tn)
        per = c // tn
        out = jax.ShapeDtypeStruct((NDEV, kdim, c), BF16)
        o_spec = pl.BlockSpec((None, tm, tn), lambda i, j, k: (j // per, i, j % per))
    else:
        tm, tn = _tile(kdim, tm), _tile(n, tn)
        out = jax.ShapeDtypeStruct((kdim, n), BF16)
        o_spec = pl.BlockSpec((tm, tn), lambda i, j, k: (i, j))
    grid = (kdim // tm, n // tn, m // tk)
    a_spec = pl.BlockSpec((tk, tm), lambda i, j, k: (k, i))
    b_spec = pl.BlockSpec((tk, tn), lambda i, j, k: (k, j))
    g = _mm(name, "tn", grid, (tm, tn), x, a_spec, dy, b_spec, [], [(out, o_spec)], _id_epi)[0]
    return g if col_shards else g.reshape(NDEV, kdim // NDEV, n)


def _ex_spec(e, kind, tm, tn):
    if kind == "tile":
        return pl.BlockSpec((tm, tn), lambda i, j, k: (i, j))
    if kind == "row":
        return pl.BlockSpec((1, tn), lambda i, j, k: (0, j))
    raise ValueError(kind)


def _rows_call(name, body, n_rows, ts, ins, outs, scratch=(), scratch_bytes=0):
    blk = sum(_nbytes(s.block_shape, a.dtype) for a, s in list(ins) + list(outs))
    return pl.pallas_call(
        body,
        name=name,
        grid=(n_rows // ts,),
        in_specs=[s for _, s in ins],
        out_specs=[s for _, s in outs],
        out_shape=[o for o, _ in outs],
        scratch_shapes=list(scratch),
        compiler_params=_params(("arbitrary",), blk, scratch_bytes + 4 * blk // 2),
    )(*[a for a, _ in ins])


def _blk(ts, d):
    return pl.BlockSpec((ts, d), lambda i: (i, 0))


def _full(shape):
    return pl.BlockSpec(shape, lambda i: tuple(0 for _ in shape))


def _rowsum8(v):
    t, d = v.shape
    return jnp.sum(v.reshape(t // SUBLANES, SUBLANES, d), axis=0)


def _accumulate(ref, val):
    @pl.when(pl.program_id(0) == 0)
    def _():
        ref[...] = val

    @pl.when(pl.program_id(0) > 0)
    def _():
        ref[...] += val


def _rstd(x):
    return lax.rsqrt(jnp.mean(x * x, axis=-1, keepdims=True) + EPS)


def _rms_bwd_math(dy, x, g):
    r = _rstd(x)
    gdy = dy * g
    c = jnp.mean(gdy * x, axis=-1, keepdims=True)
    dx = r * gdy - x * (r * r * r * c)
    return dx, dy * (x * r)


def rms_fwd(name, h, g, ts=512):
    s, d = h.shape
    ts = min(ts, s)

    def body(h_ref, g_ref, u_ref):
        x = h_ref[...]
        u_ref[...] = ((x * _rstd(x)) * g_ref[...]).astype(BF16)

    return _rows_call(name, body, s, ts, [(h, _blk(ts, d)), (g, _full((1, d)))],
                      [(jax.ShapeDtypeStruct((s, d), BF16), _blk(ts, d))])[0]


def rms_bwd(name, du, h, g, dres, ts=256):
    s, d = h.shape
    ts = min(ts, s)

    def body(du_ref, h_ref, g_ref, dres_ref, dh_ref, dhb_ref, dg_ref, cs_ref):
        dx, dg = _rms_bwd_math(du_ref[...], h_ref[...], g_ref[...])
        dh = dres_ref[...] + dx
        dh_ref[...] = dh
        dhb_ref[...] = dh.astype(BF16)
        _accumulate(dg_ref, _rowsum8(dg))
        _accumulate(cs_ref, _rowsum8(dh))

    return _rows_call(
        name, body, s, ts,
        [(du, _blk(ts, d)), (h, _blk(ts, d)), (g, _full((1, d))), (dres, _blk(ts, d))],
        [(jax.ShapeDtypeStruct((s, d), F32), _blk(ts, d)), (jax.ShapeDtypeStruct((s, d), BF16), _blk(ts, d)),
         (jax.ShapeDtypeStruct((SUBLANES, d), F32), _full((SUBLANES, d))),
         (jax.ShapeDtypeStruct((SUBLANES, d), F32), _full((SUBLANES, d)))])


def final_loss(name, h, g, target, ts=256):
    s, d = h.shape
    ts = min(ts, s)

    def body(h_ref, g_ref, t_ref, loss_ref, dh_ref, dg_ref):
        x = h_ref[...]
        gf = g_ref[...]
        y = (x * _rstd(x)) * gf
        err = y - t_ref[...]
        _accumulate(loss_ref, _rowsum8(err * err) * (0.5 / d))
        dx, dg = _rms_bwd_math(err * (1.0 / d), x, gf)
        dh_ref[...] = dx
        _accumulate(dg_ref, _rowsum8(dg))

    return _rows_call(
        name, body, s, ts,
        [(h, _blk(ts, d)), (g, _full((1, d))), (target, _blk(ts, d))],
        [(jax.ShapeDtypeStruct((SUBLANES, d), F32), _full((SUBLANES, d))),
         (jax.ShapeDtypeStruct((s, d), F32), _blk(ts, d)),
         (jax.ShapeDtypeStruct((SUBLANES, d), F32), _full((SUBLANES, d)))])


def ple_bwd_elem(name, dh, g, e, ts=512):
    s, d = dh.shape
    ts = min(ts, s)

    def body(dh_ref, g_ref, e_ref, de_ref, dgl_ref):
        dh_v, g_v = dh_ref[...], g_ref[...]
        de_ref[...] = (dh_v * g_v).astype(BF16)
        dgl_ref[...] = (dh_v * e_ref[...] * (g_v * (1.0 - g_v))).astype(BF16)

    return _rows_call(name, body, s, ts, [(dh, _blk(ts, d)), (g, _blk(ts, d)), (e, _blk(ts, d))],
                      [(jax.ShapeDtypeStruct((s, d), BF16), _blk(ts, d)),
                       (jax.ShapeDtypeStruct((s, d), BF16), _blk(ts, d))])


def _chunks(n_rows, d, fn):
    lc = min(256, d)
    rc = min(64, n_rows)

    def lane_body(c, carry):
        lanes = pl.ds(pl.multiple_of(c * lc, lc), lc)
        for r in range(n_rows // rc):
            fn(r * rc, rc, lanes)
        return carry

    lax.fori_loop(0, d // lc, lane_body, 0)


def _prev_halo_spec(ts, halo, width):
    per = ts // halo
    return pl.BlockSpec((halo, width), lambda i: (jnp.maximum(i * per - 1, 0), 0))


def _next_halo_spec(ts, halo, width, n_rows):
    per = ts // halo
    last = n_rows // halo - 1
    return pl.BlockSpec((halo, width), lambda i: (jnp.minimum((i + 1) * per, last), 0))


def cf_fwd_mid(name, a, w_dw, b_dw, gn, ts=256):
    s, d2 = a.shape
    d = d2 // 2
    ts = min(ts, s)
    hl = CONV_A_HALO
    off = hl - (CONV_A_TAPS - 1)

    def body(a_ref, ah_ref, w_ref, b_ref, gn_ref, v0_ref, v1_ref, v3_ref, buf):
        first = pl.program_id(0) == 0
        halo = ah_ref[...]
        hv0 = halo[:, :d] * jax.nn.sigmoid(halo[:, d:])
        buf[pl.ds(0, hl), :] = jnp.where(first, 0.0, hv0)
        main = a_ref[...]
        v0 = main[:, :d] * jax.nn.sigmoid(main[:, d:])
        buf[pl.ds(hl, ts), :] = v0
        v0_ref[...] = v0

        def conv(r0, rc, lanes):
            acc = jnp.zeros((rc, lanes.size), F32)
            for k in range(CONV_A_TAPS):
                acc = acc + w_ref[pl.ds(k, 1), lanes] * buf[pl.ds(r0 + off + k, rc), lanes]
            v1_ref[pl.ds(r0, rc), lanes] = acc + b_ref[:, lanes]

        _chunks(ts, d, conv)
        v1 = v1_ref[...]
        v2 = (v1 * _rstd(v1)) * gn_ref[...]
        v3_ref[...] = (v2 * jax.nn.sigmoid(v2)).astype(BF16)

    return _rows_call(
        name, body, s, ts,
        [(a, _blk(ts, d2)), (a, _prev_halo_spec(ts, hl, d2)), (w_dw, _full(w_dw.shape)),
         (b_dw, _full((1, d))), (gn, _full((1, d)))],
        [(jax.ShapeDtypeStruct((s, d), F32), _blk(ts, d)), (jax.ShapeDtypeStruct((s, d), F32), _blk(ts, d)),
         (jax.ShapeDtypeStruct((s, d), BF16), _blk(ts, d))],
        scratch=[pltpu.VMEM((hl + ts, d), F32)], scratch_bytes=_nbytes((hl + ts, d), F32))


def cf_bwd_rows(name, dv3, v1, gn, ts=256):
    s, d = v1.shape
    ts = min(ts, s)

    def body(dv3_ref, v1_ref, gn_ref, dv1_ref, dgn_ref, db_ref):
        v1 = v1_ref[...]
        gn_v = gn_ref[...]
        v2 = (v1 * _rstd(v1)) * gn_v
        sg = jax.nn.sigmoid(v2)
        dv2 = dv3_ref[...] * (sg * (1.0 + v2 * (1.0 - sg)))
        dv1, dgn = _rms_bwd_math(dv2, v1, gn_v)
        dv1_ref[...] = dv1
        _accumulate(dgn_ref, _rowsum8(dgn))
        _accumulate(db_ref, _rowsum8(dv1))

    return _rows_call(
        name, body, s, ts, [(dv3, _blk(ts, d)), (v1, _blk(ts, d)), (gn, _full((1, d)))],
        [(jax.ShapeDtypeStruct((s, d), F32), _blk(ts, d)),
         (jax.ShapeDtypeStruct((SUBLANES, d), F32), _full((SUBLANES, d))),
         (jax.ShapeDtypeStruct((SUBLANES, d), F32), _full((SUBLANES, d)))])


def cf_bwd_conv(name, dv1, v0, a, w_dw, ts=256):
    s, d = dv1.shape
    ts = min(ts, s)
    hl = CONV_A_HALO
    taps = CONV_A_TAPS
    off = hl - (taps - 1)
    last_blk = s // ts - 1

    def body(dv1_ref, dv1n_ref, v0_ref, v0p_ref, a_ref, w_ref, da_ref, dw_ref, db_ref, dbuf, vbuf, dv0_buf, dw_acc):
        i = pl.program_id(0)
        dbuf[pl.ds(0, ts), :] = dv1_ref[...]
        dbuf[pl.ds(ts, hl), :] = jnp.where(i == last_blk, 0.0, dv1n_ref[...])
        vbuf[pl.ds(0, hl), :] = jnp.where(i == 0, 0.0, v0p_ref[...])
        vbuf[pl.ds(hl, ts), :] = v0_ref[...]

        @pl.when(i == 0)
        def _():
            dw_acc[...] = jnp.zeros_like(dw_acc)

        def conv_t(r0, rc, lanes):
            g = dbuf[pl.ds(r0, rc), lanes]
            acc = jnp.zeros((rc, lanes.size), F32)
            for k in range(taps):
                acc = acc + w_ref[pl.ds(k, 1), lanes] * dbuf[pl.ds(r0 + taps - 1 - k, rc), lanes]
                prod = g * vbuf[pl.ds(r0 + off + k, rc), lanes]
                dw_acc[pl.ds(k * SUBLANES, SUBLANES), lanes] += _rowsum8(prod)
            dv0_buf[pl.ds(r0, rc), lanes] = acc

        _chunks(ts, d, conv_t)
        dv0 = dv0_buf[...]
        av = a_ref[...]
        val, sg = av[:, :d], jax.nn.sigmoid(av[:, d:])
        dval = dv0 * sg
        dgate = dv0 * val * (sg * (1.0 - sg))
        da_ref[:, :d] = dval.astype(BF16)
        da_ref[:, d:] = dgate.astype(BF16)
        _accumulate(db_ref.at[:, pl.ds(0, d)], _rowsum8(dval))
        _accumulate(db_ref.at[:, pl.ds(d, d)], _rowsum8(dgate))

        @pl.when(i == last_blk)
        def _():
            dw_ref[...] = jnp.sum(dw_acc[...].reshape(hl, SUBLANES, d), axis=1)

    scratch = [pltpu.VMEM((ts + hl, d), F32), pltpu.VMEM((hl + ts, d), F32), pltpu.VMEM((ts, d), F32),
               pltpu.VMEM((hl * SUBLANES, d), F32)]
    sbytes = _nbytes((3 * ts + 2 * hl + hl * SUBLANES, d), F32)
    return _rows_call(
        name, body, s, ts,
        [(dv1, _blk(ts, d)), (dv1, _next_halo_spec(ts, hl, d, s)), (v0, _blk(ts, d)), (v0, _prev_halo_spec(ts, hl, d)),
         (a, _blk(ts, 2 * d)), (w_dw, _full(w_dw.shape))],
        [(jax.ShapeDtypeStruct((s, 2 * d), BF16), _blk(ts, 2 * d)),
         (jax.ShapeDtypeStruct((hl, d), F32), _full((hl, d))),
         (jax.ShapeDtypeStruct((SUBLANES, 2 * d), F32), _full((SUBLANES, 2 * d)))],
        scratch=scratch, scratch_bytes=sbytes)


def sc_fwd_mid(name, bcv, w_conv, ts=256):
    s, d3 = bcv.shape
    d = d3 // 3
    ts = min(ts, s)
    hl = CONV_B_HALO
    off = hl - (CONV_B_TAPS - 1)

    def body(x_ref, xp_ref, w_ref, y_ref, buf):
        hp = xp_ref[...]
        buf[pl.ds(0, hl), :] = jnp.where(pl.program_id(0) == 0, 0.0, hp[:, d:2 * d] * hp[:, 2 * d:])
        buf[pl.ds(hl, ts), :] = x_ref[:, d:2 * d] * x_ref[:, 2 * d:]
        cc = jnp.zeros((ts, d), F32)
        for k in range(CONV_B_TAPS):
            cc = cc + w_ref[pl.ds(k, 1), :] * buf[pl.ds(off + k, ts), :]
        y_ref[...] = (x_ref[:, :d] * cc).astype(BF16)

    return _rows_call(
        name, body, s, ts,
        [(bcv, _blk(ts, d3)), (bcv, _prev_halo_spec(ts, hl, d3)), (w_conv, _full(w_conv.shape))],
        [(jax.ShapeDtypeStruct((s, d), BF16), _blk(ts, d))],
        scratch=[pltpu.VMEM((hl + ts, d), F32)], scratch_bytes=_nbytes((hl + ts, d), F32))[0]


def sc_bwd_mid(name, dy, bcv, w_conv, ts=256):
    s, d3 = bcv.shape
    d = d3 // 3
    ts = min(ts, s)
    hl = CONV_B_HALO
    taps = CONV_B_TAPS
    off = hl - (taps - 1)
    last_blk = s // ts - 1

    def body(dy_ref, dyn_ref, x_ref, xp_ref, xn_ref, w_ref, dx_ref, dw_ref, cvbuf, dbuf, dw_acc):
        i = pl.program_id(0)
        hp = xp_ref[...]
        cvbuf[pl.ds(0, hl), :] = jnp.where(i == 0, 0.0, hp[:, d:2 * d] * hp[:, 2 * d:])
        gb, gc, v = x_ref[:, :d], x_ref[:, d:2 * d], x_ref[:, 2 * d:]
        cvbuf[pl.ds(hl, ts), :] = gc * v
        dy_v = dy_ref[...]
        dcc = dy_v * gb
        dbuf[pl.ds(0, ts), :] = dcc
        dbuf[pl.ds(ts, hl), :] = jnp.where(i == last_blk, 0.0, dyn_ref[...] * xn_ref[:, :d])

        @pl.when(i == 0)
        def _():
            dw_acc[...] = jnp.zeros_like(dw_acc)

        cc = jnp.zeros((ts, d), F32)
        dcv = jnp.zeros((ts, d), F32)
        for k in range(taps):
            win = cvbuf[pl.ds(off + k, ts), :]
            cc = cc + w_ref[pl.ds(k, 1), :] * win
            dcv = dcv + w_ref[pl.ds(k, 1), :] * dbuf[pl.ds(taps - 1 - k, ts), :]
            dw_acc[pl.ds(k * SUBLANES, SUBLANES), :] += _rowsum8(dcc * win)
        dx_ref[:, :d] = (dy_v * cc).astype(BF16)
        dx_ref[:, d:2 * d] = (dcv * v).astype(BF16)
        dx_ref[:, 2 * d:] = (dcv * gc).astype(BF16)

        @pl.when(i == last_blk)
        def _():
            dw_ref[...] = jnp.sum(dw_acc[...].reshape(hl, SUBLANES, d), axis=1)

    scratch = [pltpu.VMEM((hl + ts, d), F32), pltpu.VMEM((ts + hl, d), F32), pltpu.VMEM((hl * SUBLANES, d), F32)]
    sbytes = _nbytes((2 * ts + 2 * hl + hl * SUBLANES, d), F32)
    return _rows_call(
        name, body, s, ts,
        [(dy, _blk(ts, d)), (dy, _next_halo_spec(ts, hl, d, s)), (bcv, _blk(ts, d3)), (bcv, _prev_halo_spec(ts, hl, d3)),
         (bcv, _next_halo_spec(ts, hl, d3, s)), (w_conv, _full(w_conv.shape))],
        [(jax.ShapeDtypeStruct((s, d3), BF16), _blk(ts, d3)), (jax.ShapeDtypeStruct((hl, d), F32), _full((hl, d)))],
        scratch=scratch, scratch_bytes=sbytes)


def _row(a, i):
    return lax.slice_in_dim(a, i, i + 1, axis=0)


def _local_step(x, p, target, small, get_w, conv_w, put_grads):
    depth = p.shape[0]
    acts = []
    h = x
    for i in range(depth):
        j = i // 2
        act = {"h": h}
        u = rms_fwd(f"rms_mix_{i}", h, _row(small["norm_mix"], i))
        act["u"] = u
        if i % 2 == 0:
            a = mm_x_wcol(f"cf_pw1_{i}", u, get_w("cf_w_pw1", j, u), 0, extras=[(_row(small["cf_b_pw1"], j), "row")],
                          epi=lambda acc, b: (acc + b,))[0]
            v0, v1, v3 = cf_fwd_mid(f"cf_mid_{i}", a, conv_w["cf"][j], _row(small["cf_b_dw"], j), _row(small["cf_norm"], j))
            act.update(a=a, v0=v0, v1=v1, v3=v3)
            h1 = mm_x_wrow(f"cf_pw2_{i}", v3, get_w("cf_w_pw2", j, v3),
                           extras=[(_row(small["cf_b_pw2"], j), "row"), (h, "tile")],
                           epi=lambda acc, b, res: (res + (acc + b),))[0]
        else:
            bcv = mm_x_wcol(f"sc_in_{i}", u, get_w("sc_w_in", j, u), 0, tn=768)[0]
            y = sc_fwd_mid(f"sc_mid_{i}", bcv, conv_w["sc"][j])
            act.update(bcv=bcv, y=y)
            h1 = mm_x_wrow(f"sc_out_{i}", y, get_w("sc_w_out", j, y), extras=[(h, "tile")],
                           epi=lambda acc, res: (res + acc,))[0]
        act["h1"] = h1
        u2 = rms_fwd(f"rms_mlp_{i}", h1, _row(small["norm_mlp"], i))
        z, hd = mm_x_wcol(f"mlp_w1_{i}", u2, get_w("mlp_w1", i, u2), 0, outs_dtypes=(F32, BF16),
                          epi=lambda acc: (acc, jnp.square(jnp.maximum(acc, 0.0))))
        h2 = mm_x_wrow(f"mlp_w2_{i}", hd, get_w("mlp_w2", i, hd), extras=[(h1, "tile")],
                       epi=lambda acc, res: (res + acc,), tn=1024)[0]
        act.update(u2=u2, z=z, hd=hd, h2=h2)
        n3 = rms_fwd(f"rms_ple_{i}", h2, _row(small["norm_ple"], i))
        e = mm_x_wcol(f"ple_proj_{i}", p[i], get_w("ple_w_proj", i, n3), 0)[0]

        def ple_epi(acc, e_t, res):
            g_t = jax.nn.sigmoid(acc)
            return g_t, res + g_t * e_t

        g, h3 = mm_x_wrow(f"ple_gate_{i}", n3, get_w("ple_w_gate", i, e), extras=[(e, "tile"), (h2, "tile")],
                          outs_dtypes=(F32, F32), epi=ple_epi)
        act.update(n3=n3, e=e, g=g)
        acts.append(act)
        h = h3

    loss_part, dh, dg_final = final_loss("final_loss", h, small["norm_final"], target)
    sg = {k: [None] * small[k].shape[0] for k in small if k != "norm_final"}
    sg["norm_final"] = [dg_final]
    sg["cf_w_dw"] = [None] * conv_w["cf"].shape[0]
    sg["sc_w_conv"] = [None] * conv_w["sc"].shape[0]

    for i in reversed(range(depth)):
        j = i // 2
        act = acts[i]
        de, dgl = ple_bwd_elem(f"ple_bwd_{i}", dh, act["g"], act["e"])
        g_proj = mm_xt_dy(f"d_ple_proj_{i}", p[i], de, True)
        g_gate = mm_xt_dy(f"d_ple_gate_{i}", act["n3"], dgl, False)
        dn3 = mm_dy_wrow_t(f"dn3_{i}", dgl, get_w("ple_w_gate", i))[0]
        dh2, dh2b, sg["norm_ple"][i], _ = rms_bwd(f"rms_ple_bwd_{i}", dn3, act["h2"], _row(small["norm_ple"], i), dh)
        g_w2 = mm_xt_dy(f"d_mlp_w2_{i}", act["hd"], dh2b, False)
        dz = mm_dy_wrow_t(f"dz_{i}", dh2b, get_w("mlp_w2", i), extras=[(act["z"], "tile")], outs_dtypes=(BF16,),
                          epi=lambda acc, z_t: (acc * (2.0 * jnp.maximum(z_t, 0.0)),))[0]
        g_w1 = mm_xt_dy(f"d_mlp_w1_{i}", act["u2"], dz, True)
        put_grads({("ple_w_proj", i): g_proj, ("ple_w_gate", i): g_gate, ("mlp_w2", i): g_w2, ("mlp_w1", i): g_w1})
        du2 = mm_dy_wcol_t(f"du2_{i}", dz, get_w("mlp_w1", i), 0)[0]
        dh1, dh1b, sg["norm_mlp"][i], cs1 = rms_bwd(f"rms_mlp_bwd_{i}", du2, act["h1"], _row(small["norm_mlp"], i), dh2)
        if i % 2 == 0:
            g_out = mm_xt_dy(f"d_cf_pw2_{i}", act["v3"], dh1b, False)
            sg["cf_b_pw2"][j] = cs1
            dv3 = mm_dy_wrow_t(f"dv3_{i}", dh1b, get_w("cf_w_pw2", j))[0]
            dv1, sg["cf_norm"][j], sg["cf_b_dw"][j] = cf_bwd_rows(f"cf_bwd_rows_{i}", dv3, act["v1"], _row(small["cf_norm"], j))
            da, sg["cf_w_dw"][j], sg["cf_b_pw1"][j] = cf_bwd_conv(f"cf_bwd_conv_{i}", dv1, act["v0"], act["a"], conv_w["cf"][j])
            g_in = mm_xt_dy(f"d_cf_pw1_{i}", act["u"], da, True)
            put_grads({("cf_w_pw2", j): g_out, ("cf_w_pw1", j): g_in})
            du = mm_dy_wcol_t(f"du_{i}", da, get_w("cf_w_pw1", j), 0)[0]
        else:
            g_out = mm_xt_dy(f"d_sc_out_{i}", act["y"], dh1b, False)
            dy = mm_dy_wrow_t(f"dy_{i}", dh1b, get_w("sc_w_out", j))[0]
            dbcv, sg["sc_w_conv"][j] = sc_bwd_mid(f"sc_bwd_mid_{i}", dy, act["bcv"], conv_w["sc"][j])
            g_in = mm_xt_dy(f"d_sc_in_{i}", act["u"], dbcv, True, tn=768)
            put_grads({("sc_w_out", j): g_out, ("sc_w_in", j): g_in})
            du = mm_dy_wcol_t(f"du_{i}", dbcv, get_w("sc_w_in", j), 0)[0]
        dh, _, sg["norm_mix"][i], _ = rms_bwd(f"rms_mix_bwd_{i}", du, act["h"], _row(small["norm_mix"], i), dh1)
    return loss_part, dh, sg


def _me_and_peers():
    x, y, c = lax.axis_index("x"), lax.axis_index("y"), lax.axis_index("c")
    me = 4 * x + 2 * y + c
    peers = []
    for q in range(1, NDEV):
        px = 1 - x if q & 4 else x
        py = 1 - y if q & 2 else y
        pc = 1 - c if q & 1 else c
        peers.append(((px, py, pc), 4 * px + 2 * py + pc))
    return me, peers


def _exchange(name, srcs, out_shapes, src_fns, dst_fns):
    n = len(srcs)

    def body(*refs):
        ins, outs = refs[:n], refs[n:2 * n]
        send_sems, recv_sems, local_sems = refs[2 * n:]
        me, peers = _me_and_peers()
        local, remote = [], []
        for k in range(n):
            cp = pltpu.make_async_copy(src_fns[k](ins[k], me), dst_fns[k](outs[k], me), local_sems.at[k])
            cp.start()
            local.append(cp)
        for q, (peer, peer_blk) in enumerate(peers):
            for k in range(n):
                cp = pltpu.make_async_remote_copy(
                    src_ref=src_fns[k](ins[k], peer_blk), dst_ref=dst_fns[k](outs[k], me),
                    send_sem=send_sems.at[k, q], recv_sem=recv_sems.at[k, q],
                    device_id=peer, device_id_type=MESH)
                cp.start()
                remote.append(cp)
        for q, (peer, peer_blk) in enumerate(peers):
            for k in range(n):
                pltpu.make_async_remote_copy(
                    src_ref=src_fns[k](ins[k], peer_blk), dst_ref=dst_fns[k](outs[k], peer_blk),
                    send_sem=send_sems.at[k, q], recv_sem=recv_sems.at[k, q],
                    device_id=peer, device_id_type=MESH).wait_recv()
        for cp in remote:
            cp.wait_send()
        for cp in local:
            cp.wait()

    any_spec = pl.BlockSpec(memory_space=pl.ANY)
    return pl.pallas_call(
        body,
        name=name,
        in_specs=[any_spec] * n,
        out_specs=[any_spec] * n,
        out_shape=out_shapes,
        scratch_shapes=[pltpu.SemaphoreType.DMA((n, N_PEERS)), pltpu.SemaphoreType.DMA((n, N_PEERS)),
                        pltpu.SemaphoreType.DMA((n,))],
    )(*srcs)


def sc_exchange(name, collective_id, srcs, out_shapes, src_fns, dst_fns):
    n = len(srcs)

    def body(*refs):
        ins, outs = refs[:n], refs[n:2 * n]
        send_sems, recv_sems, local_sems = refs[2 * n:]
        me, peers = _me_and_peers()
        barrier = pltpu.get_barrier_semaphore()
        for peer, _ in peers:
            pl.semaphore_signal(barrier, inc=1, device_id=peer, device_id_type=MESH)
        pl.semaphore_wait(barrier, N_PEERS)
        local, remote = [], []
        for k in range(n):
            cp = pltpu.make_async_copy(src_fns[k](ins[k], me), dst_fns[k](outs[k], me), local_sems.at[k])
            cp.start()
            local.append(cp)
        for q, (peer, peer_blk) in enumerate(peers):
            for k in range(n):
                cp = pltpu.make_async_remote_copy(
                    src_ref=src_fns[k](ins[k], peer_blk), dst_ref=dst_fns[k](outs[k], me),
                    send_sem=send_sems.at[k, q], recv_sem=recv_sems.at[k, q],
                    device_id=peer, device_id_type=MESH)
                cp.start()
                remote.append(cp)
        for q, (peer, peer_blk) in enumerate(peers):
            for k in range(n):
                pltpu.make_async_remote_copy(
                    src_ref=src_fns[k](ins[k], peer_blk), dst_ref=dst_fns[k](outs[k], peer_blk),
                    send_sem=send_sems.at[k, q], recv_sem=recv_sems.at[k, q],
                    device_id=peer, device_id_type=MESH).wait_recv()
        for cp in remote:
            cp.wait_send()
        for cp in local:
            cp.wait()

    return pl.kernel(
        body,
        out_type=out_shapes,
        mesh=plsc.ScalarSubcoreMesh(axis_name="sequencer", num_cores=1),
        name=name,
        scratch_types=[pltpu.SemaphoreType.DMA((n, N_PEERS)), pltpu.SemaphoreType.DMA((n, N_PEERS)),
                       pltpu.SemaphoreType.DMA((n,))],
        compiler_params=pltpu.CompilerParams(collective_id=collective_id),
    )(*srcs)


def sc_gather(name, collective_id, srcs, layers):
    n = len(srcs)
    outs_shape = [jax.ShapeDtypeStruct((NDEV, 1) + a.shape[1:], a.dtype) for a in srcs]

    def body(*refs):
        ins, outs = refs[:n], refs[n:2 * n]
        send_sems, recv_sems, local_sems = refs[2 * n:]
        x, y, c = lax.axis_index("x"), lax.axis_index("y"), lax.axis_index("c")
        me = 4 * x + 2 * y + c
        sibling = (x, y, 1 - c)
        chips = [(1 - x, y), (x, 1 - y), (1 - x, 1 - y)]
        barrier = pltpu.get_barrier_semaphore()
        for peer in [sibling] + [(cx, cy, c) for cx, cy in chips]:
            pl.semaphore_signal(barrier, inc=1, device_id=peer, device_id_type=MESH)
        pl.semaphore_wait(barrier, 1 + len(chips))

        def copy(k, slot, blk, to, src=None):
            place = outs[k].at[blk, 0]
            return pltpu.make_async_remote_copy(
                src_ref=place if src is None else src, dst_ref=place,
                send_sem=send_sems.at[k, slot], recv_sem=recv_sems.at[k, slot],
                device_id=to, device_id_type=MESH)

        local, sent = [], []
        for k in range(n):
            mine = ins[k].at[layers[k]]
            cp = pltpu.make_async_copy(mine, outs[k].at[me, 0], local_sems.at[k])
            cp.start()
            local.append(cp)
            sent.append(copy(k, 0, me, sibling, src=mine))
            sent += [copy(k, 1 + j, me, (cx, cy, c), src=mine) for j, (cx, cy) in enumerate(chips)]
        for cp in sent:
            cp.start()
        for k in range(n):
            for j, (cx, cy) in enumerate(chips):
                blk = 4 * cx + 2 * cy + c
                copy(k, 1 + j, blk, sibling).wait_recv()
                fwd = copy(k, 4 + j, blk, sibling)
                fwd.start()
                sent.append(fwd)
        for k in range(n):
            copy(k, 0, 4 * x + 2 * y + (1 - c), sibling).wait_recv()
            for j, (cx, cy) in enumerate(chips):
                copy(k, 4 + j, 4 * cx + 2 * cy + (1 - c), sibling).wait_recv()
        for cp in sent:
            cp.wait_send()
        for cp in local:
            cp.wait()

    return pl.kernel(
        body,
        out_type=outs_shape,
        mesh=plsc.ScalarSubcoreMesh(axis_name="sequencer", num_cores=1),
        name=name,
        scratch_types=[pltpu.SemaphoreType.DMA((n, N_PEERS)), pltpu.SemaphoreType.DMA((n, N_PEERS)),
                       pltpu.SemaphoreType.DMA((n,))],
        compiler_params=pltpu.CompilerParams(collective_id=collective_id),
    )(*srcs)


def all_gather(name, items):
    outs = [jax.ShapeDtypeStruct((NDEV, 1) + a.shape[1:], a.dtype) for a, _ in items]
    src_fns = [functools.partial(lambda ref, blk, layer: ref.at[layer], layer=l) for _, l in items]
    dst_fns = [lambda ref, blk: ref.at[blk, 0]] * len(items)
    return _exchange(name, [a for a, _ in items], outs, src_fns, dst_fns)


def _gather_src(layer):
    return lambda ref, blk: ref.at[layer]


def _gather_dst(ref, blk):
    return ref.at[blk, 0]


def _slice_of(ref, blk):
    return ref.at[blk]


def cast_bf16(name, w, tr_elems=512 * 1024):
    l, r, c = w.shape
    tr = _row_tile(r, tr_elems // c)
    spec = pl.BlockSpec((None, tr, c), lambda li, i: (li, i, 0))

    def body(w_ref, o_ref):
        o_ref[...] = w_ref[...].astype(BF16)

    return pl.pallas_call(
        body, name=name, grid=(l, r // tr), in_specs=[spec], out_specs=spec,
        out_shape=jax.ShapeDtypeStruct(w.shape, BF16),
        compiler_params=_params(("parallel", "parallel"), 6 * tr * c),
    )(w)


def _adamw_math(w, g, m, v):
    m = ADAM_B1 * m + (1.0 - ADAM_B1) * g
    v = ADAM_B2 * v + (1.0 - ADAM_B2) * (g * g)
    m_hat = m / (1.0 - ADAM_B1 ** ADAM_STEP)
    v_hat = v / (1.0 - ADAM_B2 ** ADAM_STEP)
    delta = -ADAM_LR * (m_hat / (jnp.sqrt(v_hat) + ADAM_EPS) + ADAM_WD * w)
    return delta, m, v


def _sum_blocks(ref):
    g = ref[0].astype(F32)
    for d in range(1, ref.shape[0]):
        g = g + ref[d].astype(F32)
    return g


def adamw_layer(name, recv, w, m, v, layer, stacked, tr_elems=128 * 1024):
    nd, r, c = recv.shape
    tr = _row_tile(r, tr_elems // c)
    r_spec = pl.BlockSpec((nd, tr, c), lambda i: (0, i, 0))
    w_spec = pl.BlockSpec((None, tr, c), lambda i: (layer, i, 0))
    if stacked is None:
        stacked = [lax.empty(w.shape, F32) for _ in range(4)]

    def body(r_ref, w_ref, m_ref, v_ref, g_in, d_in, m_in, v_in, g_out, d_out, m_out, v_out):
        g = _sum_blocks(r_ref)
        delta, m_new, v_new = _adamw_math(w_ref[...], g, m_ref[...], v_ref[...])
        g_out[...] = g
        d_out[...] = delta
        m_out[...] = m_new
        v_out[...] = v_new

    out = jax.ShapeDtypeStruct(w.shape, F32)
    return pl.pallas_call(
        body, name=name, grid=(r // tr,),
        in_specs=[r_spec, w_spec, w_spec, w_spec] + [pl.BlockSpec(memory_space=pl.ANY)] * 4,
        out_specs=[w_spec] * 4, out_shape=[out] * 4,
        input_output_aliases={4: 0, 5: 1, 6: 2, 7: 3},
        compiler_params=_params(("parallel",), tr * c * (2 * nd + 7 * 4)),
    )(recv, w, m, v, *stacked)


def small_update(name, part_g, tap_g, w_a, m_a, v_a, w_b, m_b, v_b):
    nd, rows, d = part_g.shape
    na = rows // SUBLANES
    nb, cb = w_b.shape

    def body(pg_ref, tg_ref, wa_ref, ma_ref, va_ref, wb_ref, mb_ref, vb_ref,
             ga_out, da_out, ma_out, va_out, gb_out, db_out, mb_out, vb_out, loss_out):
        ga = jnp.sum(_sum_blocks(pg_ref).reshape(na, SUBLANES, d), axis=1)
        delta, m_new, v_new = _adamw_math(wa_ref[...], ga, ma_ref[...], va_ref[...])
        ga_out[...] = ga
        da_out[...] = delta
        ma_out[...] = m_new
        va_out[...] = v_new
        loss_out[...] = jnp.broadcast_to(jnp.sum(ga[na - 1:na, :], axis=1, keepdims=True), loss_out.shape)
        gb = _sum_blocks(tg_ref)
        delta, m_new, v_new = _adamw_math(wb_ref[...], gb, mb_ref[...], vb_ref[...])
        gb_out[...] = gb
        db_out[...] = delta
        mb_out[...] = m_new
        vb_out[...] = v_new

    oa, ob = jax.ShapeDtypeStruct((na, d), F32), jax.ShapeDtypeStruct((nb, cb), F32)
    vm = pl.BlockSpec(memory_space=pltpu.VMEM)
    return pl.pallas_call(
        body, name=name, in_specs=[vm] * 8, out_specs=[vm] * 9,
        out_shape=[oa] * 4 + [ob] * 4 + [jax.ShapeDtypeStruct((1, LANES), F32)],
        compiler_params=pltpu.CompilerParams(vmem_limit_bytes=_vmem_limit(_nbytes(part_g.shape, F32))),
    )(part_g, tap_g, w_a, m_a, v_a, w_b, m_b, v_b)


BIG = ("cf_w_pw1", "cf_w_pw2", "sc_w_in", "sc_w_out", "mlp_w1", "mlp_w2", "ple_w_proj", "ple_w_gate")
COL_SHARDED = ("cf_w_pw1", "sc_w_in", "mlp_w1", "ple_w_proj")
WEIGHT_ORDER = ("norm_mix", "norm_mlp", "norm_ple", "cf_w_pw1", "cf_b_pw1", "cf_w_dw", "cf_b_dw", "cf_norm",
                "cf_w_pw2", "cf_b_pw2", "sc_w_in", "sc_w_conv", "sc_w_out", "mlp_w1", "mlp_w2", "ple_w_proj",
                "ple_w_gate", "norm_final")
SMALL_ROWS = (("norm_mix", 4), ("norm_mlp", 4), ("norm_ple", 4), ("cf_b_pw1", 4), ("cf_b_dw", 2), ("cf_norm", 2),
              ("cf_b_pw2", 2), ("norm_final", 1))


def _layer_weights(i):
    mixer = (("cf_w_pw1", i // 2), ("cf_w_pw2", i // 2)) if i % 2 == 0 else (("sc_w_in", i // 2), ("sc_w_out", i // 2))
    return mixer + (("mlp_w1", i), ("mlp_w2", i), ("ple_w_proj", i), ("ple_w_gate", i))


def _pad_rows(a, rows):
    return jnp.pad(a, ((0, 0), (0, rows - a.shape[1]), (0, 0)))


def _pack_taps(cf, sc):
    c = cf.shape[2]
    return jnp.concatenate([_pad_rows(cf, CONV_A_HALO).reshape(-1, c), _pad_rows(sc, CONV_B_HALO).reshape(-1, c)], axis=0)


def _unpack_taps(t, n_cf):
    c = t.shape[1]
    cf = t[:n_cf * CONV_A_HALO].reshape(n_cf, CONV_A_HALO, c)[:, :CONV_A_TAPS]
    sc = t[n_cf * CONV_A_HALO:].reshape(-1, CONV_B_HALO, c)[:, :CONV_B_TAPS]
    return cf, sc


def _pack_small(vals, d):
    return jnp.concatenate([vals[k].reshape(-1, d) for k, _ in SMALL_ROWS] + [jnp.zeros((1, d), F32)], axis=0)


def _unpack_small(a, shapes):
    out, r = {}, 0
    for k, n in SMALL_ROWS:
        out[k] = a[r:r + n].reshape(shapes[k])
        r += n
    return out


def kernel(x, p, norm_mix, norm_mlp, norm_ple, cf_w_pw1, cf_b_pw1, cf_w_dw, cf_b_dw, cf_norm, cf_w_pw2, cf_b_pw2, sc_w_in, sc_w_conv, sc_w_out, mlp_w1, mlp_w2, ple_w_proj, ple_w_gate, norm_final, loss_target, m_norm_mix, m_norm_mlp, m_norm_ple, m_cf_w_pw1, m_cf_b_pw1, m_cf_w_dw, m_cf_b_dw, m_cf_norm, m_cf_w_pw2, m_cf_b_pw2, m_sc_w_in, m_sc_w_conv, m_sc_w_out, m_mlp_w1, m_mlp_w2, m_ple_w_proj, m_ple_w_gate, m_norm_final, v_norm_mix, v_norm_mlp, v_norm_ple, v_cf_w_pw1, v_cf_b_pw1, v_cf_w_dw, v_cf_b_dw, v_cf_norm, v_cf_w_pw2, v_cf_b_pw2, v_sc_w_in, v_sc_w_conv, v_sc_w_out, v_mlp_w1, v_mlp_w2, v_ple_w_proj, v_ple_w_gate, v_norm_final):
    w = dict(norm_mix=norm_mix, norm_mlp=norm_mlp, norm_ple=norm_ple, cf_w_pw1=cf_w_pw1, cf_b_pw1=cf_b_pw1,
             cf_w_dw=cf_w_dw, cf_b_dw=cf_b_dw, cf_norm=cf_norm, cf_w_pw2=cf_w_pw2, cf_b_pw2=cf_b_pw2,
             sc_w_in=sc_w_in, sc_w_conv=sc_w_conv, sc_w_out=sc_w_out, mlp_w1=mlp_w1, mlp_w2=mlp_w2,
             ple_w_proj=ple_w_proj, ple_w_gate=ple_w_gate, norm_final=norm_final)
    m = dict(norm_mix=m_norm_mix, norm_mlp=m_norm_mlp, norm_ple=m_norm_ple, cf_w_pw1=m_cf_w_pw1, cf_b_pw1=m_cf_b_pw1,
             cf_w_dw=m_cf_w_dw, cf_b_dw=m_cf_b_dw, cf_norm=m_cf_norm, cf_w_pw2=m_cf_w_pw2, cf_b_pw2=m_cf_b_pw2,
             sc_w_in=m_sc_w_in, sc_w_conv=m_sc_w_conv, sc_w_out=m_sc_w_out, mlp_w1=m_mlp_w1, mlp_w2=m_mlp_w2,
             ple_w_proj=m_ple_w_proj, ple_w_gate=m_ple_w_gate, norm_final=m_norm_final)
    v = dict(norm_mix=v_norm_mix, norm_mlp=v_norm_mlp, norm_ple=v_norm_ple, cf_w_pw1=v_cf_w_pw1, cf_b_pw1=v_cf_b_pw1,
             cf_w_dw=v_cf_w_dw, cf_b_dw=v_cf_b_dw, cf_norm=v_cf_norm, cf_w_pw2=v_cf_w_pw2, cf_b_pw2=v_cf_b_pw2,
             sc_w_in=v_sc_w_in, sc_w_conv=v_sc_w_conv, sc_w_out=v_sc_w_out, mlp_w1=v_mlp_w1, mlp_w2=v_mlp_w2,
             ple_w_proj=v_ple_w_proj, ple_w_gate=v_ple_w_gate, norm_final=v_norm_final)
    depth, d = norm_mix.shape
    n_cf = cf_w_dw.shape[0]
    me = 4 * lax.axis_index("x") + 2 * lax.axis_index("y") + lax.axis_index("c")

    taps_w = _pack_taps(cf_w_dw, sc_w_conv)
    shards = {k: cast_bf16(f"cast_{k}", w[k]) for k in BIG}
    gathered = {}
    ids = iter(range(4 * depth))
    for i in range(depth):
        layer_names = _layer_weights(i)
        for part, names in (("mixer", layer_names[:2]), ("mlp", layer_names[2:])):
            first = i == 0 and part == "mixer"
            srcs = [shards[k] for k, _ in names] + ([taps_w[None]] if first else [])
            got = sc_gather(f"gather_{part}_{i}", next(ids), srcs, [l for _, l in names] + ([0] if first else []))
            for (k, l), g in zip(names, got):
                gathered[(k, l)] = g if k in COL_SHARDED else g.reshape(-1, g.shape[3])
            if first:
                taps_full = jnp.transpose(got[-1][:, 0], (1, 0, 2)).reshape(taps_w.shape[0], d)
    conv_w = {"cf": taps_full[:n_cf * CONV_A_HALO].reshape(n_cf, CONV_A_HALO, d),
              "sc": taps_full[n_cf * CONV_A_HALO:].reshape(-1, CONV_B_HALO, d)}

    def get_w(k, l, after=None):
        return gathered[(k, l)]

    received = {}

    def put_grads(grads):
        names = list(grads)
        got = sc_exchange(f"grad_exchange_{names[0][0]}_{names[0][1]}", next(ids), [grads[n] for n in names],
                          [jax.ShapeDtypeStruct(grads[n].shape, BF16) for n in names],
                          [_slice_of] * len(names), [_slice_of] * len(names))
        received.update(zip(names, got))
        return None

    small = {k: w[k] for k, _ in SMALL_ROWS}
    small["norm_final"] = norm_final[None]
    loss_part, grad_x, sg = _local_step(x[0], p[:, 0], loss_target[0], small, get_w, conv_w, put_grads)

    out = {k: None for k in BIG}
    for (k, l), recv in received.items():
        out[k] = adamw_layer(f"adamw_{k}_{l}", recv, w[k], m[k], v[k], l, out[k])

    parts = []
    for k, n in SMALL_ROWS:
        for g in sg[k]:
            parts += [g[:, :d], g[:, d:]] if g.shape[1] == 2 * d else [g]
    parts.append(loss_part)
    part_pack = jnp.concatenate(parts, axis=0)
    tap_pack = jnp.concatenate(sg["cf_w_dw"] + sg["sc_w_conv"], axis=0)
    part_all, tap_all = all_gather("ag_small", [(part_pack[None], 0), (tap_pack[None], 0)])
    cb = cf_w_dw.shape[2]
    tap_mine = lax.dynamic_slice_in_dim(tap_all[:, 0], me * cb, cb, axis=2)
    sm = small_update("small_update", part_all[:, 0], tap_mine,
                      _pack_small(w, d), _pack_small(m, d), _pack_small(v, d),
                      taps_w, _pack_taps(m["cf_w_dw"], m["sc_w_conv"]), _pack_taps(v["cf_w_dw"], v["sc_w_conv"]))
    shapes = {k: w[k].shape for k, _ in SMALL_ROWS}
    for t in range(4):
        un = _unpack_small(sm[t], shapes)
        cf_t, sc_t = _unpack_taps(sm[4 + t], n_cf)
        for k in un:
            out.setdefault(k, [None] * 4)[t] = un[k]
        out.setdefault("cf_w_dw", [None] * 4)[t] = cf_t
        out.setdefault("sc_w_conv", [None] * 4)[t] = sc_t
    loss = sm[8][0, 0]

    return (loss, grad_x[None], *[out[k][0] for k in WEIGHT_ORDER], *[out[k][1] for k in WEIGHT_ORDER],
            *[out[k][2] for k in WEIGHT_ORDER], *[out[k][3] for k in WEIGHT_ORDER])
```

```python
import jax
import jax.numpy as jnp
from jax import lax
from jax.experimental import pallas as pl
from jax.experimental.pallas import tpu as pltpu
from jax.experimental.pallas import tpu_sc as plsc

F32 = jnp.float32
BF16 = jnp.bfloat16
EPS = 1e-6
NDEV = 8
N_PEERS = NDEV - 1
MESH = pl.DeviceIdType.MESH

ADAM_LR = 0.001
ADAM_B1 = 0.9
ADAM_B2 = 0.999
ADAM_EPS = 1e-08
ADAM_WD = 0.01
ADAM_STEP = 10

V7X_VMEM_BYTES = 64 * 1024 * 1024
VMEM_LIMIT_MAX = 56 * 1024 * 1024
SUBLANES = 8
LANES = 128
CONV_A_TAPS = 31
CONV_A_HALO = 32
CONV_B_TAPS = 3
CONV_B_HALO = 8


def _nbytes(shape, dtype):
    n = 1
    for s in shape:
        if s is not None:
            n *= s
    return n * jnp.dtype(dtype).itemsize


def _vmem_limit(block_bytes, scratch_bytes=0):
    need = 2 * block_bytes + scratch_bytes
    return int(min(VMEM_LIMIT_MAX, max(32 * 1024 * 1024, need + need // 2 + (4 << 20))))


def _params(sem, block_bytes, scratch_bytes=0):
    return pltpu.CompilerParams(dimension_semantics=sem, vmem_limit_bytes=_vmem_limit(block_bytes, scratch_bytes))


_DIMS = {
    "nn": (((1,), (0,)), ((), ())),
    "nt": (((1,), (1,)), ((), ())),
    "tn": (((0,), (0,)), ((), ())),
}


def _mm(name, dims, grid, acc_shape, a, a_spec, b, b_spec, extras, outs, epi):
    ni, nj, nk = grid
    n_ex, n_out = len(extras), len(outs)
    dn = _DIMS[dims]

    def body(*refs):
        a_ref, b_ref = refs[0], refs[1]
        ex_refs = refs[2:2 + n_ex]
        out_refs = refs[2 + n_ex:2 + n_ex + n_out]
        d = lax.dot_general(a_ref[...].astype(BF16), b_ref[...].astype(BF16), dn, preferred_element_type=F32)

        def finish(acc):
            res = epi(acc, *[r[...] for r in ex_refs])
            for o_ref, r in zip(out_refs, res):
                o_ref[...] = r.astype(o_ref.dtype)

        if nk == 1:
            finish(d)
        else:
            acc_ref = refs[2 + n_ex + n_out]
            k = pl.program_id(2)

            @pl.when(k == 0)
            def _():
                acc_ref[...] = d

            @pl.when(jnp.logical_and(k > 0, k < nk - 1))
            def _():
                acc_ref[...] += d

            @pl.when(k == nk - 1)
            def _():
                finish(acc_ref[...] + d)

    blk = _nbytes(a_spec.block_shape, a.dtype) + _nbytes(b_spec.block_shape, b.dtype)
    for arr, spec in list(extras) + list(outs):
        blk += _nbytes(spec.block_shape, arr.dtype)
    acc_bytes = _nbytes(acc_shape, F32)
    scratch = [pltpu.VMEM(acc_shape, F32)] if nk > 1 else []
    return pl.pallas_call(
        body,
        name=name,
        grid=grid,
        in_specs=[a_spec, b_spec] + [s for _, s in extras],
        out_specs=[s for _, s in outs],
        out_shape=[o for o, _ in outs],
        scratch_shapes=scratch,
        compiler_params=_params(("parallel", "parallel", "arbitrary"), blk, 3 * acc_bytes),
    )(a, b, *[e for e, _ in extras])


def _tile(n, pref):
    if n <= pref:
        return n
    t = pref - pref % LANES
    while t > LANES and n % t:
        t -= LANES
    assert n % t == 0, (n, pref)
    return t


def _row_tile(n, pref):
    if n <= pref:
        return n
    t = max(SUBLANES, pref - pref % SUBLANES)
    while t > SUBLANES and n % t:
        t -= SUBLANES
    assert n % t == 0, (n, pref)
    return t


def _id_epi(acc):
    return (acc,)


def mm_x_wcol(name, x, w, layer, extras=(), outs_dtypes=(F32,), epi=_id_epi, tm=1024, tn=512):
    m, kdim = x.shape
    c = w.shape[3]
    n = NDEV * c
    tm, tn = _tile(m, tm), _tile(c, tn)
    tk = _tile(kdim, 2048)
    grid = (m // tm, n // tn, kdim // tk)
    per = c // tn
    a_spec = pl.BlockSpec((tm, tk), lambda i, j, k: (i, k))
    b_spec = pl.BlockSpec((None, None, tk, tn), lambda i, j, k: (j // per, layer, k, j % per))
    ex = [(e, _ex_spec(e, kind, tm, tn)) for e, kind in extras]
    o_spec = pl.BlockSpec((tm, tn), lambda i, j, k: (i, j))
    outs = [(jax.ShapeDtypeStruct((m, n), dt), o_spec) for dt in outs_dtypes]
    return _mm(name, "nn", grid, (tm, tn), x, a_spec, w, b_spec, ex, outs, epi)


def mm_x_wrow(name, x, w, extras=(), outs_dtypes=(F32,), epi=_id_epi, tm=1024, tn=512):
    m, kdim = x.shape
    n = w.shape[1]
    assert kdim == w.shape[0]
    tm, tn = _tile(m, tm), _tile(n, tn)
    tk = _tile(kdim, 2048)
    grid = (m // tm, n // tn, kdim // tk)
    a_spec = pl.BlockSpec((tm, tk), lambda i, j, k: (i, k))
    b_spec = pl.BlockSpec((tk, tn), lambda i, j, k: (k, j))
    ex = [(e, _ex_spec(e, kind, tm, tn)) for e, kind in extras]
    o_spec = pl.BlockSpec((tm, tn), lambda i, j, k: (i, j))
    outs = [(jax.ShapeDtypeStruct((m, n), dt), o_spec) for dt in outs_dtypes]
    return _mm(name, "nn", grid, (tm, tn), x, a_spec, w, b_spec, ex, outs, epi)


def mm_dy_wcol_t(name, dy, w, layer, extras=(), outs_dtypes=(F32,), epi=_id_epi, tm=1024, tn=1024):
    m, n = dy.shape
    kdim, c = w.shape[2], w.shape[3]
    assert n == NDEV * c
    tm, tn = _tile(m, tm), _tile(kdim, tn)
    tk = _tile(c, 2048)
    per = c // tk
    grid = (m // tm, kdim // tn, n // tk)
    a_spec = pl.BlockSpec((tm, tk), lambda i, j, k: (i, k))
    b_spec = pl.BlockSpec((None, None, tn, tk), lambda i, j, k: (k // per, layer, j, k % per))
    ex = [(e, _ex_spec(e, kind, tm, tn)) for e, kind in extras]
    o_spec = pl.BlockSpec((tm, tn), lambda i, j, k: (i, j))
    outs = [(jax.ShapeDtypeStruct((m, kdim), dt), o_spec) for dt in outs_dtypes]
    return _mm(name, "nt", grid, (tm, tn), dy, a_spec, w, b_spec, ex, outs, epi)


def mm_dy_wrow_t(name, dy, w, extras=(), outs_dtypes=(F32,), epi=_id_epi, tm=1024, tn=512):
    m, n = dy.shape
    kdim = w.shape[0]
    assert n == w.shape[1]
    tm, tn = _tile(m, tm), _tile(kdim, tn)
    tk = _tile(n, 2048)
    grid = (m // tm, kdim // tn, n // tk)
    a_spec = pl.BlockSpec((tm, tk), lambda i, j, k: (i, k))
    b_spec = pl.BlockSpec((tn, tk), lambda i, j, k: (j, k))
    ex = [(e, _ex_spec(e, kind, tm, tn)) for e, kind in extras]
    o_spec = pl.BlockSpec((tm, tn), lambda i, j, k: (i, j))
    outs = [(jax.ShapeDtypeStruct((m, kdim), dt), o_spec) for dt in outs_dtypes]
    return _mm(name, "nt", grid, (tm, tn), dy, a_spec, w, b_spec, ex, outs, epi)


def mm_xt_dy(name, x, dy, col_shards, tm=1024, tn=512):
    m, kdim = x.shape
    n = dy.shape[1]
    tk = _tile(m, 2048)
    if col_shards:
        c = n // NDEV
        tm, tn = _tile(kdim, tm), _tile(c, tn)
        per = c // tn
        out = jax.ShapeDtypeStruct((NDEV, kdim, c), BF16)
        o_spec = pl.BlockSpec((None, tm, tn), lambda i, j, k: (j // per, i, j % per))
    else:
        tm, tn = _tile(kdim, tm), _tile(n, tn)
        out = jax.ShapeDtypeStruct((kdim, n), BF16)
        o_spec = pl.BlockSpec((tm, tn), lambda i, j, k: (i, j))
    grid = (kdim // tm, n // tn, m // tk)
    a_spec = pl.BlockSpec((tk, tm), lambda i, j, k: (k, i))
    b_spec = pl.BlockSpec((tk, tn), lambda i, j, k: (k, j))
    g = _mm(name, "tn", grid, (tm, tn), x, a_spec, dy, b_spec, [], [(out, o_spec)], _id_epi)[0]
    return g if col_shards else g.reshape(NDEV, kdim // NDEV, n)


def _ex_spec(e, kind, tm, tn):
    if kind == "tile":
        return pl.BlockSpec((tm, tn), lambda i, j, k: (i, j))
    if kind == "row":
        return pl.BlockSpec((1, tn), lambda i, j, k: (0, j))
    raise ValueError(kind)


def _rows_call(name, body, n_rows, ts, ins, outs, scratch=(), scratch_bytes=0):
    blk = sum(_nbytes(s.block_shape, a.dtype) for a, s in list(ins) + list(outs))
    return pl.pallas_call(
        body,
        name=name,
        grid=(n_rows // ts,),
        in_specs=[s for _, s in ins],
        out_specs=[s for _, s in outs],
        out_shape=[o for o, _ in outs],
        scratch_shapes=list(scratch),
        compiler_params=_params(("arbitrary",), blk, scratch_bytes + 4 * blk // 2),
    )(*[a for a, _ in ins])


def _blk(ts, d):
    return pl.BlockSpec((ts, d), lambda i: (i, 0))


def _full(shape):
    return pl.BlockSpec(shape, lambda i: tuple(0 for _ in shape))


def _rowsum8(v):
    t, d = v.shape
    return jnp.sum(v.reshape(t // SUBLANES, SUBLANES, d), axis=0)


def _accumulate(ref, val):
    @pl.when(pl.program_id(0) == 0)
    def _():
        ref[...] = val

    @pl.when(pl.program_id(0) > 0)
    def _():
        ref[...] += val


def _rstd(x):
    return lax.rsqrt(jnp.mean(x * x, axis=-1, keepdims=True) + EPS)


def _rms_bwd_math(dy, x, g):
    r = _rstd(x)
    gdy = dy * g
    c = jnp.mean(gdy * x, axis=-1, keepdims=True)
    dx = r * gdy - x * (r * r * r * c)
    return dx, dy * (x * r)


def rms_fwd(name, h, g, ts=512):
    s, d = h.shape
    ts = min(ts, s)

    def body(h_ref, g_ref, u_ref):
        x = h_ref[...]
        u_ref[...] = ((x * _rstd(x)) * g_ref[...]).astype(BF16)

    return _rows_call(name, body, s, ts, [(h, _blk(ts, d)), (g, _full((1, d)))],
                      [(jax.ShapeDtypeStruct((s, d), BF16), _blk(ts, d))])[0]


def rms_bwd(name, du, h, g, dres, ts=256):
    s, d = h.shape
    ts = min(ts, s)

    def body(du_ref, h_ref, g_ref, dres_ref, dh_ref, dhb_ref, dg_ref, cs_ref):
        dx, dg = _rms_bwd_math(du_ref[...], h_ref[...], g_ref[...])
        dh = dres_ref[...] + dx
        dh_ref[...] = dh
        dhb_ref[...] = dh.astype(BF16)
        _accumulate(dg_ref, _rowsum8(dg))
        _accumulate(cs_ref, _rowsum8(dh))

    return _rows_call(
        name, body, s, ts,
        [(du, _blk(ts, d)), (h, _blk(ts, d)), (g, _full((1, d))), (dres, _blk(ts, d))],
        [(jax.ShapeDtypeStruct((s, d), F32), _blk(ts, d)), (jax.ShapeDtypeStruct((s, d), BF16), _blk(ts, d)),
         (jax.ShapeDtypeStruct((SUBLANES, d), F32), _full((SUBLANES, d))),
         (jax.ShapeDtypeStruct((SUBLANES, d), F32), _full((SUBLANES, d)))])


def final_loss(name, h, g, target, ts=256):
    s, d = h.shape
    ts = min(ts, s)

    def body(h_ref, g_ref, t_ref, loss_ref, dh_ref, dg_ref):
        x = h_ref[...]
        gf = g_ref[...]
        y = (x * _rstd(x)) * gf
        err = y - t_ref[...]
        _accumulate(loss_ref, _rowsum8(err * err) * (0.5 / d))
        dx, dg = _rms_bwd_math(err * (1.0 / d), x, gf)
        dh_ref[...] = dx
        _accumulate(dg_ref, _rowsum8(dg))

    return _rows_call(
        name, body, s, ts,
        [(h, _blk(ts, d)), (g, _full((1, d))), (target, _blk(ts, d))],
        [(jax.ShapeDtypeStruct((SUBLANES, d), F32), _full((SUBLANES, d))),
         (jax.ShapeDtypeStruct((s, d), F32), _blk(ts, d)),
         (jax.ShapeDtypeStruct((SUBLANES, d), F32), _full((SUBLANES, d)))])


def ple_bwd_elem(name, dh, g, e, ts=512):
    s, d = dh.shape
    ts = min(ts, s)

    def body(dh_ref, g_ref, e_ref, de_ref, dgl_ref):
        dh_v, g_v = dh_ref[...], g_ref[...]
        de_ref[...] = (dh_v * g_v).astype(BF16)
        dgl_ref[...] = (dh_v * e_ref[...] * (g_v * (1.0 - g_v))).astype(BF16)

    return _rows_call(name, body, s, ts, [(dh, _blk(ts, d)), (g, _blk(ts, d)), (e, _blk(ts, d))],
                      [(jax.ShapeDtypeStruct((s, d), BF16), _blk(ts, d)),
                       (jax.ShapeDtypeStruct((s, d), BF16), _blk(ts, d))])


CONV_LANES = 256
CONV_ROWS = 64


def _lane_chunks(d, fn):
    lc = min(CONV_LANES, d)

    def lane_body(c, carry):
        fn(pl.ds(pl.multiple_of(c * lc, lc), lc))
        return carry

    lax.fori_loop(0, d // lc, lane_body, 0)


def _shifted_copies(buf, sh, lanes):
    rows = buf.shape[0] - SUBLANES
    for s in range(1, SUBLANES):
        sh[s, pl.ds(0, rows), :] = buf[pl.ds(s, rows), lanes]


def _window(buf, sh, lanes, start, rows):
    s = start % SUBLANES
    if s == 0:
        return buf[pl.ds(start, rows), lanes]
    return sh[s, pl.ds(start - s, rows), :]


def _prev_halo_spec(ts, halo, width):
    per = ts // halo
    return pl.BlockSpec((halo, width), lambda i: (jnp.maximum(i * per - 1, 0), 0))


def _next_halo_spec(ts, halo, width, n_rows):
    per = ts // halo
    last = n_rows // halo - 1
    return pl.BlockSpec((halo, width), lambda i: (jnp.minimum((i + 1) * per, last), 0))


def cf_fwd_mid(name, a, w_dw, b_dw, gn, ts=256):
    s, d2 = a.shape
    d = d2 // 2
    ts = min(ts, s)
    hl = CONV_A_HALO
    off = hl - (CONV_A_TAPS - 1)

    rc = min(CONV_ROWS, ts)

    def body(a_ref, ah_ref, w_ref, b_ref, gn_ref, v0_ref, v1_ref, v3_ref, buf, sh):
        first = pl.program_id(0) == 0
        halo = ah_ref[...]
        hv0 = halo[:, :d] * jax.nn.sigmoid(halo[:, d:])
        buf[pl.ds(0, hl), :] = jnp.where(first, 0.0, hv0)
        main = a_ref[...]
        v0 = main[:, :d] * jax.nn.sigmoid(main[:, d:])
        buf[pl.ds(hl, ts), :] = v0
        v0_ref[...] = v0

        def conv(lanes):
            _shifted_copies(buf, sh, lanes)
            for r0 in range(0, ts, rc):
                acc = jnp.zeros((rc, lanes.size), F32)
                for k in range(CONV_A_TAPS):
                    acc = acc + w_ref[pl.ds(k, 1), lanes] * _window(buf, sh, lanes, r0 + off + k, rc)
                v1_ref[pl.ds(r0, rc), lanes] = acc + b_ref[:, lanes]

        _lane_chunks(d, conv)
        v1 = v1_ref[...]
        v2 = (v1 * _rstd(v1)) * gn_ref[...]
        v3_ref[...] = (v2 * jax.nn.sigmoid(v2)).astype(BF16)

    return _rows_call(
        name, body, s, ts,
        [(a, _blk(ts, d2)), (a, _prev_halo_spec(ts, hl, d2)), (w_dw, _full(w_dw.shape)),
         (b_dw, _full((1, d))), (gn, _full((1, d)))],
        [(jax.ShapeDtypeStruct((s, d), F32), _blk(ts, d)), (jax.ShapeDtypeStruct((s, d), F32), _blk(ts, d)),
         (jax.ShapeDtypeStruct((s, d), BF16), _blk(ts, d))],
        scratch=[pltpu.VMEM((hl + ts, d), F32), pltpu.VMEM((SUBLANES, hl + ts, min(CONV_LANES, d)), F32)],
        scratch_bytes=_nbytes((hl + ts, d + SUBLANES * CONV_LANES), F32))


def cf_bwd_rows(name, dv3, v1, gn, ts=256):
    s, d = v1.shape
    ts = min(ts, s)

    def body(dv3_ref, v1_ref, gn_ref, dv1_ref, dgn_ref, db_ref):
        v1 = v1_ref[...]
        gn_v = gn_ref[...]
        v2 = (v1 * _rstd(v1)) * gn_v
        sg = jax.nn.sigmoid(v2)
        dv2 = dv3_ref[...] * (sg * (1.0 + v2 * (1.0 - sg)))
        dv1, dgn = _rms_bwd_math(dv2, v1, gn_v)
        dv1_ref[...] = dv1
        _accumulate(dgn_ref, _rowsum8(dgn))
        _accumulate(db_ref, _rowsum8(dv1))

    return _rows_call(
        name, body, s, ts, [(dv3, _blk(ts, d)), (v1, _blk(ts, d)), (gn, _full((1, d)))],
        [(jax.ShapeDtypeStruct((s, d), F32), _blk(ts, d)),
         (jax.ShapeDtypeStruct((SUBLANES, d), F32), _full((SUBLANES, d))),
         (jax.ShapeDtypeStruct((SUBLANES, d), F32), _full((SUBLANES, d)))])


def cf_bwd_conv(name, dv1, v0, a, w_dw, ts=256):
    s, d = dv1.shape
    ts = min(ts, s)
    hl = CONV_A_HALO
    taps = CONV_A_TAPS
    off = hl - (taps - 1)
    last_blk = s // ts - 1

    rc = min(CONV_ROWS, ts)

    def body(dv1_ref, dv1n_ref, v0_ref, v0p_ref, a_ref, w_ref, da_ref, dw_ref, db_ref,
             dbuf, vbuf, dv0_buf, dw_acc, dsh, vsh):
        i = pl.program_id(0)
        dbuf[pl.ds(0, ts), :] = dv1_ref[...]
        dbuf[pl.ds(ts, hl), :] = jnp.where(i == last_blk, 0.0, dv1n_ref[...])
        vbuf[pl.ds(0, hl), :] = jnp.where(i == 0, 0.0, v0p_ref[...])
        vbuf[pl.ds(hl, ts), :] = v0_ref[...]

        @pl.when(i == 0)
        def _():
            dw_acc[...] = jnp.zeros_like(dw_acc)

        def conv_t(lanes):
            _shifted_copies(dbuf, dsh, lanes)
            _shifted_copies(vbuf, vsh, lanes)
            for r0 in range(0, ts, rc):
                g = dbuf[pl.ds(r0, rc), lanes]
                acc = jnp.zeros((rc, lanes.size), F32)
                for k in range(taps):
                    acc = acc + w_ref[pl.ds(k, 1), lanes] * _window(dbuf, dsh, lanes, r0 + taps - 1 - k, rc)
                    prod = g * _window(vbuf, vsh, lanes, r0 + off + k, rc)
                    dw_acc[pl.ds(k * SUBLANES, SUBLANES), lanes] += _rowsum8(prod)
                dv0_buf[pl.ds(r0, rc), lanes] = acc

        _lane_chunks(d, conv_t)
        dv0 = dv0_buf[...]
        av = a_ref[...]
        val, sg = av[:, :d], jax.nn.sigmoid(av[:, d:])
        dval = dv0 * sg
        dgate = dv0 * val * (sg * (1.0 - sg))
        da_ref[:, :d] = dval.astype(BF16)
        da_ref[:, d:] = dgate.astype(BF16)
        _accumulate(db_ref.at[:, pl.ds(0, d)], _rowsum8(dval))
        _accumulate(db_ref.at[:, pl.ds(d, d)], _rowsum8(dgate))

        @pl.when(i == last_blk)
        def _():
            dw_ref[...] = jnp.sum(dw_acc[...].reshape(hl, SUBLANES, d), axis=1)

    lc = min(CONV_LANES, d)
    scratch = [pltpu.VMEM((ts + hl, d), F32), pltpu.VMEM((hl + ts, d), F32), pltpu.VMEM((ts, d), F32),
               pltpu.VMEM((hl * SUBLANES, d), F32), pltpu.VMEM((SUBLANES, ts + hl, lc), F32),
               pltpu.VMEM((SUBLANES, hl + ts, lc), F32)]
    sbytes = _nbytes((3 * ts + 2 * hl + hl * SUBLANES, d), F32) + 2 * _nbytes((SUBLANES, ts + hl, lc), F32)
    return _rows_call(
        name, body, s, ts,
        [(dv1, _blk(ts, d)), (dv1, _next_halo_spec(ts, hl, d, s)), (v0, _blk(ts, d)), (v0, _prev_halo_spec(ts, hl, d)),
         (a, _blk(ts, 2 * d)), (w_dw, _full(w_dw.shape))],
        [(jax.ShapeDtypeStruct((s, 2 * d), BF16), _blk(ts, 2 * d)),
         (jax.ShapeDtypeStruct((hl, d), F32), _full((hl, d))),
         (jax.ShapeDtypeStruct((SUBLANES, 2 * d), F32), _full((SUBLANES, 2 * d)))],
        scratch=scratch, scratch_bytes=sbytes)


def sc_fwd_mid(name, bcv, w_conv, ts=256):
    s, d3 = bcv.shape
    d = d3 // 3
    ts = min(ts, s)
    hl = CONV_B_HALO
    off = hl - (CONV_B_TAPS - 1)

    def body(x_ref, xp_ref, w_ref, y_ref, buf):
        hp = xp_ref[...]
        buf[pl.ds(0, hl), :] = jnp.where(pl.program_id(0) == 0, 0.0, hp[:, d:2 * d] * hp[:, 2 * d:])
        buf[pl.ds(hl, ts), :] = x_ref[:, d:2 * d] * x_ref[:, 2 * d:]
        cc = jnp.zeros((ts, d), F32)
        for k in range(CONV_B_TAPS):
            cc = cc + w_ref[pl.ds(k, 1), :] * buf[pl.ds(off + k, ts), :]
        y_ref[...] = (x_ref[:, :d] * cc).astype(BF16)

    return _rows_call(
        name, body, s, ts,
        [(bcv, _blk(ts, d3)), (bcv, _prev_halo_spec(ts, hl, d3)), (w_conv, _full(w_conv.shape))],
        [(jax.ShapeDtypeStruct((s, d), BF16), _blk(ts, d))],
        scratch=[pltpu.VMEM((hl + ts, d), F32)], scratch_bytes=_nbytes((hl + ts, d), F32))[0]


def sc_bwd_mid(name, dy, bcv, w_conv, ts=256):
    s, d3 = bcv.shape
    d = d3 // 3
    ts = min(ts, s)
    hl = CONV_B_HALO
    taps = CONV_B_TAPS
    off = hl - (taps - 1)
    last_blk = s // ts - 1

    def body(dy_ref, dyn_ref, x_ref, xp_ref, xn_ref, w_ref, dx_ref, dw_ref, cvbuf, dbuf, dw_acc):
        i = pl.program_id(0)
        hp = xp_ref[...]
        cvbuf[pl.ds(0, hl), :] = jnp.where(i == 0, 0.0, hp[:, d:2 * d] * hp[:, 2 * d:])
        gb, gc, v = x_ref[:, :d], x_ref[:, d:2 * d], x_ref[:, 2 * d:]
        cvbuf[pl.ds(hl, ts), :] = gc * v
        dy_v = dy_ref[...]
        dcc = dy_v * gb
        dbuf[pl.ds(0, ts), :] = dcc
        dbuf[pl.ds(ts, hl), :] = jnp.where(i == last_blk, 0.0, dyn_ref[...] * xn_ref[:, :d])

        @pl.when(i == 0)
        def _():
            dw_acc[...] = jnp.zeros_like(dw_acc)

        cc = jnp.zeros((ts, d), F32)
        dcv = jnp.zeros((ts, d), F32)
        for k in range(taps):
            win = cvbuf[pl.ds(off + k, ts), :]
            cc = cc + w_ref[pl.ds(k, 1), :] * win
            dcv = dcv + w_ref[pl.ds(k, 1), :] * dbuf[pl.ds(taps - 1 - k, ts), :]
            dw_acc[pl.ds(k * SUBLANES, SUBLANES), :] += _rowsum8(dcc * win)
        dx_ref[:, :d] = (dy_v * cc).astype(BF16)
        dx_ref[:, d:2 * d] = (dcv * v).astype(BF16)
        dx_ref[:, 2 * d:] = (dcv * gc).astype(BF16)

        @pl.when(i == last_blk)
        def _():
            dw_ref[...] = jnp.sum(dw_acc[...].reshape(hl, SUBLANES, d), axis=1)

    scratch = [pltpu.VMEM((hl + ts, d), F32), pltpu.VMEM((ts + hl, d), F32), pltpu.VMEM((hl * SUBLANES, d), F32)]
    sbytes = _nbytes((2 * ts + 2 * hl + hl * SUBLANES, d), F32)
    return _rows_call(
        name, body, s, ts,
        [(dy, _blk(ts, d)), (dy, _next_halo_spec(ts, hl, d, s)), (bcv, _blk(ts, d3)), (bcv, _prev_halo_spec(ts, hl, d3)),
         (bcv, _next_halo_spec(ts, hl, d3, s)), (w_conv, _full(w_conv.shape))],
        [(jax.ShapeDtypeStruct((s, d3), BF16), _blk(ts, d3)), (jax.ShapeDtypeStruct((hl, d), F32), _full((hl, d)))],
        scratch=scratch, scratch_bytes=sbytes)


def _row(a, i):
    return lax.slice_in_dim(a, i, i + 1, axis=0)


def _local_step(x, p, target, small, get_w, conv_w, put_grads):
    depth = p.shape[0]
    acts = []
    h = x
    for i in range(depth):
        j = i // 2
        act = {"h": h}
        u = rms_fwd(f"rms_mix_{i}", h, _row(small["norm_mix"], i))
        act["u"] = u
        if i % 2 == 0:
            a = mm_x_wcol(f"cf_pw1_{i}", u, get_w("cf_w_pw1", j, u), 0, extras=[(_row(small["cf_b_pw1"], j), "row")],
                          epi=lambda acc, b: (acc + b,))[0]
            v0, v1, v3 = cf_fwd_mid(f"cf_mid_{i}", a, conv_w["cf"][j], _row(small["cf_b_dw"], j), _row(small["cf_norm"], j))
            act.update(a=a, v0=v0, v1=v1, v3=v3)
            h1 = mm_x_wrow(f"cf_pw2_{i}", v3, get_w("cf_w_pw2", j, v3),
                           extras=[(_row(small["cf_b_pw2"], j), "row"), (h, "tile")],
                           epi=lambda acc, b, res: (res + (acc + b),))[0]
        else:
            bcv = mm_x_wcol(f"sc_in_{i}", u, get_w("sc_w_in", j, u), 0, tn=768)[0]
            y = sc_fwd_mid(f"sc_mid_{i}", bcv, conv_w["sc"][j])
            act.update(bcv=bcv, y=y)
            h1 = mm_x_wrow(f"sc_out_{i}", y, get_w("sc_w_out", j, y), extras=[(h, "tile")],
                           epi=lambda acc, res: (res + acc,))[0]
        act["h1"] = h1
        u2 = rms_fwd(f"rms_mlp_{i}", h1, _row(small["norm_mlp"], i))
        z, hd = mm_x_wcol(f"mlp_w1_{i}", u2, get_w("mlp_w1", i, u2), 0, outs_dtypes=(F32, BF16),
                          epi=lambda acc: (acc, jnp.square(jnp.maximum(acc, 0.0))))
        h2 = mm_x_wrow(f"mlp_w2_{i}", hd, get_w("mlp_w2", i, hd), extras=[(h1, "tile")],
                       epi=lambda acc, res: (res + acc,), tn=1024)[0]
        act.update(u2=u2, z=z, hd=hd, h2=h2)
        n3 = rms_fwd(f"rms_ple_{i}", h2, _row(small["norm_ple"], i))
        e = mm_x_wcol(f"ple_proj_{i}", p[i], get_w("ple_w_proj", i, n3), 0)[0]

        def ple_epi(acc, e_t, res):
            g_t = jax.nn.sigmoid(acc)
            return g_t, res + g_t * e_t

        g, h3 = mm_x_wrow(f"ple_gate_{i}", n3, get_w("ple_w_gate", i, e), extras=[(e, "tile"), (h2, "tile")],
                          outs_dtypes=(F32, F32), epi=ple_epi)
        act.update(n3=n3, e=e, g=g)
        acts.append(act)
        h = h3

    loss_part, dh, dg_final = final_loss("final_loss", h, small["norm_final"], target)
    sg = {k: [None] * small[k].shape[0] for k in small if k != "norm_final"}
    sg["norm_final"] = [dg_final]
    sg["cf_w_dw"] = [None] * conv_w["cf"].shape[0]
    sg["sc_w_conv"] = [None] * conv_w["sc"].shape[0]

    for i in reversed(range(depth)):
        j = i // 2
        act = acts[i]
        de, dgl = ple_bwd_elem(f"ple_bwd_{i}", dh, act["g"], act["e"])
        g_proj = mm_xt_dy(f"d_ple_proj_{i}", p[i], de, True)
        g_gate = mm_xt_dy(f"d_ple_gate_{i}", act["n3"], dgl, False)
        dn3 = mm_dy_wrow_t(f"dn3_{i}", dgl, get_w("ple_w_gate", i))[0]
        dh2, dh2b, sg["norm_ple"][i], _ = rms_bwd(f"rms_ple_bwd_{i}", dn3, act["h2"], _row(small["norm_ple"], i), dh)
        g_w2 = mm_xt_dy(f"d_mlp_w2_{i}", act["hd"], dh2b, False)
        dz = mm_dy_wrow_t(f"dz_{i}", dh2b, get_w("mlp_w2", i), extras=[(act["z"], "tile")], outs_dtypes=(BF16,),
                          epi=lambda acc, z_t: (acc * (2.0 * jnp.maximum(z_t, 0.0)),))[0]
        g_w1 = mm_xt_dy(f"d_mlp_w1_{i}", act["u2"], dz, True)
        put_grads({("ple_w_proj", i): g_proj, ("ple_w_gate", i): g_gate, ("mlp_w2", i): g_w2, ("mlp_w1", i): g_w1})
        du2 = mm_dy_wcol_t(f"du2_{i}", dz, get_w("mlp_w1", i), 0)[0]
        dh1, dh1b, sg["norm_mlp"][i], cs1 = rms_bwd(f"rms_mlp_bwd_{i}", du2, act["h1"], _row(small["norm_mlp"], i), dh2)
        if i % 2 == 0:
            g_out = mm_xt_dy(f"d_cf_pw2_{i}", act["v3"], dh1b, False)
            sg["cf_b_pw2"][j] = cs1
            dv3 = mm_dy_wrow_t(f"dv3_{i}", dh1b, get_w("cf_w_pw2", j))[0]
            dv1, sg["cf_norm"][j], sg["cf_b_dw"][j] = cf_bwd_rows(f"cf_bwd_rows_{i}", dv3, act["v1"], _row(small["cf_norm"], j))
            da, sg["cf_w_dw"][j], sg["cf_b_pw1"][j] = cf_bwd_conv(f"cf_bwd_conv_{i}", dv1, act["v0"], act["a"], conv_w["cf"][j])
            g_in = mm_xt_dy(f"d_cf_pw1_{i}", act["u"], da, True)
            put_grads({("cf_w_pw2", j): g_out, ("cf_w_pw1", j): g_in})
            du = mm_dy_wcol_t(f"du_{i}", da, get_w("cf_w_pw1", j), 0)[0]
        else:
            g_out = mm_xt_dy(f"d_sc_out_{i}", act["y"], dh1b, False)
            dy = mm_dy_wrow_t(f"dy_{i}", dh1b, get_w("sc_w_out", j))[0]
            dbcv, sg["sc_w_conv"][j] = sc_bwd_mid(f"sc_bwd_mid_{i}", dy, act["bcv"], conv_w["sc"][j])
            g_in = mm_xt_dy(f"d_sc_in_{i}", act["u"], dbcv, True, tn=768)
            put_grads({("sc_w_out", j): g_out, ("sc_w_in", j): g_in})
            du = mm_dy_wcol_t(f"du_{i}", dbcv, get_w("sc_w_in", j), 0)[0]
        dh, _, sg["norm_mix"][i], _ = rms_bwd(f"rms_mix_bwd_{i}", du, act["h"], _row(small["norm_mix"], i), dh1)
    return loss_part, dh, sg


def _me_and_peers():
    x, y, c = lax.axis_index("x"), lax.axis_index("y"), lax.axis_index("c")
    me = 4 * x + 2 * y + c
    peers = []
    for q in range(1, NDEV):
        px = 1 - x if q & 4 else x
        py = 1 - y if q & 2 else y
        pc = 1 - c if q & 1 else c
        peers.append(((px, py, pc), 4 * px + 2 * py + pc))
    return me, peers


def _exchange(name, srcs, out_shapes, src_fns, dst_fns, after=()):
    n = len(srcs)
    n_after = len(after)

    def body(*refs):
        ins, outs = refs[:n], refs[n + n_after:2 * n + n_after]
        send_sems, recv_sems, local_sems = refs[2 * n + n_after:]
        me, peers = _me_and_peers()
        local, remote = [], []
        for k in range(n):
            cp = pltpu.make_async_copy(src_fns[k](ins[k], me), dst_fns[k](outs[k], me), local_sems.at[k])
            cp.start()
            local.append(cp)
        for q, (peer, peer_blk) in enumerate(peers):
            for k in range(n):
                cp = pltpu.make_async_remote_copy(
                    src_ref=src_fns[k](ins[k], peer_blk), dst_ref=dst_fns[k](outs[k], me),
                    send_sem=send_sems.at[k, q], recv_sem=recv_sems.at[k, q],
                    device_id=peer, device_id_type=MESH)
                cp.start()
                remote.append(cp)
        for q, (peer, peer_blk) in enumerate(peers):
            for k in range(n):
                pltpu.make_async_remote_copy(
                    src_ref=src_fns[k](ins[k], peer_blk), dst_ref=dst_fns[k](outs[k], peer_blk),
                    send_sem=send_sems.at[k, q], recv_sem=recv_sems.at[k, q],
                    device_id=peer, device_id_type=MESH).wait_recv()
        for cp in remote:
            cp.wait_send()
        for cp in local:
            cp.wait()

    any_spec = pl.BlockSpec(memory_space=pl.ANY)
    return pl.pallas_call(
        body,
        name=name,
        in_specs=[any_spec] * (n + n_after),
        out_specs=[any_spec] * n,
        out_shape=out_shapes,
        scratch_shapes=[pltpu.SemaphoreType.DMA((n, N_PEERS)), pltpu.SemaphoreType.DMA((n, N_PEERS)),
                        pltpu.SemaphoreType.DMA((n,))],
    )(*srcs, *after)


def sc_exchange(name, collective_id, srcs, out_shapes, src_fns, dst_fns):
    n = len(srcs)

    def body(*refs):
        ins, outs = refs[:n], refs[n:2 * n]
        send_sems, recv_sems, local_sems = refs[2 * n:]
        me, peers = _me_and_peers()
        barrier = pltpu.get_barrier_semaphore()
        for peer, _ in peers:
            pl.semaphore_signal(barrier, inc=1, device_id=peer, device_id_type=MESH)
        pl.semaphore_wait(barrier, N_PEERS)
        local, remote = [], []
        for k in range(n):
            cp = pltpu.make_async_copy(src_fns[k](ins[k], me), dst_fns[k](outs[k], me), local_sems.at[k])
            cp.start()
            local.append(cp)
        for q, (peer, peer_blk) in enumerate(peers):
            for k in range(n):
                cp = pltpu.make_async_remote_copy(
                    src_ref=src_fns[k](ins[k], peer_blk), dst_ref=dst_fns[k](outs[k], me),
                    send_sem=send_sems.at[k, q], recv_sem=recv_sems.at[k, q],
                    device_id=peer, device_id_type=MESH)
                cp.start()
                remote.append(cp)
        for q, (peer, peer_blk) in enumerate(peers):
            for k in range(n):
                pltpu.make_async_remote_copy(
                    src_ref=src_fns[k](ins[k], peer_blk), dst_ref=dst_fns[k](outs[k], peer_blk),
                    send_sem=send_sems.at[k, q], recv_sem=recv_sems.at[k, q],
                    device_id=peer, device_id_type=MESH).wait_recv()
        for cp in remote:
            cp.wait_send()
        for cp in local:
            cp.wait()

    return pl.kernel(
        body,
        out_type=out_shapes,
        mesh=plsc.ScalarSubcoreMesh(axis_name="sequencer", num_cores=1),
        name=name,
        scratch_types=[pltpu.SemaphoreType.DMA((n, N_PEERS)), pltpu.SemaphoreType.DMA((n, N_PEERS)),
                       pltpu.SemaphoreType.DMA((n,))],
        compiler_params=pltpu.CompilerParams(collective_id=collective_id),
    )(*srcs)


def sc_gather(name, collective_id, srcs, layers):
    n = len(srcs)
    outs_shape = [jax.ShapeDtypeStruct((NDEV, 1) + a.shape[1:], a.dtype) for a in srcs]

    def body(*refs):
        ins, outs = refs[:n], refs[n:2 * n]
        send_sems, recv_sems, local_sems = refs[2 * n:]
        x, y, c = lax.axis_index("x"), lax.axis_index("y"), lax.axis_index("c")
        me = 4 * x + 2 * y + c
        sibling = (x, y, 1 - c)
        chips = [(1 - x, y), (x, 1 - y), (1 - x, 1 - y)]
        barrier = pltpu.get_barrier_semaphore()
        for peer in [sibling] + [(cx, cy, c) for cx, cy in chips]:
            pl.semaphore_signal(barrier, inc=1, device_id=peer, device_id_type=MESH)
        pl.semaphore_wait(barrier, 1 + len(chips))

        def copy(k, slot, blk, to, src=None):
            place = outs[k].at[blk, 0]
            return pltpu.make_async_remote_copy(
                src_ref=place if src is None else src, dst_ref=place,
                send_sem=send_sems.at[k, slot], recv_sem=recv_sems.at[k, slot],
                device_id=to, device_id_type=MESH)

        local, sent = [], []
        for k in range(n):
            mine = ins[k].at[layers[k]]
            cp = pltpu.make_async_copy(mine, outs[k].at[me, 0], local_sems.at[k])
            cp.start()
            local.append(cp)
            sent.append(copy(k, 0, me, sibling, src=mine))
            sent += [copy(k, 1 + j, me, (cx, cy, c), src=mine) for j, (cx, cy) in enumerate(chips)]
        for cp in sent:
            cp.start()
        for k in range(n):
            for j, (cx, cy) in enumerate(chips):
                blk = 4 * cx + 2 * cy + c
                copy(k, 1 + j, blk, sibling).wait_recv()
                fwd = copy(k, 4 + j, blk, sibling)
                fwd.start()
                sent.append(fwd)
        for k in range(n):
            copy(k, 0, 4 * x + 2 * y + (1 - c), sibling).wait_recv()
            for j, (cx, cy) in enumerate(chips):
                copy(k, 4 + j, 4 * cx + 2 * cy + (1 - c), sibling).wait_recv()
        for cp in sent:
            cp.wait_send()
        for cp in local:
            cp.wait()

    return pl.kernel(
        body,
        out_type=outs_shape,
        mesh=plsc.ScalarSubcoreMesh(axis_name="sequencer", num_cores=1),
        name=name,
        scratch_types=[pltpu.SemaphoreType.DMA((n, N_PEERS)), pltpu.SemaphoreType.DMA((n, N_PEERS)),
                       pltpu.SemaphoreType.DMA((n,))],
        compiler_params=pltpu.CompilerParams(collective_id=collective_id),
    )(*srcs)


def all_gather(name, items, after=()):
    outs = [jax.ShapeDtypeStruct((NDEV, 1) + a.shape[1:], a.dtype) for a, _ in items]
    return _exchange(name, [a for a, _ in items], outs, [_gather_src(l) for _, l in items],
                     [_gather_dst] * len(items), after)


def _gather_src(layer):
    return lambda ref, blk: ref.at[layer]


def _gather_dst(ref, blk):
    return ref.at[blk, 0]


def _slice_of(ref, blk):
    return ref.at[blk]


def cast_bf16(name, w, tr_elems=512 * 1024):
    l, r, c = w.shape
    tr = _row_tile(r, tr_elems // c)
    spec = pl.BlockSpec((None, tr, c), lambda li, i: (li, i, 0))

    def body(w_ref, o_ref):
        o_ref[...] = w_ref[...].astype(BF16)

    return pl.pallas_call(
        body, name=name, grid=(l, r // tr), in_specs=[spec], out_specs=spec,
        out_shape=jax.ShapeDtypeStruct(w.shape, BF16),
        compiler_params=_params(("parallel", "parallel"), 6 * tr * c),
    )(w)


def _adamw_math(w, g, m, v):
    m = ADAM_B1 * m + (1.0 - ADAM_B1) * g
    v = ADAM_B2 * v + (1.0 - ADAM_B2) * (g * g)
    m_hat = m / (1.0 - ADAM_B1 ** ADAM_STEP)
    v_hat = v / (1.0 - ADAM_B2 ** ADAM_STEP)
    delta = -ADAM_LR * (m_hat / (jnp.sqrt(v_hat) + ADAM_EPS) + ADAM_WD * w)
    return delta, m, v


def _sum_blocks(ref):
    g = ref[0].astype(F32)
    for d in range(1, ref.shape[0]):
        g = g + ref[d].astype(F32)
    return g


def adamw_layer(name, recv, w, m, v, layer, stacked, tr_elems=128 * 1024):
    nd, r, c = recv.shape
    tr = _row_tile(r, tr_elems // c)
    r_spec = pl.BlockSpec((nd, tr, c), lambda i: (0, i, 0))
    w_spec = pl.BlockSpec((None, tr, c), lambda i: (layer, i, 0))
    if stacked is None:
        stacked = [lax.empty(w.shape, F32) for _ in range(4)]

    def body(r_ref, w_ref, m_ref, v_ref, g_in, d_in, m_in, v_in, g_out, d_out, m_out, v_out):
        g = _sum_blocks(r_ref)
        delta, m_new, v_new = _adamw_math(w_ref[...], g, m_ref[...], v_ref[...])
        g_out[...] = g
        d_out[...] = delta
        m_out[...] = m_new
        v_out[...] = v_new

    out = jax.ShapeDtypeStruct(w.shape, F32)
    return pl.pallas_call(
        body, name=name, grid=(r // tr,),
        in_specs=[r_spec, w_spec, w_spec, w_spec] + [pl.BlockSpec(memory_space=pl.ANY)] * 4,
        out_specs=[w_spec] * 4, out_shape=[out] * 4,
        input_output_aliases={4: 0, 5: 1, 6: 2, 7: 3},
        compiler_params=_params(("parallel",), tr * c * (2 * nd + 7 * 4)),
    )(recv, w, m, v, *stacked)


def small_update(name, part_g, tap_g, w_a, m_a, v_a, w_b, m_b, v_b):
    nd, rows, d = part_g.shape
    na = rows // SUBLANES
    nb, cb = w_b.shape

    def body(pg_ref, tg_ref, wa_ref, ma_ref, va_ref, wb_ref, mb_ref, vb_ref,
             ga_out, da_out, ma_out, va_out, gb_out, db_out, mb_out, vb_out, loss_out):
        ga = jnp.sum(_sum_blocks(pg_ref).reshape(na, SUBLANES, d), axis=1)
        delta, m_new, v_new = _adamw_math(wa_ref[...], ga, ma_ref[...], va_ref[...])
        ga_out[...] = ga
        da_out[...] = delta
        ma_out[...] = m_new
        va_out[...] = v_new
        loss_out[...] = jnp.broadcast_to(jnp.sum(ga[na - 1:na, :], axis=1, keepdims=True), loss_out.shape)
        gb = _sum_blocks(tg_ref)
        delta, m_new, v_new = _adamw_math(wb_ref[...], gb, mb_ref[...], vb_ref[...])
        gb_out[...] = gb
        db_out[...] = delta
        mb_out[...] = m_new
        vb_out[...] = v_new

    oa, ob = jax.ShapeDtypeStruct((na, d), F32), jax.ShapeDtypeStruct((nb, cb), F32)
    vm = pl.BlockSpec(memory_space=pltpu.VMEM)
    return pl.pallas_call(
        body, name=name, in_specs=[vm] * 8, out_specs=[vm] * 9,
        out_shape=[oa] * 4 + [ob] * 4 + [jax.ShapeDtypeStruct((1, LANES), F32)],
        compiler_params=pltpu.CompilerParams(vmem_limit_bytes=_vmem_limit(_nbytes(part_g.shape, F32))),
    )(part_g, tap_g, w_a, m_a, v_a, w_b, m_b, v_b)


BIG = ("cf_w_pw1", "cf_w_pw2", "sc_w_in", "sc_w_out", "mlp_w1", "mlp_w2", "ple_w_proj", "ple_w_gate")
COL_SHARDED = ("cf_w_pw1", "sc_w_in", "mlp_w1", "ple_w_proj")
WEIGHT_ORDER = ("norm_mix", "norm_mlp", "norm_ple", "cf_w_pw1", "cf_b_pw1", "cf_w_dw", "cf_b_dw", "cf_norm",
                "cf_w_pw2", "cf_b_pw2", "sc_w_in", "sc_w_conv", "sc_w_out", "mlp_w1", "mlp_w2", "ple_w_proj",
                "ple_w_gate", "norm_final")
SMALL_ROWS = (("norm_mix", 4), ("norm_mlp", 4), ("norm_ple", 4), ("cf_b_pw1", 4), ("cf_b_dw", 2), ("cf_norm", 2),
              ("cf_b_pw2", 2), ("norm_final", 1))


def _layer_weights(i):
    mixer = (("cf_w_pw1", i // 2), ("cf_w_pw2", i // 2)) if i % 2 == 0 else (("sc_w_in", i // 2), ("sc_w_out", i // 2))
    return mixer + (("mlp_w1", i), ("mlp_w2", i), ("ple_w_proj", i), ("ple_w_gate", i))


def _pad_rows(a, rows):
    return jnp.pad(a, ((0, 0), (0, rows - a.shape[1]), (0, 0)))


def _pack_taps(cf, sc):
    c = cf.shape[2]
    return jnp.concatenate([_pad_rows(cf, CONV_A_HALO).reshape(-1, c), _pad_rows(sc, CONV_B_HALO).reshape(-1, c)], axis=0)


def _unpack_taps(t, n_cf):
    c = t.shape[1]
    cf = t[:n_cf * CONV_A_HALO].reshape(n_cf, CONV_A_HALO, c)[:, :CONV_A_TAPS]
    sc = t[n_cf * CONV_A_HALO:].reshape(-1, CONV_B_HALO, c)[:, :CONV_B_TAPS]
    return cf, sc


def _pack_small(vals, d):
    return jnp.concatenate([vals[k].reshape(-1, d) for k, _ in SMALL_ROWS] + [jnp.zeros((1, d), F32)], axis=0)


def _unpack_small(a, shapes):
    out, r = {}, 0
    for k, n in SMALL_ROWS:
        out[k] = a[r:r + n].reshape(shapes[k])
        r += n
    return out


def kernel(x, p, norm_mix, norm_mlp, norm_ple, cf_w_pw1, cf_b_pw1, cf_w_dw, cf_b_dw, cf_norm, cf_w_pw2, cf_b_pw2, sc_w_in, sc_w_conv, sc_w_out, mlp_w1, mlp_w2, ple_w_proj, ple_w_gate, norm_final, loss_target, m_norm_mix, m_norm_mlp, m_norm_ple, m_cf_w_pw1, m_cf_b_pw1, m_cf_w_dw, m_cf_b_dw, m_cf_norm, m_cf_w_pw2, m_cf_b_pw2, m_sc_w_in, m_sc_w_conv, m_sc_w_out, m_mlp_w1, m_mlp_w2, m_ple_w_proj, m_ple_w_gate, m_norm_final, v_norm_mix, v_norm_mlp, v_norm_ple, v_cf_w_pw1, v_cf_b_pw1, v_cf_w_dw, v_cf_b_dw, v_cf_norm, v_cf_w_pw2, v_cf_b_pw2, v_sc_w_in, v_sc_w_conv, v_sc_w_out, v_mlp_w1, v_mlp_w2, v_ple_w_proj, v_ple_w_gate, v_norm_final):
    w = dict(norm_mix=norm_mix, norm_mlp=norm_mlp, norm_ple=norm_ple, cf_w_pw1=cf_w_pw1, cf_b_pw1=cf_b_pw1,
             cf_w_dw=cf_w_dw, cf_b_dw=cf_b_dw, cf_norm=cf_norm, cf_w_pw2=cf_w_pw2, cf_b_pw2=cf_b_pw2,
             sc_w_in=sc_w_in, sc_w_conv=sc_w_conv, sc_w_out=sc_w_out, mlp_w1=mlp_w1, mlp_w2=mlp_w2,
             ple_w_proj=ple_w_proj, ple_w_gate=ple_w_gate, norm_final=norm_final)
    m = dict(norm_mix=m_norm_mix, norm_mlp=m_norm_mlp, norm_ple=m_norm_ple, cf_w_pw1=m_cf_w_pw1, cf_b_pw1=m_cf_b_pw1,
             cf_w_dw=m_cf_w_dw, cf_b_dw=m_cf_b_dw, cf_norm=m_cf_norm, cf_w_pw2=m_cf_w_pw2, cf_b_pw2=m_cf_b_pw2,
             sc_w_in=m_sc_w_in, sc_w_conv=m_sc_w_conv, sc_w_out=m_sc_w_out, mlp_w1=m_mlp_w1, mlp_w2=m_mlp_w2,
             ple_w_proj=m_ple_w_proj, ple_w_gate=m_ple_w_gate, norm_final=m_norm_final)
    v = dict(norm_mix=v_norm_mix, norm_mlp=v_norm_mlp, norm_ple=v_norm_ple, cf_w_pw1=v_cf_w_pw1, cf_b_pw1=v_cf_b_pw1,
             cf_w_dw=v_cf_w_dw, cf_b_dw=v_cf_b_dw, cf_norm=v_cf_norm, cf_w_pw2=v_cf_w_pw2, cf_b_pw2=v_cf_b_pw2,
             sc_w_in=v_sc_w_in, sc_w_conv=v_sc_w_conv, sc_w_out=v_sc_w_out, mlp_w1=v_mlp_w1, mlp_w2=v_mlp_w2,
             ple_w_proj=v_ple_w_proj, ple_w_gate=v_ple_w_gate, norm_final=v_norm_final)
    depth, d = norm_mix.shape
    n_cf = cf_w_dw.shape[0]
    me = 4 * lax.axis_index("x") + 2 * lax.axis_index("y") + lax.axis_index("c")

    taps_w = _pack_taps(cf_w_dw, sc_w_conv)
    shards = {k: cast_bf16(f"cast_{k}", w[k]) for k in BIG}
    gathered = {}
    ids = iter(range(4 * depth))
    for i in range(depth):
        layer_names = _layer_weights(i)
        for part, names in (("mixer", layer_names[:2]), ("mlp", layer_names[2:])):
            first = i == 0 and part == "mixer"
            srcs = [shards[k] for k, _ in names] + ([taps_w[None]] if first else [])
            got = sc_gather(f"gather_{part}_{i}", next(ids), srcs, [l for _, l in names] + ([0] if first else []))
            for (k, l), g in zip(names, got):
                gathered[(k, l)] = g if k in COL_SHARDED else g.reshape(-1, g.shape[3])
            if first:
                taps_full = jnp.transpose(got[-1][:, 0], (1, 0, 2)).reshape(taps_w.shape[0], d)
    conv_w = {"cf": taps_full[:n_cf * CONV_A_HALO].reshape(n_cf, CONV_A_HALO, d),
              "sc": taps_full[n_cf * CONV_A_HALO:].reshape(-1, CONV_B_HALO, d)}

    def get_w(k, l, after=None):
        return gathered[(k, l)]

    received = {}

    def put_grads(grads):
        names = list(grads)
        got = sc_exchange(f"grad_exchange_{names[0][0]}_{names[0][1]}", next(ids), [grads[n] for n in names],
                          [jax.ShapeDtypeStruct(grads[n].shape, BF16) for n in names],
                          [_slice_of] * len(names), [_slice_of] * len(names))
        received.update(zip(names, got))
        return None

    small = {k: w[k] for k, _ in SMALL_ROWS}
    small["norm_final"] = norm_final[None]
    loss_part, grad_x, sg = _local_step(x[0], p[:, 0], loss_target[0], small, get_w, conv_w, put_grads)

    out = {k: None for k in BIG}
    last_group = [n for n in _layer_weights(0)[:2]]
    for (k, l), recv in received.items():
        if (k, l) not in last_group:
            out[k] = adamw_layer(f"adamw_{k}_{l}", recv, w[k], m[k], v[k], l, out[k])
    updated_first = [out[k][0] for k in BIG if out[k] is not None and k not in [n for n, _ in last_group]]

    parts = []
    for k, n in SMALL_ROWS:
        for g in sg[k]:
            parts += [g[:, :d], g[:, d:]] if g.shape[1] == 2 * d else [g]
    parts.append(loss_part)
    part_pack = jnp.concatenate(parts, axis=0)
    tap_pack = jnp.concatenate(sg["cf_w_dw"] + sg["sc_w_conv"], axis=0)
    part_all, tap_all = all_gather("ag_small", [(part_pack[None], 0), (tap_pack[None], 0)], after=updated_first)
    for k, l in last_group:
        out[k] = adamw_layer(f"adamw_{k}_{l}", received[(k, l)], w[k], m[k], v[k], l, out[k])
    cb = cf_w_dw.shape[2]
    tap_mine = lax.dynamic_slice_in_dim(tap_all[:, 0], me * cb, cb, axis=2)
    sm = small_update("small_update", part_all[:, 0], tap_mine,
                      _pack_small(w, d), _pack_small(m, d), _pack_small(v, d),
                      taps_w, _pack_taps(m["cf_w_dw"], m["sc_w_conv"]), _pack_taps(v["cf_w_dw"], v["sc_w_conv"]))
    shapes = {k: w[k].shape for k, _ in SMALL_ROWS}
    for t in range(4):
        un = _unpack_small(sm[t], shapes)
        cf_t, sc_t = _unpack_taps(sm[4 + t], n_cf)
        for k in un:
            out.setdefault(k, [None] * 4)[t] = un[k]
        out.setdefault("cf_w_dw", [None] * 4)[t] = cf_t
        out.setdefault("sc_w_conv", [None] * 4)[t] = sc_t
    loss = sm[8][0, 0]

    return (loss, grad_x[None], *[out[k][0] for k in WEIGHT_ORDER], *[out[k][1] for k in WEIGHT_ORDER],
            *[out[k][2] for k in WEIGHT_ORDER], *[out[k][3] for k in WEIGHT_ORDER])
```

```python
import jax
import jax.numpy as jnp
from jax import lax
from jax.experimental import pallas as pl
from jax.experimental.pallas import tpu as pltpu
from jax.experimental.pallas import tpu_sc as plsc

F32 = jnp.float32
BF16 = jnp.bfloat16
EPS = 1e-6
NDEV = 8
N_PEERS = NDEV - 1
MESH = pl.DeviceIdType.MESH

ADAM_LR = 0.001
ADAM_B1 = 0.9
ADAM_B2 = 0.999
ADAM_EPS = 1e-08
ADAM_WD = 0.01
ADAM_STEP = 10

V7X_VMEM_BYTES = 64 * 1024 * 1024
VMEM_LIMIT_MAX = 56 * 1024 * 1024
SUBLANES = 8
LANES = 128
CONV_A_TAPS = 31
CONV_A_HALO = 32
CONV_B_TAPS = 3
CONV_B_HALO = 8


def _nbytes(shape, dtype):
    n = 1
    for s in shape:
        if s is not None:
            n *= s
    return n * jnp.dtype(dtype).itemsize


def _vmem_limit(block_bytes, scratch_bytes=0):
    need = 2 * block_bytes + scratch_bytes
    return int(min(VMEM_LIMIT_MAX, max(32 * 1024 * 1024, need + need // 2 + (4 << 20))))


def _in_hbm(arrays):
    return [pltpu.with_memory_space_constraint(a, pltpu.HBM) for a in arrays]


def _params(sem, block_bytes, scratch_bytes=0):
    return pltpu.CompilerParams(dimension_semantics=sem, vmem_limit_bytes=_vmem_limit(block_bytes, scratch_bytes))


_DIMS = {
    "nn": (((1,), (0,)), ((), ())),
    "nt": (((1,), (1,)), ((), ())),
    "tn": (((0,), (0,)), ((), ())),
}


def _mm(name, dims, grid, acc_shape, a, a_spec, b, b_spec, extras, outs, epi):
    ni, nj, nk = grid
    n_ex, n_out = len(extras), len(outs)
    dn = _DIMS[dims]

    def body(*refs):
        a_ref, b_ref = refs[0], refs[1]
        ex_refs = refs[2:2 + n_ex]
        out_refs = refs[2 + n_ex:2 + n_ex + n_out]
        d = lax.dot_general(a_ref[...].astype(BF16), b_ref[...].astype(BF16), dn, preferred_element_type=F32)

        def finish(acc):
            res = epi(acc, *[r[...] for r in ex_refs])
            for o_ref, r in zip(out_refs, res):
                o_ref[...] = r.astype(o_ref.dtype)

        if nk == 1:
            finish(d)
        else:
            acc_ref = refs[2 + n_ex + n_out]
            k = pl.program_id(2)

            @pl.when(k == 0)
            def _():
                acc_ref[...] = d

            @pl.when(jnp.logical_and(k > 0, k < nk - 1))
            def _():
                acc_ref[...] += d

            @pl.when(k == nk - 1)
            def _():
                finish(acc_ref[...] + d)

    blk = _nbytes(a_spec.block_shape, a.dtype) + _nbytes(b_spec.block_shape, b.dtype)
    for arr, spec in list(extras) + list(outs):
        blk += _nbytes(spec.block_shape, arr.dtype)
    acc_bytes = _nbytes(acc_shape, F32)
    scratch = [pltpu.VMEM(acc_shape, F32)] if nk > 1 else []
    return pl.pallas_call(
        body,
        name=name,
        grid=grid,
        in_specs=[a_spec, b_spec] + [s for _, s in extras],
        out_specs=[s for _, s in outs],
        out_shape=[o for o, _ in outs],
        scratch_shapes=scratch,
        compiler_params=_params(("parallel", "parallel", "arbitrary"), blk, 3 * acc_bytes),
    )(*_in_hbm([a, b] + [e for e, _ in extras]))


def _tile(n, pref):
    if n <= pref:
        return n
    t = pref - pref % LANES
    while t > LANES and n % t:
        t -= LANES
    assert n % t == 0, (n, pref)
    return t


def _row_tile(n, pref):
    if n <= pref:
        return n
    t = max(SUBLANES, pref - pref % SUBLANES)
    while t > SUBLANES and n % t:
        t -= SUBLANES
    assert n % t == 0, (n, pref)
    return t


def _id_epi(acc):
    return (acc,)


def mm_x_wcol(name, x, w, layer, extras=(), outs_dtypes=(F32,), epi=_id_epi, tm=1024, tn=512):
    m, kdim = x.shape
    c = w.shape[3]
    n = NDEV * c
    tm, tn = _tile(m, tm), _tile(c, tn)
    tk = _tile(kdim, 2048)
    grid = (m // tm, n // tn, kdim // tk)
    per = c // tn
    a_spec = pl.BlockSpec((tm, tk), lambda i, j, k: (i, k))
    b_spec = pl.BlockSpec((None, None, tk, tn), lambda i, j, k: (j // per, layer, k, j % per))
    ex = [(e, _ex_spec(e, kind, tm, tn)) for e, kind in extras]
    o_spec = pl.BlockSpec((tm, tn), lambda i, j, k: (i, j))
    outs = [(jax.ShapeDtypeStruct((m, n), dt), o_spec) for dt in outs_dtypes]
    return _mm(name, "nn", grid, (tm, tn), x, a_spec, w, b_spec, ex, outs, epi)


def mm_x_wrow(name, x, w, extras=(), outs_dtypes=(F32,), epi=_id_epi, tm=1024, tn=512):
    m, kdim = x.shape
    n = w.shape[1]
    assert kdim == w.shape[0]
    tm, tn = _tile(m, tm), _tile(n, tn)
    tk = _tile(kdim, 2048)
    grid = (m // tm, n // tn, kdim // tk)
    a_spec = pl.BlockSpec((tm, tk), lambda i, j, k: (i, k))
    b_spec = pl.BlockSpec((tk, tn), lambda i, j, k: (k, j))
    ex = [(e, _ex_spec(e, kind, tm, tn)) for e, kind in extras]
    o_spec = pl.BlockSpec((tm, tn), lambda i, j, k: (i, j))
    outs = [(jax.ShapeDtypeStruct((m, n), dt), o_spec) for dt in outs_dtypes]
    return _mm(name, "nn", grid, (tm, tn), x, a_spec, w, b_spec, ex, outs, epi)


def mm_dy_wcol_t(name, dy, w, layer, extras=(), outs_dtypes=(F32,), epi=_id_epi, tm=1024, tn=1024):
    m, n = dy.shape
    kdim, c = w.shape[2], w.shape[3]
    assert n == NDEV * c
    tm, tn = _tile(m, tm), _tile(kdim, tn)
    tk = _tile(c, 2048)
    per = c // tk
    grid = (m // tm, kdim // tn, n // tk)
    a_spec = pl.BlockSpec((tm, tk), lambda i, j, k: (i, k))
    b_spec = pl.BlockSpec((None, None, tn, tk), lambda i, j, k: (k // per, layer, j, k % per))
    ex = [(e, _ex_spec(e, kind, tm, tn)) for e, kind in extras]
    o_spec = pl.BlockSpec((tm, tn), lambda i, j, k: (i, j))
    outs = [(jax.ShapeDtypeStruct((m, kdim), dt), o_spec) for dt in outs_dtypes]
    return _mm(name, "nt", grid, (tm, tn), dy, a_spec, w, b_spec, ex, outs, epi)


def mm_dy_wrow_t(name, dy, w, extras=(), outs_dtypes=(F32,), epi=_id_epi, tm=1024, tn=512):
    m, n = dy.shape
    kdim = w.shape[0]
    assert n == w.shape[1]
    tm, tn = _tile(m, tm), _tile(kdim, tn)
    tk = _tile(n, 2048)
    grid = (m // tm, kdim // tn, n // tk)
    a_spec = pl.BlockSpec((tm, tk), lambda i, j, k: (i, k))
    b_spec = pl.BlockSpec((tn, tk), lambda i, j, k: (j, k))
    ex = [(e, _ex_spec(e, kind, tm, tn)) for e, kind in extras]
    o_spec = pl.BlockSpec((tm, tn), lambda i, j, k: (i, j))
    outs = [(jax.ShapeDtypeStruct((m, kdim), dt), o_spec) for dt in outs_dtypes]
    return _mm(name, "nt", grid, (tm, tn), dy, a_spec, w, b_spec, ex, outs, epi)


def mm_xt_dy(name, x, dy, col_shards, tm=1024, tn=512):
    m, kdim = x.shape
    n = dy.shape[1]
    tk = _tile(m, 2048)
    if col_shards:
        c = n // NDEV
        tm, tn = _tile(kdim, tm), _tile(c, tn)
        per = c // tn
        out = jax.ShapeDtypeStruct((NDEV, kdim, c), BF16)
        o_spec = pl.BlockSpec((None, tm, tn), lambda i, j, k: (j // per, i, j % per))
    else:
        tm, tn = _tile(kdim, tm), _tile(n, tn)
        out = jax.ShapeDtypeStruct((kdim, n), BF16)
        o_spec = pl.BlockSpec((tm, tn), lambda i, j, k: (i, j))
    grid = (kdim // tm, n // tn, m // tk)
    a_spec = pl.BlockSpec((tk, tm), lambda i, j, k: (k, i))
    b_spec = pl.BlockSpec((tk, tn), lambda i, j, k: (k, j))
    g = _mm(name, "tn", grid, (tm, tn), x, a_spec, dy, b_spec, [], [(out, o_spec)], _id_epi)[0]
    return g if col_shards else g.reshape(NDEV, kdim // NDEV, n)


def _ex_spec(e, kind, tm, tn):
    if kind == "tile":
        return pl.BlockSpec((tm, tn), lambda i, j, k: (i, j))
    if kind == "row":
        return pl.BlockSpec((1, tn), lambda i, j, k: (0, j))
    raise ValueError(kind)


def _rows_call(name, body, n_rows, ts, ins, outs, scratch=(), scratch_bytes=0):
    blk = sum(_nbytes(s.block_shape, a.dtype) for a, s in list(ins) + list(outs))
    return pl.pallas_call(
        body,
        name=name,
        grid=(n_rows // ts,),
        in_specs=[s for _, s in ins],
        out_specs=[s for _, s in outs],
        out_shape=[o for o, _ in outs],
        scratch_shapes=list(scratch),
        compiler_params=_params(("arbitrary",), blk, scratch_bytes + 4 * blk // 2),
    )(*_in_hbm([a for a, _ in ins]))


def _blk(ts, d):
    return pl.BlockSpec((ts, d), lambda i: (i, 0))


def _full(shape):
    return pl.BlockSpec(shape, lambda i: tuple(0 for _ in shape))


def _rowsum8(v):
    t, d = v.shape
    return jnp.sum(v.reshape(t // SUBLANES, SUBLANES, d), axis=0)


def _accumulate(ref, val):
    @pl.when(pl.program_id(0) == 0)
    def _():
        ref[...] = val

    @pl.when(pl.program_id(0) > 0)
    def _():
        ref[...] += val


def _rstd(x):
    return lax.rsqrt(jnp.mean(x * x, axis=-1, keepdims=True) + EPS)


def _rms_bwd_math(dy, x, g):
    r = _rstd(x)
    gdy = dy * g
    c = jnp.mean(gdy * x, axis=-1, keepdims=True)
    dx = r * gdy - x * (r * r * r * c)
    return dx, dy * (x * r)


def rms_fwd(name, h, g, ts=512):
    s, d = h.shape
    ts = min(ts, s)

    def body(h_ref, g_ref, u_ref):
        x = h_ref[...]
        u_ref[...] = ((x * _rstd(x)) * g_ref[...]).astype(BF16)

    return _rows_call(name, body, s, ts, [(h, _blk(ts, d)), (g, _full((1, d)))],
                      [(jax.ShapeDtypeStruct((s, d), BF16), _blk(ts, d))])[0]


def rms_bwd(name, du, h, g, dres, ts=256):
    s, d = h.shape
    ts = min(ts, s)

    def body(du_ref, h_ref, g_ref, dres_ref, dh_ref, dhb_ref, dg_ref, cs_ref):
        dx, dg = _rms_bwd_math(du_ref[...], h_ref[...], g_ref[...])
        dh = dres_ref[...] + dx
        dh_ref[...] = dh
        dhb_ref[...] = dh.astype(BF16)
        _accumulate(dg_ref, _rowsum8(dg))
        _accumulate(cs_ref, _rowsum8(dh))

    return _rows_call(
        name, body, s, ts,
        [(du, _blk(ts, d)), (h, _blk(ts, d)), (g, _full((1, d))), (dres, _blk(ts, d))],
        [(jax.ShapeDtypeStruct((s, d), F32), _blk(ts, d)), (jax.ShapeDtypeStruct((s, d), BF16), _blk(ts, d)),
         (jax.ShapeDtypeStruct((SUBLANES, d), F32), _full((SUBLANES, d))),
         (jax.ShapeDtypeStruct((SUBLANES, d), F32), _full((SUBLANES, d)))])


def final_loss(name, h, g, target, ts=256):
    s, d = h.shape
    ts = min(ts, s)

    def body(h_ref, g_ref, t_ref, loss_ref, dh_ref, dg_ref):
        x = h_ref[...]
        gf = g_ref[...]
        y = (x * _rstd(x)) * gf
        err = y - t_ref[...]
        _accumulate(loss_ref, _rowsum8(err * err) * (0.5 / d))
        dx, dg = _rms_bwd_math(err * (1.0 / d), x, gf)
        dh_ref[...] = dx
        _accumulate(dg_ref, _rowsum8(dg))

    return _rows_call(
        name, body, s, ts,
        [(h, _blk(ts, d)), (g, _full((1, d))), (target, _blk(ts, d))],
        [(jax.ShapeDtypeStruct((SUBLANES, d), F32), _full((SUBLANES, d))),
         (jax.ShapeDtypeStruct((s, d), F32), _blk(ts, d)),
         (jax.ShapeDtypeStruct((SUBLANES, d), F32), _full((SUBLANES, d)))])


def ple_bwd_elem(name, dh, g, e, ts=512):
    s, d = dh.shape
    ts = min(ts, s)

    def body(dh_ref, g_ref, e_ref, de_ref, dgl_ref):
        dh_v, g_v = dh_ref[...], g_ref[...]
        de_ref[...] = (dh_v * g_v).astype(BF16)
        dgl_ref[...] = (dh_v * e_ref[...] * (g_v * (1.0 - g_v))).astype(BF16)

    return _rows_call(name, body, s, ts, [(dh, _blk(ts, d)), (g, _blk(ts, d)), (e, _blk(ts, d))],
                      [(jax.ShapeDtypeStruct((s, d), BF16), _blk(ts, d)),
                       (jax.ShapeDtypeStruct((s, d), BF16), _blk(ts, d))])


CONV_LANES = 256
CONV_ROWS = 64


def _lane_chunks(d, fn):
    lc = min(CONV_LANES, d)

    def lane_body(c, carry):
        fn(pl.ds(pl.multiple_of(c * lc, lc), lc))
        return carry

    lax.fori_loop(0, d // lc, lane_body, 0)


def _shifted_copies(buf, sh, lanes):
    rows = buf.shape[0] - SUBLANES
    for s in range(1, SUBLANES):
        sh[s, pl.ds(0, rows), :] = buf[pl.ds(s, rows), lanes]


def _window(buf, sh, lanes, start, rows):
    s = start % SUBLANES
    if s == 0:
        return buf[pl.ds(start, rows), lanes]
    return sh[s, pl.ds(start - s, rows), :]


def _prev_halo_spec(ts, halo, width):
    per = ts // halo
    return pl.BlockSpec((halo, width), lambda i: (jnp.maximum(i * per - 1, 0), 0))


def _next_halo_spec(ts, halo, width, n_rows):
    per = ts // halo
    last = n_rows // halo - 1
    return pl.BlockSpec((halo, width), lambda i: (jnp.minimum((i + 1) * per, last), 0))


def cf_fwd_mid(name, a, w_dw, b_dw, gn, ts=256):
    s, d2 = a.shape
    d = d2 // 2
    ts = min(ts, s)
    hl = CONV_A_HALO
    off = hl - (CONV_A_TAPS - 1)

    rc = min(CONV_ROWS, ts)

    def body(a_ref, ah_ref, w_ref, b_ref, gn_ref, v0_ref, v1_ref, v3_ref, buf, sh):
        first = pl.program_id(0) == 0
        halo = ah_ref[...]
        hv0 = halo[:, :d] * jax.nn.sigmoid(halo[:, d:])
        buf[pl.ds(0, hl), :] = jnp.where(first, 0.0, hv0)
        main = a_ref[...]
        v0 = main[:, :d] * jax.nn.sigmoid(main[:, d:])
        buf[pl.ds(hl, ts), :] = v0
        v0_ref[...] = v0

        def conv(lanes):
            _shifted_copies(buf, sh, lanes)
            for r0 in range(0, ts, rc):
                acc = jnp.zeros((rc, lanes.size), F32)
                for k in range(CONV_A_TAPS):
                    acc = acc + w_ref[pl.ds(k, 1), lanes] * _window(buf, sh, lanes, r0 + off + k, rc)
                v1_ref[pl.ds(r0, rc), lanes] = acc + b_ref[:, lanes]

        _lane_chunks(d, conv)
        v1 = v1_ref[...]
        v2 = (v1 * _rstd(v1)) * gn_ref[...]
        v3_ref[...] = (v2 * jax.nn.sigmoid(v2)).astype(BF16)

    return _rows_call(
        name, body, s, ts,
        [(a, _blk(ts, d2)), (a, _prev_halo_spec(ts, hl, d2)), (w_dw, _full(w_dw.shape)),
         (b_dw, _full((1, d))), (gn, _full((1, d)))],
        [(jax.ShapeDtypeStruct((s, d), F32), _blk(ts, d)), (jax.ShapeDtypeStruct((s, d), F32), _blk(ts, d)),
         (jax.ShapeDtypeStruct((s, d), BF16), _blk(ts, d))],
        scratch=[pltpu.VMEM((hl + ts, d), F32), pltpu.VMEM((SUBLANES, hl + ts, min(CONV_LANES, d)), F32)],
        scratch_bytes=_nbytes((hl + ts, d + SUBLANES * CONV_LANES), F32))


def cf_bwd_rows(name, dv3, v1, gn, ts=256):
    s, d = v1.shape
    ts = min(ts, s)

    def body(dv3_ref, v1_ref, gn_ref, dv1_ref, dgn_ref, db_ref):
        v1 = v1_ref[...]
        gn_v = gn_ref[...]
        v2 = (v1 * _rstd(v1)) * gn_v
        sg = jax.nn.sigmoid(v2)
        dv2 = dv3_ref[...] * (sg * (1.0 + v2 * (1.0 - sg)))
        dv1, dgn = _rms_bwd_math(dv2, v1, gn_v)
        dv1_ref[...] = dv1
        _accumulate(dgn_ref, _rowsum8(dgn))
        _accumulate(db_ref, _rowsum8(dv1))

    return _rows_call(
        name, body, s, ts, [(dv3, _blk(ts, d)), (v1, _blk(ts, d)), (gn, _full((1, d)))],
        [(jax.ShapeDtypeStruct((s, d), F32), _blk(ts, d)),
         (jax.ShapeDtypeStruct((SUBLANES, d), F32), _full((SUBLANES, d))),
         (jax.ShapeDtypeStruct((SUBLANES, d), F32), _full((SUBLANES, d)))])


def cf_bwd_conv(name, dv1, v0, a, w_dw, ts=256):
    s, d = dv1.shape
    ts = min(ts, s)
    hl = CONV_A_HALO
    taps = CONV_A_TAPS
    off = hl - (taps - 1)
    last_blk = s // ts - 1

    rc = min(CONV_ROWS, ts)

    def body(dv1_ref, dv1n_ref, v0_ref, v0p_ref, a_ref, w_ref, da_ref, dw_ref, db_ref,
             dbuf, vbuf, dv0_buf, dw_acc, dsh, vsh):
        i = pl.program_id(0)
        dbuf[pl.ds(0, ts), :] = dv1_ref[...]
        dbuf[pl.ds(ts, hl), :] = jnp.where(i == last_blk, 0.0, dv1n_ref[...])
        vbuf[pl.ds(0, hl), :] = jnp.where(i == 0, 0.0, v0p_ref[...])
        vbuf[pl.ds(hl, ts), :] = v0_ref[...]

        @pl.when(i == 0)
        def _():
            dw_acc[...] = jnp.zeros_like(dw_acc)

        def conv_t(lanes):
            _shifted_copies(dbuf, dsh, lanes)
            _shifted_copies(vbuf, vsh, lanes)
            for r0 in range(0, ts, rc):
                g = dbuf[pl.ds(r0, rc), lanes]
                acc = jnp.zeros((rc, lanes.size), F32)
                for k in range(taps):
                    acc = acc + w_ref[pl.ds(k, 1), lanes] * _window(dbuf, dsh, lanes, r0 + taps - 1 - k, rc)
                    prod = g * _window(vbuf, vsh, lanes, r0 + off + k, rc)
                    dw_acc[pl.ds(k * SUBLANES, SUBLANES), lanes] += _rowsum8(prod)
                dv0_buf[pl.ds(r0, rc), lanes] = acc

        _lane_chunks(d, conv_t)
        dv0 = dv0_buf[...]
        av = a_ref[...]
        val, sg = av[:, :d], jax.nn.sigmoid(av[:, d:])
        dval = dv0 * sg
        dgate = dv0 * val * (sg * (1.0 - sg))
        da_ref[:, :d] = dval.astype(BF16)
        da_ref[:, d:] = dgate.astype(BF16)
        _accumulate(db_ref.at[:, pl.ds(0, d)], _rowsum8(dval))
        _accumulate(db_ref.at[:, pl.ds(d, d)], _rowsum8(dgate))

        @pl.when(i == last_blk)
        def _():
            dw_ref[...] = jnp.sum(dw_acc[...].reshape(hl, SUBLANES, d), axis=1)

    lc = min(CONV_LANES, d)
    scratch = [pltpu.VMEM((ts + hl, d), F32), pltpu.VMEM((hl + ts, d), F32), pltpu.VMEM((ts, d), F32),
               pltpu.VMEM((hl * SUBLANES, d), F32), pltpu.VMEM((SUBLANES, ts + hl, lc), F32),
               pltpu.VMEM((SUBLANES, hl + ts, lc), F32)]
    sbytes = _nbytes((3 * ts + 2 * hl + hl * SUBLANES, d), F32) + 2 * _nbytes((SUBLANES, ts + hl, lc), F32)
    return _rows_call(
        name, body, s, ts,
        [(dv1, _blk(ts, d)), (dv1, _next_halo_spec(ts, hl, d, s)), (v0, _blk(ts, d)), (v0, _prev_halo_spec(ts, hl, d)),
         (a, _blk(ts, 2 * d)), (w_dw, _full(w_dw.shape))],
        [(jax.ShapeDtypeStruct((s, 2 * d), BF16), _blk(ts, 2 * d)),
         (jax.ShapeDtypeStruct((hl, d), F32), _full((hl, d))),
         (jax.ShapeDtypeStruct((SUBLANES, 2 * d), F32), _full((SUBLANES, 2 * d)))],
        scratch=scratch, scratch_bytes=sbytes)


def sc_fwd_mid(name, bcv, w_conv, ts=256):
    s, d3 = bcv.shape
    d = d3 // 3
    ts = min(ts, s)
    hl = CONV_B_HALO
    off = hl - (CONV_B_TAPS - 1)

    def body(x_ref, xp_ref, w_ref, y_ref, buf):
        hp = xp_ref[...]
        buf[pl.ds(0, hl), :] = jnp.where(pl.program_id(0) == 0, 0.0, hp[:, d:2 * d] * hp[:, 2 * d:])
        buf[pl.ds(hl, ts), :] = x_ref[:, d:2 * d] * x_ref[:, 2 * d:]
        cc = jnp.zeros((ts, d), F32)
        for k in range(CONV_B_TAPS):
            cc = cc + w_ref[pl.ds(k, 1), :] * buf[pl.ds(off + k, ts), :]
        y_ref[...] = (x_ref[:, :d] * cc).astype(BF16)

    return _rows_call(
        name, body, s, ts,
        [(bcv, _blk(ts, d3)), (bcv, _prev_halo_spec(ts, hl, d3)), (w_conv, _full(w_conv.shape))],
        [(jax.ShapeDtypeStruct((s, d), BF16), _blk(ts, d))],
        scratch=[pltpu.VMEM((hl + ts, d), F32)], scratch_bytes=_nbytes((hl + ts, d), F32))[0]


def sc_bwd_mid(name, dy, bcv, w_conv, ts=256):
    s, d3 = bcv.shape
    d = d3 // 3
    ts = min(ts, s)
    hl = CONV_B_HALO
    taps = CONV_B_TAPS
    off = hl - (taps - 1)
    last_blk = s // ts - 1

    def body(dy_ref, dyn_ref, x_ref, xp_ref, xn_ref, w_ref, dx_ref, dw_ref, cvbuf, dbuf, dw_acc):
        i = pl.program_id(0)
        hp = xp_ref[...]
        cvbuf[pl.ds(0, hl), :] = jnp.where(i == 0, 0.0, hp[:, d:2 * d] * hp[:, 2 * d:])
        gb, gc, v = x_ref[:, :d], x_ref[:, d:2 * d], x_ref[:, 2 * d:]
        cvbuf[pl.ds(hl, ts), :] = gc * v
        dy_v = dy_ref[...]
        dcc = dy_v * gb
        dbuf[pl.ds(0, ts), :] = dcc
        dbuf[pl.ds(ts, hl), :] = jnp.where(i == last_blk, 0.0, dyn_ref[...] * xn_ref[:, :d])

        @pl.when(i == 0)
        def _():
            dw_acc[...] = jnp.zeros_like(dw_acc)

        cc = jnp.zeros((ts, d), F32)
        dcv = jnp.zeros((ts, d), F32)
        for k in range(taps):
            win = cvbuf[pl.ds(off + k, ts), :]
            cc = cc + w_ref[pl.ds(k, 1), :] * win
            dcv = dcv + w_ref[pl.ds(k, 1), :] * dbuf[pl.ds(taps - 1 - k, ts), :]
            dw_acc[pl.ds(k * SUBLANES, SUBLANES), :] += _rowsum8(dcc * win)
        dx_ref[:, :d] = (dy_v * cc).astype(BF16)
        dx_ref[:, d:2 * d] = (dcv * v).astype(BF16)
        dx_ref[:, 2 * d:] = (dcv * gc).astype(BF16)

        @pl.when(i == last_blk)
        def _():
            dw_ref[...] = jnp.sum(dw_acc[...].reshape(hl, SUBLANES, d), axis=1)

    scratch = [pltpu.VMEM((hl + ts, d), F32), pltpu.VMEM((ts + hl, d), F32), pltpu.VMEM((hl * SUBLANES, d), F32)]
    sbytes = _nbytes((2 * ts + 2 * hl + hl * SUBLANES, d), F32)
    return _rows_call(
        name, body, s, ts,
        [(dy, _blk(ts, d)), (dy, _next_halo_spec(ts, hl, d, s)), (bcv, _blk(ts, d3)), (bcv, _prev_halo_spec(ts, hl, d3)),
         (bcv, _next_halo_spec(ts, hl, d3, s)), (w_conv, _full(w_conv.shape))],
        [(jax.ShapeDtypeStruct((s, d3), BF16), _blk(ts, d3)), (jax.ShapeDtypeStruct((hl, d), F32), _full((hl, d)))],
        scratch=scratch, scratch_bytes=sbytes)


def _row(a, i):
    return lax.slice_in_dim(a, i, i + 1, axis=0)


def _local_step(x, p, target, small, get_w, conv_w, put_grads):
    depth = p.shape[0]
    acts = []
    h = x
    for i in range(depth):
        j = i // 2
        act = {"h": h}
        u = rms_fwd(f"rms_mix_{i}", h, _row(small["norm_mix"], i))
        act["u"] = u
        if i % 2 == 0:
            a = mm_x_wcol(f"cf_pw1_{i}", u, get_w("cf_w_pw1", j, u), 0, extras=[(_row(small["cf_b_pw1"], j), "row")],
                          epi=lambda acc, b: (acc + b,))[0]
            v0, v1, v3 = cf_fwd_mid(f"cf_mid_{i}", a, conv_w["cf"][j], _row(small["cf_b_dw"], j), _row(small["cf_norm"], j))
            act.update(a=a, v0=v0, v1=v1, v3=v3)
            h1 = mm_x_wrow(f"cf_pw2_{i}", v3, get_w("cf_w_pw2", j, v3),
                           extras=[(_row(small["cf_b_pw2"], j), "row"), (h, "tile")],
                           epi=lambda acc, b, res: (res + (acc + b),))[0]
        else:
            bcv = mm_x_wcol(f"sc_in_{i}", u, get_w("sc_w_in", j, u), 0, tn=768)[0]
            y = sc_fwd_mid(f"sc_mid_{i}", bcv, conv_w["sc"][j])
            act.update(bcv=bcv, y=y)
            h1 = mm_x_wrow(f"sc_out_{i}", y, get_w("sc_w_out", j, y), extras=[(h, "tile")],
                           epi=lambda acc, res: (res + acc,))[0]
        act["h1"] = h1
        u2 = rms_fwd(f"rms_mlp_{i}", h1, _row(small["norm_mlp"], i))
        z, hd = mm_x_wcol(f"mlp_w1_{i}", u2, get_w("mlp_w1", i, u2), 0, outs_dtypes=(F32, BF16),
                          epi=lambda acc: (acc, jnp.square(jnp.maximum(acc, 0.0))))
        h2 = mm_x_wrow(f"mlp_w2_{i}", hd, get_w("mlp_w2", i, hd), extras=[(h1, "tile")],
                       epi=lambda acc, res: (res + acc,), tn=1024)[0]
        act.update(u2=u2, z=z, hd=hd, h2=h2)
        n3 = rms_fwd(f"rms_ple_{i}", h2, _row(small["norm_ple"], i))
        e = mm_x_wcol(f"ple_proj_{i}", p[i], get_w("ple_w_proj", i, n3), 0)[0]

        def ple_epi(acc, e_t, res):
            g_t = jax.nn.sigmoid(acc)
            return g_t, res + g_t * e_t

        g, h3 = mm_x_wrow(f"ple_gate_{i}", n3, get_w("ple_w_gate", i, e), extras=[(e, "tile"), (h2, "tile")],
                          outs_dtypes=(F32, F32), epi=ple_epi)
        act.update(n3=n3, e=e, g=g)
        acts.append(act)
        h = h3

    loss_part, dh, dg_final = final_loss("final_loss", h, small["norm_final"], target)
    sg = {k: [None] * small[k].shape[0] for k in small if k != "norm_final"}
    sg["norm_final"] = [dg_final]
    sg["cf_w_dw"] = [None] * conv_w["cf"].shape[0]
    sg["sc_w_conv"] = [None] * conv_w["sc"].shape[0]

    for i in reversed(range(depth)):
        j = i // 2
        act = acts[i]
        de, dgl = ple_bwd_elem(f"ple_bwd_{i}", dh, act["g"], act["e"])
        g_proj = mm_xt_dy(f"d_ple_proj_{i}", p[i], de, True)
        g_gate = mm_xt_dy(f"d_ple_gate_{i}", act["n3"], dgl, False)
        dn3 = mm_dy_wrow_t(f"dn3_{i}", dgl, get_w("ple_w_gate", i))[0]
        dh2, dh2b, sg["norm_ple"][i], _ = rms_bwd(f"rms_ple_bwd_{i}", dn3, act["h2"], _row(small["norm_ple"], i), dh)
        g_w2 = mm_xt_dy(f"d_mlp_w2_{i}", act["hd"], dh2b, False)
        dz = mm_dy_wrow_t(f"dz_{i}", dh2b, get_w("mlp_w2", i), extras=[(act["z"], "tile")], outs_dtypes=(BF16,),
                          epi=lambda acc, z_t: (acc * (2.0 * jnp.maximum(z_t, 0.0)),))[0]
        g_w1 = mm_xt_dy(f"d_mlp_w1_{i}", act["u2"], dz, True)
        put_grads({("ple_w_proj", i): g_proj, ("ple_w_gate", i): g_gate, ("mlp_w2", i): g_w2, ("mlp_w1", i): g_w1})
        du2 = mm_dy_wcol_t(f"du2_{i}", dz, get_w("mlp_w1", i), 0)[0]
        dh1, dh1b, sg["norm_mlp"][i], cs1 = rms_bwd(f"rms_mlp_bwd_{i}", du2, act["h1"], _row(small["norm_mlp"], i), dh2)
        if i % 2 == 0:
            g_out = mm_xt_dy(f"d_cf_pw2_{i}", act["v3"], dh1b, False)
            sg["cf_b_pw2"][j] = cs1
            dv3 = mm_dy_wrow_t(f"dv3_{i}", dh1b, get_w("cf_w_pw2", j))[0]
            dv1, sg["cf_norm"][j], sg["cf_b_dw"][j] = cf_bwd_rows(f"cf_bwd_rows_{i}", dv3, act["v1"], _row(small["cf_norm"], j))
            da, sg["cf_w_dw"][j], sg["cf_b_pw1"][j] = cf_bwd_conv(f"cf_bwd_conv_{i}", dv1, act["v0"], act["a"], conv_w["cf"][j])
            g_in = mm_xt_dy(f"d_cf_pw1_{i}", act["u"], da, True)
            put_grads({("cf_w_pw2", j): g_out, ("cf_w_pw1", j): g_in})
            du = mm_dy_wcol_t(f"du_{i}", da, get_w("cf_w_pw1", j), 0)[0]
        else:
            g_out = mm_xt_dy(f"d_sc_out_{i}", act["y"], dh1b, False)
            dy = mm_dy_wrow_t(f"dy_{i}", dh1b, get_w("sc_w_out", j))[0]
            dbcv, sg["sc_w_conv"][j] = sc_bwd_mid(f"sc_bwd_mid_{i}", dy, act["bcv"], conv_w["sc"][j])
            g_in = mm_xt_dy(f"d_sc_in_{i}", act["u"], dbcv, True, tn=768)
            put_grads({("sc_w_out", j): g_out, ("sc_w_in", j): g_in})
            du = mm_dy_wcol_t(f"du_{i}", dbcv, get_w("sc_w_in", j), 0)[0]
        dh, _, sg["norm_mix"][i], _ = rms_bwd(f"rms_mix_bwd_{i}", du, act["h"], _row(small["norm_mix"], i), dh1)
    return loss_part, dh, sg


def _me_and_peers():
    x, y, c = lax.axis_index("x"), lax.axis_index("y"), lax.axis_index("c")
    me = 4 * x + 2 * y + c
    peers = []
    for q in range(1, NDEV):
        px = 1 - x if q & 4 else x
        py = 1 - y if q & 2 else y
        pc = 1 - c if q & 1 else c
        peers.append(((px, py, pc), 4 * px + 2 * py + pc))
    return me, peers


def _exchange(name, srcs, out_shapes, src_fns, dst_fns, after=()):
    n = len(srcs)
    n_after = len(after)

    def body(*refs):
        ins, outs = refs[:n], refs[n + n_after:2 * n + n_after]
        send_sems, recv_sems, local_sems = refs[2 * n + n_after:]
        me, peers = _me_and_peers()
        local, remote = [], []
        for k in range(n):
            cp = pltpu.make_async_copy(src_fns[k](ins[k], me), dst_fns[k](outs[k], me), local_sems.at[k])
            cp.start()
            local.append(cp)
        for q, (peer, peer_blk) in enumerate(peers):
            for k in range(n):
                cp = pltpu.make_async_remote_copy(
                    src_ref=src_fns[k](ins[k], peer_blk), dst_ref=dst_fns[k](outs[k], me),
                    send_sem=send_sems.at[k, q], recv_sem=recv_sems.at[k, q],
                    device_id=peer, device_id_type=MESH)
                cp.start()
                remote.append(cp)
        for q, (peer, peer_blk) in enumerate(peers):
            for k in range(n):
                pltpu.make_async_remote_copy(
                    src_ref=src_fns[k](ins[k], peer_blk), dst_ref=dst_fns[k](outs[k], peer_blk),
                    send_sem=send_sems.at[k, q], recv_sem=recv_sems.at[k, q],
                    device_id=peer, device_id_type=MESH).wait_recv()
        for cp in remote:
            cp.wait_send()
        for cp in local:
            cp.wait()

    any_spec = pl.BlockSpec(memory_space=pl.ANY)
    return pl.pallas_call(
        body,
        name=name,
        in_specs=[any_spec] * (n + n_after),
        out_specs=[any_spec] * n,
        out_shape=out_shapes,
        scratch_shapes=[pltpu.SemaphoreType.DMA((n, N_PEERS)), pltpu.SemaphoreType.DMA((n, N_PEERS)),
                        pltpu.SemaphoreType.DMA((n,))],
    )(*srcs, *after)


def sc_exchange(name, collective_id, srcs, out_shapes, src_fns, dst_fns):
    n = len(srcs)

    def body(*refs):
        ins, outs = refs[:n], refs[n:2 * n]
        send_sems, recv_sems, local_sems = refs[2 * n:]
        me, peers = _me_and_peers()
        barrier = pltpu.get_barrier_semaphore()
        for peer, _ in peers:
            pl.semaphore_signal(barrier, inc=1, device_id=peer, device_id_type=MESH)
        pl.semaphore_wait(barrier, N_PEERS)
        local, remote = [], []
        for k in range(n):
            cp = pltpu.make_async_copy(src_fns[k](ins[k], me), dst_fns[k](outs[k], me), local_sems.at[k])
            cp.start()
            local.append(cp)
        for q, (peer, peer_blk) in enumerate(peers):
            for k in range(n):
                cp = pltpu.make_async_remote_copy(
                    src_ref=src_fns[k](ins[k], peer_blk), dst_ref=dst_fns[k](outs[k], me),
                    send_sem=send_sems.at[k, q], recv_sem=recv_sems.at[k, q],
                    device_id=peer, device_id_type=MESH)
                cp.start()
                remote.append(cp)
        for q, (peer, peer_blk) in enumerate(peers):
            for k in range(n):
                pltpu.make_async_remote_copy(
                    src_ref=src_fns[k](ins[k], peer_blk), dst_ref=dst_fns[k](outs[k], peer_blk),
                    send_sem=send_sems.at[k, q], recv_sem=recv_sems.at[k, q],
                    device_id=peer, device_id_type=MESH).wait_recv()
        for cp in remote:
            cp.wait_send()
        for cp in local:
            cp.wait()

    return pl.kernel(
        body,
        out_type=out_shapes,
        mesh=plsc.ScalarSubcoreMesh(axis_name="sequencer", num_cores=1),
        name=name,
        scratch_types=[pltpu.SemaphoreType.DMA((n, N_PEERS)), pltpu.SemaphoreType.DMA((n, N_PEERS)),
                       pltpu.SemaphoreType.DMA((n,))],
        compiler_params=pltpu.CompilerParams(collective_id=collective_id),
    )(*srcs)


def sc_gather(name, collective_id, srcs, layers):
    n = len(srcs)
    outs_shape = [jax.ShapeDtypeStruct((NDEV, 1) + a.shape[1:], a.dtype) for a in srcs]

    def body(*refs):
        ins, outs = refs[:n], refs[n:2 * n]
        send_sems, recv_sems, local_sems = refs[2 * n:]
        x, y, c = lax.axis_index("x"), lax.axis_index("y"), lax.axis_index("c")
        me = 4 * x + 2 * y + c
        sibling = (x, y, 1 - c)
        chips = [(1 - x, y), (x, 1 - y), (1 - x, 1 - y)]
        barrier = pltpu.get_barrier_semaphore()
        for peer in [sibling] + [(cx, cy, c) for cx, cy in chips]:
            pl.semaphore_signal(barrier, inc=1, device_id=peer, device_id_type=MESH)
        pl.semaphore_wait(barrier, 1 + len(chips))

        def copy(k, slot, blk, to, src=None):
            place = outs[k].at[blk, 0]
            return pltpu.make_async_remote_copy(
                src_ref=place if src is None else src, dst_ref=place,
                send_sem=send_sems.at[k, slot], recv_sem=recv_sems.at[k, slot],
                device_id=to, device_id_type=MESH)

        local, sent = [], []
        for k in range(n):
            mine = ins[k].at[layers[k]]
            cp = pltpu.make_async_copy(mine, outs[k].at[me, 0], local_sems.at[k])
            cp.start()
            local.append(cp)
            sent.append(copy(k, 0, me, sibling, src=mine))
            sent += [copy(k, 1 + j, me, (cx, cy, c), src=mine) for j, (cx, cy) in enumerate(chips)]
        for cp in sent:
            cp.start()
        for k in range(n):
            for j, (cx, cy) in enumerate(chips):
                blk = 4 * cx + 2 * cy + c
                copy(k, 1 + j, blk, sibling).wait_recv()
                fwd = copy(k, 4 + j, blk, sibling)
                fwd.start()
                sent.append(fwd)
        for k in range(n):
            copy(k, 0, 4 * x + 2 * y + (1 - c), sibling).wait_recv()
            for j, (cx, cy) in enumerate(chips):
                copy(k, 4 + j, 4 * cx + 2 * cy + (1 - c), sibling).wait_recv()
        for cp in sent:
            cp.wait_send()
        for cp in local:
            cp.wait()

    return pl.kernel(
        body,
        out_type=outs_shape,
        mesh=plsc.ScalarSubcoreMesh(axis_name="sequencer", num_cores=1),
        name=name,
        scratch_types=[pltpu.SemaphoreType.DMA((n, N_PEERS)), pltpu.SemaphoreType.DMA((n, N_PEERS)),
                       pltpu.SemaphoreType.DMA((n,))],
        compiler_params=pltpu.CompilerParams(collective_id=collective_id),
    )(*srcs)


def _gather_src(layer):
    return lambda ref, blk: ref.at[layer]


def _gather_dst(ref, blk):
    return ref.at[blk, 0]


def _slice_of(ref, blk):
    return ref.at[blk]


def cast_bf16(name, w, tr_elems=512 * 1024):
    l, r, c = w.shape
    tr = _row_tile(r, tr_elems // c)
    spec = pl.BlockSpec((None, tr, c), lambda li, i: (li, i, 0))

    def body(w_ref, o_ref):
        o_ref[...] = w_ref[...].astype(BF16)

    return pl.pallas_call(
        body, name=name, grid=(l, r // tr), in_specs=[spec], out_specs=spec,
        out_shape=jax.ShapeDtypeStruct(w.shape, BF16),
        compiler_params=_params(("parallel", "parallel"), 6 * tr * c),
    )(w)


def _adamw_math(w, g, m, v):
    m = ADAM_B1 * m + (1.0 - ADAM_B1) * g
    v = ADAM_B2 * v + (1.0 - ADAM_B2) * (g * g)
    m_hat = m * (1.0 / (1.0 - ADAM_B1 ** ADAM_STEP))
    v_hat = v * (1.0 / (1.0 - ADAM_B2 ** ADAM_STEP))
    delta = -ADAM_LR * (m_hat / (jnp.sqrt(v_hat) + ADAM_EPS) + ADAM_WD * w)
    return delta, m, v


def _sum_blocks(ref):
    g = ref[0].astype(F32)
    for d in range(1, ref.shape[0]):
        g = g + ref[d].astype(F32)
    return g


def adamw_layer(name, recv, w, m, v, layer, stacked, tr_elems=128 * 1024):
    nd, r, c = recv.shape
    tr = _row_tile(r, tr_elems // c)
    r_spec = pl.BlockSpec((nd, tr, c), lambda i: (0, i, 0))
    w_spec = pl.BlockSpec((None, tr, c), lambda i: (layer, i, 0))
    if stacked is None:
        stacked = [lax.empty(w.shape, F32) for _ in range(4)]

    def body(r_ref, w_ref, m_ref, v_ref, g_in, d_in, m_in, v_in, g_out, d_out, m_out, v_out):
        g = _sum_blocks(r_ref)
        delta, m_new, v_new = _adamw_math(w_ref[...], g, m_ref[...], v_ref[...])
        g_out[...] = g
        d_out[...] = delta
        m_out[...] = m_new
        v_out[...] = v_new

    out = jax.ShapeDtypeStruct(w.shape, F32)
    return pl.pallas_call(
        body, name=name, grid=(r // tr,),
        in_specs=[r_spec, w_spec, w_spec, w_spec] + [pl.BlockSpec(memory_space=pl.ANY)] * 4,
        out_specs=[w_spec] * 4, out_shape=[out] * 4,
        input_output_aliases={4: 0, 5: 1, 6: 2, 7: 3},
        compiler_params=_params(("parallel",), tr * c * (2 * nd + 7 * 4)),
    )(*_in_hbm([recv, w, m, v]), *stacked)


def pack_small_grads(name, parts, taps, n_blocks):
    d = parts[0].shape[1]
    n_p, rows = len(parts), [t.shape[0] for t in taps]
    cb = d // n_blocks

    def body(*refs):
        part_refs, tap_refs = refs[:n_p], refs[n_p:n_p + len(taps)]
        sums_out, taps_out = refs[n_p + len(taps):]
        for i, r in enumerate(part_refs):
            sums_out[pl.ds(i, 1), :] = jnp.sum(r[...], axis=0, keepdims=True)
        r0 = 0
        for t_ref, n in zip(tap_refs, rows):
            for b in range(n_blocks):
                taps_out[b, pl.ds(r0, n), :] = t_ref[:, pl.ds(b * cb, cb)]
            r0 += n

    vm = pl.BlockSpec(memory_space=pltpu.VMEM)
    return pl.pallas_call(
        body, name=name, in_specs=[vm] * (n_p + len(taps)), out_specs=[vm] * 2,
        out_shape=[jax.ShapeDtypeStruct((n_p, d), F32), jax.ShapeDtypeStruct((n_blocks, sum(rows), cb), F32)],
    )(*parts, *taps)


def small_update(name, part_g, tap_g, w_a, m_a, v_a, w_b, m_b, v_b):
    nd, na, d = part_g.shape
    nb, cb = w_b.shape

    def body(pg_ref, tg_ref, wa_ref, ma_ref, va_ref, wb_ref, mb_ref, vb_ref,
             ga_out, da_out, ma_out, va_out, gb_out, db_out, mb_out, vb_out, loss_out):
        ga = _sum_blocks(pg_ref)
        delta, m_new, v_new = _adamw_math(wa_ref[...], ga, ma_ref[...], va_ref[...])
        ga_out[...] = ga
        da_out[...] = delta
        ma_out[...] = m_new
        va_out[...] = v_new
        loss_out[...] = jnp.broadcast_to(jnp.sum(ga[na - 1:na, :], axis=1, keepdims=True), loss_out.shape)
        gb = _sum_blocks(tg_ref)
        delta, m_new, v_new = _adamw_math(wb_ref[...], gb, mb_ref[...], vb_ref[...])
        gb_out[...] = gb
        db_out[...] = delta
        mb_out[...] = m_new
        vb_out[...] = v_new

    oa, ob = jax.ShapeDtypeStruct((na, d), F32), jax.ShapeDtypeStruct((nb, cb), F32)
    vm = pl.BlockSpec(memory_space=pltpu.VMEM)
    return pl.pallas_call(
        body, name=name, in_specs=[vm] * 8, out_specs=[vm] * 9,
        out_shape=[oa] * 4 + [ob] * 4 + [jax.ShapeDtypeStruct((1, LANES), F32)],
        compiler_params=pltpu.CompilerParams(vmem_limit_bytes=_vmem_limit(_nbytes(part_g.shape, F32))),
    )(part_g, tap_g, w_a, m_a, v_a, w_b, m_b, v_b)


BIG = ("cf_w_pw1", "cf_w_pw2", "sc_w_in", "sc_w_out", "mlp_w1", "mlp_w2", "ple_w_proj", "ple_w_gate")
COL_SHARDED = ("cf_w_pw1", "sc_w_in", "mlp_w1", "ple_w_proj")
WEIGHT_ORDER = ("norm_mix", "norm_mlp", "norm_ple", "cf_w_pw1", "cf_b_pw1", "cf_w_dw", "cf_b_dw", "cf_norm",
                "cf_w_pw2", "cf_b_pw2", "sc_w_in", "sc_w_conv", "sc_w_out", "mlp_w1", "mlp_w2", "ple_w_proj",
                "ple_w_gate", "norm_final")
SMALL_ROWS = (("norm_mix", 4), ("norm_mlp", 4), ("norm_ple", 4), ("cf_b_pw1", 4), ("cf_b_dw", 2), ("cf_norm", 2),
              ("cf_b_pw2", 2), ("norm_final", 1))


def _layer_weights(i):
    mixer = (("cf_w_pw1", i // 2), ("cf_w_pw2", i // 2)) if i % 2 == 0 else (("sc_w_in", i // 2), ("sc_w_out", i // 2))
    return mixer + (("mlp_w1", i), ("mlp_w2", i), ("ple_w_proj", i), ("ple_w_gate", i))


def _pad_rows(a, rows):
    return jnp.pad(a, ((0, 0), (0, rows - a.shape[1]), (0, 0)))


def _pack_taps(cf, sc):
    c = cf.shape[2]
    return jnp.concatenate([_pad_rows(cf, CONV_A_HALO).reshape(-1, c), _pad_rows(sc, CONV_B_HALO).reshape(-1, c)], axis=0)


def _unpack_taps(t, n_cf):
    c = t.shape[1]
    cf = t[:n_cf * CONV_A_HALO].reshape(n_cf, CONV_A_HALO, c)[:, :CONV_A_TAPS]
    sc = t[n_cf * CONV_A_HALO:].reshape(-1, CONV_B_HALO, c)[:, :CONV_B_TAPS]
    return cf, sc


def _pack_small(vals, d):
    return jnp.concatenate([vals[k].reshape(-1, d) for k, _ in SMALL_ROWS] + [jnp.zeros((1, d), F32)], axis=0)


def _unpack_small(a, shapes):
    out, r = {}, 0
    for k, n in SMALL_ROWS:
        out[k] = a[r:r + n].reshape(shapes[k])
        r += n
    return out


def kernel(x, p, norm_mix, norm_mlp, norm_ple, cf_w_pw1, cf_b_pw1, cf_w_dw, cf_b_dw, cf_norm, cf_w_pw2, cf_b_pw2, sc_w_in, sc_w_conv, sc_w_out, mlp_w1, mlp_w2, ple_w_proj, ple_w_gate, norm_final, loss_target, m_norm_mix, m_norm_mlp, m_norm_ple, m_cf_w_pw1, m_cf_b_pw1, m_cf_w_dw, m_cf_b_dw, m_cf_norm, m_cf_w_pw2, m_cf_b_pw2, m_sc_w_in, m_sc_w_conv, m_sc_w_out, m_mlp_w1, m_mlp_w2, m_ple_w_proj, m_ple_w_gate, m_norm_final, v_norm_mix, v_norm_mlp, v_norm_ple, v_cf_w_pw1, v_cf_b_pw1, v_cf_w_dw, v_cf_b_dw, v_cf_norm, v_cf_w_pw2, v_cf_b_pw2, v_sc_w_in, v_sc_w_conv, v_sc_w_out, v_mlp_w1, v_mlp_w2, v_ple_w_proj, v_ple_w_gate, v_norm_final):
    w = dict(norm_mix=norm_mix, norm_mlp=norm_mlp, norm_ple=norm_ple, cf_w_pw1=cf_w_pw1, cf_b_pw1=cf_b_pw1,
             cf_w_dw=cf_w_dw, cf_b_dw=cf_b_dw, cf_norm=cf_norm, cf_w_pw2=cf_w_pw2, cf_b_pw2=cf_b_pw2,
             sc_w_in=sc_w_in, sc_w_conv=sc_w_conv, sc_w_out=sc_w_out, mlp_w1=mlp_w1, mlp_w2=mlp_w2,
             ple_w_proj=ple_w_proj, ple_w_gate=ple_w_gate, norm_final=norm_final)
    m = dict(norm_mix=m_norm_mix, norm_mlp=m_norm_mlp, norm_ple=m_norm_ple, cf_w_pw1=m_cf_w_pw1, cf_b_pw1=m_cf_b_pw1,
             cf_w_dw=m_cf_w_dw, cf_b_dw=m_cf_b_dw, cf_norm=m_cf_norm, cf_w_pw2=m_cf_w_pw2, cf_b_pw2=m_cf_b_pw2,
             sc_w_in=m_sc_w_in, sc_w_conv=m_sc_w_conv, sc_w_out=m_sc_w_out, mlp_w1=m_mlp_w1, mlp_w2=m_mlp_w2,
             ple_w_proj=m_ple_w_proj, ple_w_gate=m_ple_w_gate, norm_final=m_norm_final)
    v = dict(norm_mix=v_norm_mix, norm_mlp=v_norm_mlp, norm_ple=v_norm_ple, cf_w_pw1=v_cf_w_pw1, cf_b_pw1=v_cf_b_pw1,
             cf_w_dw=v_cf_w_dw, cf_b_dw=v_cf_b_dw, cf_norm=v_cf_norm, cf_w_pw2=v_cf_w_pw2, cf_b_pw2=v_cf_b_pw2,
             sc_w_in=v_sc_w_in, sc_w_conv=v_sc_w_conv, sc_w_out=v_sc_w_out, mlp_w1=v_mlp_w1, mlp_w2=v_mlp_w2,
             ple_w_proj=v_ple_w_proj, ple_w_gate=v_ple_w_gate, norm_final=v_norm_final)
    depth, d = norm_mix.shape
    n_cf = cf_w_dw.shape[0]

    taps_w = _pack_taps(cf_w_dw, sc_w_conv)
    shards = {k: cast_bf16(f"cast_{k}", w[k]) for k in BIG}
    gathered = {}
    ids = iter(range(4 * depth))
    for i in range(depth):
        layer_names = _layer_weights(i)
        for part, names in (("mixer", layer_names[:2]), ("mlp", layer_names[2:])):
            first = i == 0 and part == "mixer"
            srcs = [shards[k] for k, _ in names] + ([taps_w[None]] if first else [])
            got = sc_gather(f"gather_{part}_{i}", next(ids), srcs, [l for _, l in names] + ([0] if first else []))
            for (k, l), g in zip(names, got):
                gathered[(k, l)] = g if k in COL_SHARDED else g.reshape(-1, g.shape[3])
            if first:
                taps_full = jnp.transpose(got[-1][:, 0], (1, 0, 2)).reshape(taps_w.shape[0], d)
    conv_w = {"cf": taps_full[:n_cf * CONV_A_HALO].reshape(n_cf, CONV_A_HALO, d),
              "sc": taps_full[n_cf * CONV_A_HALO:].reshape(-1, CONV_B_HALO, d)}

    def get_w(k, l, after=None):
        return gathered[(k, l)]

    received = {}

    def put_grads(grads):
        names = list(grads)
        got = sc_exchange(f"grad_exchange_{names[0][0]}_{names[0][1]}", next(ids), [grads[n] for n in names],
                          [jax.ShapeDtypeStruct(grads[n].shape, BF16) for n in names],
                          [_slice_of] * len(names), [_slice_of] * len(names))
        received.update(zip(names, got))
        return None

    small = {k: w[k] for k, _ in SMALL_ROWS}
    small["norm_final"] = norm_final[None]
    loss_part, grad_x, sg = _local_step(x[0], p[:, 0], loss_target[0], small, get_w, conv_w, put_grads)

    out = {k: None for k in BIG}
    last_group = [n for n in _layer_weights(0)[:2]]
    for (k, l), recv in received.items():
        if (k, l) not in last_group:
            out[k] = adamw_layer(f"adamw_{k}_{l}", recv, w[k], m[k], v[k], l, out[k])
    updated_first = [out[k][0] for k in BIG if out[k] is not None and k not in [n for n, _ in last_group]]

    parts = []
    for k, n in SMALL_ROWS:
        for g in sg[k]:
            parts += [g[:, :d], g[:, d:]] if g.shape[1] == 2 * d else [g]
    parts.append(loss_part)
    sums, tap_slices = pack_small_grads("pack_small_grads", parts, sg["cf_w_dw"] + sg["sc_w_conv"], NDEV)
    part_all, tap_mine = _exchange(
        "small_exchange", [sums[None], tap_slices],
        [jax.ShapeDtypeStruct((NDEV, 1) + sums.shape, F32), jax.ShapeDtypeStruct(tap_slices.shape, F32)],
        [_gather_src(0), _slice_of], [_gather_dst, _slice_of], after=updated_first)
    for k, l in last_group:
        out[k] = adamw_layer(f"adamw_{k}_{l}", received[(k, l)], w[k], m[k], v[k], l, out[k])
    sm = small_update("small_update", part_all[:, 0], tap_mine,
                      _pack_small(w, d), _pack_small(m, d), _pack_small(v, d),
                      taps_w, _pack_taps(m["cf_w_dw"], m["sc_w_conv"]), _pack_taps(v["cf_w_dw"], v["sc_w_conv"]))
    shapes = {k: w[k].shape for k, _ in SMALL_ROWS}
    for t in range(4):
        un = _unpack_small(sm[t], shapes)
        cf_t, sc_t = _unpack_taps(sm[4 + t], n_cf)
        for k in un:
            out.setdefault(k, [None] * 4)[t] = un[k]
        out.setdefault("cf_w_dw", [None] * 4)[t] = cf_t
        out.setdefault("sc_w_conv", [None] * 4)[t] = sc_t
    loss = sm[8][0, 0]

    return (loss, grad_x[None], *[out[k][0] for k in WEIGHT_ORDER], *[out[k][1] for k in WEIGHT_ORDER],
            *[out[k][2] for k in WEIGHT_ORDER], *[out[k][3] for k in WEIGHT_ORDER])
```

```python
import jax
import jax.numpy as jnp
from jax import lax
from jax.experimental import pallas as pl
from jax.experimental.pallas import tpu as pltpu
from jax.experimental.pallas import tpu_sc as plsc

F32 = jnp.float32
BF16 = jnp.bfloat16
EPS = 1e-6
NDEV = 8
N_PEERS = NDEV - 1
MESH = pl.DeviceIdType.MESH

ADAM_LR = 0.001
ADAM_B1 = 0.9
ADAM_B2 = 0.999
ADAM_EPS = 1e-08
ADAM_WD = 0.01
ADAM_STEP = 10

V7X_VMEM_BYTES = 64 * 1024 * 1024
VMEM_LIMIT_MAX = 56 * 1024 * 1024
SUBLANES = 8
LANES = 128
CONV_A_TAPS = 31
CONV_A_HALO = 32
CONV_B_TAPS = 3
CONV_B_HALO = 8


def _nbytes(shape, dtype):
    n = 1
    for s in shape:
        if s is not None:
            n *= s
    return n * jnp.dtype(dtype).itemsize


def _vmem_limit(block_bytes, scratch_bytes=0):
    need = 2 * block_bytes + scratch_bytes
    return int(min(VMEM_LIMIT_MAX, max(32 * 1024 * 1024, need + need // 2 + (4 << 20))))


def _params(sem, block_bytes, scratch_bytes=0):
    return pltpu.CompilerParams(dimension_semantics=sem, vmem_limit_bytes=_vmem_limit(block_bytes, scratch_bytes))


_DIMS = {
    "nn": (((1,), (0,)), ((), ())),
    "nt": (((1,), (1,)), ((), ())),
    "tn": (((0,), (0,)), ((), ())),
}


def _mm(name, dims, grid, acc_shape, a, a_spec, b, b_spec, extras, outs, epi):
    ni, nj, nk = grid
    n_ex, n_out = len(extras), len(outs)
    dn = _DIMS[dims]

    def body(*refs):
        a_ref, b_ref = refs[0], refs[1]
        ex_refs = refs[2:2 + n_ex]
        out_refs = refs[2 + n_ex:2 + n_ex + n_out]
        d = lax.dot_general(a_ref[...].astype(BF16), b_ref[...].astype(BF16), dn, preferred_element_type=F32)

        def finish(acc):
            res = epi(acc, *[r[...] for r in ex_refs])
            for o_ref, r in zip(out_refs, res):
                o_ref[...] = r.astype(o_ref.dtype)

        if nk == 1:
            finish(d)
        else:
            acc_ref = refs[2 + n_ex + n_out]
            k = pl.program_id(2)

            @pl.when(k == 0)
            def _():
                acc_ref[...] = d

            @pl.when(jnp.logical_and(k > 0, k < nk - 1))
            def _():
                acc_ref[...] += d

            @pl.when(k == nk - 1)
            def _():
                finish(acc_ref[...] + d)

    blk = _nbytes(a_spec.block_shape, a.dtype) + _nbytes(b_spec.block_shape, b.dtype)
    for arr, spec in list(extras) + list(outs):
        blk += _nbytes(spec.block_shape, arr.dtype)
    acc_bytes = _nbytes(acc_shape, F32)
    scratch = [pltpu.VMEM(acc_shape, F32)] if nk > 1 else []
    return pl.pallas_call(
        body,
        name=name,
        grid=grid,
        in_specs=[a_spec, b_spec] + [s for _, s in extras],
        out_specs=[s for _, s in outs],
        out_shape=[o for o, _ in outs],
        scratch_shapes=scratch,
        compiler_params=_params(("parallel", "parallel", "arbitrary"), blk, 3 * acc_bytes),
    )(a, b, *[e for e, _ in extras])


def _tile(n, pref):
    if n <= pref:
        return n
    t = pref - pref % LANES
    while t > LANES and n % t:
        t -= LANES
    assert n % t == 0, (n, pref)
    return t


def _row_tile(n, pref):
    if n <= pref:
        return n
    t = max(SUBLANES, pref - pref % SUBLANES)
    while t > SUBLANES and n % t:
        t -= SUBLANES
    assert n % t == 0, (n, pref)
    return t


def _id_epi(acc):
    return (acc,)


def mm_x_wcol(name, x, w, layer, extras=(), outs_dtypes=(F32,), epi=_id_epi, tm=1024, tn=512):
    m, kdim = x.shape
    c = w.shape[3]
    n = NDEV * c
    tm, tn = _tile(m, tm), _tile(c, tn)
    tk = _tile(kdim, 2048)
    grid = (m // tm, n // tn, kdim // tk)
    per = c // tn
    a_spec = pl.BlockSpec((tm, tk), lambda i, j, k: (i, k))
    b_spec = pl.BlockSpec((None, None, tk, tn), lambda i, j, k: (j // per, layer, k, j % per))
    ex = [(e, _ex_spec(e, kind, tm, tn)) for e, kind in extras]
    o_spec = pl.BlockSpec((tm, tn), lambda i, j, k: (i, j))
    outs = [(jax.ShapeDtypeStruct((m, n), dt), o_spec) for dt in outs_dtypes]
    return _mm(name, "nn", grid, (tm, tn), x, a_spec, w, b_spec, ex, outs, epi)


def mm_x_wrow(name, x, w, extras=(), outs_dtypes=(F32,), epi=_id_epi, tm=1024, tn=512):
    m, kdim = x.shape
    n = w.shape[1]
    assert kdim == w.shape[0]
    tm, tn = _tile(m, tm), _tile(n, tn)
    tk = _tile(kdim, 2048)
    grid = (m // tm, n // tn, kdim // tk)
    a_spec = pl.BlockSpec((tm, tk), lambda i, j, k: (i, k))
    b_spec = pl.BlockSpec((tk, tn), lambda i, j, k: (k, j))
    ex = [(e, _ex_spec(e, kind, tm, tn)) for e, kind in extras]
    o_spec = pl.BlockSpec((tm, tn), lambda i, j, k: (i, j))
    outs = [(jax.ShapeDtypeStruct((m, n), dt), o_spec) for dt in outs_dtypes]
    return _mm(name, "nn", grid, (tm, tn), x, a_spec, w, b_spec, ex, outs, epi)


def mm_dy_wcol_t(name, dy, w, layer, extras=(), outs_dtypes=(F32,), epi=_id_epi, tm=1024, tn=1024):
    m, n = dy.shape
    kdim, c = w.shape[2], w.shape[3]
    assert n == NDEV * c
    tm, tn = _tile(m, tm), _tile(kdim, tn)
    tk = _tile(c, 2048)
    per = c // tk
    grid = (m // tm, kdim // tn, n // tk)
    a_spec = pl.BlockSpec((tm, tk), lambda i, j, k: (i, k))
    b_spec = pl.BlockSpec((None, None, tn, tk), lambda i, j, k: (k // per, layer, j, k % per))
    ex = [(e, _ex_spec(e, kind, tm, tn)) for e, kind in extras]
    o_spec = pl.BlockSpec((tm, tn), lambda i, j, k: (i, j))
    outs = [(jax.ShapeDtypeStruct((m, kdim), dt), o_spec) for dt in outs_dtypes]
    return _mm(name, "nt", grid, (tm, tn), dy, a_spec, w, b_spec, ex, outs, epi)


def mm_dy_wrow_t(name, dy, w, extras=(), outs_dtypes=(F32,), epi=_id_epi, tm=1024, tn=512):
    m, n = dy.shape
    kdim = w.shape[0]
    assert n == w.shape[1]
    tm, tn = _tile(m, tm), _tile(kdim, tn)
    tk = _tile(n, 2048)
    grid = (m // tm, kdim // tn, n // tk)
    a_spec = pl.BlockSpec((tm, tk), lambda i, j, k: (i, k))
    b_spec = pl.BlockSpec((tn, tk), lambda i, j, k: (j, k))
    ex = [(e, _ex_spec(e, kind, tm, tn)) for e, kind in extras]
    o_spec = pl.BlockSpec((tm, tn), lambda i, j, k: (i, j))
    outs = [(jax.ShapeDtypeStruct((m, kdim), dt), o_spec) for dt in outs_dtypes]
    return _mm(name, "nt", grid, (tm, tn), dy, a_spec, w, b_spec, ex, outs, epi)


def mm_xt_dy(name, x, dy, col_shards, tm=1024, tn=512):
    m, kdim = x.shape
    n = dy.shape[1]
    tk = _tile(m, 2048)
    if col_shards:
        c = n // NDEV
        tm, tn = _tile(kdim, tm), _tile(c, tn)
        per = c // tn
        out = jax.ShapeDtypeStruct((NDEV, kdim, c), BF16)
        o_spec = pl.BlockSpec((None, tm, tn), lambda i, j, k: (j // per, i, j % per))
    else:
        tm, tn = _tile(kdim, tm), _tile(n, tn)
        out = jax.ShapeDtypeStruct((kdim, n), BF16)
        o_spec = pl.BlockSpec((tm, tn), lambda i, j, k: (i, j))
    grid = (kdim // tm, n // tn, m // tk)
    a_spec = pl.BlockSpec((tk, tm), lambda i, j, k: (k, i))
    b_spec = pl.BlockSpec((tk, tn), lambda i, j, k: (k, j))
    g = _mm(name, "tn", grid, (tm, tn), x, a_spec, dy, b_spec, [], [(out, o_spec)], _id_epi)[0]
    return g if col_shards else g.reshape(NDEV, kdim // NDEV, n)


def _ex_spec(e, kind, tm, tn):
    if kind == "tile":
        return pl.BlockSpec((tm, tn), lambda i, j, k: (i, j))
    if kind == "row":
        return pl.BlockSpec((1, tn), lambda i, j, k: (0, j))
    raise ValueError(kind)


def _rows_call(name, body, n_rows, ts, ins, outs, scratch=(), scratch_bytes=0):
    blk = sum(_nbytes(s.block_shape, a.dtype) for a, s in list(ins) + list(outs))
    return pl.pallas_call(
        body,
        name=name,
        grid=(n_rows // ts,),
        in_specs=[s for _, s in ins],
        out_specs=[s for _, s in outs],
        out_shape=[o for o, _ in outs],
        scratch_shapes=list(scratch),
        compiler_params=_params(("arbitrary",), blk, scratch_bytes + 4 * blk // 2),
    )(*[a for a, _ in ins])


def _blk(ts, d):
    return pl.BlockSpec((ts, d), lambda i: (i, 0))


def _full(shape):
    return pl.BlockSpec(shape, lambda i: tuple(0 for _ in shape))


def _rowsum8(v):
    t, d = v.shape
    return jnp.sum(v.reshape(t // SUBLANES, SUBLANES, d), axis=0)


def _accumulate(ref, val):
    @pl.when(pl.program_id(0) == 0)
    def _():
        ref[...] = val

    @pl.when(pl.program_id(0) > 0)
    def _():
        ref[...] += val


def _rstd(x):
    return lax.rsqrt(jnp.mean(x * x, axis=-1, keepdims=True) + EPS)


def _rms_bwd_math(dy, x, g):
    r = _rstd(x)
    gdy = dy * g
    c = jnp.mean(gdy * x, axis=-1, keepdims=True)
    dx = r * gdy - x * (r * r * r * c)
    return dx, dy * (x * r)


def rms_fwd(name, h, g, ts=512):
    s, d = h.shape
    ts = min(ts, s)

    def body(h_ref, g_ref, u_ref):
        x = h_ref[...]
        u_ref[...] = ((x * _rstd(x)) * g_ref[...]).astype(BF16)

    return _rows_call(name, body, s, ts, [(h, _blk(ts, d)), (g, _full((1, d)))],
                      [(jax.ShapeDtypeStruct((s, d), BF16), _blk(ts, d))])[0]


def rms_bwd(name, du, h, g, dres, ts=256):
    s, d = h.shape
    ts = min(ts, s)

    def body(du_ref, h_ref, g_ref, dres_ref, dh_ref, dhb_ref, dg_ref, cs_ref):
        dx, dg = _rms_bwd_math(du_ref[...], h_ref[...], g_ref[...])
        dh = dres_ref[...] + dx
        dh_ref[...] = dh
        dhb_ref[...] = dh.astype(BF16)
        _accumulate(dg_ref, _rowsum8(dg))
        _accumulate(cs_ref, _rowsum8(dh))

    return _rows_call(
        name, body, s, ts,
        [(du, _blk(ts, d)), (h, _blk(ts, d)), (g, _full((1, d))), (dres, _blk(ts, d))],
        [(jax.ShapeDtypeStruct((s, d), F32), _blk(ts, d)), (jax.ShapeDtypeStruct((s, d), BF16), _blk(ts, d)),
         (jax.ShapeDtypeStruct((SUBLANES, d), F32), _full((SUBLANES, d))),
         (jax.ShapeDtypeStruct((SUBLANES, d), F32), _full((SUBLANES, d)))])


def final_loss(name, h, g, target, ts=256):
    s, d = h.shape
    ts = min(ts, s)

    def body(h_ref, g_ref, t_ref, loss_ref, dh_ref, dg_ref):
        x = h_ref[...]
        gf = g_ref[...]
        y = (x * _rstd(x)) * gf
        err = y - t_ref[...]
        _accumulate(loss_ref, _rowsum8(err * err) * (0.5 / d))
        dx, dg = _rms_bwd_math(err * (1.0 / d), x, gf)
        dh_ref[...] = dx
        _accumulate(dg_ref, _rowsum8(dg))

    return _rows_call(
        name, body, s, ts,
        [(h, _blk(ts, d)), (g, _full((1, d))), (target, _blk(ts, d))],
        [(jax.ShapeDtypeStruct((SUBLANES, d), F32), _full((SUBLANES, d))),
         (jax.ShapeDtypeStruct((s, d), F32), _blk(ts, d)),
         (jax.ShapeDtypeStruct((SUBLANES, d), F32), _full((SUBLANES, d)))])


def ple_bwd_elem(name, dh, g, e, ts=512):
    s, d = dh.shape
    ts = min(ts, s)

    def body(dh_ref, g_ref, e_ref, de_ref, dgl_ref):
        dh_v, g_v = dh_ref[...], g_ref[...]
        de_ref[...] = (dh_v * g_v).astype(BF16)
        dgl_ref[...] = (dh_v * e_ref[...] * (g_v * (1.0 - g_v))).astype(BF16)

    return _rows_call(name, body, s, ts, [(dh, _blk(ts, d)), (g, _blk(ts, d)), (e, _blk(ts, d))],
                      [(jax.ShapeDtypeStruct((s, d), BF16), _blk(ts, d)),
                       (jax.ShapeDtypeStruct((s, d), BF16), _blk(ts, d))])


CONV_LANES = 256
CONV_ROWS = 64


def _lane_chunks(d, fn):
    lc = min(CONV_LANES, d)

    def lane_body(c, carry):
        fn(pl.ds(pl.multiple_of(c * lc, lc), lc))
        return carry

    lax.fori_loop(0, d // lc, lane_body, 0)


def _shifted_copies(buf, sh, lanes):
    rows = buf.shape[0] - SUBLANES
    for s in range(1, SUBLANES):
        sh[s, pl.ds(0, rows), :] = buf[pl.ds(s, rows), lanes]


def _window(buf, sh, lanes, start, rows):
    s = start % SUBLANES
    if s == 0:
        return buf[pl.ds(start, rows), lanes]
    return sh[s, pl.ds(start - s, rows), :]


def _prev_halo_spec(ts, halo, width):
    per = ts // halo
    return pl.BlockSpec((halo, width), lambda i: (jnp.maximum(i * per - 1, 0), 0))


def _next_halo_spec(ts, halo, width, n_rows):
    per = ts // halo
    last = n_rows // halo - 1
    return pl.BlockSpec((halo, width), lambda i: (jnp.minimum((i + 1) * per, last), 0))


def cf_fwd_mid(name, a, w_dw, b_dw, gn, ts=256):
    s, d2 = a.shape
    d = d2 // 2
    ts = min(ts, s)
    hl = CONV_A_HALO
    off = hl - (CONV_A_TAPS - 1)

    rc = min(CONV_ROWS, ts)

    def body(a_ref, ah_ref, w_ref, b_ref, gn_ref, v0_ref, v1_ref, v3_ref, buf, sh):
        first = pl.program_id(0) == 0
        halo = ah_ref[...]
        hv0 = halo[:, :d] * jax.nn.sigmoid(halo[:, d:])
        buf[pl.ds(0, hl), :] = jnp.where(first, 0.0, hv0)
        main = a_ref[...]
        v0 = main[:, :d] * jax.nn.sigmoid(main[:, d:])
        buf[pl.ds(hl, ts), :] = v0
        v0_ref[...] = v0

        def conv(lanes):
            _shifted_copies(buf, sh, lanes)
            for r0 in range(0, ts, rc):
                acc = jnp.zeros((rc, lanes.size), F32)
                for k in range(CONV_A_TAPS):
                    acc = acc + w_ref[pl.ds(k, 1), lanes] * _window(buf, sh, lanes, r0 + off + k, rc)
                v1_ref[pl.ds(r0, rc), lanes] = acc + b_ref[:, lanes]

        _lane_chunks(d, conv)
        v1 = v1_ref[...]
        v2 = (v1 * _rstd(v1)) * gn_ref[...]
        v3_ref[...] = (v2 * jax.nn.sigmoid(v2)).astype(BF16)

    return _rows_call(
        name, body, s, ts,
        [(a, _blk(ts, d2)), (a, _prev_halo_spec(ts, hl, d2)), (w_dw, _full(w_dw.shape)),
         (b_dw, _full((1, d))), (gn, _full((1, d)))],
        [(jax.ShapeDtypeStruct((s, d), F32), _blk(ts, d)), (jax.ShapeDtypeStruct((s, d), F32), _blk(ts, d)),
         (jax.ShapeDtypeStruct((s, d), BF16), _blk(ts, d))],
        scratch=[pltpu.VMEM((hl + ts, d), F32), pltpu.VMEM((SUBLANES, hl + ts, min(CONV_LANES, d)), F32)],
        scratch_bytes=_nbytes((hl + ts, d + SUBLANES * CONV_LANES), F32))


def cf_bwd_rows(name, dv3, v1, gn, ts=256):
    s, d = v1.shape
    ts = min(ts, s)

    def body(dv3_ref, v1_ref, gn_ref, dv1_ref, dgn_ref, db_ref):
        v1 = v1_ref[...]
        gn_v = gn_ref[...]
        v2 = (v1 * _rstd(v1)) * gn_v
        sg = jax.nn.sigmoid(v2)
        dv2 = dv3_ref[...] * (sg * (1.0 + v2 * (1.0 - sg)))
        dv1, dgn = _rms_bwd_math(dv2, v1, gn_v)
        dv1_ref[...] = dv1
        _accumulate(dgn_ref, _rowsum8(dgn))
        _accumulate(db_ref, _rowsum8(dv1))

    return _rows_call(
        name, body, s, ts, [(dv3, _blk(ts, d)), (v1, _blk(ts, d)), (gn, _full((1, d)))],
        [(jax.ShapeDtypeStruct((s, d), F32), _blk(ts, d)),
         (jax.ShapeDtypeStruct((SUBLANES, d), F32), _full((SUBLANES, d))),
         (jax.ShapeDtypeStruct((SUBLANES, d), F32), _full((SUBLANES, d)))])


def cf_bwd_conv(name, dv1, v0, a, w_dw, ts=256):
    s, d = dv1.shape
    ts = min(ts, s)
    hl = CONV_A_HALO
    taps = CONV_A_TAPS
    off = hl - (taps - 1)
    last_blk = s // ts - 1

    rc = min(CONV_ROWS, ts)

    def body(dv1_ref, dv1n_ref, v0_ref, v0p_ref, a_ref, w_ref, da_ref, dw_ref, db_ref,
             dbuf, vbuf, dv0_buf, dw_acc, dsh, vsh):
        i = pl.program_id(0)
        dbuf[pl.ds(0, ts), :] = dv1_ref[...]
        dbuf[pl.ds(ts, hl), :] = jnp.where(i == last_blk, 0.0, dv1n_ref[...])
        vbuf[pl.ds(0, hl), :] = jnp.where(i == 0, 0.0, v0p_ref[...])
        vbuf[pl.ds(hl, ts), :] = v0_ref[...]

        @pl.when(i == 0)
        def _():
            dw_acc[...] = jnp.zeros_like(dw_acc)

        def conv_t(lanes):
            _shifted_copies(dbuf, dsh, lanes)
            _shifted_copies(vbuf, vsh, lanes)
            for r0 in range(0, ts, rc):
                g = dbuf[pl.ds(r0, rc), lanes]
                acc = jnp.zeros((rc, lanes.size), F32)
                for k in range(taps):
                    acc = acc + w_ref[pl.ds(k, 1), lanes] * _window(dbuf, dsh, lanes, r0 + taps - 1 - k, rc)
                    prod = g * _window(vbuf, vsh, lanes, r0 + off + k, rc)
                    dw_acc[pl.ds(k * SUBLANES, SUBLANES), lanes] += _rowsum8(prod)
                dv0_buf[pl.ds(r0, rc), lanes] = acc

        _lane_chunks(d, conv_t)
        dv0 = dv0_buf[...]
        av = a_ref[...]
        val, sg = av[:, :d], jax.nn.sigmoid(av[:, d:])
        dval = dv0 * sg
        dgate = dv0 * val * (sg * (1.0 - sg))
        da_ref[:, :d] = dval.astype(BF16)
        da_ref[:, d:] = dgate.astype(BF16)
        _accumulate(db_ref.at[:, pl.ds(0, d)], _rowsum8(dval))
        _accumulate(db_ref.at[:, pl.ds(d, d)], _rowsum8(dgate))

        @pl.when(i == last_blk)
        def _():
            dw_ref[...] = jnp.sum(dw_acc[...].reshape(hl, SUBLANES, d), axis=1)

    lc = min(CONV_LANES, d)
    scratch = [pltpu.VMEM((ts + hl, d), F32), pltpu.VMEM((hl + ts, d), F32), pltpu.VMEM((ts, d), F32),
               pltpu.VMEM((hl * SUBLANES, d), F32), pltpu.VMEM((SUBLANES, ts + hl, lc), F32),
               pltpu.VMEM((SUBLANES, hl + ts, lc), F32)]
    sbytes = _nbytes((3 * ts + 2 * hl + hl * SUBLANES, d), F32) + 2 * _nbytes((SUBLANES, ts + hl, lc), F32)
    return _rows_call(
        name, body, s, ts,
        [(dv1, _blk(ts, d)), (dv1, _next_halo_spec(ts, hl, d, s)), (v0, _blk(ts, d)), (v0, _prev_halo_spec(ts, hl, d)),
         (a, _blk(ts, 2 * d)), (w_dw, _full(w_dw.shape))],
        [(jax.ShapeDtypeStruct((s, 2 * d), BF16), _blk(ts, 2 * d)),
         (jax.ShapeDtypeStruct((hl, d), F32), _full((hl, d))),
         (jax.ShapeDtypeStruct((SUBLANES, 2 * d), F32), _full((SUBLANES, 2 * d)))],
        scratch=scratch, scratch_bytes=sbytes)


def sc_fwd_mid(name, bcv, w_conv, ts=256):
    s, d3 = bcv.shape
    d = d3 // 3
    ts = min(ts, s)
    hl = CONV_B_HALO
    off = hl - (CONV_B_TAPS - 1)

    def body(x_ref, xp_ref, w_ref, y_ref, buf):
        hp = xp_ref[...]
        buf[pl.ds(0, hl), :] = jnp.where(pl.program_id(0) == 0, 0.0, hp[:, d:2 * d] * hp[:, 2 * d:])
        buf[pl.ds(hl, ts), :] = x_ref[:, d:2 * d] * x_ref[:, 2 * d:]
        cc = jnp.zeros((ts, d), F32)
        for k in range(CONV_B_TAPS):
            cc = cc + w_ref[pl.ds(k, 1), :] * buf[pl.ds(off + k, ts), :]
        y_ref[...] = (x_ref[:, :d] * cc).astype(BF16)

    return _rows_call(
        name, body, s, ts,
        [(bcv, _blk(ts, d3)), (bcv, _prev_halo_spec(ts, hl, d3)), (w_conv, _full(w_conv.shape))],
        [(jax.ShapeDtypeStruct((s, d), BF16), _blk(ts, d))],
        scratch=[pltpu.VMEM((hl + ts, d), F32)], scratch_bytes=_nbytes((hl + ts, d), F32))[0]


def sc_bwd_mid(name, dy, bcv, w_conv, ts=256):
    s, d3 = bcv.shape
    d = d3 // 3
    ts = min(ts, s)
    hl = CONV_B_HALO
    taps = CONV_B_TAPS
    off = hl - (taps - 1)
    last_blk = s // ts - 1

    def body(dy_ref, dyn_ref, x_ref, xp_ref, xn_ref, w_ref, dx_ref, dw_ref, cvbuf, dbuf, dw_acc):
        i = pl.program_id(0)
        hp = xp_ref[...]
        cvbuf[pl.ds(0, hl), :] = jnp.where(i == 0, 0.0, hp[:, d:2 * d] * hp[:, 2 * d:])
        gb, gc, v = x_ref[:, :d], x_ref[:, d:2 * d], x_ref[:, 2 * d:]
        cvbuf[pl.ds(hl, ts), :] = gc * v
        dy_v = dy_ref[...]
        dcc = dy_v * gb
        dbuf[pl.ds(0, ts), :] = dcc
        dbuf[pl.ds(ts, hl), :] = jnp.where(i == last_blk, 0.0, dyn_ref[...] * xn_ref[:, :d])

        @pl.when(i == 0)
        def _():
            dw_acc[...] = jnp.zeros_like(dw_acc)

        cc = jnp.zeros((ts, d), F32)
        dcv = jnp.zeros((ts, d), F32)
        for k in range(taps):
            win = cvbuf[pl.ds(off + k, ts), :]
            cc = cc + w_ref[pl.ds(k, 1), :] * win
            dcv = dcv + w_ref[pl.ds(k, 1), :] * dbuf[pl.ds(taps - 1 - k, ts), :]
            dw_acc[pl.ds(k * SUBLANES, SUBLANES), :] += _rowsum8(dcc * win)
        dx_ref[:, :d] = (dy_v * cc).astype(BF16)
        dx_ref[:, d:2 * d] = (dcv * v).astype(BF16)
        dx_ref[:, 2 * d:] = (dcv * gc).astype(BF16)

        @pl.when(i == last_blk)
        def _():
            dw_ref[...] = jnp.sum(dw_acc[...].reshape(hl, SUBLANES, d), axis=1)

    scratch = [pltpu.VMEM((hl + ts, d), F32), pltpu.VMEM((ts + hl, d), F32), pltpu.VMEM((hl * SUBLANES, d), F32)]
    sbytes = _nbytes((2 * ts + 2 * hl + hl * SUBLANES, d), F32)
    return _rows_call(
        name, body, s, ts,
        [(dy, _blk(ts, d)), (dy, _next_halo_spec(ts, hl, d, s)), (bcv, _blk(ts, d3)), (bcv, _prev_halo_spec(ts, hl, d3)),
         (bcv, _next_halo_spec(ts, hl, d3, s)), (w_conv, _full(w_conv.shape))],
        [(jax.ShapeDtypeStruct((s, d3), BF16), _blk(ts, d3)), (jax.ShapeDtypeStruct((hl, d), F32), _full((hl, d)))],
        scratch=scratch, scratch_bytes=sbytes)


def _row(a, i):
    return lax.slice_in_dim(a, i, i + 1, axis=0)


def _local_step(x, p, target, small, get_w, conv_w, put_grads):
    depth = p.shape[0]
    acts = []
    h = x
    for i in range(depth):
        j = i // 2
        act = {"h": h}
        u = rms_fwd(f"rms_mix_{i}", h, _row(small["norm_mix"], i))
        act["u"] = u
        if i % 2 == 0:
            a = mm_x_wcol(f"cf_pw1_{i}", u, get_w("cf_w_pw1", j, u), 0, extras=[(_row(small["cf_b_pw1"], j), "row")],
                          epi=lambda acc, b: (acc + b,))[0]
            v0, v1, v3 = cf_fwd_mid(f"cf_mid_{i}", a, conv_w["cf"][j], _row(small["cf_b_dw"], j), _row(small["cf_norm"], j))
            act.update(a=a, v0=v0, v1=v1, v3=v3)
            h1 = mm_x_wrow(f"cf_pw2_{i}", v3, get_w("cf_w_pw2", j, v3),
                           extras=[(_row(small["cf_b_pw2"], j), "row"), (h, "tile")],
                           epi=lambda acc, b, res: (res + (acc + b),))[0]
        else:
            bcv = mm_x_wcol(f"sc_in_{i}", u, get_w("sc_w_in", j, u), 0, tn=768)[0]
            y = sc_fwd_mid(f"sc_mid_{i}", bcv, conv_w["sc"][j])
            act.update(bcv=bcv, y=y)
            h1 = mm_x_wrow(f"sc_out_{i}", y, get_w("sc_w_out", j, y), extras=[(h, "tile")],
                           epi=lambda acc, res: (res + acc,))[0]
        act["h1"] = h1
        u2 = rms_fwd(f"rms_mlp_{i}", h1, _row(small["norm_mlp"], i))
        z, hd = mm_x_wcol(f"mlp_w1_{i}", u2, get_w("mlp_w1", i, u2), 0, outs_dtypes=(F32, BF16),
                          epi=lambda acc: (acc, jnp.square(jnp.maximum(acc, 0.0))))
        h2 = mm_x_wrow(f"mlp_w2_{i}", hd, get_w("mlp_w2", i, hd), extras=[(h1, "tile")],
                       epi=lambda acc, res: (res + acc,), tn=1024)[0]
        act.update(u2=u2, z=z, hd=hd, h2=h2)
        n3 = rms_fwd(f"rms_ple_{i}", h2, _row(small["norm_ple"], i))
        e = mm_x_wcol(f"ple_proj_{i}", p[i], get_w("ple_w_proj", i, n3), 0)[0]

        def ple_epi(acc, e_t, res):
            g_t = jax.nn.sigmoid(acc)
            return g_t, res + g_t * e_t

        g, h3 = mm_x_wrow(f"ple_gate_{i}", n3, get_w("ple_w_gate", i, e), extras=[(e, "tile"), (h2, "tile")],
                          outs_dtypes=(F32, F32), epi=ple_epi)
        act.update(n3=n3, e=e, g=g)
        acts.append(act)
        h = h3

    loss_part, dh, dg_final = final_loss("final_loss", h, small["norm_final"], target)
    sg = {k: [None] * small[k].shape[0] for k in small if k != "norm_final"}
    sg["norm_final"] = [dg_final]
    sg["cf_w_dw"] = [None] * conv_w["cf"].shape[0]
    sg["sc_w_conv"] = [None] * conv_w["sc"].shape[0]

    for i in reversed(range(depth)):
        j = i // 2
        act = acts[i]
        de, dgl = ple_bwd_elem(f"ple_bwd_{i}", dh, act["g"], act["e"])
        g_proj = mm_xt_dy(f"d_ple_proj_{i}", p[i], de, True)
        g_gate = mm_xt_dy(f"d_ple_gate_{i}", act["n3"], dgl, False)
        dn3 = mm_dy_wrow_t(f"dn3_{i}", dgl, get_w("ple_w_gate", i))[0]
        dh2, dh2b, sg["norm_ple"][i], _ = rms_bwd(f"rms_ple_bwd_{i}", dn3, act["h2"], _row(small["norm_ple"], i), dh)
        g_w2 = mm_xt_dy(f"d_mlp_w2_{i}", act["hd"], dh2b, False)
        dz = mm_dy_wrow_t(f"dz_{i}", dh2b, get_w("mlp_w2", i), extras=[(act["z"], "tile")], outs_dtypes=(BF16,),
                          epi=lambda acc, z_t: (acc * (2.0 * jnp.maximum(z_t, 0.0)),))[0]
        g_w1 = mm_xt_dy(f"d_mlp_w1_{i}", act["u2"], dz, True)
        put_grads({("ple_w_proj", i): g_proj, ("ple_w_gate", i): g_gate, ("mlp_w2", i): g_w2, ("mlp_w1", i): g_w1})
        du2 = mm_dy_wcol_t(f"du2_{i}", dz, get_w("mlp_w1", i), 0)[0]
        dh1, dh1b, sg["norm_mlp"][i], cs1 = rms_bwd(f"rms_mlp_bwd_{i}", du2, act["h1"], _row(small["norm_mlp"], i), dh2)
        if i % 2 == 0:
            g_out = mm_xt_dy(f"d_cf_pw2_{i}", act["v3"], dh1b, False)
            sg["cf_b_pw2"][j] = cs1
            dv3 = mm_dy_wrow_t(f"dv3_{i}", dh1b, get_w("cf_w_pw2", j))[0]
            dv1, sg["cf_norm"][j], sg["cf_b_dw"][j] = cf_bwd_rows(f"cf_bwd_rows_{i}", dv3, act["v1"], _row(small["cf_norm"], j))
            da, sg["cf_w_dw"][j], sg["cf_b_pw1"][j] = cf_bwd_conv(f"cf_bwd_conv_{i}", dv1, act["v0"], act["a"], conv_w["cf"][j])
            g_in = mm_xt_dy(f"d_cf_pw1_{i}", act["u"], da, True)
            put_grads({("cf_w_pw2", j): g_out, ("cf_w_pw1", j): g_in})
            du = mm_dy_wcol_t(f"du_{i}", da, get_w("cf_w_pw1", j), 0)[0]
        else:
            g_out = mm_xt_dy(f"d_sc_out_{i}", act["y"], dh1b, False)
            dy = mm_dy_wrow_t(f"dy_{i}", dh1b, get_w("sc_w_out", j))[0]
            dbcv, sg["sc_w_conv"][j] = sc_bwd_mid(f"sc_bwd_mid_{i}", dy, act["bcv"], conv_w["sc"][j])
            g_in = mm_xt_dy(f"d_sc_in_{i}", act["u"], dbcv, True, tn=768)
            put_grads({("sc_w_out", j): g_out, ("sc_w_in", j): g_in})
            du = mm_dy_wcol_t(f"du_{i}", dbcv, get_w("sc_w_in", j), 0)[0]
        dh, _, sg["norm_mix"][i], _ = rms_bwd(f"rms_mix_bwd_{i}", du, act["h"], _row(small["norm_mix"], i), dh1)
    return loss_part, dh, sg


def _me_and_peers():
    x, y, c = lax.axis_index("x"), lax.axis_index("y"), lax.axis_index("c")
    me = 4 * x + 2 * y + c
    peers = []
    for q in range(1, NDEV):
        px = 1 - x if q & 4 else x
        py = 1 - y if q & 2 else y
        pc = 1 - c if q & 1 else c
        peers.append(((px, py, pc), 4 * px + 2 * py + pc))
    return me, peers


def _exchange(name, srcs, out_shapes, src_fns, dst_fns, after=()):
    n = len(srcs)
    n_after = len(after)

    def body(*refs):
        ins, outs = refs[:n], refs[n + n_after:2 * n + n_after]
        send_sems, recv_sems, local_sems = refs[2 * n + n_after:]
        me, peers = _me_and_peers()
        local, remote = [], []
        for k in range(n):
            cp = pltpu.make_async_copy(src_fns[k](ins[k], me), dst_fns[k](outs[k], me), local_sems.at[k])
            cp.start()
            local.append(cp)
        for q, (peer, peer_blk) in enumerate(peers):
            for k in range(n):
                cp = pltpu.make_async_remote_copy(
                    src_ref=src_fns[k](ins[k], peer_blk), dst_ref=dst_fns[k](outs[k], me),
                    send_sem=send_sems.at[k, q], recv_sem=recv_sems.at[k, q],
                    device_id=peer, device_id_type=MESH)
                cp.start()
                remote.append(cp)
        for q, (peer, peer_blk) in enumerate(peers):
            for k in range(n):
                pltpu.make_async_remote_copy(
                    src_ref=src_fns[k](ins[k], peer_blk), dst_ref=dst_fns[k](outs[k], peer_blk),
                    send_sem=send_sems.at[k, q], recv_sem=recv_sems.at[k, q],
                    device_id=peer, device_id_type=MESH).wait_recv()
        for cp in remote:
            cp.wait_send()
        for cp in local:
            cp.wait()

    any_spec = pl.BlockSpec(memory_space=pl.ANY)
    return pl.pallas_call(
        body,
        name=name,
        in_specs=[any_spec] * (n + n_after),
        out_specs=[any_spec] * n,
        out_shape=out_shapes,
        scratch_shapes=[pltpu.SemaphoreType.DMA((n, N_PEERS)), pltpu.SemaphoreType.DMA((n, N_PEERS)),
                        pltpu.SemaphoreType.DMA((n,))],
    )(*srcs, *after)


def sc_exchange(name, collective_id, srcs, out_shapes, src_fns, dst_fns):
    n = len(srcs)

    def body(*refs):
        ins, outs = refs[:n], refs[n:2 * n]
        send_sems, recv_sems, local_sems = refs[2 * n:]
        me, peers = _me_and_peers()
        barrier = pltpu.get_barrier_semaphore()
        for peer, _ in peers:
            pl.semaphore_signal(barrier, inc=1, device_id=peer, device_id_type=MESH)
        pl.semaphore_wait(barrier, N_PEERS)
        local, remote = [], []
        for k in range(n):
            cp = pltpu.make_async_copy(src_fns[k](ins[k], me), dst_fns[k](outs[k], me), local_sems.at[k])
            cp.start()
            local.append(cp)
        for q, (peer, peer_blk) in enumerate(peers):
            for k in range(n):
                cp = pltpu.make_async_remote_copy(
                    src_ref=src_fns[k](ins[k], peer_blk), dst_ref=dst_fns[k](outs[k], me),
                    send_sem=send_sems.at[k, q], recv_sem=recv_sems.at[k, q],
                    device_id=peer, device_id_type=MESH)
                cp.start()
                remote.append(cp)
        for q, (peer, peer_blk) in enumerate(peers):
            for k in range(n):
                pltpu.make_async_remote_copy(
                    src_ref=src_fns[k](ins[k], peer_blk), dst_ref=dst_fns[k](outs[k], peer_blk),
                    send_sem=send_sems.at[k, q], recv_sem=recv_sems.at[k, q],
                    device_id=peer, device_id_type=MESH).wait_recv()
        for cp in remote:
            cp.wait_send()
        for cp in local:
            cp.wait()

    return pl.kernel(
        body,
        out_type=out_shapes,
        mesh=plsc.ScalarSubcoreMesh(axis_name="sequencer", num_cores=1),
        name=name,
        scratch_types=[pltpu.SemaphoreType.DMA((n, N_PEERS)), pltpu.SemaphoreType.DMA((n, N_PEERS)),
                       pltpu.SemaphoreType.DMA((n,))],
        compiler_params=pltpu.CompilerParams(collective_id=collective_id),
    )(*srcs)


def sc_gather(name, collective_id, srcs, layers):
    n = len(srcs)
    outs_shape = [jax.ShapeDtypeStruct((NDEV, 1) + a.shape[1:], a.dtype) for a in srcs]

    def body(*refs):
        ins, outs = refs[:n], refs[n:2 * n]
        send_sems, recv_sems, local_sems = refs[2 * n:]
        x, y, c = lax.axis_index("x"), lax.axis_index("y"), lax.axis_index("c")
        me = 4 * x + 2 * y + c
        sibling = (x, y, 1 - c)
        chips = [(1 - x, y), (x, 1 - y), (1 - x, 1 - y)]
        barrier = pltpu.get_barrier_semaphore()
        for peer in [sibling] + [(cx, cy, c) for cx, cy in chips]:
            pl.semaphore_signal(barrier, inc=1, device_id=peer, device_id_type=MESH)
        pl.semaphore_wait(barrier, 1 + len(chips))

        def copy(k, slot, blk, to, src=None):
            place = outs[k].at[blk, 0]
            return pltpu.make_async_remote_copy(
                src_ref=place if src is None else src, dst_ref=place,
                send_sem=send_sems.at[k, slot], recv_sem=recv_sems.at[k, slot],
                device_id=to, device_id_type=MESH)

        local, sent = [], []
        for k in range(n):
            mine = ins[k].at[layers[k]]
            cp = pltpu.make_async_copy(mine, outs[k].at[me, 0], local_sems.at[k])
            cp.start()
            local.append(cp)
            sent.append(copy(k, 0, me, sibling, src=mine))
            sent += [copy(k, 1 + j, me, (cx, cy, c), src=mine) for j, (cx, cy) in enumerate(chips)]
        for cp in sent:
            cp.start()
        for k in range(n):
            for j, (cx, cy) in enumerate(chips):
                blk = 4 * cx + 2 * cy + c
                copy(k, 1 + j, blk, sibling).wait_recv()
                fwd = copy(k, 4 + j, blk, sibling)
                fwd.start()
                sent.append(fwd)
        for k in range(n):
            copy(k, 0, 4 * x + 2 * y + (1 - c), sibling).wait_recv()
            for j, (cx, cy) in enumerate(chips):
                copy(k, 4 + j, 4 * cx + 2 * cy + (1 - c), sibling).wait_recv()
        for cp in sent:
            cp.wait_send()
        for cp in local:
            cp.wait()

    return pl.kernel(
        body,
        out_type=outs_shape,
        mesh=plsc.ScalarSubcoreMesh(axis_name="sequencer", num_cores=1),
        name=name,
        scratch_types=[pltpu.SemaphoreType.DMA((n, N_PEERS)), pltpu.SemaphoreType.DMA((n, N_PEERS)),
                       pltpu.SemaphoreType.DMA((n,))],
        compiler_params=pltpu.CompilerParams(collective_id=collective_id),
    )(*srcs)


def _gather_src(layer):
    return lambda ref, blk: ref.at[layer]


def _gather_dst(ref, blk):
    return ref.at[blk, 0]


def _slice_of(ref, blk):
    return ref.at[blk]


def cast_bf16(name, w, tr_elems=512 * 1024):
    l, r, c = w.shape
    tr = _row_tile(r, tr_elems // c)
    spec = pl.BlockSpec((None, tr, c), lambda li, i: (li, i, 0))

    def body(w_ref, o_ref):
        o_ref[...] = w_ref[...].astype(BF16)

    return pl.pallas_call(
        body, name=name, grid=(l, r // tr), in_specs=[spec], out_specs=spec,
        out_shape=jax.ShapeDtypeStruct(w.shape, BF16),
        compiler_params=_params(("parallel", "parallel"), 6 * tr * c),
    )(w)


def _adamw_math(w, g, m, v):
    m = ADAM_B1 * m + (1.0 - ADAM_B1) * g
    v = ADAM_B2 * v + (1.0 - ADAM_B2) * (g * g)
    m_hat = m * (1.0 / (1.0 - ADAM_B1 ** ADAM_STEP))
    v_hat = v * (1.0 / (1.0 - ADAM_B2 ** ADAM_STEP))
    delta = -ADAM_LR * (m_hat / (jnp.sqrt(v_hat) + ADAM_EPS) + ADAM_WD * w)
    return delta, m, v


def _sum_blocks(ref):
    g = ref[0].astype(F32)
    for d in range(1, ref.shape[0]):
        g = g + ref[d].astype(F32)
    return g


def adamw_layer(name, recv, w, m, v, layer, stacked, tr_elems=256 * 1024):
    nd, r, c = recv.shape
    tr = _row_tile(r, tr_elems // c)
    r_spec = pl.BlockSpec((nd, tr, c), lambda i: (0, i, 0))
    w_spec = pl.BlockSpec((None, tr, c), lambda i: (layer, i, 0))
    if stacked is None:
        stacked = [lax.empty(w.shape, F32) for _ in range(4)]

    def body(r_ref, w_ref, m_ref, v_ref, g_in, d_in, m_in, v_in, g_out, d_out, m_out, v_out):
        g = _sum_blocks(r_ref)
        delta, m_new, v_new = _adamw_math(w_ref[...], g, m_ref[...], v_ref[...])
        g_out[...] = g
        d_out[...] = delta
        m_out[...] = m_new
        v_out[...] = v_new

    out = jax.ShapeDtypeStruct(w.shape, F32)
    return pl.pallas_call(
        body, name=name, grid=(r // tr,),
        in_specs=[r_spec, w_spec, w_spec, w_spec] + [pl.BlockSpec(memory_space=pl.ANY)] * 4,
        out_specs=[w_spec] * 4, out_shape=[out] * 4,
        input_output_aliases={4: 0, 5: 1, 6: 2, 7: 3},
        compiler_params=_params(("parallel",), tr * c * (2 * nd + 7 * 4)),
    )(recv, w, m, v, *stacked)


def pack_small_grads(name, parts, taps, n_blocks):
    d = parts[0].shape[1]
    n_p, rows = len(parts), [t.shape[0] for t in taps]
    cb = d // n_blocks

    def body(*refs):
        part_refs, tap_refs = refs[:n_p], refs[n_p:n_p + len(taps)]
        sums_out, taps_out = refs[n_p + len(taps):]
        for i, r in enumerate(part_refs):
            sums_out[pl.ds(i, 1), :] = jnp.sum(r[...], axis=0, keepdims=True)
        r0 = 0
        for t_ref, n in zip(tap_refs, rows):
            for b in range(n_blocks):
                taps_out[b, pl.ds(r0, n), :] = t_ref[:, pl.ds(b * cb, cb)]
            r0 += n

    vm = pl.BlockSpec(memory_space=pltpu.VMEM)
    return pl.pallas_call(
        body, name=name, in_specs=[vm] * (n_p + len(taps)), out_specs=[vm] * 2,
        out_shape=[jax.ShapeDtypeStruct((n_p, d), F32), jax.ShapeDtypeStruct((n_blocks, sum(rows), cb), F32)],
    )(*parts, *taps)


def small_update(name, part_g, tap_g, w_a, m_a, v_a, w_b, m_b, v_b):
    nd, na, d = part_g.shape
    nb, cb = w_b.shape

    def body(pg_ref, tg_ref, wa_ref, ma_ref, va_ref, wb_ref, mb_ref, vb_ref,
             ga_out, da_out, ma_out, va_out, gb_out, db_out, mb_out, vb_out, loss_out):
        ga = _sum_blocks(pg_ref)
        delta, m_new, v_new = _adamw_math(wa_ref[...], ga, ma_ref[...], va_ref[...])
        ga_out[...] = ga
        da_out[...] = delta
        ma_out[...] = m_new
        va_out[...] = v_new
        loss_out[...] = jnp.broadcast_to(jnp.sum(ga[na - 1:na, :], axis=1, keepdims=True), loss_out.shape)
        gb = _sum_blocks(tg_ref)
        delta, m_new, v_new = _adamw_math(wb_ref[...], gb, mb_ref[...], vb_ref[...])
        gb_out[...] = gb
        db_out[...] = delta
        mb_out[...] = m_new
        vb_out[...] = v_new

    oa, ob = jax.ShapeDtypeStruct((na, d), F32), jax.ShapeDtypeStruct((nb, cb), F32)
    vm = pl.BlockSpec(memory_space=pltpu.VMEM)
    return pl.pallas_call(
        body, name=name, in_specs=[vm] * 8, out_specs=[vm] * 9,
        out_shape=[oa] * 4 + [ob] * 4 + [jax.ShapeDtypeStruct((1, LANES), F32)],
        compiler_params=pltpu.CompilerParams(vmem_limit_bytes=_vmem_limit(_nbytes(part_g.shape, F32))),
    )(part_g, tap_g, w_a, m_a, v_a, w_b, m_b, v_b)


BIG = ("cf_w_pw1", "cf_w_pw2", "sc_w_in", "sc_w_out", "mlp_w1", "mlp_w2", "ple_w_proj", "ple_w_gate")
COL_SHARDED = ("cf_w_pw1", "sc_w_in", "mlp_w1", "ple_w_proj")
WEIGHT_ORDER = ("norm_mix", "norm_mlp", "norm_ple", "cf_w_pw1", "cf_b_pw1", "cf_w_dw", "cf_b_dw", "cf_norm",
                "cf_w_pw2", "cf_b_pw2", "sc_w_in", "sc_w_conv", "sc_w_out", "mlp_w1", "mlp_w2", "ple_w_proj",
                "ple_w_gate", "norm_final")
SMALL_ROWS = (("norm_mix", 4), ("norm_mlp", 4), ("norm_ple", 4), ("cf_b_pw1", 4), ("cf_b_dw", 2), ("cf_norm", 2),
              ("cf_b_pw2", 2), ("norm_final", 1))


def _layer_weights(i):
    mixer = (("cf_w_pw1", i // 2), ("cf_w_pw2", i // 2)) if i % 2 == 0 else (("sc_w_in", i // 2), ("sc_w_out", i // 2))
    return mixer + (("mlp_w1", i), ("mlp_w2", i), ("ple_w_proj", i), ("ple_w_gate", i))


def _pad_rows(a, rows):
    return jnp.pad(a, ((0, 0), (0, rows - a.shape[1]), (0, 0)))


def _pack_taps(cf, sc):
    c = cf.shape[2]
    return jnp.concatenate([_pad_rows(cf, CONV_A_HALO).reshape(-1, c), _pad_rows(sc, CONV_B_HALO).reshape(-1, c)], axis=0)


def _unpack_taps(t, n_cf):
    c = t.shape[1]
    cf = t[:n_cf * CONV_A_HALO].reshape(n_cf, CONV_A_HALO, c)[:, :CONV_A_TAPS]
    sc = t[n_cf * CONV_A_HALO:].reshape(-1, CONV_B_HALO, c)[:, :CONV_B_TAPS]
    return cf, sc


def _pack_small(vals, d):
    return jnp.concatenate([vals[k].reshape(-1, d) for k, _ in SMALL_ROWS] + [jnp.zeros((1, d), F32)], axis=0)


def _unpack_small(a, shapes):
    out, r = {}, 0
    for k, n in SMALL_ROWS:
        out[k] = a[r:r + n].reshape(shapes[k])
        r += n
    return out


def kernel(x, p, norm_mix, norm_mlp, norm_ple, cf_w_pw1, cf_b_pw1, cf_w_dw, cf_b_dw, cf_norm, cf_w_pw2, cf_b_pw2, sc_w_in, sc_w_conv, sc_w_out, mlp_w1, mlp_w2, ple_w_proj, ple_w_gate, norm_final, loss_target, m_norm_mix, m_norm_mlp, m_norm_ple, m_cf_w_pw1, m_cf_b_pw1, m_cf_w_dw, m_cf_b_dw, m_cf_norm, m_cf_w_pw2, m_cf_b_pw2, m_sc_w_in, m_sc_w_conv, m_sc_w_out, m_mlp_w1, m_mlp_w2, m_ple_w_proj, m_ple_w_gate, m_norm_final, v_norm_mix, v_norm_mlp, v_norm_ple, v_cf_w_pw1, v_cf_b_pw1, v_cf_w_dw, v_cf_b_dw, v_cf_norm, v_cf_w_pw2, v_cf_b_pw2, v_sc_w_in, v_sc_w_conv, v_sc_w_out, v_mlp_w1, v_mlp_w2, v_ple_w_proj, v_ple_w_gate, v_norm_final):
    w = dict(norm_mix=norm_mix, norm_mlp=norm_mlp, norm_ple=norm_ple, cf_w_pw1=cf_w_pw1, cf_b_pw1=cf_b_pw1,
             cf_w_dw=cf_w_dw, cf_b_dw=cf_b_dw, cf_norm=cf_norm, cf_w_pw2=cf_w_pw2, cf_b_pw2=cf_b_pw2,
             sc_w_in=sc_w_in, sc_w_conv=sc_w_conv, sc_w_out=sc_w_out, mlp_w1=mlp_w1, mlp_w2=mlp_w2,
             ple_w_proj=ple_w_proj, ple_w_gate=ple_w_gate, norm_final=norm_final)
    m = dict(norm_mix=m_norm_mix, norm_mlp=m_norm_mlp, norm_ple=m_norm_ple, cf_w_pw1=m_cf_w_pw1, cf_b_pw1=m_cf_b_pw1,
             cf_w_dw=m_cf_w_dw, cf_b_dw=m_cf_b_dw, cf_norm=m_cf_norm, cf_w_pw2=m_cf_w_pw2, cf_b_pw2=m_cf_b_pw2,
             sc_w_in=m_sc_w_in, sc_w_conv=m_sc_w_conv, sc_w_out=m_sc_w_out, mlp_w1=m_mlp_w1, mlp_w2=m_mlp_w2,
             ple_w_proj=m_ple_w_proj, ple_w_gate=m_ple_w_gate, norm_final=m_norm_final)
    v = dict(norm_mix=v_norm_mix, norm_mlp=v_norm_mlp, norm_ple=v_norm_ple, cf_w_pw1=v_cf_w_pw1, cf_b_pw1=v_cf_b_pw1,
             cf_w_dw=v_cf_w_dw, cf_b_dw=v_cf_b_dw, cf_norm=v_cf_norm, cf_w_pw2=v_cf_w_pw2, cf_b_pw2=v_cf_b_pw2,
             sc_w_in=v_sc_w_in, sc_w_conv=v_sc_w_conv, sc_w_out=v_sc_w_out, mlp_w1=v_mlp_w1, mlp_w2=v_mlp_w2,
             ple_w_proj=v_ple_w_proj, ple_w_gate=v_ple_w_gate, norm_final=v_norm_final)
    depth, d = norm_mix.shape
    n_cf = cf_w_dw.shape[0]

    taps_w = _pack_taps(cf_w_dw, sc_w_conv)
    shards = {k: cast_bf16(f"cast_{k}", w[k]) for k in BIG}
    gathered = {}
    ids = iter(range(4 * depth))
    for i in range(depth):
        layer_names = _layer_weights(i)
        for part, names in (("mixer", layer_names[:2]), ("mlp", layer_names[2:])):
            first = i == 0 and part == "mixer"
            srcs = [shards[k] for k, _ in names] + ([taps_w[None]] if first else [])
            got = sc_gather(f"gather_{part}_{i}", next(ids), srcs, [l for _, l in names] + ([0] if first else []))
            for (k, l), g in zip(names, got):
                gathered[(k, l)] = g if k in COL_SHARDED else g.reshape(-1, g.shape[3])
            if first:
                taps_full = jnp.transpose(got[-1][:, 0], (1, 0, 2)).reshape(taps_w.shape[0], d)
    conv_w = {"cf": taps_full[:n_cf * CONV_A_HALO].reshape(n_cf, CONV_A_HALO, d),
              "sc": taps_full[n_cf * CONV_A_HALO:].reshape(-1, CONV_B_HALO, d)}

    def get_w(k, l, after=None):
        return gathered[(k, l)]

    received, waiting = {}, {}
    last_group = list(_layer_weights(0)[:2])

    def put_grads(grads):
        names = list(grads)
        if names[0][0] in ("cf_w_pw2", "sc_w_out") and set(names) != set(last_group):
            waiting.update(grads)
            return
        grads = {**waiting, **grads}
        waiting.clear()
        names = list(grads)
        got = sc_exchange(f"grad_exchange_{names[-1][0]}_{names[-1][1]}", next(ids), [grads[n] for n in names],
                          [jax.ShapeDtypeStruct(grads[n].shape, BF16) for n in names],
                          [_slice_of] * len(names), [_slice_of] * len(names))
        received.update(zip(names, got))

    small = {k: w[k] for k, _ in SMALL_ROWS}
    small["norm_final"] = norm_final[None]
    loss_part, grad_x, sg = _local_step(x[0], p[:, 0], loss_target[0], small, get_w, conv_w, put_grads)

    out = {k: None for k in BIG}
    for (k, l), recv in received.items():
        if (k, l) not in last_group:
            out[k] = adamw_layer(f"adamw_{k}_{l}", recv, w[k], m[k], v[k], l, out[k])
    updated_first = [out[k][0] for k in BIG if out[k] is not None and k not in [n for n, _ in last_group]]

    parts = []
    for k, n in SMALL_ROWS:
        for g in sg[k]:
            parts += [g[:, :d], g[:, d:]] if g.shape[1] == 2 * d else [g]
    parts.append(loss_part)
    sums, tap_slices = pack_small_grads("pack_small_grads", parts, sg["cf_w_dw"] + sg["sc_w_conv"], NDEV)
    part_all, tap_mine = _exchange(
        "small_exchange", [sums[None], tap_slices],
        [jax.ShapeDtypeStruct((NDEV, 1) + sums.shape, F32), jax.ShapeDtypeStruct(tap_slices.shape, F32)],
        [_gather_src(0), _slice_of], [_gather_dst, _slice_of], after=updated_first)
    for k, l in last_group:
        out[k] = adamw_layer(f"adamw_{k}_{l}", received[(k, l)], w[k], m[k], v[k], l, out[k])
    sm = small_update("small_update", part_all[:, 0], tap_mine,
                      _pack_small(w, d), _pack_small(m, d), _pack_small(v, d),
                      taps_w, _pack_taps(m["cf_w_dw"], m["sc_w_conv"]), _pack_taps(v["cf_w_dw"], v["sc_w_conv"]))
    shapes = {k: w[k].shape for k, _ in SMALL_ROWS}
    for t in range(4):
        un = _unpack_small(sm[t], shapes)
        cf_t, sc_t = _unpack_taps(sm[4 + t], n_cf)
        for k in un:
            out.setdefault(k, [None] * 4)[t] = un[k]
        out.setdefault("cf_w_dw", [None] * 4)[t] = cf_t
        out.setdefault("sc_w_conv", [None] * 4)[t] = sc_t
    loss = sm[8][0, 0]

    return (loss, grad_x[None], *[out[k][0] for k in WEIGHT_ORDER], *[out[k][1] for k in WEIGHT_ORDER],
            *[out[k][2] for k in WEIGHT_ORDER], *[out[k][3] for k in WEIGHT_ORDER])
```

```python
import jax
import jax.numpy as jnp
from jax import lax
from jax.experimental import pallas as pl
from jax.experimental.pallas import tpu as pltpu
from jax.experimental.pallas import tpu_sc as plsc

F32 = jnp.float32
BF16 = jnp.bfloat16
EPS = 1e-6
NDEV = 8
N_PEERS = NDEV - 1
MESH = pl.DeviceIdType.MESH

ADAM_LR = 0.001
ADAM_B1 = 0.9
ADAM_B2 = 0.999
ADAM_EPS = 1e-08
ADAM_WD = 0.01
ADAM_STEP = 10

V7X_VMEM_BYTES = 64 * 1024 * 1024
VMEM_LIMIT_MAX = 56 * 1024 * 1024
SUBLANES = 8
LANES = 128
CONV_A_TAPS = 31
CONV_A_HALO = 32
CONV_B_TAPS = 3
CONV_B_HALO = 8


def _nbytes(shape, dtype):
    n = 1
    for s in shape:
        if s is not None:
            n *= s
    return n * jnp.dtype(dtype).itemsize


def _vmem_limit(block_bytes, scratch_bytes=0):
    need = 2 * block_bytes + scratch_bytes
    return int(min(VMEM_LIMIT_MAX, max(32 * 1024 * 1024, need + need // 2 + (4 << 20))))


def _params(sem, block_bytes, scratch_bytes=0):
    return pltpu.CompilerParams(dimension_semantics=sem, vmem_limit_bytes=_vmem_limit(block_bytes, scratch_bytes))


_DIMS = {
    "nn": (((1,), (0,)), ((), ())),
    "nt": (((1,), (1,)), ((), ())),
    "tn": (((0,), (0,)), ((), ())),
}


def _mm(name, dims, grid, acc_shape, a, a_spec, b, b_spec, extras, outs, epi):
    ni, nj, nk = grid
    n_ex, n_out = len(extras), len(outs)
    dn = _DIMS[dims]

    def body(*refs):
        a_ref, b_ref = refs[0], refs[1]
        ex_refs = refs[2:2 + n_ex]
        out_refs = refs[2 + n_ex:2 + n_ex + n_out]
        d = lax.dot_general(a_ref[...].astype(BF16), b_ref[...].astype(BF16), dn, preferred_element_type=F32)

        def finish(acc):
            res = epi(acc, *[r[...] for r in ex_refs])
            for o_ref, r in zip(out_refs, res):
                o_ref[...] = r.astype(o_ref.dtype)

        if nk == 1:
            finish(d)
        else:
            acc_ref = refs[2 + n_ex + n_out]
            k = pl.program_id(2)

            @pl.when(k == 0)
            def _():
                acc_ref[...] = d

            @pl.when(jnp.logical_and(k > 0, k < nk - 1))
            def _():
                acc_ref[...] += d

            @pl.when(k == nk - 1)
            def _():
                finish(acc_ref[...] + d)

    blk = _nbytes(a_spec.block_shape, a.dtype) + _nbytes(b_spec.block_shape, b.dtype)
    for arr, spec in list(extras) + list(outs):
        blk += _nbytes(spec.block_shape, arr.dtype)
    acc_bytes = _nbytes(acc_shape, F32)
    scratch = [pltpu.VMEM(acc_shape, F32)] if nk > 1 else []
    return pl.pallas_call(
        body,
        name=name,
        grid=grid,
        in_specs=[a_spec, b_spec] + [s for _, s in extras],
        out_specs=[s for _, s in outs],
        out_shape=[o for o, _ in outs],
        scratch_shapes=scratch,
        compiler_params=_params(("parallel", "parallel", "arbitrary"), blk, 3 * acc_bytes),
    )(a, b, *[e for e, _ in extras])


def _tile(n, pref):
    if n <= pref:
        return n
    t = pref - pref % LANES
    while t > LANES and n % t:
        t -= LANES
    assert n % t == 0, (n, pref)
    return t


def _row_tile(n, pref):
    if n <= pref:
        return n
    t = max(SUBLANES, pref - pref % SUBLANES)
    while t > SUBLANES and n % t:
        t -= SUBLANES
    assert n % t == 0, (n, pref)
    return t


def _id_epi(acc):
    return (acc,)


def mm_x_wcol(name, x, w, layer, extras=(), outs_dtypes=(F32,), epi=_id_epi, tm=1024, tn=512):
    m, kdim = x.shape
    c = w.shape[3]
    n = NDEV * c
    tm, tn = _tile(m, tm), _tile(c, tn)
    tk = _tile(kdim, 2048)
    grid = (m // tm, n // tn, kdim // tk)
    per = c // tn
    a_spec = pl.BlockSpec((tm, tk), lambda i, j, k: (i, k))
    b_spec = pl.BlockSpec((None, None, tk, tn), lambda i, j, k: (j // per, layer, k, j % per))
    ex = [(e, _ex_spec(e, kind, tm, tn)) for e, kind in extras]
    o_spec = pl.BlockSpec((tm, tn), lambda i, j, k: (i, j))
    outs = [(jax.ShapeDtypeStruct((m, n), dt), o_spec) for dt in outs_dtypes]
    return _mm(name, "nn", grid, (tm, tn), x, a_spec, w, b_spec, ex, outs, epi)


def mm_x_wrow(name, x, w, extras=(), outs_dtypes=(F32,), epi=_id_epi, tm=1024, tn=512):
    m, kdim = x.shape
    n = w.shape[1]
    assert kdim == w.shape[0]
    tm, tn = _tile(m, tm), _tile(n, tn)
    tk = _tile(kdim, 2048)
    grid = (m // tm, n // tn, kdim // tk)
    a_spec = pl.BlockSpec((tm, tk), lambda i, j, k: (i, k))
    b_spec = pl.BlockSpec((tk, tn), lambda i, j, k: (k, j))
    ex = [(e, _ex_spec(e, kind, tm, tn)) for e, kind in extras]
    o_spec = pl.BlockSpec((tm, tn), lambda i, j, k: (i, j))
    outs = [(jax.ShapeDtypeStruct((m, n), dt), o_spec) for dt in outs_dtypes]
    return _mm(name, "nn", grid, (tm, tn), x, a_spec, w, b_spec, ex, outs, epi)


def mm_dy_wcol_t(name, dy, w, layer, extras=(), outs_dtypes=(F32,), epi=_id_epi, tm=1024, tn=1024):
    m, n = dy.shape
    kdim, c = w.shape[2], w.shape[3]
    assert n == NDEV * c
    tm, tn = _tile(m, tm), _tile(kdim, tn)
    tk = _tile(c, 2048)
    per = c // tk
    grid = (m // tm, kdim // tn, n // tk)
    a_spec = pl.BlockSpec((tm, tk), lambda i, j, k: (i, k))
    b_spec = pl.BlockSpec((None, None, tn, tk), lambda i, j, k: (k // per, layer, j, k % per))
    ex = [(e, _ex_spec(e, kind, tm, tn)) for e, kind in extras]
    o_spec = pl.BlockSpec((tm, tn), lambda i, j, k: (i, j))
    outs = [(jax.ShapeDtypeStruct((m, kdim), dt), o_spec) for dt in outs_dtypes]
    return _mm(name, "nt", grid, (tm, tn), dy, a_spec, w, b_spec, ex, outs, epi)


def mm_dy_wrow_t(name, dy, w, extras=(), outs_dtypes=(F32,), epi=_id_epi, tm=1024, tn=512):
    m, n = dy.shape
    kdim = w.shape[0]
    assert n == w.shape[1]
    tm, tn = _tile(m, tm), _tile(kdim, tn)
    tk = _tile(n, 2048)
    grid = (m // tm, kdim // tn, n // tk)
    a_spec = pl.BlockSpec((tm, tk), lambda i, j, k: (i, k))
    b_spec = pl.BlockSpec((tn, tk), lambda i, j, k: (j, k))
    ex = [(e, _ex_spec(e, kind, tm, tn)) for e, kind in extras]
    o_spec = pl.BlockSpec((tm, tn), lambda i, j, k: (i, j))
    outs = [(jax.ShapeDtypeStruct((m, kdim), dt), o_spec) for dt in outs_dtypes]
    return _mm(name, "nt", grid, (tm, tn), dy, a_spec, w, b_spec, ex, outs, epi)


def mm_xt_dy(name, x, dy, col_shards, tm=1024, tn=512):
    m, kdim = x.shape
    n = dy.shape[1]
    tk = _tile(m, 2048)
    if col_shards:
        c = n // NDEV
        tm, tn = _tile(kdim, tm), _tile(c, tn)
        per = c // tn
        out = jax.ShapeDtypeStruct((NDEV, kdim, c), BF16)
        o_spec = pl.BlockSpec((None, tm, tn), lambda i, j, k: (j // per, i, j % per))
    else:
        tm, tn = _tile(kdim, tm), _tile(n, tn)
        out = jax.ShapeDtypeStruct((kdim, n), BF16)
        o_spec = pl.BlockSpec((tm, tn), lambda i, j, k: (i, j))
    grid = (kdim // tm, n // tn, m // tk)
    a_spec = pl.BlockSpec((tk, tm), lambda i, j, k: (k, i))
    b_spec = pl.BlockSpec((tk, tn), lambda i, j, k: (k, j))
    g = _mm(name, "tn", grid, (tm, tn), x, a_spec, dy, b_spec, [], [(out, o_spec)], _id_epi)[0]
    return g if col_shards else g.reshape(NDEV, kdim // NDEV, n)


def _ex_spec(e, kind, tm, tn):
    if kind == "tile":
        return pl.BlockSpec((tm, tn), lambda i, j, k: (i, j))
    if kind == "row":
        return pl.BlockSpec((1, tn), lambda i, j, k: (0, j))
    raise ValueError(kind)


def _rows_call(name, body, n_rows, ts, ins, outs, scratch=(), scratch_bytes=0):
    blk = sum(_nbytes(s.block_shape, a.dtype) for a, s in list(ins) + list(outs))
    return pl.pallas_call(
        body,
        name=name,
        grid=(n_rows // ts,),
        in_specs=[s for _, s in ins],
        out_specs=[s for _, s in outs],
        out_shape=[o for o, _ in outs],
        scratch_shapes=list(scratch),
        compiler_params=_params(("arbitrary",), blk, scratch_bytes + 4 * blk // 2),
    )(*[a for a, _ in ins])


def _blk(ts, d):
    return pl.BlockSpec((ts, d), lambda i: (i, 0))


def _full(shape):
    return pl.BlockSpec(shape, lambda i: tuple(0 for _ in shape))


def _rowsum8(v):
    t, d = v.shape
    return jnp.sum(v.reshape(t // SUBLANES, SUBLANES, d), axis=0)


def _accumulate(ref, val):
    @pl.when(pl.program_id(0) == 0)
    def _():
        ref[...] = val

    @pl.when(pl.program_id(0) > 0)
    def _():
        ref[...] += val


def _rstd(x):
    return lax.rsqrt(jnp.mean(x * x, axis=-1, keepdims=True) + EPS)


def _rms_bwd_math(dy, x, g):
    r = _rstd(x)
    gdy = dy * g
    c = jnp.mean(gdy * x, axis=-1, keepdims=True)
    dx = r * gdy - x * (r * r * r * c)
    return dx, dy * (x * r)


def rms_fwd(name, h, g, ts=512):
    s, d = h.shape
    ts = min(ts, s)

    def body(h_ref, g_ref, u_ref):
        x = h_ref[...]
        u_ref[...] = ((x * _rstd(x)) * g_ref[...]).astype(BF16)

    return _rows_call(name, body, s, ts, [(h, _blk(ts, d)), (g, _full((1, d)))],
                      [(jax.ShapeDtypeStruct((s, d), BF16), _blk(ts, d))])[0]


def rms_bwd(name, du, h, g, dres, ts=256):
    s, d = h.shape
    ts = min(ts, s)

    def body(du_ref, h_ref, g_ref, dres_ref, dh_ref, dhb_ref, dg_ref, cs_ref):
        dx, dg = _rms_bwd_math(du_ref[...], h_ref[...], g_ref[...])
        dh = dres_ref[...] + dx
        dh_ref[...] = dh
        dhb_ref[...] = dh.astype(BF16)
        _accumulate(dg_ref, _rowsum8(dg))
        _accumulate(cs_ref, _rowsum8(dh))

    return _rows_call(
        name, body, s, ts,
        [(du, _blk(ts, d)), (h, _blk(ts, d)), (g, _full((1, d))), (dres, _blk(ts, d))],
        [(jax.ShapeDtypeStruct((s, d), F32), _blk(ts, d)), (jax.ShapeDtypeStruct((s, d), BF16), _blk(ts, d)),
         (jax.ShapeDtypeStruct((SUBLANES, d), F32), _full((SUBLANES, d))),
         (jax.ShapeDtypeStruct((SUBLANES, d), F32), _full((SUBLANES, d)))])


def final_loss(name, h, g, target, ts=256):
    s, d = h.shape
    ts = min(ts, s)

    def body(h_ref, g_ref, t_ref, loss_ref, dh_ref, dg_ref):
        x = h_ref[...]
        gf = g_ref[...]
        y = (x * _rstd(x)) * gf
        err = y - t_ref[...]
        _accumulate(loss_ref, _rowsum8(err * err) * (0.5 / d))
        dx, dg = _rms_bwd_math(err * (1.0 / d), x, gf)
        dh_ref[...] = dx
        _accumulate(dg_ref, _rowsum8(dg))

    return _rows_call(
        name, body, s, ts,
        [(h, _blk(ts, d)), (g, _full((1, d))), (target, _blk(ts, d))],
        [(jax.ShapeDtypeStruct((SUBLANES, d), F32), _full((SUBLANES, d))),
         (jax.ShapeDtypeStruct((s, d), F32), _blk(ts, d)),
         (jax.ShapeDtypeStruct((SUBLANES, d), F32), _full((SUBLANES, d)))])


def ple_bwd_elem(name, dh, g, e, ts=512):
    s, d = dh.shape
    ts = min(ts, s)

    def body(dh_ref, g_ref, e_ref, de_ref, dgl_ref):
        dh_v, g_v = dh_ref[...], g_ref[...]
        de_ref[...] = (dh_v * g_v).astype(BF16)
        dgl_ref[...] = (dh_v * e_ref[...] * (g_v * (1.0 - g_v))).astype(BF16)

    return _rows_call(name, body, s, ts, [(dh, _blk(ts, d)), (g, _blk(ts, d)), (e, _blk(ts, d))],
                      [(jax.ShapeDtypeStruct((s, d), BF16), _blk(ts, d)),
                       (jax.ShapeDtypeStruct((s, d), BF16), _blk(ts, d))])


CONV_LANES = 256
CONV_ROWS = 64


def _lane_chunks(d, fn):
    lc = min(CONV_LANES, d)

    def lane_body(c, carry):
        fn(pl.ds(pl.multiple_of(c * lc, lc), lc))
        return carry

    lax.fori_loop(0, d // lc, lane_body, 0)


def _shifted_copies(buf, sh, lanes):
    rows = buf.shape[0] - SUBLANES
    for s in range(1, SUBLANES):
        sh[s, pl.ds(0, rows), :] = buf[pl.ds(s, rows), lanes]


def _window(buf, sh, lanes, start, rows):
    s = start % SUBLANES
    if s == 0:
        return buf[pl.ds(start, rows), lanes]
    return sh[s, pl.ds(start - s, rows), :]


def _prev_halo_spec(ts, halo, width):
    per = ts // halo
    return pl.BlockSpec((halo, width), lambda i: (jnp.maximum(i * per - 1, 0), 0))


def _next_halo_spec(ts, halo, width, n_rows):
    per = ts // halo
    last = n_rows // halo - 1
    return pl.BlockSpec((halo, width), lambda i: (jnp.minimum((i + 1) * per, last), 0))


def cf_fwd_mid(name, a, w_dw, b_dw, gn, ts=256):
    s, d2 = a.shape
    d = d2 // 2
    ts = min(ts, s)
    hl = CONV_A_HALO
    off = hl - (CONV_A_TAPS - 1)

    rc = min(CONV_ROWS, ts)

    def body(a_ref, ah_ref, w_ref, b_ref, gn_ref, v0_ref, v1_ref, v3_ref, buf, sh):
        first = pl.program_id(0) == 0
        halo = ah_ref[...]
        hv0 = halo[:, :d] * jax.nn.sigmoid(halo[:, d:])
        buf[pl.ds(0, hl), :] = jnp.where(first, 0.0, hv0)
        main = a_ref[...]
        v0 = main[:, :d] * jax.nn.sigmoid(main[:, d:])
        buf[pl.ds(hl, ts), :] = v0
        v0_ref[...] = v0

        def conv(lanes):
            _shifted_copies(buf, sh, lanes)
            for r0 in range(0, ts, rc):
                acc = jnp.zeros((rc, lanes.size), F32)
                for k in range(CONV_A_TAPS):
                    acc = acc + w_ref[pl.ds(k, 1), lanes] * _window(buf, sh, lanes, r0 + off + k, rc)
                v1_ref[pl.ds(r0, rc), lanes] = acc + b_ref[:, lanes]

        _lane_chunks(d, conv)
        v1 = v1_ref[...]
        v2 = (v1 * _rstd(v1)) * gn_ref[...]
        v3_ref[...] = (v2 * jax.nn.sigmoid(v2)).astype(BF16)

    return _rows_call(
        name, body, s, ts,
        [(a, _blk(ts, d2)), (a, _prev_halo_spec(ts, hl, d2)), (w_dw, _full(w_dw.shape)),
         (b_dw, _full((1, d))), (gn, _full((1, d)))],
        [(jax.ShapeDtypeStruct((s, d), F32), _blk(ts, d)), (jax.ShapeDtypeStruct((s, d), F32), _blk(ts, d)),
         (jax.ShapeDtypeStruct((s, d), BF16), _blk(ts, d))],
        scratch=[pltpu.VMEM((hl + ts, d), F32), pltpu.VMEM((SUBLANES, hl + ts, min(CONV_LANES, d)), F32)],
        scratch_bytes=_nbytes((hl + ts, d + SUBLANES * CONV_LANES), F32))


def cf_bwd_rows(name, dv3, v1, gn, ts=256):
    s, d = v1.shape
    ts = min(ts, s)

    def body(dv3_ref, v1_ref, gn_ref, dv1_ref, dgn_ref, db_ref):
        v1 = v1_ref[...]
        gn_v = gn_ref[...]
        v2 = (v1 * _rstd(v1)) * gn_v
        sg = jax.nn.sigmoid(v2)
        dv2 = dv3_ref[...] * (sg * (1.0 + v2 * (1.0 - sg)))
        dv1, dgn = _rms_bwd_math(dv2, v1, gn_v)
        dv1_ref[...] = dv1
        _accumulate(dgn_ref, _rowsum8(dgn))
        _accumulate(db_ref, _rowsum8(dv1))

    return _rows_call(
        name, body, s, ts, [(dv3, _blk(ts, d)), (v1, _blk(ts, d)), (gn, _full((1, d)))],
        [(jax.ShapeDtypeStruct((s, d), F32), _blk(ts, d)),
         (jax.ShapeDtypeStruct((SUBLANES, d), F32), _full((SUBLANES, d))),
         (jax.ShapeDtypeStruct((SUBLANES, d), F32), _full((SUBLANES, d)))])


def cf_bwd_conv(name, dv1, v0, a, w_dw, ts=256):
    s, d = dv1.shape
    ts = min(ts, s)
    hl = CONV_A_HALO
    taps = CONV_A_TAPS
    off = hl - (taps - 1)
    last_blk = s // ts - 1

    rc = min(CONV_ROWS, ts)

    def body(dv1_ref, dv1n_ref, v0_ref, v0p_ref, a_ref, w_ref, da_ref, dw_ref, db_ref,
             dbuf, vbuf, dv0_buf, dw_acc, dsh, vsh):
        i = pl.program_id(0)
        dbuf[pl.ds(0, ts), :] = dv1_ref[...]
        dbuf[pl.ds(ts, hl), :] = jnp.where(i == last_blk, 0.0, dv1n_ref[...])
        vbuf[pl.ds(0, hl), :] = jnp.where(i == 0, 0.0, v0p_ref[...])
        vbuf[pl.ds(hl, ts), :] = v0_ref[...]

        @pl.when(i == 0)
        def _():
            dw_acc[...] = jnp.zeros_like(dw_acc)

        def conv_t(lanes):
            _shifted_copies(dbuf, dsh, lanes)
            _shifted_copies(vbuf, vsh, lanes)
            for r0 in range(0, ts, rc):
                g = dbuf[pl.ds(r0, rc), lanes]
                acc = jnp.zeros((rc, lanes.size), F32)
                for k in range(taps):
                    acc = acc + w_ref[pl.ds(k, 1), lanes] * _window(dbuf, dsh, lanes, r0 + taps - 1 - k, rc)
                    prod = g * _window(vbuf, vsh, lanes, r0 + off + k, rc)
                    dw_acc[pl.ds(k * SUBLANES, SUBLANES), lanes] += _rowsum8(prod)
                dv0_buf[pl.ds(r0, rc), lanes] = acc

        _lane_chunks(d, conv_t)
        dv0 = dv0_buf[...]
        av = a_ref[...]
        val, sg = av[:, :d], jax.nn.sigmoid(av[:, d:])
        dval = dv0 * sg
        dgate = dv0 * val * (sg * (1.0 - sg))
        da_ref[:, :d] = dval.astype(BF16)
        da_ref[:, d:] = dgate.astype(BF16)
        _accumulate(db_ref.at[:, pl.ds(0, d)], _rowsum8(dval))
        _accumulate(db_ref.at[:, pl.ds(d, d)], _rowsum8(dgate))

        @pl.when(i == last_blk)
        def _():
            dw_ref[...] = jnp.sum(dw_acc[...].reshape(hl, SUBLANES, d), axis=1)

    lc = min(CONV_LANES, d)
    scratch = [pltpu.VMEM((ts + hl, d), F32), pltpu.VMEM((hl + ts, d), F32), pltpu.VMEM((ts, d), F32),
               pltpu.VMEM((hl * SUBLANES, d), F32), pltpu.VMEM((SUBLANES, ts + hl, lc), F32),
               pltpu.VMEM((SUBLANES, hl + ts, lc), F32)]
    sbytes = _nbytes((3 * ts + 2 * hl + hl * SUBLANES, d), F32) + 2 * _nbytes((SUBLANES, ts + hl, lc), F32)
    return _rows_call(
        name, body, s, ts,
        [(dv1, _blk(ts, d)), (dv1, _next_halo_spec(ts, hl, d, s)), (v0, _blk(ts, d)), (v0, _prev_halo_spec(ts, hl, d)),
         (a, _blk(ts, 2 * d)), (w_dw, _full(w_dw.shape))],
        [(jax.ShapeDtypeStruct((s, 2 * d), BF16), _blk(ts, 2 * d)),
         (jax.ShapeDtypeStruct((hl, d), F32), _full((hl, d))),
         (jax.ShapeDtypeStruct((SUBLANES, 2 * d), F32), _full((SUBLANES, 2 * d)))],
        scratch=scratch, scratch_bytes=sbytes)


def sc_fwd_mid(name, bcv, w_conv, ts=256):
    s, d3 = bcv.shape
    d = d3 // 3
    ts = min(ts, s)
    hl = CONV_B_HALO
    off = hl - (CONV_B_TAPS - 1)

    def body(x_ref, xp_ref, w_ref, y_ref, buf):
        hp = xp_ref[...]
        buf[pl.ds(0, hl), :] = jnp.where(pl.program_id(0) == 0, 0.0, hp[:, d:2 * d] * hp[:, 2 * d:])
        buf[pl.ds(hl, ts), :] = x_ref[:, d:2 * d] * x_ref[:, 2 * d:]
        cc = jnp.zeros((ts, d), F32)
        for k in range(CONV_B_TAPS):
            cc = cc + w_ref[pl.ds(k, 1), :] * buf[pl.ds(off + k, ts), :]
        y_ref[...] = (x_ref[:, :d] * cc).astype(BF16)

    return _rows_call(
        name, body, s, ts,
        [(bcv, _blk(ts, d3)), (bcv, _prev_halo_spec(ts, hl, d3)), (w_conv, _full(w_conv.shape))],
        [(jax.ShapeDtypeStruct((s, d), BF16), _blk(ts, d))],
        scratch=[pltpu.VMEM((hl + ts, d), F32)], scratch_bytes=_nbytes((hl + ts, d), F32))[0]


def sc_bwd_mid(name, dy, bcv, w_conv, ts=256):
    s, d3 = bcv.shape
    d = d3 // 3
    ts = min(ts, s)
    hl = CONV_B_HALO
    taps = CONV_B_TAPS
    off = hl - (taps - 1)
    last_blk = s // ts - 1

    def body(dy_ref, dyn_ref, x_ref, xp_ref, xn_ref, w_ref, dx_ref, dw_ref, cvbuf, dbuf, dw_acc):
        i = pl.program_id(0)
        hp = xp_ref[...]
        cvbuf[pl.ds(0, hl), :] = jnp.where(i == 0, 0.0, hp[:, d:2 * d] * hp[:, 2 * d:])
        gb, gc, v = x_ref[:, :d], x_ref[:, d:2 * d], x_ref[:, 2 * d:]
        cvbuf[pl.ds(hl, ts), :] = gc * v
        dy_v = dy_ref[...]
        dcc = dy_v * gb
        dbuf[pl.ds(0, ts), :] = dcc
        dbuf[pl.ds(ts, hl), :] = jnp.where(i == last_blk, 0.0, dyn_ref[...] * xn_ref[:, :d])

        @pl.when(i == 0)
        def _():
            dw_acc[...] = jnp.zeros_like(dw_acc)

        cc = jnp.zeros((ts, d), F32)
        dcv = jnp.zeros((ts, d), F32)
        for k in range(taps):
            win = cvbuf[pl.ds(off + k, ts), :]
            cc = cc + w_ref[pl.ds(k, 1), :] * win
            dcv = dcv + w_ref[pl.ds(k, 1), :] * dbuf[pl.ds(taps - 1 - k, ts), :]
            dw_acc[pl.ds(k * SUBLANES, SUBLANES), :] += _rowsum8(dcc * win)
        dx_ref[:, :d] = (dy_v * cc).astype(BF16)
        dx_ref[:, d:2 * d] = (dcv * v).astype(BF16)
        dx_ref[:, 2 * d:] = (dcv * gc).astype(BF16)

        @pl.when(i == last_blk)
        def _():
            dw_ref[...] = jnp.sum(dw_acc[...].reshape(hl, SUBLANES, d), axis=1)

    scratch = [pltpu.VMEM((hl + ts, d), F32), pltpu.VMEM((ts + hl, d), F32), pltpu.VMEM((hl * SUBLANES, d), F32)]
    sbytes = _nbytes((2 * ts + 2 * hl + hl * SUBLANES, d), F32)
    return _rows_call(
        name, body, s, ts,
        [(dy, _blk(ts, d)), (dy, _next_halo_spec(ts, hl, d, s)), (bcv, _blk(ts, d3)), (bcv, _prev_halo_spec(ts, hl, d3)),
         (bcv, _next_halo_spec(ts, hl, d3, s)), (w_conv, _full(w_conv.shape))],
        [(jax.ShapeDtypeStruct((s, d3), BF16), _blk(ts, d3)), (jax.ShapeDtypeStruct((hl, d), F32), _full((hl, d)))],
        scratch=scratch, scratch_bytes=sbytes)


def _row(a, i):
    return lax.slice_in_dim(a, i, i + 1, axis=0)


def _local_step(x, p, target, small, get_w, conv_w, put_grads):
    depth = p.shape[0]
    acts = []
    h = x
    for i in range(depth):
        j = i // 2
        act = {"h": h}
        u = rms_fwd(f"rms_mix_{i}", h, _row(small["norm_mix"], i))
        act["u"] = u
        if i % 2 == 0:
            a = mm_x_wcol(f"cf_pw1_{i}", u, get_w("cf_w_pw1", j, u), 0, extras=[(_row(small["cf_b_pw1"], j), "row")],
                          epi=lambda acc, b: (acc + b,))[0]
            v0, v1, v3 = cf_fwd_mid(f"cf_mid_{i}", a, conv_w["cf"][j], _row(small["cf_b_dw"], j), _row(small["cf_norm"], j))
            act.update(a=a, v0=v0, v1=v1, v3=v3)
            h1 = mm_x_wrow(f"cf_pw2_{i}", v3, get_w("cf_w_pw2", j, v3),
                           extras=[(_row(small["cf_b_pw2"], j), "row"), (h, "tile")],
                           epi=lambda acc, b, res: (res + (acc + b),))[0]
        else:
            bcv = mm_x_wcol(f"sc_in_{i}", u, get_w("sc_w_in", j, u), 0, tn=768)[0]
            y = sc_fwd_mid(f"sc_mid_{i}", bcv, conv_w["sc"][j])
            act.update(bcv=bcv, y=y)
            h1 = mm_x_wrow(f"sc_out_{i}", y, get_w("sc_w_out", j, y), extras=[(h, "tile")],
                           epi=lambda acc, res: (res + acc,))[0]
        act["h1"] = h1
        u2 = rms_fwd(f"rms_mlp_{i}", h1, _row(small["norm_mlp"], i))
        z, hd = mm_x_wcol(f"mlp_w1_{i}", u2, get_w("mlp_w1", i, u2), 0, outs_dtypes=(F32, BF16),
                          epi=lambda acc: (acc, jnp.square(jnp.maximum(acc, 0.0))))
        h2 = mm_x_wrow(f"mlp_w2_{i}", hd, get_w("mlp_w2", i, hd), extras=[(h1, "tile")],
                       epi=lambda acc, res: (res + acc,), tn=1024)[0]
        act.update(u2=u2, z=z, hd=hd, h2=h2)
        n3 = rms_fwd(f"rms_ple_{i}", h2, _row(small["norm_ple"], i))
        e = mm_x_wcol(f"ple_proj_{i}", p[i], get_w("ple_w_proj", i, n3), 0)[0]

        def ple_epi(acc, e_t, res):
            g_t = jax.nn.sigmoid(acc)
            return g_t, res + g_t * e_t

        g, h3 = mm_x_wrow(f"ple_gate_{i}", n3, get_w("ple_w_gate", i, e), extras=[(e, "tile"), (h2, "tile")],
                          outs_dtypes=(F32, F32), epi=ple_epi)
        act.update(n3=n3, e=e, g=g)
        acts.append(act)
        h = h3

    loss_part, dh, dg_final = final_loss("final_loss", h, small["norm_final"], target)
    sg = {k: [None] * small[k].shape[0] for k in small if k != "norm_final"}
    sg["norm_final"] = [dg_final]
    sg["cf_w_dw"] = [None] * conv_w["cf"].shape[0]
    sg["sc_w_conv"] = [None] * conv_w["sc"].shape[0]

    for i in reversed(range(depth)):
        j = i // 2
        act = acts[i]
        de, dgl = ple_bwd_elem(f"ple_bwd_{i}", dh, act["g"], act["e"])
        g_proj = mm_xt_dy(f"d_ple_proj_{i}", p[i], de, True)
        g_gate = mm_xt_dy(f"d_ple_gate_{i}", act["n3"], dgl, False)
        dn3 = mm_dy_wrow_t(f"dn3_{i}", dgl, get_w("ple_w_gate", i))[0]
        dh2, dh2b, sg["norm_ple"][i], _ = rms_bwd(f"rms_ple_bwd_{i}", dn3, act["h2"], _row(small["norm_ple"], i), dh)
        g_w2 = mm_xt_dy(f"d_mlp_w2_{i}", act["hd"], dh2b, False)
        dz = mm_dy_wrow_t(f"dz_{i}", dh2b, get_w("mlp_w2", i), extras=[(act["z"], "tile")], outs_dtypes=(BF16,),
                          epi=lambda acc, z_t: (acc * (2.0 * jnp.maximum(z_t, 0.0)),))[0]
        g_w1 = mm_xt_dy(f"d_mlp_w1_{i}", act["u2"], dz, True)
        put_grads({("ple_w_proj", i): g_proj, ("ple_w_gate", i): g_gate, ("mlp_w2", i): g_w2, ("mlp_w1", i): g_w1})
        du2 = mm_dy_wcol_t(f"du2_{i}", dz, get_w("mlp_w1", i), 0)[0]
        dh1, dh1b, sg["norm_mlp"][i], cs1 = rms_bwd(f"rms_mlp_bwd_{i}", du2, act["h1"], _row(small["norm_mlp"], i), dh2)
        if i % 2 == 0:
            g_out = mm_xt_dy(f"d_cf_pw2_{i}", act["v3"], dh1b, False)
            sg["cf_b_pw2"][j] = cs1
            dv3 = mm_dy_wrow_t(f"dv3_{i}", dh1b, get_w("cf_w_pw2", j))[0]
            dv1, sg["cf_norm"][j], sg["cf_b_dw"][j] = cf_bwd_rows(f"cf_bwd_rows_{i}", dv3, act["v1"], _row(small["cf_norm"], j))
            da, sg["cf_w_dw"][j], sg["cf_b_pw1"][j] = cf_bwd_conv(f"cf_bwd_conv_{i}", dv1, act["v0"], act["a"], conv_w["cf"][j])
            g_in = mm_xt_dy(f"d_cf_pw1_{i}", act["u"], da, True)
            put_grads({("cf_w_pw2", j): g_out, ("cf_w_pw1", j): g_in})
            du = mm_dy_wcol_t(f"du_{i}", da, get_w("cf_w_pw1", j), 0)[0]
        else:
            g_out = mm_xt_dy(f"d_sc_out_{i}", act["y"], dh1b, False)
            dy = mm_dy_wrow_t(f"dy_{i}", dh1b, get_w("sc_w_out", j))[0]
            dbcv, sg["sc_w_conv"][j] = sc_bwd_mid(f"sc_bwd_mid_{i}", dy, act["bcv"], conv_w["sc"][j])
            g_in = mm_xt_dy(f"d_sc_in_{i}", act["u"], dbcv, True, tn=768)
            put_grads({("sc_w_out", j): g_out, ("sc_w_in", j): g_in})
            du = mm_dy_wcol_t(f"du_{i}", dbcv, get_w("sc_w_in", j), 0)[0]
        dh, _, sg["norm_mix"][i], _ = rms_bwd(f"rms_mix_bwd_{i}", du, act["h"], _row(small["norm_mix"], i), dh1)
    return loss_part, dh, sg


def _me_and_peers():
    x, y, c = lax.axis_index("x"), lax.axis_index("y"), lax.axis_index("c")
    me = 4 * x + 2 * y + c
    peers = []
    for q in range(1, NDEV):
        px = 1 - x if q & 4 else x
        py = 1 - y if q & 2 else y
        pc = 1 - c if q & 1 else c
        peers.append(((px, py, pc), 4 * px + 2 * py + pc))
    return me, peers


def _exchange(name, srcs, out_shapes, src_fns, dst_fns, after=()):
    n = len(srcs)
    n_after = len(after)

    def body(*refs):
        ins, outs = refs[:n], refs[n + n_after:2 * n + n_after]
        send_sems, recv_sems, local_sems = refs[2 * n + n_after:]
        me, peers = _me_and_peers()
        local, remote = [], []
        for k in range(n):
            cp = pltpu.make_async_copy(src_fns[k](ins[k], me), dst_fns[k](outs[k], me), local_sems.at[k])
            cp.start()
            local.append(cp)
        for q, (peer, peer_blk) in enumerate(peers):
            for k in range(n):
                cp = pltpu.make_async_remote_copy(
                    src_ref=src_fns[k](ins[k], peer_blk), dst_ref=dst_fns[k](outs[k], me),
                    send_sem=send_sems.at[k, q], recv_sem=recv_sems.at[k, q],
                    device_id=peer, device_id_type=MESH)
                cp.start()
                remote.append(cp)
        for q, (peer, peer_blk) in enumerate(peers):
            for k in range(n):
                pltpu.make_async_remote_copy(
                    src_ref=src_fns[k](ins[k], peer_blk), dst_ref=dst_fns[k](outs[k], peer_blk),
                    send_sem=send_sems.at[k, q], recv_sem=recv_sems.at[k, q],
                    device_id=peer, device_id_type=MESH).wait_recv()
        for cp in remote:
            cp.wait_send()
        for cp in local:
            cp.wait()

    any_spec = pl.BlockSpec(memory_space=pl.ANY)
    return pl.pallas_call(
        body,
        name=name,
        in_specs=[any_spec] * (n + n_after),
        out_specs=[any_spec] * n,
        out_shape=out_shapes,
        scratch_shapes=[pltpu.SemaphoreType.DMA((n, N_PEERS)), pltpu.SemaphoreType.DMA((n, N_PEERS)),
                        pltpu.SemaphoreType.DMA((n,))],
    )(*srcs, *after)


def sc_exchange(name, collective_id, srcs, out_shapes, src_fns, dst_fns):
    n = len(srcs)

    def body(*refs):
        ins, outs = refs[:n], refs[n:2 * n]
        send_sems, recv_sems, local_sems = refs[2 * n:]
        me, peers = _me_and_peers()
        barrier = pltpu.get_barrier_semaphore()
        for peer, _ in peers:
            pl.semaphore_signal(barrier, inc=1, device_id=peer, device_id_type=MESH)
        pl.semaphore_wait(barrier, N_PEERS)
        local, remote = [], []
        for k in range(n):
            cp = pltpu.make_async_copy(src_fns[k](ins[k], me), dst_fns[k](outs[k], me), local_sems.at[k])
            cp.start()
            local.append(cp)
        for q, (peer, peer_blk) in enumerate(peers):
            for k in range(n):
                cp = pltpu.make_async_remote_copy(
                    src_ref=src_fns[k](ins[k], peer_blk), dst_ref=dst_fns[k](outs[k], me),
                    send_sem=send_sems.at[k, q], recv_sem=recv_sems.at[k, q],
                    device_id=peer, device_id_type=MESH)
                cp.start()
                remote.append(cp)
        for q, (peer, peer_blk) in enumerate(peers):
            for k in range(n):
                pltpu.make_async_remote_copy(
                    src_ref=src_fns[k](ins[k], peer_blk), dst_ref=dst_fns[k](outs[k], peer_blk),
                    send_sem=send_sems.at[k, q], recv_sem=recv_sems.at[k, q],
                    device_id=peer, device_id_type=MESH).wait_recv()
        for cp in remote:
            cp.wait_send()
        for cp in local:
            cp.wait()

    return pl.kernel(
        body,
        out_type=out_shapes,
        mesh=plsc.ScalarSubcoreMesh(axis_name="sequencer", num_cores=1),
        name=name,
        scratch_types=[pltpu.SemaphoreType.DMA((n, N_PEERS)), pltpu.SemaphoreType.DMA((n, N_PEERS)),
                       pltpu.SemaphoreType.DMA((n,))],
        compiler_params=pltpu.CompilerParams(collective_id=collective_id),
    )(*srcs)


def sc_gather(name, collective_id, srcs, layers):
    n = len(srcs)
    outs_shape = [jax.ShapeDtypeStruct((NDEV, 1) + a.shape[1:], a.dtype) for a in srcs]

    def body(*refs):
        ins, outs = refs[:n], refs[n:2 * n]
        send_sems, recv_sems, local_sems = refs[2 * n:]
        x, y, c = lax.axis_index("x"), lax.axis_index("y"), lax.axis_index("c")
        me = 4 * x + 2 * y + c
        sibling = (x, y, 1 - c)
        chips = [(1 - x, y), (x, 1 - y), (1 - x, 1 - y)]
        barrier = pltpu.get_barrier_semaphore()
        for peer in [sibling] + [(cx, cy, c) for cx, cy in chips]:
            pl.semaphore_signal(barrier, inc=1, device_id=peer, device_id_type=MESH)
        pl.semaphore_wait(barrier, 1 + len(chips))

        def copy(k, slot, blk, to, src=None):
            place = outs[k].at[blk, 0]
            return pltpu.make_async_remote_copy(
                src_ref=place if src is None else src, dst_ref=place,
                send_sem=send_sems.at[k, slot], recv_sem=recv_sems.at[k, slot],
                device_id=to, device_id_type=MESH)

        local, sent = [], []
        for k in range(n):
            mine = ins[k].at[layers[k]]
            cp = pltpu.make_async_copy(mine, outs[k].at[me, 0], local_sems.at[k])
            cp.start()
            local.append(cp)
            sent.append(copy(k, 0, me, sibling, src=mine))
            sent += [copy(k, 1 + j, me, (cx, cy, c), src=mine) for j, (cx, cy) in enumerate(chips)]
        for cp in sent:
            cp.start()
        for k in range(n):
            for j, (cx, cy) in enumerate(chips):
                blk = 4 * cx + 2 * cy + c
                copy(k, 1 + j, blk, sibling).wait_recv()
                fwd = copy(k, 4 + j, blk, sibling)
                fwd.start()
                sent.append(fwd)
        for k in range(n):
            copy(k, 0, 4 * x + 2 * y + (1 - c), sibling).wait_recv()
            for j, (cx, cy) in enumerate(chips):
                copy(k, 4 + j, 4 * cx + 2 * cy + (1 - c), sibling).wait_recv()
        for cp in sent:
            cp.wait_send()
        for cp in local:
            cp.wait()

    return pl.kernel(
        body,
        out_type=outs_shape,
        mesh=plsc.ScalarSubcoreMesh(axis_name="sequencer", num_cores=1),
        name=name,
        scratch_types=[pltpu.SemaphoreType.DMA((n, N_PEERS)), pltpu.SemaphoreType.DMA((n, N_PEERS)),
                       pltpu.SemaphoreType.DMA((n,))],
        compiler_params=pltpu.CompilerParams(collective_id=collective_id),
    )(*srcs)


def _gather_src(layer):
    return lambda ref, blk: ref.at[layer]


def _gather_dst(ref, blk):
    return ref.at[blk, 0]


def _slice_of(ref, blk):
    return ref.at[blk]


def cast_bf16(name, w, tr_elems=512 * 1024):
    l, r, c = w.shape
    tr = _row_tile(r, tr_elems // c)
    spec = pl.BlockSpec((None, tr, c), lambda li, i: (li, i, 0))

    def body(w_ref, o_ref):
        o_ref[...] = w_ref[...].astype(BF16)

    return pl.pallas_call(
        body, name=name, grid=(l, r // tr), in_specs=[spec], out_specs=spec,
        out_shape=jax.ShapeDtypeStruct(w.shape, BF16),
        compiler_params=_params(("parallel", "parallel"), 6 * tr * c),
    )(w)


def _adamw_math(w, g, m, v):
    m = ADAM_B1 * m + (1.0 - ADAM_B1) * g
    v = ADAM_B2 * v + (1.0 - ADAM_B2) * (g * g)
    m_hat = m * (1.0 / (1.0 - ADAM_B1 ** ADAM_STEP))
    v_hat = v * (1.0 / (1.0 - ADAM_B2 ** ADAM_STEP))
    delta = -ADAM_LR * (m_hat / (jnp.sqrt(v_hat) + ADAM_EPS) + ADAM_WD * w)
    return delta, m, v


def _sum_blocks(ref):
    g = ref[0].astype(F32)
    for d in range(1, ref.shape[0]):
        g = g + ref[d].astype(F32)
    return g


def adamw_layer(name, recv, w, m, v, layer, stacked, after=None, tr_elems=256 * 1024):
    nd, r, c = recv.shape
    tr = _row_tile(r, tr_elems // c)
    r_spec = pl.BlockSpec((nd, tr, c), lambda i: (0, i, 0))
    w_spec = pl.BlockSpec((None, tr, c), lambda i: (layer, i, 0))
    if stacked is None:
        stacked = [lax.empty(w.shape, F32) for _ in range(4)]
    after = [] if after is None else [after]

    def body(r_ref, w_ref, m_ref, v_ref, g_in, d_in, m_in, v_in, *rest):
        g_out, d_out, m_out, v_out = rest[len(after):]
        g = _sum_blocks(r_ref)
        delta, m_new, v_new = _adamw_math(w_ref[...], g, m_ref[...], v_ref[...])
        g_out[...] = g
        d_out[...] = delta
        m_out[...] = m_new
        v_out[...] = v_new

    out = jax.ShapeDtypeStruct(w.shape, F32)
    return pl.pallas_call(
        body, name=name, grid=(r // tr,),
        in_specs=[r_spec, w_spec, w_spec, w_spec] + [pl.BlockSpec(memory_space=pl.ANY)] * (4 + len(after)),
        out_specs=[w_spec] * 4, out_shape=[out] * 4,
        input_output_aliases={4: 0, 5: 1, 6: 2, 7: 3},
        compiler_params=_params(("parallel",), tr * c * (2 * nd + 7 * 4)),
    )(recv, w, m, v, *stacked, *after)


def pack_small_grads(name, parts, taps, n_blocks):
    d = parts[0].shape[1]
    n_p, rows = len(parts), [t.shape[0] for t in taps]
    cb = d // n_blocks

    def body(*refs):
        part_refs, tap_refs = refs[:n_p], refs[n_p:n_p + len(taps)]
        sums_out, taps_out = refs[n_p + len(taps):]
        for i, r in enumerate(part_refs):
            sums_out[pl.ds(i, 1), :] = jnp.sum(r[...], axis=0, keepdims=True)
        r0 = 0
        for t_ref, n in zip(tap_refs, rows):
            for b in range(n_blocks):
                taps_out[b, pl.ds(r0, n), :] = t_ref[:, pl.ds(b * cb, cb)]
            r0 += n

    vm = pl.BlockSpec(memory_space=pltpu.VMEM)
    return pl.pallas_call(
        body, name=name, in_specs=[vm] * (n_p + len(taps)), out_specs=[vm] * 2,
        out_shape=[jax.ShapeDtypeStruct((n_p, d), F32), jax.ShapeDtypeStruct((n_blocks, sum(rows), cb), F32)],
    )(*parts, *taps)


def small_update(name, part_g, tap_g, w_a, m_a, v_a, w_b, m_b, v_b):
    nd, na, d = part_g.shape
    nb, cb = w_b.shape

    def body(pg_ref, tg_ref, wa_ref, ma_ref, va_ref, wb_ref, mb_ref, vb_ref,
             ga_out, da_out, ma_out, va_out, gb_out, db_out, mb_out, vb_out, loss_out):
        ga = _sum_blocks(pg_ref)
        delta, m_new, v_new = _adamw_math(wa_ref[...], ga, ma_ref[...], va_ref[...])
        ga_out[...] = ga
        da_out[...] = delta
        ma_out[...] = m_new
        va_out[...] = v_new
        loss_out[...] = jnp.broadcast_to(jnp.sum(ga[na - 1:na, :], axis=1, keepdims=True), loss_out.shape)
        gb = _sum_blocks(tg_ref)
        delta, m_new, v_new = _adamw_math(wb_ref[...], gb, mb_ref[...], vb_ref[...])
        gb_out[...] = gb
        db_out[...] = delta
        mb_out[...] = m_new
        vb_out[...] = v_new

    oa, ob = jax.ShapeDtypeStruct((na, d), F32), jax.ShapeDtypeStruct((nb, cb), F32)
    vm = pl.BlockSpec(memory_space=pltpu.VMEM)
    return pl.pallas_call(
        body, name=name, in_specs=[vm] * 8, out_specs=[vm] * 9,
        out_shape=[oa] * 4 + [ob] * 4 + [jax.ShapeDtypeStruct((1, LANES), F32)],
        compiler_params=pltpu.CompilerParams(vmem_limit_bytes=_vmem_limit(_nbytes(part_g.shape, F32))),
    )(part_g, tap_g, w_a, m_a, v_a, w_b, m_b, v_b)


BIG = ("cf_w_pw1", "cf_w_pw2", "sc_w_in", "sc_w_out", "mlp_w1", "mlp_w2", "ple_w_proj", "ple_w_gate")
COL_SHARDED = ("cf_w_pw1", "sc_w_in", "mlp_w1", "ple_w_proj")
WEIGHT_ORDER = ("norm_mix", "norm_mlp", "norm_ple", "cf_w_pw1", "cf_b_pw1", "cf_w_dw", "cf_b_dw", "cf_norm",
                "cf_w_pw2", "cf_b_pw2", "sc_w_in", "sc_w_conv", "sc_w_out", "mlp_w1", "mlp_w2", "ple_w_proj",
                "ple_w_gate", "norm_final")
SMALL_ROWS = (("norm_mix", 4), ("norm_mlp", 4), ("norm_ple", 4), ("cf_b_pw1", 4), ("cf_b_dw", 2), ("cf_norm", 2),
              ("cf_b_pw2", 2), ("norm_final", 1))


def _layer_weights(i):
    mixer = (("cf_w_pw1", i // 2), ("cf_w_pw2", i // 2)) if i % 2 == 0 else (("sc_w_in", i // 2), ("sc_w_out", i // 2))
    return mixer + (("mlp_w1", i), ("mlp_w2", i), ("ple_w_proj", i), ("ple_w_gate", i))


def _pad_rows(a, rows):
    return jnp.pad(a, ((0, 0), (0, rows - a.shape[1]), (0, 0)))


def _pack_taps(cf, sc):
    c = cf.shape[2]
    return jnp.concatenate([_pad_rows(cf, CONV_A_HALO).reshape(-1, c), _pad_rows(sc, CONV_B_HALO).reshape(-1, c)], axis=0)


def _unpack_taps(t, n_cf):
    c = t.shape[1]
    cf = t[:n_cf * CONV_A_HALO].reshape(n_cf, CONV_A_HALO, c)[:, :CONV_A_TAPS]
    sc = t[n_cf * CONV_A_HALO:].reshape(-1, CONV_B_HALO, c)[:, :CONV_B_TAPS]
    return cf, sc


def _pack_small(vals, d):
    return jnp.concatenate([vals[k].reshape(-1, d) for k, _ in SMALL_ROWS] + [jnp.zeros((1, d), F32)], axis=0)


def _unpack_small(a, shapes):
    out, r = {}, 0
    for k, n in SMALL_ROWS:
        out[k] = a[r:r + n].reshape(shapes[k])
        r += n
    return out


def kernel(x, p, norm_mix, norm_mlp, norm_ple, cf_w_pw1, cf_b_pw1, cf_w_dw, cf_b_dw, cf_norm, cf_w_pw2, cf_b_pw2, sc_w_in, sc_w_conv, sc_w_out, mlp_w1, mlp_w2, ple_w_proj, ple_w_gate, norm_final, loss_target, m_norm_mix, m_norm_mlp, m_norm_ple, m_cf_w_pw1, m_cf_b_pw1, m_cf_w_dw, m_cf_b_dw, m_cf_norm, m_cf_w_pw2, m_cf_b_pw2, m_sc_w_in, m_sc_w_conv, m_sc_w_out, m_mlp_w1, m_mlp_w2, m_ple_w_proj, m_ple_w_gate, m_norm_final, v_norm_mix, v_norm_mlp, v_norm_ple, v_cf_w_pw1, v_cf_b_pw1, v_cf_w_dw, v_cf_b_dw, v_cf_norm, v_cf_w_pw2, v_cf_b_pw2, v_sc_w_in, v_sc_w_conv, v_sc_w_out, v_mlp_w1, v_mlp_w2, v_ple_w_proj, v_ple_w_gate, v_norm_final):
    w = dict(norm_mix=norm_mix, norm_mlp=norm_mlp, norm_ple=norm_ple, cf_w_pw1=cf_w_pw1, cf_b_pw1=cf_b_pw1,
             cf_w_dw=cf_w_dw, cf_b_dw=cf_b_dw, cf_norm=cf_norm, cf_w_pw2=cf_w_pw2, cf_b_pw2=cf_b_pw2,
             sc_w_in=sc_w_in, sc_w_conv=sc_w_conv, sc_w_out=sc_w_out, mlp_w1=mlp_w1, mlp_w2=mlp_w2,
             ple_w_proj=ple_w_proj, ple_w_gate=ple_w_gate, norm_final=norm_final)
    m = dict(norm_mix=m_norm_mix, norm_mlp=m_norm_mlp, norm_ple=m_norm_ple, cf_w_pw1=m_cf_w_pw1, cf_b_pw1=m_cf_b_pw1,
             cf_w_dw=m_cf_w_dw, cf_b_dw=m_cf_b_dw, cf_norm=m_cf_norm, cf_w_pw2=m_cf_w_pw2, cf_b_pw2=m_cf_b_pw2,
             sc_w_in=m_sc_w_in, sc_w_conv=m_sc_w_conv, sc_w_out=m_sc_w_out, mlp_w1=m_mlp_w1, mlp_w2=m_mlp_w2,
             ple_w_proj=m_ple_w_proj, ple_w_gate=m_ple_w_gate, norm_final=m_norm_final)
    v = dict(norm_mix=v_norm_mix, norm_mlp=v_norm_mlp, norm_ple=v_norm_ple, cf_w_pw1=v_cf_w_pw1, cf_b_pw1=v_cf_b_pw1,
             cf_w_dw=v_cf_w_dw, cf_b_dw=v_cf_b_dw, cf_norm=v_cf_norm, cf_w_pw2=v_cf_w_pw2, cf_b_pw2=v_cf_b_pw2,
             sc_w_in=v_sc_w_in, sc_w_conv=v_sc_w_conv, sc_w_out=v_sc_w_out, mlp_w1=v_mlp_w1, mlp_w2=v_mlp_w2,
             ple_w_proj=v_ple_w_proj, ple_w_gate=v_ple_w_gate, norm_final=v_norm_final)
    depth, d = norm_mix.shape
    n_cf = cf_w_dw.shape[0]

    taps_w = _pack_taps(cf_w_dw, sc_w_conv)
    shards, gathered = {}, {}
    ids = iter(range(5 * depth))
    for i in range(depth):
        names = _layer_weights(i)
        groups = [names[:1], names[1:2], names[2:3], names[3:]] if i == 0 else [names[:2], names[2:]]
        for n_group, group in enumerate(groups):
            with_taps = i == 0 and n_group == 1
            for k, _ in group:
                if k not in shards:
                    shards[k] = cast_bf16(f"cast_{k}", w[k])
            srcs = [shards[k] for k, _ in group] + ([taps_w[None]] if with_taps else [])
            got = sc_gather(f"gather_{i}_{n_group}", next(ids), srcs, [l for _, l in group] + ([0] if with_taps else []))
            for (k, l), g in zip(group, got):
                gathered[(k, l)] = g if k in COL_SHARDED else g.reshape(-1, g.shape[3])
            if with_taps:
                taps_full = jnp.transpose(got[-1][:, 0], (1, 0, 2)).reshape(taps_w.shape[0], d)
    conv_w = {"cf": taps_full[:n_cf * CONV_A_HALO].reshape(n_cf, CONV_A_HALO, d),
              "sc": taps_full[n_cf * CONV_A_HALO:].reshape(-1, CONV_B_HALO, d)}

    def get_w(k, l, after=None):
        return gathered[(k, l)]

    received, waiting = {}, {}
    last_group = list(_layer_weights(0)[:2])

    def put_grads(grads):
        names = list(grads)
        if names[0][0] in ("cf_w_pw2", "sc_w_out") and set(names) != set(last_group):
            waiting.update(grads)
            return
        grads = {**waiting, **grads}
        waiting.clear()
        names = list(grads)
        got = sc_exchange(f"grad_exchange_{names[-1][0]}_{names[-1][1]}", next(ids), [grads[n] for n in names],
                          [jax.ShapeDtypeStruct(grads[n].shape, BF16) for n in names],
                          [_slice_of] * len(names), [_slice_of] * len(names))
        received.update(zip(names, got))

    small = {k: w[k] for k, _ in SMALL_ROWS}
    small["norm_final"] = norm_final[None]
    loss_part, grad_x, sg = _local_step(x[0], p[:, 0], loss_target[0], small, get_w, conv_w, put_grads)

    out = {k: None for k in BIG}
    previous = None
    for (k, l), recv in received.items():
        if (k, l) not in last_group:
            out[k] = adamw_layer(f"adamw_{k}_{l}", recv, w[k], m[k], v[k], l, out[k], after=previous)
            previous = out[k][1]
    updated_first = [out[k][0] for k in BIG if out[k] is not None and k not in [n for n, _ in last_group]]

    parts = []
    for k, n in SMALL_ROWS:
        for g in sg[k]:
            parts += [g[:, :d], g[:, d:]] if g.shape[1] == 2 * d else [g]
    parts.append(loss_part)
    sums, tap_slices = pack_small_grads("pack_small_grads", parts, sg["cf_w_dw"] + sg["sc_w_conv"], NDEV)
    part_all, tap_mine = _exchange(
        "small_exchange", [sums[None], tap_slices],
        [jax.ShapeDtypeStruct((NDEV, 1) + sums.shape, F32), jax.ShapeDtypeStruct(tap_slices.shape, F32)],
        [_gather_src(0), _slice_of], [_gather_dst, _slice_of], after=updated_first)
    for k, l in last_group:
        out[k] = adamw_layer(f"adamw_{k}_{l}", received[(k, l)], w[k], m[k], v[k], l, out[k], after=part_all)
    sm = small_update("small_update", part_all[:, 0], tap_mine,
                      _pack_small(w, d), _pack_small(m, d), _pack_small(v, d),
                      taps_w, _pack_taps(m["cf_w_dw"], m["sc_w_conv"]), _pack_taps(v["cf_w_dw"], v["sc_w_conv"]))
    shapes = {k: w[k].shape for k, _ in SMALL_ROWS}
    for t in range(4):
        un = _unpack_small(sm[t], shapes)
        cf_t, sc_t = _unpack_taps(sm[4 + t], n_cf)
        for k in un:
            out.setdefault(k, [None] * 4)[t] = un[k]
        out.setdefault("cf_w_dw", [None] * 4)[t] = cf_t
        out.setdefault("sc_w_conv", [None] * 4)[t] = sc_t
    loss = sm[8][0, 0]

    return (loss, grad_x[None], *[out[k][0] for k in WEIGHT_ORDER], *[out[k][1] for k in WEIGHT_ORDER],
            *[out[k][2] for k in WEIGHT_ORDER], *[out[k][3] for k in WEIGHT_ORDER])
```

```python
import jax
import jax.numpy as jnp
from jax import lax
from jax.experimental import pallas as pl
from jax.experimental.pallas import tpu as pltpu
from jax.experimental.pallas import tpu_sc as plsc

F32 = jnp.float32
BF16 = jnp.bfloat16
EPS = 1e-6
NDEV = 8
N_PEERS = NDEV - 1
MESH = pl.DeviceIdType.MESH

ADAM_LR = 0.001
ADAM_B1 = 0.9
ADAM_B2 = 0.999
ADAM_EPS = 1e-08
ADAM_WD = 0.01
ADAM_STEP = 10

V7X_VMEM_BYTES = 64 * 1024 * 1024
VMEM_LIMIT_MAX = 56 * 1024 * 1024
SUBLANES = 8
LANES = 128
CONV_A_TAPS = 31
CONV_A_HALO = 32
CONV_B_TAPS = 3
CONV_B_HALO = 8


def _nbytes(shape, dtype):
    n = 1
    for s in shape:
        if s is not None:
            n *= s
    return n * jnp.dtype(dtype).itemsize


def _vmem_limit(block_bytes, scratch_bytes=0):
    need = 2 * block_bytes + scratch_bytes
    return int(min(VMEM_LIMIT_MAX, max(32 * 1024 * 1024, need + need // 2 + (4 << 20))))


def _params(sem, block_bytes, scratch_bytes=0):
    return pltpu.CompilerParams(dimension_semantics=sem, vmem_limit_bytes=_vmem_limit(block_bytes, scratch_bytes))


_DIMS = {
    "nn": (((1,), (0,)), ((), ())),
    "nt": (((1,), (1,)), ((), ())),
    "tn": (((0,), (0,)), ((), ())),
}


def _mm(name, dims, grid, acc_shape, a, a_spec, b, b_spec, extras, outs, epi):
    ni, nj, nk = grid
    n_ex, n_out = len(extras), len(outs)
    dn = _DIMS[dims]

    def body(*refs):
        a_ref, b_ref = refs[0], refs[1]
        ex_refs = refs[2:2 + n_ex]
        out_refs = refs[2 + n_ex:2 + n_ex + n_out]
        d = lax.dot_general(a_ref[...].astype(BF16), b_ref[...].astype(BF16), dn, preferred_element_type=F32)

        def finish(acc):
            res = epi(acc, *[r[...] for r in ex_refs])
            for o_ref, r in zip(out_refs, res):
                o_ref[...] = r.astype(o_ref.dtype)

        if nk == 1:
            finish(d)
        else:
            acc_ref = refs[2 + n_ex + n_out]
            k = pl.program_id(2)

            @pl.when(k == 0)
            def _():
                acc_ref[...] = d

            @pl.when(jnp.logical_and(k > 0, k < nk - 1))
            def _():
                acc_ref[...] += d

            @pl.when(k == nk - 1)
            def _():
                finish(acc_ref[...] + d)

    blk = _nbytes(a_spec.block_shape, a.dtype) + _nbytes(b_spec.block_shape, b.dtype)
    for arr, spec in list(extras) + list(outs):
        blk += _nbytes(spec.block_shape, arr.dtype)
    acc_bytes = _nbytes(acc_shape, F32)
    scratch = [pltpu.VMEM(acc_shape, F32)] if nk > 1 else []
    return pl.pallas_call(
        body,
        name=name,
        grid=grid,
        in_specs=[a_spec, b_spec] + [s for _, s in extras],
        out_specs=[s for _, s in outs],
        out_shape=[o for o, _ in outs],
        scratch_shapes=scratch,
        compiler_params=_params(("parallel", "parallel", "arbitrary"), blk, 3 * acc_bytes),
    )(a, b, *[e for e, _ in extras])


def _tile(n, pref):
    if n <= pref:
        return n
    t = pref - pref % LANES
    while t > LANES and n % t:
        t -= LANES
    assert n % t == 0, (n, pref)
    return t


def _row_tile(n, pref):
    if n <= pref:
        return n
    t = max(SUBLANES, pref - pref % SUBLANES)
    while t > SUBLANES and n % t:
        t -= SUBLANES
    assert n % t == 0, (n, pref)
    return t


def _id_epi(acc):
    return (acc,)


def mm_x_wcol(name, x, w, layer, extras=(), outs_dtypes=(F32,), epi=_id_epi, tm=1024, tn=1024):
    m, kdim = x.shape
    c = w.shape[3]
    n = NDEV * c
    tm, tn = _tile(m, tm), _tile(c, tn)
    tk = _tile(kdim, 2048)
    grid = (m // tm, n // tn, kdim // tk)
    per = c // tn
    a_spec = pl.BlockSpec((tm, tk), lambda i, j, k: (i, k))
    b_spec = pl.BlockSpec((None, None, tk, tn), lambda i, j, k: (j // per, layer, k, j % per))
    ex = [(e, _ex_spec(e, kind, tm, tn)) for e, kind in extras]
    o_spec = pl.BlockSpec((tm, tn), lambda i, j, k: (i, j))
    outs = [(jax.ShapeDtypeStruct((m, n), dt), o_spec) for dt in outs_dtypes]
    return _mm(name, "nn", grid, (tm, tn), x, a_spec, w, b_spec, ex, outs, epi)


def mm_x_wrow(name, x, w, extras=(), outs_dtypes=(F32,), epi=_id_epi, tm=1024, tn=512):
    m, kdim = x.shape
    n = w.shape[1]
    assert kdim == w.shape[0]
    tm, tn = _tile(m, tm), _tile(n, tn)
    tk = _tile(kdim, 2048)
    grid = (m // tm, n // tn, kdim // tk)
    a_spec = pl.BlockSpec((tm, tk), lambda i, j, k: (i, k))
    b_spec = pl.BlockSpec((tk, tn), lambda i, j, k: (k, j))
    ex = [(e, _ex_spec(e, kind, tm, tn)) for e, kind in extras]
    o_spec = pl.BlockSpec((tm, tn), lambda i, j, k: (i, j))
    outs = [(jax.ShapeDtypeStruct((m, n), dt), o_spec) for dt in outs_dtypes]
    return _mm(name, "nn", grid, (tm, tn), x, a_spec, w, b_spec, ex, outs, epi)


def mm_dy_wcol_t(name, dy, w, layer, extras=(), outs_dtypes=(F32,), epi=_id_epi, tm=1024, tn=1024):
    m, n = dy.shape
    kdim, c = w.shape[2], w.shape[3]
    assert n == NDEV * c
    tm, tn = _tile(m, tm), _tile(kdim, tn)
    tk = _tile(c, 2048)
    per = c // tk
    grid = (m // tm, kdim // tn, n // tk)
    a_spec = pl.BlockSpec((tm, tk), lambda i, j, k: (i, k))
    b_spec = pl.BlockSpec((None, None, tn, tk), lambda i, j, k: (k // per, layer, j, k % per))
    ex = [(e, _ex_spec(e, kind, tm, tn)) for e, kind in extras]
    o_spec = pl.BlockSpec((tm, tn), lambda i, j, k: (i, j))
    outs = [(jax.ShapeDtypeStruct((m, kdim), dt), o_spec) for dt in outs_dtypes]
    return _mm(name, "nt", grid, (tm, tn), dy, a_spec, w, b_spec, ex, outs, epi)


def mm_dy_wrow_t(name, dy, w, extras=(), outs_dtypes=(F32,), epi=_id_epi, tm=1024, tn=1024):
    m, n = dy.shape
    kdim = w.shape[0]
    assert n == w.shape[1]
    tm, tn = _tile(m, tm), _tile(kdim, tn)
    tk = _tile(n, 2048)
    grid = (m // tm, kdim // tn, n // tk)
    a_spec = pl.BlockSpec((tm, tk), lambda i, j, k: (i, k))
    b_spec = pl.BlockSpec((tn, tk), lambda i, j, k: (j, k))
    ex = [(e, _ex_spec(e, kind, tm, tn)) for e, kind in extras]
    o_spec = pl.BlockSpec((tm, tn), lambda i, j, k: (i, j))
    outs = [(jax.ShapeDtypeStruct((m, kdim), dt), o_spec) for dt in outs_dtypes]
    return _mm(name, "nt", grid, (tm, tn), dy, a_spec, w, b_spec, ex, outs, epi)


def mm_xt_dy(name, x, dy, col_shards, tm=1024, tn=1024):
    m, kdim = x.shape
    n = dy.shape[1]
    tk = _tile(m, 2048)
    if col_shards:
        c = n // NDEV
        tm, tn = _tile(kdim, tm), _tile(c, tn)
        per = c // tn
        out = jax.ShapeDtypeStruct((NDEV, kdim, c), BF16)
        o_spec = pl.BlockSpec((None, tm, tn), lambda i, j, k: (j // per, i, j % per))
    else:
        tm, tn = _tile(kdim, tm), _tile(n, tn)
        out = jax.ShapeDtypeStruct((kdim, n), BF16)
        o_spec = pl.BlockSpec((tm, tn), lambda i, j, k: (i, j))
    grid = (kdim // tm, n // tn, m // tk)
    a_spec = pl.BlockSpec((tk, tm), lambda i, j, k: (k, i))
    b_spec = pl.BlockSpec((tk, tn), lambda i, j, k: (k, j))
    g = _mm(name, "tn", grid, (tm, tn), x, a_spec, dy, b_spec, [], [(out, o_spec)], _id_epi)[0]
    return g if col_shards else g.reshape(NDEV, kdim // NDEV, n)


def _ex_spec(e, kind, tm, tn):
    if kind == "tile":
        return pl.BlockSpec((tm, tn), lambda i, j, k: (i, j))
    if kind == "row":
        return pl.BlockSpec((1, tn), lambda i, j, k: (0, j))
    raise ValueError(kind)


def _rows_call(name, body, n_rows, ts, ins, outs, scratch=(), scratch_bytes=0):
    blk = sum(_nbytes(s.block_shape, a.dtype) for a, s in list(ins) + list(outs))
    return pl.pallas_call(
        body,
        name=name,
        grid=(n_rows // ts,),
        in_specs=[s for _, s in ins],
        out_specs=[s for _, s in outs],
        out_shape=[o for o, _ in outs],
        scratch_shapes=list(scratch),
        compiler_params=_params(("arbitrary",), blk, scratch_bytes + 4 * blk // 2),
    )(*[a for a, _ in ins])


def _blk(ts, d):
    return pl.BlockSpec((ts, d), lambda i: (i, 0))


def _full(shape):
    return pl.BlockSpec(shape, lambda i: tuple(0 for _ in shape))


def _rowsum8(v):
    t, d = v.shape
    return jnp.sum(v.reshape(t // SUBLANES, SUBLANES, d), axis=0)


def _accumulate(ref, val):
    @pl.when(pl.program_id(0) == 0)
    def _():
        ref[...] = val

    @pl.when(pl.program_id(0) > 0)
    def _():
        ref[...] += val


def _rstd(x):
    return lax.rsqrt(jnp.mean(x * x, axis=-1, keepdims=True) + EPS)


def _rms_bwd_math(dy, x, g):
    r = _rstd(x)
    gdy = dy * g
    c = jnp.mean(gdy * x, axis=-1, keepdims=True)
    dx = r * gdy - x * (r * r * r * c)
    return dx, dy * (x * r)


def rms_fwd(name, h, g, ts=512):
    s, d = h.shape
    ts = min(ts, s)

    def body(h_ref, g_ref, u_ref):
        x = h_ref[...]
        u_ref[...] = ((x * _rstd(x)) * g_ref[...]).astype(BF16)

    return _rows_call(name, body, s, ts, [(h, _blk(ts, d)), (g, _full((1, d)))],
                      [(jax.ShapeDtypeStruct((s, d), BF16), _blk(ts, d))])[0]


def rms_bwd(name, du, h, g, dres, ts=256):
    s, d = h.shape
    ts = min(ts, s)

    def body(du_ref, h_ref, g_ref, dres_ref, dh_ref, dhb_ref, dg_ref, cs_ref):
        dx, dg = _rms_bwd_math(du_ref[...], h_ref[...], g_ref[...])
        dh = dres_ref[...] + dx
        dh_ref[...] = dh
        dhb_ref[...] = dh.astype(BF16)
        _accumulate(dg_ref, _rowsum8(dg))
        _accumulate(cs_ref, _rowsum8(dh))

    return _rows_call(
        name, body, s, ts,
        [(du, _blk(ts, d)), (h, _blk(ts, d)), (g, _full((1, d))), (dres, _blk(ts, d))],
        [(jax.ShapeDtypeStruct((s, d), F32), _blk(ts, d)), (jax.ShapeDtypeStruct((s, d), BF16), _blk(ts, d)),
         (jax.ShapeDtypeStruct((SUBLANES, d), F32), _full((SUBLANES, d))),
         (jax.ShapeDtypeStruct((SUBLANES, d), F32), _full((SUBLANES, d)))])


def final_loss(name, h, g, target, ts=256):
    s, d = h.shape
    ts = min(ts, s)

    def body(h_ref, g_ref, t_ref, loss_ref, dh_ref, dg_ref):
        x = h_ref[...]
        gf = g_ref[...]
        y = (x * _rstd(x)) * gf
        err = y - t_ref[...]
        _accumulate(loss_ref, _rowsum8(err * err) * (0.5 / d))
        dx, dg = _rms_bwd_math(err * (1.0 / d), x, gf)
        dh_ref[...] = dx
        _accumulate(dg_ref, _rowsum8(dg))

    return _rows_call(
        name, body, s, ts,
        [(h, _blk(ts, d)), (g, _full((1, d))), (target, _blk(ts, d))],
        [(jax.ShapeDtypeStruct((SUBLANES, d), F32), _full((SUBLANES, d))),
         (jax.ShapeDtypeStruct((s, d), F32), _blk(ts, d)),
         (jax.ShapeDtypeStruct((SUBLANES, d), F32), _full((SUBLANES, d)))])


def ple_bwd_elem(name, dh, g, e, ts=512):
    s, d = dh.shape
    ts = min(ts, s)

    def body(dh_ref, g_ref, e_ref, de_ref, dgl_ref):
        dh_v, g_v = dh_ref[...], g_ref[...]
        de_ref[...] = (dh_v * g_v).astype(BF16)
        dgl_ref[...] = (dh_v * e_ref[...] * (g_v * (1.0 - g_v))).astype(BF16)

    return _rows_call(name, body, s, ts, [(dh, _blk(ts, d)), (g, _blk(ts, d)), (e, _blk(ts, d))],
                      [(jax.ShapeDtypeStruct((s, d), BF16), _blk(ts, d)),
                       (jax.ShapeDtypeStruct((s, d), BF16), _blk(ts, d))])


CONV_LANES = 256
CONV_ROWS = 64


def _lane_chunks(d, fn):
    lc = min(CONV_LANES, d)

    def lane_body(c, carry):
        fn(pl.ds(pl.multiple_of(c * lc, lc), lc))
        return carry

    lax.fori_loop(0, d // lc, lane_body, 0)


def _shifted_copies(buf, sh, lanes):
    rows = buf.shape[0] - SUBLANES
    for s in range(1, SUBLANES):
        sh[s, pl.ds(0, rows), :] = buf[pl.ds(s, rows), lanes]


def _window(buf, sh, lanes, start, rows):
    s = start % SUBLANES
    if s == 0:
        return buf[pl.ds(start, rows), lanes]
    return sh[s, pl.ds(start - s, rows), :]


def _prev_halo_spec(ts, halo, width):
    per = ts // halo
    return pl.BlockSpec((halo, width), lambda i: (jnp.maximum(i * per - 1, 0), 0))


def _next_halo_spec(ts, halo, width, n_rows):
    per = ts // halo
    last = n_rows // halo - 1
    return pl.BlockSpec((halo, width), lambda i: (jnp.minimum((i + 1) * per, last), 0))


def cf_fwd_mid(name, a, w_dw, b_dw, gn, ts=256):
    s, d2 = a.shape
    d = d2 // 2
    ts = min(ts, s)
    hl = CONV_A_HALO
    off = hl - (CONV_A_TAPS - 1)

    rc = min(CONV_ROWS, ts)

    def body(a_ref, ah_ref, w_ref, b_ref, gn_ref, v0_ref, v1_ref, v3_ref, buf, sh):
        first = pl.program_id(0) == 0
        halo = ah_ref[...]
        hv0 = halo[:, :d] * jax.nn.sigmoid(halo[:, d:])
        buf[pl.ds(0, hl), :] = jnp.where(first, 0.0, hv0)
        main = a_ref[...]
        v0 = main[:, :d] * jax.nn.sigmoid(main[:, d:])
        buf[pl.ds(hl, ts), :] = v0
        v0_ref[...] = v0

        def conv(lanes):
            _shifted_copies(buf, sh, lanes)
            for r0 in range(0, ts, rc):
                acc = jnp.zeros((rc, lanes.size), F32)
                for k in range(CONV_A_TAPS):
                    acc = acc + w_ref[pl.ds(k, 1), lanes] * _window(buf, sh, lanes, r0 + off + k, rc)
                v1_ref[pl.ds(r0, rc), lanes] = acc + b_ref[:, lanes]

        _lane_chunks(d, conv)
        v1 = v1_ref[...]
        v2 = (v1 * _rstd(v1)) * gn_ref[...]
        v3_ref[...] = (v2 * jax.nn.sigmoid(v2)).astype(BF16)

    return _rows_call(
        name, body, s, ts,
        [(a, _blk(ts, d2)), (a, _prev_halo_spec(ts, hl, d2)), (w_dw, _full(w_dw.shape)),
         (b_dw, _full((1, d))), (gn, _full((1, d)))],
        [(jax.ShapeDtypeStruct((s, d), F32), _blk(ts, d)), (jax.ShapeDtypeStruct((s, d), F32), _blk(ts, d)),
         (jax.ShapeDtypeStruct((s, d), BF16), _blk(ts, d))],
        scratch=[pltpu.VMEM((hl + ts, d), F32), pltpu.VMEM((SUBLANES, hl + ts, min(CONV_LANES, d)), F32)],
        scratch_bytes=_nbytes((hl + ts, d + SUBLANES * CONV_LANES), F32))


def cf_bwd_rows(name, dv3, v1, gn, ts=256):
    s, d = v1.shape
    ts = min(ts, s)

    def body(dv3_ref, v1_ref, gn_ref, dv1_ref, dgn_ref, db_ref):
        v1 = v1_ref[...]
        gn_v = gn_ref[...]
        v2 = (v1 * _rstd(v1)) * gn_v
        sg = jax.nn.sigmoid(v2)
        dv2 = dv3_ref[...] * (sg * (1.0 + v2 * (1.0 - sg)))
        dv1, dgn = _rms_bwd_math(dv2, v1, gn_v)
        dv1_ref[...] = dv1
        _accumulate(dgn_ref, _rowsum8(dgn))
        _accumulate(db_ref, _rowsum8(dv1))

    return _rows_call(
        name, body, s, ts, [(dv3, _blk(ts, d)), (v1, _blk(ts, d)), (gn, _full((1, d)))],
        [(jax.ShapeDtypeStruct((s, d), F32), _blk(ts, d)),
         (jax.ShapeDtypeStruct((SUBLANES, d), F32), _full((SUBLANES, d))),
         (jax.ShapeDtypeStruct((SUBLANES, d), F32), _full((SUBLANES, d)))])


def cf_bwd_conv(name, dv1, v0, a, w_dw, ts=256):
    s, d = dv1.shape
    ts = min(ts, s)
    hl = CONV_A_HALO
    taps = CONV_A_TAPS
    off = hl - (taps - 1)
    last_blk = s // ts - 1

    rc = min(CONV_ROWS, ts)

    def body(dv1_ref, dv1n_ref, v0_ref, v0p_ref, a_ref, w_ref, da_ref, dw_ref, db_ref,
             dbuf, vbuf, dv0_buf, dw_acc, dsh, vsh):
        i = pl.program_id(0)
        dbuf[pl.ds(0, ts), :] = dv1_ref[...]
        dbuf[pl.ds(ts, hl), :] = jnp.where(i == last_blk, 0.0, dv1n_ref[...])
        vbuf[pl.ds(0, hl), :] = jnp.where(i == 0, 0.0, v0p_ref[...])
        vbuf[pl.ds(hl, ts), :] = v0_ref[...]

        @pl.when(i == 0)
        def _():
            dw_acc[...] = jnp.zeros_like(dw_acc)

        def conv_t(lanes):
            _shifted_copies(dbuf, dsh, lanes)
            _shifted_copies(vbuf, vsh, lanes)
            for r0 in range(0, ts, rc):
                g = dbuf[pl.ds(r0, rc), lanes]
                acc = jnp.zeros((rc, lanes.size), F32)
                for k in range(taps):
                    acc = acc + w_ref[pl.ds(k, 1), lanes] * _window(dbuf, dsh, lanes, r0 + taps - 1 - k, rc)
                    prod = g * _window(vbuf, vsh, lanes, r0 + off + k, rc)
                    dw_acc[pl.ds(k * SUBLANES, SUBLANES), lanes] += _rowsum8(prod)
                dv0_buf[pl.ds(r0, rc), lanes] = acc

        _lane_chunks(d, conv_t)
        dv0 = dv0_buf[...]
        av = a_ref[...]
        val, sg = av[:, :d], jax.nn.sigmoid(av[:, d:])
        dval = dv0 * sg
        dgate = dv0 * val * (sg * (1.0 - sg))
        da_ref[:, :d] = dval.astype(BF16)
        da_ref[:, d:] = dgate.astype(BF16)
        _accumulate(db_ref.at[:, pl.ds(0, d)], _rowsum8(dval))
        _accumulate(db_ref.at[:, pl.ds(d, d)], _rowsum8(dgate))

        @pl.when(i == last_blk)
        def _():
            dw_ref[...] = jnp.sum(dw_acc[...].reshape(hl, SUBLANES, d), axis=1)

    lc = min(CONV_LANES, d)
    scratch = [pltpu.VMEM((ts + hl, d), F32), pltpu.VMEM((hl + ts, d), F32), pltpu.VMEM((ts, d), F32),
               pltpu.VMEM((hl * SUBLANES, d), F32), pltpu.VMEM((SUBLANES, ts + hl, lc), F32),
               pltpu.VMEM((SUBLANES, hl + ts, lc), F32)]
    sbytes = _nbytes((3 * ts + 2 * hl + hl * SUBLANES, d), F32) + 2 * _nbytes((SUBLANES, ts + hl, lc), F32)
    return _rows_call(
        name, body, s, ts,
        [(dv1, _blk(ts, d)), (dv1, _next_halo_spec(ts, hl, d, s)), (v0, _blk(ts, d)), (v0, _prev_halo_spec(ts, hl, d)),
         (a, _blk(ts, 2 * d)), (w_dw, _full(w_dw.shape))],
        [(jax.ShapeDtypeStruct((s, 2 * d), BF16), _blk(ts, 2 * d)),
         (jax.ShapeDtypeStruct((hl, d), F32), _full((hl, d))),
         (jax.ShapeDtypeStruct((SUBLANES, 2 * d), F32), _full((SUBLANES, 2 * d)))],
        scratch=scratch, scratch_bytes=sbytes)


def sc_fwd_mid(name, bcv, w_conv, ts=256):
    s, d3 = bcv.shape
    d = d3 // 3
    ts = min(ts, s)
    hl = CONV_B_HALO
    off = hl - (CONV_B_TAPS - 1)

    def body(x_ref, xp_ref, w_ref, y_ref, buf):
        hp = xp_ref[...]
        buf[pl.ds(0, hl), :] = jnp.where(pl.program_id(0) == 0, 0.0, hp[:, d:2 * d] * hp[:, 2 * d:])
        buf[pl.ds(hl, ts), :] = x_ref[:, d:2 * d] * x_ref[:, 2 * d:]
        cc = jnp.zeros((ts, d), F32)
        for k in range(CONV_B_TAPS):
            cc = cc + w_ref[pl.ds(k, 1), :] * buf[pl.ds(off + k, ts), :]
        y_ref[...] = (x_ref[:, :d] * cc).astype(BF16)

    return _rows_call(
        name, body, s, ts,
        [(bcv, _blk(ts, d3)), (bcv, _prev_halo_spec(ts, hl, d3)), (w_conv, _full(w_conv.shape))],
        [(jax.ShapeDtypeStruct((s, d), BF16), _blk(ts, d))],
        scratch=[pltpu.VMEM((hl + ts, d), F32)], scratch_bytes=_nbytes((hl + ts, d), F32))[0]


def sc_bwd_mid(name, dy, bcv, w_conv, ts=256):
    s, d3 = bcv.shape
    d = d3 // 3
    ts = min(ts, s)
    hl = CONV_B_HALO
    taps = CONV_B_TAPS
    off = hl - (taps - 1)
    last_blk = s // ts - 1

    def body(dy_ref, dyn_ref, x_ref, xp_ref, xn_ref, w_ref, dx_ref, dw_ref, cvbuf, dbuf, dw_acc):
        i = pl.program_id(0)
        hp = xp_ref[...]
        cvbuf[pl.ds(0, hl), :] = jnp.where(i == 0, 0.0, hp[:, d:2 * d] * hp[:, 2 * d:])
        gb, gc, v = x_ref[:, :d], x_ref[:, d:2 * d], x_ref[:, 2 * d:]
        cvbuf[pl.ds(hl, ts), :] = gc * v
        dy_v = dy_ref[...]
        dcc = dy_v * gb
        dbuf[pl.ds(0, ts), :] = dcc
        dbuf[pl.ds(ts, hl), :] = jnp.where(i == last_blk, 0.0, dyn_ref[...] * xn_ref[:, :d])

        @pl.when(i == 0)
        def _():
            dw_acc[...] = jnp.zeros_like(dw_acc)

        cc = jnp.zeros((ts, d), F32)
        dcv = jnp.zeros((ts, d), F32)
        for k in range(taps):
            win = cvbuf[pl.ds(off + k, ts), :]
            cc = cc + w_ref[pl.ds(k, 1), :] * win
            dcv = dcv + w_ref[pl.ds(k, 1), :] * dbuf[pl.ds(taps - 1 - k, ts), :]
            dw_acc[pl.ds(k * SUBLANES, SUBLANES), :] += _rowsum8(dcc * win)
        dx_ref[:, :d] = (dy_v * cc).astype(BF16)
        dx_ref[:, d:2 * d] = (dcv * v).astype(BF16)
        dx_ref[:, 2 * d:] = (dcv * gc).astype(BF16)

        @pl.when(i == last_blk)
        def _():
            dw_ref[...] = jnp.sum(dw_acc[...].reshape(hl, SUBLANES, d), axis=1)

    scratch = [pltpu.VMEM((hl + ts, d), F32), pltpu.VMEM((ts + hl, d), F32), pltpu.VMEM((hl * SUBLANES, d), F32)]
    sbytes = _nbytes((2 * ts + 2 * hl + hl * SUBLANES, d), F32)
    return _rows_call(
        name, body, s, ts,
        [(dy, _blk(ts, d)), (dy, _next_halo_spec(ts, hl, d, s)), (bcv, _blk(ts, d3)), (bcv, _prev_halo_spec(ts, hl, d3)),
         (bcv, _next_halo_spec(ts, hl, d3, s)), (w_conv, _full(w_conv.shape))],
        [(jax.ShapeDtypeStruct((s, d3), BF16), _blk(ts, d3)), (jax.ShapeDtypeStruct((hl, d), F32), _full((hl, d)))],
        scratch=scratch, scratch_bytes=sbytes)


def _row(a, i):
    return lax.slice_in_dim(a, i, i + 1, axis=0)


def _local_step(x, p, target, small, get_w, conv_w, put_grads):
    depth = p.shape[0]
    acts = []
    h = x
    for i in range(depth):
        j = i // 2
        act = {"h": h}
        u = rms_fwd(f"rms_mix_{i}", h, _row(small["norm_mix"], i))
        act["u"] = u
        if i % 2 == 0:
            a = mm_x_wcol(f"cf_pw1_{i}", u, get_w("cf_w_pw1", j, u), 0, extras=[(_row(small["cf_b_pw1"], j), "row")],
                          epi=lambda acc, b: (acc + b,))[0]
            v0, v1, v3 = cf_fwd_mid(f"cf_mid_{i}", a, conv_w["cf"][j], _row(small["cf_b_dw"], j), _row(small["cf_norm"], j))
            act.update(a=a, v0=v0, v1=v1, v3=v3)
            h1 = mm_x_wrow(f"cf_pw2_{i}", v3, get_w("cf_w_pw2", j, v3),
                           extras=[(_row(small["cf_b_pw2"], j), "row"), (h, "tile")],
                           epi=lambda acc, b, res: (res + (acc + b),), tn=1024)[0]
        else:
            bcv = mm_x_wcol(f"sc_in_{i}", u, get_w("sc_w_in", j, u), 0, tn=768)[0]
            y = sc_fwd_mid(f"sc_mid_{i}", bcv, conv_w["sc"][j])
            act.update(bcv=bcv, y=y)
            h1 = mm_x_wrow(f"sc_out_{i}", y, get_w("sc_w_out", j, y), extras=[(h, "tile")],
                           epi=lambda acc, res: (res + acc,), tn=1024)[0]
        act["h1"] = h1
        u2 = rms_fwd(f"rms_mlp_{i}", h1, _row(small["norm_mlp"], i))
        z, hd = mm_x_wcol(f"mlp_w1_{i}", u2, get_w("mlp_w1", i, u2), 0, outs_dtypes=(F32, BF16),
                          epi=lambda acc: (acc, jnp.square(jnp.maximum(acc, 0.0))))
        h2 = mm_x_wrow(f"mlp_w2_{i}", hd, get_w("mlp_w2", i, hd), extras=[(h1, "tile")],
                       epi=lambda acc, res: (res + acc,), tn=1024)[0]
        act.update(u2=u2, z=z, hd=hd, h2=h2)
        n3 = rms_fwd(f"rms_ple_{i}", h2, _row(small["norm_ple"], i))
        e = mm_x_wcol(f"ple_proj_{i}", p[i], get_w("ple_w_proj", i, n3), 0)[0]

        def ple_epi(acc, e_t, res):
            g_t = jax.nn.sigmoid(acc)
            return g_t, res + g_t * e_t

        g, h3 = mm_x_wrow(f"ple_gate_{i}", n3, get_w("ple_w_gate", i, e), extras=[(e, "tile"), (h2, "tile")],
                          outs_dtypes=(F32, F32), epi=ple_epi)
        act.update(n3=n3, e=e, g=g)
        acts.append(act)
        h = h3

    loss_part, dh, dg_final = final_loss("final_loss", h, small["norm_final"], target)
    sg = {k: [None] * small[k].shape[0] for k in small if k != "norm_final"}
    sg["norm_final"] = [dg_final]
    sg["cf_w_dw"] = [None] * conv_w["cf"].shape[0]
    sg["sc_w_conv"] = [None] * conv_w["sc"].shape[0]

    for i in reversed(range(depth)):
        j = i // 2
        act = acts[i]
        de, dgl = ple_bwd_elem(f"ple_bwd_{i}", dh, act["g"], act["e"])
        g_proj = mm_xt_dy(f"d_ple_proj_{i}", p[i], de, True)
        g_gate = mm_xt_dy(f"d_ple_gate_{i}", act["n3"], dgl, False)
        dn3 = mm_dy_wrow_t(f"dn3_{i}", dgl, get_w("ple_w_gate", i))[0]
        dh2, dh2b, sg["norm_ple"][i], _ = rms_bwd(f"rms_ple_bwd_{i}", dn3, act["h2"], _row(small["norm_ple"], i), dh)
        g_w2 = mm_xt_dy(f"d_mlp_w2_{i}", act["hd"], dh2b, False)
        dz = mm_dy_wrow_t(f"dz_{i}", dh2b, get_w("mlp_w2", i), extras=[(act["z"], "tile")], outs_dtypes=(BF16,),
                          epi=lambda acc, z_t: (acc * (2.0 * jnp.maximum(z_t, 0.0)),))[0]
        g_w1 = mm_xt_dy(f"d_mlp_w1_{i}", act["u2"], dz, True)
        put_grads({("ple_w_proj", i): g_proj, ("ple_w_gate", i): g_gate, ("mlp_w2", i): g_w2, ("mlp_w1", i): g_w1})
        du2 = mm_dy_wcol_t(f"du2_{i}", dz, get_w("mlp_w1", i), 0)[0]
        dh1, dh1b, sg["norm_mlp"][i], cs1 = rms_bwd(f"rms_mlp_bwd_{i}", du2, act["h1"], _row(small["norm_mlp"], i), dh2)
        if i % 2 == 0:
            g_out = mm_xt_dy(f"d_cf_pw2_{i}", act["v3"], dh1b, False)
            sg["cf_b_pw2"][j] = cs1
            dv3 = mm_dy_wrow_t(f"dv3_{i}", dh1b, get_w("cf_w_pw2", j))[0]
            dv1, sg["cf_norm"][j], sg["cf_b_dw"][j] = cf_bwd_rows(f"cf_bwd_rows_{i}", dv3, act["v1"], _row(small["cf_norm"], j))
            da, sg["cf_w_dw"][j], sg["cf_b_pw1"][j] = cf_bwd_conv(f"cf_bwd_conv_{i}", dv1, act["v0"], act["a"], conv_w["cf"][j])
            g_in = mm_xt_dy(f"d_cf_pw1_{i}", act["u"], da, True)
            put_grads({("cf_w_pw2", j): g_out, ("cf_w_pw1", j): g_in})
            du = mm_dy_wcol_t(f"du_{i}", da, get_w("cf_w_pw1", j), 0)[0]
        else:
            g_out = mm_xt_dy(f"d_sc_out_{i}", act["y"], dh1b, False)
            dy = mm_dy_wrow_t(f"dy_{i}", dh1b, get_w("sc_w_out", j))[0]
            dbcv, sg["sc_w_conv"][j] = sc_bwd_mid(f"sc_bwd_mid_{i}", dy, act["bcv"], conv_w["sc"][j])
            g_in = mm_xt_dy(f"d_sc_in_{i}", act["u"], dbcv, True, tn=768)
            put_grads({("sc_w_out", j): g_out, ("sc_w_in", j): g_in})
            du = mm_dy_wcol_t(f"du_{i}", dbcv, get_w("sc_w_in", j), 0)[0]
        dh, _, sg["norm_mix"][i], _ = rms_bwd(f"rms_mix_bwd_{i}", du, act["h"], _row(small["norm_mix"], i), dh1)
    return loss_part, dh, sg


def _me_and_peers():
    x, y, c = lax.axis_index("x"), lax.axis_index("y"), lax.axis_index("c")
    me = 4 * x + 2 * y + c
    peers = []
    for q in range(1, NDEV):
        px = 1 - x if q & 4 else x
        py = 1 - y if q & 2 else y
        pc = 1 - c if q & 1 else c
        peers.append(((px, py, pc), 4 * px + 2 * py + pc))
    return me, peers


def _exchange(name, srcs, out_shapes, src_fns, dst_fns, after=()):
    n = len(srcs)
    n_after = len(after)

    def body(*refs):
        ins, outs = refs[:n], refs[n + n_after:2 * n + n_after]
        send_sems, recv_sems, local_sems = refs[2 * n + n_after:]
        me, peers = _me_and_peers()
        local, remote = [], []
        for k in range(n):
            cp = pltpu.make_async_copy(src_fns[k](ins[k], me), dst_fns[k](outs[k], me), local_sems.at[k])
            cp.start()
            local.append(cp)
        for q, (peer, peer_blk) in enumerate(peers):
            for k in range(n):
                cp = pltpu.make_async_remote_copy(
                    src_ref=src_fns[k](ins[k], peer_blk), dst_ref=dst_fns[k](outs[k], me),
                    send_sem=send_sems.at[k, q], recv_sem=recv_sems.at[k, q],
                    device_id=peer, device_id_type=MESH)
                cp.start()
                remote.append(cp)
        for q, (peer, peer_blk) in enumerate(peers):
            for k in range(n):
                pltpu.make_async_remote_copy(
                    src_ref=src_fns[k](ins[k], peer_blk), dst_ref=dst_fns[k](outs[k], peer_blk),
                    send_sem=send_sems.at[k, q], recv_sem=recv_sems.at[k, q],
                    device_id=peer, device_id_type=MESH).wait_recv()
        for cp in remote:
            cp.wait_send()
        for cp in local:
            cp.wait()

    any_spec = pl.BlockSpec(memory_space=pl.ANY)
    return pl.pallas_call(
        body,
        name=name,
        in_specs=[any_spec] * (n + n_after),
        out_specs=[any_spec] * n,
        out_shape=out_shapes,
        scratch_shapes=[pltpu.SemaphoreType.DMA((n, N_PEERS)), pltpu.SemaphoreType.DMA((n, N_PEERS)),
                        pltpu.SemaphoreType.DMA((n,))],
    )(*srcs, *after)


def sc_exchange(name, collective_id, srcs, out_shapes, src_fns, dst_fns):
    n = len(srcs)

    def body(*refs):
        ins, outs = refs[:n], refs[n:2 * n]
        send_sems, recv_sems, local_sems = refs[2 * n:]
        me, peers = _me_and_peers()
        barrier = pltpu.get_barrier_semaphore()
        for peer, _ in peers:
            pl.semaphore_signal(barrier, inc=1, device_id=peer, device_id_type=MESH)
        pl.semaphore_wait(barrier, N_PEERS)
        local, remote = [], []
        for k in range(n):
            cp = pltpu.make_async_copy(src_fns[k](ins[k], me), dst_fns[k](outs[k], me), local_sems.at[k])
            cp.start()
            local.append(cp)
        for q, (peer, peer_blk) in enumerate(peers):
            for k in range(n):
                cp = pltpu.make_async_remote_copy(
                    src_ref=src_fns[k](ins[k], peer_blk), dst_ref=dst_fns[k](outs[k], me),
                    send_sem=send_sems.at[k, q], recv_sem=recv_sems.at[k, q],
                    device_id=peer, device_id_type=MESH)
                cp.start()
                remote.append(cp)
        for q, (peer, peer_blk) in enumerate(peers):
            for k in range(n):
                pltpu.make_async_remote_copy(
                    src_ref=src_fns[k](ins[k], peer_blk), dst_ref=dst_fns[k](outs[k], peer_blk),
                    send_sem=send_sems.at[k, q], recv_sem=recv_sems.at[k, q],
                    device_id=peer, device_id_type=MESH).wait_recv()
        for cp in remote:
            cp.wait_send()
        for cp in local:
            cp.wait()

    return pl.kernel(
        body,
        out_type=out_shapes,
        mesh=plsc.ScalarSubcoreMesh(axis_name="sequencer", num_cores=1),
        name=name,
        scratch_types=[pltpu.SemaphoreType.DMA((n, N_PEERS)), pltpu.SemaphoreType.DMA((n, N_PEERS)),
                       pltpu.SemaphoreType.DMA((n,))],
        compiler_params=pltpu.CompilerParams(collective_id=collective_id),
    )(*srcs)


def sc_gather(name, collective_id, srcs, layers):
    n = len(srcs)
    outs_shape = [jax.ShapeDtypeStruct((NDEV, 1) + a.shape[1:], a.dtype) for a in srcs]

    def body(*refs):
        ins, outs = refs[:n], refs[n:2 * n]
        send_sems, recv_sems, local_sems = refs[2 * n:]
        x, y, c = lax.axis_index("x"), lax.axis_index("y"), lax.axis_index("c")
        me = 4 * x + 2 * y + c
        sibling = (x, y, 1 - c)
        chips = [(1 - x, y), (x, 1 - y), (1 - x, 1 - y)]
        barrier = pltpu.get_barrier_semaphore()
        for peer in [sibling] + [(cx, cy, c) for cx, cy in chips]:
            pl.semaphore_signal(barrier, inc=1, device_id=peer, device_id_type=MESH)
        pl.semaphore_wait(barrier, 1 + len(chips))

        def copy(k, slot, blk, to, src=None):
            place = outs[k].at[blk, 0]
            return pltpu.make_async_remote_copy(
                src_ref=place if src is None else src, dst_ref=place,
                send_sem=send_sems.at[k, slot], recv_sem=recv_sems.at[k, slot],
                device_id=to, device_id_type=MESH)

        local, sent = [], []
        for k in range(n):
            mine = ins[k].at[layers[k]]
            cp = pltpu.make_async_copy(mine, outs[k].at[me, 0], local_sems.at[k])
            cp.start()
            local.append(cp)
            sent.append(copy(k, 0, me, sibling, src=mine))
            sent += [copy(k, 1 + j, me, (cx, cy, c), src=mine) for j, (cx, cy) in enumerate(chips)]
        for cp in sent:
            cp.start()
        for k in range(n):
            for j, (cx, cy) in enumerate(chips):
                blk = 4 * cx + 2 * cy + c
                copy(k, 1 + j, blk, sibling).wait_recv()
                fwd = copy(k, 4 + j, blk, sibling)
                fwd.start()
                sent.append(fwd)
        for k in range(n):
            copy(k, 0, 4 * x + 2 * y + (1 - c), sibling).wait_recv()
            for j, (cx, cy) in enumerate(chips):
                copy(k, 4 + j, 4 * cx + 2 * cy + (1 - c), sibling).wait_recv()
        for cp in sent:
            cp.wait_send()
        for cp in local:
            cp.wait()

    return pl.kernel(
        body,
        out_type=outs_shape,
        mesh=plsc.ScalarSubcoreMesh(axis_name="sequencer", num_cores=1),
        name=name,
        scratch_types=[pltpu.SemaphoreType.DMA((n, N_PEERS)), pltpu.SemaphoreType.DMA((n, N_PEERS)),
                       pltpu.SemaphoreType.DMA((n,))],
        compiler_params=pltpu.CompilerParams(collective_id=collective_id),
    )(*srcs)


def _gather_src(layer):
    return lambda ref, blk: ref.at[layer]


def _gather_dst(ref, blk):
    return ref.at[blk, 0]


def _slice_of(ref, blk):
    return ref.at[blk]


def cast_bf16(name, w, tr_elems=512 * 1024):
    l, r, c = w.shape
    tr = _row_tile(r, tr_elems // c)
    spec = pl.BlockSpec((None, tr, c), lambda li, i: (li, i, 0))

    def body(w_ref, o_ref):
        o_ref[...] = w_ref[...].astype(BF16)

    return pl.pallas_call(
        body, name=name, grid=(l, r // tr), in_specs=[spec], out_specs=spec,
        out_shape=jax.ShapeDtypeStruct(w.shape, BF16),
        compiler_params=_params(("parallel", "parallel"), 6 * tr * c),
    )(w)


def _adamw_math(w, g, m, v):
    m = ADAM_B1 * m + (1.0 - ADAM_B1) * g
    v = ADAM_B2 * v + (1.0 - ADAM_B2) * (g * g)
    m_hat = m * (1.0 / (1.0 - ADAM_B1 ** ADAM_STEP))
    v_hat = v * (1.0 / (1.0 - ADAM_B2 ** ADAM_STEP))
    delta = -ADAM_LR * (m_hat / (jnp.sqrt(v_hat) + ADAM_EPS) + ADAM_WD * w)
    return delta, m, v


def _sum_blocks(ref):
    g = ref[0].astype(F32)
    for d in range(1, ref.shape[0]):
        g = g + ref[d].astype(F32)
    return g


def adamw_layer(name, recv, w, m, v, layer, stacked, after=None, tr_elems=256 * 1024):
    nd, r, c = recv.shape
    tr = _row_tile(r, tr_elems // c)
    r_spec = pl.BlockSpec((nd, tr, c), lambda i: (0, i, 0))
    w_spec = pl.BlockSpec((None, tr, c), lambda i: (layer, i, 0))
    if stacked is None:
        stacked = [lax.empty(w.shape, F32) for _ in range(4)]
    after = [] if after is None else [after]

    def body(r_ref, w_ref, m_ref, v_ref, g_in, d_in, m_in, v_in, *rest):
        g_out, d_out, m_out, v_out = rest[len(after):]
        g = _sum_blocks(r_ref)
        delta, m_new, v_new = _adamw_math(w_ref[...], g, m_ref[...], v_ref[...])
        g_out[...] = g
        d_out[...] = delta
        m_out[...] = m_new
        v_out[...] = v_new

    out = jax.ShapeDtypeStruct(w.shape, F32)
    return pl.pallas_call(
        body, name=name, grid=(r // tr,),
        in_specs=[r_spec, w_spec, w_spec, w_spec] + [pl.BlockSpec(memory_space=pl.ANY)] * (4 + len(after)),
        out_specs=[w_spec] * 4, out_shape=[out] * 4,
        input_output_aliases={4: 0, 5: 1, 6: 2, 7: 3},
        compiler_params=_params(("parallel",), tr * c * (2 * nd + 7 * 4)),
    )(recv, w, m, v, *stacked, *after)


def pack_small_grads(name, parts, taps, n_blocks):
    d = parts[0].shape[1]
    n_p, rows = len(parts), [t.shape[0] for t in taps]
    cb = d // n_blocks

    def body(*refs):
        part_refs, tap_refs = refs[:n_p], refs[n_p:n_p + len(taps)]
        sums_out, taps_out = refs[n_p + len(taps):]
        for i, r in enumerate(part_refs):
            sums_out[pl.ds(i, 1), :] = jnp.sum(r[...], axis=0, keepdims=True)
        r0 = 0
        for t_ref, n in zip(tap_refs, rows):
            for b in range(n_blocks):
                taps_out[b, pl.ds(r0, n), :] = t_ref[:, pl.ds(b * cb, cb)]
            r0 += n

    vm = pl.BlockSpec(memory_space=pltpu.VMEM)
    return pl.pallas_call(
        body, name=name, in_specs=[vm] * (n_p + len(taps)), out_specs=[vm] * 2,
        out_shape=[jax.ShapeDtypeStruct((n_p, d), F32), jax.ShapeDtypeStruct((n_blocks, sum(rows), cb), F32)],
    )(*parts, *taps)


def small_update(name, part_g, tap_g, w_a, m_a, v_a, w_b, m_b, v_b):
    nd, na, d = part_g.shape
    nb, cb = w_b.shape

    def body(pg_ref, tg_ref, wa_ref, ma_ref, va_ref, wb_ref, mb_ref, vb_ref,
             ga_out, da_out, ma_out, va_out, gb_out, db_out, mb_out, vb_out, loss_out):
        ga = _sum_blocks(pg_ref)
        delta, m_new, v_new = _adamw_math(wa_ref[...], ga, ma_ref[...], va_ref[...])
        ga_out[...] = ga
        da_out[...] = delta
        ma_out[...] = m_new
        va_out[...] = v_new
        loss_out[...] = jnp.broadcast_to(jnp.sum(ga[na - 1:na, :], axis=1, keepdims=True), loss_out.shape)
        gb = _sum_blocks(tg_ref)
        delta, m_new, v_new = _adamw_math(wb_ref[...], gb, mb_ref[...], vb_ref[...])
        gb_out[...] = gb
        db_out[...] = delta
        mb_out[...] = m_new
        vb_out[...] = v_new

    oa, ob = jax.ShapeDtypeStruct((na, d), F32), jax.ShapeDtypeStruct((nb, cb), F32)
    vm = pl.BlockSpec(memory_space=pltpu.VMEM)
    return pl.pallas_call(
        body, name=name, in_specs=[vm] * 8, out_specs=[vm] * 9,
        out_shape=[oa] * 4 + [ob] * 4 + [jax.ShapeDtypeStruct((1, LANES), F32)],
        compiler_params=pltpu.CompilerParams(vmem_limit_bytes=_vmem_limit(_nbytes(part_g.shape, F32))),
    )(part_g, tap_g, w_a, m_a, v_a, w_b, m_b, v_b)


BIG = ("cf_w_pw1", "cf_w_pw2", "sc_w_in", "sc_w_out", "mlp_w1", "mlp_w2", "ple_w_proj", "ple_w_gate")
COL_SHARDED = ("cf_w_pw1", "sc_w_in", "mlp_w1", "ple_w_proj")
WEIGHT_ORDER = ("norm_mix", "norm_mlp", "norm_ple", "cf_w_pw1", "cf_b_pw1", "cf_w_dw", "cf_b_dw", "cf_norm",
                "cf_w_pw2", "cf_b_pw2", "sc_w_in", "sc_w_conv", "sc_w_out", "mlp_w1", "mlp_w2", "ple_w_proj",
                "ple_w_gate", "norm_final")
SMALL_ROWS = (("norm_mix", 4), ("norm_mlp", 4), ("norm_ple", 4), ("cf_b_pw1", 4), ("cf_b_dw", 2), ("cf_norm", 2),
              ("cf_b_pw2", 2), ("norm_final", 1))


def _layer_weights(i):
    mixer = (("cf_w_pw1", i // 2), ("cf_w_pw2", i // 2)) if i % 2 == 0 else (("sc_w_in", i // 2), ("sc_w_out", i // 2))
    return mixer + (("mlp_w1", i), ("mlp_w2", i), ("ple_w_proj", i), ("ple_w_gate", i))


def _pad_rows(a, rows):
    return jnp.pad(a, ((0, 0), (0, rows - a.shape[1]), (0, 0)))


def _pack_taps(cf, sc):
    c = cf.shape[2]
    return jnp.concatenate([_pad_rows(cf, CONV_A_HALO).reshape(-1, c), _pad_rows(sc, CONV_B_HALO).reshape(-1, c)], axis=0)


def _unpack_taps(t, n_cf):
    c = t.shape[1]
    cf = t[:n_cf * CONV_A_HALO].reshape(n_cf, CONV_A_HALO, c)[:, :CONV_A_TAPS]
    sc = t[n_cf * CONV_A_HALO:].reshape(-1, CONV_B_HALO, c)[:, :CONV_B_TAPS]
    return cf, sc


def _pack_small(vals, d):
    return jnp.concatenate([vals[k].reshape(-1, d) for k, _ in SMALL_ROWS] + [jnp.zeros((1, d), F32)], axis=0)


def _unpack_small(a, shapes):
    out, r = {}, 0
    for k, n in SMALL_ROWS:
        out[k] = a[r:r + n].reshape(shapes[k])
        r += n
    return out


def kernel(x, p, norm_mix, norm_mlp, norm_ple, cf_w_pw1, cf_b_pw1, cf_w_dw, cf_b_dw, cf_norm, cf_w_pw2, cf_b_pw2, sc_w_in, sc_w_conv, sc_w_out, mlp_w1, mlp_w2, ple_w_proj, ple_w_gate, norm_final, loss_target, m_norm_mix, m_norm_mlp, m_norm_ple, m_cf_w_pw1, m_cf_b_pw1, m_cf_w_dw, m_cf_b_dw, m_cf_norm, m_cf_w_pw2, m_cf_b_pw2, m_sc_w_in, m_sc_w_conv, m_sc_w_out, m_mlp_w1, m_mlp_w2, m_ple_w_proj, m_ple_w_gate, m_norm_final, v_norm_mix, v_norm_mlp, v_norm_ple, v_cf_w_pw1, v_cf_b_pw1, v_cf_w_dw, v_cf_b_dw, v_cf_norm, v_cf_w_pw2, v_cf_b_pw2, v_sc_w_in, v_sc_w_conv, v_sc_w_out, v_mlp_w1, v_mlp_w2, v_ple_w_proj, v_ple_w_gate, v_norm_final):
    w = dict(norm_mix=norm_mix, norm_mlp=norm_mlp, norm_ple=norm_ple, cf_w_pw1=cf_w_pw1, cf_b_pw1=cf_b_pw1,
             cf_w_dw=cf_w_dw, cf_b_dw=cf_b_dw, cf_norm=cf_norm, cf_w_pw2=cf_w_pw2, cf_b_pw2=cf_b_pw2,
             sc_w_in=sc_w_in, sc_w_conv=sc_w_conv, sc_w_out=sc_w_out, mlp_w1=mlp_w1, mlp_w2=mlp_w2,
             ple_w_proj=ple_w_proj, ple_w_gate=ple_w_gate, norm_final=norm_final)
    m = dict(norm_mix=m_norm_mix, norm_mlp=m_norm_mlp, norm_ple=m_norm_ple, cf_w_pw1=m_cf_w_pw1, cf_b_pw1=m_cf_b_pw1,
             cf_w_dw=m_cf_w_dw, cf_b_dw=m_cf_b_dw, cf_norm=m_cf_norm, cf_w_pw2=m_cf_w_pw2, cf_b_pw2=m_cf_b_pw2,
             sc_w_in=m_sc_w_in, sc_w_conv=m_sc_w_conv, sc_w_out=m_sc_w_out, mlp_w1=m_mlp_w1, mlp_w2=m_mlp_w2,
             ple_w_proj=m_ple_w_proj, ple_w_gate=m_ple_w_gate, norm_final=m_norm_final)
    v = dict(norm_mix=v_norm_mix, norm_mlp=v_norm_mlp, norm_ple=v_norm_ple, cf_w_pw1=v_cf_w_pw1, cf_b_pw1=v_cf_b_pw1,
             cf_w_dw=v_cf_w_dw, cf_b_dw=v_cf_b_dw, cf_norm=v_cf_norm, cf_w_pw2=v_cf_w_pw2, cf_b_pw2=v_cf_b_pw2,
             sc_w_in=v_sc_w_in, sc_w_conv=v_sc_w_conv, sc_w_out=v_sc_w_out, mlp_w1=v_mlp_w1, mlp_w2=v_mlp_w2,
             ple_w_proj=v_ple_w_proj, ple_w_gate=v_ple_w_gate, norm_final=v_norm_final)
    depth, d = norm_mix.shape
    n_cf = cf_w_dw.shape[0]

    taps_w = _pack_taps(cf_w_dw, sc_w_conv)
    shards, gathered = {}, {}
    ids = iter(range(5 * depth))
    for i in range(depth):
        names = _layer_weights(i)
        groups = [names[:1], names[1:2], names[2:3], names[3:]] if i == 0 else [names[:2], names[2:]]
        for n_group, group in enumerate(groups):
            with_taps = i == 0 and n_group == 1
            for k, _ in group:
                if k not in shards:
                    shards[k] = cast_bf16(f"cast_{k}", w[k])
            srcs = [shards[k] for k, _ in group] + ([taps_w[None]] if with_taps else [])
            got = sc_gather(f"gather_{i}_{n_group}", next(ids), srcs, [l for _, l in group] + ([0] if with_taps else []))
            for (k, l), g in zip(group, got):
                gathered[(k, l)] = g if k in COL_SHARDED else g.reshape(-1, g.shape[3])
            if with_taps:
                taps_full = jnp.transpose(got[-1][:, 0], (1, 0, 2)).reshape(taps_w.shape[0], d)
    conv_w = {"cf": taps_full[:n_cf * CONV_A_HALO].reshape(n_cf, CONV_A_HALO, d),
              "sc": taps_full[n_cf * CONV_A_HALO:].reshape(-1, CONV_B_HALO, d)}

    def get_w(k, l, after=None):
        return gathered[(k, l)]

    received, waiting = {}, {}
    last_group = list(_layer_weights(0)[:2])

    def put_grads(grads):
        names = list(grads)
        if names[0][0] in ("cf_w_pw2", "sc_w_out") and set(names) != set(last_group):
            waiting.update(grads)
            return
        grads = {**waiting, **grads}
        waiting.clear()
        names = list(grads)
        got = sc_exchange(f"grad_exchange_{names[-1][0]}_{names[-1][1]}", next(ids), [grads[n] for n in names],
                          [jax.ShapeDtypeStruct(grads[n].shape, BF16) for n in names],
                          [_slice_of] * len(names), [_slice_of] * len(names))
        received.update(zip(names, got))

    small = {k: w[k] for k, _ in SMALL_ROWS}
    small["norm_final"] = norm_final[None]
    loss_part, grad_x, sg = _local_step(x[0], p[:, 0], loss_target[0], small, get_w, conv_w, put_grads)

    out = {k: None for k in BIG}
    previous = None
    for (k, l), recv in received.items():
        if (k, l) not in last_group:
            out[k] = adamw_layer(f"adamw_{k}_{l}", recv, w[k], m[k], v[k], l, out[k], after=previous)
            previous = out[k][1]
    updated_first = [out[k][0] for k in BIG if out[k] is not None and k not in [n for n, _ in last_group]]

    parts = []
    for k, n in SMALL_ROWS:
        for g in sg[k]:
            parts += [g[:, :d], g[:, d:]] if g.shape[1] == 2 * d else [g]
    parts.append(loss_part)
    sums, tap_slices = pack_small_grads("pack_small_grads", parts, sg["cf_w_dw"] + sg["sc_w_conv"], NDEV)
    part_all, tap_mine = _exchange(
        "small_exchange", [sums[None], tap_slices],
        [jax.ShapeDtypeStruct((NDEV, 1) + sums.shape, F32), jax.ShapeDtypeStruct(tap_slices.shape, F32)],
        [_gather_src(0), _slice_of], [_gather_dst, _slice_of], after=updated_first)
    for k, l in last_group:
        out[k] = adamw_layer(f"adamw_{k}_{l}", received[(k, l)], w[k], m[k], v[k], l, out[k], after=part_all)
    sm = small_update("small_update", part_all[:, 0], tap_mine,
                      _pack_small(w, d), _pack_small(m, d), _pack_small(v, d),
                      taps_w, _pack_taps(m["cf_w_dw"], m["sc_w_conv"]), _pack_taps(v["cf_w_dw"], v["sc_w_conv"]))
    shapes = {k: w[k].shape for k, _ in SMALL_ROWS}
    for t in range(4):
        un = _unpack_small(sm[t], shapes)
        cf_t, sc_t = _unpack_taps(sm[4 + t], n_cf)
        for k in un:
            out.setdefault(k, [None] * 4)[t] = un[k]
        out.setdefault("cf_w_dw", [None] * 4)[t] = cf_t
        out.setdefault("sc_w_conv", [None] * 4)[t] = sc_t
    loss = sm[8][0, 0]

    return (loss, grad_x[None], *[out[k][0] for k in WEIGHT_ORDER], *[out[k][1] for k in WEIGHT_ORDER],
            *[out[k][2] for k in WEIGHT_ORDER], *[out[k][3] for k in WEIGHT_ORDER])
```

```python
import jax
import jax.numpy as jnp
from jax import lax
from jax.experimental import pallas as pl
from jax.experimental.pallas import tpu as pltpu
from jax.experimental.pallas import tpu_sc as plsc

F32 = jnp.float32
BF16 = jnp.bfloat16
EPS = 1e-6
NDEV = 8
N_PEERS = NDEV - 1
MESH = pl.DeviceIdType.MESH

ADAM_LR = 0.001
ADAM_B1 = 0.9
ADAM_B2 = 0.999
ADAM_EPS = 1e-08
ADAM_WD = 0.01
ADAM_STEP = 10

V7X_VMEM_BYTES = 64 * 1024 * 1024
VMEM_LIMIT_MAX = 56 * 1024 * 1024
SUBLANES = 8
LANES = 128
CONV_A_TAPS = 31
CONV_A_HALO = 32
CONV_B_TAPS = 3
CONV_B_HALO = 8


def _nbytes(shape, dtype):
    n = 1
    for s in shape:
        if s is not None:
            n *= s
    return n * jnp.dtype(dtype).itemsize


def _vmem_limit(block_bytes, scratch_bytes=0):
    need = 2 * block_bytes + scratch_bytes
    return int(min(VMEM_LIMIT_MAX, max(32 * 1024 * 1024, need + need // 2 + (4 << 20))))


def _params(sem, block_bytes, scratch_bytes=0):
    return pltpu.CompilerParams(dimension_semantics=sem, vmem_limit_bytes=_vmem_limit(block_bytes, scratch_bytes))


_DIMS = {
    "nn": (((1,), (0,)), ((), ())),
    "nt": (((1,), (1,)), ((), ())),
    "tn": (((0,), (0,)), ((), ())),
}


def _mm(name, dims, grid, acc_shape, a, a_spec, b, b_spec, extras, outs, epi):
    ni, nj, nk = grid
    n_ex, n_out = len(extras), len(outs)
    dn = _DIMS[dims]
    b_sub = [s for s in b_spec.block_shape if s is not None]
    n_sub = b_sub[0] if len(b_sub) == 3 else 1

    def body(*refs):
        a_ref, b_ref = refs[0], refs[1]
        ex_refs = refs[2:2 + n_ex]
        out_refs = refs[2 + n_ex:2 + n_ex + n_out]
        if n_sub == 1:
            d = lax.dot_general(a_ref[...].astype(BF16), b_ref[...].astype(BF16), dn, preferred_element_type=F32)
        else:
            w_sub = a_ref.shape[1] // n_sub
            d = None
            for s in range(n_sub):
                part = lax.dot_general(a_ref[:, pl.ds(s * w_sub, w_sub)].astype(BF16), b_ref[s].astype(BF16), dn,
                                       preferred_element_type=F32)
                d = part if d is None else d + part

        def finish(acc):
            res = epi(acc, *[r[...] for r in ex_refs])
            for o_ref, r in zip(out_refs, res):
                o_ref[...] = r.astype(o_ref.dtype)

        if nk == 1:
            finish(d)
        else:
            acc_ref = refs[2 + n_ex + n_out]
            k = pl.program_id(2)

            @pl.when(k == 0)
            def _():
                acc_ref[...] = d

            @pl.when(jnp.logical_and(k > 0, k < nk - 1))
            def _():
                acc_ref[...] += d

            @pl.when(k == nk - 1)
            def _():
                finish(acc_ref[...] + d)

    blk = _nbytes(a_spec.block_shape, a.dtype) + _nbytes(b_spec.block_shape, b.dtype)
    for arr, spec in list(extras) + list(outs):
        blk += _nbytes(spec.block_shape, arr.dtype)
    acc_bytes = _nbytes(acc_shape, F32)
    scratch = [pltpu.VMEM(acc_shape, F32)] if nk > 1 else []
    return pl.pallas_call(
        body,
        name=name,
        grid=grid,
        in_specs=[a_spec, b_spec] + [s for _, s in extras],
        out_specs=[s for _, s in outs],
        out_shape=[o for o, _ in outs],
        scratch_shapes=scratch,
        compiler_params=_params(("parallel", "parallel", "arbitrary"), blk, 3 * acc_bytes),
    )(a, b, *[e for e, _ in extras])


def _tile(n, pref):
    if n <= pref:
        return n
    t = pref - pref % LANES
    while t > LANES and n % t:
        t -= LANES
    assert n % t == 0, (n, pref)
    return t


def _row_tile(n, pref):
    if n <= pref:
        return n
    t = max(SUBLANES, pref - pref % SUBLANES)
    while t > SUBLANES and n % t:
        t -= SUBLANES
    assert n % t == 0, (n, pref)
    return t


def _id_epi(acc):
    return (acc,)


def mm_x_wcol(name, x, w, layer, extras=(), outs_dtypes=(F32,), epi=_id_epi, tm=1024, tn=1024):
    m, kdim = x.shape
    c = w.shape[3]
    n = NDEV * c
    tm, tn = _tile(m, tm), _tile(c, tn)
    tk = _tile(kdim, 2048)
    grid = (m // tm, n // tn, kdim // tk)
    per = c // tn
    a_spec = pl.BlockSpec((tm, tk), lambda i, j, k: (i, k))
    b_spec = pl.BlockSpec((None, None, tk, tn), lambda i, j, k: (j // per, layer, k, j % per))
    ex = [(e, _ex_spec(e, kind, tm, tn)) for e, kind in extras]
    o_spec = pl.BlockSpec((tm, tn), lambda i, j, k: (i, j))
    outs = [(jax.ShapeDtypeStruct((m, n), dt), o_spec) for dt in outs_dtypes]
    return _mm(name, "nn", grid, (tm, tn), x, a_spec, w, b_spec, ex, outs, epi)


def mm_x_wrow(name, x, w, extras=(), outs_dtypes=(F32,), epi=_id_epi, tm=1024, tn=512):
    m, kdim = x.shape
    n = w.shape[1]
    assert kdim == w.shape[0]
    tm, tn = _tile(m, tm), _tile(n, tn)
    tk = _tile(kdim, 2048)
    grid = (m // tm, n // tn, kdim // tk)
    a_spec = pl.BlockSpec((tm, tk), lambda i, j, k: (i, k))
    b_spec = pl.BlockSpec((tk, tn), lambda i, j, k: (k, j))
    ex = [(e, _ex_spec(e, kind, tm, tn)) for e, kind in extras]
    o_spec = pl.BlockSpec((tm, tn), lambda i, j, k: (i, j))
    outs = [(jax.ShapeDtypeStruct((m, n), dt), o_spec) for dt in outs_dtypes]
    return _mm(name, "nn", grid, (tm, tn), x, a_spec, w, b_spec, ex, outs, epi)


def mm_dy_wcol_t(name, dy, w, layer, extras=(), outs_dtypes=(F32,), epi=_id_epi, tm=1024, tn=1024):
    m, n = dy.shape
    nd, kdim, c = w.shape[0], w.shape[2], w.shape[3]
    assert n == nd * c
    tm, tn = _tile(m, tm), _tile(kdim, tn)
    n_sub = max(1, min(nd, 2048 // c))
    while nd % n_sub:
        n_sub -= 1
    grid = (m // tm, kdim // tn, nd // n_sub)
    a_spec = pl.BlockSpec((tm, n_sub * c), lambda i, j, k: (i, k))
    if n_sub > 1:
        b_spec = pl.BlockSpec((n_sub, None, tn, c), lambda i, j, k: (k, layer, j, 0))
    else:
        b_spec = pl.BlockSpec((None, None, tn, c), lambda i, j, k: (k, layer, j, 0))
    ex = [(e, _ex_spec(e, kind, tm, tn)) for e, kind in extras]
    o_spec = pl.BlockSpec((tm, tn), lambda i, j, k: (i, j))
    outs = [(jax.ShapeDtypeStruct((m, kdim), dt), o_spec) for dt in outs_dtypes]
    return _mm(name, "nt", grid, (tm, tn), dy, a_spec, w, b_spec, ex, outs, epi)


def mm_dy_wrow_t(name, dy, w, extras=(), outs_dtypes=(F32,), epi=_id_epi, tm=1024, tn=1024):
    m, n = dy.shape
    kdim = w.shape[0]
    assert n == w.shape[1]
    tm, tn = _tile(m, tm), _tile(kdim, tn)
    tk = _tile(n, 2048)
    grid = (m // tm, kdim // tn, n // tk)
    a_spec = pl.BlockSpec((tm, tk), lambda i, j, k: (i, k))
    b_spec = pl.BlockSpec((tn, tk), lambda i, j, k: (j, k))
    ex = [(e, _ex_spec(e, kind, tm, tn)) for e, kind in extras]
    o_spec = pl.BlockSpec((tm, tn), lambda i, j, k: (i, j))
    outs = [(jax.ShapeDtypeStruct((m, kdim), dt), o_spec) for dt in outs_dtypes]
    return _mm(name, "nt", grid, (tm, tn), dy, a_spec, w, b_spec, ex, outs, epi)


def mm_xt_dy(name, x, dy, col_shards, tm=1024, tn=1024):
    m, kdim = x.shape
    n = dy.shape[1]
    tk = _tile(m, 4096)
    if col_shards:
        c = n // NDEV
        tm, tn = _tile(kdim, tm), _tile(c, tn)
        per = c // tn
        out = jax.ShapeDtypeStruct((NDEV, kdim, c), BF16)
        o_spec = pl.BlockSpec((None, tm, tn), lambda i, j, k: (j // per, i, j % per))
    else:
        tm, tn = _tile(kdim, tm), _tile(n, tn)
        out = jax.ShapeDtypeStruct((kdim, n), BF16)
        o_spec = pl.BlockSpec((tm, tn), lambda i, j, k: (i, j))
    grid = (kdim // tm, n // tn, m // tk)
    a_spec = pl.BlockSpec((tk, tm), lambda i, j, k: (k, i))
    b_spec = pl.BlockSpec((tk, tn), lambda i, j, k: (k, j))
    g = _mm(name, "tn", grid, (tm, tn), x, a_spec, dy, b_spec, [], [(out, o_spec)], _id_epi)[0]
    return g if col_shards else g.reshape(NDEV, kdim // NDEV, n)


def _ex_spec(e, kind, tm, tn):
    if kind == "tile":
        return pl.BlockSpec((tm, tn), lambda i, j, k: (i, j))
    if kind == "row":
        return pl.BlockSpec((1, tn), lambda i, j, k: (0, j))
    raise ValueError(kind)


def _rows_call(name, body, n_rows, ts, ins, outs, scratch=(), scratch_bytes=0):
    blk = sum(_nbytes(s.block_shape, a.dtype) for a, s in list(ins) + list(outs))
    return pl.pallas_call(
        body,
        name=name,
        grid=(n_rows // ts,),
        in_specs=[s for _, s in ins],
        out_specs=[s for _, s in outs],
        out_shape=[o for o, _ in outs],
        scratch_shapes=list(scratch),
        compiler_params=_params(("arbitrary",), blk, scratch_bytes + 4 * blk // 2),
    )(*[a for a, _ in ins])


def _blk(ts, d):
    return pl.BlockSpec((ts, d), lambda i: (i, 0))


def _full(shape):
    return pl.BlockSpec(shape, lambda i: tuple(0 for _ in shape))


def _rowsum8(v):
    t, d = v.shape
    return jnp.sum(v.reshape(t // SUBLANES, SUBLANES, d), axis=0)


def _accumulate(ref, val):
    @pl.when(pl.program_id(0) == 0)
    def _():
        ref[...] = val

    @pl.when(pl.program_id(0) > 0)
    def _():
        ref[...] += val


def _rstd(x):
    return lax.rsqrt(jnp.mean(x * x, axis=-1, keepdims=True) + EPS)


def _rms_bwd_math(dy, x, g):
    r = _rstd(x)
    gdy = dy * g
    c = jnp.mean(gdy * x, axis=-1, keepdims=True)
    dx = r * gdy - x * (r * r * r * c)
    return dx, dy * (x * r)


def rms_fwd(name, h, g, ts=512):
    s, d = h.shape
    ts = min(ts, s)

    def body(h_ref, g_ref, u_ref):
        x = h_ref[...]
        u_ref[...] = ((x * _rstd(x)) * g_ref[...]).astype(BF16)

    return _rows_call(name, body, s, ts, [(h, _blk(ts, d)), (g, _full((1, d)))],
                      [(jax.ShapeDtypeStruct((s, d), BF16), _blk(ts, d))])[0]


def rms_bwd(name, du, h, g, dres, ts=256):
    s, d = h.shape
    ts = min(ts, s)

    def body(du_ref, h_ref, g_ref, dres_ref, dh_ref, dhb_ref, dg_ref, cs_ref):
        dx, dg = _rms_bwd_math(du_ref[...], h_ref[...], g_ref[...])
        dh = dres_ref[...] + dx
        dh_ref[...] = dh
        dhb_ref[...] = dh.astype(BF16)
        _accumulate(dg_ref, _rowsum8(dg))
        _accumulate(cs_ref, _rowsum8(dh))

    return _rows_call(
        name, body, s, ts,
        [(du, _blk(ts, d)), (h, _blk(ts, d)), (g, _full((1, d))), (dres, _blk(ts, d))],
        [(jax.ShapeDtypeStruct((s, d), F32), _blk(ts, d)), (jax.ShapeDtypeStruct((s, d), BF16), _blk(ts, d)),
         (jax.ShapeDtypeStruct((SUBLANES, d), F32), _full((SUBLANES, d))),
         (jax.ShapeDtypeStruct((SUBLANES, d), F32), _full((SUBLANES, d)))])


def final_loss(name, h, g, target, ts=256):
    s, d = h.shape
    ts = min(ts, s)

    def body(h_ref, g_ref, t_ref, loss_ref, dh_ref, dg_ref):
        x = h_ref[...]
        gf = g_ref[...]
        y = (x * _rstd(x)) * gf
        err = y - t_ref[...]
        _accumulate(loss_ref, _rowsum8(err * err) * (0.5 / d))
        dx, dg = _rms_bwd_math(err * (1.0 / d), x, gf)
        dh_ref[...] = dx
        _accumulate(dg_ref, _rowsum8(dg))

    return _rows_call(
        name, body, s, ts,
        [(h, _blk(ts, d)), (g, _full((1, d))), (target, _blk(ts, d))],
        [(jax.ShapeDtypeStruct((SUBLANES, d), F32), _full((SUBLANES, d))),
         (jax.ShapeDtypeStruct((s, d), F32), _blk(ts, d)),
         (jax.ShapeDtypeStruct((SUBLANES, d), F32), _full((SUBLANES, d)))])


def ple_bwd_elem(name, dh, g, e, ts=512):
    s, d = dh.shape
    ts = min(ts, s)

    def body(dh_ref, g_ref, e_ref, de_ref, dgl_ref):
        dh_v, g_v = dh_ref[...], g_ref[...]
        de_ref[...] = (dh_v * g_v).astype(BF16)
        dgl_ref[...] = (dh_v * e_ref[...] * (g_v * (1.0 - g_v))).astype(BF16)

    return _rows_call(name, body, s, ts, [(dh, _blk(ts, d)), (g, _blk(ts, d)), (e, _blk(ts, d))],
                      [(jax.ShapeDtypeStruct((s, d), BF16), _blk(ts, d)),
                       (jax.ShapeDtypeStruct((s, d), BF16), _blk(ts, d))])


CONV_LANES = 256
CONV_ROWS = 64


def _lane_chunks(d, fn):
    lc = min(CONV_LANES, d)

    def lane_body(c, carry):
        fn(pl.ds(pl.multiple_of(c * lc, lc), lc))
        return carry

    lax.fori_loop(0, d // lc, lane_body, 0)


def _shifted_copies(buf, sh, lanes):
    rows = buf.shape[0] - SUBLANES
    for s in range(1, SUBLANES):
        sh[s, pl.ds(0, rows), :] = buf[pl.ds(s, rows), lanes]


def _window(buf, sh, lanes, start, rows):
    s = start % SUBLANES
    if s == 0:
        return buf[pl.ds(start, rows), lanes]
    return sh[s, pl.ds(start - s, rows), :]


def _prev_halo_spec(ts, halo, width):
    per = ts // halo
    return pl.BlockSpec((halo, width), lambda i: (jnp.maximum(i * per - 1, 0), 0))


def _next_halo_spec(ts, halo, width, n_rows):
    per = ts // halo
    last = n_rows // halo - 1
    return pl.BlockSpec((halo, width), lambda i: (jnp.minimum((i + 1) * per, last), 0))


def cf_fwd_mid(name, a, w_dw, b_dw, gn, ts=256):
    s, d2 = a.shape
    d = d2 // 2
    ts = min(ts, s)
    hl = CONV_A_HALO
    off = hl - (CONV_A_TAPS - 1)

    rc = min(CONV_ROWS, ts)

    def body(a_ref, ah_ref, w_ref, b_ref, gn_ref, v0_ref, v1_ref, v3_ref, buf, sh):
        first = pl.program_id(0) == 0
        halo = ah_ref[...]
        hv0 = halo[:, :d] * jax.nn.sigmoid(halo[:, d:])
        buf[pl.ds(0, hl), :] = jnp.where(first, 0.0, hv0)
        main = a_ref[...]
        v0 = main[:, :d] * jax.nn.sigmoid(main[:, d:])
        buf[pl.ds(hl, ts), :] = v0
        v0_ref[...] = v0

        def conv(lanes):
            _shifted_copies(buf, sh, lanes)
            for r0 in range(0, ts, rc):
                acc = jnp.zeros((rc, lanes.size), F32)
                for k in range(CONV_A_TAPS):
                    acc = acc + w_ref[pl.ds(k, 1), lanes] * _window(buf, sh, lanes, r0 + off + k, rc)
                v1_ref[pl.ds(r0, rc), lanes] = acc + b_ref[:, lanes]

        _lane_chunks(d, conv)
        v1 = v1_ref[...]
        v2 = (v1 * _rstd(v1)) * gn_ref[...]
        v3_ref[...] = (v2 * jax.nn.sigmoid(v2)).astype(BF16)

    return _rows_call(
        name, body, s, ts,
        [(a, _blk(ts, d2)), (a, _prev_halo_spec(ts, hl, d2)), (w_dw, _full(w_dw.shape)),
         (b_dw, _full((1, d))), (gn, _full((1, d)))],
        [(jax.ShapeDtypeStruct((s, d), F32), _blk(ts, d)), (jax.ShapeDtypeStruct((s, d), F32), _blk(ts, d)),
         (jax.ShapeDtypeStruct((s, d), BF16), _blk(ts, d))],
        scratch=[pltpu.VMEM((hl + ts, d), F32), pltpu.VMEM((SUBLANES, hl + ts, min(CONV_LANES, d)), F32)],
        scratch_bytes=_nbytes((hl + ts, d + SUBLANES * CONV_LANES), F32))


def cf_bwd_rows(name, dv3, v1, gn, ts=256):
    s, d = v1.shape
    ts = min(ts, s)

    def body(dv3_ref, v1_ref, gn_ref, dv1_ref, dgn_ref, db_ref):
        v1 = v1_ref[...]
        gn_v = gn_ref[...]
        v2 = (v1 * _rstd(v1)) * gn_v
        sg = jax.nn.sigmoid(v2)
        dv2 = dv3_ref[...] * (sg * (1.0 + v2 * (1.0 - sg)))
        dv1, dgn = _rms_bwd_math(dv2, v1, gn_v)
        dv1_ref[...] = dv1
        _accumulate(dgn_ref, _rowsum8(dgn))
        _accumulate(db_ref, _rowsum8(dv1))

    return _rows_call(
        name, body, s, ts, [(dv3, _blk(ts, d)), (v1, _blk(ts, d)), (gn, _full((1, d)))],
        [(jax.ShapeDtypeStruct((s, d), F32), _blk(ts, d)),
         (jax.ShapeDtypeStruct((SUBLANES, d), F32), _full((SUBLANES, d))),
         (jax.ShapeDtypeStruct((SUBLANES, d), F32), _full((SUBLANES, d)))])


def cf_bwd_conv(name, dv1, v0, a, w_dw, ts=256):
    s, d = dv1.shape
    ts = min(ts, s)
    hl = CONV_A_HALO
    taps = CONV_A_TAPS
    off = hl - (taps - 1)
    last_blk = s // ts - 1

    rc = min(CONV_ROWS, ts)

    def body(dv1_ref, dv1n_ref, v0_ref, v0p_ref, a_ref, w_ref, da_ref, dw_ref, db_ref,
             dbuf, vbuf, dv0_buf, dw_acc, dsh, vsh):
        i = pl.program_id(0)
        dbuf[pl.ds(0, ts), :] = dv1_ref[...]
        dbuf[pl.ds(ts, hl), :] = jnp.where(i == last_blk, 0.0, dv1n_ref[...])
        vbuf[pl.ds(0, hl), :] = jnp.where(i == 0, 0.0, v0p_ref[...])
        vbuf[pl.ds(hl, ts), :] = v0_ref[...]

        @pl.when(i == 0)
        def _():
            dw_acc[...] = jnp.zeros_like(dw_acc)

        def conv_t(lanes):
            _shifted_copies(dbuf, dsh, lanes)
            _shifted_copies(vbuf, vsh, lanes)
            for r0 in range(0, ts, rc):
                g = dbuf[pl.ds(r0, rc), lanes]
                acc = jnp.zeros((rc, lanes.size), F32)
                for k in range(taps):
                    acc = acc + w_ref[pl.ds(k, 1), lanes] * _window(dbuf, dsh, lanes, r0 + taps - 1 - k, rc)
                    prod = g * _window(vbuf, vsh, lanes, r0 + off + k, rc)
                    dw_acc[pl.ds(k * SUBLANES, SUBLANES), lanes] += _rowsum8(prod)
                dv0_buf[pl.ds(r0, rc), lanes] = acc

        _lane_chunks(d, conv_t)
        dv0 = dv0_buf[...]
        av = a_ref[...]
        val, sg = av[:, :d], jax.nn.sigmoid(av[:, d:])
        dval = dv0 * sg
        dgate = dv0 * val * (sg * (1.0 - sg))
        da_ref[:, :d] = dval.astype(BF16)
        da_ref[:, d:] = dgate.astype(BF16)
        _accumulate(db_ref.at[:, pl.ds(0, d)], _rowsum8(dval))
        _accumulate(db_ref.at[:, pl.ds(d, d)], _rowsum8(dgate))

        @pl.when(i == last_blk)
        def _():
            dw_ref[...] = jnp.sum(dw_acc[...].reshape(hl, SUBLANES, d), axis=1)

    lc = min(CONV_LANES, d)
    scratch = [pltpu.VMEM((ts + hl, d), F32), pltpu.VMEM((hl + ts, d), F32), pltpu.VMEM((ts, d), F32),
               pltpu.VMEM((hl * SUBLANES, d), F32), pltpu.VMEM((SUBLANES, ts + hl, lc), F32),
               pltpu.VMEM((SUBLANES, hl + ts, lc), F32)]
    sbytes = _nbytes((3 * ts + 2 * hl + hl * SUBLANES, d), F32) + 2 * _nbytes((SUBLANES, ts + hl, lc), F32)
    return _rows_call(
        name, body, s, ts,
        [(dv1, _blk(ts, d)), (dv1, _next_halo_spec(ts, hl, d, s)), (v0, _blk(ts, d)), (v0, _prev_halo_spec(ts, hl, d)),
         (a, _blk(ts, 2 * d)), (w_dw, _full(w_dw.shape))],
        [(jax.ShapeDtypeStruct((s, 2 * d), BF16), _blk(ts, 2 * d)),
         (jax.ShapeDtypeStruct((hl, d), F32), _full((hl, d))),
         (jax.ShapeDtypeStruct((SUBLANES, 2 * d), F32), _full((SUBLANES, 2 * d)))],
        scratch=scratch, scratch_bytes=sbytes)


def sc_fwd_mid(name, bcv, w_conv, ts=256):
    s, d3 = bcv.shape
    d = d3 // 3
    ts = min(ts, s)
    hl = CONV_B_HALO
    off = hl - (CONV_B_TAPS - 1)

    def body(x_ref, xp_ref, w_ref, y_ref, buf):
        hp = xp_ref[...]
        buf[pl.ds(0, hl), :] = jnp.where(pl.program_id(0) == 0, 0.0, hp[:, d:2 * d] * hp[:, 2 * d:])
        buf[pl.ds(hl, ts), :] = x_ref[:, d:2 * d] * x_ref[:, 2 * d:]
        cc = jnp.zeros((ts, d), F32)
        for k in range(CONV_B_TAPS):
            cc = cc + w_ref[pl.ds(k, 1), :] * buf[pl.ds(off + k, ts), :]
        y_ref[...] = (x_ref[:, :d] * cc).astype(BF16)

    return _rows_call(
        name, body, s, ts,
        [(bcv, _blk(ts, d3)), (bcv, _prev_halo_spec(ts, hl, d3)), (w_conv, _full(w_conv.shape))],
        [(jax.ShapeDtypeStruct((s, d), BF16), _blk(ts, d))],
        scratch=[pltpu.VMEM((hl + ts, d), F32)], scratch_bytes=_nbytes((hl + ts, d), F32))[0]


def sc_bwd_mid(name, dy, bcv, w_conv, ts=256):
    s, d3 = bcv.shape
    d = d3 // 3
    ts = min(ts, s)
    hl = CONV_B_HALO
    taps = CONV_B_TAPS
    off = hl - (taps - 1)
    last_blk = s // ts - 1

    def body(dy_ref, dyn_ref, x_ref, xp_ref, xn_ref, w_ref, dx_ref, dw_ref, cvbuf, dbuf, dw_acc):
        i = pl.program_id(0)
        hp = xp_ref[...]
        cvbuf[pl.ds(0, hl), :] = jnp.where(i == 0, 0.0, hp[:, d:2 * d] * hp[:, 2 * d:])
        gb, gc, v = x_ref[:, :d], x_ref[:, d:2 * d], x_ref[:, 2 * d:]
        cvbuf[pl.ds(hl, ts), :] = gc * v
        dy_v = dy_ref[...]
        dcc = dy_v * gb
        dbuf[pl.ds(0, ts), :] = dcc
        dbuf[pl.ds(ts, hl), :] = jnp.where(i == last_blk, 0.0, dyn_ref[...] * xn_ref[:, :d])

        @pl.when(i == 0)
        def _():
            dw_acc[...] = jnp.zeros_like(dw_acc)

        cc = jnp.zeros((ts, d), F32)
        dcv = jnp.zeros((ts, d), F32)
        for k in range(taps):
            win = cvbuf[pl.ds(off + k, ts), :]
            cc = cc + w_ref[pl.ds(k, 1), :] * win
            dcv = dcv + w_ref[pl.ds(k, 1), :] * dbuf[pl.ds(taps - 1 - k, ts), :]
            dw_acc[pl.ds(k * SUBLANES, SUBLANES), :] += _rowsum8(dcc * win)
        dx_ref[:, :d] = (dy_v * cc).astype(BF16)
        dx_ref[:, d:2 * d] = (dcv * v).astype(BF16)
        dx_ref[:, 2 * d:] = (dcv * gc).astype(BF16)

        @pl.when(i == last_blk)
        def _():
            dw_ref[...] = jnp.sum(dw_acc[...].reshape(hl, SUBLANES, d), axis=1)

    scratch = [pltpu.VMEM((hl + ts, d), F32), pltpu.VMEM((ts + hl, d), F32), pltpu.VMEM((hl * SUBLANES, d), F32)]
    sbytes = _nbytes((2 * ts + 2 * hl + hl * SUBLANES, d), F32)
    return _rows_call(
        name, body, s, ts,
        [(dy, _blk(ts, d)), (dy, _next_halo_spec(ts, hl, d, s)), (bcv, _blk(ts, d3)), (bcv, _prev_halo_spec(ts, hl, d3)),
         (bcv, _next_halo_spec(ts, hl, d3, s)), (w_conv, _full(w_conv.shape))],
        [(jax.ShapeDtypeStruct((s, d3), BF16), _blk(ts, d3)), (jax.ShapeDtypeStruct((hl, d), F32), _full((hl, d)))],
        scratch=scratch, scratch_bytes=sbytes)


def _row(a, i):
    return lax.slice_in_dim(a, i, i + 1, axis=0)


def _local_step(x, p, target, small, get_w, conv_w, put_grads):
    depth = p.shape[0]
    acts = []
    h = x
    for i in range(depth):
        j = i // 2
        act = {"h": h}
        u = rms_fwd(f"rms_mix_{i}", h, _row(small["norm_mix"], i))
        act["u"] = u
        if i % 2 == 0:
            a = mm_x_wcol(f"cf_pw1_{i}", u, get_w("cf_w_pw1", j, u), 0, extras=[(_row(small["cf_b_pw1"], j), "row")],
                          epi=lambda acc, b: (acc + b,))[0]
            v0, v1, v3 = cf_fwd_mid(f"cf_mid_{i}", a, conv_w["cf"][j], _row(small["cf_b_dw"], j), _row(small["cf_norm"], j))
            act.update(a=a, v0=v0, v1=v1, v3=v3)
            h1 = mm_x_wrow(f"cf_pw2_{i}", v3, get_w("cf_w_pw2", j, v3),
                           extras=[(_row(small["cf_b_pw2"], j), "row"), (h, "tile")],
                           epi=lambda acc, b, res: (res + (acc + b),), tn=1024)[0]
        else:
            bcv = mm_x_wcol(f"sc_in_{i}", u, get_w("sc_w_in", j, u), 0, tn=768)[0]
            y = sc_fwd_mid(f"sc_mid_{i}", bcv, conv_w["sc"][j])
            act.update(bcv=bcv, y=y)
            h1 = mm_x_wrow(f"sc_out_{i}", y, get_w("sc_w_out", j, y), extras=[(h, "tile")],
                           epi=lambda acc, res: (res + acc,), tn=1024)[0]
        act["h1"] = h1
        u2 = rms_fwd(f"rms_mlp_{i}", h1, _row(small["norm_mlp"], i))
        z, hd = mm_x_wcol(f"mlp_w1_{i}", u2, get_w("mlp_w1", i, u2), 0, outs_dtypes=(F32, BF16),
                          epi=lambda acc: (acc, jnp.square(jnp.maximum(acc, 0.0))))
        h2 = mm_x_wrow(f"mlp_w2_{i}", hd, get_w("mlp_w2", i, hd), extras=[(h1, "tile")],
                       epi=lambda acc, res: (res + acc,), tn=1024)[0]
        act.update(u2=u2, z=z, hd=hd, h2=h2)
        n3 = rms_fwd(f"rms_ple_{i}", h2, _row(small["norm_ple"], i))
        e = mm_x_wcol(f"ple_proj_{i}", p[i], get_w("ple_w_proj", i, n3), 0)[0]

        def ple_epi(acc, e_t, res):
            g_t = jax.nn.sigmoid(acc)
            return g_t, res + g_t * e_t

        g, h3 = mm_x_wrow(f"ple_gate_{i}", n3, get_w("ple_w_gate", i, e), extras=[(e, "tile"), (h2, "tile")],
                          outs_dtypes=(F32, F32), epi=ple_epi)
        act.update(n3=n3, e=e, g=g)
        acts.append(act)
        h = h3

    loss_part, dh, dg_final = final_loss("final_loss", h, small["norm_final"], target)
    sg = {k: [None] * small[k].shape[0] for k in small if k != "norm_final"}
    sg["norm_final"] = [dg_final]
    sg["cf_w_dw"] = [None] * conv_w["cf"].shape[0]
    sg["sc_w_conv"] = [None] * conv_w["sc"].shape[0]

    for i in reversed(range(depth)):
        j = i // 2
        act = acts[i]
        de, dgl = ple_bwd_elem(f"ple_bwd_{i}", dh, act["g"], act["e"])
        g_proj = mm_xt_dy(f"d_ple_proj_{i}", p[i], de, True)
        g_gate = mm_xt_dy(f"d_ple_gate_{i}", act["n3"], dgl, False)
        dn3 = mm_dy_wrow_t(f"dn3_{i}", dgl, get_w("ple_w_gate", i))[0]
        dh2, dh2b, sg["norm_ple"][i], _ = rms_bwd(f"rms_ple_bwd_{i}", dn3, act["h2"], _row(small["norm_ple"], i), dh)
        g_w2 = mm_xt_dy(f"d_mlp_w2_{i}", act["hd"], dh2b, False)
        dz = mm_dy_wrow_t(f"dz_{i}", dh2b, get_w("mlp_w2", i), extras=[(act["z"], "tile")], outs_dtypes=(BF16,),
                          epi=lambda acc, z_t: (acc * (2.0 * jnp.maximum(z_t, 0.0)),))[0]
        g_w1 = mm_xt_dy(f"d_mlp_w1_{i}", act["u2"], dz, True)
        put_grads({("ple_w_proj", i): g_proj, ("ple_w_gate", i): g_gate, ("mlp_w2", i): g_w2, ("mlp_w1", i): g_w1})
        du2 = mm_dy_wcol_t(f"du2_{i}", dz, get_w("mlp_w1", i), 0)[0]
        dh1, dh1b, sg["norm_mlp"][i], cs1 = rms_bwd(f"rms_mlp_bwd_{i}", du2, act["h1"], _row(small["norm_mlp"], i), dh2)
        if i % 2 == 0:
            g_out = mm_xt_dy(f"d_cf_pw2_{i}", act["v3"], dh1b, False)
            sg["cf_b_pw2"][j] = cs1
            dv3 = mm_dy_wrow_t(f"dv3_{i}", dh1b, get_w("cf_w_pw2", j))[0]
            dv1, sg["cf_norm"][j], sg["cf_b_dw"][j] = cf_bwd_rows(f"cf_bwd_rows_{i}", dv3, act["v1"], _row(small["cf_norm"], j))
            da, sg["cf_w_dw"][j], sg["cf_b_pw1"][j] = cf_bwd_conv(f"cf_bwd_conv_{i}", dv1, act["v0"], act["a"], conv_w["cf"][j])
            g_in = mm_xt_dy(f"d_cf_pw1_{i}", act["u"], da, True)
            put_grads({("cf_w_pw2", j): g_out, ("cf_w_pw1", j): g_in})
            du = mm_dy_wcol_t(f"du_{i}", da, get_w("cf_w_pw1", j), 0)[0]
        else:
            g_out = mm_xt_dy(f"d_sc_out_{i}", act["y"], dh1b, False)
            dy = mm_dy_wrow_t(f"dy_{i}", dh1b, get_w("sc_w_out", j))[0]
            dbcv, sg["sc_w_conv"][j] = sc_bwd_mid(f"sc_bwd_mid_{i}", dy, act["bcv"], conv_w["sc"][j])
            g_in = mm_xt_dy(f"d_sc_in_{i}", act["u"], dbcv, True, tn=768)
            put_grads({("sc_w_out", j): g_out, ("sc_w_in", j): g_in})
            du = mm_dy_wcol_t(f"du_{i}", dbcv, get_w("sc_w_in", j), 0)[0]
        dh, _, sg["norm_mix"][i], _ = rms_bwd(f"rms_mix_bwd_{i}", du, act["h"], _row(small["norm_mix"], i), dh1)
    return loss_part, dh, sg


def _me_and_peers():
    x, y, c = lax.axis_index("x"), lax.axis_index("y"), lax.axis_index("c")
    me = 4 * x + 2 * y + c
    peers = []
    for q in range(1, NDEV):
        px = 1 - x if q & 4 else x
        py = 1 - y if q & 2 else y
        pc = 1 - c if q & 1 else c
        peers.append(((px, py, pc), 4 * px + 2 * py + pc))
    return me, peers


def _exchange(name, srcs, out_shapes, src_fns, dst_fns, after=()):
    n = len(srcs)
    n_after = len(after)

    def body(*refs):
        ins, outs = refs[:n], refs[n + n_after:2 * n + n_after]
        send_sems, recv_sems, local_sems = refs[2 * n + n_after:]
        me, peers = _me_and_peers()
        local, remote = [], []
        for k in range(n):
            cp = pltpu.make_async_copy(src_fns[k](ins[k], me), dst_fns[k](outs[k], me), local_sems.at[k])
            cp.start()
            local.append(cp)
        for q, (peer, peer_blk) in enumerate(peers):
            for k in range(n):
                cp = pltpu.make_async_remote_copy(
                    src_ref=src_fns[k](ins[k], peer_blk), dst_ref=dst_fns[k](outs[k], me),
                    send_sem=send_sems.at[k, q], recv_sem=recv_sems.at[k, q],
                    device_id=peer, device_id_type=MESH)
                cp.start()
                remote.append(cp)
        for q, (peer, peer_blk) in enumerate(peers):
            for k in range(n):
                pltpu.make_async_remote_copy(
                    src_ref=src_fns[k](ins[k], peer_blk), dst_ref=dst_fns[k](outs[k], peer_blk),
                    send_sem=send_sems.at[k, q], recv_sem=recv_sems.at[k, q],
                    device_id=peer, device_id_type=MESH).wait_recv()
        for cp in remote:
            cp.wait_send()
        for cp in local:
            cp.wait()

    any_spec = pl.BlockSpec(memory_space=pl.ANY)
    return pl.pallas_call(
        body,
        name=name,
        in_specs=[any_spec] * (n + n_after),
        out_specs=[any_spec] * n,
        out_shape=out_shapes,
        scratch_shapes=[pltpu.SemaphoreType.DMA((n, N_PEERS)), pltpu.SemaphoreType.DMA((n, N_PEERS)),
                        pltpu.SemaphoreType.DMA((n,))],
    )(*srcs, *after)


def sc_exchange(name, collective_id, srcs, out_shapes, src_fns, dst_fns):
    n = len(srcs)

    def body(*refs):
        ins, outs = refs[:n], refs[n:2 * n]
        send_sems, recv_sems, local_sems = refs[2 * n:]
        me, peers = _me_and_peers()
        barrier = pltpu.get_barrier_semaphore()
        for peer, _ in peers:
            pl.semaphore_signal(barrier, inc=1, device_id=peer, device_id_type=MESH)
        pl.semaphore_wait(barrier, N_PEERS)
        local, remote = [], []
        for k in range(n):
            cp = pltpu.make_async_copy(src_fns[k](ins[k], me), dst_fns[k](outs[k], me), local_sems.at[k])
            cp.start()
            local.append(cp)
        for q, (peer, peer_blk) in enumerate(peers):
            for k in range(n):
                cp = pltpu.make_async_remote_copy(
                    src_ref=src_fns[k](ins[k], peer_blk), dst_ref=dst_fns[k](outs[k], me),
                    send_sem=send_sems.at[k, q], recv_sem=recv_sems.at[k, q],
                    device_id=peer, device_id_type=MESH)
                cp.start()
                remote.append(cp)
        for q, (peer, peer_blk) in enumerate(peers):
            for k in range(n):
                pltpu.make_async_remote_copy(
                    src_ref=src_fns[k](ins[k], peer_blk), dst_ref=dst_fns[k](outs[k], peer_blk),
                    send_sem=send_sems.at[k, q], recv_sem=recv_sems.at[k, q],
                    device_id=peer, device_id_type=MESH).wait_recv()
        for cp in remote:
            cp.wait_send()
        for cp in local:
            cp.wait()

    return pl.kernel(
        body,
        out_type=out_shapes,
        mesh=plsc.ScalarSubcoreMesh(axis_name="sequencer", num_cores=1),
        name=name,
        scratch_types=[pltpu.SemaphoreType.DMA((n, N_PEERS)), pltpu.SemaphoreType.DMA((n, N_PEERS)),
                       pltpu.SemaphoreType.DMA((n,))],
        compiler_params=pltpu.CompilerParams(collective_id=collective_id),
    )(*srcs)


def sc_gather(name, collective_id, srcs, layers):
    n = len(srcs)
    outs_shape = [jax.ShapeDtypeStruct((NDEV, 1) + a.shape[1:], a.dtype) for a in srcs]

    def body(*refs):
        ins, outs = refs[:n], refs[n:2 * n]
        send_sems, recv_sems, local_sems = refs[2 * n:]
        x, y, c = lax.axis_index("x"), lax.axis_index("y"), lax.axis_index("c")
        me = 4 * x + 2 * y + c
        sibling = (x, y, 1 - c)
        chips = [(1 - x, y), (x, 1 - y), (1 - x, 1 - y)]
        barrier = pltpu.get_barrier_semaphore()
        for peer in [sibling] + [(cx, cy, c) for cx, cy in chips]:
            pl.semaphore_signal(barrier, inc=1, device_id=peer, device_id_type=MESH)
        pl.semaphore_wait(barrier, 1 + len(chips))

        def copy(k, slot, blk, to, src=None):
            place = outs[k].at[blk, 0]
            return pltpu.make_async_remote_copy(
                src_ref=place if src is None else src, dst_ref=place,
                send_sem=send_sems.at[k, slot], recv_sem=recv_sems.at[k, slot],
                device_id=to, device_id_type=MESH)

        local, sent = [], []
        for k in range(n):
            mine = ins[k].at[layers[k]]
            cp = pltpu.make_async_copy(mine, outs[k].at[me, 0], local_sems.at[k])
            cp.start()
            local.append(cp)
            sent.append(copy(k, 0, me, sibling, src=mine))
            sent += [copy(k, 1 + j, me, (cx, cy, c), src=mine) for j, (cx, cy) in enumerate(chips)]
        for cp in sent:
            cp.start()
        for k in range(n):
            for j, (cx, cy) in enumerate(chips):
                blk = 4 * cx + 2 * cy + c
                copy(k, 1 + j, blk, sibling).wait_recv()
                fwd = copy(k, 4 + j, blk, sibling)
                fwd.start()
                sent.append(fwd)
        for k in range(n):
            copy(k, 0, 4 * x + 2 * y + (1 - c), sibling).wait_recv()
            for j, (cx, cy) in enumerate(chips):
                copy(k, 4 + j, 4 * cx + 2 * cy + (1 - c), sibling).wait_recv()
        for cp in sent:
            cp.wait_send()
        for cp in local:
            cp.wait()

    return pl.kernel(
        body,
        out_type=outs_shape,
        mesh=plsc.ScalarSubcoreMesh(axis_name="sequencer", num_cores=1),
        name=name,
        scratch_types=[pltpu.SemaphoreType.DMA((n, N_PEERS)), pltpu.SemaphoreType.DMA((n, N_PEERS)),
                       pltpu.SemaphoreType.DMA((n,))],
        compiler_params=pltpu.CompilerParams(collective_id=collective_id),
    )(*srcs)


def _gather_src(layer):
    return lambda ref, blk: ref.at[layer]


def _gather_dst(ref, blk):
    return ref.at[blk, 0]


def _slice_of(ref, blk):
    return ref.at[blk]


def cast_bf16(name, w, tr_elems=512 * 1024):
    l, r, c = w.shape
    tr = _row_tile(r, tr_elems // c)
    spec = pl.BlockSpec((None, tr, c), lambda li, i: (li, i, 0))

    def body(w_ref, o_ref):
        o_ref[...] = w_ref[...].astype(BF16)

    return pl.pallas_call(
        body, name=name, grid=(l, r // tr), in_specs=[spec], out_specs=spec,
        out_shape=jax.ShapeDtypeStruct(w.shape, BF16),
        compiler_params=_params(("parallel", "parallel"), 6 * tr * c),
    )(w)


def _adamw_math(w, g, m, v):
    m = ADAM_B1 * m + (1.0 - ADAM_B1) * g
    v = ADAM_B2 * v + (1.0 - ADAM_B2) * (g * g)
    m_hat = m * (1.0 / (1.0 - ADAM_B1 ** ADAM_STEP))
    v_hat = v * (1.0 / (1.0 - ADAM_B2 ** ADAM_STEP))
    delta = -ADAM_LR * (m_hat / (jnp.sqrt(v_hat) + ADAM_EPS) + ADAM_WD * w)
    return delta, m, v


def _sum_blocks(ref):
    g = ref[0].astype(F32)
    for d in range(1, ref.shape[0]):
        g = g + ref[d].astype(F32)
    return g


def adamw_layer(name, recv, w, m, v, layer, stacked, after=None, tr_elems=256 * 1024):
    nd, r, c = recv.shape
    tr = _row_tile(r, tr_elems // c)
    r_spec = pl.BlockSpec((nd, tr, c), lambda i: (0, i, 0))
    w_spec = pl.BlockSpec((None, tr, c), lambda i: (layer, i, 0))
    if stacked is None:
        stacked = [lax.empty(w.shape, F32) for _ in range(4)]
    after = [] if after is None else [after]

    def body(r_ref, w_ref, m_ref, v_ref, g_in, d_in, m_in, v_in, *rest):
        g_out, d_out, m_out, v_out = rest[len(after):]
        g = _sum_blocks(r_ref)
        delta, m_new, v_new = _adamw_math(w_ref[...], g, m_ref[...], v_ref[...])
        g_out[...] = g
        d_out[...] = delta
        m_out[...] = m_new
        v_out[...] = v_new

    out = jax.ShapeDtypeStruct(w.shape, F32)
    return pl.pallas_call(
        body, name=name, grid=(r // tr,),
        in_specs=[r_spec, w_spec, w_spec, w_spec] + [pl.BlockSpec(memory_space=pl.ANY)] * (4 + len(after)),
        out_specs=[w_spec] * 4, out_shape=[out] * 4,
        input_output_aliases={4: 0, 5: 1, 6: 2, 7: 3},
        compiler_params=_params(("parallel",), tr * c * (2 * nd + 7 * 4)),
    )(recv, w, m, v, *stacked, *after)


def pack_small_grads(name, parts, taps, n_blocks):
    d = parts[0].shape[1]
    n_p, rows = len(parts), [t.shape[0] for t in taps]
    cb = d // n_blocks

    def body(*refs):
        part_refs, tap_refs = refs[:n_p], refs[n_p:n_p + len(taps)]
        sums_out, taps_out = refs[n_p + len(taps):]
        for i, r in enumerate(part_refs):
            sums_out[pl.ds(i, 1), :] = jnp.sum(r[...], axis=0, keepdims=True)
        r0 = 0
        for t_ref, n in zip(tap_refs, rows):
            for b in range(n_blocks):
                taps_out[b, pl.ds(r0, n), :] = t_ref[:, pl.ds(b * cb, cb)]
            r0 += n

    vm = pl.BlockSpec(memory_space=pltpu.VMEM)
    return pl.pallas_call(
        body, name=name, in_specs=[vm] * (n_p + len(taps)), out_specs=[vm] * 2,
        out_shape=[jax.ShapeDtypeStruct((n_p, d), F32), jax.ShapeDtypeStruct((n_blocks, sum(rows), cb), F32)],
    )(*parts, *taps)


def small_update(name, part_g, tap_g, w_a, m_a, v_a, w_b, m_b, v_b):
    nd, na, d = part_g.shape
    nb, cb = w_b.shape

    def body(pg_ref, tg_ref, wa_ref, ma_ref, va_ref, wb_ref, mb_ref, vb_ref,
             ga_out, da_out, ma_out, va_out, gb_out, db_out, mb_out, vb_out, loss_out):
        ga = _sum_blocks(pg_ref)
        delta, m_new, v_new = _adamw_math(wa_ref[...], ga, ma_ref[...], va_ref[...])
        ga_out[...] = ga
        da_out[...] = delta
        ma_out[...] = m_new
        va_out[...] = v_new
        loss_out[...] = jnp.broadcast_to(jnp.sum(ga[na - 1:na, :], axis=1, keepdims=True), loss_out.shape)
        gb = _sum_blocks(tg_ref)
        delta, m_new, v_new = _adamw_math(wb_ref[...], gb, mb_ref[...], vb_ref[...])
        gb_out[...] = gb
        db_out[...] = delta
        mb_out[...] = m_new
        vb_out[...] = v_new

    oa, ob = jax.ShapeDtypeStruct((na, d), F32), jax.ShapeDtypeStruct((nb, cb), F32)
    vm = pl.BlockSpec(memory_space=pltpu.VMEM)
    return pl.pallas_call(
        body, name=name, in_specs=[vm] * 8, out_specs=[vm] * 9,
        out_shape=[oa] * 4 + [ob] * 4 + [jax.ShapeDtypeStruct((1, LANES), F32)],
        compiler_params=pltpu.CompilerParams(vmem_limit_bytes=_vmem_limit(_nbytes(part_g.shape, F32))),
    )(part_g, tap_g, w_a, m_a, v_a, w_b, m_b, v_b)


BIG = ("cf_w_pw1", "cf_w_pw2", "sc_w_in", "sc_w_out", "mlp_w1", "mlp_w2", "ple_w_proj", "ple_w_gate")
COL_SHARDED = ("cf_w_pw1", "sc_w_in", "mlp_w1", "ple_w_proj")
WEIGHT_ORDER = ("norm_mix", "norm_mlp", "norm_ple", "cf_w_pw1", "cf_b_pw1", "cf_w_dw", "cf_b_dw", "cf_norm",
                "cf_w_pw2", "cf_b_pw2", "sc_w_in", "sc_w_conv", "sc_w_out", "mlp_w1", "mlp_w2", "ple_w_proj",
                "ple_w_gate", "norm_final")
SMALL_ROWS = (("norm_mix", 4), ("norm_mlp", 4), ("norm_ple", 4), ("cf_b_pw1", 4), ("cf_b_dw", 2), ("cf_norm", 2),
              ("cf_b_pw2", 2), ("norm_final", 1))


def _layer_weights(i):
    mixer = (("cf_w_pw1", i // 2), ("cf_w_pw2", i // 2)) if i % 2 == 0 else (("sc_w_in", i // 2), ("sc_w_out", i // 2))
    return mixer + (("mlp_w1", i), ("mlp_w2", i), ("ple_w_proj", i), ("ple_w_gate", i))


def _pad_rows(a, rows):
    return jnp.pad(a, ((0, 0), (0, rows - a.shape[1]), (0, 0)))


def _pack_taps(cf, sc):
    c = cf.shape[2]
    return jnp.concatenate([_pad_rows(cf, CONV_A_HALO).reshape(-1, c), _pad_rows(sc, CONV_B_HALO).reshape(-1, c)], axis=0)


def _unpack_taps(t, n_cf):
    c = t.shape[1]
    cf = t[:n_cf * CONV_A_HALO].reshape(n_cf, CONV_A_HALO, c)[:, :CONV_A_TAPS]
    sc = t[n_cf * CONV_A_HALO:].reshape(-1, CONV_B_HALO, c)[:, :CONV_B_TAPS]
    return cf, sc


def _pack_small(vals, d):
    return jnp.concatenate([vals[k].reshape(-1, d) for k, _ in SMALL_ROWS] + [jnp.zeros((1, d), F32)], axis=0)


def _unpack_small(a, shapes):
    out, r = {}, 0
    for k, n in SMALL_ROWS:
        out[k] = a[r:r + n].reshape(shapes[k])
        r += n
    return out


def kernel(x, p, norm_mix, norm_mlp, norm_ple, cf_w_pw1, cf_b_pw1, cf_w_dw, cf_b_dw, cf_norm, cf_w_pw2, cf_b_pw2, sc_w_in, sc_w_conv, sc_w_out, mlp_w1, mlp_w2, ple_w_proj, ple_w_gate, norm_final, loss_target, m_norm_mix, m_norm_mlp, m_norm_ple, m_cf_w_pw1, m_cf_b_pw1, m_cf_w_dw, m_cf_b_dw, m_cf_norm, m_cf_w_pw2, m_cf_b_pw2, m_sc_w_in, m_sc_w_conv, m_sc_w_out, m_mlp_w1, m_mlp_w2, m_ple_w_proj, m_ple_w_gate, m_norm_final, v_norm_mix, v_norm_mlp, v_norm_ple, v_cf_w_pw1, v_cf_b_pw1, v_cf_w_dw, v_cf_b_dw, v_cf_norm, v_cf_w_pw2, v_cf_b_pw2, v_sc_w_in, v_sc_w_conv, v_sc_w_out, v_mlp_w1, v_mlp_w2, v_ple_w_proj, v_ple_w_gate, v_norm_final):
    w = dict(norm_mix=norm_mix, norm_mlp=norm_mlp, norm_ple=norm_ple, cf_w_pw1=cf_w_pw1, cf_b_pw1=cf_b_pw1,
             cf_w_dw=cf_w_dw, cf_b_dw=cf_b_dw, cf_norm=cf_norm, cf_w_pw2=cf_w_pw2, cf_b_pw2=cf_b_pw2,
             sc_w_in=sc_w_in, sc_w_conv=sc_w_conv, sc_w_out=sc_w_out, mlp_w1=mlp_w1, mlp_w2=mlp_w2,
             ple_w_proj=ple_w_proj, ple_w_gate=ple_w_gate, norm_final=norm_final)
    m = dict(norm_mix=m_norm_mix, norm_mlp=m_norm_mlp, norm_ple=m_norm_ple, cf_w_pw1=m_cf_w_pw1, cf_b_pw1=m_cf_b_pw1,
             cf_w_dw=m_cf_w_dw, cf_b_dw=m_cf_b_dw, cf_norm=m_cf_norm, cf_w_pw2=m_cf_w_pw2, cf_b_pw2=m_cf_b_pw2,
             sc_w_in=m_sc_w_in, sc_w_conv=m_sc_w_conv, sc_w_out=m_sc_w_out, mlp_w1=m_mlp_w1, mlp_w2=m_mlp_w2,
             ple_w_proj=m_ple_w_proj, ple_w_gate=m_ple_w_gate, norm_final=m_norm_final)
    v = dict(norm_mix=v_norm_mix, norm_mlp=v_norm_mlp, norm_ple=v_norm_ple, cf_w_pw1=v_cf_w_pw1, cf_b_pw1=v_cf_b_pw1,
             cf_w_dw=v_cf_w_dw, cf_b_dw=v_cf_b_dw, cf_norm=v_cf_norm, cf_w_pw2=v_cf_w_pw2, cf_b_pw2=v_cf_b_pw2,
             sc_w_in=v_sc_w_in, sc_w_conv=v_sc_w_conv, sc_w_out=v_sc_w_out, mlp_w1=v_mlp_w1, mlp_w2=v_mlp_w2,
             ple_w_proj=v_ple_w_proj, ple_w_gate=v_ple_w_gate, norm_final=v_norm_final)
    depth, d = norm_mix.shape
    n_cf = cf_w_dw.shape[0]

    taps_w = _pack_taps(cf_w_dw, sc_w_conv)
    shards, gathered = {}, {}
    ids = iter(range(5 * depth))
    for i in range(depth):
        names = _layer_weights(i)
        groups = [names[:1], names[1:2], names[2:3], names[3:]] if i == 0 else [names[:2], names[2:]]
        for n_group, group in enumerate(groups):
            with_taps = i == 0 and n_group == 1
            for k, _ in group:
                if k not in shards:
                    shards[k] = cast_bf16(f"cast_{k}", w[k])
            srcs = [shards[k] for k, _ in group] + ([taps_w[None]] if with_taps else [])
            got = sc_gather(f"gather_{i}_{n_group}", next(ids), srcs, [l for _, l in group] + ([0] if with_taps else []))
            for (k, l), g in zip(group, got):
                gathered[(k, l)] = g if k in COL_SHARDED else g.reshape(-1, g.shape[3])
            if with_taps:
                taps_full = jnp.transpose(got[-1][:, 0], (1, 0, 2)).reshape(taps_w.shape[0], d)
    conv_w = {"cf": taps_full[:n_cf * CONV_A_HALO].reshape(n_cf, CONV_A_HALO, d),
              "sc": taps_full[n_cf * CONV_A_HALO:].reshape(-1, CONV_B_HALO, d)}

    def get_w(k, l, after=None):
        return gathered[(k, l)]

    received, waiting = {}, {}
    last_group = list(_layer_weights(0)[:2])

    def put_grads(grads):
        names = list(grads)
        if names[0][0] in ("cf_w_pw2", "sc_w_out") and set(names) != set(last_group):
            waiting.update(grads)
            return
        grads = {**waiting, **grads}
        waiting.clear()
        names = list(grads)
        got = sc_exchange(f"grad_exchange_{names[-1][0]}_{names[-1][1]}", next(ids), [grads[n] for n in names],
                          [jax.ShapeDtypeStruct(grads[n].shape, BF16) for n in names],
                          [_slice_of] * len(names), [_slice_of] * len(names))
        received.update(zip(names, got))

    small = {k: w[k] for k, _ in SMALL_ROWS}
    small["norm_final"] = norm_final[None]
    loss_part, grad_x, sg = _local_step(x[0], p[:, 0], loss_target[0], small, get_w, conv_w, put_grads)

    out = {k: None for k in BIG}
    previous = None
    for (k, l), recv in received.items():
        if (k, l) not in last_group:
            out[k] = adamw_layer(f"adamw_{k}_{l}", recv, w[k], m[k], v[k], l, out[k], after=previous)
            previous = out[k][1]
    updated_first = [out[k][0] for k in BIG if out[k] is not None and k not in [n for n, _ in last_group]]

    parts = []
    for k, n in SMALL_ROWS:
        for g in sg[k]:
            parts += [g[:, :d], g[:, d:]] if g.shape[1] == 2 * d else [g]
    parts.append(loss_part)
    sums, tap_slices = pack_small_grads("pack_small_grads", parts, sg["cf_w_dw"] + sg["sc_w_conv"], NDEV)
    part_all, tap_mine = _exchange(
        "small_exchange", [sums[None], tap_slices],
        [jax.ShapeDtypeStruct((NDEV, 1) + sums.shape, F32), jax.ShapeDtypeStruct(tap_slices.shape, F32)],
        [_gather_src(0), _slice_of], [_gather_dst, _slice_of], after=updated_first)
    for k, l in last_group:
        out[k] = adamw_layer(f"adamw_{k}_{l}", received[(k, l)], w[k], m[k], v[k], l, out[k], after=part_all)
    sm = small_update("small_update", part_all[:, 0], tap_mine,
                      _pack_small(w, d), _pack_small(m, d), _pack_small(v, d),
                      taps_w, _pack_taps(m["cf_w_dw"], m["sc_w_conv"]), _pack_taps(v["cf_w_dw"], v["sc_w_conv"]))
    shapes = {k: w[k].shape for k, _ in SMALL_ROWS}
    for t in range(4):
        un = _unpack_small(sm[t], shapes)
        cf_t, sc_t = _unpack_taps(sm[4 + t], n_cf)
        for k in un:
            out.setdefault(k, [None] * 4)[t] = un[k]
        out.setdefault("cf_w_dw", [None] * 4)[t] = cf_t
        out.setdefault("sc_w_conv", [None] * 4)[t] = sc_t
    loss = sm[8][0, 0]

    return (loss, grad_x[None], *[out[k][0] for k in WEIGHT_ORDER], *[out[k][1] for k in WEIGHT_ORDER],
            *[out[k][2] for k in WEIGHT_ORDER], *[out[k][3] for k in WEIGHT_ORDER])
```

```python
import jax
import jax.numpy as jnp
from jax import lax
from jax.experimental import pallas as pl
from jax.experimental.pallas import tpu as pltpu
from jax.experimental.pallas import tpu_sc as plsc

F32 = jnp.float32
BF16 = jnp.bfloat16
EPS = 1e-6
NDEV = 8
N_PEERS = NDEV - 1
MESH = pl.DeviceIdType.MESH

ADAM_LR = 0.001
ADAM_B1 = 0.9
ADAM_B2 = 0.999
ADAM_EPS = 1e-08
ADAM_WD = 0.01
ADAM_STEP = 10

V7X_VMEM_BYTES = 64 * 1024 * 1024
VMEM_LIMIT_MAX = 56 * 1024 * 1024
SUBLANES = 8
LANES = 128
CONV_A_TAPS = 31
CONV_A_HALO = 32
CONV_B_TAPS = 3
CONV_B_HALO = 8


def _nbytes(shape, dtype):
    n = 1
    for s in shape:
        if s is not None:
            n *= s
    return n * jnp.dtype(dtype).itemsize


def _vmem_limit(block_bytes, scratch_bytes=0):
    need = 2 * block_bytes + scratch_bytes
    return int(min(VMEM_LIMIT_MAX, max(32 * 1024 * 1024, need + need // 2 + (4 << 20))))


def _params(sem, block_bytes, scratch_bytes=0):
    return pltpu.CompilerParams(dimension_semantics=sem, vmem_limit_bytes=_vmem_limit(block_bytes, scratch_bytes))


_DIMS = {
    "nn": (((1,), (0,)), ((), ())),
    "nt": (((1,), (1,)), ((), ())),
    "tn": (((0,), (0,)), ((), ())),
}


def _mm(name, dims, grid, acc_shape, a, a_spec, b, b_spec, extras, outs, epi):
    ni, nj, nk = grid
    n_ex, n_out = len(extras), len(outs)
    dn = _DIMS[dims]
    b_sub = [s for s in b_spec.block_shape if s is not None]
    n_sub = b_sub[0] if len(b_sub) == 3 else 1

    def body(*refs):
        a_ref, b_ref = refs[0], refs[1]
        ex_refs = refs[2:2 + n_ex]
        out_refs = refs[2 + n_ex:2 + n_ex + n_out]
        if n_sub == 1:
            d = lax.dot_general(a_ref[...].astype(BF16), b_ref[...].astype(BF16), dn, preferred_element_type=F32)
        else:
            w_sub = a_ref.shape[1] // n_sub
            d = None
            for s in range(n_sub):
                part = lax.dot_general(a_ref[:, pl.ds(s * w_sub, w_sub)].astype(BF16), b_ref[s].astype(BF16), dn,
                                       preferred_element_type=F32)
                d = part if d is None else d + part

        def finish(acc):
            res = epi(acc, *[r[...] for r in ex_refs])
            for o_ref, r in zip(out_refs, res):
                o_ref[...] = r.astype(o_ref.dtype)

        if nk == 1:
            finish(d)
        else:
            acc_ref = refs[2 + n_ex + n_out]
            k = pl.program_id(2)

            @pl.when(k == 0)
            def _():
                acc_ref[...] = d

            @pl.when(jnp.logical_and(k > 0, k < nk - 1))
            def _():
                acc_ref[...] += d

            @pl.when(k == nk - 1)
            def _():
                finish(acc_ref[...] + d)

    blk = _nbytes(a_spec.block_shape, a.dtype) + _nbytes(b_spec.block_shape, b.dtype)
    for arr, spec in list(extras) + list(outs):
        blk += _nbytes(spec.block_shape, arr.dtype)
    acc_bytes = _nbytes(acc_shape, F32)
    scratch = [pltpu.VMEM(acc_shape, F32)] if nk > 1 else []
    return pl.pallas_call(
        body,
        name=name,
        grid=grid,
        in_specs=[a_spec, b_spec] + [s for _, s in extras],
        out_specs=[s for _, s in outs],
        out_shape=[o for o, _ in outs],
        scratch_shapes=scratch,
        compiler_params=_params(("parallel", "parallel", "arbitrary"), blk, 3 * acc_bytes),
    )(a, b, *[e for e, _ in extras])


def _tile(n, pref):
    if n <= pref:
        return n
    t = pref - pref % LANES
    while t > LANES and n % t:
        t -= LANES
    assert n % t == 0, (n, pref)
    return t


def _row_tile(n, pref):
    if n <= pref:
        return n
    t = max(SUBLANES, pref - pref % SUBLANES)
    while t > SUBLANES and n % t:
        t -= SUBLANES
    assert n % t == 0, (n, pref)
    return t


def _id_epi(acc):
    return (acc,)


def mm_x_wcol(name, x, w, layer, extras=(), outs_dtypes=(F32,), epi=_id_epi, tm=1024, tn=1024):
    m, kdim = x.shape
    c = w.shape[3]
    n = NDEV * c
    tm, tn = _tile(m, tm), _tile(c, tn)
    tk = _tile(kdim, 2048)
    grid = (m // tm, n // tn, kdim // tk)
    per = c // tn
    a_spec = pl.BlockSpec((tm, tk), lambda i, j, k: (i, k))
    b_spec = pl.BlockSpec((None, None, tk, tn), lambda i, j, k: (j // per, layer, k, j % per))
    ex = [(e, _ex_spec(e, kind, tm, tn)) for e, kind in extras]
    o_spec = pl.BlockSpec((tm, tn), lambda i, j, k: (i, j))
    outs = [(jax.ShapeDtypeStruct((m, n), dt), o_spec) for dt in outs_dtypes]
    return _mm(name, "nn", grid, (tm, tn), x, a_spec, w, b_spec, ex, outs, epi)


def mm_x_wrow(name, x, w, extras=(), outs_dtypes=(F32,), epi=_id_epi, tm=1024, tn=512, tk=2048):
    m, kdim = x.shape
    n = w.shape[1]
    assert kdim == w.shape[0]
    tm, tn = _tile(m, tm), _tile(n, tn)
    tk = _tile(kdim, tk)
    grid = (m // tm, n // tn, kdim // tk)
    a_spec = pl.BlockSpec((tm, tk), lambda i, j, k: (i, k))
    b_spec = pl.BlockSpec((tk, tn), lambda i, j, k: (k, j))
    ex = [(e, _ex_spec(e, kind, tm, tn)) for e, kind in extras]
    o_spec = pl.BlockSpec((tm, tn), lambda i, j, k: (i, j))
    outs = [(jax.ShapeDtypeStruct((m, n), dt), o_spec) for dt in outs_dtypes]
    return _mm(name, "nn", grid, (tm, tn), x, a_spec, w, b_spec, ex, outs, epi)


def mm_dy_wcol_t(name, dy, w, layer, extras=(), outs_dtypes=(F32,), epi=_id_epi, tm=512, tn=512):
    m, n = dy.shape
    nd, kdim, c = w.shape[0], w.shape[2], w.shape[3]
    assert n == nd * c
    tm, tn = _tile(m, tm), _tile(kdim, tn)
    n_sub = nd
    grid = (m // tm, kdim // tn, nd // n_sub)
    a_spec = pl.BlockSpec((tm, n_sub * c), lambda i, j, k: (i, k))
    if n_sub > 1:
        b_spec = pl.BlockSpec((n_sub, None, tn, c), lambda i, j, k: (k, layer, j, 0))
    else:
        b_spec = pl.BlockSpec((None, None, tn, c), lambda i, j, k: (k, layer, j, 0))
    ex = [(e, _ex_spec(e, kind, tm, tn)) for e, kind in extras]
    o_spec = pl.BlockSpec((tm, tn), lambda i, j, k: (i, j))
    outs = [(jax.ShapeDtypeStruct((m, kdim), dt), o_spec) for dt in outs_dtypes]
    return _mm(name, "nt", grid, (tm, tn), dy, a_spec, w, b_spec, ex, outs, epi)


def mm_dy_wrow_t(name, dy, w, extras=(), outs_dtypes=(F32,), epi=_id_epi, tm=1024, tn=1024):
    m, n = dy.shape
    kdim = w.shape[0]
    assert n == w.shape[1]
    tm, tn = _tile(m, tm), _tile(kdim, tn)
    tk = _tile(n, 2048)
    grid = (m // tm, kdim // tn, n // tk)
    a_spec = pl.BlockSpec((tm, tk), lambda i, j, k: (i, k))
    b_spec = pl.BlockSpec((tn, tk), lambda i, j, k: (j, k))
    ex = [(e, _ex_spec(e, kind, tm, tn)) for e, kind in extras]
    o_spec = pl.BlockSpec((tm, tn), lambda i, j, k: (i, j))
    outs = [(jax.ShapeDtypeStruct((m, kdim), dt), o_spec) for dt in outs_dtypes]
    return _mm(name, "nt", grid, (tm, tn), dy, a_spec, w, b_spec, ex, outs, epi)


def mm_xt_dy(name, x, dy, col_shards, tm=1024, tn=1024):
    m, kdim = x.shape
    n = dy.shape[1]
    tk = _tile(m, 4096)
    if col_shards:
        c = n // NDEV
        tm, tn = _tile(kdim, tm), _tile(c, tn)
        per = c // tn
        out = jax.ShapeDtypeStruct((NDEV, kdim, c), BF16)
        o_spec = pl.BlockSpec((None, tm, tn), lambda i, j, k: (j // per, i, j % per))
    else:
        tm, tn = _tile(kdim, tm), _tile(n, tn)
        out = jax.ShapeDtypeStruct((kdim, n), BF16)
        o_spec = pl.BlockSpec((tm, tn), lambda i, j, k: (i, j))
    grid = (kdim // tm, n // tn, m // tk)
    a_spec = pl.BlockSpec((tk, tm), lambda i, j, k: (k, i))
    b_spec = pl.BlockSpec((tk, tn), lambda i, j, k: (k, j))
    g = _mm(name, "tn", grid, (tm, tn), x, a_spec, dy, b_spec, [], [(out, o_spec)], _id_epi)[0]
    return g if col_shards else g.reshape(NDEV, kdim // NDEV, n)


def _ex_spec(e, kind, tm, tn):
    if kind == "tile":
        return pl.BlockSpec((tm, tn), lambda i, j, k: (i, j))
    if kind == "row":
        return pl.BlockSpec((1, tn), lambda i, j, k: (0, j))
    raise ValueError(kind)


def _rows_call(name, body, n_rows, ts, ins, outs, scratch=(), scratch_bytes=0):
    blk = sum(_nbytes(s.block_shape, a.dtype) for a, s in list(ins) + list(outs))
    return pl.pallas_call(
        body,
        name=name,
        grid=(n_rows // ts,),
        in_specs=[s for _, s in ins],
        out_specs=[s for _, s in outs],
        out_shape=[o for o, _ in outs],
        scratch_shapes=list(scratch),
        compiler_params=_params(("arbitrary",), blk, scratch_bytes + 4 * blk // 2),
    )(*[a for a, _ in ins])


def _blk(ts, d):
    return pl.BlockSpec((ts, d), lambda i: (i, 0))


def _full(shape):
    return pl.BlockSpec(shape, lambda i: tuple(0 for _ in shape))


def _rowsum8(v):
    t, d = v.shape
    return jnp.sum(v.reshape(t // SUBLANES, SUBLANES, d), axis=0)


def _accumulate(ref, val):
    @pl.when(pl.program_id(0) == 0)
    def _():
        ref[...] = val

    @pl.when(pl.program_id(0) > 0)
    def _():
        ref[...] += val


def _rstd(x):
    return lax.rsqrt(jnp.mean(x * x, axis=-1, keepdims=True) + EPS)


def _rms_bwd_math(dy, x, g):
    r = _rstd(x)
    gdy = dy * g
    c = jnp.mean(gdy * x, axis=-1, keepdims=True)
    dx = r * gdy - x * (r * r * r * c)
    return dx, dy * (x * r)


def rms_fwd(name, h, g, ts=512):
    s, d = h.shape
    ts = min(ts, s)

    def body(h_ref, g_ref, u_ref):
        x = h_ref[...]
        u_ref[...] = ((x * _rstd(x)) * g_ref[...]).astype(BF16)

    return _rows_call(name, body, s, ts, [(h, _blk(ts, d)), (g, _full((1, d)))],
                      [(jax.ShapeDtypeStruct((s, d), BF16), _blk(ts, d))])[0]


def rms_bwd(name, du, h, g, dres, ts=256):
    s, d = h.shape
    ts = min(ts, s)

    def body(du_ref, h_ref, g_ref, dres_ref, dh_ref, dhb_ref, dg_ref, cs_ref):
        dx, dg = _rms_bwd_math(du_ref[...], h_ref[...], g_ref[...])
        dh = dres_ref[...] + dx
        dh_ref[...] = dh
        dhb_ref[...] = dh.astype(BF16)
        _accumulate(dg_ref, _rowsum8(dg))
        _accumulate(cs_ref, _rowsum8(dh))

    return _rows_call(
        name, body, s, ts,
        [(du, _blk(ts, d)), (h, _blk(ts, d)), (g, _full((1, d))), (dres, _blk(ts, d))],
        [(jax.ShapeDtypeStruct((s, d), F32), _blk(ts, d)), (jax.ShapeDtypeStruct((s, d), BF16), _blk(ts, d)),
         (jax.ShapeDtypeStruct((SUBLANES, d), F32), _full((SUBLANES, d))),
         (jax.ShapeDtypeStruct((SUBLANES, d), F32), _full((SUBLANES, d)))])


def final_loss(name, h, g, target, ts=256):
    s, d = h.shape
    ts = min(ts, s)

    def body(h_ref, g_ref, t_ref, loss_ref, dh_ref, dg_ref):
        x = h_ref[...]
        gf = g_ref[...]
        y = (x * _rstd(x)) * gf
        err = y - t_ref[...]
        _accumulate(loss_ref, _rowsum8(err * err) * (0.5 / d))
        dx, dg = _rms_bwd_math(err * (1.0 / d), x, gf)
        dh_ref[...] = dx
        _accumulate(dg_ref, _rowsum8(dg))

    return _rows_call(
        name, body, s, ts,
        [(h, _blk(ts, d)), (g, _full((1, d))), (target, _blk(ts, d))],
        [(jax.ShapeDtypeStruct((SUBLANES, d), F32), _full((SUBLANES, d))),
         (jax.ShapeDtypeStruct((s, d), F32), _blk(ts, d)),
         (jax.ShapeDtypeStruct((SUBLANES, d), F32), _full((SUBLANES, d)))])


def ple_bwd_elem(name, dh, g, e, ts=512):
    s, d = dh.shape
    ts = min(ts, s)

    def body(dh_ref, g_ref, e_ref, de_ref, dgl_ref):
        dh_v, g_v = dh_ref[...], g_ref[...]
        de_ref[...] = (dh_v * g_v).astype(BF16)
        dgl_ref[...] = (dh_v * e_ref[...] * (g_v * (1.0 - g_v))).astype(BF16)

    return _rows_call(name, body, s, ts, [(dh, _blk(ts, d)), (g, _blk(ts, d)), (e, _blk(ts, d))],
                      [(jax.ShapeDtypeStruct((s, d), BF16), _blk(ts, d)),
                       (jax.ShapeDtypeStruct((s, d), BF16), _blk(ts, d))])


CONV_LANES = 256
CONV_ROWS = 64


def _lane_chunks(d, fn):
    lc = min(CONV_LANES, d)

    def lane_body(c, carry):
        fn(pl.ds(pl.multiple_of(c * lc, lc), lc))
        return carry

    lax.fori_loop(0, d // lc, lane_body, 0)


def _shifted_copies(buf, sh, lanes):
    rows = buf.shape[0] - SUBLANES
    for s in range(1, SUBLANES):
        sh[s, pl.ds(0, rows), :] = buf[pl.ds(s, rows), lanes]


def _window(buf, sh, lanes, start, rows):
    s = start % SUBLANES
    if s == 0:
        return buf[pl.ds(start, rows), lanes]
    return sh[s, pl.ds(start - s, rows), :]


def _prev_halo_spec(ts, halo, width):
    per = ts // halo
    return pl.BlockSpec((halo, width), lambda i: (jnp.maximum(i * per - 1, 0), 0))


def _next_halo_spec(ts, halo, width, n_rows):
    per = ts // halo
    last = n_rows // halo - 1
    return pl.BlockSpec((halo, width), lambda i: (jnp.minimum((i + 1) * per, last), 0))


def cf_fwd_mid(name, a, w_dw, b_dw, gn, ts=256):
    s, d2 = a.shape
    d = d2 // 2
    ts = min(ts, s)
    hl = CONV_A_HALO
    off = hl - (CONV_A_TAPS - 1)

    rc = min(CONV_ROWS, ts)

    def body(a_ref, ah_ref, w_ref, b_ref, gn_ref, v0_ref, v1_ref, v3_ref, buf, sh):
        first = pl.program_id(0) == 0
        halo = ah_ref[...]
        hv0 = halo[:, :d] * jax.nn.sigmoid(halo[:, d:])
        buf[pl.ds(0, hl), :] = jnp.where(first, 0.0, hv0)
        main = a_ref[...]
        v0 = main[:, :d] * jax.nn.sigmoid(main[:, d:])
        buf[pl.ds(hl, ts), :] = v0
        v0_ref[...] = v0

        def conv(lanes):
            _shifted_copies(buf, sh, lanes)
            for r0 in range(0, ts, rc):
                acc = jnp.zeros((rc, lanes.size), F32)
                for k in range(CONV_A_TAPS):
                    acc = acc + w_ref[pl.ds(k, 1), lanes] * _window(buf, sh, lanes, r0 + off + k, rc)
                v1_ref[pl.ds(r0, rc), lanes] = acc + b_ref[:, lanes]

        _lane_chunks(d, conv)
        v1 = v1_ref[...]
        v2 = (v1 * _rstd(v1)) * gn_ref[...]
        v3_ref[...] = (v2 * jax.nn.sigmoid(v2)).astype(BF16)

    return _rows_call(
        name, body, s, ts,
        [(a, _blk(ts, d2)), (a, _prev_halo_spec(ts, hl, d2)), (w_dw, _full(w_dw.shape)),
         (b_dw, _full((1, d))), (gn, _full((1, d)))],
        [(jax.ShapeDtypeStruct((s, d), F32), _blk(ts, d)), (jax.ShapeDtypeStruct((s, d), F32), _blk(ts, d)),
         (jax.ShapeDtypeStruct((s, d), BF16), _blk(ts, d))],
        scratch=[pltpu.VMEM((hl + ts, d), F32), pltpu.VMEM((SUBLANES, hl + ts, min(CONV_LANES, d)), F32)],
        scratch_bytes=_nbytes((hl + ts, d + SUBLANES * CONV_LANES), F32))


def cf_bwd_rows(name, dv3, v1, gn, ts=256):
    s, d = v1.shape
    ts = min(ts, s)

    def body(dv3_ref, v1_ref, gn_ref, dv1_ref, dgn_ref, db_ref):
        v1 = v1_ref[...]
        gn_v = gn_ref[...]
        v2 = (v1 * _rstd(v1)) * gn_v
        sg = jax.nn.sigmoid(v2)
        dv2 = dv3_ref[...] * (sg * (1.0 + v2 * (1.0 - sg)))
        dv1, dgn = _rms_bwd_math(dv2, v1, gn_v)
        dv1_ref[...] = dv1
        _accumulate(dgn_ref, _rowsum8(dgn))
        _accumulate(db_ref, _rowsum8(dv1))

    return _rows_call(
        name, body, s, ts, [(dv3, _blk(ts, d)), (v1, _blk(ts, d)), (gn, _full((1, d)))],
        [(jax.ShapeDtypeStruct((s, d), F32), _blk(ts, d)),
         (jax.ShapeDtypeStruct((SUBLANES, d), F32), _full((SUBLANES, d))),
         (jax.ShapeDtypeStruct((SUBLANES, d), F32), _full((SUBLANES, d)))])


def cf_bwd_conv(name, dv1, v0, a, w_dw, ts=256):
    s, d = dv1.shape
    ts = min(ts, s)
    hl = CONV_A_HALO
    taps = CONV_A_TAPS
    off = hl - (taps - 1)
    last_blk = s // ts - 1

    rc = min(CONV_ROWS, ts)

    def body(dv1_ref, dv1n_ref, v0_ref, v0p_ref, a_ref, w_ref, da_ref, dw_ref, db_ref,
             dbuf, vbuf, dv0_buf, dw_acc, dsh, vsh):
        i = pl.program_id(0)
        dbuf[pl.ds(0, ts), :] = dv1_ref[...]
        dbuf[pl.ds(ts, hl), :] = jnp.where(i == last_blk, 0.0, dv1n_ref[...])
        vbuf[pl.ds(0, hl), :] = jnp.where(i == 0, 0.0, v0p_ref[...])
        vbuf[pl.ds(hl, ts), :] = v0_ref[...]

        @pl.when(i == 0)
        def _():
            dw_acc[...] = jnp.zeros_like(dw_acc)

        def conv_t(lanes):
            _shifted_copies(dbuf, dsh, lanes)
            _shifted_copies(vbuf, vsh, lanes)
            for r0 in range(0, ts, rc):
                g = dbuf[pl.ds(r0, rc), lanes]
                acc = jnp.zeros((rc, lanes.size), F32)
                for k in range(taps):
                    acc = acc + w_ref[pl.ds(k, 1), lanes] * _window(dbuf, dsh, lanes, r0 + taps - 1 - k, rc)
                    prod = g * _window(vbuf, vsh, lanes, r0 + off + k, rc)
                    dw_acc[pl.ds(k * SUBLANES, SUBLANES), lanes] += _rowsum8(prod)
                dv0_buf[pl.ds(r0, rc), lanes] = acc

        _lane_chunks(d, conv_t)
        dv0 = dv0_buf[...]
        av = a_ref[...]
        val, sg = av[:, :d], jax.nn.sigmoid(av[:, d:])
        dval = dv0 * sg
        dgate = dv0 * val * (sg * (1.0 - sg))
        da_ref[:, :d] = dval.astype(BF16)
        da_ref[:, d:] = dgate.astype(BF16)
        _accumulate(db_ref.at[:, pl.ds(0, d)], _rowsum8(dval))
        _accumulate(db_ref.at[:, pl.ds(d, d)], _rowsum8(dgate))

        @pl.when(i == last_blk)
        def _():
            dw_ref[...] = jnp.sum(dw_acc[...].reshape(hl, SUBLANES, d), axis=1)

    lc = min(CONV_LANES, d)
    scratch = [pltpu.VMEM((ts + hl, d), F32), pltpu.VMEM((hl + ts, d), F32), pltpu.VMEM((ts, d), F32),
               pltpu.VMEM((hl * SUBLANES, d), F32), pltpu.VMEM((SUBLANES, ts + hl, lc), F32),
               pltpu.VMEM((SUBLANES, hl + ts, lc), F32)]
    sbytes = _nbytes((3 * ts + 2 * hl + hl * SUBLANES, d), F32) + 2 * _nbytes((SUBLANES, ts + hl, lc), F32)
    return _rows_call(
        name, body, s, ts,
        [(dv1, _blk(ts, d)), (dv1, _next_halo_spec(ts, hl, d, s)), (v0, _blk(ts, d)), (v0, _prev_halo_spec(ts, hl, d)),
         (a, _blk(ts, 2 * d)), (w_dw, _full(w_dw.shape))],
        [(jax.ShapeDtypeStruct((s, 2 * d), BF16), _blk(ts, 2 * d)),
         (jax.ShapeDtypeStruct((hl, d), F32), _full((hl, d))),
         (jax.ShapeDtypeStruct((SUBLANES, 2 * d), F32), _full((SUBLANES, 2 * d)))],
        scratch=scratch, scratch_bytes=sbytes)


def sc_fwd_mid(name, bcv, w_conv, ts=256):
    s, d3 = bcv.shape
    d = d3 // 3
    ts = min(ts, s)
    hl = CONV_B_HALO
    off = hl - (CONV_B_TAPS - 1)

    def body(x_ref, xp_ref, w_ref, y_ref, buf):
        hp = xp_ref[...]
        buf[pl.ds(0, hl), :] = jnp.where(pl.program_id(0) == 0, 0.0, hp[:, d:2 * d] * hp[:, 2 * d:])
        buf[pl.ds(hl, ts), :] = x_ref[:, d:2 * d] * x_ref[:, 2 * d:]
        cc = jnp.zeros((ts, d), F32)
        for k in range(CONV_B_TAPS):
            cc = cc + w_ref[pl.ds(k, 1), :] * buf[pl.ds(off + k, ts), :]
        y_ref[...] = (x_ref[:, :d] * cc).astype(BF16)

    return _rows_call(
        name, body, s, ts,
        [(bcv, _blk(ts, d3)), (bcv, _prev_halo_spec(ts, hl, d3)), (w_conv, _full(w_conv.shape))],
        [(jax.ShapeDtypeStruct((s, d), BF16), _blk(ts, d))],
        scratch=[pltpu.VMEM((hl + ts, d), F32)], scratch_bytes=_nbytes((hl + ts, d), F32))[0]


def sc_bwd_mid(name, dy, bcv, w_conv, ts=256):
    s, d3 = bcv.shape
    d = d3 // 3
    ts = min(ts, s)
    hl = CONV_B_HALO
    taps = CONV_B_TAPS
    off = hl - (taps - 1)
    last_blk = s // ts - 1

    def body(dy_ref, dyn_ref, x_ref, xp_ref, xn_ref, w_ref, dx_ref, dw_ref, cvbuf, dbuf, dw_acc):
        i = pl.program_id(0)
        hp = xp_ref[...]
        cvbuf[pl.ds(0, hl), :] = jnp.where(i == 0, 0.0, hp[:, d:2 * d] * hp[:, 2 * d:])
        gb, gc, v = x_ref[:, :d], x_ref[:, d:2 * d], x_ref[:, 2 * d:]
        cvbuf[pl.ds(hl, ts), :] = gc * v
        dy_v = dy_ref[...]
        dcc = dy_v * gb
        dbuf[pl.ds(0, ts), :] = dcc
        dbuf[pl.ds(ts, hl), :] = jnp.where(i == last_blk, 0.0, dyn_ref[...] * xn_ref[:, :d])

        @pl.when(i == 0)
        def _():
            dw_acc[...] = jnp.zeros_like(dw_acc)

        cc = jnp.zeros((ts, d), F32)
        dcv = jnp.zeros((ts, d), F32)
        for k in range(taps):
            win = cvbuf[pl.ds(off + k, ts), :]
            cc = cc + w_ref[pl.ds(k, 1), :] * win
            dcv = dcv + w_ref[pl.ds(k, 1), :] * dbuf[pl.ds(taps - 1 - k, ts), :]
            dw_acc[pl.ds(k * SUBLANES, SUBLANES), :] += _rowsum8(dcc * win)
        dx_ref[:, :d] = (dy_v * cc).astype(BF16)
        dx_ref[:, d:2 * d] = (dcv * v).astype(BF16)
        dx_ref[:, 2 * d:] = (dcv * gc).astype(BF16)

        @pl.when(i == last_blk)
        def _():
            dw_ref[...] = jnp.sum(dw_acc[...].reshape(hl, SUBLANES, d), axis=1)

    scratch = [pltpu.VMEM((hl + ts, d), F32), pltpu.VMEM((ts + hl, d), F32), pltpu.VMEM((hl * SUBLANES, d), F32)]
    sbytes = _nbytes((2 * ts + 2 * hl + hl * SUBLANES, d), F32)
    return _rows_call(
        name, body, s, ts,
        [(dy, _blk(ts, d)), (dy, _next_halo_spec(ts, hl, d, s)), (bcv, _blk(ts, d3)), (bcv, _prev_halo_spec(ts, hl, d3)),
         (bcv, _next_halo_spec(ts, hl, d3, s)), (w_conv, _full(w_conv.shape))],
        [(jax.ShapeDtypeStruct((s, d3), BF16), _blk(ts, d3)), (jax.ShapeDtypeStruct((hl, d), F32), _full((hl, d)))],
        scratch=scratch, scratch_bytes=sbytes)


def _row(a, i):
    return lax.slice_in_dim(a, i, i + 1, axis=0)


def _local_step(x, p, target, small, get_w, conv_w, put_grads):
    depth = p.shape[0]
    acts = []
    h = x
    for i in range(depth):
        j = i // 2
        act = {"h": h}
        u = rms_fwd(f"rms_mix_{i}", h, _row(small["norm_mix"], i))
        act["u"] = u
        if i % 2 == 0:
            a = mm_x_wcol(f"cf_pw1_{i}", u, get_w("cf_w_pw1", j, u), 0, extras=[(_row(small["cf_b_pw1"], j), "row")],
                          epi=lambda acc, b: (acc + b,))[0]
            v0, v1, v3 = cf_fwd_mid(f"cf_mid_{i}", a, conv_w["cf"][j], _row(small["cf_b_dw"], j), _row(small["cf_norm"], j))
            act.update(a=a, v0=v0, v1=v1, v3=v3)
            h1 = mm_x_wrow(f"cf_pw2_{i}", v3, get_w("cf_w_pw2", j, v3),
                           extras=[(_row(small["cf_b_pw2"], j), "row"), (h, "tile")],
                           epi=lambda acc, b, res: (res + (acc + b),), tn=1024)[0]
        else:
            bcv = mm_x_wcol(f"sc_in_{i}", u, get_w("sc_w_in", j, u), 0, tn=768)[0]
            y = sc_fwd_mid(f"sc_mid_{i}", bcv, conv_w["sc"][j])
            act.update(bcv=bcv, y=y)
            h1 = mm_x_wrow(f"sc_out_{i}", y, get_w("sc_w_out", j, y), extras=[(h, "tile")],
                           epi=lambda acc, res: (res + acc,), tn=1024)[0]
        act["h1"] = h1
        u2 = rms_fwd(f"rms_mlp_{i}", h1, _row(small["norm_mlp"], i))
        z, hd = mm_x_wcol(f"mlp_w1_{i}", u2, get_w("mlp_w1", i, u2), 0, outs_dtypes=(F32, BF16),
                          epi=lambda acc: (acc, jnp.square(jnp.maximum(acc, 0.0))))
        h2 = mm_x_wrow(f"mlp_w2_{i}", hd, get_w("mlp_w2", i, hd), extras=[(h1, "tile")],
                       epi=lambda acc, res: (res + acc,), tm=512, tn=512, tk=8192)[0]
        act.update(u2=u2, z=z, hd=hd, h2=h2)
        n3 = rms_fwd(f"rms_ple_{i}", h2, _row(small["norm_ple"], i))
        e = mm_x_wcol(f"ple_proj_{i}", p[i], get_w("ple_w_proj", i, n3), 0)[0]

        def ple_epi(acc, e_t, res):
            g_t = jax.nn.sigmoid(acc)
            return g_t, res + g_t * e_t

        g, h3 = mm_x_wrow(f"ple_gate_{i}", n3, get_w("ple_w_gate", i, e), extras=[(e, "tile"), (h2, "tile")],
                          outs_dtypes=(F32, F32), epi=ple_epi)
        act.update(n3=n3, e=e, g=g)
        acts.append(act)
        h = h3

    loss_part, dh, dg_final = final_loss("final_loss", h, small["norm_final"], target)
    sg = {k: [None] * small[k].shape[0] for k in small if k != "norm_final"}
    sg["norm_final"] = [dg_final]
    sg["cf_w_dw"] = [None] * conv_w["cf"].shape[0]
    sg["sc_w_conv"] = [None] * conv_w["sc"].shape[0]

    for i in reversed(range(depth)):
        j = i // 2
        act = acts[i]
        de, dgl = ple_bwd_elem(f"ple_bwd_{i}", dh, act["g"], act["e"])
        g_proj = mm_xt_dy(f"d_ple_proj_{i}", p[i], de, True)
        g_gate = mm_xt_dy(f"d_ple_gate_{i}", act["n3"], dgl, False)
        dn3 = mm_dy_wrow_t(f"dn3_{i}", dgl, get_w("ple_w_gate", i))[0]
        dh2, dh2b, sg["norm_ple"][i], _ = rms_bwd(f"rms_ple_bwd_{i}", dn3, act["h2"], _row(small["norm_ple"], i), dh)
        g_w2 = mm_xt_dy(f"d_mlp_w2_{i}", act["hd"], dh2b, False)
        dz = mm_dy_wrow_t(f"dz_{i}", dh2b, get_w("mlp_w2", i), extras=[(act["z"], "tile")], outs_dtypes=(BF16,),
                          epi=lambda acc, z_t: (acc * (2.0 * jnp.maximum(z_t, 0.0)),))[0]
        g_w1 = mm_xt_dy(f"d_mlp_w1_{i}", act["u2"], dz, True)
        put_grads({("ple_w_proj", i): g_proj, ("ple_w_gate", i): g_gate, ("mlp_w2", i): g_w2, ("mlp_w1", i): g_w1})
        du2 = mm_dy_wcol_t(f"du2_{i}", dz, get_w("mlp_w1", i), 0)[0]
        dh1, dh1b, sg["norm_mlp"][i], cs1 = rms_bwd(f"rms_mlp_bwd_{i}", du2, act["h1"], _row(small["norm_mlp"], i), dh2)
        if i % 2 == 0:
            g_out = mm_xt_dy(f"d_cf_pw2_{i}", act["v3"], dh1b, False)
            sg["cf_b_pw2"][j] = cs1
            dv3 = mm_dy_wrow_t(f"dv3_{i}", dh1b, get_w("cf_w_pw2", j))[0]
            dv1, sg["cf_norm"][j], sg["cf_b_dw"][j] = cf_bwd_rows(f"cf_bwd_rows_{i}", dv3, act["v1"], _row(small["cf_norm"], j))
            da, sg["cf_w_dw"][j], sg["cf_b_pw1"][j] = cf_bwd_conv(f"cf_bwd_conv_{i}", dv1, act["v0"], act["a"], conv_w["cf"][j])
            g_in = mm_xt_dy(f"d_cf_pw1_{i}", act["u"], da, True)
            put_grads({("cf_w_pw2", j): g_out, ("cf_w_pw1", j): g_in})
            du = mm_dy_wcol_t(f"du_{i}", da, get_w("cf_w_pw1", j), 0)[0]
        else:
            g_out = mm_xt_dy(f"d_sc_out_{i}", act["y"], dh1b, False)
            dy = mm_dy_wrow_t(f"dy_{i}", dh1b, get_w("sc_w_out", j))[0]
            dbcv, sg["sc_w_conv"][j] = sc_bwd_mid(f"sc_bwd_mid_{i}", dy, act["bcv"], conv_w["sc"][j])
            g_in = mm_xt_dy(f"d_sc_in_{i}", act["u"], dbcv, True, tn=768)
            put_grads({("sc_w_out", j): g_out, ("sc_w_in", j): g_in})
            du = mm_dy_wcol_t(f"du_{i}", dbcv, get_w("sc_w_in", j), 0)[0]
        dh, _, sg["norm_mix"][i], _ = rms_bwd(f"rms_mix_bwd_{i}", du, act["h"], _row(small["norm_mix"], i), dh1)
    return loss_part, dh, sg


def _me_and_peers():
    x, y, c = lax.axis_index("x"), lax.axis_index("y"), lax.axis_index("c")
    me = 4 * x + 2 * y + c
    peers = []
    for q in range(1, NDEV):
        px = 1 - x if q & 4 else x
        py = 1 - y if q & 2 else y
        pc = 1 - c if q & 1 else c
        peers.append(((px, py, pc), 4 * px + 2 * py + pc))
    return me, peers


def _exchange(name, srcs, out_shapes, src_fns, dst_fns, after=()):
    n = len(srcs)
    n_after = len(after)

    def body(*refs):
        ins, outs = refs[:n], refs[n + n_after:2 * n + n_after]
        send_sems, recv_sems, local_sems = refs[2 * n + n_after:]
        me, peers = _me_and_peers()
        local, remote = [], []
        for k in range(n):
            cp = pltpu.make_async_copy(src_fns[k](ins[k], me), dst_fns[k](outs[k], me), local_sems.at[k])
            cp.start()
            local.append(cp)
        for q, (peer, peer_blk) in enumerate(peers):
            for k in range(n):
                cp = pltpu.make_async_remote_copy(
                    src_ref=src_fns[k](ins[k], peer_blk), dst_ref=dst_fns[k](outs[k], me),
                    send_sem=send_sems.at[k, q], recv_sem=recv_sems.at[k, q],
                    device_id=peer, device_id_type=MESH)
                cp.start()
                remote.append(cp)
        for q, (peer, peer_blk) in enumerate(peers):
            for k in range(n):
                pltpu.make_async_remote_copy(
                    src_ref=src_fns[k](ins[k], peer_blk), dst_ref=dst_fns[k](outs[k], peer_blk),
                    send_sem=send_sems.at[k, q], recv_sem=recv_sems.at[k, q],
                    device_id=peer, device_id_type=MESH).wait_recv()
        for cp in remote:
            cp.wait_send()
        for cp in local:
            cp.wait()

    any_spec = pl.BlockSpec(memory_space=pl.ANY)
    return pl.pallas_call(
        body,
        name=name,
        in_specs=[any_spec] * (n + n_after),
        out_specs=[any_spec] * n,
        out_shape=out_shapes,
        scratch_shapes=[pltpu.SemaphoreType.DMA((n, N_PEERS)), pltpu.SemaphoreType.DMA((n, N_PEERS)),
                        pltpu.SemaphoreType.DMA((n,))],
    )(*srcs, *after)


def sc_exchange(name, collective_id, srcs, out_shapes, src_fns, dst_fns):
    n = len(srcs)

    def body(*refs):
        ins, outs = refs[:n], refs[n:2 * n]
        send_sems, recv_sems, local_sems = refs[2 * n:]
        me, peers = _me_and_peers()
        barrier = pltpu.get_barrier_semaphore()
        for peer, _ in peers:
            pl.semaphore_signal(barrier, inc=1, device_id=peer, device_id_type=MESH)
        pl.semaphore_wait(barrier, N_PEERS)
        local, remote = [], []
        for k in range(n):
            cp = pltpu.make_async_copy(src_fns[k](ins[k], me), dst_fns[k](outs[k], me), local_sems.at[k])
            cp.start()
            local.append(cp)
        for q, (peer, peer_blk) in enumerate(peers):
            for k in range(n):
                cp = pltpu.make_async_remote_copy(
                    src_ref=src_fns[k](ins[k], peer_blk), dst_ref=dst_fns[k](outs[k], me),
                    send_sem=send_sems.at[k, q], recv_sem=recv_sems.at[k, q],
                    device_id=peer, device_id_type=MESH)
                cp.start()
                remote.append(cp)
        for q, (peer, peer_blk) in enumerate(peers):
            for k in range(n):
                pltpu.make_async_remote_copy(
                    src_ref=src_fns[k](ins[k], peer_blk), dst_ref=dst_fns[k](outs[k], peer_blk),
                    send_sem=send_sems.at[k, q], recv_sem=recv_sems.at[k, q],
                    device_id=peer, device_id_type=MESH).wait_recv()
        for cp in remote:
            cp.wait_send()
        for cp in local:
            cp.wait()

    return pl.kernel(
        body,
        out_type=out_shapes,
        mesh=plsc.ScalarSubcoreMesh(axis_name="sequencer", num_cores=1),
        name=name,
        scratch_types=[pltpu.SemaphoreType.DMA((n, N_PEERS)), pltpu.SemaphoreType.DMA((n, N_PEERS)),
                       pltpu.SemaphoreType.DMA((n,))],
        compiler_params=pltpu.CompilerParams(collective_id=collective_id),
    )(*srcs)


def sc_gather(name, collective_id, srcs, layers):
    n = len(srcs)
    outs_shape = [jax.ShapeDtypeStruct((NDEV, 1) + a.shape[1:], a.dtype) for a in srcs]

    def body(*refs):
        ins, outs = refs[:n], refs[n:2 * n]
        send_sems, recv_sems, local_sems = refs[2 * n:]
        x, y, c = lax.axis_index("x"), lax.axis_index("y"), lax.axis_index("c")
        me = 4 * x + 2 * y + c
        sibling = (x, y, 1 - c)
        chips = [(1 - x, y), (x, 1 - y), (1 - x, 1 - y)]
        barrier = pltpu.get_barrier_semaphore()
        for peer in [sibling] + [(cx, cy, c) for cx, cy in chips]:
            pl.semaphore_signal(barrier, inc=1, device_id=peer, device_id_type=MESH)
        pl.semaphore_wait(barrier, 1 + len(chips))

        def copy(k, slot, blk, to, src=None):
            place = outs[k].at[blk, 0]
            return pltpu.make_async_remote_copy(
                src_ref=place if src is None else src, dst_ref=place,
                send_sem=send_sems.at[k, slot], recv_sem=recv_sems.at[k, slot],
                device_id=to, device_id_type=MESH)

        local, sent = [], []
        for k in range(n):
            mine = ins[k].at[layers[k]]
            cp = pltpu.make_async_copy(mine, outs[k].at[me, 0], local_sems.at[k])
            cp.start()
            local.append(cp)
            sent.append(copy(k, 0, me, sibling, src=mine))
            sent += [copy(k, 1 + j, me, (cx, cy, c), src=mine) for j, (cx, cy) in enumerate(chips)]
        for cp in sent:
            cp.start()
        for k in range(n):
            for j, (cx, cy) in enumerate(chips):
                blk = 4 * cx + 2 * cy + c
                copy(k, 1 + j, blk, sibling).wait_recv()
                fwd = copy(k, 4 + j, blk, sibling)
                fwd.start()
                sent.append(fwd)
        for k in range(n):
            copy(k, 0, 4 * x + 2 * y + (1 - c), sibling).wait_recv()
            for j, (cx, cy) in enumerate(chips):
                copy(k, 4 + j, 4 * cx + 2 * cy + (1 - c), sibling).wait_recv()
        for cp in sent:
            cp.wait_send()
        for cp in local:
            cp.wait()

    return pl.kernel(
        body,
        out_type=outs_shape,
        mesh=plsc.ScalarSubcoreMesh(axis_name="sequencer", num_cores=1),
        name=name,
        scratch_types=[pltpu.SemaphoreType.DMA((n, N_PEERS)), pltpu.SemaphoreType.DMA((n, N_PEERS)),
                       pltpu.SemaphoreType.DMA((n,))],
        compiler_params=pltpu.CompilerParams(collective_id=collective_id),
    )(*srcs)


def _gather_src(layer):
    return lambda ref, blk: ref.at[layer]


def _gather_dst(ref, blk):
    return ref.at[blk, 0]


def _slice_of(ref, blk):
    return ref.at[blk]


def cast_bf16(name, w, tr_elems=512 * 1024):
    l, r, c = w.shape
    tr = _row_tile(r, tr_elems // c)
    spec = pl.BlockSpec((None, tr, c), lambda li, i: (li, i, 0))

    def body(w_ref, o_ref):
        o_ref[...] = w_ref[...].astype(BF16)

    return pl.pallas_call(
        body, name=name, grid=(l, r // tr), in_specs=[spec], out_specs=spec,
        out_shape=jax.ShapeDtypeStruct(w.shape, BF16),
        compiler_params=_params(("parallel", "parallel"), 6 * tr * c),
    )(w)


def _adamw_math(w, g, m, v):
    m = ADAM_B1 * m + (1.0 - ADAM_B1) * g
    v = ADAM_B2 * v + (1.0 - ADAM_B2) * (g * g)
    m_hat = m * (1.0 / (1.0 - ADAM_B1 ** ADAM_STEP))
    v_hat = v * (1.0 / (1.0 - ADAM_B2 ** ADAM_STEP))
    delta = -ADAM_LR * (m_hat / (jnp.sqrt(v_hat) + ADAM_EPS) + ADAM_WD * w)
    return delta, m, v


def _sum_blocks(ref):
    g = ref[0].astype(F32)
    for d in range(1, ref.shape[0]):
        g = g + ref[d].astype(F32)
    return g


def adamw_layer(name, recv, w, m, v, layer, stacked, after=None, tr_elems=256 * 1024):
    nd, r, c = recv.shape
    tr = _row_tile(r, tr_elems // c)
    r_spec = pl.BlockSpec((nd, tr, c), lambda i: (0, i, 0))
    w_spec = pl.BlockSpec((None, tr, c), lambda i: (layer, i, 0))
    if stacked is None:
        stacked = [lax.empty(w.shape, F32) for _ in range(4)]
    after = [] if after is None else [after]

    def body(r_ref, w_ref, m_ref, v_ref, g_in, d_in, m_in, v_in, *rest):
        g_out, d_out, m_out, v_out = rest[len(after):]
        g = _sum_blocks(r_ref)
        delta, m_new, v_new = _adamw_math(w_ref[...], g, m_ref[...], v_ref[...])
        g_out[...] = g
        d_out[...] = delta
        m_out[...] = m_new
        v_out[...] = v_new

    out = jax.ShapeDtypeStruct(w.shape, F32)
    return pl.pallas_call(
        body, name=name, grid=(r // tr,),
        in_specs=[r_spec, w_spec, w_spec, w_spec] + [pl.BlockSpec(memory_space=pl.ANY)] * (4 + len(after)),
        out_specs=[w_spec] * 4, out_shape=[out] * 4,
        input_output_aliases={4: 0, 5: 1, 6: 2, 7: 3},
        compiler_params=_params(("parallel",), tr * c * (2 * nd + 7 * 4)),
    )(recv, w, m, v, *stacked, *after)


def pack_small_grads(name, parts, taps, n_blocks):
    d = parts[0].shape[1]
    n_p, rows = len(parts), [t.shape[0] for t in taps]
    cb = d // n_blocks

    def body(*refs):
        part_refs, tap_refs = refs[:n_p], refs[n_p:n_p + len(taps)]
        sums_out, taps_out = refs[n_p + len(taps):]
        for i, r in enumerate(part_refs):
            sums_out[pl.ds(i, 1), :] = jnp.sum(r[...], axis=0, keepdims=True)
        r0 = 0
        for t_ref, n in zip(tap_refs, rows):
            for b in range(n_blocks):
                taps_out[b, pl.ds(r0, n), :] = t_ref[:, pl.ds(b * cb, cb)]
            r0 += n

    vm = pl.BlockSpec(memory_space=pltpu.VMEM)
    return pl.pallas_call(
        body, name=name, in_specs=[vm] * (n_p + len(taps)), out_specs=[vm] * 2,
        out_shape=[jax.ShapeDtypeStruct((n_p, d), F32), jax.ShapeDtypeStruct((n_blocks, sum(rows), cb), F32)],
    )(*parts, *taps)


def small_update(name, part_g, tap_g, w_a, m_a, v_a, w_b, m_b, v_b):
    nd, na, d = part_g.shape
    nb, cb = w_b.shape

    def body(pg_ref, tg_ref, wa_ref, ma_ref, va_ref, wb_ref, mb_ref, vb_ref,
             ga_out, da_out, ma_out, va_out, gb_out, db_out, mb_out, vb_out, loss_out):
        ga = _sum_blocks(pg_ref)
        delta, m_new, v_new = _adamw_math(wa_ref[...], ga, ma_ref[...], va_ref[...])
        ga_out[...] = ga
        da_out[...] = delta
        ma_out[...] = m_new
        va_out[...] = v_new
        loss_out[...] = jnp.broadcast_to(jnp.sum(ga[na - 1:na, :], axis=1, keepdims=True), loss_out.shape)
        gb = _sum_blocks(tg_ref)
        delta, m_new, v_new = _adamw_math(wb_ref[...], gb, mb_ref[...], vb_ref[...])
        gb_out[...] = gb
        db_out[...] = delta
        mb_out[...] = m_new
        vb_out[...] = v_new

    oa, ob = jax.ShapeDtypeStruct((na, d), F32), jax.ShapeDtypeStruct((nb, cb), F32)
    vm = pl.BlockSpec(memory_space=pltpu.VMEM)
    return pl.pallas_call(
        body, name=name, in_specs=[vm] * 8, out_specs=[vm] * 9,
        out_shape=[oa] * 4 + [ob] * 4 + [jax.ShapeDtypeStruct((1, LANES), F32)],
        compiler_params=pltpu.CompilerParams(vmem_limit_bytes=_vmem_limit(_nbytes(part_g.shape, F32))),
    )(part_g, tap_g, w_a, m_a, v_a, w_b, m_b, v_b)


BIG = ("cf_w_pw1", "cf_w_pw2", "sc_w_in", "sc_w_out", "mlp_w1", "mlp_w2", "ple_w_proj", "ple_w_gate")
COL_SHARDED = ("cf_w_pw1", "sc_w_in", "mlp_w1", "ple_w_proj")
WEIGHT_ORDER = ("norm_mix", "norm_mlp", "norm_ple", "cf_w_pw1", "cf_b_pw1", "cf_w_dw", "cf_b_dw", "cf_norm",
                "cf_w_pw2", "cf_b_pw2", "sc_w_in", "sc_w_conv", "sc_w_out", "mlp_w1", "mlp_w2", "ple_w_proj",
                "ple_w_gate", "norm_final")
SMALL_ROWS = (("norm_mix", 4), ("norm_mlp", 4), ("norm_ple", 4), ("cf_b_pw1", 4), ("cf_b_dw", 2), ("cf_norm", 2),
              ("cf_b_pw2", 2), ("norm_final", 1))


def _layer_weights(i):
    mixer = (("cf_w_pw1", i // 2), ("cf_w_pw2", i // 2)) if i % 2 == 0 else (("sc_w_in", i // 2), ("sc_w_out", i // 2))
    return mixer + (("mlp_w1", i), ("mlp_w2", i), ("ple_w_proj", i), ("ple_w_gate", i))


def _pad_rows(a, rows):
    return jnp.pad(a, ((0, 0), (0, rows - a.shape[1]), (0, 0)))


def _pack_taps(cf, sc):
    c = cf.shape[2]
    return jnp.concatenate([_pad_rows(cf, CONV_A_HALO).reshape(-1, c), _pad_rows(sc, CONV_B_HALO).reshape(-1, c)], axis=0)


def _unpack_taps(t, n_cf):
    c = t.shape[1]
    cf = t[:n_cf * CONV_A_HALO].reshape(n_cf, CONV_A_HALO, c)[:, :CONV_A_TAPS]
    sc = t[n_cf * CONV_A_HALO:].reshape(-1, CONV_B_HALO, c)[:, :CONV_B_TAPS]
    return cf, sc


def _pack_small(vals, d):
    return jnp.concatenate([vals[k].reshape(-1, d) for k, _ in SMALL_ROWS] + [jnp.zeros((1, d), F32)], axis=0)


def _unpack_small(a, shapes):
    out, r = {}, 0
    for k, n in SMALL_ROWS:
        out[k] = a[r:r + n].reshape(shapes[k])
        r += n
    return out


def kernel(x, p, norm_mix, norm_mlp, norm_ple, cf_w_pw1, cf_b_pw1, cf_w_dw, cf_b_dw, cf_norm, cf_w_pw2, cf_b_pw2, sc_w_in, sc_w_conv, sc_w_out, mlp_w1, mlp_w2, ple_w_proj, ple_w_gate, norm_final, loss_target, m_norm_mix, m_norm_mlp, m_norm_ple, m_cf_w_pw1, m_cf_b_pw1, m_cf_w_dw, m_cf_b_dw, m_cf_norm, m_cf_w_pw2, m_cf_b_pw2, m_sc_w_in, m_sc_w_conv, m_sc_w_out, m_mlp_w1, m_mlp_w2, m_ple_w_proj, m_ple_w_gate, m_norm_final, v_norm_mix, v_norm_mlp, v_norm_ple, v_cf_w_pw1, v_cf_b_pw1, v_cf_w_dw, v_cf_b_dw, v_cf_norm, v_cf_w_pw2, v_cf_b_pw2, v_sc_w_in, v_sc_w_conv, v_sc_w_out, v_mlp_w1, v_mlp_w2, v_ple_w_proj, v_ple_w_gate, v_norm_final):
    w = dict(norm_mix=norm_mix, norm_mlp=norm_mlp, norm_ple=norm_ple, cf_w_pw1=cf_w_pw1, cf_b_pw1=cf_b_pw1,
             cf_w_dw=cf_w_dw, cf_b_dw=cf_b_dw, cf_norm=cf_norm, cf_w_pw2=cf_w_pw2, cf_b_pw2=cf_b_pw2,
             sc_w_in=sc_w_in, sc_w_conv=sc_w_conv, sc_w_out=sc_w_out, mlp_w1=mlp_w1, mlp_w2=mlp_w2,
             ple_w_proj=ple_w_proj, ple_w_gate=ple_w_gate, norm_final=norm_final)
    m = dict(norm_mix=m_norm_mix, norm_mlp=m_norm_mlp, norm_ple=m_norm_ple, cf_w_pw1=m_cf_w_pw1, cf_b_pw1=m_cf_b_pw1,
             cf_w_dw=m_cf_w_dw, cf_b_dw=m_cf_b_dw, cf_norm=m_cf_norm, cf_w_pw2=m_cf_w_pw2, cf_b_pw2=m_cf_b_pw2,
             sc_w_in=m_sc_w_in, sc_w_conv=m_sc_w_conv, sc_w_out=m_sc_w_out, mlp_w1=m_mlp_w1, mlp_w2=m_mlp_w2,
             ple_w_proj=m_ple_w_proj, ple_w_gate=m_ple_w_gate, norm_final=m_norm_final)
    v = dict(norm_mix=v_norm_mix, norm_mlp=v_norm_mlp, norm_ple=v_norm_ple, cf_w_pw1=v_cf_w_pw1, cf_b_pw1=v_cf_b_pw1,
             cf_w_dw=v_cf_w_dw, cf_b_dw=v_cf_b_dw, cf_norm=v_cf_norm, cf_w_pw2=v_cf_w_pw2, cf_b_pw2=v_cf_b_pw2,
             sc_w_in=v_sc_w_in, sc_w_conv=v_sc_w_conv, sc_w_out=v_sc_w_out, mlp_w1=v_mlp_w1, mlp_w2=v_mlp_w2,
             ple_w_proj=v_ple_w_proj, ple_w_gate=v_ple_w_gate, norm_final=v_norm_final)
    depth, d = norm_mix.shape
    n_cf = cf_w_dw.shape[0]

    taps_w = _pack_taps(cf_w_dw, sc_w_conv)
    shards, gathered = {}, {}
    ids = iter(range(5 * depth))
    for i in range(depth):
        names = _layer_weights(i)
        groups = [names[:1], names[1:2], names[2:3], names[3:]] if i == 0 else [names[:2], names[2:]]
        for n_group, group in enumerate(groups):
            with_taps = i == 0 and n_group == 1
            for k, _ in group:
                if k not in shards:
                    shards[k] = cast_bf16(f"cast_{k}", w[k])
            srcs = [shards[k] for k, _ in group] + ([taps_w[None]] if with_taps else [])
            got = sc_gather(f"gather_{i}_{n_group}", next(ids), srcs, [l for _, l in group] + ([0] if with_taps else []))
            for (k, l), g in zip(group, got):
                gathered[(k, l)] = g if k in COL_SHARDED else g.reshape(-1, g.shape[3])
            if with_taps:
                taps_full = jnp.transpose(got[-1][:, 0], (1, 0, 2)).reshape(taps_w.shape[0], d)
    conv_w = {"cf": taps_full[:n_cf * CONV_A_HALO].reshape(n_cf, CONV_A_HALO, d),
              "sc": taps_full[n_cf * CONV_A_HALO:].reshape(-1, CONV_B_HALO, d)}

    def get_w(k, l, after=None):
        return gathered[(k, l)]

    received, waiting = {}, {}
    last_group = list(_layer_weights(0)[:2])

    def put_grads(grads):
        names = list(grads)
        if names[0][0] in ("cf_w_pw2", "sc_w_out") and set(names) != set(last_group):
            waiting.update(grads)
            return
        grads = {**waiting, **grads}
        waiting.clear()
        names = list(grads)
        got = sc_exchange(f"grad_exchange_{names[-1][0]}_{names[-1][1]}", next(ids), [grads[n] for n in names],
                          [jax.ShapeDtypeStruct(grads[n].shape, BF16) for n in names],
                          [_slice_of] * len(names), [_slice_of] * len(names))
        received.update(zip(names, got))

    small = {k: w[k] for k, _ in SMALL_ROWS}
    small["norm_final"] = norm_final[None]
    loss_part, grad_x, sg = _local_step(x[0], p[:, 0], loss_target[0], small, get_w, conv_w, put_grads)

    out = {k: None for k in BIG}
    previous = None
    for (k, l), recv in received.items():
        if (k, l) not in last_group:
            out[k] = adamw_layer(f"adamw_{k}_{l}", recv, w[k], m[k], v[k], l, out[k], after=previous)
            previous = out[k][1]
    updated_first = [out[k][0] for k in BIG if out[k] is not None and k not in [n for n, _ in last_group]]

    parts = []
    for k, n in SMALL_ROWS:
        for g in sg[k]:
            parts += [g[:, :d], g[:, d:]] if g.shape[1] == 2 * d else [g]
    parts.append(loss_part)
    sums, tap_slices = pack_small_grads("pack_small_grads", parts, sg["cf_w_dw"] + sg["sc_w_conv"], NDEV)
    part_all, tap_mine = _exchange(
        "small_exchange", [sums[None], tap_slices],
        [jax.ShapeDtypeStruct((NDEV, 1) + sums.shape, F32), jax.ShapeDtypeStruct(tap_slices.shape, F32)],
        [_gather_src(0), _slice_of], [_gather_dst, _slice_of], after=updated_first)
    for k, l in last_group:
        out[k] = adamw_layer(f"adamw_{k}_{l}", received[(k, l)], w[k], m[k], v[k], l, out[k], after=part_all)
    sm = small_update("small_update", part_all[:, 0], tap_mine,
                      _pack_small(w, d), _pack_small(m, d), _pack_small(v, d),
                      taps_w, _pack_taps(m["cf_w_dw"], m["sc_w_conv"]), _pack_taps(v["cf_w_dw"], v["sc_w_conv"]))
    shapes = {k: w[k].shape for k, _ in SMALL_ROWS}
    for t in range(4):
        un = _unpack_small(sm[t], shapes)
        cf_t, sc_t = _unpack_taps(sm[4 + t], n_cf)
        for k in un:
            out.setdefault(k, [None] * 4)[t] = un[k]
        out.setdefault("cf_w_dw", [None] * 4)[t] = cf_t
        out.setdefault("sc_w_conv", [None] * 4)[t] = sc_t
    loss = sm[8][0, 0]

    return (loss, grad_x[None], *[out[k][0] for k in WEIGHT_ORDER], *[out[k][1] for k in WEIGHT_ORDER],
            *[out[k][2] for k in WEIGHT_ORDER], *[out[k][3] for k in WEIGHT_ORDER])
```

```python
import jax
import jax.numpy as jnp
from jax import lax
from jax.experimental import pallas as pl
from jax.experimental.pallas import tpu as pltpu
from jax.experimental.pallas import tpu_sc as plsc

F32 = jnp.float32
BF16 = jnp.bfloat16
EPS = 1e-6
NDEV = 8
N_PEERS = NDEV - 1
MESH = pl.DeviceIdType.MESH

ADAM_LR = 0.001
ADAM_B1 = 0.9
ADAM_B2 = 0.999
ADAM_EPS = 1e-08
ADAM_WD = 0.01
ADAM_STEP = 10

V7X_VMEM_BYTES = 64 * 1024 * 1024
VMEM_LIMIT_MAX = 56 * 1024 * 1024
SUBLANES = 8
LANES = 128
CONV_A_TAPS = 31
CONV_A_HALO = 32
CONV_B_TAPS = 3
CONV_B_HALO = 8


def _nbytes(shape, dtype):
    n = 1
    for s in shape:
        if s is not None:
            n *= s
    return n * jnp.dtype(dtype).itemsize


def _vmem_limit(block_bytes, scratch_bytes=0):
    need = 2 * block_bytes + scratch_bytes
    return int(min(VMEM_LIMIT_MAX, max(32 * 1024 * 1024, need + need // 2 + (4 << 20))))


def _params(sem, block_bytes, scratch_bytes=0):
    return pltpu.CompilerParams(dimension_semantics=sem, vmem_limit_bytes=_vmem_limit(block_bytes, scratch_bytes))


_DIMS = {
    "nn": (((1,), (0,)), ((), ())),
    "nt": (((1,), (1,)), ((), ())),
    "tn": (((0,), (0,)), ((), ())),
}


def _mm(name, dims, grid, acc_shape, a, a_spec, b, b_spec, extras, outs, epi):
    ni, nj, nk = grid
    n_ex, n_out = len(extras), len(outs)
    dn = _DIMS[dims]
    b_sub = [s for s in b_spec.block_shape if s is not None]
    n_sub = b_sub[0] if len(b_sub) == 3 else 1

    def body(*refs):
        a_ref, b_ref = refs[0], refs[1]
        ex_refs = refs[2:2 + n_ex]
        out_refs = refs[2 + n_ex:2 + n_ex + n_out]
        if n_sub == 1:
            d = lax.dot_general(a_ref[...].astype(BF16), b_ref[...].astype(BF16), dn, preferred_element_type=F32)
        else:
            w_sub = a_ref.shape[1] // n_sub
            d = None
            for s in range(n_sub):
                part = lax.dot_general(a_ref[:, pl.ds(s * w_sub, w_sub)].astype(BF16), b_ref[s].astype(BF16), dn,
                                       preferred_element_type=F32)
                d = part if d is None else d + part

        def finish(acc):
            res = epi(acc, *[r[...] for r in ex_refs])
            for o_ref, r in zip(out_refs, res):
                o_ref[...] = r.astype(o_ref.dtype)

        if nk == 1:
            finish(d)
        else:
            acc_ref = refs[2 + n_ex + n_out]
            k = pl.program_id(2)

            @pl.when(k == 0)
            def _():
                acc_ref[...] = d

            @pl.when(jnp.logical_and(k > 0, k < nk - 1))
            def _():
                acc_ref[...] += d

            @pl.when(k == nk - 1)
            def _():
                finish(acc_ref[...] + d)

    blk = _nbytes(a_spec.block_shape, a.dtype) + _nbytes(b_spec.block_shape, b.dtype)
    for arr, spec in list(extras) + list(outs):
        blk += _nbytes(spec.block_shape, arr.dtype)
    acc_bytes = _nbytes(acc_shape, F32)
    scratch = [pltpu.VMEM(acc_shape, F32)] if nk > 1 else []
    return pl.pallas_call(
        body,
        name=name,
        grid=grid,
        in_specs=[a_spec, b_spec] + [s for _, s in extras],
        out_specs=[s for _, s in outs],
        out_shape=[o for o, _ in outs],
        scratch_shapes=scratch,
        compiler_params=_params(("parallel", "parallel", "arbitrary"), blk, 3 * acc_bytes),
    )(a, b, *[e for e, _ in extras])


def _tile(n, pref):
    if n <= pref:
        return n
    t = pref - pref % LANES
    while t > LANES and n % t:
        t -= LANES
    assert n % t == 0, (n, pref)
    return t


def _row_tile(n, pref):
    if n <= pref:
        return n
    t = max(SUBLANES, pref - pref % SUBLANES)
    while t > SUBLANES and n % t:
        t -= SUBLANES
    assert n % t == 0, (n, pref)
    return t


def _id_epi(acc):
    return (acc,)


def mm_x_wcol(name, x, w, layer, extras=(), outs_dtypes=(F32,), epi=_id_epi, tm=1024, tn=1024):
    m, kdim = x.shape
    c = w.shape[3]
    n = NDEV * c
    tm, tn = _tile(m, tm), _tile(c, tn)
    tk = _tile(kdim, 2048)
    grid = (m // tm, n // tn, kdim // tk)
    per = c // tn
    a_spec = pl.BlockSpec((tm, tk), lambda i, j, k: (i, k))
    b_spec = pl.BlockSpec((None, None, tk, tn), lambda i, j, k: (j // per, layer, k, j % per))
    ex = [(e, _ex_spec(e, kind, tm, tn)) for e, kind in extras]
    o_spec = pl.BlockSpec((tm, tn), lambda i, j, k: (i, j))
    outs = [(jax.ShapeDtypeStruct((m, n), dt), o_spec) for dt in outs_dtypes]
    return _mm(name, "nn", grid, (tm, tn), x, a_spec, w, b_spec, ex, outs, epi)


def mm_x_wrow(name, x, w, extras=(), outs_dtypes=(F32,), epi=_id_epi, tm=1024, tn=512, tk=2048):
    m, kdim = x.shape
    n = w.shape[1]
    assert kdim == w.shape[0]
    tm, tn = _tile(m, tm), _tile(n, tn)
    tk = _tile(kdim, tk)
    grid = (m // tm, n // tn, kdim // tk)
    a_spec = pl.BlockSpec((tm, tk), lambda i, j, k: (i, k))
    b_spec = pl.BlockSpec((tk, tn), lambda i, j, k: (k, j))
    ex = [(e, _ex_spec(e, kind, tm, tn)) for e, kind in extras]
    o_spec = pl.BlockSpec((tm, tn), lambda i, j, k: (i, j))
    outs = [(jax.ShapeDtypeStruct((m, n), dt), o_spec) for dt in outs_dtypes]
    return _mm(name, "nn", grid, (tm, tn), x, a_spec, w, b_spec, ex, outs, epi)


def mm_dy_wcol_t(name, dy, w, layer, extras=(), outs_dtypes=(F32,), epi=_id_epi, tm=1024, b_block_bytes=4 << 20):
    m, n = dy.shape
    nd, kdim, c = w.shape[0], w.shape[2], w.shape[3]
    assert n == nd * c
    tm, tn = _tile(m, tm), _tile(kdim, max(LANES, b_block_bytes // (n * 2)))
    n_sub = nd
    grid = (m // tm, kdim // tn, nd // n_sub)
    a_spec = pl.BlockSpec((tm, n_sub * c), lambda i, j, k: (i, k))
    if n_sub > 1:
        b_spec = pl.BlockSpec((n_sub, None, tn, c), lambda i, j, k: (k, layer, j, 0))
    else:
        b_spec = pl.BlockSpec((None, None, tn, c), lambda i, j, k: (k, layer, j, 0))
    ex = [(e, _ex_spec(e, kind, tm, tn)) for e, kind in extras]
    o_spec = pl.BlockSpec((tm, tn), lambda i, j, k: (i, j))
    outs = [(jax.ShapeDtypeStruct((m, kdim), dt), o_spec) for dt in outs_dtypes]
    return _mm(name, "nt", grid, (tm, tn), dy, a_spec, w, b_spec, ex, outs, epi)


def mm_dy_wrow_t(name, dy, w, extras=(), outs_dtypes=(F32,), epi=_id_epi, tm=1024, tn=1024):
    m, n = dy.shape
    kdim = w.shape[0]
    assert n == w.shape[1]
    tm, tn = _tile(m, tm), _tile(kdim, tn)
    tk = _tile(n, 2048)
    grid = (m // tm, kdim // tn, n // tk)
    a_spec = pl.BlockSpec((tm, tk), lambda i, j, k: (i, k))
    b_spec = pl.BlockSpec((tn, tk), lambda i, j, k: (j, k))
    ex = [(e, _ex_spec(e, kind, tm, tn)) for e, kind in extras]
    o_spec = pl.BlockSpec((tm, tn), lambda i, j, k: (i, j))
    outs = [(jax.ShapeDtypeStruct((m, kdim), dt), o_spec) for dt in outs_dtypes]
    return _mm(name, "nt", grid, (tm, tn), dy, a_spec, w, b_spec, ex, outs, epi)


def mm_xt_dy(name, x, dy, col_shards, tm=1024, tn=1024):
    m, kdim = x.shape
    n = dy.shape[1]
    tk = _tile(m, 4096)
    if col_shards:
        c = n // NDEV
        tm, tn = _tile(kdim, tm), _tile(c, tn)
        per = c // tn
        out = jax.ShapeDtypeStruct((NDEV, kdim, c), BF16)
        o_spec = pl.BlockSpec((None, tm, tn), lambda i, j, k: (j // per, i, j % per))
    else:
        tm, tn = _tile(kdim, tm), _tile(n, tn)
        out = jax.ShapeDtypeStruct((kdim, n), BF16)
        o_spec = pl.BlockSpec((tm, tn), lambda i, j, k: (i, j))
    grid = (kdim // tm, n // tn, m // tk)
    a_spec = pl.BlockSpec((tk, tm), lambda i, j, k: (k, i))
    b_spec = pl.BlockSpec((tk, tn), lambda i, j, k: (k, j))
    g = _mm(name, "tn", grid, (tm, tn), x, a_spec, dy, b_spec, [], [(out, o_spec)], _id_epi)[0]
    return g if col_shards else g.reshape(NDEV, kdim // NDEV, n)


def _ex_spec(e, kind, tm, tn):
    if kind == "tile":
        return pl.BlockSpec((tm, tn), lambda i, j, k: (i, j))
    if kind == "row":
        return pl.BlockSpec((1, tn), lambda i, j, k: (0, j))
    raise ValueError(kind)


def _rows_call(name, body, n_rows, ts, ins, outs, scratch=(), scratch_bytes=0):
    blk = sum(_nbytes(s.block_shape, a.dtype) for a, s in list(ins) + list(outs))
    return pl.pallas_call(
        body,
        name=name,
        grid=(n_rows // ts,),
        in_specs=[s for _, s in ins],
        out_specs=[s for _, s in outs],
        out_shape=[o for o, _ in outs],
        scratch_shapes=list(scratch),
        compiler_params=_params(("arbitrary",), blk, scratch_bytes + 4 * blk // 2),
    )(*[a for a, _ in ins])


def _blk(ts, d):
    return pl.BlockSpec((ts, d), lambda i: (i, 0))


def _full(shape):
    return pl.BlockSpec(shape, lambda i: tuple(0 for _ in shape))


def _rowsum8(v):
    t, d = v.shape
    return jnp.sum(v.reshape(t // SUBLANES, SUBLANES, d), axis=0)


def _accumulate(ref, val):
    @pl.when(pl.program_id(0) == 0)
    def _():
        ref[...] = val

    @pl.when(pl.program_id(0) > 0)
    def _():
        ref[...] += val


def _rstd(x):
    return lax.rsqrt(jnp.mean(x * x, axis=-1, keepdims=True) + EPS)


def _rms_bwd_math(dy, x, g):
    r = _rstd(x)
    gdy = dy * g
    c = jnp.mean(gdy * x, axis=-1, keepdims=True)
    dx = r * gdy - x * (r * r * r * c)
    return dx, dy * (x * r)


def rms_fwd(name, h, g, ts=512):
    s, d = h.shape
    ts = min(ts, s)

    def body(h_ref, g_ref, u_ref):
        x = h_ref[...]
        u_ref[...] = ((x * _rstd(x)) * g_ref[...]).astype(BF16)

    return _rows_call(name, body, s, ts, [(h, _blk(ts, d)), (g, _full((1, d)))],
                      [(jax.ShapeDtypeStruct((s, d), BF16), _blk(ts, d))])[0]


def rms_bwd(name, du, h, g, dres, ts=256):
    s, d = h.shape
    ts = min(ts, s)

    def body(du_ref, h_ref, g_ref, dres_ref, dh_ref, dhb_ref, dg_ref, cs_ref):
        dx, dg = _rms_bwd_math(du_ref[...], h_ref[...], g_ref[...])
        dh = dres_ref[...] + dx
        dh_ref[...] = dh
        dhb_ref[...] = dh.astype(BF16)
        _accumulate(dg_ref, _rowsum8(dg))
        _accumulate(cs_ref, _rowsum8(dh))

    return _rows_call(
        name, body, s, ts,
        [(du, _blk(ts, d)), (h, _blk(ts, d)), (g, _full((1, d))), (dres, _blk(ts, d))],
        [(jax.ShapeDtypeStruct((s, d), F32), _blk(ts, d)), (jax.ShapeDtypeStruct((s, d), BF16), _blk(ts, d)),
         (jax.ShapeDtypeStruct((SUBLANES, d), F32), _full((SUBLANES, d))),
         (jax.ShapeDtypeStruct((SUBLANES, d), F32), _full((SUBLANES, d)))])


def final_loss(name, h, g, target, ts=256):
    s, d = h.shape
    ts = min(ts, s)

    def body(h_ref, g_ref, t_ref, loss_ref, dh_ref, dg_ref):
        x = h_ref[...]
        gf = g_ref[...]
        y = (x * _rstd(x)) * gf
        err = y - t_ref[...]
        _accumulate(loss_ref, _rowsum8(err * err) * (0.5 / d))
        dx, dg = _rms_bwd_math(err * (1.0 / d), x, gf)
        dh_ref[...] = dx
        _accumulate(dg_ref, _rowsum8(dg))

    return _rows_call(
        name, body, s, ts,
        [(h, _blk(ts, d)), (g, _full((1, d))), (target, _blk(ts, d))],
        [(jax.ShapeDtypeStruct((SUBLANES, d), F32), _full((SUBLANES, d))),
         (jax.ShapeDtypeStruct((s, d), F32), _blk(ts, d)),
         (jax.ShapeDtypeStruct((SUBLANES, d), F32), _full((SUBLANES, d)))])


def ple_bwd_elem(name, dh, g, e, ts=512):
    s, d = dh.shape
    ts = min(ts, s)

    def body(dh_ref, g_ref, e_ref, de_ref, dgl_ref):
        dh_v, g_v = dh_ref[...], g_ref[...]
        de_ref[...] = (dh_v * g_v).astype(BF16)
        dgl_ref[...] = (dh_v * e_ref[...] * (g_v * (1.0 - g_v))).astype(BF16)

    return _rows_call(name, body, s, ts, [(dh, _blk(ts, d)), (g, _blk(ts, d)), (e, _blk(ts, d))],
                      [(jax.ShapeDtypeStruct((s, d), BF16), _blk(ts, d)),
                       (jax.ShapeDtypeStruct((s, d), BF16), _blk(ts, d))])


CONV_LANES = 256
CONV_ROWS = 64


def _lane_chunks(d, fn):
    lc = min(CONV_LANES, d)

    def lane_body(c, carry):
        fn(pl.ds(pl.multiple_of(c * lc, lc), lc))
        return carry

    lax.fori_loop(0, d // lc, lane_body, 0)


def _shifted_copies(buf, sh, lanes):
    rows = buf.shape[0] - SUBLANES
    for s in range(1, SUBLANES):
        sh[s, pl.ds(0, rows), :] = buf[pl.ds(s, rows), lanes]


def _window(buf, sh, lanes, start, rows):
    s = start % SUBLANES
    if s == 0:
        return buf[pl.ds(start, rows), lanes]
    return sh[s, pl.ds(start - s, rows), :]


def _prev_halo_spec(ts, halo, width):
    per = ts // halo
    return pl.BlockSpec((halo, width), lambda i: (jnp.maximum(i * per - 1, 0), 0))


def _next_halo_spec(ts, halo, width, n_rows):
    per = ts // halo
    last = n_rows // halo - 1
    return pl.BlockSpec((halo, width), lambda i: (jnp.minimum((i + 1) * per, last), 0))


def cf_fwd_mid(name, a, w_dw, b_dw, gn, ts=256):
    s, d2 = a.shape
    d = d2 // 2
    ts = min(ts, s)
    hl = CONV_A_HALO
    off = hl - (CONV_A_TAPS - 1)

    rc = min(CONV_ROWS, ts)

    def body(a_ref, ah_ref, w_ref, b_ref, gn_ref, v0_ref, v1_ref, v3_ref, buf, sh):
        first = pl.program_id(0) == 0
        halo = ah_ref[...]
        hv0 = halo[:, :d] * jax.nn.sigmoid(halo[:, d:])
        buf[pl.ds(0, hl), :] = jnp.where(first, 0.0, hv0)
        main = a_ref[...]
        v0 = main[:, :d] * jax.nn.sigmoid(main[:, d:])
        buf[pl.ds(hl, ts), :] = v0
        v0_ref[...] = v0

        def conv(lanes):
            _shifted_copies(buf, sh, lanes)
            for r0 in range(0, ts, rc):
                acc = jnp.zeros((rc, lanes.size), F32)
                for k in range(CONV_A_TAPS):
                    acc = acc + w_ref[pl.ds(k, 1), lanes] * _window(buf, sh, lanes, r0 + off + k, rc)
                v1_ref[pl.ds(r0, rc), lanes] = acc + b_ref[:, lanes]

        _lane_chunks(d, conv)
        v1 = v1_ref[...]
        v2 = (v1 * _rstd(v1)) * gn_ref[...]
        v3_ref[...] = (v2 * jax.nn.sigmoid(v2)).astype(BF16)

    return _rows_call(
        name, body, s, ts,
        [(a, _blk(ts, d2)), (a, _prev_halo_spec(ts, hl, d2)), (w_dw, _full(w_dw.shape)),
         (b_dw, _full((1, d))), (gn, _full((1, d)))],
        [(jax.ShapeDtypeStruct((s, d), F32), _blk(ts, d)), (jax.ShapeDtypeStruct((s, d), F32), _blk(ts, d)),
         (jax.ShapeDtypeStruct((s, d), BF16), _blk(ts, d))],
        scratch=[pltpu.VMEM((hl + ts, d), F32), pltpu.VMEM((SUBLANES, hl + ts, min(CONV_LANES, d)), F32)],
        scratch_bytes=_nbytes((hl + ts, d + SUBLANES * CONV_LANES), F32))


def cf_bwd_rows(name, dv3, v1, gn, ts=256):
    s, d = v1.shape
    ts = min(ts, s)

    def body(dv3_ref, v1_ref, gn_ref, dv1_ref, dgn_ref, db_ref):
        v1 = v1_ref[...]
        gn_v = gn_ref[...]
        v2 = (v1 * _rstd(v1)) * gn_v
        sg = jax.nn.sigmoid(v2)
        dv2 = dv3_ref[...] * (sg * (1.0 + v2 * (1.0 - sg)))
        dv1, dgn = _rms_bwd_math(dv2, v1, gn_v)
        dv1_ref[...] = dv1
        _accumulate(dgn_ref, _rowsum8(dgn))
        _accumulate(db_ref, _rowsum8(dv1))

    return _rows_call(
        name, body, s, ts, [(dv3, _blk(ts, d)), (v1, _blk(ts, d)), (gn, _full((1, d)))],
        [(jax.ShapeDtypeStruct((s, d), F32), _blk(ts, d)),
         (jax.ShapeDtypeStruct((SUBLANES, d), F32), _full((SUBLANES, d))),
         (jax.ShapeDtypeStruct((SUBLANES, d), F32), _full((SUBLANES, d)))])


def cf_bwd_conv(name, dv1, v0, a, w_dw, ts=256):
    s, d = dv1.shape
    ts = min(ts, s)
    hl = CONV_A_HALO
    taps = CONV_A_TAPS
    off = hl - (taps - 1)
    last_blk = s // ts - 1

    rc = min(CONV_ROWS, ts)

    def body(dv1_ref, dv1n_ref, v0_ref, v0p_ref, a_ref, w_ref, da_ref, dw_ref, db_ref,
             dbuf, vbuf, dv0_buf, dw_acc, dsh, vsh):
        i = pl.program_id(0)
        dbuf[pl.ds(0, ts), :] = dv1_ref[...]
        dbuf[pl.ds(ts, hl), :] = jnp.where(i == last_blk, 0.0, dv1n_ref[...])
        vbuf[pl.ds(0, hl), :] = jnp.where(i == 0, 0.0, v0p_ref[...])
        vbuf[pl.ds(hl, ts), :] = v0_ref[...]

        @pl.when(i == 0)
        def _():
            dw_acc[...] = jnp.zeros_like(dw_acc)

        def conv_t(lanes):
            _shifted_copies(dbuf, dsh, lanes)
            _shifted_copies(vbuf, vsh, lanes)
            for r0 in range(0, ts, rc):
                g = dbuf[pl.ds(r0, rc), lanes]
                acc = jnp.zeros((rc, lanes.size), F32)
                for k in range(taps):
                    acc = acc + w_ref[pl.ds(k, 1), lanes] * _window(dbuf, dsh, lanes, r0 + taps - 1 - k, rc)
                    prod = g * _window(vbuf, vsh, lanes, r0 + off + k, rc)
                    dw_acc[pl.ds(k * SUBLANES, SUBLANES), lanes] += _rowsum8(prod)
                dv0_buf[pl.ds(r0, rc), lanes] = acc

        _lane_chunks(d, conv_t)
        dv0 = dv0_buf[...]
        av = a_ref[...]
        val, sg = av[:, :d], jax.nn.sigmoid(av[:, d:])
        dval = dv0 * sg
        dgate = dv0 * val * (sg * (1.0 - sg))
        da_ref[:, :d] = dval.astype(BF16)
        da_ref[:, d:] = dgate.astype(BF16)
        _accumulate(db_ref.at[:, pl.ds(0, d)], _rowsum8(dval))
        _accumulate(db_ref.at[:, pl.ds(d, d)], _rowsum8(dgate))

        @pl.when(i == last_blk)
        def _():
            dw_ref[...] = jnp.sum(dw_acc[...].reshape(hl, SUBLANES, d), axis=1)

    lc = min(CONV_LANES, d)
    scratch = [pltpu.VMEM((ts + hl, d), F32), pltpu.VMEM((hl + ts, d), F32), pltpu.VMEM((ts, d), F32),
               pltpu.VMEM((hl * SUBLANES, d), F32), pltpu.VMEM((SUBLANES, ts + hl, lc), F32),
               pltpu.VMEM((SUBLANES, hl + ts, lc), F32)]
    sbytes = _nbytes((3 * ts + 2 * hl + hl * SUBLANES, d), F32) + 2 * _nbytes((SUBLANES, ts + hl, lc), F32)
    return _rows_call(
        name, body, s, ts,
        [(dv1, _blk(ts, d)), (dv1, _next_halo_spec(ts, hl, d, s)), (v0, _blk(ts, d)), (v0, _prev_halo_spec(ts, hl, d)),
         (a, _blk(ts, 2 * d)), (w_dw, _full(w_dw.shape))],
        [(jax.ShapeDtypeStruct((s, 2 * d), BF16), _blk(ts, 2 * d)),
         (jax.ShapeDtypeStruct((hl, d), F32), _full((hl, d))),
         (jax.ShapeDtypeStruct((SUBLANES, 2 * d), F32), _full((SUBLANES, 2 * d)))],
        scratch=scratch, scratch_bytes=sbytes)


def sc_fwd_mid(name, bcv, w_conv, ts=256):
    s, d3 = bcv.shape
    d = d3 // 3
    ts = min(ts, s)
    hl = CONV_B_HALO
    off = hl - (CONV_B_TAPS - 1)

    def body(x_ref, xp_ref, w_ref, y_ref, buf):
        hp = xp_ref[...]
        buf[pl.ds(0, hl), :] = jnp.where(pl.program_id(0) == 0, 0.0, hp[:, d:2 * d] * hp[:, 2 * d:])
        buf[pl.ds(hl, ts), :] = x_ref[:, d:2 * d] * x_ref[:, 2 * d:]
        cc = jnp.zeros((ts, d), F32)
        for k in range(CONV_B_TAPS):
            cc = cc + w_ref[pl.ds(k, 1), :] * buf[pl.ds(off + k, ts), :]
        y_ref[...] = (x_ref[:, :d] * cc).astype(BF16)

    return _rows_call(
        name, body, s, ts,
        [(bcv, _blk(ts, d3)), (bcv, _prev_halo_spec(ts, hl, d3)), (w_conv, _full(w_conv.shape))],
        [(jax.ShapeDtypeStruct((s, d), BF16), _blk(ts, d))],
        scratch=[pltpu.VMEM((hl + ts, d), F32)], scratch_bytes=_nbytes((hl + ts, d), F32))[0]


def sc_bwd_mid(name, dy, bcv, w_conv, ts=256):
    s, d3 = bcv.shape
    d = d3 // 3
    ts = min(ts, s)
    hl = CONV_B_HALO
    taps = CONV_B_TAPS
    off = hl - (taps - 1)
    last_blk = s // ts - 1

    def body(dy_ref, dyn_ref, x_ref, xp_ref, xn_ref, w_ref, dx_ref, dw_ref, cvbuf, dbuf, dw_acc):
        i = pl.program_id(0)
        hp = xp_ref[...]
        cvbuf[pl.ds(0, hl), :] = jnp.where(i == 0, 0.0, hp[:, d:2 * d] * hp[:, 2 * d:])
        gb, gc, v = x_ref[:, :d], x_ref[:, d:2 * d], x_ref[:, 2 * d:]
        cvbuf[pl.ds(hl, ts), :] = gc * v
        dy_v = dy_ref[...]
        dcc = dy_v * gb
        dbuf[pl.ds(0, ts), :] = dcc
        dbuf[pl.ds(ts, hl), :] = jnp.where(i == last_blk, 0.0, dyn_ref[...] * xn_ref[:, :d])

        @pl.when(i == 0)
        def _():
            dw_acc[...] = jnp.zeros_like(dw_acc)

        cc = jnp.zeros((ts, d), F32)
        dcv = jnp.zeros((ts, d), F32)
        for k in range(taps):
            win = cvbuf[pl.ds(off + k, ts), :]
            cc = cc + w_ref[pl.ds(k, 1), :] * win
            dcv = dcv + w_ref[pl.ds(k, 1), :] * dbuf[pl.ds(taps - 1 - k, ts), :]
            dw_acc[pl.ds(k * SUBLANES, SUBLANES), :] += _rowsum8(dcc * win)
        dx_ref[:, :d] = (dy_v * cc).astype(BF16)
        dx_ref[:, d:2 * d] = (dcv * v).astype(BF16)
        dx_ref[:, 2 * d:] = (dcv * gc).astype(BF16)

        @pl.when(i == last_blk)
        def _():
            dw_ref[...] = jnp.sum(dw_acc[...].reshape(hl, SUBLANES, d), axis=1)

    scratch = [pltpu.VMEM((hl + ts, d), F32), pltpu.VMEM((ts + hl, d), F32), pltpu.VMEM((hl * SUBLANES, d), F32)]
    sbytes = _nbytes((2 * ts + 2 * hl + hl * SUBLANES, d), F32)
    return _rows_call(
        name, body, s, ts,
        [(dy, _blk(ts, d)), (dy, _next_halo_spec(ts, hl, d, s)), (bcv, _blk(ts, d3)), (bcv, _prev_halo_spec(ts, hl, d3)),
         (bcv, _next_halo_spec(ts, hl, d3, s)), (w_conv, _full(w_conv.shape))],
        [(jax.ShapeDtypeStruct((s, d3), BF16), _blk(ts, d3)), (jax.ShapeDtypeStruct((hl, d), F32), _full((hl, d)))],
        scratch=scratch, scratch_bytes=sbytes)


def _row(a, i):
    return lax.slice_in_dim(a, i, i + 1, axis=0)


def _local_step(x, p, target, small, get_w, conv_w, put_grads):
    depth = p.shape[0]
    acts = []
    h = x
    for i in range(depth):
        j = i // 2
        act = {"h": h}
        u = rms_fwd(f"rms_mix_{i}", h, _row(small["norm_mix"], i))
        act["u"] = u
        if i % 2 == 0:
            a = mm_x_wcol(f"cf_pw1_{i}", u, get_w("cf_w_pw1", j, u), 0, extras=[(_row(small["cf_b_pw1"], j), "row")],
                          epi=lambda acc, b: (acc + b,))[0]
            v0, v1, v3 = cf_fwd_mid(f"cf_mid_{i}", a, conv_w["cf"][j], _row(small["cf_b_dw"], j), _row(small["cf_norm"], j))
            act.update(a=a, v0=v0, v1=v1, v3=v3)
            h1 = mm_x_wrow(f"cf_pw2_{i}", v3, get_w("cf_w_pw2", j, v3),
                           extras=[(_row(small["cf_b_pw2"], j), "row"), (h, "tile")],
                           epi=lambda acc, b, res: (res + (acc + b),), tn=1024)[0]
        else:
            bcv = mm_x_wcol(f"sc_in_{i}", u, get_w("sc_w_in", j, u), 0, tn=768)[0]
            y = sc_fwd_mid(f"sc_mid_{i}", bcv, conv_w["sc"][j])
            act.update(bcv=bcv, y=y)
            h1 = mm_x_wrow(f"sc_out_{i}", y, get_w("sc_w_out", j, y), extras=[(h, "tile")],
                           epi=lambda acc, res: (res + acc,), tn=1024)[0]
        act["h1"] = h1
        u2 = rms_fwd(f"rms_mlp_{i}", h1, _row(small["norm_mlp"], i))
        z, hd = mm_x_wcol(f"mlp_w1_{i}", u2, get_w("mlp_w1", i, u2), 0, outs_dtypes=(F32, BF16),
                          epi=lambda acc: (acc, jnp.square(jnp.maximum(acc, 0.0))))
        h2 = mm_x_wrow(f"mlp_w2_{i}", hd, get_w("mlp_w2", i, hd), extras=[(h1, "tile")],
                       epi=lambda acc, res: (res + acc,), tm=1024, tn=256, tk=8192)[0]
        act.update(u2=u2, z=z, hd=hd, h2=h2)
        n3 = rms_fwd(f"rms_ple_{i}", h2, _row(small["norm_ple"], i))
        e = mm_x_wcol(f"ple_proj_{i}", p[i], get_w("ple_w_proj", i, n3), 0)[0]

        def ple_epi(acc, e_t, res):
            g_t = jax.nn.sigmoid(acc)
            return g_t, res + g_t * e_t

        g, h3 = mm_x_wrow(f"ple_gate_{i}", n3, get_w("ple_w_gate", i, e), extras=[(e, "tile"), (h2, "tile")],
                          outs_dtypes=(F32, F32), epi=ple_epi)
        act.update(n3=n3, e=e, g=g)
        acts.append(act)
        h = h3

    loss_part, dh, dg_final = final_loss("final_loss", h, small["norm_final"], target)
    sg = {k: [None] * small[k].shape[0] for k in small if k != "norm_final"}
    sg["norm_final"] = [dg_final]
    sg["cf_w_dw"] = [None] * conv_w["cf"].shape[0]
    sg["sc_w_conv"] = [None] * conv_w["sc"].shape[0]

    for i in reversed(range(depth)):
        j = i // 2
        act = acts[i]
        de, dgl = ple_bwd_elem(f"ple_bwd_{i}", dh, act["g"], act["e"])
        g_proj = mm_xt_dy(f"d_ple_proj_{i}", p[i], de, True)
        g_gate = mm_xt_dy(f"d_ple_gate_{i}", act["n3"], dgl, False)
        dn3 = mm_dy_wrow_t(f"dn3_{i}", dgl, get_w("ple_w_gate", i))[0]
        dh2, dh2b, sg["norm_ple"][i], _ = rms_bwd(f"rms_ple_bwd_{i}", dn3, act["h2"], _row(small["norm_ple"], i), dh)
        g_w2 = mm_xt_dy(f"d_mlp_w2_{i}", act["hd"], dh2b, False)
        dz = mm_dy_wrow_t(f"dz_{i}", dh2b, get_w("mlp_w2", i), extras=[(act["z"], "tile")], outs_dtypes=(BF16,),
                          epi=lambda acc, z_t: (acc * (2.0 * jnp.maximum(z_t, 0.0)),))[0]
        g_w1 = mm_xt_dy(f"d_mlp_w1_{i}", act["u2"], dz, True)
        put_grads({("ple_w_proj", i): g_proj, ("ple_w_gate", i): g_gate, ("mlp_w2", i): g_w2, ("mlp_w1", i): g_w1})
        du2 = mm_dy_wcol_t(f"du2_{i}", dz, get_w("mlp_w1", i), 0)[0]
        dh1, dh1b, sg["norm_mlp"][i], cs1 = rms_bwd(f"rms_mlp_bwd_{i}", du2, act["h1"], _row(small["norm_mlp"], i), dh2)
        if i % 2 == 0:
            g_out = mm_xt_dy(f"d_cf_pw2_{i}", act["v3"], dh1b, False)
            sg["cf_b_pw2"][j] = cs1
            dv3 = mm_dy_wrow_t(f"dv3_{i}", dh1b, get_w("cf_w_pw2", j))[0]
            dv1, sg["cf_norm"][j], sg["cf_b_dw"][j] = cf_bwd_rows(f"cf_bwd_rows_{i}", dv3, act["v1"], _row(small["cf_norm"], j))
            da, sg["cf_w_dw"][j], sg["cf_b_pw1"][j] = cf_bwd_conv(f"cf_bwd_conv_{i}", dv1, act["v0"], act["a"], conv_w["cf"][j])
            g_in = mm_xt_dy(f"d_cf_pw1_{i}", act["u"], da, True)
            put_grads({("cf_w_pw2", j): g_out, ("cf_w_pw1", j): g_in})
            du = mm_dy_wcol_t(f"du_{i}", da, get_w("cf_w_pw1", j), 0)[0]
        else:
            g_out = mm_xt_dy(f"d_sc_out_{i}", act["y"], dh1b, False)
            dy = mm_dy_wrow_t(f"dy_{i}", dh1b, get_w("sc_w_out", j))[0]
            dbcv, sg["sc_w_conv"][j] = sc_bwd_mid(f"sc_bwd_mid_{i}", dy, act["bcv"], conv_w["sc"][j])
            g_in = mm_xt_dy(f"d_sc_in_{i}", act["u"], dbcv, True, tn=768)
            put_grads({("sc_w_out", j): g_out, ("sc_w_in", j): g_in})
            du = mm_dy_wcol_t(f"du_{i}", dbcv, get_w("sc_w_in", j), 0)[0]
        dh, _, sg["norm_mix"][i], _ = rms_bwd(f"rms_mix_bwd_{i}", du, act["h"], _row(small["norm_mix"], i), dh1)
    return loss_part, dh, sg


def _me_and_peers():
    x, y, c = lax.axis_index("x"), lax.axis_index("y"), lax.axis_index("c")
    me = 4 * x + 2 * y + c
    peers = []
    for q in range(1, NDEV):
        px = 1 - x if q & 4 else x
        py = 1 - y if q & 2 else y
        pc = 1 - c if q & 1 else c
        peers.append(((px, py, pc), 4 * px + 2 * py + pc))
    return me, peers


def _exchange(name, srcs, out_shapes, src_fns, dst_fns, after=()):
    n = len(srcs)
    n_after = len(after)

    def body(*refs):
        ins, outs = refs[:n], refs[n + n_after:2 * n + n_after]
        send_sems, recv_sems, local_sems = refs[2 * n + n_after:]
        me, peers = _me_and_peers()
        local, remote = [], []
        for k in range(n):
            cp = pltpu.make_async_copy(src_fns[k](ins[k], me), dst_fns[k](outs[k], me), local_sems.at[k])
            cp.start()
            local.append(cp)
        for q, (peer, peer_blk) in enumerate(peers):
            for k in range(n):
                cp = pltpu.make_async_remote_copy(
                    src_ref=src_fns[k](ins[k], peer_blk), dst_ref=dst_fns[k](outs[k], me),
                    send_sem=send_sems.at[k, q], recv_sem=recv_sems.at[k, q],
                    device_id=peer, device_id_type=MESH)
                cp.start()
                remote.append(cp)
        for q, (peer, peer_blk) in enumerate(peers):
            for k in range(n):
                pltpu.make_async_remote_copy(
                    src_ref=src_fns[k](ins[k], peer_blk), dst_ref=dst_fns[k](outs[k], peer_blk),
                    send_sem=send_sems.at[k, q], recv_sem=recv_sems.at[k, q],
                    device_id=peer, device_id_type=MESH).wait_recv()
        for cp in remote:
            cp.wait_send()
        for cp in local:
            cp.wait()

    any_spec = pl.BlockSpec(memory_space=pl.ANY)
    return pl.pallas_call(
        body,
        name=name,
        in_specs=[any_spec] * (n + n_after),
        out_specs=[any_spec] * n,
        out_shape=out_shapes,
        scratch_shapes=[pltpu.SemaphoreType.DMA((n, N_PEERS)), pltpu.SemaphoreType.DMA((n, N_PEERS)),
                        pltpu.SemaphoreType.DMA((n,))],
    )(*srcs, *after)


def sc_exchange(name, collective_id, srcs, out_shapes, src_fns, dst_fns):
    n = len(srcs)

    def body(*refs):
        ins, outs = refs[:n], refs[n:2 * n]
        send_sems, recv_sems, local_sems = refs[2 * n:]
        me, peers = _me_and_peers()
        barrier = pltpu.get_barrier_semaphore()
        for peer, _ in peers:
            pl.semaphore_signal(barrier, inc=1, device_id=peer, device_id_type=MESH)
        pl.semaphore_wait(barrier, N_PEERS)
        local, remote = [], []
        for k in range(n):
            cp = pltpu.make_async_copy(src_fns[k](ins[k], me), dst_fns[k](outs[k], me), local_sems.at[k])
            cp.start()
            local.append(cp)
        for q, (peer, peer_blk) in enumerate(peers):
            for k in range(n):
                cp = pltpu.make_async_remote_copy(
                    src_ref=src_fns[k](ins[k], peer_blk), dst_ref=dst_fns[k](outs[k], me),
                    send_sem=send_sems.at[k, q], recv_sem=recv_sems.at[k, q],
                    device_id=peer, device_id_type=MESH)
                cp.start()
                remote.append(cp)
        for q, (peer, peer_blk) in enumerate(peers):
            for k in range(n):
                pltpu.make_async_remote_copy(
                    src_ref=src_fns[k](ins[k], peer_blk), dst_ref=dst_fns[k](outs[k], peer_blk),
                    send_sem=send_sems.at[k, q], recv_sem=recv_sems.at[k, q],
                    device_id=peer, device_id_type=MESH).wait_recv()
        for cp in remote:
            cp.wait_send()
        for cp in local:
            cp.wait()

    return pl.kernel(
        body,
        out_type=out_shapes,
        mesh=plsc.ScalarSubcoreMesh(axis_name="sequencer", num_cores=1),
        name=name,
        scratch_types=[pltpu.SemaphoreType.DMA((n, N_PEERS)), pltpu.SemaphoreType.DMA((n, N_PEERS)),
                       pltpu.SemaphoreType.DMA((n,))],
        compiler_params=pltpu.CompilerParams(collective_id=collective_id),
    )(*srcs)


def sc_gather(name, collective_id, srcs, layers):
    n = len(srcs)
    outs_shape = [jax.ShapeDtypeStruct((NDEV, 1) + a.shape[1:], a.dtype) for a in srcs]

    def body(*refs):
        ins, outs = refs[:n], refs[n:2 * n]
        send_sems, recv_sems, local_sems = refs[2 * n:]
        x, y, c = lax.axis_index("x"), lax.axis_index("y"), lax.axis_index("c")
        me = 4 * x + 2 * y + c
        sibling = (x, y, 1 - c)
        chips = [(1 - x, y), (x, 1 - y), (1 - x, 1 - y)]
        barrier = pltpu.get_barrier_semaphore()
        for peer in [sibling] + [(cx, cy, c) for cx, cy in chips]:
            pl.semaphore_signal(barrier, inc=1, device_id=peer, device_id_type=MESH)
        pl.semaphore_wait(barrier, 1 + len(chips))

        def copy(k, slot, blk, to, src=None):
            place = outs[k].at[blk, 0]
            return pltpu.make_async_remote_copy(
                src_ref=place if src is None else src, dst_ref=place,
                send_sem=send_sems.at[k, slot], recv_sem=recv_sems.at[k, slot],
                device_id=to, device_id_type=MESH)

        local, sent = [], []
        for k in range(n):
            mine = ins[k].at[layers[k]]
            cp = pltpu.make_async_copy(mine, outs[k].at[me, 0], local_sems.at[k])
            cp.start()
            local.append(cp)
            sent.append(copy(k, 0, me, sibling, src=mine))
            sent += [copy(k, 1 + j, me, (cx, cy, c), src=mine) for j, (cx, cy) in enumerate(chips)]
        for cp in sent:
            cp.start()
        for k in range(n):
            for j, (cx, cy) in enumerate(chips):
                blk = 4 * cx + 2 * cy + c
                copy(k, 1 + j, blk, sibling).wait_recv()
                fwd = copy(k, 4 + j, blk, sibling)
                fwd.start()
                sent.append(fwd)
        for k in range(n):
            copy(k, 0, 4 * x + 2 * y + (1 - c), sibling).wait_recv()
            for j, (cx, cy) in enumerate(chips):
                copy(k, 4 + j, 4 * cx + 2 * cy + (1 - c), sibling).wait_recv()
        for cp in sent:
            cp.wait_send()
        for cp in local:
            cp.wait()

    return pl.kernel(
        body,
        out_type=outs_shape,
        mesh=plsc.ScalarSubcoreMesh(axis_name="sequencer", num_cores=1),
        name=name,
        scratch_types=[pltpu.SemaphoreType.DMA((n, N_PEERS)), pltpu.SemaphoreType.DMA((n, N_PEERS)),
                       pltpu.SemaphoreType.DMA((n,))],
        compiler_params=pltpu.CompilerParams(collective_id=collective_id),
    )(*srcs)


def _gather_src(layer):
    return lambda ref, blk: ref.at[layer]


def _gather_dst(ref, blk):
    return ref.at[blk, 0]


def _slice_of(ref, blk):
    return ref.at[blk]


def cast_bf16(name, w, tr_elems=512 * 1024):
    l, r, c = w.shape
    tr = _row_tile(r, tr_elems // c)
    spec = pl.BlockSpec((None, tr, c), lambda li, i: (li, i, 0))

    def body(w_ref, o_ref):
        o_ref[...] = w_ref[...].astype(BF16)

    return pl.pallas_call(
        body, name=name, grid=(l, r // tr), in_specs=[spec], out_specs=spec,
        out_shape=jax.ShapeDtypeStruct(w.shape, BF16),
        compiler_params=_params(("parallel", "parallel"), 6 * tr * c),
    )(w)


def _adamw_math(w, g, m, v):
    m = ADAM_B1 * m + (1.0 - ADAM_B1) * g
    v = ADAM_B2 * v + (1.0 - ADAM_B2) * (g * g)
    m_hat = m * (1.0 / (1.0 - ADAM_B1 ** ADAM_STEP))
    v_hat = v * (1.0 / (1.0 - ADAM_B2 ** ADAM_STEP))
    delta = -ADAM_LR * (m_hat / (jnp.sqrt(v_hat) + ADAM_EPS) + ADAM_WD * w)
    return delta, m, v


def _sum_blocks(ref):
    g = ref[0].astype(F32)
    for d in range(1, ref.shape[0]):
        g = g + ref[d].astype(F32)
    return g


def adamw_layer(name, recv, w, m, v, layer, stacked, after=None, tr_elems=256 * 1024):
    nd, r, c = recv.shape
    tr = _row_tile(r, tr_elems // c)
    r_spec = pl.BlockSpec((nd, tr, c), lambda i: (0, i, 0))
    w_spec = pl.BlockSpec((None, tr, c), lambda i: (layer, i, 0))
    if stacked is None:
        stacked = [lax.empty(w.shape, F32) for _ in range(4)]
    after = [] if after is None else [after]

    def body(r_ref, w_ref, m_ref, v_ref, g_in, d_in, m_in, v_in, *rest):
        g_out, d_out, m_out, v_out = rest[len(after):]
        g = _sum_blocks(r_ref)
        delta, m_new, v_new = _adamw_math(w_ref[...], g, m_ref[...], v_ref[...])
        g_out[...] = g
        d_out[...] = delta
        m_out[...] = m_new
        v_out[...] = v_new

    out = jax.ShapeDtypeStruct(w.shape, F32)
    return pl.pallas_call(
        body, name=name, grid=(r // tr,),
        in_specs=[r_spec, w_spec, w_spec, w_spec] + [pl.BlockSpec(memory_space=pl.ANY)] * (4 + len(after)),
        out_specs=[w_spec] * 4, out_shape=[out] * 4,
        input_output_aliases={4: 0, 5: 1, 6: 2, 7: 3},
        compiler_params=_params(("parallel",), tr * c * (2 * nd + 7 * 4)),
    )(recv, w, m, v, *stacked, *after)


def pack_small_grads(name, parts, taps, n_blocks):
    d = parts[0].shape[1]
    n_p, rows = len(parts), [t.shape[0] for t in taps]
    cb = d // n_blocks

    def body(*refs):
        part_refs, tap_refs = refs[:n_p], refs[n_p:n_p + len(taps)]
        sums_out, taps_out = refs[n_p + len(taps):]
        for i, r in enumerate(part_refs):
            sums_out[pl.ds(i, 1), :] = jnp.sum(r[...], axis=0, keepdims=True)
        r0 = 0
        for t_ref, n in zip(tap_refs, rows):
            for b in range(n_blocks):
                taps_out[b, pl.ds(r0, n), :] = t_ref[:, pl.ds(b * cb, cb)]
            r0 += n

    vm = pl.BlockSpec(memory_space=pltpu.VMEM)
    return pl.pallas_call(
        body, name=name, in_specs=[vm] * (n_p + len(taps)), out_specs=[vm] * 2,
        out_shape=[jax.ShapeDtypeStruct((n_p, d), F32), jax.ShapeDtypeStruct((n_blocks, sum(rows), cb), F32)],
    )(*parts, *taps)


def small_update(name, part_g, tap_g, w_a, m_a, v_a, w_b, m_b, v_b):
    nd, na, d = part_g.shape
    nb, cb = w_b.shape

    def body(pg_ref, tg_ref, wa_ref, ma_ref, va_ref, wb_ref, mb_ref, vb_ref,
             ga_out, da_out, ma_out, va_out, gb_out, db_out, mb_out, vb_out, loss_out):
        ga = _sum_blocks(pg_ref)
        delta, m_new, v_new = _adamw_math(wa_ref[...], ga, ma_ref[...], va_ref[...])
        ga_out[...] = ga
        da_out[...] = delta
        ma_out[...] = m_new
        va_out[...] = v_new
        loss_out[...] = jnp.broadcast_to(jnp.sum(ga[na - 1:na, :], axis=1, keepdims=True), loss_out.shape)
        gb = _sum_blocks(tg_ref)
        delta, m_new, v_new = _adamw_math(wb_ref[...], gb, mb_ref[...], vb_ref[...])
        gb_out[...] = gb
        db_out[...] = delta
        mb_out[...] = m_new
        vb_out[...] = v_new

    oa, ob = jax.ShapeDtypeStruct((na, d), F32), jax.ShapeDtypeStruct((nb, cb), F32)
    vm = pl.BlockSpec(memory_space=pltpu.VMEM)
    return pl.pallas_call(
        body, name=name, in_specs=[vm] * 8, out_specs=[vm] * 9,
        out_shape=[oa] * 4 + [ob] * 4 + [jax.ShapeDtypeStruct((1, LANES), F32)],
        compiler_params=pltpu.CompilerParams(vmem_limit_bytes=_vmem_limit(_nbytes(part_g.shape, F32))),
    )(part_g, tap_g, w_a, m_a, v_a, w_b, m_b, v_b)


BIG = ("cf_w_pw1", "cf_w_pw2", "sc_w_in", "sc_w_out", "mlp_w1", "mlp_w2", "ple_w_proj", "ple_w_gate")
COL_SHARDED = ("cf_w_pw1", "sc_w_in", "mlp_w1", "ple_w_proj")
WEIGHT_ORDER = ("norm_mix", "norm_mlp", "norm_ple", "cf_w_pw1", "cf_b_pw1", "cf_w_dw", "cf_b_dw", "cf_norm",
                "cf_w_pw2", "cf_b_pw2", "sc_w_in", "sc_w_conv", "sc_w_out", "mlp_w1", "mlp_w2", "ple_w_proj",
                "ple_w_gate", "norm_final")
SMALL_ROWS = (("norm_mix", 4), ("norm_mlp", 4), ("norm_ple", 4), ("cf_b_pw1", 4), ("cf_b_dw", 2), ("cf_norm", 2),
              ("cf_b_pw2", 2), ("norm_final", 1))


def _layer_weights(i):
    mixer = (("cf_w_pw1", i // 2), ("cf_w_pw2", i // 2)) if i % 2 == 0 else (("sc_w_in", i // 2), ("sc_w_out", i // 2))
    return mixer + (("mlp_w1", i), ("mlp_w2", i), ("ple_w_proj", i), ("ple_w_gate", i))


def _pad_rows(a, rows):
    return jnp.pad(a, ((0, 0), (0, rows - a.shape[1]), (0, 0)))


def _pack_taps(cf, sc):
    c = cf.shape[2]
    return jnp.concatenate([_pad_rows(cf, CONV_A_HALO).reshape(-1, c), _pad_rows(sc, CONV_B_HALO).reshape(-1, c)], axis=0)


def _unpack_taps(t, n_cf):
    c = t.shape[1]
    cf = t[:n_cf * CONV_A_HALO].reshape(n_cf, CONV_A_HALO, c)[:, :CONV_A_TAPS]
    sc = t[n_cf * CONV_A_HALO:].reshape(-1, CONV_B_HALO, c)[:, :CONV_B_TAPS]
    return cf, sc


def _pack_small(vals, d):
    return jnp.concatenate([vals[k].reshape(-1, d) for k, _ in SMALL_ROWS] + [jnp.zeros((1, d), F32)], axis=0)


def _unpack_small(a, shapes):
    out, r = {}, 0
    for k, n in SMALL_ROWS:
        out[k] = a[r:r + n].reshape(shapes[k])
        r += n
    return out


def kernel(x, p, norm_mix, norm_mlp, norm_ple, cf_w_pw1, cf_b_pw1, cf_w_dw, cf_b_dw, cf_norm, cf_w_pw2, cf_b_pw2, sc_w_in, sc_w_conv, sc_w_out, mlp_w1, mlp_w2, ple_w_proj, ple_w_gate, norm_final, loss_target, m_norm_mix, m_norm_mlp, m_norm_ple, m_cf_w_pw1, m_cf_b_pw1, m_cf_w_dw, m_cf_b_dw, m_cf_norm, m_cf_w_pw2, m_cf_b_pw2, m_sc_w_in, m_sc_w_conv, m_sc_w_out, m_mlp_w1, m_mlp_w2, m_ple_w_proj, m_ple_w_gate, m_norm_final, v_norm_mix, v_norm_mlp, v_norm_ple, v_cf_w_pw1, v_cf_b_pw1, v_cf_w_dw, v_cf_b_dw, v_cf_norm, v_cf_w_pw2, v_cf_b_pw2, v_sc_w_in, v_sc_w_conv, v_sc_w_out, v_mlp_w1, v_mlp_w2, v_ple_w_proj, v_ple_w_gate, v_norm_final):
    w = dict(norm_mix=norm_mix, norm_mlp=norm_mlp, norm_ple=norm_ple, cf_w_pw1=cf_w_pw1, cf_b_pw1=cf_b_pw1,
             cf_w_dw=cf_w_dw, cf_b_dw=cf_b_dw, cf_norm=cf_norm, cf_w_pw2=cf_w_pw2, cf_b_pw2=cf_b_pw2,
             sc_w_in=sc_w_in, sc_w_conv=sc_w_conv, sc_w_out=sc_w_out, mlp_w1=mlp_w1, mlp_w2=mlp_w2,
             ple_w_proj=ple_w_proj, ple_w_gate=ple_w_gate, norm_final=norm_final)
    m = dict(norm_mix=m_norm_mix, norm_mlp=m_norm_mlp, norm_ple=m_norm_ple, cf_w_pw1=m_cf_w_pw1, cf_b_pw1=m_cf_b_pw1,
             cf_w_dw=m_cf_w_dw, cf_b_dw=m_cf_b_dw, cf_norm=m_cf_norm, cf_w_pw2=m_cf_w_pw2, cf_b_pw2=m_cf_b_pw2,
             sc_w_in=m_sc_w_in, sc_w_conv=m_sc_w_conv, sc_w_out=m_sc_w_out, mlp_w1=m_mlp_w1, mlp_w2=m_mlp_w2,
             ple_w_proj=m_ple_w_proj, ple_w_gate=m_ple_w_gate, norm_final=m_norm_final)
    v = dict(norm_mix=v_norm_mix, norm_mlp=v_norm_mlp, norm_ple=v_norm_ple, cf_w_pw1=v_cf_w_pw1, cf_b_pw1=v_cf_b_pw1,
             cf_w_dw=v_cf_w_dw, cf_b_dw=v_cf_b_dw, cf_norm=v_cf_norm, cf_w_pw2=v_cf_w_pw2, cf_b_pw2=v_cf_b_pw2,
             sc_w_in=v_sc_w_in, sc_w_conv=v_sc_w_conv, sc_w_out=v_sc_w_out, mlp_w1=v_mlp_w1, mlp_w2=v_mlp_w2,
             ple_w_proj=v_ple_w_proj, ple_w_gate=v_ple_w_gate, norm_final=v_norm_final)
    depth, d = norm_mix.shape
    n_cf = cf_w_dw.shape[0]

    taps_w = _pack_taps(cf_w_dw, sc_w_conv)
    shards, gathered = {}, {}
    ids = iter(range(5 * depth))
    for i in range(depth):
        names = _layer_weights(i)
        groups = [names[:1], names[1:2], names[2:3], names[3:]] if i == 0 else [names[:2], names[2:]]
        for n_group, group in enumerate(groups):
            with_taps = i == 0 and n_group == 1
            for k, _ in group:
                if k not in shards:
                    shards[k] = cast_bf16(f"cast_{k}", w[k])
            srcs = [shards[k] for k, _ in group] + ([taps_w[None]] if with_taps else [])
            got = sc_gather(f"gather_{i}_{n_group}", next(ids), srcs, [l for _, l in group] + ([0] if with_taps else []))
            for (k, l), g in zip(group, got):
                gathered[(k, l)] = g if k in COL_SHARDED else g.reshape(-1, g.shape[3])
            if with_taps:
                taps_full = jnp.transpose(got[-1][:, 0], (1, 0, 2)).reshape(taps_w.shape[0], d)
    conv_w = {"cf": taps_full[:n_cf * CONV_A_HALO].reshape(n_cf, CONV_A_HALO, d),
              "sc": taps_full[n_cf * CONV_A_HALO:].reshape(-1, CONV_B_HALO, d)}

    def get_w(k, l, after=None):
        return gathered[(k, l)]

    received, waiting = {}, {}
    last_group = list(_layer_weights(0)[:2])

    def put_grads(grads):
        names = list(grads)
        if names[0][0] in ("cf_w_pw2", "sc_w_out") and set(names) != set(last_group):
            waiting.update(grads)
            return
        grads = {**waiting, **grads}
        waiting.clear()
        names = list(grads)
        got = sc_exchange(f"grad_exchange_{names[-1][0]}_{names[-1][1]}", next(ids), [grads[n] for n in names],
                          [jax.ShapeDtypeStruct(grads[n].shape, BF16) for n in names],
                          [_slice_of] * len(names), [_slice_of] * len(names))
        received.update(zip(names, got))

    small = {k: w[k] for k, _ in SMALL_ROWS}
    small["norm_final"] = norm_final[None]
    loss_part, grad_x, sg = _local_step(x[0], p[:, 0], loss_target[0], small, get_w, conv_w, put_grads)

    out = {k: None for k in BIG}
    previous = None
    for (k, l), recv in received.items():
        if (k, l) not in last_group:
            out[k] = adamw_layer(f"adamw_{k}_{l}", recv, w[k], m[k], v[k], l, out[k], after=previous)
            previous = out[k][1]
    updated_first = [out[k][0] for k in BIG if out[k] is not None and k not in [n for n, _ in last_group]]

    parts = []
    for k, n in SMALL_ROWS:
        for g in sg[k]:
            parts += [g[:, :d], g[:, d:]] if g.shape[1] == 2 * d else [g]
    parts.append(loss_part)
    sums, tap_slices = pack_small_grads("pack_small_grads", parts, sg["cf_w_dw"] + sg["sc_w_conv"], NDEV)
    part_all, tap_mine = _exchange(
        "small_exchange", [sums[None], tap_slices],
        [jax.ShapeDtypeStruct((NDEV, 1) + sums.shape, F32), jax.ShapeDtypeStruct(tap_slices.shape, F32)],
        [_gather_src(0), _slice_of], [_gather_dst, _slice_of], after=updated_first)
    for k, l in last_group:
        out[k] = adamw_layer(f"adamw_{k}_{l}", received[(k, l)], w[k], m[k], v[k], l, out[k], after=part_all)
    sm = small_update("small_update", part_all[:, 0], tap_mine,
                      _pack_small(w, d), _pack_small(m, d), _pack_small(v, d),
                      taps_w, _pack_taps(m["cf_w_dw"], m["sc_w_conv"]), _pack_taps(v["cf_w_dw"], v["sc_w_conv"]))
    shapes = {k: w[k].shape for k, _ in SMALL_ROWS}
    for t in range(4):
        un = _unpack_small(sm[t], shapes)
        cf_t, sc_t = _unpack_taps(sm[4 + t], n_cf)
        for k in un:
            out.setdefault(k, [None] * 4)[t] = un[k]
        out.setdefault("cf_w_dw", [None] * 4)[t] = cf_t
        out.setdefault("sc_w_conv", [None] * 4)[t] = sc_t
    loss = sm[8][0, 0]

    return (loss, grad_x[None], *[out[k][0] for k in WEIGHT_ORDER], *[out[k][1] for k in WEIGHT_ORDER],
            *[out[k][2] for k in WEIGHT_ORDER], *[out[k][3] for k in WEIGHT_ORDER])
```

```python
import jax
import jax.numpy as jnp
from jax import lax
from jax.experimental import pallas as pl
from jax.experimental.pallas import tpu as pltpu
from jax.experimental.pallas import tpu_sc as plsc

F32 = jnp.float32
BF16 = jnp.bfloat16
EPS = 1e-6
NDEV = 8
N_PEERS = NDEV - 1
MESH = pl.DeviceIdType.MESH

ADAM_LR = 0.001
ADAM_B1 = 0.9
ADAM_B2 = 0.999
ADAM_EPS = 1e-08
ADAM_WD = 0.01
ADAM_STEP = 10

V7X_VMEM_BYTES = 64 * 1024 * 1024
VMEM_LIMIT_MAX = 56 * 1024 * 1024
SUBLANES = 8
LANES = 128
CONV_A_TAPS = 31
CONV_A_HALO = 32
CONV_B_TAPS = 3
CONV_B_HALO = 8


def _nbytes(shape, dtype):
    n = 1
    for s in shape:
        if s is not None:
            n *= s
    return n * jnp.dtype(dtype).itemsize


def _vmem_limit(block_bytes, scratch_bytes=0):
    need = 2 * block_bytes + scratch_bytes
    return int(min(VMEM_LIMIT_MAX, max(32 * 1024 * 1024, need + need // 2 + (4 << 20))))


def _params(sem, block_bytes, scratch_bytes=0):
    return pltpu.CompilerParams(dimension_semantics=sem, vmem_limit_bytes=_vmem_limit(block_bytes, scratch_bytes))


_DIMS = {
    "nn": (((1,), (0,)), ((), ())),
    "nt": (((1,), (1,)), ((), ())),
    "tn": (((0,), (0,)), ((), ())),
}


def _mm(name, dims, grid, acc_shape, a, a_spec, b, b_spec, extras, outs, epi):
    ni, nj, nk = grid
    n_ex, n_out = len(extras), len(outs)
    dn = _DIMS[dims]
    b_sub = [s for s in b_spec.block_shape if s is not None]
    n_sub = b_sub[0] if len(b_sub) == 3 else 1

    def body(*refs):
        a_ref, b_ref = refs[0], refs[1]
        ex_refs = refs[2:2 + n_ex]
        out_refs = refs[2 + n_ex:2 + n_ex + n_out]
        if n_sub == 1:
            d = lax.dot_general(a_ref[...].astype(BF16), b_ref[...].astype(BF16), dn, preferred_element_type=F32)
        else:
            w_sub = a_ref.shape[1] // n_sub
            d = None
            for s in range(n_sub):
                part = lax.dot_general(a_ref[:, pl.ds(s * w_sub, w_sub)].astype(BF16), b_ref[s].astype(BF16), dn,
                                       preferred_element_type=F32)
                d = part if d is None else d + part

        def finish(acc):
            res = epi(acc, *[r[...] for r in ex_refs])
            for o_ref, r in zip(out_refs, res):
                o_ref[...] = r.astype(o_ref.dtype)

        if nk == 1:
            finish(d)
        else:
            acc_ref = refs[2 + n_ex + n_out]
            k = pl.program_id(2)

            @pl.when(k == 0)
            def _():
                acc_ref[...] = d

            @pl.when(jnp.logical_and(k > 0, k < nk - 1))
            def _():
                acc_ref[...] += d

            @pl.when(k == nk - 1)
            def _():
                finish(acc_ref[...] + d)

    blk = _nbytes(a_spec.block_shape, a.dtype) + _nbytes(b_spec.block_shape, b.dtype)
    for arr, spec in list(extras) + list(outs):
        blk += _nbytes(spec.block_shape, arr.dtype)
    acc_bytes = _nbytes(acc_shape, F32)
    scratch = [pltpu.VMEM(acc_shape, F32)] if nk > 1 else []
    return pl.pallas_call(
        body,
        name=name,
        grid=grid,
        in_specs=[a_spec, b_spec] + [s for _, s in extras],
        out_specs=[s for _, s in outs],
        out_shape=[o for o, _ in outs],
        scratch_shapes=scratch,
        compiler_params=_params(("parallel", "parallel", "arbitrary"), blk, 3 * acc_bytes),
    )(a, b, *[e for e, _ in extras])


def _tile(n, pref):
    if n <= pref:
        return n
    t = pref - pref % LANES
    while t > LANES and n % t:
        t -= LANES
    assert n % t == 0, (n, pref)
    return t


def _row_tile(n, pref):
    if n <= pref:
        return n
    t = max(SUBLANES, pref - pref % SUBLANES)
    while t > SUBLANES and n % t:
        t -= SUBLANES
    assert n % t == 0, (n, pref)
    return t


def _id_epi(acc):
    return (acc,)


def mm_x_wcol(name, x, w, layer, extras=(), outs_dtypes=(F32,), epi=_id_epi, tm=1024, tn=1024):
    m, kdim = x.shape
    c = w.shape[3]
    n = NDEV * c
    tm, tn = _tile(m, tm), _tile(c, tn)
    tk = _tile(kdim, 2048)
    grid = (m // tm, n // tn, kdim // tk)
    per = c // tn
    a_spec = pl.BlockSpec((tm, tk), lambda i, j, k: (i, k))
    b_spec = pl.BlockSpec((None, None, tk, tn), lambda i, j, k: (j // per, layer, k, j % per))
    ex = [(e, _ex_spec(e, kind, tm, tn)) for e, kind in extras]
    o_spec = pl.BlockSpec((tm, tn), lambda i, j, k: (i, j))
    outs = [(jax.ShapeDtypeStruct((m, n), dt), o_spec) for dt in outs_dtypes]
    return _mm(name, "nn", grid, (tm, tn), x, a_spec, w, b_spec, ex, outs, epi)


def mm_x_wrow(name, x, w, extras=(), outs_dtypes=(F32,), epi=_id_epi, tm=1024, tn=512, tk=2048):
    m, kdim = x.shape
    n = w.shape[1]
    assert kdim == w.shape[0]
    tm, tn = _tile(m, tm), _tile(n, tn)
    tk = _tile(kdim, tk)
    grid = (m // tm, n // tn, kdim // tk)
    a_spec = pl.BlockSpec((tm, tk), lambda i, j, k: (i, k))
    b_spec = pl.BlockSpec((tk, tn), lambda i, j, k: (k, j))
    ex = [(e, _ex_spec(e, kind, tm, tn)) for e, kind in extras]
    o_spec = pl.BlockSpec((tm, tn), lambda i, j, k: (i, j))
    outs = [(jax.ShapeDtypeStruct((m, n), dt), o_spec) for dt in outs_dtypes]
    return _mm(name, "nn", grid, (tm, tn), x, a_spec, w, b_spec, ex, outs, epi)


def mm_dy_wcol_t(name, dy, w, layer, extras=(), outs_dtypes=(F32,), epi=_id_epi, tm=1024, b_block_bytes=4 << 20):
    m, n = dy.shape
    nd, kdim, c = w.shape[0], w.shape[2], w.shape[3]
    assert n == nd * c
    tm, tn = _tile(m, tm), _tile(kdim, max(LANES, b_block_bytes // (n * 2)))
    n_sub = nd
    grid = (m // tm, kdim // tn, nd // n_sub)
    a_spec = pl.BlockSpec((tm, n_sub * c), lambda i, j, k: (i, k))
    if n_sub > 1:
        b_spec = pl.BlockSpec((n_sub, None, tn, c), lambda i, j, k: (k, layer, j, 0))
    else:
        b_spec = pl.BlockSpec((None, None, tn, c), lambda i, j, k: (k, layer, j, 0))
    ex = [(e, _ex_spec(e, kind, tm, tn)) for e, kind in extras]
    o_spec = pl.BlockSpec((tm, tn), lambda i, j, k: (i, j))
    outs = [(jax.ShapeDtypeStruct((m, kdim), dt), o_spec) for dt in outs_dtypes]
    return _mm(name, "nt", grid, (tm, tn), dy, a_spec, w, b_spec, ex, outs, epi)


def mm_dy_wrow_t(name, dy, w, extras=(), outs_dtypes=(F32,), epi=_id_epi, tm=1024, tn=1024):
    m, n = dy.shape
    kdim = w.shape[0]
    assert n == w.shape[1]
    tm, tn = _tile(m, tm), _tile(kdim, tn)
    tk = _tile(n, 2048)
    grid = (m // tm, kdim // tn, n // tk)
    a_spec = pl.BlockSpec((tm, tk), lambda i, j, k: (i, k))
    b_spec = pl.BlockSpec((tn, tk), lambda i, j, k: (j, k))
    ex = [(e, _ex_spec(e, kind, tm, tn)) for e, kind in extras]
    o_spec = pl.BlockSpec((tm, tn), lambda i, j, k: (i, j))
    outs = [(jax.ShapeDtypeStruct((m, kdim), dt), o_spec) for dt in outs_dtypes]
    return _mm(name, "nt", grid, (tm, tn), dy, a_spec, w, b_spec, ex, outs, epi)


def mm_xt_dy(name, x, dy, col_shards, tm=1024, tn=1024):
    m, kdim = x.shape
    n = dy.shape[1]
    tk = _tile(m, 4096)
    if col_shards:
        c = n // NDEV
        tm, tn = _tile(kdim, tm), _tile(c, tn)
        per = c // tn
        out = jax.ShapeDtypeStruct((NDEV, kdim, c), BF16)
        o_spec = pl.BlockSpec((None, tm, tn), lambda i, j, k: (j // per, i, j % per))
    else:
        tm, tn = _tile(kdim, tm), _tile(n, tn)
        out = jax.ShapeDtypeStruct((kdim, n), BF16)
        o_spec = pl.BlockSpec((tm, tn), lambda i, j, k: (i, j))
    grid = (kdim // tm, n // tn, m // tk)
    a_spec = pl.BlockSpec((tk, tm), lambda i, j, k: (k, i))
    b_spec = pl.BlockSpec((tk, tn), lambda i, j, k: (k, j))
    g = _mm(name, "tn", grid, (tm, tn), x, a_spec, dy, b_spec, [], [(out, o_spec)], _id_epi)[0]
    return g if col_shards else g.reshape(NDEV, kdim // NDEV, n)


def _ex_spec(e, kind, tm, tn):
    if kind == "tile":
        return pl.BlockSpec((tm, tn), lambda i, j, k: (i, j))
    if kind == "row":
        return pl.BlockSpec((1, tn), lambda i, j, k: (0, j))
    raise ValueError(kind)


def _rows_call(name, body, n_rows, ts, ins, outs, scratch=(), scratch_bytes=0):
    blk = sum(_nbytes(s.block_shape, a.dtype) for a, s in list(ins) + list(outs))
    return pl.pallas_call(
        body,
        name=name,
        grid=(n_rows // ts,),
        in_specs=[s for _, s in ins],
        out_specs=[s for _, s in outs],
        out_shape=[o for o, _ in outs],
        scratch_shapes=list(scratch),
        compiler_params=_params(("arbitrary",), blk, scratch_bytes + 4 * blk // 2),
    )(*[a for a, _ in ins])


def _blk(ts, d):
    return pl.BlockSpec((ts, d), lambda i: (i, 0))


def _full(shape):
    return pl.BlockSpec(shape, lambda i: tuple(0 for _ in shape))


def _rowsum8(v):
    t, d = v.shape
    return jnp.sum(v.reshape(t // SUBLANES, SUBLANES, d), axis=0)


def _accumulate(ref, val):
    @pl.when(pl.program_id(0) == 0)
    def _():
        ref[...] = val

    @pl.when(pl.program_id(0) > 0)
    def _():
        ref[...] += val


def _rstd(x):
    return lax.rsqrt(jnp.mean(x * x, axis=-1, keepdims=True) + EPS)


def _rms_bwd_math(dy, x, g):
    r = _rstd(x)
    gdy = dy * g
    c = jnp.mean(gdy * x, axis=-1, keepdims=True)
    dx = r * gdy - x * (r * r * r * c)
    return dx, dy * (x * r)


def rms_fwd(name, h, g, ts=512):
    s, d = h.shape
    ts = min(ts, s)

    def body(h_ref, g_ref, u_ref):
        x = h_ref[...]
        u_ref[...] = ((x * _rstd(x)) * g_ref[...]).astype(BF16)

    return _rows_call(name, body, s, ts, [(h, _blk(ts, d)), (g, _full((1, d)))],
                      [(jax.ShapeDtypeStruct((s, d), BF16), _blk(ts, d))])[0]


def rms_bwd(name, du, h, g, dres, ts=256):
    s, d = h.shape
    ts = min(ts, s)

    def body(du_ref, h_ref, g_ref, dres_ref, dh_ref, dhb_ref, dg_ref, cs_ref):
        dx, dg = _rms_bwd_math(du_ref[...], h_ref[...], g_ref[...])
        dh = dres_ref[...] + dx
        dh_ref[...] = dh
        dhb_ref[...] = dh.astype(BF16)
        _accumulate(dg_ref, _rowsum8(dg))
        _accumulate(cs_ref, _rowsum8(dh))

    return _rows_call(
        name, body, s, ts,
        [(du, _blk(ts, d)), (h, _blk(ts, d)), (g, _full((1, d))), (dres, _blk(ts, d))],
        [(jax.ShapeDtypeStruct((s, d), F32), _blk(ts, d)), (jax.ShapeDtypeStruct((s, d), BF16), _blk(ts, d)),
         (jax.ShapeDtypeStruct((SUBLANES, d), F32), _full((SUBLANES, d))),
         (jax.ShapeDtypeStruct((SUBLANES, d), F32), _full((SUBLANES, d)))])


def final_loss(name, h, g, target, ts=256):
    s, d = h.shape
    ts = min(ts, s)

    def body(h_ref, g_ref, t_ref, loss_ref, dh_ref, dg_ref):
        x = h_ref[...]
        gf = g_ref[...]
        y = (x * _rstd(x)) * gf
        err = y - t_ref[...]
        _accumulate(loss_ref, _rowsum8(err * err) * (0.5 / d))
        dx, dg = _rms_bwd_math(err * (1.0 / d), x, gf)
        dh_ref[...] = dx
        _accumulate(dg_ref, _rowsum8(dg))

    return _rows_call(
        name, body, s, ts,
        [(h, _blk(ts, d)), (g, _full((1, d))), (target, _blk(ts, d))],
        [(jax.ShapeDtypeStruct((SUBLANES, d), F32), _full((SUBLANES, d))),
         (jax.ShapeDtypeStruct((s, d), F32), _blk(ts, d)),
         (jax.ShapeDtypeStruct((SUBLANES, d), F32), _full((SUBLANES, d)))])


def ple_bwd_elem(name, dh, g, e, ts=512):
    s, d = dh.shape
    ts = min(ts, s)

    def body(dh_ref, g_ref, e_ref, de_ref, dgl_ref):
        dh_v, g_v = dh_ref[...], g_ref[...].astype(F32)
        de_ref[...] = (dh_v * g_v).astype(BF16)
        dgl_ref[...] = (dh_v * e_ref[...] * (g_v * (1.0 - g_v))).astype(BF16)

    return _rows_call(name, body, s, ts, [(dh, _blk(ts, d)), (g, _blk(ts, d)), (e, _blk(ts, d))],
                      [(jax.ShapeDtypeStruct((s, d), BF16), _blk(ts, d)),
                       (jax.ShapeDtypeStruct((s, d), BF16), _blk(ts, d))])


CONV_LANES = 256
CONV_ROWS = 64


def _lane_chunks(d, fn):
    lc = min(CONV_LANES, d)

    def lane_body(c, carry):
        fn(pl.ds(pl.multiple_of(c * lc, lc), lc))
        return carry

    lax.fori_loop(0, d // lc, lane_body, 0)


def _shifted_copies(buf, sh, lanes):
    rows = buf.shape[0] - SUBLANES
    for s in range(1, SUBLANES):
        sh[s, pl.ds(0, rows), :] = buf[pl.ds(s, rows), lanes]


def _window(buf, sh, lanes, start, rows):
    s = start % SUBLANES
    if s == 0:
        return buf[pl.ds(start, rows), lanes]
    return sh[s, pl.ds(start - s, rows), :]


def _prev_halo_spec(ts, halo, width):
    per = ts // halo
    return pl.BlockSpec((halo, width), lambda i: (jnp.maximum(i * per - 1, 0), 0))


def _next_halo_spec(ts, halo, width, n_rows):
    per = ts // halo
    last = n_rows // halo - 1
    return pl.BlockSpec((halo, width), lambda i: (jnp.minimum((i + 1) * per, last), 0))


def cf_fwd_mid(name, a, w_dw, b_dw, gn, ts=256):
    s, d2 = a.shape
    d = d2 // 2
    ts = min(ts, s)
    hl = CONV_A_HALO
    off = hl - (CONV_A_TAPS - 1)

    rc = min(CONV_ROWS, ts)

    def body(a_ref, ah_ref, w_ref, b_ref, gn_ref, v0_ref, v1_ref, v3_ref, buf, sh):
        first = pl.program_id(0) == 0
        halo = ah_ref[...]
        hv0 = halo[:, :d] * jax.nn.sigmoid(halo[:, d:])
        buf[pl.ds(0, hl), :] = jnp.where(first, 0.0, hv0)
        main = a_ref[...]
        v0 = main[:, :d] * jax.nn.sigmoid(main[:, d:])
        buf[pl.ds(hl, ts), :] = v0
        v0_ref[...] = v0

        def conv(lanes):
            _shifted_copies(buf, sh, lanes)
            for r0 in range(0, ts, rc):
                acc = jnp.zeros((rc, lanes.size), F32)
                for k in range(CONV_A_TAPS):
                    acc = acc + w_ref[pl.ds(k, 1), lanes] * _window(buf, sh, lanes, r0 + off + k, rc)
                v1_ref[pl.ds(r0, rc), lanes] = acc + b_ref[:, lanes]

        _lane_chunks(d, conv)
        v1 = v1_ref[...]
        v2 = (v1 * _rstd(v1)) * gn_ref[...]
        v3_ref[...] = (v2 * jax.nn.sigmoid(v2)).astype(BF16)

    return _rows_call(
        name, body, s, ts,
        [(a, _blk(ts, d2)), (a, _prev_halo_spec(ts, hl, d2)), (w_dw, _full(w_dw.shape)),
         (b_dw, _full((1, d))), (gn, _full((1, d)))],
        [(jax.ShapeDtypeStruct((s, d), F32), _blk(ts, d)), (jax.ShapeDtypeStruct((s, d), F32), _blk(ts, d)),
         (jax.ShapeDtypeStruct((s, d), BF16), _blk(ts, d))],
        scratch=[pltpu.VMEM((hl + ts, d), F32), pltpu.VMEM((SUBLANES, hl + ts, min(CONV_LANES, d)), F32)],
        scratch_bytes=_nbytes((hl + ts, d + SUBLANES * CONV_LANES), F32))


def cf_bwd_rows(name, dv3, v1, gn, ts=256):
    s, d = v1.shape
    ts = min(ts, s)

    def body(dv3_ref, v1_ref, gn_ref, dv1_ref, dgn_ref, db_ref):
        v1 = v1_ref[...]
        gn_v = gn_ref[...]
        v2 = (v1 * _rstd(v1)) * gn_v
        sg = jax.nn.sigmoid(v2)
        dv2 = dv3_ref[...] * (sg * (1.0 + v2 * (1.0 - sg)))
        dv1, dgn = _rms_bwd_math(dv2, v1, gn_v)
        dv1_ref[...] = dv1
        _accumulate(dgn_ref, _rowsum8(dgn))
        _accumulate(db_ref, _rowsum8(dv1))

    return _rows_call(
        name, body, s, ts, [(dv3, _blk(ts, d)), (v1, _blk(ts, d)), (gn, _full((1, d)))],
        [(jax.ShapeDtypeStruct((s, d), F32), _blk(ts, d)),
         (jax.ShapeDtypeStruct((SUBLANES, d), F32), _full((SUBLANES, d))),
         (jax.ShapeDtypeStruct((SUBLANES, d), F32), _full((SUBLANES, d)))])


def cf_bwd_conv(name, dv1, v0, a, w_dw, ts=256):
    s, d = dv1.shape
    ts = min(ts, s)
    hl = CONV_A_HALO
    taps = CONV_A_TAPS
    off = hl - (taps - 1)
    last_blk = s // ts - 1

    rc = min(CONV_ROWS, ts)

    def body(dv1_ref, dv1n_ref, v0_ref, v0p_ref, a_ref, w_ref, da_ref, dw_ref, db_ref,
             dbuf, vbuf, dv0_buf, dw_acc, dsh, vsh):
        i = pl.program_id(0)
        dbuf[pl.ds(0, ts), :] = dv1_ref[...]
        dbuf[pl.ds(ts, hl), :] = jnp.where(i == last_blk, 0.0, dv1n_ref[...])
        vbuf[pl.ds(0, hl), :] = jnp.where(i == 0, 0.0, v0p_ref[...])
        vbuf[pl.ds(hl, ts), :] = v0_ref[...]

        @pl.when(i == 0)
        def _():
            dw_acc[...] = jnp.zeros_like(dw_acc)

        def conv_t(lanes):
            _shifted_copies(dbuf, dsh, lanes)
            _shifted_copies(vbuf, vsh, lanes)
            for r0 in range(0, ts, rc):
                g = dbuf[pl.ds(r0, rc), lanes]
                acc = jnp.zeros((rc, lanes.size), F32)
                for k in range(taps):
                    acc = acc + w_ref[pl.ds(k, 1), lanes] * _window(dbuf, dsh, lanes, r0 + taps - 1 - k, rc)
                    prod = g * _window(vbuf, vsh, lanes, r0 + off + k, rc)
                    dw_acc[pl.ds(k * SUBLANES, SUBLANES), lanes] += _rowsum8(prod)
                dv0_buf[pl.ds(r0, rc), lanes] = acc

        _lane_chunks(d, conv_t)
        dv0 = dv0_buf[...]
        av = a_ref[...]
        val, sg = av[:, :d], jax.nn.sigmoid(av[:, d:])
        dval = dv0 * sg
        dgate = dv0 * val * (sg * (1.0 - sg))
        da_ref[:, :d] = dval.astype(BF16)
        da_ref[:, d:] = dgate.astype(BF16)
        _accumulate(db_ref.at[:, pl.ds(0, d)], _rowsum8(dval))
        _accumulate(db_ref.at[:, pl.ds(d, d)], _rowsum8(dgate))

        @pl.when(i == last_blk)
        def _():
            dw_ref[...] = jnp.sum(dw_acc[...].reshape(hl, SUBLANES, d), axis=1)

    lc = min(CONV_LANES, d)
    scratch = [pltpu.VMEM((ts + hl, d), F32), pltpu.VMEM((hl + ts, d), F32), pltpu.VMEM((ts, d), F32),
               pltpu.VMEM((hl * SUBLANES, d), F32), pltpu.VMEM((SUBLANES, ts + hl, lc), F32),
               pltpu.VMEM((SUBLANES, hl + ts, lc), F32)]
    sbytes = _nbytes((3 * ts + 2 * hl + hl * SUBLANES, d), F32) + 2 * _nbytes((SUBLANES, ts + hl, lc), F32)
    return _rows_call(
        name, body, s, ts,
        [(dv1, _blk(ts, d)), (dv1, _next_halo_spec(ts, hl, d, s)), (v0, _blk(ts, d)), (v0, _prev_halo_spec(ts, hl, d)),
         (a, _blk(ts, 2 * d)), (w_dw, _full(w_dw.shape))],
        [(jax.ShapeDtypeStruct((s, 2 * d), BF16), _blk(ts, 2 * d)),
         (jax.ShapeDtypeStruct((hl, d), F32), _full((hl, d))),
         (jax.ShapeDtypeStruct((SUBLANES, 2 * d), F32), _full((SUBLANES, 2 * d)))],
        scratch=scratch, scratch_bytes=sbytes)


def sc_fwd_mid(name, bcv, w_conv, ts=256):
    s, d3 = bcv.shape
    d = d3 // 3
    ts = min(ts, s)
    hl = CONV_B_HALO
    off = hl - (CONV_B_TAPS - 1)

    def body(x_ref, xp_ref, w_ref, y_ref, buf):
        hp = xp_ref[...]
        buf[pl.ds(0, hl), :] = jnp.where(pl.program_id(0) == 0, 0.0, hp[:, d:2 * d] * hp[:, 2 * d:])
        buf[pl.ds(hl, ts), :] = x_ref[:, d:2 * d] * x_ref[:, 2 * d:]
        cc = jnp.zeros((ts, d), F32)
        for k in range(CONV_B_TAPS):
            cc = cc + w_ref[pl.ds(k, 1), :] * buf[pl.ds(off + k, ts), :]
        y_ref[...] = (x_ref[:, :d] * cc).astype(BF16)

    return _rows_call(
        name, body, s, ts,
        [(bcv, _blk(ts, d3)), (bcv, _prev_halo_spec(ts, hl, d3)), (w_conv, _full(w_conv.shape))],
        [(jax.ShapeDtypeStruct((s, d), BF16), _blk(ts, d))],
        scratch=[pltpu.VMEM((hl + ts, d), F32)], scratch_bytes=_nbytes((hl + ts, d), F32))[0]


def sc_bwd_mid(name, dy, bcv, w_conv, ts=256):
    s, d3 = bcv.shape
    d = d3 // 3
    ts = min(ts, s)
    hl = CONV_B_HALO
    taps = CONV_B_TAPS
    off = hl - (taps - 1)
    last_blk = s // ts - 1

    def body(dy_ref, dyn_ref, x_ref, xp_ref, xn_ref, w_ref, dx_ref, dw_ref, cvbuf, dbuf, dw_acc):
        i = pl.program_id(0)
        hp = xp_ref[...]
        cvbuf[pl.ds(0, hl), :] = jnp.where(i == 0, 0.0, hp[:, d:2 * d] * hp[:, 2 * d:])
        gb, gc, v = x_ref[:, :d], x_ref[:, d:2 * d], x_ref[:, 2 * d:]
        cvbuf[pl.ds(hl, ts), :] = gc * v
        dy_v = dy_ref[...]
        dcc = dy_v * gb
        dbuf[pl.ds(0, ts), :] = dcc
        dbuf[pl.ds(ts, hl), :] = jnp.where(i == last_blk, 0.0, dyn_ref[...] * xn_ref[:, :d])

        @pl.when(i == 0)
        def _():
            dw_acc[...] = jnp.zeros_like(dw_acc)

        cc = jnp.zeros((ts, d), F32)
        dcv = jnp.zeros((ts, d), F32)
        for k in range(taps):
            win = cvbuf[pl.ds(off + k, ts), :]
            cc = cc + w_ref[pl.ds(k, 1), :] * win
            dcv = dcv + w_ref[pl.ds(k, 1), :] * dbuf[pl.ds(taps - 1 - k, ts), :]
            dw_acc[pl.ds(k * SUBLANES, SUBLANES), :] += _rowsum8(dcc * win)
        dx_ref[:, :d] = (dy_v * cc).astype(BF16)
        dx_ref[:, d:2 * d] = (dcv * v).astype(BF16)
        dx_ref[:, 2 * d:] = (dcv * gc).astype(BF16)

        @pl.when(i == last_blk)
        def _():
            dw_ref[...] = jnp.sum(dw_acc[...].reshape(hl, SUBLANES, d), axis=1)

    scratch = [pltpu.VMEM((hl + ts, d), F32), pltpu.VMEM((ts + hl, d), F32), pltpu.VMEM((hl * SUBLANES, d), F32)]
    sbytes = _nbytes((2 * ts + 2 * hl + hl * SUBLANES, d), F32)
    return _rows_call(
        name, body, s, ts,
        [(dy, _blk(ts, d)), (dy, _next_halo_spec(ts, hl, d, s)), (bcv, _blk(ts, d3)), (bcv, _prev_halo_spec(ts, hl, d3)),
         (bcv, _next_halo_spec(ts, hl, d3, s)), (w_conv, _full(w_conv.shape))],
        [(jax.ShapeDtypeStruct((s, d3), BF16), _blk(ts, d3)), (jax.ShapeDtypeStruct((hl, d), F32), _full((hl, d)))],
        scratch=scratch, scratch_bytes=sbytes)


def _row(a, i):
    return lax.slice_in_dim(a, i, i + 1, axis=0)


def _local_step(x, p, target, small, get_w, conv_w, put_grads):
    depth = p.shape[0]
    acts = []
    h = x
    for i in range(depth):
        j = i // 2
        act = {"h": h}
        u = rms_fwd(f"rms_mix_{i}", h, _row(small["norm_mix"], i))
        act["u"] = u
        if i % 2 == 0:
            a = mm_x_wcol(f"cf_pw1_{i}", u, get_w("cf_w_pw1", j, u), 0, extras=[(_row(small["cf_b_pw1"], j), "row")],
                          epi=lambda acc, b: (acc + b,))[0]
            v0, v1, v3 = cf_fwd_mid(f"cf_mid_{i}", a, conv_w["cf"][j], _row(small["cf_b_dw"], j), _row(small["cf_norm"], j))
            act.update(a=a, v0=v0, v1=v1, v3=v3)
            h1 = mm_x_wrow(f"cf_pw2_{i}", v3, get_w("cf_w_pw2", j, v3),
                           extras=[(_row(small["cf_b_pw2"], j), "row"), (h, "tile")],
                           epi=lambda acc, b, res: (res + (acc + b),), tn=1024)[0]
        else:
            bcv = mm_x_wcol(f"sc_in_{i}", u, get_w("sc_w_in", j, u), 0, tn=768)[0]
            y = sc_fwd_mid(f"sc_mid_{i}", bcv, conv_w["sc"][j])
            act.update(bcv=bcv, y=y)
            h1 = mm_x_wrow(f"sc_out_{i}", y, get_w("sc_w_out", j, y), extras=[(h, "tile")],
                           epi=lambda acc, res: (res + acc,), tn=1024)[0]
        act["h1"] = h1
        u2 = rms_fwd(f"rms_mlp_{i}", h1, _row(small["norm_mlp"], i))
        z, hd = mm_x_wcol(f"mlp_w1_{i}", u2, get_w("mlp_w1", i, u2), 0, outs_dtypes=(BF16, BF16),
                          epi=lambda acc: (acc, jnp.square(jnp.maximum(acc, 0.0))))
        h2 = mm_x_wrow(f"mlp_w2_{i}", hd, get_w("mlp_w2", i, hd), extras=[(h1, "tile")],
                       epi=lambda acc, res: (res + acc,), tm=1024, tn=256, tk=8192)[0]
        act.update(u2=u2, z=z, hd=hd, h2=h2)
        n3 = rms_fwd(f"rms_ple_{i}", h2, _row(small["norm_ple"], i))
        e = mm_x_wcol(f"ple_proj_{i}", p[i], get_w("ple_w_proj", i, n3), 0)[0]

        def ple_epi(acc, e_t, res):
            g_t = jax.nn.sigmoid(acc)
            return g_t, res + g_t * e_t

        g, h3 = mm_x_wrow(f"ple_gate_{i}", n3, get_w("ple_w_gate", i, e), extras=[(e, "tile"), (h2, "tile")],
                          outs_dtypes=(BF16, F32), epi=ple_epi)
        act.update(n3=n3, e=e, g=g)
        acts.append(act)
        h = h3

    loss_part, dh, dg_final = final_loss("final_loss", h, small["norm_final"], target)
    sg = {k: [None] * small[k].shape[0] for k in small if k != "norm_final"}
    sg["norm_final"] = [dg_final]
    sg["cf_w_dw"] = [None] * conv_w["cf"].shape[0]
    sg["sc_w_conv"] = [None] * conv_w["sc"].shape[0]

    for i in reversed(range(depth)):
        j = i // 2
        act = acts[i]
        de, dgl = ple_bwd_elem(f"ple_bwd_{i}", dh, act["g"], act["e"])
        g_proj = mm_xt_dy(f"d_ple_proj_{i}", p[i], de, True)
        g_gate = mm_xt_dy(f"d_ple_gate_{i}", act["n3"], dgl, False)
        dn3 = mm_dy_wrow_t(f"dn3_{i}", dgl, get_w("ple_w_gate", i))[0]
        dh2, dh2b, sg["norm_ple"][i], _ = rms_bwd(f"rms_ple_bwd_{i}", dn3, act["h2"], _row(small["norm_ple"], i), dh)
        g_w2 = mm_xt_dy(f"d_mlp_w2_{i}", act["hd"], dh2b, False)
        dz = mm_dy_wrow_t(f"dz_{i}", dh2b, get_w("mlp_w2", i), extras=[(act["z"], "tile")], outs_dtypes=(BF16,),
                          epi=lambda acc, z_t: (acc * (2.0 * jnp.maximum(z_t.astype(F32), 0.0)),))[0]
        g_w1 = mm_xt_dy(f"d_mlp_w1_{i}", act["u2"], dz, True)
        put_grads({("ple_w_proj", i): g_proj, ("ple_w_gate", i): g_gate, ("mlp_w2", i): g_w2, ("mlp_w1", i): g_w1})
        du2 = mm_dy_wcol_t(f"du2_{i}", dz, get_w("mlp_w1", i), 0)[0]
        dh1, dh1b, sg["norm_mlp"][i], cs1 = rms_bwd(f"rms_mlp_bwd_{i}", du2, act["h1"], _row(small["norm_mlp"], i), dh2)
        if i % 2 == 0:
            g_out = mm_xt_dy(f"d_cf_pw2_{i}", act["v3"], dh1b, False)
            sg["cf_b_pw2"][j] = cs1
            dv3 = mm_dy_wrow_t(f"dv3_{i}", dh1b, get_w("cf_w_pw2", j))[0]
            dv1, sg["cf_norm"][j], sg["cf_b_dw"][j] = cf_bwd_rows(f"cf_bwd_rows_{i}", dv3, act["v1"], _row(small["cf_norm"], j))
            da, sg["cf_w_dw"][j], sg["cf_b_pw1"][j] = cf_bwd_conv(f"cf_bwd_conv_{i}", dv1, act["v0"], act["a"], conv_w["cf"][j])
            g_in = mm_xt_dy(f"d_cf_pw1_{i}", act["u"], da, True)
            put_grads({("cf_w_pw2", j): g_out, ("cf_w_pw1", j): g_in})
            du = mm_dy_wcol_t(f"du_{i}", da, get_w("cf_w_pw1", j), 0)[0]
        else:
            g_out = mm_xt_dy(f"d_sc_out_{i}", act["y"], dh1b, False)
            dy = mm_dy_wrow_t(f"dy_{i}", dh1b, get_w("sc_w_out", j))[0]
            dbcv, sg["sc_w_conv"][j] = sc_bwd_mid(f"sc_bwd_mid_{i}", dy, act["bcv"], conv_w["sc"][j])
            g_in = mm_xt_dy(f"d_sc_in_{i}", act["u"], dbcv, True, tn=768)
            put_grads({("sc_w_out", j): g_out, ("sc_w_in", j): g_in})
            du = mm_dy_wcol_t(f"du_{i}", dbcv, get_w("sc_w_in", j), 0)[0]
        dh, _, sg["norm_mix"][i], _ = rms_bwd(f"rms_mix_bwd_{i}", du, act["h"], _row(small["norm_mix"], i), dh1)
    return loss_part, dh, sg


def _me_and_peers():
    x, y, c = lax.axis_index("x"), lax.axis_index("y"), lax.axis_index("c")
    me = 4 * x + 2 * y + c
    peers = []
    for q in range(1, NDEV):
        px = 1 - x if q & 4 else x
        py = 1 - y if q & 2 else y
        pc = 1 - c if q & 1 else c
        peers.append(((px, py, pc), 4 * px + 2 * py + pc))
    return me, peers


def _exchange(name, srcs, out_shapes, src_fns, dst_fns, after=()):
    n = len(srcs)
    n_after = len(after)

    def body(*refs):
        ins, outs = refs[:n], refs[n + n_after:2 * n + n_after]
        send_sems, recv_sems, local_sems = refs[2 * n + n_after:]
        me, peers = _me_and_peers()
        local, remote = [], []
        for k in range(n):
            cp = pltpu.make_async_copy(src_fns[k](ins[k], me), dst_fns[k](outs[k], me), local_sems.at[k])
            cp.start()
            local.append(cp)
        for q, (peer, peer_blk) in enumerate(peers):
            for k in range(n):
                cp = pltpu.make_async_remote_copy(
                    src_ref=src_fns[k](ins[k], peer_blk), dst_ref=dst_fns[k](outs[k], me),
                    send_sem=send_sems.at[k, q], recv_sem=recv_sems.at[k, q],
                    device_id=peer, device_id_type=MESH)
                cp.start()
                remote.append(cp)
        for q, (peer, peer_blk) in enumerate(peers):
            for k in range(n):
                pltpu.make_async_remote_copy(
                    src_ref=src_fns[k](ins[k], peer_blk), dst_ref=dst_fns[k](outs[k], peer_blk),
                    send_sem=send_sems.at[k, q], recv_sem=recv_sems.at[k, q],
                    device_id=peer, device_id_type=MESH).wait_recv()
        for cp in remote:
            cp.wait_send()
        for cp in local:
            cp.wait()

    any_spec = pl.BlockSpec(memory_space=pl.ANY)
    return pl.pallas_call(
        body,
        name=name,
        in_specs=[any_spec] * (n + n_after),
        out_specs=[any_spec] * n,
        out_shape=out_shapes,
        scratch_shapes=[pltpu.SemaphoreType.DMA((n, N_PEERS)), pltpu.SemaphoreType.DMA((n, N_PEERS)),
                        pltpu.SemaphoreType.DMA((n,))],
    )(*srcs, *after)


def sc_exchange(name, collective_id, srcs, out_shapes, src_fns, dst_fns):
    n = len(srcs)

    def body(*refs):
        ins, outs = refs[:n], refs[n:2 * n]
        send_sems, recv_sems, local_sems = refs[2 * n:]
        me, peers = _me_and_peers()
        barrier = pltpu.get_barrier_semaphore()
        for peer, _ in peers:
            pl.semaphore_signal(barrier, inc=1, device_id=peer, device_id_type=MESH)
        pl.semaphore_wait(barrier, N_PEERS)
        local, remote = [], []
        for k in range(n):
            cp = pltpu.make_async_copy(src_fns[k](ins[k], me), dst_fns[k](outs[k], me), local_sems.at[k])
            cp.start()
            local.append(cp)
        for q, (peer, peer_blk) in enumerate(peers):
            for k in range(n):
                cp = pltpu.make_async_remote_copy(
                    src_ref=src_fns[k](ins[k], peer_blk), dst_ref=dst_fns[k](outs[k], me),
                    send_sem=send_sems.at[k, q], recv_sem=recv_sems.at[k, q],
                    device_id=peer, device_id_type=MESH)
                cp.start()
                remote.append(cp)
        for q, (peer, peer_blk) in enumerate(peers):
            for k in range(n):
                pltpu.make_async_remote_copy(
                    src_ref=src_fns[k](ins[k], peer_blk), dst_ref=dst_fns[k](outs[k], peer_blk),
                    send_sem=send_sems.at[k, q], recv_sem=recv_sems.at[k, q],
                    device_id=peer, device_id_type=MESH).wait_recv()
        for cp in remote:
            cp.wait_send()
        for cp in local:
            cp.wait()

    return pl.kernel(
        body,
        out_type=out_shapes,
        mesh=plsc.ScalarSubcoreMesh(axis_name="sequencer", num_cores=1),
        name=name,
        scratch_types=[pltpu.SemaphoreType.DMA((n, N_PEERS)), pltpu.SemaphoreType.DMA((n, N_PEERS)),
                       pltpu.SemaphoreType.DMA((n,))],
        compiler_params=pltpu.CompilerParams(collective_id=collective_id),
    )(*srcs)


def sc_gather(name, collective_id, srcs, layers):
    n = len(srcs)
    outs_shape = [jax.ShapeDtypeStruct((NDEV, 1) + a.shape[1:], a.dtype) for a in srcs]

    def body(*refs):
        ins, outs = refs[:n], refs[n:2 * n]
        send_sems, recv_sems, local_sems = refs[2 * n:]
        x, y, c = lax.axis_index("x"), lax.axis_index("y"), lax.axis_index("c")
        me = 4 * x + 2 * y + c
        sibling = (x, y, 1 - c)
        chips = [(1 - x, y), (x, 1 - y), (1 - x, 1 - y)]
        barrier = pltpu.get_barrier_semaphore()
        for peer in [sibling] + [(cx, cy, c) for cx, cy in chips]:
            pl.semaphore_signal(barrier, inc=1, device_id=peer, device_id_type=MESH)
        pl.semaphore_wait(barrier, 1 + len(chips))

        def copy(k, slot, blk, to, src=None):
            place = outs[k].at[blk, 0]
            return pltpu.make_async_remote_copy(
                src_ref=place if src is None else src, dst_ref=place,
                send_sem=send_sems.at[k, slot], recv_sem=recv_sems.at[k, slot],
                device_id=to, device_id_type=MESH)

        local, sent = [], []
        for k in range(n):
            mine = ins[k].at[layers[k]]
            cp = pltpu.make_async_copy(mine, outs[k].at[me, 0], local_sems.at[k])
            cp.start()
            local.append(cp)
            sent.append(copy(k, 0, me, sibling, src=mine))
            sent += [copy(k, 1 + j, me, (cx, cy, c), src=mine) for j, (cx, cy) in enumerate(chips)]
        for cp in sent:
            cp.start()
        for k in range(n):
            for j, (cx, cy) in enumerate(chips):
                blk = 4 * cx + 2 * cy + c
                copy(k, 1 + j, blk, sibling).wait_recv()
                fwd = copy(k, 4 + j, blk, sibling)
                fwd.start()
                sent.append(fwd)
        for k in range(n):
            copy(k, 0, 4 * x + 2 * y + (1 - c), sibling).wait_recv()
            for j, (cx, cy) in enumerate(chips):
                copy(k, 4 + j, 4 * cx + 2 * cy + (1 - c), sibling).wait_recv()
        for cp in sent:
            cp.wait_send()
        for cp in local:
            cp.wait()

    return pl.kernel(
        body,
        out_type=outs_shape,
        mesh=plsc.ScalarSubcoreMesh(axis_name="sequencer", num_cores=1),
        name=name,
        scratch_types=[pltpu.SemaphoreType.DMA((n, N_PEERS)), pltpu.SemaphoreType.DMA((n, N_PEERS)),
                       pltpu.SemaphoreType.DMA((n,))],
        compiler_params=pltpu.CompilerParams(collective_id=collective_id),
    )(*srcs)


def _gather_src(layer):
    return lambda ref, blk: ref.at[layer]


def _gather_dst(ref, blk):
    return ref.at[blk, 0]


def _slice_of(ref, blk):
    return ref.at[blk]


def cast_bf16(name, w, tr_elems=512 * 1024):
    l, r, c = w.shape
    tr = _row_tile(r, tr_elems // c)
    spec = pl.BlockSpec((None, tr, c), lambda li, i: (li, i, 0))

    def body(w_ref, o_ref):
        o_ref[...] = w_ref[...].astype(BF16)

    return pl.pallas_call(
        body, name=name, grid=(l, r // tr), in_specs=[spec], out_specs=spec,
        out_shape=jax.ShapeDtypeStruct(w.shape, BF16),
        compiler_params=_params(("parallel", "parallel"), 6 * tr * c),
    )(w)


def _adamw_math(w, g, m, v):
    m = ADAM_B1 * m + (1.0 - ADAM_B1) * g
    v = ADAM_B2 * v + (1.0 - ADAM_B2) * (g * g)
    m_hat = m * (1.0 / (1.0 - ADAM_B1 ** ADAM_STEP))
    v_hat = v * (1.0 / (1.0 - ADAM_B2 ** ADAM_STEP))
    delta = -ADAM_LR * (m_hat / (jnp.sqrt(v_hat) + ADAM_EPS) + ADAM_WD * w)
    return delta, m, v


def _sum_blocks(ref):
    g = ref[0].astype(F32)
    for d in range(1, ref.shape[0]):
        g = g + ref[d].astype(F32)
    return g


def adamw_layer(name, recv, w, m, v, layer, stacked, after=None, tr_elems=256 * 1024):
    nd, r, c = recv.shape
    tr = _row_tile(r, tr_elems // c)
    r_spec = pl.BlockSpec((nd, tr, c), lambda i: (0, i, 0))
    w_spec = pl.BlockSpec((None, tr, c), lambda i: (layer, i, 0))
    if stacked is None:
        stacked = [lax.empty(w.shape, F32) for _ in range(4)]
    after = [] if after is None else [after]

    def body(r_ref, w_ref, m_ref, v_ref, g_in, d_in, m_in, v_in, *rest):
        g_out, d_out, m_out, v_out = rest[len(after):]
        g = _sum_blocks(r_ref)
        delta, m_new, v_new = _adamw_math(w_ref[...], g, m_ref[...], v_ref[...])
        g_out[...] = g
        d_out[...] = delta
        m_out[...] = m_new
        v_out[...] = v_new

    out = jax.ShapeDtypeStruct(w.shape, F32)
    return pl.pallas_call(
        body, name=name, grid=(r // tr,),
        in_specs=[r_spec, w_spec, w_spec, w_spec] + [pl.BlockSpec(memory_space=pl.ANY)] * (4 + len(after)),
        out_specs=[w_spec] * 4, out_shape=[out] * 4,
        input_output_aliases={4: 0, 5: 1, 6: 2, 7: 3},
        compiler_params=_params(("parallel",), tr * c * (2 * nd + 7 * 4)),
    )(recv, w, m, v, *stacked, *after)


def pack_small_grads(name, parts, taps, n_blocks):
    d = parts[0].shape[1]
    n_p, rows = len(parts), [t.shape[0] for t in taps]
    cb = d // n_blocks

    def body(*refs):
        part_refs, tap_refs = refs[:n_p], refs[n_p:n_p + len(taps)]
        sums_out, taps_out = refs[n_p + len(taps):]
        for i, r in enumerate(part_refs):
            sums_out[pl.ds(i, 1), :] = jnp.sum(r[...], axis=0, keepdims=True)
        r0 = 0
        for t_ref, n in zip(tap_refs, rows):
            for b in range(n_blocks):
                taps_out[b, pl.ds(r0, n), :] = t_ref[:, pl.ds(b * cb, cb)]
            r0 += n

    vm = pl.BlockSpec(memory_space=pltpu.VMEM)
    return pl.pallas_call(
        body, name=name, in_specs=[vm] * (n_p + len(taps)), out_specs=[vm] * 2,
        out_shape=[jax.ShapeDtypeStruct((n_p, d), F32), jax.ShapeDtypeStruct((n_blocks, sum(rows), cb), F32)],
    )(*parts, *taps)


def small_update(name, part_g, tap_g, w_a, m_a, v_a, w_b, m_b, v_b):
    nd, na, d = part_g.shape
    nb, cb = w_b.shape

    def body(pg_ref, tg_ref, wa_ref, ma_ref, va_ref, wb_ref, mb_ref, vb_ref,
             ga_out, da_out, ma_out, va_out, gb_out, db_out, mb_out, vb_out, loss_out):
        ga = _sum_blocks(pg_ref)
        delta, m_new, v_new = _adamw_math(wa_ref[...], ga, ma_ref[...], va_ref[...])
        ga_out[...] = ga
        da_out[...] = delta
        ma_out[...] = m_new
        va_out[...] = v_new
        loss_out[...] = jnp.broadcast_to(jnp.sum(ga[na - 1:na, :], axis=1, keepdims=True), loss_out.shape)
        gb = _sum_blocks(tg_ref)
        delta, m_new, v_new = _adamw_math(wb_ref[...], gb, mb_ref[...], vb_ref[...])
        gb_out[...] = gb
        db_out[...] = delta
        mb_out[...] = m_new
        vb_out[...] = v_new

    oa, ob = jax.ShapeDtypeStruct((na, d), F32), jax.ShapeDtypeStruct((nb, cb), F32)
    vm = pl.BlockSpec(memory_space=pltpu.VMEM)
    return pl.pallas_call(
        body, name=name, in_specs=[vm] * 8, out_specs=[vm] * 9,
        out_shape=[oa] * 4 + [ob] * 4 + [jax.ShapeDtypeStruct((1, LANES), F32)],
        compiler_params=pltpu.CompilerParams(vmem_limit_bytes=_vmem_limit(_nbytes(part_g.shape, F32))),
    )(part_g, tap_g, w_a, m_a, v_a, w_b, m_b, v_b)


BIG = ("cf_w_pw1", "cf_w_pw2", "sc_w_in", "sc_w_out", "mlp_w1", "mlp_w2", "ple_w_proj", "ple_w_gate")
COL_SHARDED = ("cf_w_pw1", "sc_w_in", "mlp_w1", "ple_w_proj")
WEIGHT_ORDER = ("norm_mix", "norm_mlp", "norm_ple", "cf_w_pw1", "cf_b_pw1", "cf_w_dw", "cf_b_dw", "cf_norm",
                "cf_w_pw2", "cf_b_pw2", "sc_w_in", "sc_w_conv", "sc_w_out", "mlp_w1", "mlp_w2", "ple_w_proj",
                "ple_w_gate", "norm_final")
SMALL_ROWS = (("norm_mix", 4), ("norm_mlp", 4), ("norm_ple", 4), ("cf_b_pw1", 4), ("cf_b_dw", 2), ("cf_norm", 2),
              ("cf_b_pw2", 2), ("norm_final", 1))


def _layer_weights(i):
    mixer = (("cf_w_pw1", i // 2), ("cf_w_pw2", i // 2)) if i % 2 == 0 else (("sc_w_in", i // 2), ("sc_w_out", i // 2))
    return mixer + (("mlp_w1", i), ("mlp_w2", i), ("ple_w_proj", i), ("ple_w_gate", i))


def _pad_rows(a, rows):
    return jnp.pad(a, ((0, 0), (0, rows - a.shape[1]), (0, 0)))


def _pack_taps(cf, sc):
    c = cf.shape[2]
    return jnp.concatenate([_pad_rows(cf, CONV_A_HALO).reshape(-1, c), _pad_rows(sc, CONV_B_HALO).reshape(-1, c)], axis=0)


def _unpack_taps(t, n_cf):
    c = t.shape[1]
    cf = t[:n_cf * CONV_A_HALO].reshape(n_cf, CONV_A_HALO, c)[:, :CONV_A_TAPS]
    sc = t[n_cf * CONV_A_HALO:].reshape(-1, CONV_B_HALO, c)[:, :CONV_B_TAPS]
    return cf, sc


def _pack_small(vals, d):
    return jnp.concatenate([vals[k].reshape(-1, d) for k, _ in SMALL_ROWS] + [jnp.zeros((1, d), F32)], axis=0)


def _unpack_small(a, shapes):
    out, r = {}, 0
    for k, n in SMALL_ROWS:
        out[k] = a[r:r + n].reshape(shapes[k])
        r += n
    return out


def kernel(x, p, norm_mix, norm_mlp, norm_ple, cf_w_pw1, cf_b_pw1, cf_w_dw, cf_b_dw, cf_norm, cf_w_pw2, cf_b_pw2, sc_w_in, sc_w_conv, sc_w_out, mlp_w1, mlp_w2, ple_w_proj, ple_w_gate, norm_final, loss_target, m_norm_mix, m_norm_mlp, m_norm_ple, m_cf_w_pw1, m_cf_b_pw1, m_cf_w_dw, m_cf_b_dw, m_cf_norm, m_cf_w_pw2, m_cf_b_pw2, m_sc_w_in, m_sc_w_conv, m_sc_w_out, m_mlp_w1, m_mlp_w2, m_ple_w_proj, m_ple_w_gate, m_norm_final, v_norm_mix, v_norm_mlp, v_norm_ple, v_cf_w_pw1, v_cf_b_pw1, v_cf_w_dw, v_cf_b_dw, v_cf_norm, v_cf_w_pw2, v_cf_b_pw2, v_sc_w_in, v_sc_w_conv, v_sc_w_out, v_mlp_w1, v_mlp_w2, v_ple_w_proj, v_ple_w_gate, v_norm_final):
    w = dict(norm_mix=norm_mix, norm_mlp=norm_mlp, norm_ple=norm_ple, cf_w_pw1=cf_w_pw1, cf_b_pw1=cf_b_pw1,
             cf_w_dw=cf_w_dw, cf_b_dw=cf_b_dw, cf_norm=cf_norm, cf_w_pw2=cf_w_pw2, cf_b_pw2=cf_b_pw2,
             sc_w_in=sc_w_in, sc_w_conv=sc_w_conv, sc_w_out=sc_w_out, mlp_w1=mlp_w1, mlp_w2=mlp_w2,
             ple_w_proj=ple_w_proj, ple_w_gate=ple_w_gate, norm_final=norm_final)
    m = dict(norm_mix=m_norm_mix, norm_mlp=m_norm_mlp, norm_ple=m_norm_ple, cf_w_pw1=m_cf_w_pw1, cf_b_pw1=m_cf_b_pw1,
             cf_w_dw=m_cf_w_dw, cf_b_dw=m_cf_b_dw, cf_norm=m_cf_norm, cf_w_pw2=m_cf_w_pw2, cf_b_pw2=m_cf_b_pw2,
             sc_w_in=m_sc_w_in, sc_w_conv=m_sc_w_conv, sc_w_out=m_sc_w_out, mlp_w1=m_mlp_w1, mlp_w2=m_mlp_w2,
             ple_w_proj=m_ple_w_proj, ple_w_gate=m_ple_w_gate, norm_final=m_norm_final)
    v = dict(norm_mix=v_norm_mix, norm_mlp=v_norm_mlp, norm_ple=v_norm_ple, cf_w_pw1=v_cf_w_pw1, cf_b_pw1=v_cf_b_pw1,
             cf_w_dw=v_cf_w_dw, cf_b_dw=v_cf_b_dw, cf_norm=v_cf_norm, cf_w_pw2=v_cf_w_pw2, cf_b_pw2=v_cf_b_pw2,
             sc_w_in=v_sc_w_in, sc_w_conv=v_sc_w_conv, sc_w_out=v_sc_w_out, mlp_w1=v_mlp_w1, mlp_w2=v_mlp_w2,
             ple_w_proj=v_ple_w_proj, ple_w_gate=v_ple_w_gate, norm_final=v_norm_final)
    depth, d = norm_mix.shape
    n_cf = cf_w_dw.shape[0]

    taps_w = _pack_taps(cf_w_dw, sc_w_conv)
    shards, gathered = {}, {}
    ids = iter(range(5 * depth))
    for i in range(depth):
        names = _layer_weights(i)
        groups = [names[:1], names[1:2], names[2:3], names[3:]] if i == 0 else [names[:2], names[2:]]
        for n_group, group in enumerate(groups):
            with_taps = i == 0 and n_group == 1
            for k, _ in group:
                if k not in shards:
                    shards[k] = cast_bf16(f"cast_{k}", w[k])
            srcs = [shards[k] for k, _ in group] + ([taps_w[None]] if with_taps else [])
            got = sc_gather(f"gather_{i}_{n_group}", next(ids), srcs, [l for _, l in group] + ([0] if with_taps else []))
            for (k, l), g in zip(group, got):
                gathered[(k, l)] = g if k in COL_SHARDED else g.reshape(-1, g.shape[3])
            if with_taps:
                taps_full = jnp.transpose(got[-1][:, 0], (1, 0, 2)).reshape(taps_w.shape[0], d)
    conv_w = {"cf": taps_full[:n_cf * CONV_A_HALO].reshape(n_cf, CONV_A_HALO, d),
              "sc": taps_full[n_cf * CONV_A_HALO:].reshape(-1, CONV_B_HALO, d)}

    def get_w(k, l, after=None):
        return gathered[(k, l)]

    received, waiting = {}, {}
    last_group = list(_layer_weights(0)[:2])

    def put_grads(grads):
        names = list(grads)
        if names[0][0] in ("cf_w_pw2", "sc_w_out") and set(names) != set(last_group):
            waiting.update(grads)
            return
        grads = {**waiting, **grads}
        waiting.clear()
        names = list(grads)
        got = sc_exchange(f"grad_exchange_{names[-1][0]}_{names[-1][1]}", next(ids), [grads[n] for n in names],
                          [jax.ShapeDtypeStruct(grads[n].shape, BF16) for n in names],
                          [_slice_of] * len(names), [_slice_of] * len(names))
        received.update(zip(names, got))

    small = {k: w[k] for k, _ in SMALL_ROWS}
    small["norm_final"] = norm_final[None]
    loss_part, grad_x, sg = _local_step(x[0], p[:, 0], loss_target[0], small, get_w, conv_w, put_grads)

    out = {k: None for k in BIG}
    previous = None
    for (k, l), recv in received.items():
        if (k, l) not in last_group:
            out[k] = adamw_layer(f"adamw_{k}_{l}", recv, w[k], m[k], v[k], l, out[k], after=previous)
            previous = out[k][1]
    updated_first = [out[k][0] for k in BIG if out[k] is not None and k not in [n for n, _ in last_group]]

    parts = []
    for k, n in SMALL_ROWS:
        for g in sg[k]:
            parts += [g[:, :d], g[:, d:]] if g.shape[1] == 2 * d else [g]
    parts.append(loss_part)
    sums, tap_slices = pack_small_grads("pack_small_grads", parts, sg["cf_w_dw"] + sg["sc_w_conv"], NDEV)
    part_all, tap_mine = _exchange(
        "small_exchange", [sums[None], tap_slices],
        [jax.ShapeDtypeStruct((NDEV, 1) + sums.shape, F32), jax.ShapeDtypeStruct(tap_slices.shape, F32)],
        [_gather_src(0), _slice_of], [_gather_dst, _slice_of], after=updated_first)
    for k, l in last_group:
        out[k] = adamw_layer(f"adamw_{k}_{l}", received[(k, l)], w[k], m[k], v[k], l, out[k], after=part_all)
    sm = small_update("small_update", part_all[:, 0], tap_mine,
                      _pack_small(w, d), _pack_small(m, d), _pack_small(v, d),
                      taps_w, _pack_taps(m["cf_w_dw"], m["sc_w_conv"]), _pack_taps(v["cf_w_dw"], v["sc_w_conv"]))
    shapes = {k: w[k].shape for k, _ in SMALL_ROWS}
    for t in range(4):
        un = _unpack_small(sm[t], shapes)
        cf_t, sc_t = _unpack_taps(sm[4 + t], n_cf)
        for k in un:
            out.setdefault(k, [None] * 4)[t] = un[k]
        out.setdefault("cf_w_dw", [None] * 4)[t] = cf_t
        out.setdefault("sc_w_conv", [None] * 4)[t] = sc_t
    loss = sm[8][0, 0]

    return (loss, grad_x[None], *[out[k][0] for k in WEIGHT_ORDER], *[out[k][1] for k in WEIGHT_ORDER],
            *[out[k][2] for k in WEIGHT_ORDER], *[out[k][3] for k in WEIGHT_ORDER])
```

```python
import jax
import jax.numpy as jnp
from jax import lax
from jax.experimental import pallas as pl
from jax.experimental.pallas import tpu as pltpu
from jax.experimental.pallas import tpu_sc as plsc

F32 = jnp.float32
BF16 = jnp.bfloat16
EPS = 1e-6
NDEV = 8
N_PEERS = NDEV - 1
MESH = pl.DeviceIdType.MESH

ADAM_LR = 0.001
ADAM_B1 = 0.9
ADAM_B2 = 0.999
ADAM_EPS = 1e-08
ADAM_WD = 0.01
ADAM_STEP = 10

V7X_VMEM_BYTES = 64 * 1024 * 1024
VMEM_LIMIT_MAX = 56 * 1024 * 1024
SUBLANES = 8
LANES = 128
CONV_A_TAPS = 31
CONV_A_HALO = 32
CONV_B_TAPS = 3
CONV_B_HALO = 8


def _nbytes(shape, dtype):
    n = 1
    for s in shape:
        if s is not None:
            n *= s
    return n * jnp.dtype(dtype).itemsize


def _vmem_limit(block_bytes, scratch_bytes=0):
    need = 2 * block_bytes + scratch_bytes
    return int(min(VMEM_LIMIT_MAX, max(32 * 1024 * 1024, need + need // 2 + (4 << 20))))


def _params(sem, block_bytes, scratch_bytes=0):
    return pltpu.CompilerParams(dimension_semantics=sem, vmem_limit_bytes=_vmem_limit(block_bytes, scratch_bytes))


_DIMS = {
    "nn": (((1,), (0,)), ((), ())),
    "nt": (((1,), (1,)), ((), ())),
    "tn": (((0,), (0,)), ((), ())),
}


def _mm(name, dims, grid, acc_shape, a, a_spec, b, b_spec, extras, outs, epi):
    ni, nj, nk = grid
    n_ex, n_out = len(extras), len(outs)
    dn = _DIMS[dims]
    b_sub = [s for s in b_spec.block_shape if s is not None]
    n_sub = b_sub[0] if len(b_sub) == 3 else 1

    def body(*refs):
        a_ref, b_ref = refs[0], refs[1]
        ex_refs = refs[2:2 + n_ex]
        out_refs = refs[2 + n_ex:2 + n_ex + n_out]
        if n_sub == 1:
            d = lax.dot_general(a_ref[...].astype(BF16), b_ref[...].astype(BF16), dn, preferred_element_type=F32)
        else:
            w_sub = a_ref.shape[1] // n_sub
            d = None
            for s in range(n_sub):
                part = lax.dot_general(a_ref[:, pl.ds(s * w_sub, w_sub)].astype(BF16), b_ref[s].astype(BF16), dn,
                                       preferred_element_type=F32)
                d = part if d is None else d + part

        def finish(acc):
            res = epi(acc, *[r[...] for r in ex_refs])
            for o_ref, r in zip(out_refs, res):
                o_ref[...] = r.astype(o_ref.dtype)

        if nk == 1:
            finish(d)
        else:
            acc_ref = refs[2 + n_ex + n_out]
            k = pl.program_id(2)

            @pl.when(k == 0)
            def _():
                acc_ref[...] = d

            @pl.when(jnp.logical_and(k > 0, k < nk - 1))
            def _():
                acc_ref[...] += d

            @pl.when(k == nk - 1)
            def _():
                finish(acc_ref[...] + d)

    blk = _nbytes(a_spec.block_shape, a.dtype) + _nbytes(b_spec.block_shape, b.dtype)
    for arr, spec in list(extras) + list(outs):
        blk += _nbytes(spec.block_shape, arr.dtype)
    acc_bytes = _nbytes(acc_shape, F32)
    scratch = [pltpu.VMEM(acc_shape, F32)] if nk > 1 else []
    return pl.pallas_call(
        body,
        name=name,
        grid=grid,
        in_specs=[a_spec, b_spec] + [s for _, s in extras],
        out_specs=[s for _, s in outs],
        out_shape=[o for o, _ in outs],
        scratch_shapes=scratch,
        compiler_params=_params(("parallel", "parallel", "arbitrary"), blk, 3 * acc_bytes),
    )(a, b, *[e for e, _ in extras])


def _tile(n, pref):
    if n <= pref:
        return n
    t = pref - pref % LANES
    while t > LANES and n % t:
        t -= LANES
    assert n % t == 0, (n, pref)
    return t


def _row_tile(n, pref):
    if n <= pref:
        return n
    t = max(SUBLANES, pref - pref % SUBLANES)
    while t > SUBLANES and n % t:
        t -= SUBLANES
    assert n % t == 0, (n, pref)
    return t


def _id_epi(acc):
    return (acc,)


def mm_x_wcol(name, x, w, layer, extras=(), outs_dtypes=(F32,), epi=_id_epi, tm=1024, tn=1024):
    m, kdim = x.shape
    c = w.shape[3]
    n = NDEV * c
    tm, tn = _tile(m, tm), _tile(c, tn)
    tk = _tile(kdim, 2048)
    grid = (m // tm, n // tn, kdim // tk)
    per = c // tn
    a_spec = pl.BlockSpec((tm, tk), lambda i, j, k: (i, k))
    b_spec = pl.BlockSpec((None, None, tk, tn), lambda i, j, k: (j // per, layer, k, j % per))
    ex = [(e, _ex_spec(e, kind, tm, tn)) for e, kind in extras]
    o_spec = pl.BlockSpec((tm, tn), lambda i, j, k: (i, j))
    outs = [(jax.ShapeDtypeStruct((m, n), dt), o_spec) for dt in outs_dtypes]
    return _mm(name, "nn", grid, (tm, tn), x, a_spec, w, b_spec, ex, outs, epi)


def mm_x_wrow(name, x, w, extras=(), outs_dtypes=(F32,), epi=_id_epi, tm=1024, tn=512, tk=2048):
    m, kdim = x.shape
    n = w.shape[1]
    assert kdim == w.shape[0]
    tm, tn = _tile(m, tm), _tile(n, tn)
    tk = _tile(kdim, tk)
    grid = (m // tm, n // tn, kdim // tk)
    a_spec = pl.BlockSpec((tm, tk), lambda i, j, k: (i, k))
    b_spec = pl.BlockSpec((tk, tn), lambda i, j, k: (k, j))
    ex = [(e, _ex_spec(e, kind, tm, tn)) for e, kind in extras]
    o_spec = pl.BlockSpec((tm, tn), lambda i, j, k: (i, j))
    outs = [(jax.ShapeDtypeStruct((m, n), dt), o_spec) for dt in outs_dtypes]
    return _mm(name, "nn", grid, (tm, tn), x, a_spec, w, b_spec, ex, outs, epi)


def mm_dy_wcol_t(name, dy, w, layer, extras=(), outs_dtypes=(F32,), epi=_id_epi, tm=1024, b_block_bytes=4 << 20):
    m, n = dy.shape
    nd, kdim, c = w.shape[0], w.shape[2], w.shape[3]
    assert n == nd * c
    tm, tn = _tile(m, tm), _tile(kdim, max(LANES, b_block_bytes // (n * 2)))
    n_sub = nd
    grid = (m // tm, kdim // tn, nd // n_sub)
    a_spec = pl.BlockSpec((tm, n_sub * c), lambda i, j, k: (i, k))
    if n_sub > 1:
        b_spec = pl.BlockSpec((n_sub, None, tn, c), lambda i, j, k: (k, layer, j, 0))
    else:
        b_spec = pl.BlockSpec((None, None, tn, c), lambda i, j, k: (k, layer, j, 0))
    ex = [(e, _ex_spec(e, kind, tm, tn)) for e, kind in extras]
    o_spec = pl.BlockSpec((tm, tn), lambda i, j, k: (i, j))
    outs = [(jax.ShapeDtypeStruct((m, kdim), dt), o_spec) for dt in outs_dtypes]
    return _mm(name, "nt", grid, (tm, tn), dy, a_spec, w, b_spec, ex, outs, epi)


def mm_dy_wrow_t(name, dy, w, extras=(), outs_dtypes=(F32,), epi=_id_epi, tm=1024, tn=1024):
    m, n = dy.shape
    kdim = w.shape[0]
    assert n == w.shape[1]
    tm, tn = _tile(m, tm), _tile(kdim, tn)
    tk = _tile(n, 2048)
    grid = (m // tm, kdim // tn, n // tk)
    a_spec = pl.BlockSpec((tm, tk), lambda i, j, k: (i, k))
    b_spec = pl.BlockSpec((tn, tk), lambda i, j, k: (j, k))
    ex = [(e, _ex_spec(e, kind, tm, tn)) for e, kind in extras]
    o_spec = pl.BlockSpec((tm, tn), lambda i, j, k: (i, j))
    outs = [(jax.ShapeDtypeStruct((m, kdim), dt), o_spec) for dt in outs_dtypes]
    return _mm(name, "nt", grid, (tm, tn), dy, a_spec, w, b_spec, ex, outs, epi)


def mm_xt_dy(name, x, dy, col_shards, tm=1024, tn=1024):
    m, kdim = x.shape
    n = dy.shape[1]
    tk = _tile(m, 4096)
    if col_shards:
        c = n // NDEV
        tm, tn = _tile(kdim, tm), _tile(c, tn)
        per = c // tn
        out = jax.ShapeDtypeStruct((NDEV, kdim, c), BF16)
        o_spec = pl.BlockSpec((None, tm, tn), lambda i, j, k: (j // per, i, j % per))
    else:
        tm, tn = _tile(kdim, tm), _tile(n, tn)
        out = jax.ShapeDtypeStruct((kdim, n), BF16)
        o_spec = pl.BlockSpec((tm, tn), lambda i, j, k: (i, j))
    grid = (kdim // tm, n // tn, m // tk)
    a_spec = pl.BlockSpec((tk, tm), lambda i, j, k: (k, i))
    b_spec = pl.BlockSpec((tk, tn), lambda i, j, k: (k, j))
    g = _mm(name, "tn", grid, (tm, tn), x, a_spec, dy, b_spec, [], [(out, o_spec)], _id_epi)[0]
    return g if col_shards else g.reshape(NDEV, kdim // NDEV, n)


def _ex_spec(e, kind, tm, tn):
    if kind == "tile":
        return pl.BlockSpec((tm, tn), lambda i, j, k: (i, j))
    if kind == "row":
        return pl.BlockSpec((1, tn), lambda i, j, k: (0, j))
    raise ValueError(kind)


def _rows_call(name, body, n_rows, ts, ins, outs, scratch=(), scratch_bytes=0):
    blk = sum(_nbytes(s.block_shape, a.dtype) for a, s in list(ins) + list(outs))
    return pl.pallas_call(
        body,
        name=name,
        grid=(n_rows // ts,),
        in_specs=[s for _, s in ins],
        out_specs=[s for _, s in outs],
        out_shape=[o for o, _ in outs],
        scratch_shapes=list(scratch),
        compiler_params=_params(("arbitrary",), blk, scratch_bytes + 4 * blk // 2),
    )(*[a for a, _ in ins])


def _blk(ts, d):
    return pl.BlockSpec((ts, d), lambda i: (i, 0))


def _full(shape):
    return pl.BlockSpec(shape, lambda i: tuple(0 for _ in shape))


def _rowsum8(v):
    t, d = v.shape
    return jnp.sum(v.reshape(t // SUBLANES, SUBLANES, d), axis=0)


def _accumulate(ref, val):
    @pl.when(pl.program_id(0) == 0)
    def _():
        ref[...] = val

    @pl.when(pl.program_id(0) > 0)
    def _():
        ref[...] += val


def _rstd(x):
    return lax.rsqrt(jnp.mean(x * x, axis=-1, keepdims=True) + EPS)


def _rms_bwd_math(dy, x, g):
    r = _rstd(x)
    gdy = dy * g
    c = jnp.mean(gdy * x, axis=-1, keepdims=True)
    dx = r * gdy - x * (r * r * r * c)
    return dx, dy * (x * r)


def rms_fwd(name, h, g, ts=512):
    s, d = h.shape
    ts = min(ts, s)

    def body(h_ref, g_ref, u_ref):
        x = h_ref[...]
        u_ref[...] = ((x * _rstd(x)) * g_ref[...]).astype(BF16)

    return _rows_call(name, body, s, ts, [(h, _blk(ts, d)), (g, _full((1, d)))],
                      [(jax.ShapeDtypeStruct((s, d), BF16), _blk(ts, d))])[0]


def rms_bwd(name, du, h, g, dres, ts=256):
    s, d = h.shape
    ts = min(ts, s)

    def body(du_ref, h_ref, g_ref, dres_ref, dh_ref, dhb_ref, dg_ref, cs_ref):
        dx, dg = _rms_bwd_math(du_ref[...], h_ref[...], g_ref[...])
        dh = dres_ref[...] + dx
        dh_ref[...] = dh
        dhb_ref[...] = dh.astype(BF16)
        _accumulate(dg_ref, _rowsum8(dg))
        _accumulate(cs_ref, _rowsum8(dh))

    return _rows_call(
        name, body, s, ts,
        [(du, _blk(ts, d)), (h, _blk(ts, d)), (g, _full((1, d))), (dres, _blk(ts, d))],
        [(jax.ShapeDtypeStruct((s, d), F32), _blk(ts, d)), (jax.ShapeDtypeStruct((s, d), BF16), _blk(ts, d)),
         (jax.ShapeDtypeStruct((SUBLANES, d), F32), _full((SUBLANES, d))),
         (jax.ShapeDtypeStruct((SUBLANES, d), F32), _full((SUBLANES, d)))])


def rms_bwd_ple(name, du, h, g, dres, gate, e, ts=256):
    s, d = h.shape
    ts = min(ts, s)

    def body(du_ref, h_ref, g_ref, dres_ref, gate_ref, e_ref, dh_ref, de_ref, dgl_ref, dg_ref):
        dx, dg = _rms_bwd_math(du_ref[...], h_ref[...], g_ref[...])
        dh = dres_ref[...] + dx
        dh_ref[...] = dh
        gate_v = gate_ref[...].astype(F32)
        de_ref[...] = (dh * gate_v).astype(BF16)
        dgl_ref[...] = (dh * e_ref[...] * (gate_v * (1.0 - gate_v))).astype(BF16)
        _accumulate(dg_ref, _rowsum8(dg))

    return _rows_call(
        name, body, s, ts,
        [(du, _blk(ts, d)), (h, _blk(ts, d)), (g, _full((1, d))), (dres, _blk(ts, d)), (gate, _blk(ts, d)),
         (e, _blk(ts, d))],
        [(jax.ShapeDtypeStruct((s, d), F32), _blk(ts, d)), (jax.ShapeDtypeStruct((s, d), BF16), _blk(ts, d)),
         (jax.ShapeDtypeStruct((s, d), BF16), _blk(ts, d)),
         (jax.ShapeDtypeStruct((SUBLANES, d), F32), _full((SUBLANES, d)))])


def final_loss(name, h, g, target, ts=256):
    s, d = h.shape
    ts = min(ts, s)

    def body(h_ref, g_ref, t_ref, loss_ref, dh_ref, dg_ref):
        x = h_ref[...]
        gf = g_ref[...]
        y = (x * _rstd(x)) * gf
        err = y - t_ref[...]
        _accumulate(loss_ref, _rowsum8(err * err) * (0.5 / d))
        dx, dg = _rms_bwd_math(err * (1.0 / d), x, gf)
        dh_ref[...] = dx
        _accumulate(dg_ref, _rowsum8(dg))

    return _rows_call(
        name, body, s, ts,
        [(h, _blk(ts, d)), (g, _full((1, d))), (target, _blk(ts, d))],
        [(jax.ShapeDtypeStruct((SUBLANES, d), F32), _full((SUBLANES, d))),
         (jax.ShapeDtypeStruct((s, d), F32), _blk(ts, d)),
         (jax.ShapeDtypeStruct((SUBLANES, d), F32), _full((SUBLANES, d)))])


def ple_bwd_elem(name, dh, g, e, ts=512):
    s, d = dh.shape
    ts = min(ts, s)

    def body(dh_ref, g_ref, e_ref, de_ref, dgl_ref):
        dh_v, g_v = dh_ref[...], g_ref[...].astype(F32)
        de_ref[...] = (dh_v * g_v).astype(BF16)
        dgl_ref[...] = (dh_v * e_ref[...] * (g_v * (1.0 - g_v))).astype(BF16)

    return _rows_call(name, body, s, ts, [(dh, _blk(ts, d)), (g, _blk(ts, d)), (e, _blk(ts, d))],
                      [(jax.ShapeDtypeStruct((s, d), BF16), _blk(ts, d)),
                       (jax.ShapeDtypeStruct((s, d), BF16), _blk(ts, d))])


CONV_LANES = 256
CONV_ROWS = 64


def _lane_chunks(d, fn):
    lc = min(CONV_LANES, d)

    def lane_body(c, carry):
        fn(pl.ds(pl.multiple_of(c * lc, lc), lc))
        return carry

    lax.fori_loop(0, d // lc, lane_body, 0)


def _shifted_copies(buf, sh, lanes):
    rows = buf.shape[0] - SUBLANES
    for s in range(1, SUBLANES):
        sh[s, pl.ds(0, rows), :] = buf[pl.ds(s, rows), lanes]


def _window(buf, sh, lanes, start, rows):
    s = start % SUBLANES
    if s == 0:
        return buf[pl.ds(start, rows), lanes]
    return sh[s, pl.ds(start - s, rows), :]


def _prev_halo_spec(ts, halo, width):
    per = ts // halo
    return pl.BlockSpec((halo, width), lambda i: (jnp.maximum(i * per - 1, 0), 0))


def _next_halo_spec(ts, halo, width, n_rows):
    per = ts // halo
    last = n_rows // halo - 1
    return pl.BlockSpec((halo, width), lambda i: (jnp.minimum((i + 1) * per, last), 0))


def cf_fwd_mid(name, a, w_dw, b_dw, gn, ts=256):
    s, d2 = a.shape
    d = d2 // 2
    ts = min(ts, s)
    hl = CONV_A_HALO
    off = hl - (CONV_A_TAPS - 1)

    rc = min(CONV_ROWS, ts)

    def body(a_ref, ah_ref, w_ref, b_ref, gn_ref, v0_ref, v1_ref, v3_ref, buf, sh):
        first = pl.program_id(0) == 0
        halo = ah_ref[...]
        hv0 = halo[:, :d] * jax.nn.sigmoid(halo[:, d:])
        buf[pl.ds(0, hl), :] = jnp.where(first, 0.0, hv0)
        main = a_ref[...]
        v0 = main[:, :d] * jax.nn.sigmoid(main[:, d:])
        buf[pl.ds(hl, ts), :] = v0
        v0_ref[...] = v0

        def conv(lanes):
            _shifted_copies(buf, sh, lanes)
            for r0 in range(0, ts, rc):
                acc = jnp.zeros((rc, lanes.size), F32)
                for k in range(CONV_A_TAPS):
                    acc = acc + w_ref[pl.ds(k, 1), lanes] * _window(buf, sh, lanes, r0 + off + k, rc)
                v1_ref[pl.ds(r0, rc), lanes] = acc + b_ref[:, lanes]

        _lane_chunks(d, conv)
        v1 = v1_ref[...]
        v2 = (v1 * _rstd(v1)) * gn_ref[...]
        v3_ref[...] = (v2 * jax.nn.sigmoid(v2)).astype(BF16)

    return _rows_call(
        name, body, s, ts,
        [(a, _blk(ts, d2)), (a, _prev_halo_spec(ts, hl, d2)), (w_dw, _full(w_dw.shape)),
         (b_dw, _full((1, d))), (gn, _full((1, d)))],
        [(jax.ShapeDtypeStruct((s, d), F32), _blk(ts, d)), (jax.ShapeDtypeStruct((s, d), F32), _blk(ts, d)),
         (jax.ShapeDtypeStruct((s, d), BF16), _blk(ts, d))],
        scratch=[pltpu.VMEM((hl + ts, d), F32), pltpu.VMEM((SUBLANES, hl + ts, min(CONV_LANES, d)), F32)],
        scratch_bytes=_nbytes((hl + ts, d + SUBLANES * CONV_LANES), F32))


def cf_bwd_rows(name, dv3, v1, gn, ts=256):
    s, d = v1.shape
    ts = min(ts, s)

    def body(dv3_ref, v1_ref, gn_ref, dv1_ref, dgn_ref, db_ref):
        v1 = v1_ref[...]
        gn_v = gn_ref[...]
        v2 = (v1 * _rstd(v1)) * gn_v
        sg = jax.nn.sigmoid(v2)
        dv2 = dv3_ref[...] * (sg * (1.0 + v2 * (1.0 - sg)))
        dv1, dgn = _rms_bwd_math(dv2, v1, gn_v)
        dv1_ref[...] = dv1
        _accumulate(dgn_ref, _rowsum8(dgn))
        _accumulate(db_ref, _rowsum8(dv1))

    return _rows_call(
        name, body, s, ts, [(dv3, _blk(ts, d)), (v1, _blk(ts, d)), (gn, _full((1, d)))],
        [(jax.ShapeDtypeStruct((s, d), F32), _blk(ts, d)),
         (jax.ShapeDtypeStruct((SUBLANES, d), F32), _full((SUBLANES, d))),
         (jax.ShapeDtypeStruct((SUBLANES, d), F32), _full((SUBLANES, d)))])


def cf_bwd_conv(name, dv1, v0, a, w_dw, ts=256):
    s, d = dv1.shape
    ts = min(ts, s)
    hl = CONV_A_HALO
    taps = CONV_A_TAPS
    off = hl - (taps - 1)
    last_blk = s // ts - 1

    rc = min(CONV_ROWS, ts)

    def body(dv1_ref, dv1n_ref, v0_ref, v0p_ref, a_ref, w_ref, da_ref, dw_ref, db_ref,
             dbuf, vbuf, dv0_buf, dw_acc, dsh, vsh):
        i = pl.program_id(0)
        dbuf[pl.ds(0, ts), :] = dv1_ref[...]
        dbuf[pl.ds(ts, hl), :] = jnp.where(i == last_blk, 0.0, dv1n_ref[...])
        vbuf[pl.ds(0, hl), :] = jnp.where(i == 0, 0.0, v0p_ref[...])
        vbuf[pl.ds(hl, ts), :] = v0_ref[...]

        @pl.when(i == 0)
        def _():
            dw_acc[...] = jnp.zeros_like(dw_acc)

        def conv_t(lanes):
            _shifted_copies(dbuf, dsh, lanes)
            _shifted_copies(vbuf, vsh, lanes)
            for r0 in range(0, ts, rc):
                g = dbuf[pl.ds(r0, rc), lanes]
                acc = jnp.zeros((rc, lanes.size), F32)
                for k in range(taps):
                    acc = acc + w_ref[pl.ds(k, 1), lanes] * _window(dbuf, dsh, lanes, r0 + taps - 1 - k, rc)
                    prod = g * _window(vbuf, vsh, lanes, r0 + off + k, rc)
                    dw_acc[pl.ds(k * SUBLANES, SUBLANES), lanes] += _rowsum8(prod)
                dv0_buf[pl.ds(r0, rc), lanes] = acc

        _lane_chunks(d, conv_t)
        dv0 = dv0_buf[...]
        av = a_ref[...]
        val, sg = av[:, :d], jax.nn.sigmoid(av[:, d:])
        dval = dv0 * sg
        dgate = dv0 * val * (sg * (1.0 - sg))
        da_ref[:, :d] = dval.astype(BF16)
        da_ref[:, d:] = dgate.astype(BF16)
        _accumulate(db_ref.at[:, pl.ds(0, d)], _rowsum8(dval))
        _accumulate(db_ref.at[:, pl.ds(d, d)], _rowsum8(dgate))

        @pl.when(i == last_blk)
        def _():
            dw_ref[...] = jnp.sum(dw_acc[...].reshape(hl, SUBLANES, d), axis=1)

    lc = min(CONV_LANES, d)
    scratch = [pltpu.VMEM((ts + hl, d), F32), pltpu.VMEM((hl + ts, d), F32), pltpu.VMEM((ts, d), F32),
               pltpu.VMEM((hl * SUBLANES, d), F32), pltpu.VMEM((SUBLANES, ts + hl, lc), F32),
               pltpu.VMEM((SUBLANES, hl + ts, lc), F32)]
    sbytes = _nbytes((3 * ts + 2 * hl + hl * SUBLANES, d), F32) + 2 * _nbytes((SUBLANES, ts + hl, lc), F32)
    return _rows_call(
        name, body, s, ts,
        [(dv1, _blk(ts, d)), (dv1, _next_halo_spec(ts, hl, d, s)), (v0, _blk(ts, d)), (v0, _prev_halo_spec(ts, hl, d)),
         (a, _blk(ts, 2 * d)), (w_dw, _full(w_dw.shape))],
        [(jax.ShapeDtypeStruct((s, 2 * d), BF16), _blk(ts, 2 * d)),
         (jax.ShapeDtypeStruct((hl, d), F32), _full((hl, d))),
         (jax.ShapeDtypeStruct((SUBLANES, 2 * d), F32), _full((SUBLANES, 2 * d)))],
        scratch=scratch, scratch_bytes=sbytes)


def sc_fwd_mid(name, bcv, w_conv, ts=256):
    s, d3 = bcv.shape
    d = d3 // 3
    ts = min(ts, s)
    hl = CONV_B_HALO
    off = hl - (CONV_B_TAPS - 1)

    def body(x_ref, xp_ref, w_ref, y_ref, buf):
        hp = xp_ref[...]
        buf[pl.ds(0, hl), :] = jnp.where(pl.program_id(0) == 0, 0.0, hp[:, d:2 * d] * hp[:, 2 * d:])
        buf[pl.ds(hl, ts), :] = x_ref[:, d:2 * d] * x_ref[:, 2 * d:]
        cc = jnp.zeros((ts, d), F32)
        for k in range(CONV_B_TAPS):
            cc = cc + w_ref[pl.ds(k, 1), :] * buf[pl.ds(off + k, ts), :]
        y_ref[...] = (x_ref[:, :d] * cc).astype(BF16)

    return _rows_call(
        name, body, s, ts,
        [(bcv, _blk(ts, d3)), (bcv, _prev_halo_spec(ts, hl, d3)), (w_conv, _full(w_conv.shape))],
        [(jax.ShapeDtypeStruct((s, d), BF16), _blk(ts, d))],
        scratch=[pltpu.VMEM((hl + ts, d), F32)], scratch_bytes=_nbytes((hl + ts, d), F32))[0]


def sc_bwd_mid(name, dy, bcv, w_conv, ts=256):
    s, d3 = bcv.shape
    d = d3 // 3
    ts = min(ts, s)
    hl = CONV_B_HALO
    taps = CONV_B_TAPS
    off = hl - (taps - 1)
    last_blk = s // ts - 1

    def body(dy_ref, dyn_ref, x_ref, xp_ref, xn_ref, w_ref, dx_ref, dw_ref, cvbuf, dbuf, dw_acc):
        i = pl.program_id(0)
        hp = xp_ref[...]
        cvbuf[pl.ds(0, hl), :] = jnp.where(i == 0, 0.0, hp[:, d:2 * d] * hp[:, 2 * d:])
        gb, gc, v = x_ref[:, :d], x_ref[:, d:2 * d], x_ref[:, 2 * d:]
        cvbuf[pl.ds(hl, ts), :] = gc * v
        dy_v = dy_ref[...]
        dcc = dy_v * gb
        dbuf[pl.ds(0, ts), :] = dcc
        dbuf[pl.ds(ts, hl), :] = jnp.where(i == last_blk, 0.0, dyn_ref[...] * xn_ref[:, :d])

        @pl.when(i == 0)
        def _():
            dw_acc[...] = jnp.zeros_like(dw_acc)

        cc = jnp.zeros((ts, d), F32)
        dcv = jnp.zeros((ts, d), F32)
        for k in range(taps):
            win = cvbuf[pl.ds(off + k, ts), :]
            cc = cc + w_ref[pl.ds(k, 1), :] * win
            dcv = dcv + w_ref[pl.ds(k, 1), :] * dbuf[pl.ds(taps - 1 - k, ts), :]
            dw_acc[pl.ds(k * SUBLANES, SUBLANES), :] += _rowsum8(dcc * win)
        dx_ref[:, :d] = (dy_v * cc).astype(BF16)
        dx_ref[:, d:2 * d] = (dcv * v).astype(BF16)
        dx_ref[:, 2 * d:] = (dcv * gc).astype(BF16)

        @pl.when(i == last_blk)
        def _():
            dw_ref[...] = jnp.sum(dw_acc[...].reshape(hl, SUBLANES, d), axis=1)

    scratch = [pltpu.VMEM((hl + ts, d), F32), pltpu.VMEM((ts + hl, d), F32), pltpu.VMEM((hl * SUBLANES, d), F32)]
    sbytes = _nbytes((2 * ts + 2 * hl + hl * SUBLANES, d), F32)
    return _rows_call(
        name, body, s, ts,
        [(dy, _blk(ts, d)), (dy, _next_halo_spec(ts, hl, d, s)), (bcv, _blk(ts, d3)), (bcv, _prev_halo_spec(ts, hl, d3)),
         (bcv, _next_halo_spec(ts, hl, d3, s)), (w_conv, _full(w_conv.shape))],
        [(jax.ShapeDtypeStruct((s, d3), BF16), _blk(ts, d3)), (jax.ShapeDtypeStruct((hl, d), F32), _full((hl, d)))],
        scratch=scratch, scratch_bytes=sbytes)


def _row(a, i):
    return lax.slice_in_dim(a, i, i + 1, axis=0)


def _local_step(x, p, target, small, get_w, conv_w, put_grads):
    depth = p.shape[0]
    acts = []
    h = x
    for i in range(depth):
        j = i // 2
        act = {"h": h}
        u = rms_fwd(f"rms_mix_{i}", h, _row(small["norm_mix"], i))
        act["u"] = u
        if i % 2 == 0:
            a = mm_x_wcol(f"cf_pw1_{i}", u, get_w("cf_w_pw1", j, u), 0, extras=[(_row(small["cf_b_pw1"], j), "row")],
                          epi=lambda acc, b: (acc + b,))[0]
            v0, v1, v3 = cf_fwd_mid(f"cf_mid_{i}", a, conv_w["cf"][j], _row(small["cf_b_dw"], j), _row(small["cf_norm"], j))
            act.update(a=a, v0=v0, v1=v1, v3=v3)
            h1 = mm_x_wrow(f"cf_pw2_{i}", v3, get_w("cf_w_pw2", j, v3),
                           extras=[(_row(small["cf_b_pw2"], j), "row"), (h, "tile")],
                           epi=lambda acc, b, res: (res + (acc + b),), tn=1024)[0]
        else:
            bcv = mm_x_wcol(f"sc_in_{i}", u, get_w("sc_w_in", j, u), 0, tn=768)[0]
            y = sc_fwd_mid(f"sc_mid_{i}", bcv, conv_w["sc"][j])
            act.update(bcv=bcv, y=y)
            h1 = mm_x_wrow(f"sc_out_{i}", y, get_w("sc_w_out", j, y), extras=[(h, "tile")],
                           epi=lambda acc, res: (res + acc,), tn=1024)[0]
        act["h1"] = h1
        u2 = rms_fwd(f"rms_mlp_{i}", h1, _row(small["norm_mlp"], i))
        z, hd = mm_x_wcol(f"mlp_w1_{i}", u2, get_w("mlp_w1", i, u2), 0, outs_dtypes=(BF16, BF16),
                          epi=lambda acc: (acc, jnp.square(jnp.maximum(acc, 0.0))))
        h2 = mm_x_wrow(f"mlp_w2_{i}", hd, get_w("mlp_w2", i, hd), extras=[(h1, "tile")],
                       epi=lambda acc, res: (res + acc,), tm=1024, tn=256, tk=8192)[0]
        act.update(u2=u2, z=z, hd=hd, h2=h2)
        n3 = rms_fwd(f"rms_ple_{i}", h2, _row(small["norm_ple"], i))
        e = mm_x_wcol(f"ple_proj_{i}", p[i], get_w("ple_w_proj", i, n3), 0)[0]

        def ple_epi(acc, e_t, res):
            g_t = jax.nn.sigmoid(acc)
            return g_t, res + g_t * e_t

        g, h3 = mm_x_wrow(f"ple_gate_{i}", n3, get_w("ple_w_gate", i, e), extras=[(e, "tile"), (h2, "tile")],
                          outs_dtypes=(BF16, F32), epi=ple_epi)
        act.update(n3=n3, e=e, g=g)
        acts.append(act)
        h = h3

    loss_part, dh, dg_final = final_loss("final_loss", h, small["norm_final"], target)
    sg = {k: [None] * small[k].shape[0] for k in small if k != "norm_final"}
    sg["norm_final"] = [dg_final]
    sg["cf_w_dw"] = [None] * conv_w["cf"].shape[0]
    sg["sc_w_conv"] = [None] * conv_w["sc"].shape[0]

    ple_grads = None
    for i in reversed(range(depth)):
        j = i // 2
        act = acts[i]
        de, dgl = ple_grads if ple_grads is not None else ple_bwd_elem(f"ple_bwd_{i}", dh, act["g"], act["e"])
        g_proj = mm_xt_dy(f"d_ple_proj_{i}", p[i], de, True)
        g_gate = mm_xt_dy(f"d_ple_gate_{i}", act["n3"], dgl, False)
        dn3 = mm_dy_wrow_t(f"dn3_{i}", dgl, get_w("ple_w_gate", i))[0]
        dh2, dh2b, sg["norm_ple"][i], _ = rms_bwd(f"rms_ple_bwd_{i}", dn3, act["h2"], _row(small["norm_ple"], i), dh)
        g_w2 = mm_xt_dy(f"d_mlp_w2_{i}", act["hd"], dh2b, False)
        dz = mm_dy_wrow_t(f"dz_{i}", dh2b, get_w("mlp_w2", i), extras=[(act["z"], "tile")], outs_dtypes=(BF16,),
                          epi=lambda acc, z_t: (acc * (2.0 * jnp.maximum(z_t.astype(F32), 0.0)),))[0]
        g_w1 = mm_xt_dy(f"d_mlp_w1_{i}", act["u2"], dz, True)
        put_grads({("ple_w_proj", i): g_proj, ("ple_w_gate", i): g_gate, ("mlp_w2", i): g_w2, ("mlp_w1", i): g_w1})
        du2 = mm_dy_wcol_t(f"du2_{i}", dz, get_w("mlp_w1", i), 0)[0]
        dh1, dh1b, sg["norm_mlp"][i], cs1 = rms_bwd(f"rms_mlp_bwd_{i}", du2, act["h1"], _row(small["norm_mlp"], i), dh2)
        if i % 2 == 0:
            g_out = mm_xt_dy(f"d_cf_pw2_{i}", act["v3"], dh1b, False)
            sg["cf_b_pw2"][j] = cs1
            dv3 = mm_dy_wrow_t(f"dv3_{i}", dh1b, get_w("cf_w_pw2", j))[0]
            dv1, sg["cf_norm"][j], sg["cf_b_dw"][j] = cf_bwd_rows(f"cf_bwd_rows_{i}", dv3, act["v1"], _row(small["cf_norm"], j))
            da, sg["cf_w_dw"][j], sg["cf_b_pw1"][j] = cf_bwd_conv(f"cf_bwd_conv_{i}", dv1, act["v0"], act["a"], conv_w["cf"][j])
            g_in = mm_xt_dy(f"d_cf_pw1_{i}", act["u"], da, True)
            put_grads({("cf_w_pw2", j): g_out, ("cf_w_pw1", j): g_in})
            du = mm_dy_wcol_t(f"du_{i}", da, get_w("cf_w_pw1", j), 0)[0]
        else:
            g_out = mm_xt_dy(f"d_sc_out_{i}", act["y"], dh1b, False)
            dy = mm_dy_wrow_t(f"dy_{i}", dh1b, get_w("sc_w_out", j))[0]
            dbcv, sg["sc_w_conv"][j] = sc_bwd_mid(f"sc_bwd_mid_{i}", dy, act["bcv"], conv_w["sc"][j])
            g_in = mm_xt_dy(f"d_sc_in_{i}", act["u"], dbcv, True, tn=768)
            put_grads({("sc_w_out", j): g_out, ("sc_w_in", j): g_in})
            du = mm_dy_wcol_t(f"du_{i}", dbcv, get_w("sc_w_in", j), 0)[0]
        if i > 0:
            below = acts[i - 1]
            dh, de, dgl, sg["norm_mix"][i] = rms_bwd_ple(f"rms_mix_bwd_{i}", du, act["h"], _row(small["norm_mix"], i), dh1,
                                                         below["g"], below["e"])
            ple_grads = (de, dgl)
        else:
            dh, _, sg["norm_mix"][i], _ = rms_bwd(f"rms_mix_bwd_{i}", du, act["h"], _row(small["norm_mix"], i), dh1)
    return loss_part, dh, sg


def _me_and_peers():
    x, y, c = lax.axis_index("x"), lax.axis_index("y"), lax.axis_index("c")
    me = 4 * x + 2 * y + c
    peers = []
    for q in range(1, NDEV):
        px = 1 - x if q & 4 else x
        py = 1 - y if q & 2 else y
        pc = 1 - c if q & 1 else c
        peers.append(((px, py, pc), 4 * px + 2 * py + pc))
    return me, peers


def _exchange(name, srcs, out_shapes, src_fns, dst_fns, after=()):
    n = len(srcs)
    n_after = len(after)

    def body(*refs):
        ins, outs = refs[:n], refs[n + n_after:2 * n + n_after]
        send_sems, recv_sems, local_sems = refs[2 * n + n_after:]
        me, peers = _me_and_peers()
        local, remote = [], []
        for k in range(n):
            cp = pltpu.make_async_copy(src_fns[k](ins[k], me), dst_fns[k](outs[k], me), local_sems.at[k])
            cp.start()
            local.append(cp)
        for q, (peer, peer_blk) in enumerate(peers):
            for k in range(n):
                cp = pltpu.make_async_remote_copy(
                    src_ref=src_fns[k](ins[k], peer_blk), dst_ref=dst_fns[k](outs[k], me),
                    send_sem=send_sems.at[k, q], recv_sem=recv_sems.at[k, q],
                    device_id=peer, device_id_type=MESH)
                cp.start()
                remote.append(cp)
        for q, (peer, peer_blk) in enumerate(peers):
            for k in range(n):
                pltpu.make_async_remote_copy(
                    src_ref=src_fns[k](ins[k], peer_blk), dst_ref=dst_fns[k](outs[k], peer_blk),
                    send_sem=send_sems.at[k, q], recv_sem=recv_sems.at[k, q],
                    device_id=peer, device_id_type=MESH).wait_recv()
        for cp in remote:
            cp.wait_send()
        for cp in local:
            cp.wait()

    any_spec = pl.BlockSpec(memory_space=pl.ANY)
    return pl.pallas_call(
        body,
        name=name,
        in_specs=[any_spec] * (n + n_after),
        out_specs=[any_spec] * n,
        out_shape=out_shapes,
        scratch_shapes=[pltpu.SemaphoreType.DMA((n, N_PEERS)), pltpu.SemaphoreType.DMA((n, N_PEERS)),
                        pltpu.SemaphoreType.DMA((n,))],
    )(*srcs, *after)


def sc_exchange(name, collective_id, srcs, out_shapes, src_fns, dst_fns):
    n = len(srcs)

    def body(*refs):
        ins, outs = refs[:n], refs[n:2 * n]
        send_sems, recv_sems, local_sems = refs[2 * n:]
        me, peers = _me_and_peers()
        barrier = pltpu.get_barrier_semaphore()
        for peer, _ in peers:
            pl.semaphore_signal(barrier, inc=1, device_id=peer, device_id_type=MESH)
        pl.semaphore_wait(barrier, N_PEERS)
        local, remote = [], []
        for k in range(n):
            cp = pltpu.make_async_copy(src_fns[k](ins[k], me), dst_fns[k](outs[k], me), local_sems.at[k])
            cp.start()
            local.append(cp)
        for q, (peer, peer_blk) in enumerate(peers):
            for k in range(n):
                cp = pltpu.make_async_remote_copy(
                    src_ref=src_fns[k](ins[k], peer_blk), dst_ref=dst_fns[k](outs[k], me),
                    send_sem=send_sems.at[k, q], recv_sem=recv_sems.at[k, q],
                    device_id=peer, device_id_type=MESH)
                cp.start()
                remote.append(cp)
        for q, (peer, peer_blk) in enumerate(peers):
            for k in range(n):
                pltpu.make_async_remote_copy(
                    src_ref=src_fns[k](ins[k], peer_blk), dst_ref=dst_fns[k](outs[k], peer_blk),
                    send_sem=send_sems.at[k, q], recv_sem=recv_sems.at[k, q],
                    device_id=peer, device_id_type=MESH).wait_recv()
        for cp in remote:
            cp.wait_send()
        for cp in local:
            cp.wait()

    return pl.kernel(
        body,
        out_type=out_shapes,
        mesh=plsc.ScalarSubcoreMesh(axis_name="sequencer", num_cores=1),
        name=name,
        scratch_types=[pltpu.SemaphoreType.DMA((n, N_PEERS)), pltpu.SemaphoreType.DMA((n, N_PEERS)),
                       pltpu.SemaphoreType.DMA((n,))],
        compiler_params=pltpu.CompilerParams(collective_id=collective_id),
    )(*srcs)


def sc_gather(name, collective_id, srcs, layers):
    n = len(srcs)
    outs_shape = [jax.ShapeDtypeStruct((NDEV, 1) + a.shape[1:], a.dtype) for a in srcs]

    def body(*refs):
        ins, outs = refs[:n], refs[n:2 * n]
        send_sems, recv_sems, local_sems = refs[2 * n:]
        x, y, c = lax.axis_index("x"), lax.axis_index("y"), lax.axis_index("c")
        me = 4 * x + 2 * y + c
        sibling = (x, y, 1 - c)
        chips = [(1 - x, y), (x, 1 - y), (1 - x, 1 - y)]
        barrier = pltpu.get_barrier_semaphore()
        for peer in [sibling] + [(cx, cy, c) for cx, cy in chips]:
            pl.semaphore_signal(barrier, inc=1, device_id=peer, device_id_type=MESH)
        pl.semaphore_wait(barrier, 1 + len(chips))

        def copy(k, slot, blk, to, src=None):
            place = outs[k].at[blk, 0]
            return pltpu.make_async_remote_copy(
                src_ref=place if src is None else src, dst_ref=place,
                send_sem=send_sems.at[k, slot], recv_sem=recv_sems.at[k, slot],
                device_id=to, device_id_type=MESH)

        local, sent = [], []
        for k in range(n):
            mine = ins[k].at[layers[k]]
            cp = pltpu.make_async_copy(mine, outs[k].at[me, 0], local_sems.at[k])
            cp.start()
            local.append(cp)
            sent.append(copy(k, 0, me, sibling, src=mine))
            sent += [copy(k, 1 + j, me, (cx, cy, c), src=mine) for j, (cx, cy) in enumerate(chips)]
        for cp in sent:
            cp.start()
        for k in range(n):
            for j, (cx, cy) in enumerate(chips):
                blk = 4 * cx + 2 * cy + c
                copy(k, 1 + j, blk, sibling).wait_recv()
                fwd = copy(k, 4 + j, blk, sibling)
                fwd.start()
                sent.append(fwd)
        for k in range(n):
            copy(k, 0, 4 * x + 2 * y + (1 - c), sibling).wait_recv()
            for j, (cx, cy) in enumerate(chips):
                copy(k, 4 + j, 4 * cx + 2 * cy + (1 - c), sibling).wait_recv()
        for cp in sent:
            cp.wait_send()
        for cp in local:
            cp.wait()

    return pl.kernel(
        body,
        out_type=outs_shape,
        mesh=plsc.ScalarSubcoreMesh(axis_name="sequencer", num_cores=1),
        name=name,
        scratch_types=[pltpu.SemaphoreType.DMA((n, N_PEERS)), pltpu.SemaphoreType.DMA((n, N_PEERS)),
                       pltpu.SemaphoreType.DMA((n,))],
        compiler_params=pltpu.CompilerParams(collective_id=collective_id),
    )(*srcs)


def _gather_src(layer):
    return lambda ref, blk: ref.at[layer]


def _gather_dst(ref, blk):
    return ref.at[blk, 0]


def _slice_of(ref, blk):
    return ref.at[blk]


def cast_bf16(name, w, tr_elems=512 * 1024):
    l, r, c = w.shape
    tr = _row_tile(r, tr_elems // c)
    spec = pl.BlockSpec((None, tr, c), lambda li, i: (li, i, 0))

    def body(w_ref, o_ref):
        o_ref[...] = w_ref[...].astype(BF16)

    return pl.pallas_call(
        body, name=name, grid=(l, r // tr), in_specs=[spec], out_specs=spec,
        out_shape=jax.ShapeDtypeStruct(w.shape, BF16),
        compiler_params=_params(("parallel", "parallel"), 6 * tr * c),
    )(w)


def _adamw_math(w, g, m, v):
    m = ADAM_B1 * m + (1.0 - ADAM_B1) * g
    v = ADAM_B2 * v + (1.0 - ADAM_B2) * (g * g)
    m_hat = m * (1.0 / (1.0 - ADAM_B1 ** ADAM_STEP))
    v_hat = v * (1.0 / (1.0 - ADAM_B2 ** ADAM_STEP))
    delta = -ADAM_LR * (m_hat / (jnp.sqrt(v_hat) + ADAM_EPS) + ADAM_WD * w)
    return delta, m, v


def _sum_blocks(ref):
    g = ref[0].astype(F32)
    for d in range(1, ref.shape[0]):
        g = g + ref[d].astype(F32)
    return g


def adamw_layer(name, recv, w, m, v, layer, stacked, after=None, tr_elems=256 * 1024):
    nd, r, c = recv.shape
    tr = _row_tile(r, tr_elems // c)
    r_spec = pl.BlockSpec((nd, tr, c), lambda i: (0, i, 0))
    w_spec = pl.BlockSpec((None, tr, c), lambda i: (layer, i, 0))
    if stacked is None:
        stacked = [lax.empty(w.shape, F32) for _ in range(4)]
    after = [] if after is None else [after]

    def body(r_ref, w_ref, m_ref, v_ref, g_in, d_in, m_in, v_in, *rest):
        g_out, d_out, m_out, v_out = rest[len(after):]
        g = _sum_blocks(r_ref)
        delta, m_new, v_new = _adamw_math(w_ref[...], g, m_ref[...], v_ref[...])
        g_out[...] = g
        d_out[...] = delta
        m_out[...] = m_new
        v_out[...] = v_new

    out = jax.ShapeDtypeStruct(w.shape, F32)
    return pl.pallas_call(
        body, name=name, grid=(r // tr,),
        in_specs=[r_spec, w_spec, w_spec, w_spec] + [pl.BlockSpec(memory_space=pl.ANY)] * (4 + len(after)),
        out_specs=[w_spec] * 4, out_shape=[out] * 4,
        input_output_aliases={4: 0, 5: 1, 6: 2, 7: 3},
        compiler_params=_params(("parallel",), tr * c * (2 * nd + 7 * 4)),
    )(recv, w, m, v, *stacked, *after)


def pack_small_grads(name, parts, taps, n_blocks):
    d = parts[0].shape[1]
    n_p, rows = len(parts), [t.shape[0] for t in taps]
    cb = d // n_blocks

    def body(*refs):
        part_refs, tap_refs = refs[:n_p], refs[n_p:n_p + len(taps)]
        sums_out, taps_out = refs[n_p + len(taps):]
        for i, r in enumerate(part_refs):
            sums_out[pl.ds(i, 1), :] = jnp.sum(r[...], axis=0, keepdims=True)
        r0 = 0
        for t_ref, n in zip(tap_refs, rows):
            for b in range(n_blocks):
                taps_out[b, pl.ds(r0, n), :] = t_ref[:, pl.ds(b * cb, cb)]
            r0 += n

    vm = pl.BlockSpec(memory_space=pltpu.VMEM)
    return pl.pallas_call(
        body, name=name, in_specs=[vm] * (n_p + len(taps)), out_specs=[vm] * 2,
        out_shape=[jax.ShapeDtypeStruct((n_p, d), F32), jax.ShapeDtypeStruct((n_blocks, sum(rows), cb), F32)],
    )(*parts, *taps)


def small_update(name, part_g, tap_g, w_a, m_a, v_a, w_b, m_b, v_b):
    nd, na, d = part_g.shape
    nb, cb = w_b.shape

    def body(pg_ref, tg_ref, wa_ref, ma_ref, va_ref, wb_ref, mb_ref, vb_ref,
             ga_out, da_out, ma_out, va_out, gb_out, db_out, mb_out, vb_out, loss_out):
        ga = _sum_blocks(pg_ref)
        delta, m_new, v_new = _adamw_math(wa_ref[...], ga, ma_ref[...], va_ref[...])
        ga_out[...] = ga
        da_out[...] = delta
        ma_out[...] = m_new
        va_out[...] = v_new
        loss_out[...] = jnp.broadcast_to(jnp.sum(ga[na - 1:na, :], axis=1, keepdims=True), loss_out.shape)
        gb = _sum_blocks(tg_ref)
        delta, m_new, v_new = _adamw_math(wb_ref[...], gb, mb_ref[...], vb_ref[...])
        gb_out[...] = gb
        db_out[...] = delta
        mb_out[...] = m_new
        vb_out[...] = v_new

    oa, ob = jax.ShapeDtypeStruct((na, d), F32), jax.ShapeDtypeStruct((nb, cb), F32)
    vm = pl.BlockSpec(memory_space=pltpu.VMEM)
    return pl.pallas_call(
        body, name=name, in_specs=[vm] * 8, out_specs=[vm] * 9,
        out_shape=[oa] * 4 + [ob] * 4 + [jax.ShapeDtypeStruct((1, LANES), F32)],
        compiler_params=pltpu.CompilerParams(vmem_limit_bytes=_vmem_limit(_nbytes(part_g.shape, F32))),
    )(part_g, tap_g, w_a, m_a, v_a, w_b, m_b, v_b)


BIG = ("cf_w_pw1", "cf_w_pw2", "sc_w_in", "sc_w_out", "mlp_w1", "mlp_w2", "ple_w_proj", "ple_w_gate")
COL_SHARDED = ("cf_w_pw1", "sc_w_in", "mlp_w1", "ple_w_proj")
WEIGHT_ORDER = ("norm_mix", "norm_mlp", "norm_ple", "cf_w_pw1", "cf_b_pw1", "cf_w_dw", "cf_b_dw", "cf_norm",
                "cf_w_pw2", "cf_b_pw2", "sc_w_in", "sc_w_conv", "sc_w_out", "mlp_w1", "mlp_w2", "ple_w_proj",
                "ple_w_gate", "norm_final")
SMALL_ROWS = (("norm_mix", 4), ("norm_mlp", 4), ("norm_ple", 4), ("cf_b_pw1", 4), ("cf_b_dw", 2), ("cf_norm", 2),
              ("cf_b_pw2", 2), ("norm_final", 1))


def _layer_weights(i):
    mixer = (("cf_w_pw1", i // 2), ("cf_w_pw2", i // 2)) if i % 2 == 0 else (("sc_w_in", i // 2), ("sc_w_out", i // 2))
    return mixer + (("mlp_w1", i), ("mlp_w2", i), ("ple_w_proj", i), ("ple_w_gate", i))


def _pad_rows(a, rows):
    return jnp.pad(a, ((0, 0), (0, rows - a.shape[1]), (0, 0)))


def _pack_taps(cf, sc):
    c = cf.shape[2]
    return jnp.concatenate([_pad_rows(cf, CONV_A_HALO).reshape(-1, c), _pad_rows(sc, CONV_B_HALO).reshape(-1, c)], axis=0)


def _unpack_taps(t, n_cf):
    c = t.shape[1]
    cf = t[:n_cf * CONV_A_HALO].reshape(n_cf, CONV_A_HALO, c)[:, :CONV_A_TAPS]
    sc = t[n_cf * CONV_A_HALO:].reshape(-1, CONV_B_HALO, c)[:, :CONV_B_TAPS]
    return cf, sc


def _pack_small(vals, d):
    return jnp.concatenate([vals[k].reshape(-1, d) for k, _ in SMALL_ROWS] + [jnp.zeros((1, d), F32)], axis=0)


def _unpack_small(a, shapes):
    out, r = {}, 0
    for k, n in SMALL_ROWS:
        out[k] = a[r:r + n].reshape(shapes[k])
        r += n
    return out


def kernel(x, p, norm_mix, norm_mlp, norm_ple, cf_w_pw1, cf_b_pw1, cf_w_dw, cf_b_dw, cf_norm, cf_w_pw2, cf_b_pw2, sc_w_in, sc_w_conv, sc_w_out, mlp_w1, mlp_w2, ple_w_proj, ple_w_gate, norm_final, loss_target, m_norm_mix, m_norm_mlp, m_norm_ple, m_cf_w_pw1, m_cf_b_pw1, m_cf_w_dw, m_cf_b_dw, m_cf_norm, m_cf_w_pw2, m_cf_b_pw2, m_sc_w_in, m_sc_w_conv, m_sc_w_out, m_mlp_w1, m_mlp_w2, m_ple_w_proj, m_ple_w_gate, m_norm_final, v_norm_mix, v_norm_mlp, v_norm_ple, v_cf_w_pw1, v_cf_b_pw1, v_cf_w_dw, v_cf_b_dw, v_cf_norm, v_cf_w_pw2, v_cf_b_pw2, v_sc_w_in, v_sc_w_conv, v_sc_w_out, v_mlp_w1, v_mlp_w2, v_ple_w_proj, v_ple_w_gate, v_norm_final):
    w = dict(norm_mix=norm_mix, norm_mlp=norm_mlp, norm_ple=norm_ple, cf_w_pw1=cf_w_pw1, cf_b_pw1=cf_b_pw1,
             cf_w_dw=cf_w_dw, cf_b_dw=cf_b_dw, cf_norm=cf_norm, cf_w_pw2=cf_w_pw2, cf_b_pw2=cf_b_pw2,
             sc_w_in=sc_w_in, sc_w_conv=sc_w_conv, sc_w_out=sc_w_out, mlp_w1=mlp_w1, mlp_w2=mlp_w2,
             ple_w_proj=ple_w_proj, ple_w_gate=ple_w_gate, norm_final=norm_final)
    m = dict(norm_mix=m_norm_mix, norm_mlp=m_norm_mlp, norm_ple=m_norm_ple, cf_w_pw1=m_cf_w_pw1, cf_b_pw1=m_cf_b_pw1,
             cf_w_dw=m_cf_w_dw, cf_b_dw=m_cf_b_dw, cf_norm=m_cf_norm, cf_w_pw2=m_cf_w_pw2, cf_b_pw2=m_cf_b_pw2,
             sc_w_in=m_sc_w_in, sc_w_conv=m_sc_w_conv, sc_w_out=m_sc_w_out, mlp_w1=m_mlp_w1, mlp_w2=m_mlp_w2,
             ple_w_proj=m_ple_w_proj, ple_w_gate=m_ple_w_gate, norm_final=m_norm_final)
    v = dict(norm_mix=v_norm_mix, norm_mlp=v_norm_mlp, norm_ple=v_norm_ple, cf_w_pw1=v_cf_w_pw1, cf_b_pw1=v_cf_b_pw1,
             cf_w_dw=v_cf_w_dw, cf_b_dw=v_cf_b_dw, cf_norm=v_cf_norm, cf_w_pw2=v_cf_w_pw2, cf_b_pw2=v_cf_b_pw2,
             sc_w_in=v_sc_w_in, sc_w_conv=v_sc_w_conv, sc_w_out=v_sc_w_out, mlp_w1=v_mlp_w1, mlp_w2=v_mlp_w2,
             ple_w_proj=v_ple_w_proj, ple_w_gate=v_ple_w_gate, norm_final=v_norm_final)
    depth, d = norm_mix.shape
    n_cf = cf_w_dw.shape[0]

    taps_w = _pack_taps(cf_w_dw, sc_w_conv)
    shards, gathered = {}, {}
    ids = iter(range(5 * depth))
    for i in range(depth):
        names = _layer_weights(i)
        groups = [names[:1], names[1:2], names[2:3], names[3:]] if i == 0 else [names[:2], names[2:]]
        for n_group, group in enumerate(groups):
            with_taps = i == 0 and n_group == 1
            for k, _ in group:
                if k not in shards:
                    shards[k] = cast_bf16(f"cast_{k}", w[k])
            srcs = [shards[k] for k, _ in group] + ([taps_w[None]] if with_taps else [])
            got = sc_gather(f"gather_{i}_{n_group}", next(ids), srcs, [l for _, l in group] + ([0] if with_taps else []))
            for (k, l), g in zip(group, got):
                gathered[(k, l)] = g if k in COL_SHARDED else g.reshape(-1, g.shape[3])
            if with_taps:
                taps_full = jnp.transpose(got[-1][:, 0], (1, 0, 2)).reshape(taps_w.shape[0], d)
    conv_w = {"cf": taps_full[:n_cf * CONV_A_HALO].reshape(n_cf, CONV_A_HALO, d),
              "sc": taps_full[n_cf * CONV_A_HALO:].reshape(-1, CONV_B_HALO, d)}

    def get_w(k, l, after=None):
        return gathered[(k, l)]

    received, waiting = {}, {}
    last_group = list(_layer_weights(0)[:2])

    def put_grads(grads):
        names = list(grads)
        if names[0][0] in ("cf_w_pw2", "sc_w_out") and set(names) != set(last_group):
            waiting.update(grads)
            return
        grads = {**waiting, **grads}
        waiting.clear()
        names = list(grads)
        got = sc_exchange(f"grad_exchange_{names[-1][0]}_{names[-1][1]}", next(ids), [grads[n] for n in names],
                          [jax.ShapeDtypeStruct(grads[n].shape, BF16) for n in names],
                          [_slice_of] * len(names), [_slice_of] * len(names))
        received.update(zip(names, got))

    small = {k: w[k] for k, _ in SMALL_ROWS}
    small["norm_final"] = norm_final[None]
    loss_part, grad_x, sg = _local_step(x[0], p[:, 0], loss_target[0], small, get_w, conv_w, put_grads)

    out = {k: None for k in BIG}
    previous = None
    for (k, l), recv in received.items():
        if (k, l) not in last_group:
            out[k] = adamw_layer(f"adamw_{k}_{l}", recv, w[k], m[k], v[k], l, out[k], after=previous)
            previous = out[k][1]
    updated_first = [out[k][0] for k in BIG if out[k] is not None and k not in [n for n, _ in last_group]]

    parts = []
    for k, n in SMALL_ROWS:
        for g in sg[k]:
            parts += [g[:, :d], g[:, d:]] if g.shape[1] == 2 * d else [g]
    parts.append(loss_part)
    sums, tap_slices = pack_small_grads("pack_small_grads", parts, sg["cf_w_dw"] + sg["sc_w_conv"], NDEV)
    part_all, tap_mine = _exchange(
        "small_exchange", [sums[None], tap_slices],
        [jax.ShapeDtypeStruct((NDEV, 1) + sums.shape, F32), jax.ShapeDtypeStruct(tap_slices.shape, F32)],
        [_gather_src(0), _slice_of], [_gather_dst, _slice_of], after=updated_first)
    for k, l in last_group:
        out[k] = adamw_layer(f"adamw_{k}_{l}", received[(k, l)], w[k], m[k], v[k], l, out[k], after=part_all)
    sm = small_update("small_update", part_all[:, 0], tap_mine,
                      _pack_small(w, d), _pack_small(m, d), _pack_small(v, d),
                      taps_w, _pack_taps(m["cf_w_dw"], m["sc_w_conv"]), _pack_taps(v["cf_w_dw"], v["sc_w_conv"]))
    shapes = {k: w[k].shape for k, _ in SMALL_ROWS}
    for t in range(4):
        un = _unpack_small(sm[t], shapes)
        cf_t, sc_t = _unpack_taps(sm[4 + t], n_cf)
        for k in un:
            out.setdefault(k, [None] * 4)[t] = un[k]
        out.setdefault("cf_w_dw", [None] * 4)[t] = cf_t
        out.setdefault("sc_w_conv", [None] * 4)[t] = sc_t
    loss = sm[8][0, 0]

    return (loss, grad_x[None], *[out[k][0] for k in WEIGHT_ORDER], *[out[k][1] for k in WEIGHT_ORDER],
            *[out[k][2] for k in WEIGHT_ORDER], *[out[k][3] for k in WEIGHT_ORDER])
```

```python
import jax
import jax.numpy as jnp
from jax import lax
from jax.experimental import pallas as pl
from jax.experimental.pallas import tpu as pltpu
from jax.experimental.pallas import tpu_sc as plsc

F32 = jnp.float32
BF16 = jnp.bfloat16
EPS = 1e-6
NDEV = 8
N_PEERS = NDEV - 1
MESH = pl.DeviceIdType.MESH

ADAM_LR = 0.001
ADAM_B1 = 0.9
ADAM_B2 = 0.999
ADAM_EPS = 1e-08
ADAM_WD = 0.01
ADAM_STEP = 10

V7X_VMEM_BYTES = 64 * 1024 * 1024
VMEM_LIMIT_MAX = 56 * 1024 * 1024
SUBLANES = 8
LANES = 128
CONV_A_TAPS = 31
CONV_A_HALO = 32
CONV_B_TAPS = 3
CONV_B_HALO = 8


def _nbytes(shape, dtype):
    n = 1
    for s in shape:
        if s is not None:
            n *= s
    return n * jnp.dtype(dtype).itemsize


def _vmem_limit(block_bytes, scratch_bytes=0):
    need = 2 * block_bytes + scratch_bytes
    return int(min(VMEM_LIMIT_MAX, max(32 * 1024 * 1024, need + need // 2 + (4 << 20))))


def _params(sem, block_bytes, scratch_bytes=0):
    return pltpu.CompilerParams(dimension_semantics=sem, vmem_limit_bytes=_vmem_limit(block_bytes, scratch_bytes))


_DIMS = {
    "nn": (((1,), (0,)), ((), ())),
    "nt": (((1,), (1,)), ((), ())),
    "tn": (((0,), (0,)), ((), ())),
}


def _mm(name, dims, grid, acc_shape, a, a_spec, b, b_spec, extras, outs, epi):
    ni, nj, nk = grid
    n_ex, n_out = len(extras), len(outs)
    dn = _DIMS[dims]
    b_sub = [s for s in b_spec.block_shape if s is not None]
    n_sub = b_sub[0] if len(b_sub) == 3 else 1

    def body(*refs):
        a_ref, b_ref = refs[0], refs[1]
        ex_refs = refs[2:2 + n_ex]
        out_refs = refs[2 + n_ex:2 + n_ex + n_out]
        if n_sub == 1:
            d = lax.dot_general(a_ref[...].astype(BF16), b_ref[...].astype(BF16), dn, preferred_element_type=F32)
        else:
            w_sub = a_ref.shape[1] // n_sub
            d = None
            for s in range(n_sub):
                part = lax.dot_general(a_ref[:, pl.ds(s * w_sub, w_sub)].astype(BF16), b_ref[s].astype(BF16), dn,
                                       preferred_element_type=F32)
                d = part if d is None else d + part

        def finish(acc):
            res = epi(acc, *[r[...] for r in ex_refs])
            for o_ref, r in zip(out_refs, res):
                o_ref[...] = r.astype(o_ref.dtype)

        if nk == 1:
            finish(d)
        else:
            acc_ref = refs[2 + n_ex + n_out]
            k = pl.program_id(2)

            @pl.when(k == 0)
            def _():
                acc_ref[...] = d

            @pl.when(jnp.logical_and(k > 0, k < nk - 1))
            def _():
                acc_ref[...] += d

            @pl.when(k == nk - 1)
            def _():
                finish(acc_ref[...] + d)

    blk = _nbytes(a_spec.block_shape, a.dtype) + _nbytes(b_spec.block_shape, b.dtype)
    for arr, spec in list(extras) + list(outs):
        blk += _nbytes(spec.block_shape, arr.dtype)
    acc_bytes = _nbytes(acc_shape, F32)
    scratch = [pltpu.VMEM(acc_shape, F32)] if nk > 1 else []
    return pl.pallas_call(
        body,
        name=name,
        grid=grid,
        in_specs=[a_spec, b_spec] + [s for _, s in extras],
        out_specs=[s for _, s in outs],
        out_shape=[o for o, _ in outs],
        scratch_shapes=scratch,
        compiler_params=_params(("parallel", "parallel", "arbitrary"), blk, 3 * acc_bytes),
    )(a, b, *[e for e, _ in extras])


def _tile(n, pref):
    if n <= pref:
        return n
    t = pref - pref % LANES
    while t > LANES and n % t:
        t -= LANES
    assert n % t == 0, (n, pref)
    return t


def _row_tile(n, pref):
    if n <= pref:
        return n
    t = max(SUBLANES, pref - pref % SUBLANES)
    while t > SUBLANES and n % t:
        t -= SUBLANES
    assert n % t == 0, (n, pref)
    return t


def _id_epi(acc):
    return (acc,)


def mm_x_wcol(name, x, w, layer, extras=(), outs_dtypes=(F32,), epi=_id_epi, tm=1024, tn=1024):
    m, kdim = x.shape
    c = w.shape[3]
    n = NDEV * c
    tm, tn = _tile(m, tm), _tile(c, tn)
    tk = _tile(kdim, 2048)
    grid = (m // tm, n // tn, kdim // tk)
    per = c // tn
    a_spec = pl.BlockSpec((tm, tk), lambda i, j, k: (i, k))
    b_spec = pl.BlockSpec((None, None, tk, tn), lambda i, j, k: (j // per, layer, k, j % per))
    ex = [(e, _ex_spec(e, kind, tm, tn)) for e, kind in extras]
    o_spec = pl.BlockSpec((tm, tn), lambda i, j, k: (i, j))
    outs = [(jax.ShapeDtypeStruct((m, n), dt), o_spec) for dt in outs_dtypes]
    return _mm(name, "nn", grid, (tm, tn), x, a_spec, w, b_spec, ex, outs, epi)


def mm_x_wrow(name, x, w, extras=(), outs_dtypes=(F32,), epi=_id_epi, tm=1024, tn=512, tk=2048):
    m, kdim = x.shape
    n = w.shape[1]
    assert kdim == w.shape[0]
    tm, tn = _tile(m, tm), _tile(n, tn)
    tk = _tile(kdim, tk)
    grid = (m // tm, n // tn, kdim // tk)
    a_spec = pl.BlockSpec((tm, tk), lambda i, j, k: (i, k))
    b_spec = pl.BlockSpec((tk, tn), lambda i, j, k: (k, j))
    ex = [(e, _ex_spec(e, kind, tm, tn)) for e, kind in extras]
    o_spec = pl.BlockSpec((tm, tn), lambda i, j, k: (i, j))
    outs = [(jax.ShapeDtypeStruct((m, n), dt), o_spec) for dt in outs_dtypes]
    return _mm(name, "nn", grid, (tm, tn), x, a_spec, w, b_spec, ex, outs, epi)


def mm_dy_wcol_t(name, dy, w, layer, extras=(), outs_dtypes=(F32,), epi=_id_epi, tm=1024, b_block_bytes=4 << 20):
    m, n = dy.shape
    nd, kdim, c = w.shape[0], w.shape[2], w.shape[3]
    assert n == nd * c
    tm, tn = _tile(m, tm), _tile(kdim, max(LANES, b_block_bytes // (n * 2)))
    n_sub = nd
    grid = (m // tm, kdim // tn, nd // n_sub)
    a_spec = pl.BlockSpec((tm, n_sub * c), lambda i, j, k: (i, k))
    if n_sub > 1:
        b_spec = pl.BlockSpec((n_sub, None, tn, c), lambda i, j, k: (k, layer, j, 0))
    else:
        b_spec = pl.BlockSpec((None, None, tn, c), lambda i, j, k: (k, layer, j, 0))
    ex = [(e, _ex_spec(e, kind, tm, tn)) for e, kind in extras]
    o_spec = pl.BlockSpec((tm, tn), lambda i, j, k: (i, j))
    outs = [(jax.ShapeDtypeStruct((m, kdim), dt), o_spec) for dt in outs_dtypes]
    return _mm(name, "nt", grid, (tm, tn), dy, a_spec, w, b_spec, ex, outs, epi)


def mm_dy_wrow_t(name, dy, w, extras=(), outs_dtypes=(F32,), epi=_id_epi, tm=1024, tn=1024):
    m, n = dy.shape
    kdim = w.shape[0]
    assert n == w.shape[1]
    tm, tn = _tile(m, tm), _tile(kdim, tn)
    tk = _tile(n, 2048)
    grid = (m // tm, kdim // tn, n // tk)
    a_spec = pl.BlockSpec((tm, tk), lambda i, j, k: (i, k))
    b_spec = pl.BlockSpec((tn, tk), lambda i, j, k: (j, k))
    ex = [(e, _ex_spec(e, kind, tm, tn)) for e, kind in extras]
    o_spec = pl.BlockSpec((tm, tn), lambda i, j, k: (i, j))
    outs = [(jax.ShapeDtypeStruct((m, kdim), dt), o_spec) for dt in outs_dtypes]
    return _mm(name, "nt", grid, (tm, tn), dy, a_spec, w, b_spec, ex, outs, epi)


def mm_xt_dy(name, x, dy, col_shards, tm=1024, tn=1024):
    m, kdim = x.shape
    n = dy.shape[1]
    tk = _tile(m, 4096)
    if col_shards:
        c = n // NDEV
        tm, tn = _tile(kdim, tm), _tile(c, tn)
        per = c // tn
        out = jax.ShapeDtypeStruct((NDEV, kdim, c), BF16)
        o_spec = pl.BlockSpec((None, tm, tn), lambda i, j, k: (j // per, i, j % per))
    else:
        tm, tn = _tile(kdim, tm), _tile(n, tn)
        out = jax.ShapeDtypeStruct((kdim, n), BF16)
        o_spec = pl.BlockSpec((tm, tn), lambda i, j, k: (i, j))
    grid = (kdim // tm, n // tn, m // tk)
    a_spec = pl.BlockSpec((tk, tm), lambda i, j, k: (k, i))
    b_spec = pl.BlockSpec((tk, tn), lambda i, j, k: (k, j))
    g = _mm(name, "tn", grid, (tm, tn), x, a_spec, dy, b_spec, [], [(out, o_spec)], _id_epi)[0]
    return g if col_shards else g.reshape(NDEV, kdim // NDEV, n)


def _ex_spec(e, kind, tm, tn):
    if kind == "tile":
        return pl.BlockSpec((tm, tn), lambda i, j, k: (i, j))
    if kind == "row":
        return pl.BlockSpec((1, tn), lambda i, j, k: (0, j))
    raise ValueError(kind)


def _rows_call(name, body, n_rows, ts, ins, outs, scratch=(), scratch_bytes=0):
    blk = sum(_nbytes(s.block_shape, a.dtype) for a, s in list(ins) + list(outs))
    return pl.pallas_call(
        body,
        name=name,
        grid=(n_rows // ts,),
        in_specs=[s for _, s in ins],
        out_specs=[s for _, s in outs],
        out_shape=[o for o, _ in outs],
        scratch_shapes=list(scratch),
        compiler_params=_params(("arbitrary",), blk, scratch_bytes + 4 * blk // 2),
    )(*[a for a, _ in ins])


def _blk(ts, d):
    return pl.BlockSpec((ts, d), lambda i: (i, 0))


def _full(shape):
    return pl.BlockSpec(shape, lambda i: tuple(0 for _ in shape))


def _rowsum8(v):
    t, d = v.shape
    return jnp.sum(v.reshape(t // SUBLANES, SUBLANES, d), axis=0)


def _accumulate(ref, val):
    @pl.when(pl.program_id(0) == 0)
    def _():
        ref[...] = val

    @pl.when(pl.program_id(0) > 0)
    def _():
        ref[...] += val


def _rstd(x):
    return lax.rsqrt(jnp.mean(x * x, axis=-1, keepdims=True) + EPS)


def _rms_bwd_math(dy, x, g):
    r = _rstd(x)
    gdy = dy * g
    c = jnp.mean(gdy * x, axis=-1, keepdims=True)
    dx = r * gdy - x * (r * r * r * c)
    return dx, dy * (x * r)


def rms_fwd(name, h, g, ts=512):
    s, d = h.shape
    ts = min(ts, s)

    def body(h_ref, g_ref, u_ref):
        x = h_ref[...]
        u_ref[...] = ((x * _rstd(x)) * g_ref[...]).astype(BF16)

    return _rows_call(name, body, s, ts, [(h, _blk(ts, d)), (g, _full((1, d)))],
                      [(jax.ShapeDtypeStruct((s, d), BF16), _blk(ts, d))])[0]


def rms_bwd(name, du, h, g, dres, ts=256):
    s, d = h.shape
    ts = min(ts, s)

    def body(du_ref, h_ref, g_ref, dres_ref, dh_ref, dhb_ref, dg_ref, cs_ref):
        dx, dg = _rms_bwd_math(du_ref[...], h_ref[...], g_ref[...])
        dh = dres_ref[...] + dx
        dh_ref[...] = dh
        dhb_ref[...] = dh.astype(BF16)
        _accumulate(dg_ref, _rowsum8(dg))
        _accumulate(cs_ref, _rowsum8(dh))

    return _rows_call(
        name, body, s, ts,
        [(du, _blk(ts, d)), (h, _blk(ts, d)), (g, _full((1, d))), (dres, _blk(ts, d))],
        [(jax.ShapeDtypeStruct((s, d), F32), _blk(ts, d)), (jax.ShapeDtypeStruct((s, d), BF16), _blk(ts, d)),
         (jax.ShapeDtypeStruct((SUBLANES, d), F32), _full((SUBLANES, d))),
         (jax.ShapeDtypeStruct((SUBLANES, d), F32), _full((SUBLANES, d)))])


def rms_bwd_ple(name, du, h, g, dres, gate, e, ts=256):
    s, d = h.shape
    ts = min(ts, s)

    def body(du_ref, h_ref, g_ref, dres_ref, gate_ref, e_ref, dh_ref, de_ref, dgl_ref, dg_ref):
        dx, dg = _rms_bwd_math(du_ref[...], h_ref[...], g_ref[...])
        dh = dres_ref[...] + dx
        dh_ref[...] = dh
        gate_v = gate_ref[...].astype(F32)
        de_ref[...] = (dh * gate_v).astype(BF16)
        dgl_ref[...] = (dh * e_ref[...] * (gate_v * (1.0 - gate_v))).astype(BF16)
        _accumulate(dg_ref, _rowsum8(dg))

    return _rows_call(
        name, body, s, ts,
        [(du, _blk(ts, d)), (h, _blk(ts, d)), (g, _full((1, d))), (dres, _blk(ts, d)), (gate, _blk(ts, d)),
         (e, _blk(ts, d))],
        [(jax.ShapeDtypeStruct((s, d), F32), _blk(ts, d)), (jax.ShapeDtypeStruct((s, d), BF16), _blk(ts, d)),
         (jax.ShapeDtypeStruct((s, d), BF16), _blk(ts, d)),
         (jax.ShapeDtypeStruct((SUBLANES, d), F32), _full((SUBLANES, d)))])


def final_loss(name, h, g, target, gate, e, ts=256):
    s, d = h.shape
    ts = min(ts, s)

    def body(h_ref, g_ref, t_ref, gate_ref, e_ref, loss_ref, dh_ref, dg_ref, de_ref, dgl_ref):
        x = h_ref[...]
        gf = g_ref[...]
        y = (x * _rstd(x)) * gf
        err = y - t_ref[...]
        _accumulate(loss_ref, _rowsum8(err * err) * (0.5 / d))
        dx, dg = _rms_bwd_math(err * (1.0 / d), x, gf)
        dh_ref[...] = dx
        _accumulate(dg_ref, _rowsum8(dg))
        gate_v = gate_ref[...].astype(F32)
        de_ref[...] = (dx * gate_v).astype(BF16)
        dgl_ref[...] = (dx * e_ref[...] * (gate_v * (1.0 - gate_v))).astype(BF16)

    return _rows_call(
        name, body, s, ts,
        [(h, _blk(ts, d)), (g, _full((1, d))), (target, _blk(ts, d)), (gate, _blk(ts, d)), (e, _blk(ts, d))],
        [(jax.ShapeDtypeStruct((SUBLANES, d), F32), _full((SUBLANES, d))),
         (jax.ShapeDtypeStruct((s, d), F32), _blk(ts, d)),
         (jax.ShapeDtypeStruct((SUBLANES, d), F32), _full((SUBLANES, d))),
         (jax.ShapeDtypeStruct((s, d), BF16), _blk(ts, d)), (jax.ShapeDtypeStruct((s, d), BF16), _blk(ts, d))])


CONV_LANES = 256
CONV_ROWS = 64


def _lane_chunks(d, fn):
    lc = min(CONV_LANES, d)

    def lane_body(c, carry):
        fn(pl.ds(pl.multiple_of(c * lc, lc), lc))
        return carry

    lax.fori_loop(0, d // lc, lane_body, 0)


def _shifted_copies(buf, sh, lanes):
    rows = buf.shape[0] - SUBLANES
    for s in range(1, SUBLANES):
        sh[s, pl.ds(0, rows), :] = buf[pl.ds(s, rows), lanes]


def _window(buf, sh, lanes, start, rows):
    s = start % SUBLANES
    if s == 0:
        return buf[pl.ds(start, rows), lanes]
    return sh[s, pl.ds(start - s, rows), :]


def _prev_halo_spec(ts, halo, width):
    per = ts // halo
    return pl.BlockSpec((halo, width), lambda i: (jnp.maximum(i * per - 1, 0), 0))


def _next_halo_spec(ts, halo, width, n_rows):
    per = ts // halo
    last = n_rows // halo - 1
    return pl.BlockSpec((halo, width), lambda i: (jnp.minimum((i + 1) * per, last), 0))


def cf_fwd_mid(name, a, w_dw, b_dw, gn, ts=256):
    s, d2 = a.shape
    d = d2 // 2
    ts = min(ts, s)
    hl = CONV_A_HALO
    off = hl - (CONV_A_TAPS - 1)

    rc = min(CONV_ROWS, ts)

    def body(a_ref, ah_ref, w_ref, b_ref, gn_ref, v0_ref, v1_ref, v3_ref, buf, sh):
        first = pl.program_id(0) == 0
        halo = ah_ref[...]
        hv0 = halo[:, :d] * jax.nn.sigmoid(halo[:, d:])
        buf[pl.ds(0, hl), :] = jnp.where(first, 0.0, hv0)
        main = a_ref[...]
        v0 = main[:, :d] * jax.nn.sigmoid(main[:, d:])
        buf[pl.ds(hl, ts), :] = v0
        v0_ref[...] = v0

        def conv(lanes):
            _shifted_copies(buf, sh, lanes)
            for r0 in range(0, ts, rc):
                acc = jnp.zeros((rc, lanes.size), F32)
                for k in range(CONV_A_TAPS):
                    acc = acc + w_ref[pl.ds(k, 1), lanes] * _window(buf, sh, lanes, r0 + off + k, rc)
                v1_ref[pl.ds(r0, rc), lanes] = acc + b_ref[:, lanes]

        _lane_chunks(d, conv)
        v1 = v1_ref[...]
        v2 = (v1 * _rstd(v1)) * gn_ref[...]
        v3_ref[...] = (v2 * jax.nn.sigmoid(v2)).astype(BF16)

    return _rows_call(
        name, body, s, ts,
        [(a, _blk(ts, d2)), (a, _prev_halo_spec(ts, hl, d2)), (w_dw, _full(w_dw.shape)),
         (b_dw, _full((1, d))), (gn, _full((1, d)))],
        [(jax.ShapeDtypeStruct((s, d), F32), _blk(ts, d)), (jax.ShapeDtypeStruct((s, d), F32), _blk(ts, d)),
         (jax.ShapeDtypeStruct((s, d), BF16), _blk(ts, d))],
        scratch=[pltpu.VMEM((hl + ts, d), F32), pltpu.VMEM((SUBLANES, hl + ts, min(CONV_LANES, d)), F32)],
        scratch_bytes=_nbytes((hl + ts, d + SUBLANES * CONV_LANES), F32))


def cf_bwd_rows(name, dv3, v1, gn, ts=256):
    s, d = v1.shape
    ts = min(ts, s)

    def body(dv3_ref, v1_ref, gn_ref, dv1_ref, dgn_ref, db_ref):
        v1 = v1_ref[...]
        gn_v = gn_ref[...]
        v2 = (v1 * _rstd(v1)) * gn_v
        sg = jax.nn.sigmoid(v2)
        dv2 = dv3_ref[...] * (sg * (1.0 + v2 * (1.0 - sg)))
        dv1, dgn = _rms_bwd_math(dv2, v1, gn_v)
        dv1_ref[...] = dv1
        _accumulate(dgn_ref, _rowsum8(dgn))
        _accumulate(db_ref, _rowsum8(dv1))

    return _rows_call(
        name, body, s, ts, [(dv3, _blk(ts, d)), (v1, _blk(ts, d)), (gn, _full((1, d)))],
        [(jax.ShapeDtypeStruct((s, d), F32), _blk(ts, d)),
         (jax.ShapeDtypeStruct((SUBLANES, d), F32), _full((SUBLANES, d))),
         (jax.ShapeDtypeStruct((SUBLANES, d), F32), _full((SUBLANES, d)))])


def cf_bwd_conv(name, dv1, v0, a, w_dw, ts=256):
    s, d = dv1.shape
    ts = min(ts, s)
    hl = CONV_A_HALO
    taps = CONV_A_TAPS
    off = hl - (taps - 1)
    last_blk = s // ts - 1

    rc = min(CONV_ROWS, ts)

    def body(dv1_ref, dv1n_ref, v0_ref, v0p_ref, a_ref, w_ref, da_ref, dw_ref, db_ref,
             dbuf, vbuf, dv0_buf, dw_acc, dsh, vsh):
        i = pl.program_id(0)
        dbuf[pl.ds(0, ts), :] = dv1_ref[...]
        dbuf[pl.ds(ts, hl), :] = jnp.where(i == last_blk, 0.0, dv1n_ref[...])
        vbuf[pl.ds(0, hl), :] = jnp.where(i == 0, 0.0, v0p_ref[...])
        vbuf[pl.ds(hl, ts), :] = v0_ref[...]

        @pl.when(i == 0)
        def _():
            dw_acc[...] = jnp.zeros_like(dw_acc)

        def conv_t(lanes):
            _shifted_copies(dbuf, dsh, lanes)
            _shifted_copies(vbuf, vsh, lanes)
            for r0 in range(0, ts, rc):
                g = dbuf[pl.ds(r0, rc), lanes]
                acc = jnp.zeros((rc, lanes.size), F32)
                for k in range(taps):
                    acc = acc + w_ref[pl.ds(k, 1), lanes] * _window(dbuf, dsh, lanes, r0 + taps - 1 - k, rc)
                    prod = g * _window(vbuf, vsh, lanes, r0 + off + k, rc)
                    dw_acc[pl.ds(k * SUBLANES, SUBLANES), lanes] += _rowsum8(prod)
                dv0_buf[pl.ds(r0, rc), lanes] = acc

        _lane_chunks(d, conv_t)
        dv0 = dv0_buf[...]
        av = a_ref[...]
        val, sg = av[:, :d], jax.nn.sigmoid(av[:, d:])
        dval = dv0 * sg
        dgate = dv0 * val * (sg * (1.0 - sg))
        da_ref[:, :d] = dval.astype(BF16)
        da_ref[:, d:] = dgate.astype(BF16)
        _accumulate(db_ref.at[:, pl.ds(0, d)], _rowsum8(dval))
        _accumulate(db_ref.at[:, pl.ds(d, d)], _rowsum8(dgate))

        @pl.when(i == last_blk)
        def _():
            dw_ref[...] = jnp.sum(dw_acc[...].reshape(hl, SUBLANES, d), axis=1)

    lc = min(CONV_LANES, d)
    scratch = [pltpu.VMEM((ts + hl, d), F32), pltpu.VMEM((hl + ts, d), F32), pltpu.VMEM((ts, d), F32),
               pltpu.VMEM((hl * SUBLANES, d), F32), pltpu.VMEM((SUBLANES, ts + hl, lc), F32),
               pltpu.VMEM((SUBLANES, hl + ts, lc), F32)]
    sbytes = _nbytes((3 * ts + 2 * hl + hl * SUBLANES, d), F32) + 2 * _nbytes((SUBLANES, ts + hl, lc), F32)
    return _rows_call(
        name, body, s, ts,
        [(dv1, _blk(ts, d)), (dv1, _next_halo_spec(ts, hl, d, s)), (v0, _blk(ts, d)), (v0, _prev_halo_spec(ts, hl, d)),
         (a, _blk(ts, 2 * d)), (w_dw, _full(w_dw.shape))],
        [(jax.ShapeDtypeStruct((s, 2 * d), BF16), _blk(ts, 2 * d)),
         (jax.ShapeDtypeStruct((hl, d), F32), _full((hl, d))),
         (jax.ShapeDtypeStruct((SUBLANES, 2 * d), F32), _full((SUBLANES, 2 * d)))],
        scratch=scratch, scratch_bytes=sbytes)


def sc_fwd_mid(name, bcv, w_conv, ts=256):
    s, d3 = bcv.shape
    d = d3 // 3
    ts = min(ts, s)
    hl = CONV_B_HALO
    off = hl - (CONV_B_TAPS - 1)

    def body(x_ref, xp_ref, w_ref, y_ref, buf):
        hp = xp_ref[...]
        buf[pl.ds(0, hl), :] = jnp.where(pl.program_id(0) == 0, 0.0, hp[:, d:2 * d] * hp[:, 2 * d:])
        buf[pl.ds(hl, ts), :] = x_ref[:, d:2 * d] * x_ref[:, 2 * d:]
        cc = jnp.zeros((ts, d), F32)
        for k in range(CONV_B_TAPS):
            cc = cc + w_ref[pl.ds(k, 1), :] * buf[pl.ds(off + k, ts), :]
        y_ref[...] = (x_ref[:, :d] * cc).astype(BF16)

    return _rows_call(
        name, body, s, ts,
        [(bcv, _blk(ts, d3)), (bcv, _prev_halo_spec(ts, hl, d3)), (w_conv, _full(w_conv.shape))],
        [(jax.ShapeDtypeStruct((s, d), BF16), _blk(ts, d))],
        scratch=[pltpu.VMEM((hl + ts, d), F32)], scratch_bytes=_nbytes((hl + ts, d), F32))[0]


def sc_bwd_mid(name, dy, bcv, w_conv, ts=256):
    s, d3 = bcv.shape
    d = d3 // 3
    ts = min(ts, s)
    hl = CONV_B_HALO
    taps = CONV_B_TAPS
    off = hl - (taps - 1)
    last_blk = s // ts - 1

    def body(dy_ref, dyn_ref, x_ref, xp_ref, xn_ref, w_ref, dx_ref, dw_ref, cvbuf, dbuf, dw_acc):
        i = pl.program_id(0)
        hp = xp_ref[...]
        cvbuf[pl.ds(0, hl), :] = jnp.where(i == 0, 0.0, hp[:, d:2 * d] * hp[:, 2 * d:])
        gb, gc, v = x_ref[:, :d], x_ref[:, d:2 * d], x_ref[:, 2 * d:]
        cvbuf[pl.ds(hl, ts), :] = gc * v
        dy_v = dy_ref[...]
        dcc = dy_v * gb
        dbuf[pl.ds(0, ts), :] = dcc
        dbuf[pl.ds(ts, hl), :] = jnp.where(i == last_blk, 0.0, dyn_ref[...] * xn_ref[:, :d])

        @pl.when(i == 0)
        def _():
            dw_acc[...] = jnp.zeros_like(dw_acc)

        cc = jnp.zeros((ts, d), F32)
        dcv = jnp.zeros((ts, d), F32)
        for k in range(taps):
            win = cvbuf[pl.ds(off + k, ts), :]
            cc = cc + w_ref[pl.ds(k, 1), :] * win
            dcv = dcv + w_ref[pl.ds(k, 1), :] * dbuf[pl.ds(taps - 1 - k, ts), :]
            dw_acc[pl.ds(k * SUBLANES, SUBLANES), :] += _rowsum8(dcc * win)
        dx_ref[:, :d] = (dy_v * cc).astype(BF16)
        dx_ref[:, d:2 * d] = (dcv * v).astype(BF16)
        dx_ref[:, 2 * d:] = (dcv * gc).astype(BF16)

        @pl.when(i == last_blk)
        def _():
            dw_ref[...] = jnp.sum(dw_acc[...].reshape(hl, SUBLANES, d), axis=1)

    scratch = [pltpu.VMEM((hl + ts, d), F32), pltpu.VMEM((ts + hl, d), F32), pltpu.VMEM((hl * SUBLANES, d), F32)]
    sbytes = _nbytes((2 * ts + 2 * hl + hl * SUBLANES, d), F32)
    return _rows_call(
        name, body, s, ts,
        [(dy, _blk(ts, d)), (dy, _next_halo_spec(ts, hl, d, s)), (bcv, _blk(ts, d3)), (bcv, _prev_halo_spec(ts, hl, d3)),
         (bcv, _next_halo_spec(ts, hl, d3, s)), (w_conv, _full(w_conv.shape))],
        [(jax.ShapeDtypeStruct((s, d3), BF16), _blk(ts, d3)), (jax.ShapeDtypeStruct((hl, d), F32), _full((hl, d)))],
        scratch=scratch, scratch_bytes=sbytes)


def _row(a, i):
    return lax.slice_in_dim(a, i, i + 1, axis=0)


def _local_step(x, p, target, small, get_w, conv_w, put_grads):
    depth = p.shape[0]
    acts = []
    h = x
    for i in range(depth):
        j = i // 2
        act = {"h": h}
        u = rms_fwd(f"rms_mix_{i}", h, _row(small["norm_mix"], i))
        act["u"] = u
        if i % 2 == 0:
            a = mm_x_wcol(f"cf_pw1_{i}", u, get_w("cf_w_pw1", j, u), 0, extras=[(_row(small["cf_b_pw1"], j), "row")],
                          epi=lambda acc, b: (acc + b,))[0]
            v0, v1, v3 = cf_fwd_mid(f"cf_mid_{i}", a, conv_w["cf"][j], _row(small["cf_b_dw"], j), _row(small["cf_norm"], j))
            act.update(a=a, v0=v0, v1=v1, v3=v3)
            h1 = mm_x_wrow(f"cf_pw2_{i}", v3, get_w("cf_w_pw2", j, v3),
                           extras=[(_row(small["cf_b_pw2"], j), "row"), (h, "tile")],
                           epi=lambda acc, b, res: (res + (acc + b),), tn=1024)[0]
        else:
            bcv = mm_x_wcol(f"sc_in_{i}", u, get_w("sc_w_in", j, u), 0, tn=768)[0]
            y = sc_fwd_mid(f"sc_mid_{i}", bcv, conv_w["sc"][j])
            act.update(bcv=bcv, y=y)
            h1 = mm_x_wrow(f"sc_out_{i}", y, get_w("sc_w_out", j, y), extras=[(h, "tile")],
                           epi=lambda acc, res: (res + acc,), tn=1024)[0]
        act["h1"] = h1
        u2 = rms_fwd(f"rms_mlp_{i}", h1, _row(small["norm_mlp"], i))
        z, hd = mm_x_wcol(f"mlp_w1_{i}", u2, get_w("mlp_w1", i, u2), 0, outs_dtypes=(BF16, BF16),
                          epi=lambda acc: (acc, jnp.square(jnp.maximum(acc, 0.0))))
        h2 = mm_x_wrow(f"mlp_w2_{i}", hd, get_w("mlp_w2", i, hd), extras=[(h1, "tile")],
                       epi=lambda acc, res: (res + acc,), tm=1024, tn=256, tk=8192)[0]
        act.update(u2=u2, z=z, hd=hd, h2=h2)
        n3 = rms_fwd(f"rms_ple_{i}", h2, _row(small["norm_ple"], i))
        e = mm_x_wcol(f"ple_proj_{i}", p[i], get_w("ple_w_proj", i, n3), 0)[0]

        def ple_epi(acc, e_t, res):
            g_t = jax.nn.sigmoid(acc)
            return g_t, res + g_t * e_t

        g, h3 = mm_x_wrow(f"ple_gate_{i}", n3, get_w("ple_w_gate", i, e), extras=[(e, "tile"), (h2, "tile")],
                          outs_dtypes=(BF16, F32), epi=ple_epi)
        act.update(n3=n3, e=e, g=g)
        acts.append(act)
        h = h3

    loss_part, dh, dg_final, de, dgl = final_loss("final_loss", h, small["norm_final"], target,
                                                  acts[-1]["g"], acts[-1]["e"])
    sg = {k: [None] * small[k].shape[0] for k in small if k != "norm_final"}
    sg["norm_final"] = [dg_final]
    sg["cf_w_dw"] = [None] * conv_w["cf"].shape[0]
    sg["sc_w_conv"] = [None] * conv_w["sc"].shape[0]

    for i in reversed(range(depth)):
        j = i // 2
        act = acts[i]
        g_proj = mm_xt_dy(f"d_ple_proj_{i}", p[i], de, True)
        g_gate = mm_xt_dy(f"d_ple_gate_{i}", act["n3"], dgl, False)
        dn3 = mm_dy_wrow_t(f"dn3_{i}", dgl, get_w("ple_w_gate", i))[0]
        dh2, dh2b, sg["norm_ple"][i], _ = rms_bwd(f"rms_ple_bwd_{i}", dn3, act["h2"], _row(small["norm_ple"], i), dh)
        g_w2 = mm_xt_dy(f"d_mlp_w2_{i}", act["hd"], dh2b, False)
        dz = mm_dy_wrow_t(f"dz_{i}", dh2b, get_w("mlp_w2", i), extras=[(act["z"], "tile")], outs_dtypes=(BF16,),
                          epi=lambda acc, z_t: (acc * (2.0 * jnp.maximum(z_t.astype(F32), 0.0)),))[0]
        g_w1 = mm_xt_dy(f"d_mlp_w1_{i}", act["u2"], dz, True)
        put_grads({("ple_w_proj", i): g_proj, ("ple_w_gate", i): g_gate, ("mlp_w2", i): g_w2, ("mlp_w1", i): g_w1})
        du2 = mm_dy_wcol_t(f"du2_{i}", dz, get_w("mlp_w1", i), 0)[0]
        dh1, dh1b, sg["norm_mlp"][i], cs1 = rms_bwd(f"rms_mlp_bwd_{i}", du2, act["h1"], _row(small["norm_mlp"], i), dh2)
        if i % 2 == 0:
            g_out = mm_xt_dy(f"d_cf_pw2_{i}", act["v3"], dh1b, False)
            sg["cf_b_pw2"][j] = cs1
            dv3 = mm_dy_wrow_t(f"dv3_{i}", dh1b, get_w("cf_w_pw2", j))[0]
            dv1, sg["cf_norm"][j], sg["cf_b_dw"][j] = cf_bwd_rows(f"cf_bwd_rows_{i}", dv3, act["v1"], _row(small["cf_norm"], j))
            da, sg["cf_w_dw"][j], sg["cf_b_pw1"][j] = cf_bwd_conv(f"cf_bwd_conv_{i}", dv1, act["v0"], act["a"], conv_w["cf"][j])
            g_in = mm_xt_dy(f"d_cf_pw1_{i}", act["u"], da, True)
            put_grads({("cf_w_pw2", j): g_out, ("cf_w_pw1", j): g_in})
            du = mm_dy_wcol_t(f"du_{i}", da, get_w("cf_w_pw1", j), 0)[0]
        else:
            g_out = mm_xt_dy(f"d_sc_out_{i}", act["y"], dh1b, False)
            dy = mm_dy_wrow_t(f"dy_{i}", dh1b, get_w("sc_w_out", j))[0]
            dbcv, sg["sc_w_conv"][j] = sc_bwd_mid(f"sc_bwd_mid_{i}", dy, act["bcv"], conv_w["sc"][j])
            g_in = mm_xt_dy(f"d_sc_in_{i}", act["u"], dbcv, True, tn=768)
            put_grads({("sc_w_out", j): g_out, ("sc_w_in", j): g_in})
            du = mm_dy_wcol_t(f"du_{i}", dbcv, get_w("sc_w_in", j), 0)[0]
        if i > 0:
            below = acts[i - 1]
            dh, de, dgl, sg["norm_mix"][i] = rms_bwd_ple(f"rms_mix_bwd_{i}", du, act["h"], _row(small["norm_mix"], i), dh1,
                                                         below["g"], below["e"])
        else:
            dh, _, sg["norm_mix"][i], _ = rms_bwd(f"rms_mix_bwd_{i}", du, act["h"], _row(small["norm_mix"], i), dh1)
    return loss_part, dh, sg


def _me_and_peers():
    x, y, c = lax.axis_index("x"), lax.axis_index("y"), lax.axis_index("c")
    me = 4 * x + 2 * y + c
    peers = []
    for q in range(1, NDEV):
        px = 1 - x if q & 4 else x
        py = 1 - y if q & 2 else y
        pc = 1 - c if q & 1 else c
        peers.append(((px, py, pc), 4 * px + 2 * py + pc))
    return me, peers


def _exchange(name, srcs, out_shapes, src_fns, dst_fns, after=()):
    n = len(srcs)
    n_after = len(after)

    def body(*refs):
        ins, outs = refs[:n], refs[n + n_after:2 * n + n_after]
        send_sems, recv_sems, local_sems = refs[2 * n + n_after:]
        me, peers = _me_and_peers()
        local, remote = [], []
        for k in range(n):
            cp = pltpu.make_async_copy(src_fns[k](ins[k], me), dst_fns[k](outs[k], me), local_sems.at[k])
            cp.start()
            local.append(cp)
        for q, (peer, peer_blk) in enumerate(peers):
            for k in range(n):
                cp = pltpu.make_async_remote_copy(
                    src_ref=src_fns[k](ins[k], peer_blk), dst_ref=dst_fns[k](outs[k], me),
                    send_sem=send_sems.at[k, q], recv_sem=recv_sems.at[k, q],
                    device_id=peer, device_id_type=MESH)
                cp.start()
                remote.append(cp)
        for q, (peer, peer_blk) in enumerate(peers):
            for k in range(n):
                pltpu.make_async_remote_copy(
                    src_ref=src_fns[k](ins[k], peer_blk), dst_ref=dst_fns[k](outs[k], peer_blk),
                    send_sem=send_sems.at[k, q], recv_sem=recv_sems.at[k, q],
                    device_id=peer, device_id_type=MESH).wait_recv()
        for cp in remote:
            cp.wait_send()
        for cp in local:
            cp.wait()

    any_spec = pl.BlockSpec(memory_space=pl.ANY)
    return pl.pallas_call(
        body,
        name=name,
        in_specs=[any_spec] * (n + n_after),
        out_specs=[any_spec] * n,
        out_shape=out_shapes,
        scratch_shapes=[pltpu.SemaphoreType.DMA((n, N_PEERS)), pltpu.SemaphoreType.DMA((n, N_PEERS)),
                        pltpu.SemaphoreType.DMA((n,))],
    )(*srcs, *after)


def sc_exchange(name, collective_id, srcs, out_shapes, src_fns, dst_fns):
    n = len(srcs)

    def body(*refs):
        ins, outs = refs[:n], refs[n:2 * n]
        send_sems, recv_sems, local_sems = refs[2 * n:]
        me, peers = _me_and_peers()
        barrier = pltpu.get_barrier_semaphore()
        for peer, _ in peers:
            pl.semaphore_signal(barrier, inc=1, device_id=peer, device_id_type=MESH)
        pl.semaphore_wait(barrier, N_PEERS)
        local, remote = [], []
        for k in range(n):
            cp = pltpu.make_async_copy(src_fns[k](ins[k], me), dst_fns[k](outs[k], me), local_sems.at[k])
            cp.start()
            local.append(cp)
        for q, (peer, peer_blk) in enumerate(peers):
            for k in range(n):
                cp = pltpu.make_async_remote_copy(
                    src_ref=src_fns[k](ins[k], peer_blk), dst_ref=dst_fns[k](outs[k], me),
                    send_sem=send_sems.at[k, q], recv_sem=recv_sems.at[k, q],
                    device_id=peer, device_id_type=MESH)
                cp.start()
                remote.append(cp)
        for q, (peer, peer_blk) in enumerate(peers):
            for k in range(n):
                pltpu.make_async_remote_copy(
                    src_ref=src_fns[k](ins[k], peer_blk), dst_ref=dst_fns[k](outs[k], peer_blk),
                    send_sem=send_sems.at[k, q], recv_sem=recv_sems.at[k, q],
                    device_id=peer, device_id_type=MESH).wait_recv()
        for cp in remote:
            cp.wait_send()
        for cp in local:
            cp.wait()

    return pl.kernel(
        body,
        out_type=out_shapes,
        mesh=plsc.ScalarSubcoreMesh(axis_name="sequencer", num_cores=1),
        name=name,
        scratch_types=[pltpu.SemaphoreType.DMA((n, N_PEERS)), pltpu.SemaphoreType.DMA((n, N_PEERS)),
                       pltpu.SemaphoreType.DMA((n,))],
        compiler_params=pltpu.CompilerParams(collective_id=collective_id),
    )(*srcs)


def sc_gather(name, collective_id, srcs, layers):
    n = len(srcs)
    outs_shape = [jax.ShapeDtypeStruct((NDEV, 1) + a.shape[1:], a.dtype) for a in srcs]

    def body(*refs):
        ins, outs = refs[:n], refs[n:2 * n]
        send_sems, recv_sems, local_sems = refs[2 * n:]
        x, y, c = lax.axis_index("x"), lax.axis_index("y"), lax.axis_index("c")
        me = 4 * x + 2 * y + c
        sibling = (x, y, 1 - c)
        chips = [(1 - x, y), (x, 1 - y), (1 - x, 1 - y)]
        barrier = pltpu.get_barrier_semaphore()
        for peer in [sibling] + [(cx, cy, c) for cx, cy in chips]:
            pl.semaphore_signal(barrier, inc=1, device_id=peer, device_id_type=MESH)
        pl.semaphore_wait(barrier, 1 + len(chips))

        def copy(k, slot, blk, to, src=None):
            place = outs[k].at[blk, 0]
            return pltpu.make_async_remote_copy(
                src_ref=place if src is None else src, dst_ref=place,
                send_sem=send_sems.at[k, slot], recv_sem=recv_sems.at[k, slot],
                device_id=to, device_id_type=MESH)

        local, sent = [], []
        for k in range(n):
            mine = ins[k].at[layers[k]]
            cp = pltpu.make_async_copy(mine, outs[k].at[me, 0], local_sems.at[k])
            cp.start()
            local.append(cp)
            sent.append(copy(k, 0, me, sibling, src=mine))
            sent += [copy(k, 1 + j, me, (cx, cy, c), src=mine) for j, (cx, cy) in enumerate(chips)]
        for cp in sent:
            cp.start()
        for k in range(n):
            for j, (cx, cy) in enumerate(chips):
                blk = 4 * cx + 2 * cy + c
                copy(k, 1 + j, blk, sibling).wait_recv()
                fwd = copy(k, 4 + j, blk, sibling)
                fwd.start()
                sent.append(fwd)
        for k in range(n):
            copy(k, 0, 4 * x + 2 * y + (1 - c), sibling).wait_recv()
            for j, (cx, cy) in enumerate(chips):
                copy(k, 4 + j, 4 * cx + 2 * cy + (1 - c), sibling).wait_recv()
        for cp in sent:
            cp.wait_send()
        for cp in local:
            cp.wait()

    return pl.kernel(
        body,
        out_type=outs_shape,
        mesh=plsc.ScalarSubcoreMesh(axis_name="sequencer", num_cores=1),
        name=name,
        scratch_types=[pltpu.SemaphoreType.DMA((n, N_PEERS)), pltpu.SemaphoreType.DMA((n, N_PEERS)),
                       pltpu.SemaphoreType.DMA((n,))],
        compiler_params=pltpu.CompilerParams(collective_id=collective_id),
    )(*srcs)


def _gather_src(layer):
    return lambda ref, blk: ref.at[layer]


def _gather_dst(ref, blk):
    return ref.at[blk, 0]


def _slice_of(ref, blk):
    return ref.at[blk]


def cast_bf16(name, w, tr_elems=512 * 1024):
    l, r, c = w.shape
    tr = _row_tile(r, tr_elems // c)
    spec = pl.BlockSpec((None, tr, c), lambda li, i: (li, i, 0))

    def body(w_ref, o_ref):
        o_ref[...] = w_ref[...].astype(BF16)

    return pl.pallas_call(
        body, name=name, grid=(l, r // tr), in_specs=[spec], out_specs=spec,
        out_shape=jax.ShapeDtypeStruct(w.shape, BF16),
        compiler_params=_params(("parallel", "parallel"), 6 * tr * c),
    )(w)


def _adamw_math(w, g, m, v):
    m = ADAM_B1 * m + (1.0 - ADAM_B1) * g
    v = ADAM_B2 * v + (1.0 - ADAM_B2) * (g * g)
    m_hat = m * (1.0 / (1.0 - ADAM_B1 ** ADAM_STEP))
    v_hat = v * (1.0 / (1.0 - ADAM_B2 ** ADAM_STEP))
    delta = -ADAM_LR * (m_hat / (jnp.sqrt(v_hat) + ADAM_EPS) + ADAM_WD * w)
    return delta, m, v


def _sum_blocks(ref):
    g = ref[0].astype(F32)
    for d in range(1, ref.shape[0]):
        g = g + ref[d].astype(F32)
    return g


def adamw_layer(name, recv, w, m, v, layer, stacked, after=None, tr_elems=256 * 1024):
    nd, r, c = recv.shape
    tr = _row_tile(r, tr_elems // c)
    r_spec = pl.BlockSpec((nd, tr, c), lambda i: (0, i, 0))
    w_spec = pl.BlockSpec((None, tr, c), lambda i: (layer, i, 0))
    if stacked is None:
        stacked = [lax.empty(w.shape, F32) for _ in range(4)]
    after = [] if after is None else [after]

    def body(r_ref, w_ref, m_ref, v_ref, g_in, d_in, m_in, v_in, *rest):
        g_out, d_out, m_out, v_out = rest[len(after):]
        g = _sum_blocks(r_ref)
        delta, m_new, v_new = _adamw_math(w_ref[...], g, m_ref[...], v_ref[...])
        g_out[...] = g
        d_out[...] = delta
        m_out[...] = m_new
        v_out[...] = v_new

    out = jax.ShapeDtypeStruct(w.shape, F32)
    return pl.pallas_call(
        body, name=name, grid=(r // tr,),
        in_specs=[r_spec, w_spec, w_spec, w_spec] + [pl.BlockSpec(memory_space=pl.ANY)] * (4 + len(after)),
        out_specs=[w_spec] * 4, out_shape=[out] * 4,
        input_output_aliases={4: 0, 5: 1, 6: 2, 7: 3},
        compiler_params=_params(("parallel",), tr * c * (2 * nd + 7 * 4)),
    )(recv, w, m, v, *stacked, *after)


def pack_small_grads(name, parts, taps, n_blocks):
    d = parts[0].shape[1]
    n_p, rows = len(parts), [t.shape[0] for t in taps]
    cb = d // n_blocks

    def body(*refs):
        part_refs, tap_refs = refs[:n_p], refs[n_p:n_p + len(taps)]
        sums_out, taps_out = refs[n_p + len(taps):]
        for i, r in enumerate(part_refs):
            sums_out[pl.ds(i, 1), :] = jnp.sum(r[...], axis=0, keepdims=True)
        r0 = 0
        for t_ref, n in zip(tap_refs, rows):
            for b in range(n_blocks):
                taps_out[b, pl.ds(r0, n), :] = t_ref[:, pl.ds(b * cb, cb)]
            r0 += n

    vm = pl.BlockSpec(memory_space=pltpu.VMEM)
    return pl.pallas_call(
        body, name=name, in_specs=[vm] * (n_p + len(taps)), out_specs=[vm] * 2,
        out_shape=[jax.ShapeDtypeStruct((n_p, d), F32), jax.ShapeDtypeStruct((n_blocks, sum(rows), cb), F32)],
    )(*parts, *taps)


def small_update(name, part_g, tap_g, w_a, m_a, v_a, w_b, m_b, v_b):
    nd, na, d = part_g.shape
    nb, cb = w_b.shape

    def body(pg_ref, tg_ref, wa_ref, ma_ref, va_ref, wb_ref, mb_ref, vb_ref,
             ga_out, da_out, ma_out, va_out, gb_out, db_out, mb_out, vb_out, loss_out):
        ga = _sum_blocks(pg_ref)
        delta, m_new, v_new = _adamw_math(wa_ref[...], ga, ma_ref[...], va_ref[...])
        ga_out[...] = ga
        da_out[...] = delta
        ma_out[...] = m_new
        va_out[...] = v_new
        loss_out[...] = jnp.broadcast_to(jnp.sum(ga[na - 1:na, :], axis=1, keepdims=True), loss_out.shape)
        gb = _sum_blocks(tg_ref)
        delta, m_new, v_new = _adamw_math(wb_ref[...], gb, mb_ref[...], vb_ref[...])
        gb_out[...] = gb
        db_out[...] = delta
        mb_out[...] = m_new
        vb_out[...] = v_new

    oa, ob = jax.ShapeDtypeStruct((na, d), F32), jax.ShapeDtypeStruct((nb, cb), F32)
    vm = pl.BlockSpec(memory_space=pltpu.VMEM)
    return pl.pallas_call(
        body, name=name, in_specs=[vm] * 8, out_specs=[vm] * 9,
        out_shape=[oa] * 4 + [ob] * 4 + [jax.ShapeDtypeStruct((1, LANES), F32)],
        compiler_params=pltpu.CompilerParams(vmem_limit_bytes=_vmem_limit(_nbytes(part_g.shape, F32))),
    )(part_g, tap_g, w_a, m_a, v_a, w_b, m_b, v_b)


BIG = ("cf_w_pw1", "cf_w_pw2", "sc_w_in", "sc_w_out", "mlp_w1", "mlp_w2", "ple_w_proj", "ple_w_gate")
COL_SHARDED = ("cf_w_pw1", "sc_w_in", "mlp_w1", "ple_w_proj")
WEIGHT_ORDER = ("norm_mix", "norm_mlp", "norm_ple", "cf_w_pw1", "cf_b_pw1", "cf_w_dw", "cf_b_dw", "cf_norm",
                "cf_w_pw2", "cf_b_pw2", "sc_w_in", "sc_w_conv", "sc_w_out", "mlp_w1", "mlp_w2", "ple_w_proj",
                "ple_w_gate", "norm_final")
SMALL_ROWS = (("norm_mix", 4), ("norm_mlp", 4), ("norm_ple", 4), ("cf_b_pw1", 4), ("cf_b_dw", 2), ("cf_norm", 2),
              ("cf_b_pw2", 2), ("norm_final", 1))


def _layer_weights(i):
    mixer = (("cf_w_pw1", i // 2), ("cf_w_pw2", i // 2)) if i % 2 == 0 else (("sc_w_in", i // 2), ("sc_w_out", i // 2))
    return mixer + (("mlp_w1", i), ("mlp_w2", i), ("ple_w_proj", i), ("ple_w_gate", i))


def _pad_rows(a, rows):
    return jnp.pad(a, ((0, 0), (0, rows - a.shape[1]), (0, 0)))


def _pack_taps(cf, sc):
    c = cf.shape[2]
    return jnp.concatenate([_pad_rows(cf, CONV_A_HALO).reshape(-1, c), _pad_rows(sc, CONV_B_HALO).reshape(-1, c)], axis=0)


def _unpack_taps(t, n_cf):
    c = t.shape[1]
    cf = t[:n_cf * CONV_A_HALO].reshape(n_cf, CONV_A_HALO, c)[:, :CONV_A_TAPS]
    sc = t[n_cf * CONV_A_HALO:].reshape(-1, CONV_B_HALO, c)[:, :CONV_B_TAPS]
    return cf, sc


def _pack_small(vals, d):
    return jnp.concatenate([vals[k].reshape(-1, d) for k, _ in SMALL_ROWS] + [jnp.zeros((1, d), F32)], axis=0)


def _unpack_small(a, shapes):
    out, r = {}, 0
    for k, n in SMALL_ROWS:
        out[k] = a[r:r + n].reshape(shapes[k])
        r += n
    return out


def kernel(x, p, norm_mix, norm_mlp, norm_ple, cf_w_pw1, cf_b_pw1, cf_w_dw, cf_b_dw, cf_norm, cf_w_pw2, cf_b_pw2, sc_w_in, sc_w_conv, sc_w_out, mlp_w1, mlp_w2, ple_w_proj, ple_w_gate, norm_final, loss_target, m_norm_mix, m_norm_mlp, m_norm_ple, m_cf_w_pw1, m_cf_b_pw1, m_cf_w_dw, m_cf_b_dw, m_cf_norm, m_cf_w_pw2, m_cf_b_pw2, m_sc_w_in, m_sc_w_conv, m_sc_w_out, m_mlp_w1, m_mlp_w2, m_ple_w_proj, m_ple_w_gate, m_norm_final, v_norm_mix, v_norm_mlp, v_norm_ple, v_cf_w_pw1, v_cf_b_pw1, v_cf_w_dw, v_cf_b_dw, v_cf_norm, v_cf_w_pw2, v_cf_b_pw2, v_sc_w_in, v_sc_w_conv, v_sc_w_out, v_mlp_w1, v_mlp_w2, v_ple_w_proj, v_ple_w_gate, v_norm_final):
    w = dict(norm_mix=norm_mix, norm_mlp=norm_mlp, norm_ple=norm_ple, cf_w_pw1=cf_w_pw1, cf_b_pw1=cf_b_pw1,
             cf_w_dw=cf_w_dw, cf_b_dw=cf_b_dw, cf_norm=cf_norm, cf_w_pw2=cf_w_pw2, cf_b_pw2=cf_b_pw2,
             sc_w_in=sc_w_in, sc_w_conv=sc_w_conv, sc_w_out=sc_w_out, mlp_w1=mlp_w1, mlp_w2=mlp_w2,
             ple_w_proj=ple_w_proj, ple_w_gate=ple_w_gate, norm_final=norm_final)
    m = dict(norm_mix=m_norm_mix, norm_mlp=m_norm_mlp, norm_ple=m_norm_ple, cf_w_pw1=m_cf_w_pw1, cf_b_pw1=m_cf_b_pw1,
             cf_w_dw=m_cf_w_dw, cf_b_dw=m_cf_b_dw, cf_norm=m_cf_norm, cf_w_pw2=m_cf_w_pw2, cf_b_pw2=m_cf_b_pw2,
             sc_w_in=m_sc_w_in, sc_w_conv=m_sc_w_conv, sc_w_out=m_sc_w_out, mlp_w1=m_mlp_w1, mlp_w2=m_mlp_w2,
             ple_w_proj=m_ple_w_proj, ple_w_gate=m_ple_w_gate, norm_final=m_norm_final)
    v = dict(norm_mix=v_norm_mix, norm_mlp=v_norm_mlp, norm_ple=v_norm_ple, cf_w_pw1=v_cf_w_pw1, cf_b_pw1=v_cf_b_pw1,
             cf_w_dw=v_cf_w_dw, cf_b_dw=v_cf_b_dw, cf_norm=v_cf_norm, cf_w_pw2=v_cf_w_pw2, cf_b_pw2=v_cf_b_pw2,
             sc_w_in=v_sc_w_in, sc_w_conv=v_sc_w_conv, sc_w_out=v_sc_w_out, mlp_w1=v_mlp_w1, mlp_w2=v_mlp_w2,
             ple_w_proj=v_ple_w_proj, ple_w_gate=v_ple_w_gate, norm_final=v_norm_final)
    depth, d = norm_mix.shape
    n_cf = cf_w_dw.shape[0]

    taps_w = _pack_taps(cf_w_dw, sc_w_conv)
    shards, gathered = {}, {}
    ids = iter(range(5 * depth))
    for i in range(depth):
        names = _layer_weights(i)
        groups = [names[:1], names[1:2], names[2:3], names[3:]] if i == 0 else [names[:2], names[2:]]
        for n_group, group in enumerate(groups):
            with_taps = i == 0 and n_group == 1
            for k, _ in group:
                if k not in shards:
                    shards[k] = cast_bf16(f"cast_{k}", w[k])
            srcs = [shards[k] for k, _ in group] + ([taps_w[None]] if with_taps else [])
            got = sc_gather(f"gather_{i}_{n_group}", next(ids), srcs, [l for _, l in group] + ([0] if with_taps else []))
            for (k, l), g in zip(group, got):
                gathered[(k, l)] = g if k in COL_SHARDED else g.reshape(-1, g.shape[3])
            if with_taps:
                taps_full = jnp.transpose(got[-1][:, 0], (1, 0, 2)).reshape(taps_w.shape[0], d)
    conv_w = {"cf": taps_full[:n_cf * CONV_A_HALO].reshape(n_cf, CONV_A_HALO, d),
              "sc": taps_full[n_cf * CONV_A_HALO:].reshape(-1, CONV_B_HALO, d)}

    def get_w(k, l, after=None):
        return gathered[(k, l)]

    received, waiting = {}, {}
    last_group = list(_layer_weights(0)[:2])

    def put_grads(grads):
        names = list(grads)
        if names[0][0] in ("cf_w_pw2", "sc_w_out") and set(names) != set(last_group):
            waiting.update(grads)
            return
        grads = {**waiting, **grads}
        waiting.clear()
        names = list(grads)
        got = sc_exchange(f"grad_exchange_{names[-1][0]}_{names[-1][1]}", next(ids), [grads[n] for n in names],
                          [jax.ShapeDtypeStruct(grads[n].shape, BF16) for n in names],
                          [_slice_of] * len(names), [_slice_of] * len(names))
        received.update(zip(names, got))

    small = {k: w[k] for k, _ in SMALL_ROWS}
    small["norm_final"] = norm_final[None]
    loss_part, grad_x, sg = _local_step(x[0], p[:, 0], loss_target[0], small, get_w, conv_w, put_grads)

    out = {k: None for k in BIG}
    previous = None
    for (k, l), recv in received.items():
        if (k, l) not in last_group:
            out[k] = adamw_layer(f"adamw_{k}_{l}", recv, w[k], m[k], v[k], l, out[k], after=previous)
            previous = out[k][1]
    updated_first = [out[k][0] for k in BIG if out[k] is not None and k not in [n for n, _ in last_group]]

    parts = []
    for k, n in SMALL_ROWS:
        for g in sg[k]:
            parts += [g[:, :d], g[:, d:]] if g.shape[1] == 2 * d else [g]
    parts.append(loss_part)
    sums, tap_slices = pack_small_grads("pack_small_grads", parts, sg["cf_w_dw"] + sg["sc_w_conv"], NDEV)
    part_all, tap_mine = _exchange(
        "small_exchange", [sums[None], tap_slices],
        [jax.ShapeDtypeStruct((NDEV, 1) + sums.shape, F32), jax.ShapeDtypeStruct(tap_slices.shape, F32)],
        [_gather_src(0), _slice_of], [_gather_dst, _slice_of], after=updated_first)
    for k, l in last_group:
        out[k] = adamw_layer(f"adamw_{k}_{l}", received[(k, l)], w[k], m[k], v[k], l, out[k], after=part_all)
    sm = small_update("small_update", part_all[:, 0], tap_mine,
                      _pack_small(w, d), _pack_small(m, d), _pack_small(v, d),
                      taps_w, _pack_taps(m["cf_w_dw"], m["sc_w_conv"]), _pack_taps(v["cf_w_dw"], v["sc_w_conv"]))
    shapes = {k: w[k].shape for k, _ in SMALL_ROWS}
    for t in range(4):
        un = _unpack_small(sm[t], shapes)
        cf_t, sc_t = _unpack_taps(sm[4 + t], n_cf)
        for k in un:
            out.setdefault(k, [None] * 4)[t] = un[k]
        out.setdefault("cf_w_dw", [None] * 4)[t] = cf_t
        out.setdefault("sc_w_conv", [None] * 4)[t] = sc_t
    loss = sm[8][0, 0]

    return (loss, grad_x[None], *[out[k][0] for k in WEIGHT_ORDER], *[out[k][1] for k in WEIGHT_ORDER],
            *[out[k][2] for k in WEIGHT_ORDER], *[out[k][3] for k in WEIGHT_ORDER])
```

```python
import jax
import jax.numpy as jnp
from jax import lax
from jax.experimental import pallas as pl
from jax.experimental.pallas import tpu as pltpu
from jax.experimental.pallas import tpu_sc as plsc

F32 = jnp.float32
BF16 = jnp.bfloat16
EPS = 1e-6
NDEV = 8
N_PEERS = NDEV - 1
MESH = pl.DeviceIdType.MESH

ADAM_LR = 0.001
ADAM_B1 = 0.9
ADAM_B2 = 0.999
ADAM_EPS = 1e-08
ADAM_WD = 0.01
ADAM_STEP = 10

V7X_VMEM_BYTES = 64 * 1024 * 1024
VMEM_LIMIT_MAX = 56 * 1024 * 1024
SUBLANES = 8
LANES = 128
CONV_A_TAPS = 31
CONV_A_HALO = 32
CONV_B_TAPS = 3
CONV_B_HALO = 8


def _nbytes(shape, dtype):
    n = 1
    for s in shape:
        if s is not None:
            n *= s
    return n * jnp.dtype(dtype).itemsize


def _vmem_limit(block_bytes, scratch_bytes=0):
    need = 2 * block_bytes + scratch_bytes
    return int(min(VMEM_LIMIT_MAX, max(32 * 1024 * 1024, need + need // 2 + (4 << 20))))


def _params(sem, block_bytes, scratch_bytes=0):
    return pltpu.CompilerParams(dimension_semantics=sem, vmem_limit_bytes=_vmem_limit(block_bytes, scratch_bytes))


_DIMS = {
    "nn": (((1,), (0,)), ((), ())),
    "nt": (((1,), (1,)), ((), ())),
    "tn": (((0,), (0,)), ((), ())),
}


def _mm(name, dims, grid, acc_shape, a, a_spec, b, b_spec, extras, outs, epi):
    ni, nj, nk = grid
    n_ex, n_out = len(extras), len(outs)
    dn = _DIMS[dims]
    b_sub = [s for s in b_spec.block_shape if s is not None]
    n_sub = b_sub[0] if len(b_sub) == 3 else 1

    def body(*refs):
        a_ref, b_ref = refs[0], refs[1]
        ex_refs = refs[2:2 + n_ex]
        out_refs = refs[2 + n_ex:2 + n_ex + n_out]
        if n_sub == 1:
            d = lax.dot_general(a_ref[...].astype(BF16), b_ref[...].astype(BF16), dn, preferred_element_type=F32)
        else:
            w_sub = a_ref.shape[1] // n_sub
            d = None
            for s in range(n_sub):
                part = lax.dot_general(a_ref[:, pl.ds(s * w_sub, w_sub)].astype(BF16), b_ref[s].astype(BF16), dn,
                                       preferred_element_type=F32)
                d = part if d is None else d + part

        def finish(acc):
            res = epi(acc, *[r[...] for r in ex_refs])
            for o_ref, r in zip(out_refs, res):
                o_ref[...] = r.astype(o_ref.dtype)

        if nk == 1:
            finish(d)
        else:
            acc_ref = refs[2 + n_ex + n_out]
            k = pl.program_id(2)

            @pl.when(k == 0)
            def _():
                acc_ref[...] = d

            @pl.when(jnp.logical_and(k > 0, k < nk - 1))
            def _():
                acc_ref[...] += d

            @pl.when(k == nk - 1)
            def _():
                finish(acc_ref[...] + d)

    blk = _nbytes(a_spec.block_shape, a.dtype) + _nbytes(b_spec.block_shape, b.dtype)
    for arr, spec in list(extras) + list(outs):
        blk += _nbytes(spec.block_shape, arr.dtype)
    acc_bytes = _nbytes(acc_shape, F32)
    scratch = [pltpu.VMEM(acc_shape, F32)] if nk > 1 else []
    return pl.pallas_call(
        body,
        name=name,
        grid=grid,
        in_specs=[a_spec, b_spec] + [s for _, s in extras],
        out_specs=[s for _, s in outs],
        out_shape=[o for o, _ in outs],
        scratch_shapes=scratch,
        compiler_params=_params(("parallel", "parallel", "arbitrary"), blk, 3 * acc_bytes),
    )(a, b, *[e for e, _ in extras])


def _tile(n, pref):
    if n <= pref:
        return n
    t = pref - pref % LANES
    while t > LANES and n % t:
        t -= LANES
    assert n % t == 0, (n, pref)
    return t


def _row_tile(n, pref):
    if n <= pref:
        return n
    t = max(SUBLANES, pref - pref % SUBLANES)
    while t > SUBLANES and n % t:
        t -= SUBLANES
    assert n % t == 0, (n, pref)
    return t


def _id_epi(acc):
    return (acc,)


def mm_x_wcol(name, x, w, layer, extras=(), outs_dtypes=(F32,), epi=_id_epi, tm=1024, tn=1024):
    m, kdim = x.shape
    c = w.shape[3]
    n = NDEV * c
    tm, tn = _tile(m, tm), _tile(c, tn)
    tk = _tile(kdim, 2048)
    grid = (m // tm, n // tn, kdim // tk)
    per = c // tn
    a_spec = pl.BlockSpec((tm, tk), lambda i, j, k: (i, k))
    b_spec = pl.BlockSpec((None, None, tk, tn), lambda i, j, k: (j // per, layer, k, j % per))
    ex = [(e, _ex_spec(e, kind, tm, tn)) for e, kind in extras]
    o_spec = pl.BlockSpec((tm, tn), lambda i, j, k: (i, j))
    outs = [(jax.ShapeDtypeStruct((m, n), dt), o_spec) for dt in outs_dtypes]
    return _mm(name, "nn", grid, (tm, tn), x, a_spec, w, b_spec, ex, outs, epi)


def mm_x_wrow(name, x, w, extras=(), outs_dtypes=(F32,), epi=_id_epi, tm=1024, tn=512, tk=2048):
    m, kdim = x.shape
    n = w.shape[1]
    assert kdim == w.shape[0]
    tm, tn = _tile(m, tm), _tile(n, tn)
    tk = _tile(kdim, tk)
    grid = (m // tm, n // tn, kdim // tk)
    a_spec = pl.BlockSpec((tm, tk), lambda i, j, k: (i, k))
    b_spec = pl.BlockSpec((tk, tn), lambda i, j, k: (k, j))
    ex = [(e, _ex_spec(e, kind, tm, tn)) for e, kind in extras]
    o_spec = pl.BlockSpec((tm, tn), lambda i, j, k: (i, j))
    outs = [(jax.ShapeDtypeStruct((m, n), dt), o_spec) for dt in outs_dtypes]
    return _mm(name, "nn", grid, (tm, tn), x, a_spec, w, b_spec, ex, outs, epi)


def mm_dy_wcol_t(name, dy, w, layer, extras=(), outs_dtypes=(F32,), epi=_id_epi, tm=1024, b_block_bytes=4 << 20):
    m, n = dy.shape
    nd, kdim, c = w.shape[0], w.shape[2], w.shape[3]
    assert n == nd * c
    tm, tn = _tile(m, tm), _tile(kdim, max(LANES, b_block_bytes // (n * 2)))
    n_sub = nd
    grid = (m // tm, kdim // tn, nd // n_sub)
    a_spec = pl.BlockSpec((tm, n_sub * c), lambda i, j, k: (i, k))
    if n_sub > 1:
        b_spec = pl.BlockSpec((n_sub, None, tn, c), lambda i, j, k: (k, layer, j, 0))
    else:
        b_spec = pl.BlockSpec((None, None, tn, c), lambda i, j, k: (k, layer, j, 0))
    ex = [(e, _ex_spec(e, kind, tm, tn)) for e, kind in extras]
    o_spec = pl.BlockSpec((tm, tn), lambda i, j, k: (i, j))
    outs = [(jax.ShapeDtypeStruct((m, kdim), dt), o_spec) for dt in outs_dtypes]
    return _mm(name, "nt", grid, (tm, tn), dy, a_spec, w, b_spec, ex, outs, epi)


def mm_dy_wrow_t(name, dy, w, extras=(), outs_dtypes=(F32,), epi=_id_epi, tm=1024, tn=1024):
    m, n = dy.shape
    kdim = w.shape[0]
    assert n == w.shape[1]
    tm, tn = _tile(m, tm), _tile(kdim, tn)
    tk = _tile(n, 2048)
    grid = (m // tm, kdim // tn, n // tk)
    a_spec = pl.BlockSpec((tm, tk), lambda i, j, k: (i, k))
    b_spec = pl.BlockSpec((tn, tk), lambda i, j, k: (j, k))
    ex = [(e, _ex_spec(e, kind, tm, tn)) for e, kind in extras]
    o_spec = pl.BlockSpec((tm, tn), lambda i, j, k: (i, j))
    outs = [(jax.ShapeDtypeStruct((m, kdim), dt), o_spec) for dt in outs_dtypes]
    return _mm(name, "nt", grid, (tm, tn), dy, a_spec, w, b_spec, ex, outs, epi)


def mm_xt_dy(name, x, dy, col_shards, tm=1024, tn=1024):
    m, kdim = x.shape
    n = dy.shape[1]
    tk = _tile(m, 4096)
    if col_shards:
        c = n // NDEV
        tm, tn = _tile(kdim, tm), _tile(c, tn)
        per = c // tn
        out = jax.ShapeDtypeStruct((NDEV, kdim, c), BF16)
        o_spec = pl.BlockSpec((None, tm, tn), lambda i, j, k: (j // per, i, j % per))
    else:
        tm, tn = _tile(kdim, tm), _tile(n, tn)
        out = jax.ShapeDtypeStruct((kdim, n), BF16)
        o_spec = pl.BlockSpec((tm, tn), lambda i, j, k: (i, j))
    grid = (kdim // tm, n // tn, m // tk)
    a_spec = pl.BlockSpec((tk, tm), lambda i, j, k: (k, i))
    b_spec = pl.BlockSpec((tk, tn), lambda i, j, k: (k, j))
    g = _mm(name, "tn", grid, (tm, tn), x, a_spec, dy, b_spec, [], [(out, o_spec)], _id_epi)[0]
    return g if col_shards else g.reshape(NDEV, kdim // NDEV, n)


def _ex_spec(e, kind, tm, tn):
    if kind == "tile":
        return pl.BlockSpec((tm, tn), lambda i, j, k: (i, j))
    if kind == "row":
        return pl.BlockSpec((1, tn), lambda i, j, k: (0, j))
    raise ValueError(kind)


def _rows_call(name, body, n_rows, ts, ins, outs, scratch=(), scratch_bytes=0):
    blk = sum(_nbytes(s.block_shape, a.dtype) for a, s in list(ins) + list(outs))
    return pl.pallas_call(
        body,
        name=name,
        grid=(n_rows // ts,),
        in_specs=[s for _, s in ins],
        out_specs=[s for _, s in outs],
        out_shape=[o for o, _ in outs],
        scratch_shapes=list(scratch),
        compiler_params=_params(("arbitrary",), blk, scratch_bytes + 4 * blk // 2),
    )(*[a for a, _ in ins])


def _blk(ts, d):
    return pl.BlockSpec((ts, d), lambda i: (i, 0))


def _full(shape):
    return pl.BlockSpec(shape, lambda i: tuple(0 for _ in shape))


def _rowsum8(v):
    t, d = v.shape
    return jnp.sum(v.reshape(t // SUBLANES, SUBLANES, d), axis=0)


def _accumulate(ref, val):
    @pl.when(pl.program_id(0) == 0)
    def _():
        ref[...] = val

    @pl.when(pl.program_id(0) > 0)
    def _():
        ref[...] += val


def _rstd(x):
    return lax.rsqrt(jnp.mean(x * x, axis=-1, keepdims=True) + EPS)


def _rms_bwd_math(dy, x, g):
    r = _rstd(x)
    gdy = dy * g
    c = jnp.mean(gdy * x, axis=-1, keepdims=True)
    dx = r * gdy - x * (r * r * r * c)
    return dx, dy * (x * r)


def rms_fwd(name, h, g, ts=512):
    s, d = h.shape
    ts = min(ts, s)

    def body(h_ref, g_ref, u_ref):
        x = h_ref[...]
        u_ref[...] = ((x * _rstd(x)) * g_ref[...]).astype(BF16)

    return _rows_call(name, body, s, ts, [(h, _blk(ts, d)), (g, _full((1, d)))],
                      [(jax.ShapeDtypeStruct((s, d), BF16), _blk(ts, d))])[0]


def rms_bwd(name, du, h, g, dres, ts=256):
    s, d = h.shape
    ts = min(ts, s)

    def body(du_ref, h_ref, g_ref, dres_ref, dh_ref, dhb_ref, dg_ref, cs_ref):
        dx, dg = _rms_bwd_math(du_ref[...], h_ref[...], g_ref[...])
        dh = dres_ref[...] + dx
        dh_ref[...] = dh
        dhb_ref[...] = dh.astype(BF16)
        _accumulate(dg_ref, _rowsum8(dg))
        _accumulate(cs_ref, _rowsum8(dh))

    return _rows_call(
        name, body, s, ts,
        [(du, _blk(ts, d)), (h, _blk(ts, d)), (g, _full((1, d))), (dres, _blk(ts, d))],
        [(jax.ShapeDtypeStruct((s, d), F32), _blk(ts, d)), (jax.ShapeDtypeStruct((s, d), BF16), _blk(ts, d)),
         (jax.ShapeDtypeStruct((SUBLANES, d), F32), _full((SUBLANES, d))),
         (jax.ShapeDtypeStruct((SUBLANES, d), F32), _full((SUBLANES, d)))])


def rms_bwd_ple(name, du, h, g, dres, gate, e, ts=256):
    s, d = h.shape
    ts = min(ts, s)

    def body(du_ref, h_ref, g_ref, dres_ref, gate_ref, e_ref, dh_ref, de_ref, dgl_ref, dg_ref):
        dx, dg = _rms_bwd_math(du_ref[...], h_ref[...], g_ref[...])
        dh = dres_ref[...] + dx
        dh_ref[...] = dh
        gate_v = gate_ref[...].astype(F32)
        de_ref[...] = (dh * gate_v).astype(BF16)
        dgl_ref[...] = (dh * e_ref[...] * (gate_v * (1.0 - gate_v))).astype(BF16)
        _accumulate(dg_ref, _rowsum8(dg))

    return _rows_call(
        name, body, s, ts,
        [(du, _blk(ts, d)), (h, _blk(ts, d)), (g, _full((1, d))), (dres, _blk(ts, d)), (gate, _blk(ts, d)),
         (e, _blk(ts, d))],
        [(jax.ShapeDtypeStruct((s, d), F32), _blk(ts, d)), (jax.ShapeDtypeStruct((s, d), BF16), _blk(ts, d)),
         (jax.ShapeDtypeStruct((s, d), BF16), _blk(ts, d)),
         (jax.ShapeDtypeStruct((SUBLANES, d), F32), _full((SUBLANES, d)))])


def final_loss(name, h, g, target, gate, e, ts=256):
    s, d = h.shape
    ts = min(ts, s)

    def body(h_ref, g_ref, t_ref, gate_ref, e_ref, loss_ref, dh_ref, dg_ref, de_ref, dgl_ref):
        x = h_ref[...]
        gf = g_ref[...]
        y = (x * _rstd(x)) * gf
        err = y - t_ref[...]
        _accumulate(loss_ref, _rowsum8(err * err) * (0.5 / d))
        dx, dg = _rms_bwd_math(err * (1.0 / d), x, gf)
        dh_ref[...] = dx
        _accumulate(dg_ref, _rowsum8(dg))
        gate_v = gate_ref[...].astype(F32)
        de_ref[...] = (dx * gate_v).astype(BF16)
        dgl_ref[...] = (dx * e_ref[...] * (gate_v * (1.0 - gate_v))).astype(BF16)

    return _rows_call(
        name, body, s, ts,
        [(h, _blk(ts, d)), (g, _full((1, d))), (target, _blk(ts, d)), (gate, _blk(ts, d)), (e, _blk(ts, d))],
        [(jax.ShapeDtypeStruct((SUBLANES, d), F32), _full((SUBLANES, d))),
         (jax.ShapeDtypeStruct((s, d), F32), _blk(ts, d)),
         (jax.ShapeDtypeStruct((SUBLANES, d), F32), _full((SUBLANES, d))),
         (jax.ShapeDtypeStruct((s, d), BF16), _blk(ts, d)), (jax.ShapeDtypeStruct((s, d), BF16), _blk(ts, d))])


CONV_LANES = 256
CONV_ROWS = 64


def _lane_chunks(d, fn):
    lc = min(CONV_LANES, d)

    def lane_body(c, carry):
        fn(pl.ds(pl.multiple_of(c * lc, lc), lc))
        return carry

    lax.fori_loop(0, d // lc, lane_body, 0)


def _shifted_copies(buf, sh, lanes):
    rows = buf.shape[0] - SUBLANES
    for s in range(1, SUBLANES):
        sh[s, pl.ds(0, rows), :] = buf[pl.ds(s, rows), lanes]


def _window(buf, sh, lanes, start, rows):
    s = start % SUBLANES
    if s == 0:
        return buf[pl.ds(start, rows), lanes]
    return sh[s, pl.ds(start - s, rows), :]


def _prev_halo_spec(ts, halo, width):
    per = ts // halo
    return pl.BlockSpec((halo, width), lambda i: (jnp.maximum(i * per - 1, 0), 0))


def _next_halo_spec(ts, halo, width, n_rows):
    per = ts // halo
    last = n_rows // halo - 1
    return pl.BlockSpec((halo, width), lambda i: (jnp.minimum((i + 1) * per, last), 0))


def cf_fwd_mid(name, a, w_dw, b_dw, gn, ts=256):
    s, d2 = a.shape
    d = d2 // 2
    ts = min(ts, s)
    hl = CONV_A_HALO
    off = hl - (CONV_A_TAPS - 1)

    rc = min(CONV_ROWS, ts)

    def body(a_ref, ah_ref, w_ref, b_ref, gn_ref, v0_ref, v1_ref, v3_ref, buf, sh):
        first = pl.program_id(0) == 0
        halo = ah_ref[...]
        hv0 = halo[:, :d] * jax.nn.sigmoid(halo[:, d:])
        buf[pl.ds(0, hl), :] = jnp.where(first, 0.0, hv0)
        main = a_ref[...]
        v0 = main[:, :d] * jax.nn.sigmoid(main[:, d:])
        buf[pl.ds(hl, ts), :] = v0
        v0_ref[...] = v0

        def conv(lanes):
            _shifted_copies(buf, sh, lanes)
            for r0 in range(0, ts, rc):
                acc = jnp.zeros((rc, lanes.size), F32)
                for k in range(CONV_A_TAPS):
                    acc = acc + w_ref[pl.ds(k, 1), lanes] * _window(buf, sh, lanes, r0 + off + k, rc)
                v1_ref[pl.ds(r0, rc), lanes] = acc + b_ref[:, lanes]

        _lane_chunks(d, conv)
        v1 = v1_ref[...]
        v2 = (v1 * _rstd(v1)) * gn_ref[...]
        v3_ref[...] = (v2 * jax.nn.sigmoid(v2)).astype(BF16)

    return _rows_call(
        name, body, s, ts,
        [(a, _blk(ts, d2)), (a, _prev_halo_spec(ts, hl, d2)), (w_dw, _full(w_dw.shape)),
         (b_dw, _full((1, d))), (gn, _full((1, d)))],
        [(jax.ShapeDtypeStruct((s, d), F32), _blk(ts, d)), (jax.ShapeDtypeStruct((s, d), F32), _blk(ts, d)),
         (jax.ShapeDtypeStruct((s, d), BF16), _blk(ts, d))],
        scratch=[pltpu.VMEM((hl + ts, d), F32), pltpu.VMEM((SUBLANES, hl + ts, min(CONV_LANES, d)), F32)],
        scratch_bytes=_nbytes((hl + ts, d + SUBLANES * CONV_LANES), F32))


def cf_bwd_rows(name, dv3, v1, gn, ts=256):
    s, d = v1.shape
    ts = min(ts, s)

    def body(dv3_ref, v1_ref, gn_ref, dv1_ref, dgn_ref, db_ref):
        v1 = v1_ref[...]
        gn_v = gn_ref[...]
        v2 = (v1 * _rstd(v1)) * gn_v
        sg = jax.nn.sigmoid(v2)
        dv2 = dv3_ref[...] * (sg * (1.0 + v2 * (1.0 - sg)))
        dv1, dgn = _rms_bwd_math(dv2, v1, gn_v)
        dv1_ref[...] = dv1
        _accumulate(dgn_ref, _rowsum8(dgn))
        _accumulate(db_ref, _rowsum8(dv1))

    return _rows_call(
        name, body, s, ts, [(dv3, _blk(ts, d)), (v1, _blk(ts, d)), (gn, _full((1, d)))],
        [(jax.ShapeDtypeStruct((s, d), F32), _blk(ts, d)),
         (jax.ShapeDtypeStruct((SUBLANES, d), F32), _full((SUBLANES, d))),
         (jax.ShapeDtypeStruct((SUBLANES, d), F32), _full((SUBLANES, d)))])


def cf_bwd_conv(name, dv1, v0, a, w_dw, ts=256):
    s, d = dv1.shape
    ts = min(ts, s)
    hl = CONV_A_HALO
    taps = CONV_A_TAPS
    off = hl - (taps - 1)
    last_blk = s // ts - 1

    rc = min(CONV_ROWS, ts)

    def body(dv1_ref, dv1n_ref, v0_ref, v0p_ref, a_ref, w_ref, da_ref, dw_ref, db_ref,
             dbuf, vbuf, dv0_buf, dw_acc, dsh, vsh):
        i = pl.program_id(0)
        dbuf[pl.ds(0, ts), :] = dv1_ref[...]
        dbuf[pl.ds(ts, hl), :] = jnp.where(i == last_blk, 0.0, dv1n_ref[...])
        vbuf[pl.ds(0, hl), :] = jnp.where(i == 0, 0.0, v0p_ref[...])
        vbuf[pl.ds(hl, ts), :] = v0_ref[...]

        @pl.when(i == 0)
        def _():
            dw_acc[...] = jnp.zeros_like(dw_acc)

        def conv_t(lanes):
            _shifted_copies(dbuf, dsh, lanes)
            _shifted_copies(vbuf, vsh, lanes)
            for r0 in range(0, ts, rc):
                g = dbuf[pl.ds(r0, rc), lanes]
                acc = jnp.zeros((rc, lanes.size), F32)
                for k in range(taps):
                    acc = acc + w_ref[pl.ds(k, 1), lanes] * _window(dbuf, dsh, lanes, r0 + taps - 1 - k, rc)
                    prod = g * _window(vbuf, vsh, lanes, r0 + off + k, rc)
                    dw_acc[pl.ds(k * SUBLANES, SUBLANES), lanes] += _rowsum8(prod)
                dv0_buf[pl.ds(r0, rc), lanes] = acc

        _lane_chunks(d, conv_t)
        dv0 = dv0_buf[...]
        av = a_ref[...]
        val, sg = av[:, :d], jax.nn.sigmoid(av[:, d:])
        dval = dv0 * sg
        dgate = dv0 * val * (sg * (1.0 - sg))
        da_ref[:, :d] = dval.astype(BF16)
        da_ref[:, d:] = dgate.astype(BF16)
        _accumulate(db_ref.at[:, pl.ds(0, d)], _rowsum8(dval))
        _accumulate(db_ref.at[:, pl.ds(d, d)], _rowsum8(dgate))

        @pl.when(i == last_blk)
        def _():
            dw_ref[...] = jnp.sum(dw_acc[...].reshape(hl, SUBLANES, d), axis=1)

    lc = min(CONV_LANES, d)
    scratch = [pltpu.VMEM((ts + hl, d), F32), pltpu.VMEM((hl + ts, d), F32), pltpu.VMEM((ts, d), F32),
               pltpu.VMEM((hl * SUBLANES, d), F32), pltpu.VMEM((SUBLANES, ts + hl, lc), F32),
               pltpu.VMEM((SUBLANES, hl + ts, lc), F32)]
    sbytes = _nbytes((3 * ts + 2 * hl + hl * SUBLANES, d), F32) + 2 * _nbytes((SUBLANES, ts + hl, lc), F32)
    return _rows_call(
        name, body, s, ts,
        [(dv1, _blk(ts, d)), (dv1, _next_halo_spec(ts, hl, d, s)), (v0, _blk(ts, d)), (v0, _prev_halo_spec(ts, hl, d)),
         (a, _blk(ts, 2 * d)), (w_dw, _full(w_dw.shape))],
        [(jax.ShapeDtypeStruct((s, 2 * d), BF16), _blk(ts, 2 * d)),
         (jax.ShapeDtypeStruct((hl, d), F32), _full((hl, d))),
         (jax.ShapeDtypeStruct((SUBLANES, 2 * d), F32), _full((SUBLANES, 2 * d)))],
        scratch=scratch, scratch_bytes=sbytes)


def sc_fwd_mid(name, bcv, w_conv, ts=256):
    s, d3 = bcv.shape
    d = d3 // 3
    ts = min(ts, s)
    hl = CONV_B_HALO
    off = hl - (CONV_B_TAPS - 1)

    def body(x_ref, xp_ref, w_ref, y_ref, buf):
        hp = xp_ref[...]
        buf[pl.ds(0, hl), :] = jnp.where(pl.program_id(0) == 0, 0.0, hp[:, d:2 * d] * hp[:, 2 * d:])
        buf[pl.ds(hl, ts), :] = x_ref[:, d:2 * d] * x_ref[:, 2 * d:]
        cc = jnp.zeros((ts, d), F32)
        for k in range(CONV_B_TAPS):
            cc = cc + w_ref[pl.ds(k, 1), :] * buf[pl.ds(off + k, ts), :]
        y_ref[...] = (x_ref[:, :d] * cc).astype(BF16)

    return _rows_call(
        name, body, s, ts,
        [(bcv, _blk(ts, d3)), (bcv, _prev_halo_spec(ts, hl, d3)), (w_conv, _full(w_conv.shape))],
        [(jax.ShapeDtypeStruct((s, d), BF16), _blk(ts, d))],
        scratch=[pltpu.VMEM((hl + ts, d), F32)], scratch_bytes=_nbytes((hl + ts, d), F32))[0]


def sc_bwd_mid(name, dy, bcv, w_conv, ts=256):
    s, d3 = bcv.shape
    d = d3 // 3
    ts = min(ts, s)
    hl = CONV_B_HALO
    taps = CONV_B_TAPS
    off = hl - (taps - 1)
    last_blk = s // ts - 1

    def body(dy_ref, dyn_ref, x_ref, xp_ref, xn_ref, w_ref, dx_ref, dw_ref, cvbuf, dbuf, dw_acc):
        i = pl.program_id(0)
        hp = xp_ref[...]
        cvbuf[pl.ds(0, hl), :] = jnp.where(i == 0, 0.0, hp[:, d:2 * d] * hp[:, 2 * d:])
        gb, gc, v = x_ref[:, :d], x_ref[:, d:2 * d], x_ref[:, 2 * d:]
        cvbuf[pl.ds(hl, ts), :] = gc * v
        dy_v = dy_ref[...]
        dcc = dy_v * gb
        dbuf[pl.ds(0, ts), :] = dcc
        dbuf[pl.ds(ts, hl), :] = jnp.where(i == last_blk, 0.0, dyn_ref[...] * xn_ref[:, :d])

        @pl.when(i == 0)
        def _():
            dw_acc[...] = jnp.zeros_like(dw_acc)

        cc = jnp.zeros((ts, d), F32)
        dcv = jnp.zeros((ts, d), F32)
        for k in range(taps):
            win = cvbuf[pl.ds(off + k, ts), :]
            cc = cc + w_ref[pl.ds(k, 1), :] * win
            dcv = dcv + w_ref[pl.ds(k, 1), :] * dbuf[pl.ds(taps - 1 - k, ts), :]
            dw_acc[pl.ds(k * SUBLANES, SUBLANES), :] += _rowsum8(dcc * win)
        dx_ref[:, :d] = (dy_v * cc).astype(BF16)
        dx_ref[:, d:2 * d] = (dcv * v).astype(BF16)
        dx_ref[:, 2 * d:] = (dcv * gc).astype(BF16)

        @pl.when(i == last_blk)
        def _():
            dw_ref[...] = jnp.sum(dw_acc[...].reshape(hl, SUBLANES, d), axis=1)

    scratch = [pltpu.VMEM((hl + ts, d), F32), pltpu.VMEM((ts + hl, d), F32), pltpu.VMEM((hl * SUBLANES, d), F32)]
    sbytes = _nbytes((2 * ts + 2 * hl + hl * SUBLANES, d), F32)
    return _rows_call(
        name, body, s, ts,
        [(dy, _blk(ts, d)), (dy, _next_halo_spec(ts, hl, d, s)), (bcv, _blk(ts, d3)), (bcv, _prev_halo_spec(ts, hl, d3)),
         (bcv, _next_halo_spec(ts, hl, d3, s)), (w_conv, _full(w_conv.shape))],
        [(jax.ShapeDtypeStruct((s, d3), BF16), _blk(ts, d3)), (jax.ShapeDtypeStruct((hl, d), F32), _full((hl, d)))],
        scratch=scratch, scratch_bytes=sbytes)


def _row(a, i):
    return lax.slice_in_dim(a, i, i + 1, axis=0)


def _local_step(x, p, target, small, get_w, conv_w, put_grads):
    depth = p.shape[0]
    acts = []
    h = x
    for i in range(depth):
        j = i // 2
        act = {"h": h}
        u = rms_fwd(f"rms_mix_{i}", h, _row(small["norm_mix"], i))
        act["u"] = u
        if i % 2 == 0:
            a = mm_x_wcol(f"cf_pw1_{i}", u, get_w("cf_w_pw1", j, u), 0, extras=[(_row(small["cf_b_pw1"], j), "row")],
                          epi=lambda acc, b: (acc + b,))[0]
            v0, v1, v3 = cf_fwd_mid(f"cf_mid_{i}", a, conv_w["cf"][j], _row(small["cf_b_dw"], j), _row(small["cf_norm"], j))
            act.update(a=a, v0=v0, v1=v1, v3=v3)
            h1 = mm_x_wrow(f"cf_pw2_{i}", v3, get_w("cf_w_pw2", j, v3),
                           extras=[(_row(small["cf_b_pw2"], j), "row"), (h, "tile")],
                           epi=lambda acc, b, res: (res + (acc + b),), tn=1024)[0]
        else:
            bcv = mm_x_wcol(f"sc_in_{i}", u, get_w("sc_w_in", j, u), 0, tn=768)[0]
            y = sc_fwd_mid(f"sc_mid_{i}", bcv, conv_w["sc"][j])
            act.update(bcv=bcv, y=y)
            h1 = mm_x_wrow(f"sc_out_{i}", y, get_w("sc_w_out", j, y), extras=[(h, "tile")],
                           epi=lambda acc, res: (res + acc,), tn=1024)[0]
        act["h1"] = h1
        u2 = rms_fwd(f"rms_mlp_{i}", h1, _row(small["norm_mlp"], i))
        z, hd = mm_x_wcol(f"mlp_w1_{i}", u2, get_w("mlp_w1", i, u2), 0, outs_dtypes=(BF16, BF16),
                          epi=lambda acc: (acc, jnp.square(jnp.maximum(acc, 0.0))))
        h2 = mm_x_wrow(f"mlp_w2_{i}", hd, get_w("mlp_w2", i, hd), extras=[(h1, "tile")],
                       epi=lambda acc, res: (res + acc,), tm=1024, tn=256, tk=8192)[0]
        act.update(u2=u2, z=z, hd=hd, h2=h2)
        n3 = rms_fwd(f"rms_ple_{i}", h2, _row(small["norm_ple"], i))
        e = mm_x_wcol(f"ple_proj_{i}", p[i], get_w("ple_w_proj", i, n3), 0)[0]

        def ple_epi(acc, e_t, res):
            g_t = jax.nn.sigmoid(acc)
            return g_t, res + g_t * e_t

        g, h3 = mm_x_wrow(f"ple_gate_{i}", n3, get_w("ple_w_gate", i, e), extras=[(e, "tile"), (h2, "tile")],
                          outs_dtypes=(BF16, F32), epi=ple_epi, tn=1024)
        act.update(n3=n3, e=e, g=g)
        acts.append(act)
        h = h3

    loss_part, dh, dg_final, de, dgl = final_loss("final_loss", h, small["norm_final"], target,
                                                  acts[-1]["g"], acts[-1]["e"])
    sg = {k: [None] * small[k].shape[0] for k in small if k != "norm_final"}
    sg["norm_final"] = [dg_final]
    sg["cf_w_dw"] = [None] * conv_w["cf"].shape[0]
    sg["sc_w_conv"] = [None] * conv_w["sc"].shape[0]

    for i in reversed(range(depth)):
        j = i // 2
        act = acts[i]
        g_proj = mm_xt_dy(f"d_ple_proj_{i}", p[i], de, True)
        g_gate = mm_xt_dy(f"d_ple_gate_{i}", act["n3"], dgl, False)
        dn3 = mm_dy_wrow_t(f"dn3_{i}", dgl, get_w("ple_w_gate", i))[0]
        dh2, dh2b, sg["norm_ple"][i], _ = rms_bwd(f"rms_ple_bwd_{i}", dn3, act["h2"], _row(small["norm_ple"], i), dh)
        g_w2 = mm_xt_dy(f"d_mlp_w2_{i}", act["hd"], dh2b, False)
        dz = mm_dy_wrow_t(f"dz_{i}", dh2b, get_w("mlp_w2", i), extras=[(act["z"], "tile")], outs_dtypes=(BF16,),
                          epi=lambda acc, z_t: (acc * (2.0 * jnp.maximum(z_t.astype(F32), 0.0)),))[0]
        g_w1 = mm_xt_dy(f"d_mlp_w1_{i}", act["u2"], dz, True)
        put_grads({("ple_w_proj", i): g_proj, ("ple_w_gate", i): g_gate, ("mlp_w2", i): g_w2, ("mlp_w1", i): g_w1})
        du2 = mm_dy_wcol_t(f"du2_{i}", dz, get_w("mlp_w1", i), 0)[0]
        dh1, dh1b, sg["norm_mlp"][i], cs1 = rms_bwd(f"rms_mlp_bwd_{i}", du2, act["h1"], _row(small["norm_mlp"], i), dh2)
        if i % 2 == 0:
            g_out = mm_xt_dy(f"d_cf_pw2_{i}", act["v3"], dh1b, False)
            sg["cf_b_pw2"][j] = cs1
            dv3 = mm_dy_wrow_t(f"dv3_{i}", dh1b, get_w("cf_w_pw2", j))[0]
            dv1, sg["cf_norm"][j], sg["cf_b_dw"][j] = cf_bwd_rows(f"cf_bwd_rows_{i}", dv3, act["v1"], _row(small["cf_norm"], j))
            da, sg["cf_w_dw"][j], sg["cf_b_pw1"][j] = cf_bwd_conv(f"cf_bwd_conv_{i}", dv1, act["v0"], act["a"], conv_w["cf"][j])
            g_in = mm_xt_dy(f"d_cf_pw1_{i}", act["u"], da, True)
            put_grads({("cf_w_pw2", j): g_out, ("cf_w_pw1", j): g_in})
            du = mm_dy_wcol_t(f"du_{i}", da, get_w("cf_w_pw1", j), 0)[0]
        else:
            g_out = mm_xt_dy(f"d_sc_out_{i}", act["y"], dh1b, False)
            dy = mm_dy_wrow_t(f"dy_{i}", dh1b, get_w("sc_w_out", j))[0]
            dbcv, sg["sc_w_conv"][j] = sc_bwd_mid(f"sc_bwd_mid_{i}", dy, act["bcv"], conv_w["sc"][j])
            g_in = mm_xt_dy(f"d_sc_in_{i}", act["u"], dbcv, True, tn=768)
            put_grads({("sc_w_out", j): g_out, ("sc_w_in", j): g_in})
            du = mm_dy_wcol_t(f"du_{i}", dbcv, get_w("sc_w_in", j), 0)[0]
        if i > 0:
            below = acts[i - 1]
            dh, de, dgl, sg["norm_mix"][i] = rms_bwd_ple(f"rms_mix_bwd_{i}", du, act["h"], _row(small["norm_mix"], i), dh1,
                                                         below["g"], below["e"])
        else:
            dh, _, sg["norm_mix"][i], _ = rms_bwd(f"rms_mix_bwd_{i}", du, act["h"], _row(small["norm_mix"], i), dh1)
    return loss_part, dh, sg


def _me_and_peers():
    x, y, c = lax.axis_index("x"), lax.axis_index("y"), lax.axis_index("c")
    me = 4 * x + 2 * y + c
    peers = []
    for q in range(1, NDEV):
        px = 1 - x if q & 4 else x
        py = 1 - y if q & 2 else y
        pc = 1 - c if q & 1 else c
        peers.append(((px, py, pc), 4 * px + 2 * py + pc))
    return me, peers


def _exchange(name, srcs, out_shapes, src_fns, dst_fns, after=()):
    n = len(srcs)
    n_after = len(after)

    def body(*refs):
        ins, outs = refs[:n], refs[n + n_after:2 * n + n_after]
        send_sems, recv_sems, local_sems = refs[2 * n + n_after:]
        me, peers = _me_and_peers()
        local, remote = [], []
        for k in range(n):
            cp = pltpu.make_async_copy(src_fns[k](ins[k], me), dst_fns[k](outs[k], me), local_sems.at[k])
            cp.start()
            local.append(cp)
        for q, (peer, peer_blk) in enumerate(peers):
            for k in range(n):
                cp = pltpu.make_async_remote_copy(
                    src_ref=src_fns[k](ins[k], peer_blk), dst_ref=dst_fns[k](outs[k], me),
                    send_sem=send_sems.at[k, q], recv_sem=recv_sems.at[k, q],
                    device_id=peer, device_id_type=MESH)
                cp.start()
                remote.append(cp)
        for q, (peer, peer_blk) in enumerate(peers):
            for k in range(n):
                pltpu.make_async_remote_copy(
                    src_ref=src_fns[k](ins[k], peer_blk), dst_ref=dst_fns[k](outs[k], peer_blk),
                    send_sem=send_sems.at[k, q], recv_sem=recv_sems.at[k, q],
                    device_id=peer, device_id_type=MESH).wait_recv()
        for cp in remote:
            cp.wait_send()
        for cp in local:
            cp.wait()

    any_spec = pl.BlockSpec(memory_space=pl.ANY)
    return pl.pallas_call(
        body,
        name=name,
        in_specs=[any_spec] * (n + n_after),
        out_specs=[any_spec] * n,
        out_shape=out_shapes,
        scratch_shapes=[pltpu.SemaphoreType.DMA((n, N_PEERS)), pltpu.SemaphoreType.DMA((n, N_PEERS)),
                        pltpu.SemaphoreType.DMA((n,))],
    )(*srcs, *after)


def sc_exchange(name, collective_id, srcs, out_shapes, src_fns, dst_fns):
    n = len(srcs)

    def body(*refs):
        ins, outs = refs[:n], refs[n:2 * n]
        send_sems, recv_sems, local_sems = refs[2 * n:]
        me, peers = _me_and_peers()
        barrier = pltpu.get_barrier_semaphore()
        for peer, _ in peers:
            pl.semaphore_signal(barrier, inc=1, device_id=peer, device_id_type=MESH)
        pl.semaphore_wait(barrier, N_PEERS)
        local, remote = [], []
        for k in range(n):
            cp = pltpu.make_async_copy(src_fns[k](ins[k], me), dst_fns[k](outs[k], me), local_sems.at[k])
            cp.start()
            local.append(cp)
        for q, (peer, peer_blk) in enumerate(peers):
            for k in range(n):
                cp = pltpu.make_async_remote_copy(
                    src_ref=src_fns[k](ins[k], peer_blk), dst_ref=dst_fns[k](outs[k], me),
                    send_sem=send_sems.at[k, q], recv_sem=recv_sems.at[k, q],
                    device_id=peer, device_id_type=MESH)
                cp.start()
                remote.append(cp)
        for q, (peer, peer_blk) in enumerate(peers):
            for k in range(n):
                pltpu.make_async_remote_copy(
                    src_ref=src_fns[k](ins[k], peer_blk), dst_ref=dst_fns[k](outs[k], peer_blk),
                    send_sem=send_sems.at[k, q], recv_sem=recv_sems.at[k, q],
                    device_id=peer, device_id_type=MESH).wait_recv()
        for cp in remote:
            cp.wait_send()
        for cp in local:
            cp.wait()

    return pl.kernel(
        body,
        out_type=out_shapes,
        mesh=plsc.ScalarSubcoreMesh(axis_name="sequencer", num_cores=1),
        name=name,
        scratch_types=[pltpu.SemaphoreType.DMA((n, N_PEERS)), pltpu.SemaphoreType.DMA((n, N_PEERS)),
                       pltpu.SemaphoreType.DMA((n,))],
        compiler_params=pltpu.CompilerParams(collective_id=collective_id),
    )(*srcs)


def sc_gather(name, collective_id, srcs, layers):
    n = len(srcs)
    outs_shape = [jax.ShapeDtypeStruct((NDEV, 1) + a.shape[1:], a.dtype) for a in srcs]

    def body(*refs):
        ins, outs = refs[:n], refs[n:2 * n]
        send_sems, recv_sems, local_sems = refs[2 * n:]
        x, y, c = lax.axis_index("x"), lax.axis_index("y"), lax.axis_index("c")
        me = 4 * x + 2 * y + c
        sibling = (x, y, 1 - c)
        chips = [(1 - x, y), (x, 1 - y), (1 - x, 1 - y)]
        barrier = pltpu.get_barrier_semaphore()
        for peer in [sibling] + [(cx, cy, c) for cx, cy in chips]:
            pl.semaphore_signal(barrier, inc=1, device_id=peer, device_id_type=MESH)
        pl.semaphore_wait(barrier, 1 + len(chips))

        def copy(k, slot, blk, to, src=None):
            place = outs[k].at[blk, 0]
            return pltpu.make_async_remote_copy(
                src_ref=place if src is None else src, dst_ref=place,
                send_sem=send_sems.at[k, slot], recv_sem=recv_sems.at[k, slot],
                device_id=to, device_id_type=MESH)

        local, sent = [], []
        for k in range(n):
            mine = ins[k].at[layers[k]]
            cp = pltpu.make_async_copy(mine, outs[k].at[me, 0], local_sems.at[k])
            cp.start()
            local.append(cp)
            sent.append(copy(k, 0, me, sibling, src=mine))
            sent += [copy(k, 1 + j, me, (cx, cy, c), src=mine) for j, (cx, cy) in enumerate(chips)]
        for cp in sent:
            cp.start()
        for k in range(n):
            for j, (cx, cy) in enumerate(chips):
                blk = 4 * cx + 2 * cy + c
                copy(k, 1 + j, blk, sibling).wait_recv()
                fwd = copy(k, 4 + j, blk, sibling)
                fwd.start()
                sent.append(fwd)
        for k in range(n):
            copy(k, 0, 4 * x + 2 * y + (1 - c), sibling).wait_recv()
            for j, (cx, cy) in enumerate(chips):
                copy(k, 4 + j, 4 * cx + 2 * cy + (1 - c), sibling).wait_recv()
        for cp in sent:
            cp.wait_send()
        for cp in local:
            cp.wait()

    return pl.kernel(
        body,
        out_type=outs_shape,
        mesh=plsc.ScalarSubcoreMesh(axis_name="sequencer", num_cores=1),
        name=name,
        scratch_types=[pltpu.SemaphoreType.DMA((n, N_PEERS)), pltpu.SemaphoreType.DMA((n, N_PEERS)),
                       pltpu.SemaphoreType.DMA((n,))],
        compiler_params=pltpu.CompilerParams(collective_id=collective_id),
    )(*srcs)


def _gather_src(layer):
    return lambda ref, blk: ref.at[layer]


def _gather_dst(ref, blk):
    return ref.at[blk, 0]


def _slice_of(ref, blk):
    return ref.at[blk]


def cast_bf16(name, w, tr_elems=512 * 1024):
    l, r, c = w.shape
    tr = _row_tile(r, tr_elems // c)
    spec = pl.BlockSpec((None, tr, c), lambda li, i: (li, i, 0))

    def body(w_ref, o_ref):
        o_ref[...] = w_ref[...].astype(BF16)

    return pl.pallas_call(
        body, name=name, grid=(l, r // tr), in_specs=[spec], out_specs=spec,
        out_shape=jax.ShapeDtypeStruct(w.shape, BF16),
        compiler_params=_params(("parallel", "parallel"), 6 * tr * c),
    )(w)


def _adamw_math(w, g, m, v):
    m = ADAM_B1 * m + (1.0 - ADAM_B1) * g
    v = ADAM_B2 * v + (1.0 - ADAM_B2) * (g * g)
    m_hat = m * (1.0 / (1.0 - ADAM_B1 ** ADAM_STEP))
    v_hat = v * (1.0 / (1.0 - ADAM_B2 ** ADAM_STEP))
    delta = -ADAM_LR * (m_hat / (jnp.sqrt(v_hat) + ADAM_EPS) + ADAM_WD * w)
    return delta, m, v


def _sum_blocks(ref):
    g = ref[0].astype(F32)
    for d in range(1, ref.shape[0]):
        g = g + ref[d].astype(F32)
    return g


def adamw_layer(name, recv, w, m, v, layer, stacked, after=None, tr_elems=256 * 1024):
    nd, r, c = recv.shape
    tr = _row_tile(r, tr_elems // c)
    r_spec = pl.BlockSpec((nd, tr, c), lambda i: (0, i, 0))
    w_spec = pl.BlockSpec((None, tr, c), lambda i: (layer, i, 0))
    if stacked is None:
        stacked = [lax.empty(w.shape, F32) for _ in range(4)]
    after = [] if after is None else [after]

    def body(r_ref, w_ref, m_ref, v_ref, g_in, d_in, m_in, v_in, *rest):
        g_out, d_out, m_out, v_out = rest[len(after):]
        g = _sum_blocks(r_ref)
        delta, m_new, v_new = _adamw_math(w_ref[...], g, m_ref[...], v_ref[...])
        g_out[...] = g
        d_out[...] = delta
        m_out[...] = m_new
        v_out[...] = v_new

    out = jax.ShapeDtypeStruct(w.shape, F32)
    return pl.pallas_call(
        body, name=name, grid=(r // tr,),
        in_specs=[r_spec, w_spec, w_spec, w_spec] + [pl.BlockSpec(memory_space=pl.ANY)] * (4 + len(after)),
        out_specs=[w_spec] * 4, out_shape=[out] * 4,
        input_output_aliases={4: 0, 5: 1, 6: 2, 7: 3},
        compiler_params=_params(("parallel",), tr * c * (2 * nd + 7 * 4)),
    )(recv, w, m, v, *stacked, *after)


def pack_small_grads(name, parts, taps, n_blocks):
    d = parts[0].shape[1]
    n_p, rows = len(parts), [t.shape[0] for t in taps]
    cb = d // n_blocks

    def body(*refs):
        part_refs, tap_refs = refs[:n_p], refs[n_p:n_p + len(taps)]
        sums_out, taps_out = refs[n_p + len(taps):]
        for i, r in enumerate(part_refs):
            sums_out[pl.ds(i, 1), :] = jnp.sum(r[...], axis=0, keepdims=True)
        r0 = 0
        for t_ref, n in zip(tap_refs, rows):
            for b in range(n_blocks):
                taps_out[b, pl.ds(r0, n), :] = t_ref[:, pl.ds(b * cb, cb)]
            r0 += n

    vm = pl.BlockSpec(memory_space=pltpu.VMEM)
    return pl.pallas_call(
        body, name=name, in_specs=[vm] * (n_p + len(taps)), out_specs=[vm] * 2,
        out_shape=[jax.ShapeDtypeStruct((n_p, d), F32), jax.ShapeDtypeStruct((n_blocks, sum(rows), cb), F32)],
    )(*parts, *taps)


def small_update(name, part_g, tap_g, w_a, m_a, v_a, w_b, m_b, v_b):
    nd, na, d = part_g.shape
    nb, cb = w_b.shape

    def body(pg_ref, tg_ref, wa_ref, ma_ref, va_ref, wb_ref, mb_ref, vb_ref,
             ga_out, da_out, ma_out, va_out, gb_out, db_out, mb_out, vb_out, loss_out):
        ga = _sum_blocks(pg_ref)
        delta, m_new, v_new = _adamw_math(wa_ref[...], ga, ma_ref[...], va_ref[...])
        ga_out[...] = ga
        da_out[...] = delta
        ma_out[...] = m_new
        va_out[...] = v_new
        loss_out[...] = jnp.broadcast_to(jnp.sum(ga[na - 1:na, :], axis=1, keepdims=True), loss_out.shape)
        gb = _sum_blocks(tg_ref)
        delta, m_new, v_new = _adamw_math(wb_ref[...], gb, mb_ref[...], vb_ref[...])
        gb_out[...] = gb
        db_out[...] = delta
        mb_out[...] = m_new
        vb_out[...] = v_new

    oa, ob = jax.ShapeDtypeStruct((na, d), F32), jax.ShapeDtypeStruct((nb, cb), F32)
    vm = pl.BlockSpec(memory_space=pltpu.VMEM)
    return pl.pallas_call(
        body, name=name, in_specs=[vm] * 8, out_specs=[vm] * 9,
        out_shape=[oa] * 4 + [ob] * 4 + [jax.ShapeDtypeStruct((1, LANES), F32)],
        compiler_params=pltpu.CompilerParams(vmem_limit_bytes=_vmem_limit(_nbytes(part_g.shape, F32))),
    )(part_g, tap_g, w_a, m_a, v_a, w_b, m_b, v_b)


BIG = ("cf_w_pw1", "cf_w_pw2", "sc_w_in", "sc_w_out", "mlp_w1", "mlp_w2", "ple_w_proj", "ple_w_gate")
COL_SHARDED = ("cf_w_pw1", "sc_w_in", "mlp_w1", "ple_w_proj")
WEIGHT_ORDER = ("norm_mix", "norm_mlp", "norm_ple", "cf_w_pw1", "cf_b_pw1", "cf_w_dw", "cf_b_dw", "cf_norm",
                "cf_w_pw2", "cf_b_pw2", "sc_w_in", "sc_w_conv", "sc_w_out", "mlp_w1", "mlp_w2", "ple_w_proj",
                "ple_w_gate", "norm_final")
SMALL_ROWS = (("norm_mix", 4), ("norm_mlp", 4), ("norm_ple", 4), ("cf_b_pw1", 4), ("cf_b_dw", 2), ("cf_norm", 2),
              ("cf_b_pw2", 2), ("norm_final", 1))


def _layer_weights(i):
    mixer = (("cf_w_pw1", i // 2), ("cf_w_pw2", i // 2)) if i % 2 == 0 else (("sc_w_in", i // 2), ("sc_w_out", i // 2))
    return mixer + (("mlp_w1", i), ("mlp_w2", i), ("ple_w_proj", i), ("ple_w_gate", i))


def _pad_rows(a, rows):
    return jnp.pad(a, ((0, 0), (0, rows - a.shape[1]), (0, 0)))


def _pack_taps(cf, sc):
    c = cf.shape[2]
    return jnp.concatenate([_pad_rows(cf, CONV_A_HALO).reshape(-1, c), _pad_rows(sc, CONV_B_HALO).reshape(-1, c)], axis=0)


def _unpack_taps(t, n_cf):
    c = t.shape[1]
    cf = t[:n_cf * CONV_A_HALO].reshape(n_cf, CONV_A_HALO, c)[:, :CONV_A_TAPS]
    sc = t[n_cf * CONV_A_HALO:].reshape(-1, CONV_B_HALO, c)[:, :CONV_B_TAPS]
    return cf, sc


def _pack_small(vals, d):
    return jnp.concatenate([vals[k].reshape(-1, d) for k, _ in SMALL_ROWS] + [jnp.zeros((1, d), F32)], axis=0)


def _unpack_small(a, shapes):
    out, r = {}, 0
    for k, n in SMALL_ROWS:
        out[k] = a[r:r + n].reshape(shapes[k])
        r += n
    return out


def kernel(x, p, norm_mix, norm_mlp, norm_ple, cf_w_pw1, cf_b_pw1, cf_w_dw, cf_b_dw, cf_norm, cf_w_pw2, cf_b_pw2, sc_w_in, sc_w_conv, sc_w_out, mlp_w1, mlp_w2, ple_w_proj, ple_w_gate, norm_final, loss_target, m_norm_mix, m_norm_mlp, m_norm_ple, m_cf_w_pw1, m_cf_b_pw1, m_cf_w_dw, m_cf_b_dw, m_cf_norm, m_cf_w_pw2, m_cf_b_pw2, m_sc_w_in, m_sc_w_conv, m_sc_w_out, m_mlp_w1, m_mlp_w2, m_ple_w_proj, m_ple_w_gate, m_norm_final, v_norm_mix, v_norm_mlp, v_norm_ple, v_cf_w_pw1, v_cf_b_pw1, v_cf_w_dw, v_cf_b_dw, v_cf_norm, v_cf_w_pw2, v_cf_b_pw2, v_sc_w_in, v_sc_w_conv, v_sc_w_out, v_mlp_w1, v_mlp_w2, v_ple_w_proj, v_ple_w_gate, v_norm_final):
    w = dict(norm_mix=norm_mix, norm_mlp=norm_mlp, norm_ple=norm_ple, cf_w_pw1=cf_w_pw1, cf_b_pw1=cf_b_pw1,
             cf_w_dw=cf_w_dw, cf_b_dw=cf_b_dw, cf_norm=cf_norm, cf_w_pw2=cf_w_pw2, cf_b_pw2=cf_b_pw2,
             sc_w_in=sc_w_in, sc_w_conv=sc_w_conv, sc_w_out=sc_w_out, mlp_w1=mlp_w1, mlp_w2=mlp_w2,
             ple_w_proj=ple_w_proj, ple_w_gate=ple_w_gate, norm_final=norm_final)
    m = dict(norm_mix=m_norm_mix, norm_mlp=m_norm_mlp, norm_ple=m_norm_ple, cf_w_pw1=m_cf_w_pw1, cf_b_pw1=m_cf_b_pw1,
             cf_w_dw=m_cf_w_dw, cf_b_dw=m_cf_b_dw, cf_norm=m_cf_norm, cf_w_pw2=m_cf_w_pw2, cf_b_pw2=m_cf_b_pw2,
             sc_w_in=m_sc_w_in, sc_w_conv=m_sc_w_conv, sc_w_out=m_sc_w_out, mlp_w1=m_mlp_w1, mlp_w2=m_mlp_w2,
             ple_w_proj=m_ple_w_proj, ple_w_gate=m_ple_w_gate, norm_final=m_norm_final)
    v = dict(norm_mix=v_norm_mix, norm_mlp=v_norm_mlp, norm_ple=v_norm_ple, cf_w_pw1=v_cf_w_pw1, cf_b_pw1=v_cf_b_pw1,
             cf_w_dw=v_cf_w_dw, cf_b_dw=v_cf_b_dw, cf_norm=v_cf_norm, cf_w_pw2=v_cf_w_pw2, cf_b_pw2=v_cf_b_pw2,
             sc_w_in=v_sc_w_in, sc_w_conv=v_sc_w_conv, sc_w_out=v_sc_w_out, mlp_w1=v_mlp_w1, mlp_w2=v_mlp_w2,
             ple_w_proj=v_ple_w_proj, ple_w_gate=v_ple_w_gate, norm_final=v_norm_final)
    depth, d = norm_mix.shape
    n_cf = cf_w_dw.shape[0]

    taps_w = _pack_taps(cf_w_dw, sc_w_conv)
    shards, gathered = {}, {}
    ids = iter(range(5 * depth))
    for i in range(depth):
        names = _layer_weights(i)
        groups = [names[:1], names[1:2], names[2:3], names[3:]] if i == 0 else [names[:2], names[2:]]
        for n_group, group in enumerate(groups):
            with_taps = i == 0 and n_group == 1
            for k, _ in group:
                if k not in shards:
                    shards[k] = cast_bf16(f"cast_{k}", w[k])
            srcs = [shards[k] for k, _ in group] + ([taps_w[None]] if with_taps else [])
            got = sc_gather(f"gather_{i}_{n_group}", next(ids), srcs, [l for _, l in group] + ([0] if with_taps else []))
            for (k, l), g in zip(group, got):
                gathered[(k, l)] = g if k in COL_SHARDED else g.reshape(-1, g.shape[3])
            if with_taps:
                taps_full = jnp.transpose(got[-1][:, 0], (1, 0, 2)).reshape(taps_w.shape[0], d)
    conv_w = {"cf": taps_full[:n_cf * CONV_A_HALO].reshape(n_cf, CONV_A_HALO, d),
              "sc": taps_full[n_cf * CONV_A_HALO:].reshape(-1, CONV_B_HALO, d)}

    def get_w(k, l, after=None):
        return gathered[(k, l)]

    received, waiting = {}, {}
    last_group = list(_layer_weights(0)[:2])

    def put_grads(grads):
        names = list(grads)
        if names[0][0] in ("cf_w_pw2", "sc_w_out") and set(names) != set(last_group):
            waiting.update(grads)
            return
        grads = {**waiting, **grads}
        waiting.clear()
        names = list(grads)
        got = sc_exchange(f"grad_exchange_{names[-1][0]}_{names[-1][1]}", next(ids), [grads[n] for n in names],
                          [jax.ShapeDtypeStruct(grads[n].shape, BF16) for n in names],
                          [_slice_of] * len(names), [_slice_of] * len(names))
        received.update(zip(names, got))

    small = {k: w[k] for k, _ in SMALL_ROWS}
    small["norm_final"] = norm_final[None]
    loss_part, grad_x, sg = _local_step(x[0], p[:, 0], loss_target[0], small, get_w, conv_w, put_grads)

    out = {k: None for k in BIG}
    previous = None
    for (k, l), recv in received.items():
        if (k, l) not in last_group:
            out[k] = adamw_layer(f"adamw_{k}_{l}", recv, w[k], m[k], v[k], l, out[k], after=previous)
            previous = out[k][1]
    updated_first = [out[k][0] for k in BIG if out[k] is not None and k not in [n for n, _ in last_group]]

    parts = []
    for k, n in SMALL_ROWS:
        for g in sg[k]:
            parts += [g[:, :d], g[:, d:]] if g.shape[1] == 2 * d else [g]
    parts.append(loss_part)
    sums, tap_slices = pack_small_grads("pack_small_grads", parts, sg["cf_w_dw"] + sg["sc_w_conv"], NDEV)
    part_all, tap_mine = _exchange(
        "small_exchange", [sums[None], tap_slices],
        [jax.ShapeDtypeStruct((NDEV, 1) + sums.shape, F32), jax.ShapeDtypeStruct(tap_slices.shape, F32)],
        [_gather_src(0), _slice_of], [_gather_dst, _slice_of], after=updated_first)
    for k, l in last_group:
        out[k] = adamw_layer(f"adamw_{k}_{l}", received[(k, l)], w[k], m[k], v[k], l, out[k], after=part_all)
    sm = small_update("small_update", part_all[:, 0], tap_mine,
                      _pack_small(w, d), _pack_small(m, d), _pack_small(v, d),
                      taps_w, _pack_taps(m["cf_w_dw"], m["sc_w_conv"]), _pack_taps(v["cf_w_dw"], v["sc_w_conv"]))
    shapes = {k: w[k].shape for k, _ in SMALL_ROWS}
    for t in range(4):
        un = _unpack_small(sm[t], shapes)
        cf_t, sc_t = _unpack_taps(sm[4 + t], n_cf)
        for k in un:
            out.setdefault(k, [None] * 4)[t] = un[k]
        out.setdefault("cf_w_dw", [None] * 4)[t] = cf_t
        out.setdefault("sc_w_conv", [None] * 4)[t] = sc_t
    loss = sm[8][0, 0]

    return (loss, grad_x[None], *[out[k][0] for k in WEIGHT_ORDER], *[out[k][1] for k in WEIGHT_ORDER],
            *[out[k][2] for k in WEIGHT_ORDER], *[out[k][3] for k in WEIGHT_ORDER])
```
